```python
import jax, jax.numpy as jnp
from jax import lax
import numpy as np

D_MODEL = 1024
BATCH = 16
SEQ = 256
DEPTH = 4
DEC_BATCH = 2
DEC_SEQ = 4096
PAST_LEN = 512

GRID_W = 64
FFT_W = D_MODEL // 4
POOL_W = D_MODEL // 4
POOL_WINDOWS = (2, 4, 8, 16)
POOL_GROUP = POOL_W // len(POOL_WINDOWS)
SGU_W = D_MODEL // 4
SGU_HEADS = 4
SGU_HEAD_DIM = SGU_W // SGU_HEADS
CHUNK = 128
HEAD_DIM = 64
ATTN_W = D_MODEL // 4
N_HEADS = ATTN_W // HEAD_DIM
N_KV_HEADS = N_HEADS // 2
KV_W = N_KV_HEADS * HEAD_DIM
MIX_W = FFT_W + POOL_W + SGU_W + ATTN_W
IN_W = FFT_W + POOL_W + 2 * SGU_W + ATTN_W + 2 * KV_W
Q_BLOCK = 128
ROPE_THETA = 10000.0
N_GROUPS = 4
EXPERTS_PER_GROUP = 8
N_EXPERTS = N_GROUPS * EXPERTS_PER_GROUP
TOP_K_INNER = 2
EXPERT_FF = 512
DEEPNORM_ALPHA = float((2 * DEPTH) ** 0.25)
DEEPNORM_BETA = float((8 * DEPTH) ** -0.25)
LN_EPS = 1e-5
RMS_EPS = 1e-6

kernel_name = "hybrid_diffusion_parallel_groups_step"


def layer_norm(x, g, b):
    xf = x.astype(jnp.float32)
    mu = jnp.mean(xf, axis=-1, keepdims=True)
    var = jnp.mean(jnp.square(xf - mu), axis=-1, keepdims=True)
    y = (xf - mu) * lax.rsqrt(var + LN_EPS) * g.astype(jnp.float32) + b.astype(jnp.float32)
    return y.astype(x.dtype)


def rms_norm(x, g):
    xf = x.astype(jnp.float32)
    y = xf * lax.rsqrt(jnp.mean(jnp.square(xf), axis=-1, keepdims=True) + RMS_EPS)
    return (y * g.astype(jnp.float32)).astype(x.dtype)


def rope_tables(rows):
    row = jnp.repeat(jnp.arange(rows), GRID_W).astype(jnp.float32)
    col = jnp.tile(jnp.arange(GRID_W), rows).astype(jnp.float32)
    n_freq = HEAD_DIM // 4
    inv = ROPE_THETA ** (-jnp.arange(n_freq, dtype=jnp.float32) / n_freq)
    ar = row[:, None] * inv
    ac = col[:, None] * inv
    ang = jnp.concatenate([ar, ar, ac, ac], axis=-1)
    return jnp.cos(ang), jnp.sin(ang)


def apply_rope(x, cos, sin):
    xf = x.astype(jnp.float32)
    xr = xf.reshape(*x.shape[:-1], 2, 2, HEAD_DIM // 4)
    rot = jnp.stack([-xr[..., 1, :], xr[..., 0, :]], axis=-2).reshape(x.shape)
    return (xf * cos[None, :, None, :] + rot * sin[None, :, None, :]).astype(x.dtype)


def block_attention(q, k, v):
    bsz, nq, _, dh = q.shape
    groups = N_HEADS // N_KV_HEADS
    nb = nq // Q_BLOCK
    qb = q.reshape(bsz, nb, Q_BLOCK, N_KV_HEADS, groups, dh).transpose(1, 0, 2, 3, 4, 5)
    kf = k.astype(jnp.float32)
    vf = v.astype(jnp.float32)
    scale = dh ** -0.5

    def one_block(qblk):
        s = jnp.einsum('bqhgd,bshd->bhgqs', qblk.astype(jnp.float32), kf) * scale
        p = jax.nn.softmax(s, axis=-1)
        return jnp.einsum('bhgqs,bshd->bqhgd', p, vf).astype(q.dtype)

    o = lax.map(one_block, qb)
    return o.transpose(1, 0, 2, 3, 4, 5).reshape(bsz, nq, N_HEADS * dh)


def fourier_mix(a, w_fft):
    f = jnp.fft.fft2(a.astype(jnp.float32), axes=(1, 2), norm='ortho').real
    return f.astype(a.dtype) @ w_fft


def pool_mix(p, w_pool, pool_scale):
    bsz, n, _ = p.shape
    pf = p.astype(jnp.float32)
    cs = jnp.concatenate([jnp.zeros((bsz, 1, POOL_W), jnp.float32), jnp.cumsum(pf, axis=1)], axis=1)
    t = jnp.arange(n)
    outs = []
    for gi, w in enumerate(POOL_WINDOWS):
        lo = jnp.clip(t - w // 2, 0, n)
        hi = jnp.clip(t - w // 2 + w, 0, n)
        sl = slice(gi * POOL_GROUP, (gi + 1) * POOL_GROUP)
        s = cs[:, hi, sl] - cs[:, lo, sl]
        cnt = (hi - lo).astype(jnp.float32)[None, :, None]
        outs.append(s / cnt - pf[..., sl])
    y = jnp.stack(outs, axis=2)
    y = jnp.einsum('bngc,gcd->bngd', y, w_pool.astype(jnp.float32)).reshape(bsz, n, POOL_W)
    return (y * pool_scale.astype(jnp.float32)).astype(p.dtype)


def sgu_mix(gu, ln_g, ln_b, w_s, b_s):
    bsz, n, _ = gu.shape
    h = jax.nn.gelu(gu)
    u, v = h[..., :SGU_W], h[..., SGU_W:]
    v = layer_norm(v, ln_g, ln_b)
    vc = v.reshape(bsz, n // CHUNK, CHUNK, SGU_HEADS, SGU_HEAD_DIM)
    sp = jnp.einsum('gpq,bnqgc->bnpgc', w_s, vc) + b_s.T[None, None, :, :, None]
    return u * sp.reshape(bsz, n, SGU_W).astype(u.dtype)


def hier_moe(h, lp):
    bsz, n, d = h.shape
    t = h.reshape(-1, d)
    gprob = jax.nn.softmax((t @ lp['w_rg'] + lp['b_rg']).astype(jnp.float32), axis=-1)
    pg, gsel = lax.top_k(gprob, 1)
    elog = (t @ lp['w_re'] + lp['b_re']).astype(jnp.float32).reshape(-1, N_GROUPS, EXPERTS_PER_GROUP)
    elog_g = jnp.take_along_axis(elog, gsel[:, :, None], axis=1)[:, 0]
    top_v, top_i = lax.top_k(elog_g, TOP_K_INNER)
    pe = jax.nn.softmax(top_v, axis=-1) * pg
    eid = gsel * EXPERTS_PER_GROUP + top_i
    gates = jnp.sum(jax.nn.one_hot(eid, N_EXPERTS, dtype=jnp.float32) * pe[..., None], axis=1)
    hg = jnp.einsum('td,edf->tef', t, lp['w_gate'])
    hu = jnp.einsum('td,edf->tef', t, lp['w_up'])
    act = jax.nn.silu(hg) * hu * gates[:, :, None].astype(t.dtype)
    y = jnp.einsum('tef,efd->td', act, lp['w_down'])
    return y.reshape(bsz, n, d)


def trunk_layer(x, mod, lp, ctx_kv=None, rope=None):
    bsz, n, _ = x.shape
    sh1, sc1, g1, sh2, sc2, g2 = jnp.split(mod, 6, axis=-1)
    h = x * (1 + sc1) + sh1
    proj = h @ lp['w_in']
    cuts = np.cumsum([FFT_W, POOL_W, 2 * SGU_W, ATTN_W, KV_W]).tolist()
    a, pl, gu, q, k, v = jnp.split(proj, cuts, axis=-1)
    q = rms_norm(q.reshape(bsz, n, N_HEADS, HEAD_DIM), lp['q_norm'])
    k = rms_norm(k.reshape(bsz, n, N_KV_HEADS, HEAD_DIM), lp['k_norm'])
    v = v.reshape(bsz, n, N_KV_HEADS, HEAD_DIM)
    k_own, v_own = k, v
    if ctx_kv is None:
        k_all, v_all = k, v
    else:
        cos, sin = rope
        q = apply_rope(q, cos, sin)
        k_lat = apply_rope(k, cos, sin)
        k_all = jnp.concatenate([ctx_kv[0].astype(k.dtype), k_lat], axis=1)
        v_all = jnp.concatenate([ctx_kv[1].astype(v.dtype), v], axis=1)
    att = block_attention(q, k_all, v_all)
    mix = jnp.concatenate([
        fourier_mix(a, lp['w_fft']),
        pool_mix(pl, lp['w_pool'], lp['pool_scale']),
        sgu_mix(gu, lp['sgu_ln_g'], lp['sgu_ln_b'], lp['w_sgu'], lp['b_sgu']),
        att,
    ], axis=-1) @ lp['w_out']
    x = layer_norm(DEEPNORM_ALPHA * x + g1 * mix, lp['ln1_g'], lp['ln1_b'])
    h2 = x * (1 + sc2) + sh2
    x = layer_norm(DEEPNORM_ALPHA * x + g2 * hier_moe(h2, lp), lp['ln2_g'], lp['ln2_b'])
    return x, k_own, v_own


def setup_inputs(seed: int = 0) -> dict:
    key = jax.random.key(seed)
    ks = iter(jax.random.split(key, 40))
    f32 = jnp.float32

    def nrm(shape, scale):
        return jax.random.normal(next(ks), shape, f32) * scale

    L, D = DEPTH, D_MODEL
    return {
        'x_prompt': nrm((BATCH, SEQ, D), 1.0),
        'x_sample': nrm((DEC_BATCH, DEC_SEQ, D), 1.0),
        'cache_k': nrm((DEC_BATCH, DEPTH, PAST_LEN, N_KV_HEADS, HEAD_DIM), 1.0),
        'cache_v': nrm((DEC_BATCH, DEPTH, PAST_LEN, N_KV_HEADS, HEAD_DIM), 1.0),
        'c': nrm((DEC_BATCH, D), 1.0),
        'c_ctx': nrm((D,), 1.0),
        'w_mod': nrm((L, D, 6 * D), 0.5 * D ** -0.5),
        'b_mod': nrm((L, 6 * D), 0.01),
        'w_in': nrm((L, D, IN_W), D ** -0.5),
        'w_fft': nrm((L, FFT_W, FFT_W), FFT_W ** -0.5),
        'w_pool': nrm((L, len(POOL_WINDOWS), POOL_GROUP, POOL_GROUP), POOL_GROUP ** -0.5),
        'pool_scale': 1.0 + nrm((L, POOL_W), 0.1),
        'sgu_ln_g': 1.0 + nrm((L, SGU_W), 0.1),
        'sgu_ln_b': nrm((L, SGU_W), 0.01),
        'w_sgu': nrm((L, SGU_HEADS, CHUNK, CHUNK), 0.5 * CHUNK ** -0.5),
        'b_sgu': 1.0 + nrm((L, SGU_HEADS, CHUNK), 0.1),
        'q_norm_g': 1.0 + nrm((L, HEAD_DIM), 0.1),
        'k_norm_g': 1.0 + nrm((L, HEAD_DIM), 0.1),
        'w_out': nrm((L, MIX_W, D), DEEPNORM_BETA * MIX_W ** -0.5),
        'ln1_g': 1.0 + nrm((L, D), 0.1),
        'ln1_b': nrm((L, D), 0.01),
        'w_router_group': nrm((L, D, N_GROUPS), D ** -0.5),
        'b_router_group': nrm((L, N_GROUPS), 0.01),
        'w_router_expert': nrm((L, D, N_EXPERTS), D ** -0.5),
        'b_router_expert': nrm((L, N_EXPERTS), 0.01),
        'w_gate': nrm((L, N_EXPERTS, D, EXPERT_FF), D ** -0.5),
        'w_up': nrm((L, N_EXPERTS, D, EXPERT_FF), D ** -0.5),
        'w_down': nrm((L, N_EXPERTS, EXPERT_FF, D), DEEPNORM_BETA * EXPERT_FF ** -0.5),
        'ln2_g': 1.0 + nrm((L, D), 0.1),
        'ln2_b': nrm((L, D), 0.01),
    }


def reference(x_prompt, x_sample, cache_k, cache_v, c, c_ctx, w_mod, b_mod, w_in, w_fft, w_pool,
              pool_scale, sgu_ln_g, sgu_ln_b, w_sgu, b_sgu, q_norm_g, k_norm_g, w_out, ln1_g, ln1_b,
              w_router_group, b_router_group, w_router_expert, b_router_expert, w_gate, w_up,
              w_down, ln2_g, ln2_b):
    rows = x_sample.shape[1] // GRID_W
    rope = rope_tables(rows)
    silu_c = jax.nn.silu(c)
    silu_ctx = jax.nn.silu(c_ctx)
    yp, ys = x_prompt, x_sample
    ks, vs = [], []
    for l in range(DEPTH):
        lp = {
            'w_in': w_in[l], 'w_fft': w_fft[l], 'w_pool': w_pool[l], 'pool_scale': pool_scale[l],
            'sgu_ln_g': sgu_ln_g[l], 'sgu_ln_b': sgu_ln_b[l], 'w_sgu': w_sgu[l], 'b_sgu': b_sgu[l],
            'q_norm': q_norm_g[l], 'k_norm': k_norm_g[l], 'w_out': w_out[l],
            'ln1_g': ln1_g[l], 'ln1_b': ln1_b[l],
            'w_rg': w_router_group[l], 'b_rg': b_router_group[l],
            'w_re': w_router_expert[l], 'b_re': b_router_expert[l],
            'w_gate': w_gate[l], 'w_up': w_up[l], 'w_down': w_down[l],
            'ln2_g': ln2_g[l], 'ln2_b': ln2_b[l],
        }
        mod_ctx = silu_ctx @ w_mod[l] + b_mod[l]
        mod_lat = (silu_c @ w_mod[l] + b_mod[l])[:, None, :]
        yp, k_l, v_l = trunk_layer(yp, mod_ctx, lp)
        ks.append(k_l)
        vs.append(v_l)
        ys, _, _ = trunk_layer(ys, mod_lat, lp, ctx_kv=(cache_k[:, l], cache_v[:, l]), rope=rope)
    new_k = jnp.stack(ks, axis=1)
    new_v = jnp.stack(vs, axis=1)
    return (yp, ys, new_k, new_v)
```

```python
import functools

import numpy as np
import jax
import jax.numpy as jnp
from jax import lax
from jax.experimental import pallas as pl
from jax.experimental.pallas import tpu as pltpu

f32 = jnp.float32
bf16 = jnp.bfloat16
i32 = jnp.int32

D_MODEL = 1024
BATCH = 16
SEQ = 256
DEPTH = 4
DEC_BATCH = 2
DEC_SEQ = 4096
PAST_LEN = 512
GRID_W = 64
FFT_W = 256
POOL_W = 256
POOL_WINDOWS = (2, 4, 8, 16)
POOL_GROUP = 64
SGU_W = 256
SGU_HEADS = 4
CHUNK = 128
HEAD_DIM = 64
ATTN_W = 256
N_HEADS = 4
N_KV_HEADS = 2
KV_W = 128
IN_W = 1536
ROPE_THETA = 10000.0
N_GROUPS = 4
EXPERTS_PER_GROUP = 8
N_EXPERTS = 32
EXPERT_FF = 512
DEEPNORM_ALPHA = float((2 * DEPTH) ** 0.25)
LN_EPS = 1e-5
RMS_EPS = 1e-6

T_CTX = BATCH * SEQ
T_LAT = DEC_BATCH * DEC_SEQ
T_ALL = T_CTX + T_LAT
SEG = 4096
N_SEG = T_ALL // SEG

TM = 512
POOL_TB = 512
POOL_HALO = 8
FFT_TR = 512
ATT_TQ = 256
MOE_TM = 256
MOE_ROWS = 2 * T_ALL
MOE_NT = MOE_ROWS // MOE_TM + N_EXPERTS
ROUTE_E0 = 32
VMEM_LIMIT = 56 * 1024 * 1024


def _cparams(sem):
    return pltpu.CompilerParams(dimension_semantics=sem, vmem_limit_bytes=VMEM_LIMIT)


def _split_hi_lo(a):
    hi = a.astype(bf16)
    lo = (a - hi.astype(f32)).astype(bf16)
    return hi, lo


def _dot(a, b):
    return jnp.dot(a, b, preferred_element_type=f32)


def _mod_kernel(c_ref, w_ref, b_ref, o_ref):
    c = c_ref[...]
    s = c * jax.nn.sigmoid(c)
    s_hi, s_lo = _split_hi_lo(s)
    w_hi, w_lo = _split_hi_lo(w_ref[...])
    o_ref[...] = _dot(s_hi, w_hi) + _dot(s_hi, w_lo) + _dot(s_lo, w_hi) + b_ref[...]


def _modulation(cond8, w_mod, b_mod):
    tn = 1536
    return pl.pallas_call(
        _mod_kernel,
        grid=(DEPTH, 6 * D_MODEL // tn),
        in_specs=[
            pl.BlockSpec((8, D_MODEL), lambda l, j: (0, 0)),
            pl.BlockSpec((None, D_MODEL, tn), lambda l, j: (l, 0, j)),
            pl.BlockSpec((None, 1, tn), lambda l, j: (l, 0, j)),
        ],
        out_specs=pl.BlockSpec((None, 8, tn), lambda l, j: (l, 0, j)),
        out_shape=jax.ShapeDtypeStruct((DEPTH, 8, 6 * D_MODEL), f32),
        compiler_params=_cparams(("arbitrary", "arbitrary")),
        name="modulation",
    )(cond8, w_mod, b_mod.reshape(DEPTH, 1, 6 * D_MODEL))


def _head_rms(x, ones_bd, gain):
    sq = x * x
    hi, lo = _split_hi_lo(sq)
    ss = _dot(hi, ones_bd) + _dot(lo, ones_bd)
    return x * lax.rsqrt(ss * (1.0 / HEAD_DIM) + RMS_EPS) * gain


def _rope(x, cos, sin_a, sin_b):
    w = x.shape[-1]
    q4 = HEAD_DIM // 4
    return x * cos + pltpu.roll(x, w - q4, 1) * sin_a + pltpu.roll(x, q4, 1) * sin_b


def _dup_half(x, first):
    lane = lax.broadcasted_iota(i32, x.shape, 1)
    r = pltpu.roll(x, HEAD_DIM, 1)
    if first:
        return jnp.where(lane < HEAD_DIM, x, r)
    return jnp.where(lane >= HEAD_DIM, x, r)


def _gelu_tanh(x):
    c = np.sqrt(2.0 / np.pi).astype(np.float32)
    return x * (0.5 * (1.0 + jnp.tanh(c * (x + 0.044715 * (x * x * x)))))


def _inproj_kernel(x_ref, mod_ref, win_ref, csc_ref, wsgu_ref, bsgu_ref, lng_ref, lnb_ref,
                   qg_ref, kg_ref, cos_ref, sina_ref, sinb_ref, ones_ref,
                   pq_ref, pool_ref, sgu_ref, q_ref, kd_ref, vd_ref, nk_ref, nv_ref):
    x = x_ref[...]
    mod = mod_ref[...]
    h = (x * (1.0 + mod[1:2]) + mod[0:1]).astype(bf16)
    proj = _dot(h, win_ref[...])

    a = proj[:, 0:FFT_W].astype(bf16)
    pq_ref[...] = _dot(a, csc_ref[...]).astype(bf16)

    pool_ref[...] = proj[:, FFT_W:FFT_W + POOL_W]

    o = FFT_W + POOL_W
    hgu = _gelu_tanh(proj[:, o:o + 2 * SGU_W])
    u = hgu[:, :SGU_W]
    v = hgu[:, SGU_W:]
    mu = jnp.mean(v, axis=-1, keepdims=True)
    vc = v - mu
    var = jnp.mean(vc * vc, axis=-1, keepdims=True)
    v = vc * lax.rsqrt(var + LN_EPS) * lng_ref[...] + lnb_ref[...]
    lane = lax.broadcasted_iota(i32, (CHUNK, SGU_W), 1)
    head = lane // (SGU_W // SGU_HEADS)
    wcat = wsgu_ref[...]
    for cidx in range(TM // CHUNK):
        rows = slice(cidx * CHUNK, (cidx + 1) * CHUNK)
        vch = v[rows]
        vblk = jnp.concatenate(
            [jnp.where(head == g, vch, 0.0) for g in range(SGU_HEADS)], axis=0).astype(bf16)
        sp = _dot(wcat, vblk) + bsgu_ref[...]
        sgu_ref[rows, :] = (u[rows] * sp).astype(bf16)

    o = o + 2 * SGU_W
    ones_bd = ones_ref[...]
    cos = cos_ref[...]
    sin_a = sina_ref[...]
    sin_b = sinb_ref[...]
    q = _head_rms(proj[:, o:o + ATTN_W], ones_bd, qg_ref[...])
    q = _rope(q, cos, sin_a, sin_b) * (HEAD_DIM ** -0.5)
    q_ref[...] = q.astype(bf16)
    o = o + ATTN_W
    k = _head_rms(proj[:, o:o + KV_W], ones_bd[:KV_W, :KV_W], kg_ref[...])
    nk_ref[...] = k
    k = _rope(k, cos[:, :KV_W], sin_a[:, :KV_W], sin_b[:, :KV_W])
    kd_ref[0] = _dup_half(k, True).astype(bf16)
    kd_ref[1] = _dup_half(k, False).astype(bf16)
    o = o + KV_W
    vv = proj[:, o:o + KV_W]
    nv_ref[...] = vv
    vd_ref[0] = _dup_half(vv, True).astype(bf16)
    vd_ref[1] = _dup_half(vv, False).astype(bf16)


def _rope_block(i):
    nlat = DEC_SEQ // TM
    nctx = T_CTX // TM
    return jnp.where(i < nctx, nlat, (i - nctx) % nlat)


def _inproj(x, mod, l, w):
    nt = T_ALL // TM
    tile = lambda wd: pl.BlockSpec((TM, wd), lambda i: (i, 0))
    const = lambda shape: pl.BlockSpec(shape, lambda i: (0,) * len(shape))
    rope_spec = pl.BlockSpec((TM, ATTN_W), lambda i: (_rope_block(i), 0))
    return pl.pallas_call(
        _inproj_kernel,
        grid=(nt,),
        in_specs=[
            tile(D_MODEL),
            pl.BlockSpec((None, None, 6, D_MODEL), lambda i: (l, i // (SEG // TM), 0, 0)),
            pl.BlockSpec((None, D_MODEL, IN_W), lambda i: (l, 0, 0)),
            const((FFT_W, 2 * FFT_W)),
            pl.BlockSpec((None, CHUNK, SGU_HEADS * CHUNK), lambda i: (l, 0, 0)),
            pl.BlockSpec((None, CHUNK, SGU_W), lambda i: (l, 0, 0)),
            pl.BlockSpec((None, 1, SGU_W), lambda i: (l, 0, 0)),
            pl.BlockSpec((None, 1, SGU_W), lambda i: (l, 0, 0)),
            pl.BlockSpec((None, 1, ATTN_W), lambda i: (l, 0, 0)),
            pl.BlockSpec((None, 1, KV_W), lambda i: (l, 0, 0)),
            rope_spec, rope_spec, rope_spec,
            const((ATTN_W, ATTN_W)),
        ],
        out_specs=[
            tile(2 * FFT_W), tile(POOL_W), tile(SGU_W), tile(ATTN_W),
            pl.BlockSpec((N_KV_HEADS, TM, KV_W), lambda i: (0, i, 0)),
            pl.BlockSpec((N_KV_HEADS, TM, KV_W), lambda i: (0, i, 0)),
            tile(KV_W), tile(KV_W),
        ],
        out_shape=[
            jax.ShapeDtypeStruct((T_ALL, 2 * FFT_W), bf16),
            jax.ShapeDtypeStruct((T_ALL, POOL_W), f32),
            jax.ShapeDtypeStruct((T_ALL, SGU_W), bf16),
            jax.ShapeDtypeStruct((T_ALL, ATTN_W), bf16),
            jax.ShapeDtypeStruct((N_KV_HEADS, T_ALL, KV_W), bf16),
            jax.ShapeDtypeStruct((N_KV_HEADS, T_ALL, KV_W), bf16),
            jax.ShapeDtypeStruct((T_ALL, KV_W), f32),
            jax.ShapeDtypeStruct((T_ALL, KV_W), f32),
        ],
        compiler_params=_cparams(("arbitrary",)),
        name="inproj",
    )(x, mod, w["w_in"], w["csc"], w["w_sgu"], w["b_sgu"], w["sgu_ln_g"], w["sgu_ln_b"],
      w["q_norm_g"], w["k_norm_g"], w["rope_cos"], w["rope_sin_a"], w["rope_sin_b"], w["ones_bd"])


def _pool_kernel(prev_ref, cur_ref, next_ref, wp_ref, scale_ref, o_ref):
    i = pl.program_id(0)
    n = jnp.where(i < T_CTX // POOL_TB, SEQ, DEC_SEQ)
    hl = POOL_HALO
    ext = jnp.concatenate([prev_ref[POOL_TB - hl:, :], cur_ref[...], next_ref[:hl, :]], axis=0)
    rows = POOL_TB + 2 * hl
    r = lax.broadcasted_iota(i32, (rows, 1), 0)
    pos = (i * POOL_TB + r - hl) & (n - 1)

    def back(a, s):
        return jnp.where(pos >= s, pltpu.roll(a, s, 0), 0.0)

    def fwd(a, s):
        return jnp.where(pos + s < n, pltpu.roll(a, rows - s, 0), 0.0)

    bsum = [back(ext, 1)]
    fsum = [ext]
    for k in range(3):
        s = 1 << k
        bsum.append(bsum[k] + back(bsum[k], s))
        fsum.append(fsum[k] + fwd(fsum[k], s))
    lane = lax.broadcasted_iota(i32, (1, POOL_W), 1)
    grp = lane // POOL_GROUP
    win = bsum[3] + fsum[3]
    half = jnp.full((1, POOL_W), POOL_WINDOWS[3] // 2, i32)
    for g in (2, 1, 0):
        win = jnp.where(grp == g, bsum[g] + fsum[g], win)
        half = jnp.where(grp == g, POOL_WINDOWS[g] // 2, half)
    cnt = (jnp.minimum(pos + half, n) - jnp.maximum(pos - half, 0)).astype(f32)
    y = (win / cnt - ext)[hl:hl + POOL_TB]
    o_ref[...] = (_dot(y.astype(bf16), wp_ref[...]) * scale_ref[...]).astype(bf16)


def _pool(p, l, w):
    nt = T_ALL // POOL_TB
    blk = lambda f: pl.BlockSpec((POOL_TB, POOL_W), lambda i: (f(i), 0))
    return pl.pallas_call(
        _pool_kernel,
        grid=(nt,),
        in_specs=[
            blk(lambda i: jnp.maximum(i - 1, 0)), blk(lambda i: i),
            blk(lambda i: jnp.minimum(i + 1, nt - 1)),
            pl.BlockSpec((None, POOL_W, POOL_W), lambda i: (l, 0, 0)),
            pl.BlockSpec((None, 1, POOL_W), lambda i: (l, 0, 0)),
        ],
        out_specs=blk(lambda i: i),
        out_shape=jax.ShapeDtypeStruct((T_ALL, POOL_W), bf16),
        compiler_params=_cparams(("arbitrary",)),
        name="pool",
    )(p, p, p, w["w_pool_bd"], w["pool_scale"])


def _seqdft_kernel(m_ref, pq_ref, w_ref, o_ref, *, n):
    f = _dot(m_ref[:, :n], pq_ref[:, :FFT_W]) + _dot(m_ref[:, n:], pq_ref[:, FFT_W:])
    o_ref[...] = _dot(f.astype(bf16), w_ref[...]).astype(bf16)


def _seqdft(pq, m, l, w, *, n, tr, nseq, row0):
    nr = n // tr
    b0 = row0 // n
    return pl.pallas_call(
        functools.partial(_seqdft_kernel, n=n),
        grid=(nr, nseq),
        in_specs=[
            pl.BlockSpec((tr, 2 * n), lambda i, b: (i, 0)),
            pl.BlockSpec((n, 2 * FFT_W), lambda i, b: (b0 + b, 0)),
            pl.BlockSpec((None, FFT_W, FFT_W), lambda i, b: (l, 0, 0)),
        ],
        out_specs=pl.BlockSpec((tr, FFT_W), lambda i, b: (b * nr + i, 0)),
        out_shape=jax.ShapeDtypeStruct((nseq * n, FFT_W), bf16),
        compiler_params=_cparams(("arbitrary", "arbitrary")),
        name="seqdft_%d" % n,
    )(m, pq, w["w_fft"])


def _attn_kernel(*refs, has_cache):
    if has_cache:
        q_ref, k_ref, v_ref, kc_ref, vc_ref, o_ref = refs
    else:
        q_ref, k_ref, v_ref, o_ref = refs
    q = q_ref[...]
    tq = q.shape[0]
    lane = lax.broadcasted_iota(i32, q.shape, 1)
    zero = jnp.zeros_like(q)
    qs = jnp.concatenate([jnp.where(lane < HEAD_DIM, q, zero),
                          jnp.where(lane >= HEAD_DIM, q, zero)], axis=0)
    nt = (((1,), (1,)), ((), ()))
    s = lax.dot_general(qs, k_ref[...], nt, preferred_element_type=f32)
    m = jnp.max(s, axis=-1, keepdims=True)
    if has_cache:
        sc = lax.dot_general(qs, kc_ref[...], nt, preferred_element_type=f32)
        m = jnp.maximum(m, jnp.max(sc, axis=-1, keepdims=True))
    p = jnp.exp(s - m)
    den = jnp.sum(p, axis=-1, keepdims=True)
    acc = _dot(p.astype(bf16), v_ref[...])
    if has_cache:
        pc = jnp.exp(sc - m)
        den = den + jnp.sum(pc, axis=-1, keepdims=True)
        acc = acc + _dot(pc.astype(bf16), vc_ref[...])
    out = acc / den
    o_ref[...] = jnp.where(lane < HEAD_DIM, out[:tq], out[tq:]).astype(bf16)


def _attention(q, kd, vd, cache, *, n, tq, nseq, row0):
    nq = n // tq
    b0 = row0 // n
    q0 = row0 // tq
    in_specs = [
        pl.BlockSpec((tq, 2 * HEAD_DIM), lambda b, h, i: (q0 + b * nq + i, h)),
        pl.BlockSpec((None, n, KV_W), lambda b, h, i: (h, b0 + b, 0)),
        pl.BlockSpec((None, n, KV_W), lambda b, h, i: (h, b0 + b, 0)),
    ]
    args = [q, kd, vd]
    if cache is not None:
        cspec = pl.BlockSpec((None, None, PAST_LEN, KV_W), lambda b, h, i: (h, b, 0, 0))
        in_specs += [cspec, cspec]
        args += list(cache)
    return pl.pallas_call(
        functools.partial(_attn_kernel, has_cache=cache is not None),
        grid=(nseq, N_KV_HEADS, nq),
        in_specs=in_specs,
        out_specs=pl.BlockSpec((tq, 2 * HEAD_DIM), lambda b, h, i: (b * nq + i, h)),
        out_shape=jax.ShapeDtypeStruct((nseq * n, ATTN_W), bf16),
        compiler_params=_cparams(("arbitrary", "arbitrary", "arbitrary")),
        name="attention_%d" % n,
    )(*args)


def _layer_norm(x, g, b):
    mu = jnp.mean(x, axis=-1, keepdims=True)
    xc = x - mu
    var = jnp.mean(xc * xc, axis=-1, keepdims=True)
    return xc * lax.rsqrt(var + LN_EPS) * g + b


def _outproj_kernel(x_ref, mod_ref, f_ref, p_ref, s_ref, a_ref, wout_ref, g_ref, b_ref,
                    wrh_ref, wrl_ref, br_ref, x1_ref, h2_ref, route_ref):
    mod = mod_ref[...]
    mix = (_dot(f_ref[...], wout_ref[0:256, :]) + _dot(p_ref[...], wout_ref[256:512, :])
           + _dot(s_ref[...], wout_ref[512:768, :]) + _dot(a_ref[...], wout_ref[768:1024, :]))
    x1 = _layer_norm(DEEPNORM_ALPHA * x_ref[...] + mod[2:3] * mix, g_ref[...], b_ref[...])
    x1_ref[...] = x1
    h2 = x1 * (1.0 + mod[4:5]) + mod[3:4]
    h2_ref[...] = h2.astype(bf16)

    h_hi, h_lo = _split_hi_lo(h2)
    logits = (_dot(h_hi, wrh_ref[...]) + _dot(h_hi, wrl_ref[...]) + _dot(h_lo, wrh_ref[...])
              + br_ref[...])
    lane = lax.broadcasted_iota(i32, logits.shape, 1).astype(f32)
    neg = jnp.float32(-jnp.inf)
    big = jnp.float32(1 << 20)
    gl = jnp.where(lane < N_GROUPS, logits, neg)
    gmax = jnp.max(gl, axis=-1, keepdims=True)
    gsel = jnp.min(jnp.where(gl == gmax, lane, big), axis=-1, keepdims=True)
    pg = 1.0 / jnp.sum(jnp.exp(gl - gmax), axis=-1, keepdims=True)
    e_lo = ROUTE_E0 + gsel * EXPERTS_PER_GROUP
    el = jnp.where((lane >= e_lo) & (lane < e_lo + EXPERTS_PER_GROUP), logits, neg)
    v1 = jnp.max(el, axis=-1, keepdims=True)
    i1 = jnp.min(jnp.where(el == v1, lane, big), axis=-1, keepdims=True)
    el2 = jnp.where(lane == i1, neg, el)
    v2 = jnp.max(el2, axis=-1, keepdims=True)
    i2 = jnp.min(jnp.where(el2 == v2, lane, big), axis=-1, keepdims=True)
    e2 = jnp.exp(v2 - v1)
    w1 = pg / (1.0 + e2)
    w2 = pg * e2 / (1.0 + e2)
    route = jnp.where(lane == 0, i1 - ROUTE_E0,
                      jnp.where(lane == 1, i2 - ROUTE_E0,
                                jnp.where(lane == 2, w1, jnp.where(lane == 3, w2, 0.0))))
    route_ref[...] = route


def _outproj(x, mod, fo, po, so, ao, l, w):
    nt = T_ALL // TM
    tile = lambda wd: pl.BlockSpec((TM, wd), lambda i: (i, 0))
    vec = lambda wd: pl.BlockSpec((None, 1, wd), lambda i: (l, 0, 0))
    return pl.pallas_call(
        _outproj_kernel,
        grid=(nt,),
        in_specs=[
            tile(D_MODEL),
            pl.BlockSpec((None, None, 6, D_MODEL), lambda i: (l, i // (SEG // TM), 0, 0)),
            tile(FFT_W), tile(POOL_W), tile(SGU_W), tile(ATTN_W),
            pl.BlockSpec((None, D_MODEL, D_MODEL), lambda i: (l, 0, 0)),
            vec(D_MODEL), vec(D_MODEL),
            pl.BlockSpec((None, D_MODEL, 128), lambda i: (l, 0, 0)),
            pl.BlockSpec((None, D_MODEL, 128), lambda i: (l, 0, 0)),
            vec(128),
        ],
        out_specs=[tile(D_MODEL), tile(D_MODEL), tile(128)],
        out_shape=[
            jax.ShapeDtypeStruct((T_ALL, D_MODEL), f32),
            jax.ShapeDtypeStruct((T_ALL, D_MODEL), bf16),
            jax.ShapeDtypeStruct((T_ALL, 128), f32),
        ],
        compiler_params=_cparams(("arbitrary",)),
        name="outproj",
    )(x, mod, fo, po, so, ao, w["w_out"], w["ln1_g"], w["ln1_b"], w["w_r_hi"], w["w_r_lo"], w["b_r"])


def _moe_kernel(te_ref, nused_ref, xs_ref, rw_ref, wg_ref, wu_ref, wd_ref, ys_ref,
                wg_bf, wu_bf, wd_bf):
    i = pl.program_id(0)

    @pl.when(i < nused_ref[0])
    def _():
        prev = te_ref[jnp.maximum(i - 1, 0)]

        @pl.when((i == 0) | (te_ref[i] != prev))
        def _():
            wg_bf[...] = wg_ref[...].astype(bf16)
            wu_bf[...] = wu_ref[...].astype(bf16)
            wd_bf[...] = wd_ref[...].astype(bf16)

        x = xs_ref[...]
        hg = _dot(x, wg_bf[...])
        hu = _dot(x, wu_bf[...])
        act = (hg * jax.nn.sigmoid(hg)) * hu * rw_ref[...]
        ys_ref[...] = _dot(act.astype(bf16), wd_bf[...])

    @pl.when(i >= nused_ref[0])
    def _():
        ys_ref[...] = jnp.zeros_like(ys_ref)


def _moe(tile_expert, nused, xs, roww, l, w_gate, w_up, w_down):
    wspec = lambda a, b: pl.BlockSpec((None, None, a, b), lambda i, te, nu: (l, te[i], 0, 0))
    grid_spec = pltpu.PrefetchScalarGridSpec(
        num_scalar_prefetch=2,
        grid=(MOE_NT,),
        in_specs=[
            pl.BlockSpec((MOE_TM, D_MODEL), lambda i, te, nu: (i, 0)),
            pl.BlockSpec((MOE_TM, 1), lambda i, te, nu: (i, 0)),
            wspec(D_MODEL, EXPERT_FF), wspec(D_MODEL, EXPERT_FF), wspec(EXPERT_FF, D_MODEL),
        ],
        out_specs=pl.BlockSpec((MOE_TM, D_MODEL), lambda i, te, nu: (i, 0)),
        scratch_shapes=[
            pltpu.VMEM((D_MODEL, EXPERT_FF), bf16),
            pltpu.VMEM((D_MODEL, EXPERT_FF), bf16),
            pltpu.VMEM((EXPERT_FF, D_MODEL), bf16),
        ],
    )
    return pl.pallas_call(
        _moe_kernel,
        grid_spec=grid_spec,
        out_shape=jax.ShapeDtypeStruct((MOE_NT * MOE_TM, D_MODEL), f32),
        compiler_params=_cparams(("arbitrary",)),
        name="experts",
    )(tile_expert, nused, xs, roww, w_gate, w_up, w_down)


def _route_tables(route):
    eid = route[:, 0:2].astype(i32)
    e_flat = eid.T.reshape(-1)
    w_flat = route[:, 2:4].T.reshape(-1)
    order = jnp.argsort(e_flat, stable=True).astype(i32)
    e_sorted = e_flat[order]
    counts = jnp.sum((e_flat[:, None] == jnp.arange(N_EXPERTS, dtype=i32)[None, :]).astype(i32), axis=0)
    tiles_e = (counts + MOE_TM - 1) // MOE_TM
    tiles_cum = jnp.cumsum(tiles_e)
    pstart = (tiles_cum - tiles_e) * MOE_TM
    ustart = jnp.cumsum(counts) - counts
    nused = tiles_cum[-1]
    tile_ids = jnp.arange(MOE_NT, dtype=i32)
    te = jnp.searchsorted(tiles_cum, tile_ids, side="right").astype(i32)
    last_e = jnp.searchsorted(tiles_cum, nused - 1, side="right").astype(i32)
    te = jnp.where(tile_ids < nused, te, last_e)
    r = jnp.arange(MOE_NT * MOE_TM, dtype=i32)
    er = te[r // MOE_TM]
    off = r - pstart[er]
    valid = (off < counts[er]) & (r // MOE_TM < nused)
    pair = order[jnp.clip(ustart[er] + off, 0, MOE_ROWS - 1)]
    src_tok = jnp.where(valid, pair % T_ALL, 0)
    roww = jnp.where(valid, w_flat[pair], 0.0)
    rank = jnp.arange(MOE_ROWS, dtype=i32)
    dest_sorted = rank - ustart[e_sorted] + pstart[e_sorted]
    pos = jnp.zeros((MOE_ROWS,), i32).at[order].set(dest_sorted)
    return te, nused.reshape(1).astype(i32), src_tok, roww.reshape(-1, 1), pos


def _combine_kernel(x1_ref, mod_ref, y_ref, g_ref, b_ref, o_ref):
    mod = mod_ref[...]
    moe = y_ref[0] + y_ref[1]
    o_ref[...] = _layer_norm(DEEPNORM_ALPHA * x1_ref[...] + mod[5:6] * moe, g_ref[...], b_ref[...])


def _combine(x1, mod, yc, l, w):
    nt = T_ALL // TM
    vec = pl.BlockSpec((None, 1, D_MODEL), lambda i: (l, 0, 0))
    return pl.pallas_call(
        _combine_kernel,
        grid=(nt,),
        in_specs=[
            pl.BlockSpec((TM, D_MODEL), lambda i: (i, 0)),
            pl.BlockSpec((None, None, 6, D_MODEL), lambda i: (l, i // (SEG // TM), 0, 0)),
            pl.BlockSpec((2, TM, D_MODEL), lambda i: (0, i, 0)),
            vec, vec,
        ],
        out_specs=pl.BlockSpec((TM, D_MODEL), lambda i: (i, 0)),
        out_shape=jax.ShapeDtypeStruct((T_ALL, D_MODEL), f32),
        compiler_params=_cparams(("arbitrary",)),
        name="combine",
    )(x1, mod, yc, w["ln2_g"], w["ln2_b"])


def _dft_cos_sin(n, scale):
    k = jnp.arange(n, dtype=i32)
    ang = ((k[:, None] * k[None, :]) % n).astype(f32) * np.float32(2.0 * np.pi / n)
    return jnp.cos(ang) * scale, jnp.sin(ang) * scale


def _seq_dft_matrix(n):
    c, s = _dft_cos_sin(n, np.float32(n ** -0.5))
    return jnp.concatenate([c, -s], axis=1).astype(bf16)


def _rope_tables():
    rows = DEC_SEQ // GRID_W
    row = jnp.repeat(jnp.arange(rows), GRID_W).astype(f32)
    col = jnp.tile(jnp.arange(GRID_W), rows).astype(f32)
    n_freq = HEAD_DIM // 4
    inv = ROPE_THETA ** (-jnp.arange(n_freq, dtype=f32) / n_freq)
    ar = row[:, None] * inv
    ac = col[:, None] * inv
    ang = jnp.concatenate([ar, ar, ac, ac], axis=-1)
    cos = jnp.tile(jnp.cos(ang), (1, N_HEADS))
    sin = jnp.tile(jnp.sin(ang), (1, N_HEADS))
    first = (jnp.arange(ATTN_W) % (HEAD_DIM // 2)) < n_freq
    sin_a = jnp.where(first[None, :], -sin, 0.0)
    sin_b = jnp.where(first[None, :], 0.0, sin)
    ident = jnp.zeros((TM, ATTN_W), f32)
    return (jnp.concatenate([cos, ident + 1.0], axis=0),
            jnp.concatenate([sin_a, ident], axis=0),
            jnp.concatenate([sin_b, ident], axis=0))


def _dup_cache(cache):
    c = jnp.transpose(cache, (1, 3, 0, 2, 4))
    return jnp.concatenate([c, c], axis=-1).astype(bf16)


def kernel(x_prompt, x_sample, cache_k, cache_v, c, c_ctx, w_mod, b_mod, w_in, w_fft, w_pool, pool_scale, sgu_ln_g, sgu_ln_b, w_sgu, b_sgu, q_norm_g, k_norm_g, w_out, ln1_g, ln1_b, w_router_group, b_router_group, w_router_expert, b_router_expert, w_gate, w_up, w_down, ln2_g, ln2_b):
    L = DEPTH
    x = jnp.concatenate([x_prompt.reshape(T_CTX, D_MODEL), x_sample.reshape(T_LAT, D_MODEL)], axis=0)

    cond8 = jnp.concatenate([c_ctx[None, :], c, jnp.zeros((8 - 1 - DEC_BATCH, D_MODEL), f32)], axis=0)
    mod = _modulation(cond8, w_mod, b_mod)[:, :N_SEG].reshape(L, N_SEG, 6, D_MODEL)

    cc, sc = _dft_cos_sin(FFT_W, np.float32(FFT_W ** -0.5))
    rope_cos, rope_sin_a, rope_sin_b = _rope_tables()
    head_id = jnp.arange(ATTN_W) // HEAD_DIM
    eye_g = jnp.eye(len(POOL_WINDOWS), dtype=f32)
    w_r = jnp.zeros((L, D_MODEL, 128), f32)
    w_r = w_r.at[:, :, :N_GROUPS].set(w_router_group).at[:, :, ROUTE_E0:ROUTE_E0 + N_EXPERTS].set(w_router_expert)
    b_r = jnp.zeros((L, 1, 128), f32)
    b_r = b_r.at[:, 0, :N_GROUPS].set(b_router_group).at[:, 0, ROUTE_E0:ROUTE_E0 + N_EXPERTS].set(b_router_expert)
    w_r_hi, w_r_lo = _split_hi_lo(w_r)
    w = {
        "w_in": w_in.astype(bf16),
        "csc": jnp.concatenate([cc, sc], axis=1).astype(bf16),
        "w_sgu": jnp.transpose(w_sgu, (0, 2, 1, 3)).reshape(L, CHUNK, SGU_HEADS * CHUNK).astype(bf16),
        "b_sgu": jnp.repeat(jnp.transpose(b_sgu, (0, 2, 1)), SGU_W // SGU_HEADS, axis=2),
        "sgu_ln_g": sgu_ln_g.reshape(L, 1, SGU_W),
        "sgu_ln_b": sgu_ln_b.reshape(L, 1, SGU_W),
        "q_norm_g": jnp.tile(q_norm_g, (1, N_HEADS)).reshape(L, 1, ATTN_W),
        "k_norm_g": jnp.tile(k_norm_g, (1, N_KV_HEADS)).reshape(L, 1, KV_W),
        "rope_cos": rope_cos, "rope_sin_a": rope_sin_a, "rope_sin_b": rope_sin_b,
        "ones_bd": (head_id[:, None] == head_id[None, :]).astype(bf16),
        "w_pool_bd": jnp.einsum("lgcd,gh->lgchd", w_pool, eye_g).reshape(L, POOL_W, POOL_W).astype(bf16),
        "pool_scale": pool_scale.reshape(L, 1, POOL_W),
        "w_fft": w_fft.astype(bf16),
        "w_out": w_out.astype(bf16),
        "ln1_g": ln1_g.reshape(L, 1, D_MODEL), "ln1_b": ln1_b.reshape(L, 1, D_MODEL),
        "ln2_g": ln2_g.reshape(L, 1, D_MODEL), "ln2_b": ln2_b.reshape(L, 1, D_MODEL),
        "w_r_hi": w_r_hi, "w_r_lo": w_r_lo, "b_r": b_r,
    }
    m_ctx = _seq_dft_matrix(SEQ)
    m_lat = _seq_dft_matrix(DEC_SEQ)
    kc_all = _dup_cache(cache_k)
    vc_all = _dup_cache(cache_v)

    new_k, new_v = [], []
    for l in range(L):
        pq, praw, sgu, q, kd, vd, nk, nv = _inproj(x, mod, l, w)
        new_k.append(nk[:T_CTX].reshape(BATCH, SEQ, N_KV_HEADS, HEAD_DIM))
        new_v.append(nv[:T_CTX].reshape(BATCH, SEQ, N_KV_HEADS, HEAD_DIM))
        po = _pool(praw, l, w)
        fo = jnp.concatenate([
            _seqdft(pq, m_ctx, l, w, n=SEQ, tr=SEQ, nseq=BATCH, row0=0),
            _seqdft(pq, m_lat, l, w, n=DEC_SEQ, tr=FFT_TR, nseq=DEC_BATCH, row0=T_CTX)], axis=0)
        ao = jnp.concatenate([
            _attention(q, kd, vd, None, n=SEQ, tq=SEQ, nseq=BATCH, row0=0),
            _attention(q, kd, vd, (kc_all[l], vc_all[l]), n=DEC_SEQ, tq=ATT_TQ, nseq=DEC_BATCH,
                       row0=T_CTX)], axis=0)
        x1, h2, route = _outproj(x, mod, fo, po, sgu, ao, l, w)
        te, nused, src_tok, roww, pos = _route_tables(route)
        xs = jnp.take(h2, src_tok, axis=0)
        ys = _moe(te, nused, xs, roww, l, w_gate, w_up, w_down)
        yc = jnp.take(ys, pos, axis=0).reshape(2, T_ALL, D_MODEL)
        x = _combine(x1, mod, yc, l, w)

    y_prompt = x[:T_CTX].reshape(BATCH, SEQ, D_MODEL)
    y_sample = x[T_CTX:].reshape(DEC_BATCH, DEC_SEQ, D_MODEL)
    return (y_prompt, y_sample, jnp.stack(new_k, axis=1), jnp.stack(new_v, axis=1))
```

```python
import functools

import numpy as np
import jax
import jax.numpy as jnp
from jax import lax
from jax.experimental import pallas as pl
from jax.experimental.pallas import tpu as pltpu

f32 = jnp.float32
bf16 = jnp.bfloat16
i32 = jnp.int32

D_MODEL = 1024
BATCH = 16
SEQ = 256
DEPTH = 4
DEC_BATCH = 2
DEC_SEQ = 4096
PAST_LEN = 512
GRID_W = 64
FFT_W = 256
POOL_W = 256
POOL_WINDOWS = (2, 4, 8, 16)
POOL_GROUP = 64
SGU_W = 256
SGU_HEADS = 4
CHUNK = 128
HEAD_DIM = 64
ATTN_W = 256
N_HEADS = 4
N_KV_HEADS = 2
KV_W = 128
IN_W = 1536
ROPE_THETA = 10000.0
N_GROUPS = 4
EXPERTS_PER_GROUP = 8
N_EXPERTS = 32
EXPERT_FF = 512
DEEPNORM_ALPHA = float((2 * DEPTH) ** 0.25)
LN_EPS = 1e-5
RMS_EPS = 1e-6

T_CTX = BATCH * SEQ
T_LAT = DEC_BATCH * DEC_SEQ
T_ALL = T_CTX + T_LAT
SEG = 4096
N_SEG = T_ALL // SEG

TM = 512
POOL_TB = 512
POOL_HALO = 8
FFT_TR = 512
ATT_TQ = 256
MOE_TM = 256
MOE_ROWS = 2 * T_ALL
MOE_NT = MOE_ROWS // MOE_TM + N_EXPERTS
PLAN_TB = 2048
DISP_TB = 512
ROUTE_E0 = 32
VMEM_LIMIT = 56 * 1024 * 1024


def _cparams(sem):
    return pltpu.CompilerParams(dimension_semantics=sem, vmem_limit_bytes=VMEM_LIMIT)


def _split_hi_lo(a):
    hi = a.astype(bf16)
    lo = (a - hi.astype(f32)).astype(bf16)
    return hi, lo


def _dot(a, b):
    return jnp.dot(a, b, preferred_element_type=f32)


def _mod_kernel(c_ref, w_ref, b_ref, o_ref):
    c = c_ref[...]
    s = c * jax.nn.sigmoid(c)
    s_hi, s_lo = _split_hi_lo(s)
    w_hi, w_lo = _split_hi_lo(w_ref[...])
    o_ref[...] = _dot(s_hi, w_hi) + _dot(s_hi, w_lo) + _dot(s_lo, w_hi) + b_ref[...]


def _modulation(cond8, w_mod, b_mod):
    tn = 1536
    return pl.pallas_call(
        _mod_kernel,
        grid=(DEPTH, 6 * D_MODEL // tn),
        in_specs=[
            pl.BlockSpec((8, D_MODEL), lambda l, j: (0, 0)),
            pl.BlockSpec((None, D_MODEL, tn), lambda l, j: (l, 0, j)),
            pl.BlockSpec((None, 1, tn), lambda l, j: (l, 0, j)),
        ],
        out_specs=pl.BlockSpec((None, 8, tn), lambda l, j: (l, 0, j)),
        out_shape=jax.ShapeDtypeStruct((DEPTH, 8, 6 * D_MODEL), f32),
        compiler_params=_cparams(("arbitrary", "arbitrary")),
        name="modulation",
    )(cond8, w_mod, b_mod.reshape(DEPTH, 1, 6 * D_MODEL))


def _head_rms(x, ones_bd, gain):
    sq = x * x
    hi, lo = _split_hi_lo(sq)
    ss = _dot(hi, ones_bd) + _dot(lo, ones_bd)
    return x * lax.rsqrt(ss * (1.0 / HEAD_DIM) + RMS_EPS) * gain


def _rope(x, cos, sin_a, sin_b):
    w = x.shape[-1]
    q4 = HEAD_DIM // 4
    return x * cos + pltpu.roll(x, w - q4, 1) * sin_a + pltpu.roll(x, q4, 1) * sin_b


def _dup_half(x, first):
    lane = lax.broadcasted_iota(i32, x.shape, 1)
    r = pltpu.roll(x, HEAD_DIM, 1)
    if first:
        return jnp.where(lane < HEAD_DIM, x, r)
    return jnp.where(lane >= HEAD_DIM, x, r)


def _gelu_tanh(x):
    c = np.sqrt(2.0 / np.pi).astype(np.float32)
    return x * (0.5 * (1.0 + jnp.tanh(c * (x + 0.044715 * (x * x * x)))))


def _inproj_kernel(x_ref, mod_ref, win_ref, csc_ref, wsgu_ref, bsgu_ref, lng_ref, lnb_ref,
                   qg_ref, kg_ref, cos_ref, sina_ref, sinb_ref, ones_ref,
                   pq_ref, pool_ref, sgu_ref, q_ref, kd_ref, vd_ref, nk_ref, nv_ref):
    x = x_ref[...]
    mod = mod_ref[...]
    h = (x * (1.0 + mod[1:2]) + mod[0:1]).astype(bf16)
    proj = _dot(h, win_ref[...])

    a = proj[:, 0:FFT_W].astype(bf16)
    pq_ref[...] = _dot(a, csc_ref[...]).astype(bf16)

    pool_ref[...] = proj[:, FFT_W:FFT_W + POOL_W]

    o = FFT_W + POOL_W
    hgu = _gelu_tanh(proj[:, o:o + 2 * SGU_W])
    u = hgu[:, :SGU_W]
    v = hgu[:, SGU_W:]
    mu = jnp.mean(v, axis=-1, keepdims=True)
    vc = v - mu
    var = jnp.mean(vc * vc, axis=-1, keepdims=True)
    v = vc * lax.rsqrt(var + LN_EPS) * lng_ref[...] + lnb_ref[...]
    lane = lax.broadcasted_iota(i32, (CHUNK, SGU_W), 1)
    head = lane // (SGU_W // SGU_HEADS)
    wcat = wsgu_ref[...]
    for cidx in range(TM // CHUNK):
        rows = slice(cidx * CHUNK, (cidx + 1) * CHUNK)
        vch = v[rows]
        vblk = jnp.concatenate(
            [jnp.where(head == g, vch, 0.0) for g in range(SGU_HEADS)], axis=0).astype(bf16)
        sp = _dot(wcat, vblk) + bsgu_ref[...]
        sgu_ref[rows, :] = (u[rows] * sp).astype(bf16)

    o = o + 2 * SGU_W
    ones_bd = ones_ref[...]
    cos = cos_ref[...]
    sin_a = sina_ref[...]
    sin_b = sinb_ref[...]
    q = _head_rms(proj[:, o:o + ATTN_W], ones_bd, qg_ref[...])
    q = _rope(q, cos, sin_a, sin_b) * (HEAD_DIM ** -0.5)
    q_ref[...] = q.astype(bf16)
    o = o + ATTN_W
    k = _head_rms(proj[:, o:o + KV_W], ones_bd[:KV_W, :KV_W], kg_ref[...])
    nk_ref[...] = k
    k = _rope(k, cos[:, :KV_W], sin_a[:, :KV_W], sin_b[:, :KV_W])
    kd_ref[0] = _dup_half(k, True).astype(bf16)
    kd_ref[1] = _dup_half(k, False).astype(bf16)
    o = o + KV_W
    vv = proj[:, o:o + KV_W]
    nv_ref[...] = vv
    vd_ref[0] = _dup_half(vv, True).astype(bf16)
    vd_ref[1] = _dup_half(vv, False).astype(bf16)


def _rope_block(i):
    nlat = DEC_SEQ // TM
    nctx = T_CTX // TM
    return jnp.where(i < nctx, nlat, (i - nctx) % nlat)


def _inproj(x, mod, l, w):
    nt = T_ALL // TM
    tile = lambda wd: pl.BlockSpec((TM, wd), lambda i: (i, 0))
    const = lambda shape: pl.BlockSpec(shape, lambda i: (0,) * len(shape))
    rope_spec = pl.BlockSpec((TM, ATTN_W), lambda i: (_rope_block(i), 0))
    return pl.pallas_call(
        _inproj_kernel,
        grid=(nt,),
        in_specs=[
            tile(D_MODEL),
            pl.BlockSpec((None, None, 6, D_MODEL), lambda i: (l, i // (SEG // TM), 0, 0)),
            pl.BlockSpec((None, D_MODEL, IN_W), lambda i: (l, 0, 0)),
            const((FFT_W, 2 * FFT_W)),
            pl.BlockSpec((None, CHUNK, SGU_HEADS * CHUNK), lambda i: (l, 0, 0)),
            pl.BlockSpec((None, CHUNK, SGU_W), lambda i: (l, 0, 0)),
            pl.BlockSpec((None, 1, SGU_W), lambda i: (l, 0, 0)),
            pl.BlockSpec((None, 1, SGU_W), lambda i: (l, 0, 0)),
            pl.BlockSpec((None, 1, ATTN_W), lambda i: (l, 0, 0)),
            pl.BlockSpec((None, 1, KV_W), lambda i: (l, 0, 0)),
            rope_spec, rope_spec, rope_spec,
            const((ATTN_W, ATTN_W)),
        ],
        out_specs=[
            tile(2 * FFT_W), tile(POOL_W), tile(SGU_W), tile(ATTN_W),
            pl.BlockSpec((N_KV_HEADS, TM, KV_W), lambda i: (0, i, 0)),
            pl.BlockSpec((N_KV_HEADS, TM, KV_W), lambda i: (0, i, 0)),
            tile(KV_W), tile(KV_W),
        ],
        out_shape=[
            jax.ShapeDtypeStruct((T_ALL, 2 * FFT_W), bf16),
            jax.ShapeDtypeStruct((T_ALL, POOL_W), f32),
            jax.ShapeDtypeStruct((T_ALL, SGU_W), bf16),
            jax.ShapeDtypeStruct((T_ALL, ATTN_W), bf16),
            jax.ShapeDtypeStruct((N_KV_HEADS, T_ALL, KV_W), bf16),
            jax.ShapeDtypeStruct((N_KV_HEADS, T_ALL, KV_W), bf16),
            jax.ShapeDtypeStruct((T_ALL, KV_W), f32),
            jax.ShapeDtypeStruct((T_ALL, KV_W), f32),
        ],
        compiler_params=_cparams(("arbitrary",)),
        name="inproj",
    )(x, mod, w["w_in"], w["csc"], w["w_sgu"], w["b_sgu"], w["sgu_ln_g"], w["sgu_ln_b"],
      w["q_norm_g"], w["k_norm_g"], w["rope_cos"], w["rope_sin_a"], w["rope_sin_b"], w["ones_bd"])


def _pool_kernel(prev_ref, cur_ref, next_ref, wp_ref, scale_ref, o_ref):
    i = pl.program_id(0)
    n = jnp.where(i < T_CTX // POOL_TB, SEQ, DEC_SEQ)
    hl = POOL_HALO
    ext = jnp.concatenate([prev_ref[POOL_TB - hl:, :], cur_ref[...], next_ref[:hl, :]], axis=0)
    rows = POOL_TB + 2 * hl
    r = lax.broadcasted_iota(i32, (rows, 1), 0)
    pos = (i * POOL_TB + r - hl) & (n - 1)

    def back(a, s):
        return jnp.where(pos >= s, pltpu.roll(a, s, 0), 0.0)

    def fwd(a, s):
        return jnp.where(pos + s < n, pltpu.roll(a, rows - s, 0), 0.0)

    bsum = [back(ext, 1)]
    fsum = [ext]
    for k in range(3):
        s = 1 << k
        bsum.append(bsum[k] + back(bsum[k], s))
        fsum.append(fsum[k] + fwd(fsum[k], s))
    lane = lax.broadcasted_iota(i32, (1, POOL_W), 1)
    grp = lane // POOL_GROUP
    win = bsum[3] + fsum[3]
    half = jnp.full((1, POOL_W), POOL_WINDOWS[3] // 2, i32)
    for g in (2, 1, 0):
        win = jnp.where(grp == g, bsum[g] + fsum[g], win)
        half = jnp.where(grp == g, POOL_WINDOWS[g] // 2, half)
    cnt = (jnp.minimum(pos + half, n) - jnp.maximum(pos - half, 0)).astype(f32)
    y = (win / cnt - ext)[hl:hl + POOL_TB]
    o_ref[...] = (_dot(y.astype(bf16), wp_ref[...]) * scale_ref[...]).astype(bf16)


def _pool(p, l, w):
    nt = T_ALL // POOL_TB
    blk = lambda f: pl.BlockSpec((POOL_TB, POOL_W), lambda i: (f(i), 0))
    return pl.pallas_call(
        _pool_kernel,
        grid=(nt,),
        in_specs=[
            blk(lambda i: jnp.maximum(i - 1, 0)), blk(lambda i: i),
            blk(lambda i: jnp.minimum(i + 1, nt - 1)),
            pl.BlockSpec((None, POOL_W, POOL_W), lambda i: (l, 0, 0)),
            pl.BlockSpec((None, 1, POOL_W), lambda i: (l, 0, 0)),
        ],
        out_specs=blk(lambda i: i),
        out_shape=jax.ShapeDtypeStruct((T_ALL, POOL_W), bf16),
        compiler_params=_cparams(("arbitrary",)),
        name="pool",
    )(p, p, p, w["w_pool_bd"], w["pool_scale"])


def _seqdft_kernel(m_ref, pq_ref, w_ref, o_ref, *, n):
    f = _dot(m_ref[:, :n], pq_ref[:, :FFT_W]) + _dot(m_ref[:, n:], pq_ref[:, FFT_W:])
    o_ref[...] = _dot(f.astype(bf16), w_ref[...]).astype(bf16)


def _seqdft(pq, m, l, w, *, n, tr, nseq, row0):
    nr = n // tr
    b0 = row0 // n
    return pl.pallas_call(
        functools.partial(_seqdft_kernel, n=n),
        grid=(nr, nseq),
        in_specs=[
            pl.BlockSpec((tr, 2 * n), lambda i, b: (i, 0)),
            pl.BlockSpec((n, 2 * FFT_W), lambda i, b: (b0 + b, 0)),
            pl.BlockSpec((None, FFT_W, FFT_W), lambda i, b: (l, 0, 0)),
        ],
        out_specs=pl.BlockSpec((tr, FFT_W), lambda i, b: (b * nr + i, 0)),
        out_shape=jax.ShapeDtypeStruct((nseq * n, FFT_W), bf16),
        compiler_params=_cparams(("arbitrary", "arbitrary")),
        name="seqdft_%d" % n,
    )(m, pq, w["w_fft"])


def _attn_kernel(*refs, has_cache):
    if has_cache:
        q_ref, k_ref, v_ref, kc_ref, vc_ref, o_ref = refs
    else:
        q_ref, k_ref, v_ref, o_ref = refs
    q = q_ref[...]
    tq = q.shape[0]
    lane = lax.broadcasted_iota(i32, q.shape, 1)
    zero = jnp.zeros_like(q)
    qs = jnp.concatenate([jnp.where(lane < HEAD_DIM, q, zero),
                          jnp.where(lane >= HEAD_DIM, q, zero)], axis=0)
    nt = (((1,), (1,)), ((), ()))
    s = lax.dot_general(qs, k_ref[...], nt, preferred_element_type=f32)
    m = jnp.max(s, axis=-1, keepdims=True)
    if has_cache:
        sc = lax.dot_general(qs, kc_ref[...], nt, preferred_element_type=f32)
        m = jnp.maximum(m, jnp.max(sc, axis=-1, keepdims=True))
    p = jnp.exp(s - m)
    den = jnp.sum(p, axis=-1, keepdims=True)
    acc = _dot(p.astype(bf16), v_ref[...])
    if has_cache:
        pc = jnp.exp(sc - m)
        den = den + jnp.sum(pc, axis=-1, keepdims=True)
        acc = acc + _dot(pc.astype(bf16), vc_ref[...])
    out = acc / den
    o_ref[...] = jnp.where(lane < HEAD_DIM, out[:tq], out[tq:]).astype(bf16)


def _attention(q, kd, vd, cache, *, n, tq, nseq, row0):
    nq = n // tq
    b0 = row0 // n
    q0 = row0 // tq
    in_specs = [
        pl.BlockSpec((tq, 2 * HEAD_DIM), lambda b, h, i: (q0 + b * nq + i, h)),
        pl.BlockSpec((None, n, KV_W), lambda b, h, i: (h, b0 + b, 0)),
        pl.BlockSpec((None, n, KV_W), lambda b, h, i: (h, b0 + b, 0)),
    ]
    args = [q, kd, vd]
    if cache is not None:
        cspec = pl.BlockSpec((None, None, PAST_LEN, KV_W), lambda b, h, i: (h, b, 0, 0))
        in_specs += [cspec, cspec]
        args += list(cache)
    return pl.pallas_call(
        functools.partial(_attn_kernel, has_cache=cache is not None),
        grid=(nseq, N_KV_HEADS, nq),
        in_specs=in_specs,
        out_specs=pl.BlockSpec((tq, 2 * HEAD_DIM), lambda b, h, i: (b * nq + i, h)),
        out_shape=jax.ShapeDtypeStruct((nseq * n, ATTN_W), bf16),
        compiler_params=_cparams(("arbitrary", "arbitrary", "arbitrary")),
        name="attention_%d" % n,
    )(*args)


def _layer_norm(x, g, b):
    mu = jnp.mean(x, axis=-1, keepdims=True)
    xc = x - mu
    var = jnp.mean(xc * xc, axis=-1, keepdims=True)
    return xc * lax.rsqrt(var + LN_EPS) * g + b


def _outproj_kernel(x_ref, mod_ref, f_ref, p_ref, s_ref, a_ref, wout_ref, g_ref, b_ref,
                    wrh_ref, wrl_ref, br_ref, tril_ref, x1_ref, h2_ref, route_ref, cnt_ref, carry_ref):
    @pl.when(pl.program_id(0) == 0)
    def _():
        carry_ref[...] = jnp.zeros_like(carry_ref)

    mod = mod_ref[...]
    mix = (_dot(f_ref[...], wout_ref[0:256, :]) + _dot(p_ref[...], wout_ref[256:512, :])
           + _dot(s_ref[...], wout_ref[512:768, :]) + _dot(a_ref[...], wout_ref[768:1024, :]))
    x1 = _layer_norm(DEEPNORM_ALPHA * x_ref[...] + mod[2:3] * mix, g_ref[...], b_ref[...])
    x1_ref[...] = x1
    h2 = x1 * (1.0 + mod[4:5]) + mod[3:4]
    h2_ref[...] = h2

    h_hi, h_lo = _split_hi_lo(h2)
    logits = (_dot(h_hi, wrh_ref[...]) + _dot(h_hi, wrl_ref[...]) + _dot(h_lo, wrh_ref[...])
              + br_ref[...])
    lane = lax.broadcasted_iota(i32, logits.shape, 1).astype(f32)
    neg = jnp.float32(-jnp.inf)
    big = jnp.float32(1 << 20)
    gl = jnp.where(lane < N_GROUPS, logits, neg)
    gmax = jnp.max(gl, axis=-1, keepdims=True)
    gsel = jnp.min(jnp.where(gl == gmax, lane, big), axis=-1, keepdims=True)
    pg = 1.0 / jnp.sum(jnp.exp(gl - gmax), axis=-1, keepdims=True)
    e_lo = ROUTE_E0 + gsel * EXPERTS_PER_GROUP
    el = jnp.where((lane >= e_lo) & (lane < e_lo + EXPERTS_PER_GROUP), logits, neg)
    v1 = jnp.max(el, axis=-1, keepdims=True)
    i1 = jnp.min(jnp.where(el == v1, lane, big), axis=-1, keepdims=True)
    el2 = jnp.where(lane == i1, neg, el)
    v2 = jnp.max(el2, axis=-1, keepdims=True)
    i2 = jnp.min(jnp.where(el2 == v2, lane, big), axis=-1, keepdims=True)
    e2 = jnp.exp(v2 - v1)
    w1 = pg / (1.0 + e2)
    w2 = pg * e2 / (1.0 + e2)
    oh1 = lane == i1
    oh2 = lane == i2
    oh = jnp.where(oh1 | oh2, 1.0, 0.0)
    prefix = _dot(tril_ref[...], oh.astype(bf16)) + carry_ref[0:1, :]
    rank1 = jnp.sum(jnp.where(oh1, prefix, 0.0), axis=-1, keepdims=True)
    rank2 = jnp.sum(jnp.where(oh2, prefix, 0.0), axis=-1, keepdims=True)
    carry = carry_ref[...] + jnp.sum(oh, axis=0, keepdims=True)
    carry_ref[...] = carry
    cnt_ref[...] = carry
    cols = (i1 - ROUTE_E0, i2 - ROUTE_E0, w1, w2, rank1, rank2)
    route = jnp.zeros_like(logits)
    for j, col in enumerate(cols):
        route = jnp.where(lane == j, col, route)
    route_ref[...] = route


def _outproj(x, mod, fo, po, so, ao, l, w):
    nt = T_ALL // TM
    tile = lambda wd: pl.BlockSpec((TM, wd), lambda i: (i, 0))
    vec = lambda wd: pl.BlockSpec((None, 1, wd), lambda i: (l, 0, 0))
    return pl.pallas_call(
        _outproj_kernel,
        grid=(nt,),
        in_specs=[
            tile(D_MODEL),
            pl.BlockSpec((None, None, 6, D_MODEL), lambda i: (l, i // (SEG // TM), 0, 0)),
            tile(FFT_W), tile(POOL_W), tile(SGU_W), tile(ATTN_W),
            pl.BlockSpec((None, D_MODEL, D_MODEL), lambda i: (l, 0, 0)),
            vec(D_MODEL), vec(D_MODEL),
            pl.BlockSpec((None, D_MODEL, 128), lambda i: (l, 0, 0)),
            pl.BlockSpec((None, D_MODEL, 128), lambda i: (l, 0, 0)),
            vec(128),
            pl.BlockSpec((TM, TM), lambda i: (0, 0)),
        ],
        out_specs=[tile(D_MODEL), tile(D_MODEL), tile(128), pl.BlockSpec((8, 128), lambda i: (0, 0))],
        out_shape=[
            jax.ShapeDtypeStruct((T_ALL, D_MODEL), f32),
            jax.ShapeDtypeStruct((T_ALL, D_MODEL), f32),
            jax.ShapeDtypeStruct((T_ALL, 128), f32),
            jax.ShapeDtypeStruct((8, 128), f32),
        ],
        scratch_shapes=[pltpu.VMEM((8, 128), f32)],
        compiler_params=_cparams(("arbitrary",)),
        name="outproj",
    )(x, mod, fo, po, so, ao, w["w_out"], w["ln1_g"], w["ln1_b"], w["w_r_hi"], w["w_r_lo"], w["b_r"],
      w["tril"])


def _plan_kernel(route_ref, cnt_ref, pos_ref, meta_ref):
    lane = lax.broadcasted_iota(i32, (8, 128), 1)
    sub = lax.broadcasted_iota(i32, (8, 128), 0)
    cnt = cnt_ref[...]
    is_e = (lane >= ROUTE_E0) & (lane < ROUTE_E0 + N_EXPERTS)
    tiles = jnp.where(is_e, jnp.floor((cnt + (MOE_TM - 1.0)) * (1.0 / MOE_TM)), 0.0)
    cum = tiles
    for s in (1, 2, 4, 8, 16):
        cum = cum + jnp.where(lane >= s, pltpu.roll(cum, s, 1), 0.0)
    pstart = (cum - tiles) * MOE_TM
    nused = jnp.max(cum, axis=-1, keepdims=True)
    fill = jnp.where(is_e & (cnt != tiles * MOE_TM), pstart + (tiles - 1.0) * MOE_TM, -1.0)
    meta = jnp.where(sub == 0, cnt, jnp.where(sub == 1, nused, jnp.where(sub == 2, fill, 0.0)))
    meta_ref[...] = meta.astype(i32)

    r = route_ref[...]
    lane_t = lax.broadcasted_iota(i32, r.shape, 1).astype(f32)
    ps = pstart[0:1, :]

    def dest(ecol, rcol):
        hit = lane_t == (r[:, ecol:ecol + 1] + ROUTE_E0)
        return jnp.sum(jnp.where(hit, ps, 0.0), axis=-1, keepdims=True) + r[:, rcol:rcol + 1]

    pos = jnp.where(lane_t == 0, dest(0, 4), jnp.where(lane_t == 1, dest(1, 5), 0.0))
    pos_ref[...] = pos.astype(i32)


def _plan(route, cnt):
    return pl.pallas_call(
        _plan_kernel,
        grid=(T_ALL // PLAN_TB,),
        in_specs=[pl.BlockSpec((PLAN_TB, 128), lambda i: (i, 0)),
                  pl.BlockSpec((8, 128), lambda i: (0, 0))],
        out_specs=[pl.BlockSpec((PLAN_TB, 128), lambda i: (i, 0)),
                   pl.BlockSpec((8, 128), lambda i: (0, 0))],
        out_shape=[jax.ShapeDtypeStruct((T_ALL, 128), i32), jax.ShapeDtypeStruct((8, 128), i32)],
        compiler_params=_cparams(("arbitrary",)),
        name="plan",
    )(route, cnt)


def _dispatch_kernel(pos_ref, fill_ref, nused_ref, h_ref, xs_ref, zero_ref, sem, fill_sem):
    i = pl.program_id(0)

    def tile_fill(row0):
        return pltpu.make_async_copy(zero_ref, xs_ref.at[pl.ds(pl.multiple_of(row0, MOE_TM), MOE_TM)],
                                     fill_sem)

    @pl.when(i == 0)
    def _():
        zero_ref[...] = jnp.zeros_like(zero_ref)

        def start(e, c):
            @pl.when(fill_ref[e] >= 0)
            def _():
                tile_fill(jnp.maximum(fill_ref[e], 0)).start()
            return c

        def wait(e, c):
            @pl.when(fill_ref[e] >= 0)
            def _():
                tile_fill(jnp.maximum(fill_ref[e], 0)).wait()
            return c

        def start_tail(t, c):
            tile_fill(t * MOE_TM).start()
            return c

        def wait_tail(t, c):
            tile_fill(t * MOE_TM).wait()
            return c

        lax.fori_loop(0, N_EXPERTS, start, 0)
        lax.fori_loop(nused_ref[0], MOE_NT, start_tail, 0)
        lax.fori_loop(0, N_EXPERTS, wait, 0)
        lax.fori_loop(nused_ref[0], MOE_NT, wait_tail, 0)

    for k in range(2):
        def issue(j, c, k=k):
            row = pos_ref[k * T_ALL + i * DISP_TB + j]
            pltpu.make_async_copy(h_ref.at[pl.ds(j, 1)], xs_ref.at[pl.ds(row, 1)], sem).start()
            return c

        lax.fori_loop(0, DISP_TB, issue, 0, unroll=8)
    for k in range(2):
        pltpu.make_async_copy(h_ref, xs_ref.at[pl.ds(0, DISP_TB)], sem).wait()


def _dispatch(pos_flat, fill, nused, h2):
    grid_spec = pltpu.PrefetchScalarGridSpec(
        num_scalar_prefetch=3,
        grid=(T_ALL // DISP_TB,),
        in_specs=[pl.BlockSpec((DISP_TB, D_MODEL), lambda i, p, f, nu: (i, 0))],
        out_specs=pl.BlockSpec(memory_space=pl.ANY),
        scratch_shapes=[pltpu.VMEM((MOE_TM, D_MODEL), f32),
                        pltpu.SemaphoreType.DMA(()), pltpu.SemaphoreType.DMA(())],
    )
    return pl.pallas_call(
        _dispatch_kernel,
        grid_spec=grid_spec,
        out_shape=jax.ShapeDtypeStruct((MOE_NT * MOE_TM, D_MODEL), f32),
        compiler_params=_cparams(("arbitrary",)),
        name="dispatch",
    )(pos_flat, fill, nused, h2)


def _experts_kernel(cnt_ref, nused_ref, xs_ref, wg_hbm, wu_hbm, wd_hbm, ys_ref,
                    wg_f, wu_f, wd_f, wg_b, wu_b, wd_b, st, sem, *, layer):
    i = pl.program_id(0)
    NXT, NSLOT, LEFT = 0, 1, 2

    def w_copies(e, slot):
        return (pltpu.make_async_copy(wg_hbm.at[layer, e], wg_f.at[slot], sem.at[slot, 0]),
                pltpu.make_async_copy(wu_hbm.at[layer, e], wu_f.at[slot], sem.at[slot, 1]),
                pltpu.make_async_copy(wd_hbm.at[layer, e], wd_f.at[slot], sem.at[slot, 2]))

    def next_nonempty(e):
        return lax.while_loop(
            lambda v: (v < N_EXPERTS) & (cnt_ref[jnp.minimum(v, N_EXPERTS - 1)] == 0),
            lambda v: v + 1, e)

    @pl.when(i == 0)
    def _():
        e0 = next_nonempty(jnp.int32(0))
        for c in w_copies(e0, 0):
            c.start()
        st[NXT] = e0
        st[NSLOT] = 0
        st[LEFT] = 0

    @pl.when(i < nused_ref[0])
    def _():
        @pl.when(st[LEFT] == 0)
        def _():
            e = st[NXT]
            slot = st[NSLOT]
            for c in w_copies(e, slot):
                c.wait()
            e2 = next_nonempty(e + 1)

            @pl.when(e2 < N_EXPERTS)
            def _():
                for c in w_copies(e2, 1 - slot):
                    c.start()

            st[NXT] = e2
            st[NSLOT] = 1 - slot
            st[LEFT] = (cnt_ref[e] + (MOE_TM - 1)) // MOE_TM
            wg_b[...] = wg_f[slot].astype(bf16)
            wu_b[...] = wu_f[slot].astype(bf16)
            wd_b[...] = wd_f[slot].astype(bf16)

        x = xs_ref[...].astype(bf16)
        hg = _dot(x, wg_b[...])
        hu = _dot(x, wu_b[...])
        act = (hg * jax.nn.sigmoid(hg)) * hu
        ys_ref[...] = _dot(act.astype(bf16), wd_b[...])
        st[LEFT] = st[LEFT] - 1

    @pl.when(i >= nused_ref[0])
    def _():
        ys_ref[...] = jnp.zeros_like(ys_ref)


def _experts(counts, nused, xs, l, w_gate, w_up, w_down):
    hbm = pl.BlockSpec(memory_space=pl.ANY)
    grid_spec = pltpu.PrefetchScalarGridSpec(
        num_scalar_prefetch=2,
        grid=(MOE_NT,),
        in_specs=[pl.BlockSpec((MOE_TM, D_MODEL), lambda i, c, nu: (jnp.minimum(i, nu[0] - 1), 0)),
                  hbm, hbm, hbm],
        out_specs=pl.BlockSpec((MOE_TM, D_MODEL), lambda i, c, nu: (i, 0)),
        scratch_shapes=[
            pltpu.VMEM((2, D_MODEL, EXPERT_FF), f32),
            pltpu.VMEM((2, D_MODEL, EXPERT_FF), f32),
            pltpu.VMEM((2, EXPERT_FF, D_MODEL), f32),
            pltpu.VMEM((D_MODEL, EXPERT_FF), bf16),
            pltpu.VMEM((D_MODEL, EXPERT_FF), bf16),
            pltpu.VMEM((EXPERT_FF, D_MODEL), bf16),
            pltpu.SMEM((4,), i32),
            pltpu.SemaphoreType.DMA((2, 3)),
        ],
    )
    return pl.pallas_call(
        functools.partial(_experts_kernel, layer=l),
        grid_spec=grid_spec,
        out_shape=jax.ShapeDtypeStruct((MOE_NT * MOE_TM, D_MODEL), f32),
        compiler_params=_cparams(("arbitrary",)),
        name="experts",
    )(counts, nused, xs, w_gate, w_up, w_down)


def _combine_kernel(pos_ref, x1_ref, mod_ref, route_ref, ys_hbm, g_ref, b_ref, o_ref, ybuf, sem):
    i = pl.program_id(0)
    nt = pl.num_programs(0)

    def issue(tile, slot):
        for k in range(2):
            def body(j, c, k=k):
                row = pos_ref[k * T_ALL + tile * TM + j]
                pltpu.make_async_copy(ys_hbm.at[pl.ds(row, 1)], ybuf.at[slot, k, pl.ds(j, 1)],
                                      sem.at[slot]).start()
                return c

            lax.fori_loop(0, TM, body, 0, unroll=8)

    @pl.when(i == 0)
    def _():
        issue(0, 0)

    @pl.when(i + 1 < nt)
    def _():
        issue(i + 1, (i + 1) % 2)

    slot = i % 2
    for k in range(2):
        pltpu.make_async_copy(ys_hbm.at[pl.ds(0, TM)], ybuf.at[slot, k], sem.at[slot]).wait()
    route = route_ref[...]
    mod = mod_ref[...]
    moe = route[:, 2:3] * ybuf[slot, 0] + route[:, 3:4] * ybuf[slot, 1]
    o_ref[...] = _layer_norm(DEEPNORM_ALPHA * x1_ref[...] + mod[5:6] * moe, g_ref[...], b_ref[...])


def _combine(pos_flat, x1, mod, route, ys, l, w):
    nt = T_ALL // TM
    vec = pl.BlockSpec((None, 1, D_MODEL), lambda i, p: (l, 0, 0))
    grid_spec = pltpu.PrefetchScalarGridSpec(
        num_scalar_prefetch=1,
        grid=(nt,),
        in_specs=[
            pl.BlockSpec((TM, D_MODEL), lambda i, p: (i, 0)),
            pl.BlockSpec((None, None, 6, D_MODEL), lambda i, p: (l, i // (SEG // TM), 0, 0)),
            pl.BlockSpec((TM, 128), lambda i, p: (i, 0)),
            pl.BlockSpec(memory_space=pl.ANY),
            vec, vec,
        ],
        out_specs=pl.BlockSpec((TM, D_MODEL), lambda i, p: (i, 0)),
        scratch_shapes=[pltpu.VMEM((2, 2, TM, D_MODEL), f32), pltpu.SemaphoreType.DMA((2,))],
    )
    return pl.pallas_call(
        _combine_kernel,
        grid_spec=grid_spec,
        out_shape=jax.ShapeDtypeStruct((T_ALL, D_MODEL), f32),
        compiler_params=_cparams(("arbitrary",)),
        name="combine",
    )(pos_flat, x1, mod, route, ys, w["ln2_g"], w["ln2_b"])


def _dft_cos_sin(n, scale):
    k = jnp.arange(n, dtype=i32)
    ang = ((k[:, None] * k[None, :]) % n).astype(f32) * np.float32(2.0 * np.pi / n)
    return jnp.cos(ang) * scale, jnp.sin(ang) * scale


def _seq_dft_matrix(n):
    c, s = _dft_cos_sin(n, np.float32(n ** -0.5))
    return jnp.concatenate([c, -s], axis=1).astype(bf16)


def _rope_tables():
    rows = DEC_SEQ // GRID_W
    row = jnp.repeat(jnp.arange(rows), GRID_W).astype(f32)
    col = jnp.tile(jnp.arange(GRID_W), rows).astype(f32)
    n_freq = HEAD_DIM // 4
    inv = ROPE_THETA ** (-jnp.arange(n_freq, dtype=f32) / n_freq)
    ar = row[:, None] * inv
    ac = col[:, None] * inv
    ang = jnp.concatenate([ar, ar, ac, ac], axis=-1)
    cos = jnp.tile(jnp.cos(ang), (1, N_HEADS))
    sin = jnp.tile(jnp.sin(ang), (1, N_HEADS))
    first = (jnp.arange(ATTN_W) % (HEAD_DIM // 2)) < n_freq
    sin_a = jnp.where(first[None, :], -sin, 0.0)
    sin_b = jnp.where(first[None, :], 0.0, sin)
    ident = jnp.zeros((TM, ATTN_W), f32)
    return (jnp.concatenate([cos, ident + 1.0], axis=0),
            jnp.concatenate([sin_a, ident], axis=0),
            jnp.concatenate([sin_b, ident], axis=0))


def _dup_cache(cache):
    c = jnp.transpose(cache, (1, 3, 0, 2, 4))
    return jnp.concatenate([c, c], axis=-1).astype(bf16)


def kernel(x_prompt, x_sample, cache_k, cache_v, c, c_ctx, w_mod, b_mod, w_in, w_fft, w_pool, pool_scale, sgu_ln_g, sgu_ln_b, w_sgu, b_sgu, q_norm_g, k_norm_g, w_out, ln1_g, ln1_b, w_router_group, b_router_group, w_router_expert, b_router_expert, w_gate, w_up, w_down, ln2_g, ln2_b):
    L = DEPTH
    x = jnp.concatenate([x_prompt.reshape(T_CTX, D_MODEL), x_sample.reshape(T_LAT, D_MODEL)], axis=0)

    cond8 = jnp.concatenate([c_ctx[None, :], c, jnp.zeros((8 - 1 - DEC_BATCH, D_MODEL), f32)], axis=0)
    mod = _modulation(cond8, w_mod, b_mod)[:, :N_SEG].reshape(L, N_SEG, 6, D_MODEL)

    cc, sc = _dft_cos_sin(FFT_W, np.float32(FFT_W ** -0.5))
    rope_cos, rope_sin_a, rope_sin_b = _rope_tables()
    head_id = jnp.arange(ATTN_W) // HEAD_DIM
    eye_g = jnp.eye(len(POOL_WINDOWS), dtype=f32)
    w_r = jnp.zeros((L, D_MODEL, 128), f32)
    w_r = w_r.at[:, :, :N_GROUPS].set(w_router_group).at[:, :, ROUTE_E0:ROUTE_E0 + N_EXPERTS].set(w_router_expert)
    b_r = jnp.zeros((L, 1, 128), f32)
    b_r = b_r.at[:, 0, :N_GROUPS].set(b_router_group).at[:, 0, ROUTE_E0:ROUTE_E0 + N_EXPERTS].set(b_router_expert)
    w_r_hi, w_r_lo = _split_hi_lo(w_r)
    w = {
        "w_in": w_in.astype(bf16),
        "csc": jnp.concatenate([cc, sc], axis=1).astype(bf16),
        "w_sgu": jnp.transpose(w_sgu, (0, 2, 1, 3)).reshape(L, CHUNK, SGU_HEADS * CHUNK).astype(bf16),
        "b_sgu": jnp.repeat(jnp.transpose(b_sgu, (0, 2, 1)), SGU_W // SGU_HEADS, axis=2),
        "sgu_ln_g": sgu_ln_g.reshape(L, 1, SGU_W),
        "sgu_ln_b": sgu_ln_b.reshape(L, 1, SGU_W),
        "q_norm_g": jnp.tile(q_norm_g, (1, N_HEADS)).reshape(L, 1, ATTN_W),
        "k_norm_g": jnp.tile(k_norm_g, (1, N_KV_HEADS)).reshape(L, 1, KV_W),
        "rope_cos": rope_cos, "rope_sin_a": rope_sin_a, "rope_sin_b": rope_sin_b,
        "ones_bd": (head_id[:, None] == head_id[None, :]).astype(bf16),
        "w_pool_bd": jnp.einsum("lgcd,gh->lgchd", w_pool, eye_g).reshape(L, POOL_W, POOL_W).astype(bf16),
        "pool_scale": pool_scale.reshape(L, 1, POOL_W),
        "w_fft": w_fft.astype(bf16),
        "w_out": w_out.astype(bf16),
        "ln1_g": ln1_g.reshape(L, 1, D_MODEL), "ln1_b": ln1_b.reshape(L, 1, D_MODEL),
        "ln2_g": ln2_g.reshape(L, 1, D_MODEL), "ln2_b": ln2_b.reshape(L, 1, D_MODEL),
        "w_r_hi": w_r_hi, "w_r_lo": w_r_lo, "b_r": b_r,
        "tril": (jnp.arange(TM)[:, None] > jnp.arange(TM)[None, :]).astype(bf16),
    }
    m_ctx = _seq_dft_matrix(SEQ)
    m_lat = _seq_dft_matrix(DEC_SEQ)
    kc_all = _dup_cache(cache_k)
    vc_all = _dup_cache(cache_v)

    new_k, new_v = [], []
    for l in range(L):
        pq, praw, sgu, q, kd, vd, nk, nv = _inproj(x, mod, l, w)
        new_k.append(nk[:T_CTX].reshape(BATCH, SEQ, N_KV_HEADS, HEAD_DIM))
        new_v.append(nv[:T_CTX].reshape(BATCH, SEQ, N_KV_HEADS, HEAD_DIM))
        po = _pool(praw, l, w)
        fo = jnp.concatenate([
            _seqdft(pq, m_ctx, l, w, n=SEQ, tr=SEQ, nseq=BATCH, row0=0),
            _seqdft(pq, m_lat, l, w, n=DEC_SEQ, tr=FFT_TR, nseq=DEC_BATCH, row0=T_CTX)], axis=0)
        ao = jnp.concatenate([
            _attention(q, kd, vd, None, n=SEQ, tq=SEQ, nseq=BATCH, row0=0),
            _attention(q, kd, vd, (kc_all[l], vc_all[l]), n=DEC_SEQ, tq=ATT_TQ, nseq=DEC_BATCH,
                       row0=T_CTX)], axis=0)
        x1, h2, route, cnt = _outproj(x, mod, fo, po, sgu, ao, l, w)
        pos_slab, meta = _plan(route, cnt)
        pos_flat = pos_slab[:, :2].T.reshape(-1)
        counts = meta[0, ROUTE_E0:ROUTE_E0 + N_EXPERTS]
        nused = meta[1, :1]
        fill = meta[2, ROUTE_E0:ROUTE_E0 + N_EXPERTS]
        xs = _dispatch(pos_flat, fill, nused, h2)
        ys = _experts(counts, nused, xs, l, w_gate, w_up, w_down)
        x = _combine(pos_flat, x1, mod, route, ys, l, w)

    y_prompt = x[:T_CTX].reshape(BATCH, SEQ, D_MODEL)
    y_sample = x[T_CTX:].reshape(DEC_BATCH, DEC_SEQ, D_MODEL)
    return (y_prompt, y_sample, jnp.stack(new_k, axis=1), jnp.stack(new_v, axis=1))
```

```python
import functools

import numpy as np
import jax
import jax.numpy as jnp
from jax import lax
from jax.experimental import pallas as pl
from jax.experimental.pallas import tpu as pltpu

f32 = jnp.float32
bf16 = jnp.bfloat16
i32 = jnp.int32

D_MODEL = 1024
BATCH = 16
SEQ = 256
DEPTH = 4
DEC_BATCH = 2
DEC_SEQ = 4096
PAST_LEN = 512
GRID_W = 64
FFT_W = 256
POOL_W = 256
POOL_WINDOWS = (2, 4, 8, 16)
POOL_GROUP = 64
SGU_W = 256
SGU_HEADS = 4
CHUNK = 128
HEAD_DIM = 64
ATTN_W = 256
N_HEADS = 4
N_KV_HEADS = 2
KV_W = 128
IN_W = 1536
ROPE_THETA = 10000.0
N_GROUPS = 4
EXPERTS_PER_GROUP = 8
N_EXPERTS = 32
EXPERT_FF = 512
DEEPNORM_ALPHA = float((2 * DEPTH) ** 0.25)
LN_EPS = 1e-5
RMS_EPS = 1e-6

T_CTX = BATCH * SEQ
T_LAT = DEC_BATCH * DEC_SEQ
T_ALL = T_CTX + T_LAT
SEG = 4096
N_SEG = T_ALL // SEG

TM = 512
POOL_TB = 512
POOL_HALO = 8
FFT_TR = 512
ATT_TQ = 256
ATT_CHUNK = 512
DFT_SPLIT = 64
MOE_TM = 256
MOE_ROWS = 2 * T_ALL
MOE_NT = MOE_ROWS // MOE_TM + N_EXPERTS
PLAN_TB = 2048
DISP_TB = 512
ROUTE_E0 = 32
VMEM_LIMIT = 56 * 1024 * 1024


def _cparams(sem):
    return pltpu.CompilerParams(dimension_semantics=sem, vmem_limit_bytes=VMEM_LIMIT)


def _split_hi_lo(a):
    hi = a.astype(bf16)
    lo = (a - hi.astype(f32)).astype(bf16)
    return hi, lo


def _dot(a, b):
    return jnp.dot(a, b, preferred_element_type=f32)


def _mod_kernel(c_ref, w_ref, b_ref, o_ref):
    c = c_ref[...]
    s = c * jax.nn.sigmoid(c)
    s_hi, s_lo = _split_hi_lo(s)
    w_hi, w_lo = _split_hi_lo(w_ref[...])
    o_ref[...] = _dot(s_hi, w_hi) + _dot(s_hi, w_lo) + _dot(s_lo, w_hi) + b_ref[...]


def _modulation(cond8, w_mod, b_mod):
    tn = 1536
    return pl.pallas_call(
        _mod_kernel,
        grid=(DEPTH, 6 * D_MODEL // tn),
        in_specs=[
            pl.BlockSpec((8, D_MODEL), lambda l, j: (0, 0)),
            pl.BlockSpec((None, D_MODEL, tn), lambda l, j: (l, 0, j)),
            pl.BlockSpec((None, 1, tn), lambda l, j: (l, 0, j)),
        ],
        out_specs=pl.BlockSpec((None, 8, tn), lambda l, j: (l, 0, j)),
        out_shape=jax.ShapeDtypeStruct((DEPTH, 8, 6 * D_MODEL), f32),
        compiler_params=_cparams(("arbitrary", "arbitrary")),
        name="modulation",
    )(cond8, w_mod, b_mod.reshape(DEPTH, 1, 6 * D_MODEL))


def _head_rms(x, ones_bd, gain):
    sq = x * x
    hi, lo = _split_hi_lo(sq)
    ss = _dot(hi, ones_bd) + _dot(lo, ones_bd)
    return x * lax.rsqrt(ss * (1.0 / HEAD_DIM) + RMS_EPS) * gain


def _rope(x, cos, sin_a, sin_b):
    w = x.shape[-1]
    q4 = HEAD_DIM // 4
    return x * cos + pltpu.roll(x, w - q4, 1) * sin_a + pltpu.roll(x, q4, 1) * sin_b


def _dup_half(x, first):
    lane = lax.broadcasted_iota(i32, x.shape, 1)
    r = pltpu.roll(x, HEAD_DIM, 1)
    if first:
        return jnp.where(lane < HEAD_DIM, x, r)
    return jnp.where(lane >= HEAD_DIM, x, r)


def _gelu_tanh(x):
    c = np.sqrt(2.0 / np.pi).astype(np.float32)
    return x * (0.5 * (1.0 + jnp.tanh(c * (x + 0.044715 * (x * x * x)))))


def _inproj_kernel(x_ref, mod_ref, win_ref, csc_ref, wsgu_ref, bsgu_ref, lng_ref, lnb_ref,
                   qg_ref, kg_ref, cos_ref, sina_ref, sinb_ref, ones_ref,
                   pq_ref, pool_ref, sgu_ref, q_ref, kd_ref, vd_ref, nk_ref, nv_ref):
    x = x_ref[...]
    mod = mod_ref[...]
    h = (x * (1.0 + mod[1:2]) + mod[0:1]).astype(bf16)
    proj = _dot(h, win_ref[...])

    a = proj[:, 0:FFT_W].astype(bf16)
    pq_ref[...] = _dot(a, csc_ref[...]).astype(bf16)

    pool_ref[...] = proj[:, FFT_W:FFT_W + POOL_W]

    o = FFT_W + POOL_W
    hgu = _gelu_tanh(proj[:, o:o + 2 * SGU_W])
    u = hgu[:, :SGU_W]
    v = hgu[:, SGU_W:]
    mu = jnp.mean(v, axis=-1, keepdims=True)
    vc = v - mu
    var = jnp.mean(vc * vc, axis=-1, keepdims=True)
    v = vc * lax.rsqrt(var + LN_EPS) * lng_ref[...] + lnb_ref[...]
    lane = lax.broadcasted_iota(i32, (CHUNK, SGU_W), 1)
    head = lane // (SGU_W // SGU_HEADS)
    wcat = wsgu_ref[...]
    for cidx in range(TM // CHUNK):
        rows = slice(cidx * CHUNK, (cidx + 1) * CHUNK)
        vch = v[rows]
        vblk = jnp.concatenate(
            [jnp.where(head == g, vch, 0.0) for g in range(SGU_HEADS)], axis=0).astype(bf16)
        sp = _dot(wcat, vblk) + bsgu_ref[...]
        sgu_ref[rows, :] = (u[rows] * sp).astype(bf16)

    o = o + 2 * SGU_W
    ones_bd = ones_ref[...]
    cos = cos_ref[...]
    sin_a = sina_ref[...]
    sin_b = sinb_ref[...]
    q = _head_rms(proj[:, o:o + ATTN_W], ones_bd, qg_ref[...])
    q = _rope(q, cos, sin_a, sin_b) * np.float32(HEAD_DIM ** -0.5 * np.log2(np.e))
    q_ref[...] = q.astype(bf16)
    o = o + ATTN_W
    k = _head_rms(proj[:, o:o + KV_W], ones_bd[:KV_W, :KV_W], kg_ref[...])
    nk_ref[...] = k
    k = _rope(k, cos[:, :KV_W], sin_a[:, :KV_W], sin_b[:, :KV_W])
    kd_ref[0] = _dup_half(k, True).astype(bf16)
    kd_ref[1] = _dup_half(k, False).astype(bf16)
    o = o + KV_W
    vv = proj[:, o:o + KV_W]
    nv_ref[...] = vv
    vd_ref[0] = _dup_half(vv, True).astype(bf16)
    vd_ref[1] = _dup_half(vv, False).astype(bf16)


def _rope_block(i):
    nlat = DEC_SEQ // TM
    nctx = T_CTX // TM
    return jnp.where(i < nctx, nlat, (i - nctx) % nlat)


def _inproj(x, mod, l, w):
    nt = T_ALL // TM
    tile = lambda wd: pl.BlockSpec((TM, wd), lambda i: (i, 0))
    const = lambda shape: pl.BlockSpec(shape, lambda i: (0,) * len(shape))
    rope_spec = pl.BlockSpec((TM, ATTN_W), lambda i: (_rope_block(i), 0))
    return pl.pallas_call(
        _inproj_kernel,
        grid=(nt,),
        in_specs=[
            tile(D_MODEL),
            pl.BlockSpec((None, None, 6, D_MODEL), lambda i: (l, i // (SEG // TM), 0, 0)),
            pl.BlockSpec((None, D_MODEL, IN_W), lambda i: (l, 0, 0)),
            const((FFT_W, 2 * FFT_W)),
            pl.BlockSpec((None, CHUNK, SGU_HEADS * CHUNK), lambda i: (l, 0, 0)),
            pl.BlockSpec((None, CHUNK, SGU_W), lambda i: (l, 0, 0)),
            pl.BlockSpec((None, 1, SGU_W), lambda i: (l, 0, 0)),
            pl.BlockSpec((None, 1, SGU_W), lambda i: (l, 0, 0)),
            pl.BlockSpec((None, 1, ATTN_W), lambda i: (l, 0, 0)),
            pl.BlockSpec((None, 1, KV_W), lambda i: (l, 0, 0)),
            rope_spec, rope_spec, rope_spec,
            const((ATTN_W, ATTN_W)),
        ],
        out_specs=[
            tile(2 * FFT_W), tile(POOL_W), tile(SGU_W), tile(ATTN_W),
            pl.BlockSpec((N_KV_HEADS, TM, KV_W), lambda i: (0, i, 0)),
            pl.BlockSpec((N_KV_HEADS, TM, KV_W), lambda i: (0, i, 0)),
            tile(KV_W), tile(KV_W),
        ],
        out_shape=[
            jax.ShapeDtypeStruct((T_ALL, 2 * FFT_W), bf16),
            jax.ShapeDtypeStruct((T_ALL, POOL_W), f32),
            jax.ShapeDtypeStruct((T_ALL, SGU_W), bf16),
            jax.ShapeDtypeStruct((T_ALL, ATTN_W), bf16),
            jax.ShapeDtypeStruct((N_KV_HEADS, T_ALL, KV_W), bf16),
            jax.ShapeDtypeStruct((N_KV_HEADS, T_ALL, KV_W), bf16),
            jax.ShapeDtypeStruct((T_ALL, KV_W), f32),
            jax.ShapeDtypeStruct((T_ALL, KV_W), f32),
        ],
        compiler_params=_cparams(("arbitrary",)),
        name="inproj",
    )(x, mod, w["w_in"], w["csc"], w["w_sgu"], w["b_sgu"], w["sgu_ln_g"], w["sgu_ln_b"],
      w["q_norm_g"], w["k_norm_g"], w["rope_cos"], w["rope_sin_a"], w["rope_sin_b"], w["ones_bd"])


def _pool_kernel(prev_ref, cur_ref, next_ref, wp_ref, scale_ref, o_ref):
    i = pl.program_id(0)
    n = jnp.where(i < T_CTX // POOL_TB, SEQ, DEC_SEQ)
    hl = POOL_HALO
    ext = jnp.concatenate([prev_ref[POOL_TB - hl:, :], cur_ref[...], next_ref[:hl, :]], axis=0)
    rows = POOL_TB + 2 * hl
    r = lax.broadcasted_iota(i32, (rows, 1), 0)
    pos = (i * POOL_TB + r - hl) & (n - 1)

    def back(a, s):
        return jnp.where(pos >= s, pltpu.roll(a, s, 0), 0.0)

    def fwd(a, s):
        return jnp.where(pos + s < n, pltpu.roll(a, rows - s, 0), 0.0)

    bsum = [back(ext, 1)]
    fsum = [ext]
    for k in range(3):
        s = 1 << k
        bsum.append(bsum[k] + back(bsum[k], s))
        fsum.append(fsum[k] + fwd(fsum[k], s))
    lane = lax.broadcasted_iota(i32, (1, POOL_W), 1)
    grp = lane // POOL_GROUP
    win = bsum[3] + fsum[3]
    half = jnp.full((1, POOL_W), POOL_WINDOWS[3] // 2, i32)
    for g in (2, 1, 0):
        win = jnp.where(grp == g, bsum[g] + fsum[g], win)
        half = jnp.where(grp == g, POOL_WINDOWS[g] // 2, half)
    cnt = (jnp.minimum(pos + half, n) - jnp.maximum(pos - half, 0)).astype(f32)
    y = (win / cnt - ext)[hl:hl + POOL_TB]
    o_ref[...] = (_dot(y.astype(bf16), wp_ref[...]) * scale_ref[...]).astype(bf16)


def _pool(p, l, w):
    nt = T_ALL // POOL_TB
    blk = lambda f: pl.BlockSpec((POOL_TB, POOL_W), lambda i: (f(i), 0))
    return pl.pallas_call(
        _pool_kernel,
        grid=(nt,),
        in_specs=[
            blk(lambda i: jnp.maximum(i - 1, 0)), blk(lambda i: i),
            blk(lambda i: jnp.minimum(i + 1, nt - 1)),
            pl.BlockSpec((None, POOL_W, POOL_W), lambda i: (l, 0, 0)),
            pl.BlockSpec((None, 1, POOL_W), lambda i: (l, 0, 0)),
        ],
        out_specs=blk(lambda i: i),
        out_shape=jax.ShapeDtypeStruct((T_ALL, POOL_W), bf16),
        compiler_params=_cparams(("arbitrary",)),
        name="pool",
    )(p, p, p, w["w_pool_bd"], w["pool_scale"])


def _seqdft_kernel(m_ref, pq_ref, w_ref, o_ref, *, n):
    f = _dot(m_ref[:, :n], pq_ref[:, :FFT_W]) + _dot(m_ref[:, n:], pq_ref[:, FFT_W:])
    o_ref[...] = _dot(f.astype(bf16), w_ref[...]).astype(bf16)


def _seqdft(pq, m, l, w, *, n, tr, nseq, row0):
    nr = n // tr
    b0 = row0 // n
    return pl.pallas_call(
        functools.partial(_seqdft_kernel, n=n),
        grid=(nr, nseq),
        in_specs=[
            pl.BlockSpec((tr, 2 * n), lambda i, b: (i, 0)),
            pl.BlockSpec((n, 2 * FFT_W), lambda i, b: (b0 + b, 0)),
            pl.BlockSpec((None, FFT_W, FFT_W), lambda i, b: (l, 0, 0)),
        ],
        out_specs=pl.BlockSpec((tr, FFT_W), lambda i, b: (b * nr + i, 0)),
        out_shape=jax.ShapeDtypeStruct((nseq * n, FFT_W), bf16),
        compiler_params=_cparams(("arbitrary", "arbitrary")),
        name="seqdft_%d" % n,
    )(m, pq, w["w_fft"])


def _attn_kernel(*refs, has_cache):
    if has_cache:
        q_ref, k_ref, v_ref, kc_ref, vc_ref, o_ref = refs
    else:
        q_ref, k_ref, v_ref, o_ref = refs
    q = q_ref[...]
    tq = q.shape[0]
    lane = lax.broadcasted_iota(i32, q.shape, 1)
    zero = jnp.zeros_like(q)
    qs = jnp.concatenate([jnp.where(lane < HEAD_DIM, q, zero),
                          jnp.where(lane >= HEAD_DIM, q, zero)], axis=0)
    nt = (((1,), (1,)), ((), ()))
    n = k_ref.shape[0]
    chunk = min(n, ATT_CHUNK)
    parts = [(k_ref, v_ref, c * chunk, chunk) for c in range(n // chunk)]
    if has_cache:
        parts = [(kc_ref, vc_ref, 0, PAST_LEN)] + parts
    m = jnp.full((2 * tq, 1), -jnp.inf, f32)
    den = jnp.zeros((2 * tq, 1), f32)
    acc = jnp.zeros((2 * tq, 2 * HEAD_DIM), f32)
    for kr, vr, off, size in parts:
        s = lax.dot_general(qs, kr[off:off + size, :], nt, preferred_element_type=f32)
        m_new = jnp.maximum(m, jnp.max(s, axis=-1, keepdims=True))
        alpha = jnp.exp2(m - m_new)
        p = jnp.exp2(s - m_new).astype(bf16)
        den = alpha * den + jnp.sum(p.astype(f32), axis=-1, keepdims=True)
        acc = alpha * acc + _dot(p, vr[off:off + size, :])
        m = m_new
    out = acc / den
    o_ref[...] = jnp.where(lane < HEAD_DIM, out[:tq], out[tq:]).astype(bf16)


def _attention(q, kd, vd, cache, *, n, tq, nseq, row0):
    nq = n // tq
    b0 = row0 // n
    q0 = row0 // tq
    in_specs = [
        pl.BlockSpec((tq, 2 * HEAD_DIM), lambda b, h, i: (q0 + b * nq + i, h)),
        pl.BlockSpec((None, n, KV_W), lambda b, h, i: (h, b0 + b, 0)),
        pl.BlockSpec((None, n, KV_W), lambda b, h, i: (h, b0 + b, 0)),
    ]
    args = [q, kd, vd]
    if cache is not None:
        cspec = pl.BlockSpec((None, None, PAST_LEN, KV_W), lambda b, h, i: (h, b, 0, 0))
        in_specs += [cspec, cspec]
        args += list(cache)
    return pl.pallas_call(
        functools.partial(_attn_kernel, has_cache=cache is not None),
        grid=(nseq, N_KV_HEADS, nq),
        in_specs=in_specs,
        out_specs=pl.BlockSpec((tq, 2 * HEAD_DIM), lambda b, h, i: (b * nq + i, h)),
        out_shape=jax.ShapeDtypeStruct((nseq * n, ATTN_W), bf16),
        compiler_params=_cparams(("arbitrary", "arbitrary", "arbitrary")),
        name="attention_%d" % n,
    )(*args)


def _layer_norm(x, g, b):
    mu = jnp.mean(x, axis=-1, keepdims=True)
    xc = x - mu
    var = jnp.mean(xc * xc, axis=-1, keepdims=True)
    return xc * lax.rsqrt(var + LN_EPS) * g + b


def _outproj_kernel(x_ref, mod_ref, fc_ref, fl_ref, p_ref, s_ref, ac_ref, al_ref, wout_ref, g_ref, b_ref,
                    wrh_ref, wrl_ref, br_ref, tril_ref, x1_ref, h2_ref, route_ref, cnt_ref, carry_ref):
    i = pl.program_id(0)

    @pl.when(i == 0)
    def _():
        carry_ref[...] = jnp.zeros_like(carry_ref)

    mod = mod_ref[...]
    is_ctx = i < T_CTX // TM
    f_mix = jnp.where(is_ctx, fc_ref[...], fl_ref[...])
    a_mix = jnp.where(is_ctx, ac_ref[...], al_ref[...])
    mix = (_dot(f_mix, wout_ref[0:256, :]) + _dot(p_ref[...], wout_ref[256:512, :])
           + _dot(s_ref[...], wout_ref[512:768, :]) + _dot(a_mix, wout_ref[768:1024, :]))
    x1 = _layer_norm(DEEPNORM_ALPHA * x_ref[...] + mod[2:3] * mix, g_ref[...], b_ref[...])
    x1_ref[...] = x1
    h2 = x1 * (1.0 + mod[4:5]) + mod[3:4]
    h2_ref[...] = h2

    h_hi, h_lo = _split_hi_lo(h2)
    logits = (_dot(h_hi, wrh_ref[...]) + _dot(h_hi, wrl_ref[...]) + _dot(h_lo, wrh_ref[...])
              + br_ref[...])
    lane = lax.broadcasted_iota(i32, logits.shape, 1).astype(f32)
    neg = jnp.float32(-jnp.inf)
    big = jnp.float32(1 << 20)
    gl = jnp.where(lane < N_GROUPS, logits, neg)
    gmax = jnp.max(gl, axis=-1, keepdims=True)
    gsel = jnp.min(jnp.where(gl == gmax, lane, big), axis=-1, keepdims=True)
    pg = 1.0 / jnp.sum(jnp.exp(gl - gmax), axis=-1, keepdims=True)
    e_lo = ROUTE_E0 + gsel * EXPERTS_PER_GROUP
    el = jnp.where((lane >= e_lo) & (lane < e_lo + EXPERTS_PER_GROUP), logits, neg)
    v1 = jnp.max(el, axis=-1, keepdims=True)
    i1 = jnp.min(jnp.where(el == v1, lane, big), axis=-1, keepdims=True)
    el2 = jnp.where(lane == i1, neg, el)
    v2 = jnp.max(el2, axis=-1, keepdims=True)
    i2 = jnp.min(jnp.where(el2 == v2, lane, big), axis=-1, keepdims=True)
    e2 = jnp.exp(v2 - v1)
    w1 = pg / (1.0 + e2)
    w2 = pg * e2 / (1.0 + e2)
    oh1 = lane == i1
    oh2 = lane == i2
    oh = jnp.where(oh1 | oh2, 1.0, 0.0)
    prefix = _dot(tril_ref[...], oh.astype(bf16)) + carry_ref[0:1, :]
    rank1 = jnp.sum(jnp.where(oh1, prefix, 0.0), axis=-1, keepdims=True)
    rank2 = jnp.sum(jnp.where(oh2, prefix, 0.0), axis=-1, keepdims=True)
    carry = carry_ref[...] + jnp.sum(oh, axis=0, keepdims=True)
    carry_ref[...] = carry
    cnt_ref[...] = carry
    cols = (i1 - ROUTE_E0, i2 - ROUTE_E0, w1, w2, rank1, rank2)
    route = jnp.zeros_like(logits)
    for j, col in enumerate(cols):
        route = jnp.where(lane == j, col, route)
    route_ref[...] = route


def _outproj(x, mod, fo_ctx, fo_lat, po, so, ao_ctx, ao_lat, l, w):
    nt = T_ALL // TM
    nctx = T_CTX // TM
    tile = lambda wd: pl.BlockSpec((TM, wd), lambda i: (i, 0))
    ctx_tile = lambda wd: pl.BlockSpec((TM, wd), lambda i: (jnp.minimum(i, nctx - 1), 0))
    lat_tile = lambda wd: pl.BlockSpec((TM, wd), lambda i: (jnp.maximum(i - nctx, 0), 0))
    vec = lambda wd: pl.BlockSpec((None, 1, wd), lambda i: (l, 0, 0))
    return pl.pallas_call(
        _outproj_kernel,
        grid=(nt,),
        in_specs=[
            tile(D_MODEL),
            pl.BlockSpec((None, None, 6, D_MODEL), lambda i: (l, i // (SEG // TM), 0, 0)),
            ctx_tile(FFT_W), lat_tile(FFT_W), tile(POOL_W), tile(SGU_W),
            ctx_tile(ATTN_W), lat_tile(ATTN_W),
            pl.BlockSpec((None, D_MODEL, D_MODEL), lambda i: (l, 0, 0)),
            vec(D_MODEL), vec(D_MODEL),
            pl.BlockSpec((None, D_MODEL, 128), lambda i: (l, 0, 0)),
            pl.BlockSpec((None, D_MODEL, 128), lambda i: (l, 0, 0)),
            vec(128),
            pl.BlockSpec((TM, TM), lambda i: (0, 0)),
        ],
        out_specs=[tile(D_MODEL), tile(D_MODEL), tile(128), pl.BlockSpec((8, 128), lambda i: (0, 0))],
        out_shape=[
            jax.ShapeDtypeStruct((T_ALL, D_MODEL), f32),
            jax.ShapeDtypeStruct((T_ALL, D_MODEL), f32),
            jax.ShapeDtypeStruct((T_ALL, 128), f32),
            jax.ShapeDtypeStruct((8, 128), f32),
        ],
        scratch_shapes=[pltpu.VMEM((8, 128), f32)],
        compiler_params=_cparams(("arbitrary",)),
        name="outproj",
    )(x, mod, fo_ctx, fo_lat, po, so, ao_ctx, ao_lat,
      w["w_out"], w["ln1_g"], w["ln1_b"], w["w_r_hi"], w["w_r_lo"], w["b_r"],
      w["tril"])


def _plan_kernel(route_ref, cnt_ref, pos_ref, meta_ref):
    lane = lax.broadcasted_iota(i32, (8, 128), 1)
    sub = lax.broadcasted_iota(i32, (8, 128), 0)
    cnt = cnt_ref[...]
    is_e = (lane >= ROUTE_E0) & (lane < ROUTE_E0 + N_EXPERTS)
    tiles = jnp.where(is_e, jnp.floor((cnt + (MOE_TM - 1.0)) * (1.0 / MOE_TM)), 0.0)
    cum = tiles
    for s in (1, 2, 4, 8, 16):
        cum = cum + jnp.where(lane >= s, pltpu.roll(cum, s, 1), 0.0)
    pstart = (cum - tiles) * MOE_TM
    nused = jnp.max(cum, axis=-1, keepdims=True)
    fill = jnp.where(is_e & (cnt != tiles * MOE_TM), pstart + (tiles - 1.0) * MOE_TM, -1.0)
    meta = jnp.where(sub == 0, cnt, jnp.where(sub == 1, nused, jnp.where(sub == 2, fill, 0.0)))
    meta_ref[...] = meta.astype(i32)

    r = route_ref[...]
    lane_t = lax.broadcasted_iota(i32, r.shape, 1).astype(f32)
    ps = pstart[0:1, :]

    def dest(ecol, rcol):
        hit = lane_t == (r[:, ecol:ecol + 1] + ROUTE_E0)
        return jnp.sum(jnp.where(hit, ps, 0.0), axis=-1, keepdims=True) + r[:, rcol:rcol + 1]

    pos = jnp.where(lane_t == 0, dest(0, 4), jnp.where(lane_t == 1, dest(1, 5), 0.0))
    pos_ref[...] = pos.astype(i32)


def _plan(route, cnt):
    return pl.pallas_call(
        _plan_kernel,
        grid=(T_ALL // PLAN_TB,),
        in_specs=[pl.BlockSpec((PLAN_TB, 128), lambda i: (i, 0)),
                  pl.BlockSpec((8, 128), lambda i: (0, 0))],
        out_specs=[pl.BlockSpec((PLAN_TB, 128), lambda i: (i, 0)),
                   pl.BlockSpec((8, 128), lambda i: (0, 0))],
        out_shape=[jax.ShapeDtypeStruct((T_ALL, 128), i32), jax.ShapeDtypeStruct((8, 128), i32)],
        compiler_params=_cparams(("arbitrary",)),
        name="plan",
    )(route, cnt)


def _dispatch_kernel(pos_ref, fill_ref, nused_ref, h_ref, xs_ref, zero_ref, sem, fill_sem):
    i = pl.program_id(0)

    def tile_fill(row0):
        return pltpu.make_async_copy(zero_ref, xs_ref.at[pl.ds(pl.multiple_of(row0, MOE_TM), MOE_TM)],
                                     fill_sem)

    @pl.when(i == 0)
    def _():
        zero_ref[...] = jnp.zeros_like(zero_ref)

        def start(e, c):
            @pl.when(fill_ref[e] >= 0)
            def _():
                tile_fill(jnp.maximum(fill_ref[e], 0)).start()
            return c

        def wait(e, c):
            @pl.when(fill_ref[e] >= 0)
            def _():
                tile_fill(jnp.maximum(fill_ref[e], 0)).wait()
            return c

        def start_tail(t, c):
            tile_fill(t * MOE_TM).start()
            return c

        def wait_tail(t, c):
            tile_fill(t * MOE_TM).wait()
            return c

        lax.fori_loop(0, N_EXPERTS, start, 0)
        lax.fori_loop(nused_ref[0], MOE_NT, start_tail, 0)
        lax.fori_loop(0, N_EXPERTS, wait, 0)
        lax.fori_loop(nused_ref[0], MOE_NT, wait_tail, 0)

    for k in range(2):
        base = k * T_ALL + i * DISP_TB
        for j in range(DISP_TB):
            row = pos_ref[base + j]
            pltpu.make_async_copy(h_ref.at[pl.ds(j, 1)], xs_ref.at[pl.ds(row, 1)], sem).start()
    for k in range(2):
        pltpu.make_async_copy(h_ref, xs_ref.at[pl.ds(0, DISP_TB)], sem).wait()


def _dispatch(pos_flat, fill, nused, h2):
    grid_spec = pltpu.PrefetchScalarGridSpec(
        num_scalar_prefetch=3,
        grid=(T_ALL // DISP_TB,),
        in_specs=[pl.BlockSpec((DISP_TB, D_MODEL), lambda i, p, f, nu: (i, 0))],
        out_specs=pl.BlockSpec(memory_space=pl.ANY),
        scratch_shapes=[pltpu.VMEM((MOE_TM, D_MODEL), f32),
                        pltpu.SemaphoreType.DMA(()), pltpu.SemaphoreType.DMA(())],
    )
    return pl.pallas_call(
        _dispatch_kernel,
        grid_spec=grid_spec,
        out_shape=jax.ShapeDtypeStruct((MOE_NT * MOE_TM, D_MODEL), f32),
        compiler_params=_cparams(("arbitrary",)),
        name="dispatch",
    )(pos_flat, fill, nused, h2)


def _experts_kernel(cnt_ref, nused_ref, xs_ref, wg_hbm, wu_hbm, wd_hbm, ys_ref,
                    wg_f, wu_f, wd_f, wg_b, wu_b, wd_b, st, sem, *, layer):
    i = pl.program_id(0)
    NXT, NSLOT, LEFT = 0, 1, 2

    def w_copies(e, slot):
        return (pltpu.make_async_copy(wg_hbm.at[layer, e], wg_f.at[slot], sem.at[slot, 0]),
                pltpu.make_async_copy(wu_hbm.at[layer, e], wu_f.at[slot], sem.at[slot, 1]),
                pltpu.make_async_copy(wd_hbm.at[layer, e], wd_f.at[slot], sem.at[slot, 2]))

    def next_nonempty(e):
        return lax.while_loop(
            lambda v: (v < N_EXPERTS) & (cnt_ref[jnp.minimum(v, N_EXPERTS - 1)] == 0),
            lambda v: v + 1, e)

    @pl.when(i == 0)
    def _():
        e0 = next_nonempty(jnp.int32(0))
        for c in w_copies(e0, 0):
            c.start(priority=1)
        st[NXT] = e0
        st[NSLOT] = 0
        st[LEFT] = 0

    @pl.when(i < nused_ref[0])
    def _():
        @pl.when(st[LEFT] == 0)
        def _():
            e = st[NXT]
            slot = st[NSLOT]
            for c in w_copies(e, slot):
                c.wait()
            e2 = next_nonempty(e + 1)

            @pl.when(e2 < N_EXPERTS)
            def _():
                for c in w_copies(e2, 1 - slot):
                    c.start(priority=1)

            st[NXT] = e2
            st[NSLOT] = 1 - slot
            st[LEFT] = (cnt_ref[e] + (MOE_TM - 1)) // MOE_TM
            wg_b[...] = wg_f[slot].astype(bf16)
            wu_b[...] = wu_f[slot].astype(bf16)
            wd_b[...] = wd_f[slot].astype(bf16)

        x = xs_ref[...].astype(bf16)
        hg = _dot(x, wg_b[...])
        hu = _dot(x, wu_b[...])
        act = (hg * jax.nn.sigmoid(hg)) * hu
        ys_ref[...] = _dot(act.astype(bf16), wd_b[...])
        st[LEFT] = st[LEFT] - 1

    @pl.when(i >= nused_ref[0])
    def _():
        ys_ref[...] = jnp.zeros_like(ys_ref)


def _experts(counts, nused, xs, l, w_gate, w_up, w_down):
    hbm = pl.BlockSpec(memory_space=pl.ANY)
    grid_spec = pltpu.PrefetchScalarGridSpec(
        num_scalar_prefetch=2,
        grid=(MOE_NT,),
        in_specs=[pl.BlockSpec((MOE_TM, D_MODEL), lambda i, c, nu: (jnp.minimum(i, nu[0] - 1), 0)),
                  hbm, hbm, hbm],
        out_specs=pl.BlockSpec((MOE_TM, D_MODEL), lambda i, c, nu: (i, 0)),
        scratch_shapes=[
            pltpu.VMEM((2, D_MODEL, EXPERT_FF), f32),
            pltpu.VMEM((2, D_MODEL, EXPERT_FF), f32),
            pltpu.VMEM((2, EXPERT_FF, D_MODEL), f32),
            pltpu.VMEM((D_MODEL, EXPERT_FF), bf16),
            pltpu.VMEM((D_MODEL, EXPERT_FF), bf16),
            pltpu.VMEM((EXPERT_FF, D_MODEL), bf16),
            pltpu.SMEM((4,), i32),
            pltpu.SemaphoreType.DMA((2, 3)),
        ],
    )
    return pl.pallas_call(
        functools.partial(_experts_kernel, layer=l),
        grid_spec=grid_spec,
        out_shape=jax.ShapeDtypeStruct((MOE_NT * MOE_TM, D_MODEL), f32),
        compiler_params=_cparams(("arbitrary",)),
        name="experts",
    )(counts, nused, xs, w_gate, w_up, w_down)


def _combine_kernel(pos_ref, x1_ref, mod_ref, route_ref, ys_hbm, g_ref, b_ref, o_ref, ybuf, sem):
    i = pl.program_id(0)
    nt = pl.num_programs(0) - 1

    @pl.when(i < nt)
    def _():
        slot = i % 2
        for k in range(2):
            base = k * T_ALL + i * TM
            for j in range(TM):
                row = pos_ref[base + j]
                pltpu.make_async_copy(ys_hbm.at[pl.ds(row, 1)], ybuf.at[slot, k, pl.ds(j, 1)],
                                      sem.at[slot]).start()

    @pl.when(i >= 1)
    def _():
        slot = (i - 1) % 2
        for k in range(2):
            pltpu.make_async_copy(ys_hbm.at[pl.ds(0, TM)], ybuf.at[slot, k], sem.at[slot]).wait()
        route = route_ref[...]
        mod = mod_ref[...]
        moe = route[:, 2:3] * ybuf[slot, 0] + route[:, 3:4] * ybuf[slot, 1]
        o_ref[...] = _layer_norm(DEEPNORM_ALPHA * x1_ref[...] + mod[5:6] * moe, g_ref[...], b_ref[...])


def _combine(pos_flat, x1, mod, route, ys, l, w):
    nt = T_ALL // TM
    vec = pl.BlockSpec((None, 1, D_MODEL), lambda i, p: (l, 0, 0))
    prev = lambda i: jnp.maximum(i - 1, 0)
    grid_spec = pltpu.PrefetchScalarGridSpec(
        num_scalar_prefetch=1,
        grid=(nt + 1,),
        in_specs=[
            pl.BlockSpec((TM, D_MODEL), lambda i, p: (prev(i), 0)),
            pl.BlockSpec((None, None, 6, D_MODEL), lambda i, p: (l, prev(i) // (SEG // TM), 0, 0)),
            pl.BlockSpec((TM, 128), lambda i, p: (prev(i), 0)),
            pl.BlockSpec(memory_space=pl.ANY),
            vec, vec,
        ],
        out_specs=pl.BlockSpec((TM, D_MODEL), lambda i, p: (prev(i), 0)),
        scratch_shapes=[pltpu.VMEM((2, 2, TM, D_MODEL), f32), pltpu.SemaphoreType.DMA((2,))],
    )
    return pl.pallas_call(
        _combine_kernel,
        grid_spec=grid_spec,
        out_shape=jax.ShapeDtypeStruct((T_ALL, D_MODEL), f32),
        compiler_params=_cparams(("arbitrary",)),
        name="combine",
    )(pos_flat, x1, mod, route, ys, w["ln2_g"], w["ln2_b"])


def _dft_cos_sin(n, scale):
    k = jnp.arange(n, dtype=i32)
    ang = ((k[:, None] * k[None, :]) % n).astype(f32) * np.float32(2.0 * np.pi / n)
    return jnp.cos(ang) * scale, jnp.sin(ang) * scale


def _seq_dft_matrix(n):
    g = min(DFT_SPLIT, n)
    j = jnp.arange(n, dtype=i32)[None, :]
    k1 = jnp.arange(n // g, dtype=i32)[:, None]
    k2 = jnp.arange(g, dtype=i32)[:, None]
    ang_a = ((k1 * j) % (n // g)).astype(f32) * np.float32(2.0 * np.pi * g / n)
    ang_b = ((k2 * j) % n).astype(f32) * np.float32(2.0 * np.pi / n)
    scale = np.float32(n ** -0.5)
    ca, sa = jnp.cos(ang_a), jnp.sin(ang_a)
    cb, sb = jnp.cos(ang_b) * scale, jnp.sin(ang_b) * scale
    ca2 = jnp.concatenate([ca, ca], axis=1)[:, None, :]
    sa2 = jnp.concatenate([sa, sa], axis=1)[:, None, :]
    cb2 = jnp.concatenate([cb, -sb], axis=1)[None, :, :]
    sb2 = jnp.concatenate([sb, cb], axis=1)[None, :, :]
    return (ca2 * cb2 - sa2 * sb2).astype(bf16).reshape(n, 2 * n)


def _rope_tables():
    rows = DEC_SEQ // GRID_W
    row = jnp.repeat(jnp.arange(rows), GRID_W).astype(f32)
    col = jnp.tile(jnp.arange(GRID_W), rows).astype(f32)
    n_freq = HEAD_DIM // 4
    inv = ROPE_THETA ** (-jnp.arange(n_freq, dtype=f32) / n_freq)
    ar = row[:, None] * inv
    ac = col[:, None] * inv
    ang = jnp.concatenate([ar, ar, ac, ac], axis=-1)
    cos = jnp.tile(jnp.cos(ang), (1, N_HEADS))
    sin = jnp.tile(jnp.sin(ang), (1, N_HEADS))
    first = (jnp.arange(ATTN_W) % (HEAD_DIM // 2)) < n_freq
    sin_a = jnp.where(first[None, :], -sin, 0.0)
    sin_b = jnp.where(first[None, :], 0.0, sin)
    ident = jnp.zeros((TM, ATTN_W), f32)
    return (jnp.concatenate([cos, ident + 1.0], axis=0),
            jnp.concatenate([sin_a, ident], axis=0),
            jnp.concatenate([sin_b, ident], axis=0))


def _dup_cache(cache):
    c = jnp.transpose(cache, (1, 3, 0, 2, 4))
    return jnp.concatenate([c, c], axis=-1).astype(bf16)


def kernel(x_prompt, x_sample, cache_k, cache_v, c, c_ctx, w_mod, b_mod, w_in, w_fft, w_pool, pool_scale, sgu_ln_g, sgu_ln_b, w_sgu, b_sgu, q_norm_g, k_norm_g, w_out, ln1_g, ln1_b, w_router_group, b_router_group, w_router_expert, b_router_expert, w_gate, w_up, w_down, ln2_g, ln2_b):
    L = DEPTH
    x = jnp.concatenate([x_prompt.reshape(T_CTX, D_MODEL), x_sample.reshape(T_LAT, D_MODEL)], axis=0)

    cond8 = jnp.concatenate([c_ctx[None, :], c, jnp.zeros((8 - 1 - DEC_BATCH, D_MODEL), f32)], axis=0)
    mod = _modulation(cond8, w_mod, b_mod)[:, :N_SEG].reshape(L, N_SEG, 6, D_MODEL)

    cc, sc = _dft_cos_sin(FFT_W, np.float32(FFT_W ** -0.5))
    rope_cos, rope_sin_a, rope_sin_b = _rope_tables()
    head_id = jnp.arange(ATTN_W) // HEAD_DIM
    eye_g = jnp.eye(len(POOL_WINDOWS), dtype=f32)
    w_r = jnp.zeros((L, D_MODEL, 128), f32)
    w_r = w_r.at[:, :, :N_GROUPS].set(w_router_group).at[:, :, ROUTE_E0:ROUTE_E0 + N_EXPERTS].set(w_router_expert)
    b_r = jnp.zeros((L, 1, 128), f32)
    b_r = b_r.at[:, 0, :N_GROUPS].set(b_router_group).at[:, 0, ROUTE_E0:ROUTE_E0 + N_EXPERTS].set(b_router_expert)
    w_r_hi, w_r_lo = _split_hi_lo(w_r)
    w = {
        "w_in": w_in.astype(bf16),
        "csc": jnp.concatenate([cc, sc], axis=1).astype(bf16),
        "w_sgu": jnp.transpose(w_sgu, (0, 2, 1, 3)).reshape(L, CHUNK, SGU_HEADS * CHUNK).astype(bf16),
        "b_sgu": jnp.repeat(jnp.transpose(b_sgu, (0, 2, 1)), SGU_W // SGU_HEADS, axis=2),
        "sgu_ln_g": sgu_ln_g.reshape(L, 1, SGU_W),
        "sgu_ln_b": sgu_ln_b.reshape(L, 1, SGU_W),
        "q_norm_g": jnp.tile(q_norm_g, (1, N_HEADS)).reshape(L, 1, ATTN_W),
        "k_norm_g": jnp.tile(k_norm_g, (1, N_KV_HEADS)).reshape(L, 1, KV_W),
        "rope_cos": rope_cos, "rope_sin_a": rope_sin_a, "rope_sin_b": rope_sin_b,
        "ones_bd": (head_id[:, None] == head_id[None, :]).astype(bf16),
        "w_pool_bd": jnp.einsum("lgcd,gh->lgchd", w_pool, eye_g).reshape(L, POOL_W, POOL_W).astype(bf16),
        "pool_scale": pool_scale.reshape(L, 1, POOL_W),
        "w_fft": w_fft.astype(bf16),
        "w_out": w_out.astype(bf16),
        "ln1_g": ln1_g.reshape(L, 1, D_MODEL), "ln1_b": ln1_b.reshape(L, 1, D_MODEL),
        "ln2_g": ln2_g.reshape(L, 1, D_MODEL), "ln2_b": ln2_b.reshape(L, 1, D_MODEL),
        "w_r_hi": w_r_hi, "w_r_lo": w_r_lo, "b_r": b_r,
        "tril": (jnp.arange(TM)[:, None] > jnp.arange(TM)[None, :]).astype(bf16),
    }
    m_ctx = _seq_dft_matrix(SEQ)
    m_lat = _seq_dft_matrix(DEC_SEQ)
    kc_all = _dup_cache(cache_k)
    vc_all = _dup_cache(cache_v)

    new_k, new_v = [], []
    for l in range(L):
        pq, praw, sgu, q, kd, vd, nk, nv = _inproj(x, mod, l, w)
        new_k.append(nk[:T_CTX].reshape(BATCH, SEQ, N_KV_HEADS, HEAD_DIM))
        new_v.append(nv[:T_CTX].reshape(BATCH, SEQ, N_KV_HEADS, HEAD_DIM))
        po = _pool(praw, l, w)
        fo_ctx = _seqdft(pq, m_ctx, l, w, n=SEQ, tr=SEQ, nseq=BATCH, row0=0)
        fo_lat = _seqdft(pq, m_lat, l, w, n=DEC_SEQ, tr=FFT_TR, nseq=DEC_BATCH, row0=T_CTX)
        ao_ctx = _attention(q, kd, vd, None, n=SEQ, tq=SEQ, nseq=BATCH, row0=0)
        ao_lat = _attention(q, kd, vd, (kc_all[l], vc_all[l]), n=DEC_SEQ, tq=ATT_TQ, nseq=DEC_BATCH,
                            row0=T_CTX)
        x1, h2, route, cnt = _outproj(x, mod, fo_ctx, fo_lat, po, sgu, ao_ctx, ao_lat, l, w)
        pos_slab, meta = _plan(route, cnt)
        pos_flat = pos_slab[:, :2].T.reshape(-1)
        counts = meta[0, ROUTE_E0:ROUTE_E0 + N_EXPERTS]
        nused = meta[1, :1]
        fill = meta[2, ROUTE_E0:ROUTE_E0 + N_EXPERTS]
        xs = _dispatch(pos_flat, fill, nused, h2)
        ys = _experts(counts, nused, xs, l, w_gate, w_up, w_down)
        x = _combine(pos_flat, x1, mod, route, ys, l, w)

    y_prompt = x[:T_CTX].reshape(BATCH, SEQ, D_MODEL)
    y_sample = x[T_CTX:].reshape(DEC_BATCH, DEC_SEQ, D_MODEL)
    return (y_prompt, y_sample, jnp.stack(new_k, axis=1), jnp.stack(new_v, axis=1))
```

```python
import functools

import numpy as np
import jax
import jax.numpy as jnp
from jax import lax
from jax.experimental import pallas as pl
from jax.experimental.pallas import tpu as pltpu

f32 = jnp.float32
bf16 = jnp.bfloat16
i32 = jnp.int32

D_MODEL = 1024
BATCH = 16
SEQ = 256
DEPTH = 4
DEC_BATCH = 2
DEC_SEQ = 4096
PAST_LEN = 512
GRID_W = 64
FFT_W = 256
POOL_W = 256
POOL_WINDOWS = (2, 4, 8, 16)
POOL_GROUP = 64
SGU_W = 256
SGU_HEADS = 4
CHUNK = 128
HEAD_DIM = 64
ATTN_W = 256
N_HEADS = 4
N_KV_HEADS = 2
KV_W = 128
IN_W = 1536
ROPE_THETA = 10000.0
N_GROUPS = 4
EXPERTS_PER_GROUP = 8
N_EXPERTS = 32
EXPERT_FF = 512
DEEPNORM_ALPHA = float((2 * DEPTH) ** 0.25)
LN_EPS = 1e-5
RMS_EPS = 1e-6

T_CTX = BATCH * SEQ
T_LAT = DEC_BATCH * DEC_SEQ
T_ALL = T_CTX + T_LAT
SEG = 4096
N_SEG = T_ALL // SEG

TM = 512
POOL_TB = 512
POOL_HALO = 8
FFT_TR = 512
ATT_TQ = 256
ATT_CHUNK = 512
DFT_SPLIT = 64
MOE_TM = 256
MOE_ROWS = 2 * T_ALL
MOE_NT = MOE_ROWS // MOE_TM + N_EXPERTS
PLAN_TB = 2048
ROUTE_E0 = 32
VMEM_LIMIT = 56 * 1024 * 1024


def _cparams(sem):
    return pltpu.CompilerParams(dimension_semantics=sem, vmem_limit_bytes=VMEM_LIMIT)


def _split_hi_lo(a):
    hi = a.astype(bf16)
    lo = (a - hi.astype(f32)).astype(bf16)
    return hi, lo


def _dot(a, b):
    return jnp.dot(a, b, preferred_element_type=f32)


def _mod_kernel(c_ref, w_ref, b_ref, o_ref):
    c = c_ref[...]
    s = c * jax.nn.sigmoid(c)
    s_hi, s_lo = _split_hi_lo(s)
    w_hi, w_lo = _split_hi_lo(w_ref[...])
    o_ref[...] = _dot(s_hi, w_hi) + _dot(s_hi, w_lo) + _dot(s_lo, w_hi) + b_ref[...]


def _modulation(cond8, w_mod, b_mod):
    tn = 1536
    return pl.pallas_call(
        _mod_kernel,
        grid=(DEPTH, 6 * D_MODEL // tn),
        in_specs=[
            pl.BlockSpec((8, D_MODEL), lambda l, j: (0, 0)),
            pl.BlockSpec((None, D_MODEL, tn), lambda l, j: (l, 0, j)),
            pl.BlockSpec((None, 1, tn), lambda l, j: (l, 0, j)),
        ],
        out_specs=pl.BlockSpec((None, 8, tn), lambda l, j: (l, 0, j)),
        out_shape=jax.ShapeDtypeStruct((DEPTH, 8, 6 * D_MODEL), f32),
        compiler_params=_cparams(("arbitrary", "arbitrary")),
        name="modulation",
    )(cond8, w_mod, b_mod.reshape(DEPTH, 1, 6 * D_MODEL))


def _head_rms(x, ones_bd, gain):
    sq = x * x
    hi, lo = _split_hi_lo(sq)
    ss = _dot(hi, ones_bd) + _dot(lo, ones_bd)
    return x * lax.rsqrt(ss * (1.0 / HEAD_DIM) + RMS_EPS) * gain


def _rope(x, cos, sin_a, sin_b):
    w = x.shape[-1]
    q4 = HEAD_DIM // 4
    return x * cos + pltpu.roll(x, w - q4, 1) * sin_a + pltpu.roll(x, q4, 1) * sin_b


def _dup_half(x, first):
    lane = lax.broadcasted_iota(i32, x.shape, 1)
    r = pltpu.roll(x, HEAD_DIM, 1)
    if first:
        return jnp.where(lane < HEAD_DIM, x, r)
    return jnp.where(lane >= HEAD_DIM, x, r)


def _gelu_tanh(x):
    c = np.sqrt(2.0 / np.pi).astype(np.float32)
    return x * (0.5 * (1.0 + jnp.tanh(c * (x + 0.044715 * (x * x * x)))))


def _inproj_kernel(xc_ref, xl_ref, mod_ref, win_ref, csc_ref, wsgu_ref, bsgu_ref, lng_ref, lnb_ref,
                   qg_ref, kg_ref, cos_ref, sina_ref, sinb_ref, ones_ref,
                   pq_ref, pool_ref, sgu_ref, q_ref, kd_ref, vd_ref, nk_ref, nv_ref):
    x = jnp.where(pl.program_id(0) < T_CTX // TM, xc_ref[...], xl_ref[...])
    mod = mod_ref[...]
    h = (x * (1.0 + mod[1:2]) + mod[0:1]).astype(bf16)
    proj = _dot(h, win_ref[...])

    a = proj[:, 0:FFT_W].astype(bf16)
    pq_ref[...] = _dot(a, csc_ref[...]).astype(bf16)

    pool_ref[...] = proj[:, FFT_W:FFT_W + POOL_W]

    o = FFT_W + POOL_W
    hgu = _gelu_tanh(proj[:, o:o + 2 * SGU_W])
    u = hgu[:, :SGU_W]
    v = hgu[:, SGU_W:]
    mu = jnp.mean(v, axis=-1, keepdims=True)
    vc = v - mu
    var = jnp.mean(vc * vc, axis=-1, keepdims=True)
    v = vc * lax.rsqrt(var + LN_EPS) * lng_ref[...] + lnb_ref[...]
    lane = lax.broadcasted_iota(i32, (CHUNK, SGU_W), 1)
    head = lane // (SGU_W // SGU_HEADS)
    wcat = wsgu_ref[...]
    for cidx in range(TM // CHUNK):
        rows = slice(cidx * CHUNK, (cidx + 1) * CHUNK)
        vch = v[rows]
        vblk = jnp.concatenate(
            [jnp.where(head == g, vch, 0.0) for g in range(SGU_HEADS)], axis=0).astype(bf16)
        sp = _dot(wcat, vblk) + bsgu_ref[...]
        sgu_ref[rows, :] = (u[rows] * sp).astype(bf16)

    o = o + 2 * SGU_W
    ones_bd = ones_ref[...]
    cos = cos_ref[...]
    sin_a = sina_ref[...]
    sin_b = sinb_ref[...]
    q = _head_rms(proj[:, o:o + ATTN_W], ones_bd, qg_ref[...])
    q = _rope(q, cos, sin_a, sin_b) * np.float32(HEAD_DIM ** -0.5 * np.log2(np.e))
    q_ref[...] = q.astype(bf16)
    o = o + ATTN_W
    k = _head_rms(proj[:, o:o + KV_W], ones_bd[:KV_W, :KV_W], kg_ref[...])
    nk_ref[...] = k
    k = _rope(k, cos[:, :KV_W], sin_a[:, :KV_W], sin_b[:, :KV_W])
    kd_ref[0] = _dup_half(k, True).astype(bf16)
    kd_ref[1] = _dup_half(k, False).astype(bf16)
    o = o + KV_W
    vv = proj[:, o:o + KV_W]
    nv_ref[...] = vv
    vd_ref[0] = _dup_half(vv, True).astype(bf16)
    vd_ref[1] = _dup_half(vv, False).astype(bf16)


def _rope_block(i):
    nlat = DEC_SEQ // TM
    nctx = T_CTX // TM
    return jnp.where(i < nctx, nlat, (i - nctx) % nlat)


def _ctx_tile(wd):
    return pl.BlockSpec((TM, wd), lambda i, *_: (jnp.minimum(i, T_CTX // TM - 1), 0))


def _lat_tile(wd):
    return pl.BlockSpec((TM, wd), lambda i, *_: (jnp.maximum(i - T_CTX // TM, 0), 0))


def _inproj(x_ctx, x_lat, mod, l, w):
    nt = T_ALL // TM
    tile = lambda wd: pl.BlockSpec((TM, wd), lambda i: (i, 0))
    const = lambda shape: pl.BlockSpec(shape, lambda i: (0,) * len(shape))
    rope_spec = pl.BlockSpec((TM, ATTN_W), lambda i: (_rope_block(i), 0))
    return pl.pallas_call(
        _inproj_kernel,
        grid=(nt,),
        in_specs=[
            _ctx_tile(D_MODEL), _lat_tile(D_MODEL),
            pl.BlockSpec((None, None, 6, D_MODEL), lambda i: (l, i // (SEG // TM), 0, 0)),
            pl.BlockSpec((None, D_MODEL, IN_W), lambda i: (l, 0, 0)),
            const((FFT_W, 2 * FFT_W)),
            pl.BlockSpec((None, CHUNK, SGU_HEADS * CHUNK), lambda i: (l, 0, 0)),
            pl.BlockSpec((None, CHUNK, SGU_W), lambda i: (l, 0, 0)),
            pl.BlockSpec((None, 1, SGU_W), lambda i: (l, 0, 0)),
            pl.BlockSpec((None, 1, SGU_W), lambda i: (l, 0, 0)),
            pl.BlockSpec((None, 1, ATTN_W), lambda i: (l, 0, 0)),
            pl.BlockSpec((None, 1, KV_W), lambda i: (l, 0, 0)),
            rope_spec, rope_spec, rope_spec,
            const((ATTN_W, ATTN_W)),
        ],
        out_specs=[
            tile(2 * FFT_W), tile(POOL_W), tile(SGU_W), tile(ATTN_W),
            pl.BlockSpec((N_KV_HEADS, TM, KV_W), lambda i: (0, i, 0)),
            pl.BlockSpec((N_KV_HEADS, TM, KV_W), lambda i: (0, i, 0)),
            tile(KV_W), tile(KV_W),
        ],
        out_shape=[
            jax.ShapeDtypeStruct((T_ALL, 2 * FFT_W), bf16),
            jax.ShapeDtypeStruct((T_ALL, POOL_W), f32),
            jax.ShapeDtypeStruct((T_ALL, SGU_W), bf16),
            jax.ShapeDtypeStruct((T_ALL, ATTN_W), bf16),
            jax.ShapeDtypeStruct((N_KV_HEADS, T_ALL, KV_W), bf16),
            jax.ShapeDtypeStruct((N_KV_HEADS, T_ALL, KV_W), bf16),
            jax.ShapeDtypeStruct((T_ALL, KV_W), f32),
            jax.ShapeDtypeStruct((T_ALL, KV_W), f32),
        ],
        compiler_params=_cparams(("arbitrary",)),
        name="inproj",
    )(x_ctx, x_lat, mod, w["w_in"], w["csc"], w["w_sgu"], w["b_sgu"], w["sgu_ln_g"], w["sgu_ln_b"],
      w["q_norm_g"], w["k_norm_g"], w["rope_cos"], w["rope_sin_a"], w["rope_sin_b"], w["ones_bd"])


def _pool_kernel(prev_ref, cur_ref, next_ref, wp_ref, scale_ref, o_ref):
    i = pl.program_id(0)
    n = jnp.where(i < T_CTX // POOL_TB, SEQ, DEC_SEQ)
    hl = POOL_HALO
    ext = jnp.concatenate([prev_ref[POOL_TB - hl:, :], cur_ref[...], next_ref[:hl, :]], axis=0)
    rows = POOL_TB + 2 * hl
    r = lax.broadcasted_iota(i32, (rows, 1), 0)
    pos = (i * POOL_TB + r - hl) & (n - 1)

    def back(a, s):
        return jnp.where(pos >= s, pltpu.roll(a, s, 0), 0.0)

    def fwd(a, s):
        return jnp.where(pos + s < n, pltpu.roll(a, rows - s, 0), 0.0)

    bsum = [back(ext, 1)]
    fsum = [ext]
    for k in range(3):
        s = 1 << k
        bsum.append(bsum[k] + back(bsum[k], s))
        fsum.append(fsum[k] + fwd(fsum[k], s))
    lane = lax.broadcasted_iota(i32, (1, POOL_W), 1)
    grp = lane // POOL_GROUP
    win = bsum[3] + fsum[3]
    half = jnp.full((1, POOL_W), POOL_WINDOWS[3] // 2, i32)
    for g in (2, 1, 0):
        win = jnp.where(grp == g, bsum[g] + fsum[g], win)
        half = jnp.where(grp == g, POOL_WINDOWS[g] // 2, half)
    cnt = (jnp.minimum(pos + half, n) - jnp.maximum(pos - half, 0)).astype(f32)
    y = (win / cnt - ext)[hl:hl + POOL_TB]
    o_ref[...] = (_dot(y.astype(bf16), wp_ref[...]) * scale_ref[...]).astype(bf16)


def _pool(p, l, w):
    nt = T_ALL // POOL_TB
    blk = lambda f: pl.BlockSpec((POOL_TB, POOL_W), lambda i: (f(i), 0))
    return pl.pallas_call(
        _pool_kernel,
        grid=(nt,),
        in_specs=[
            blk(lambda i: jnp.maximum(i - 1, 0)), blk(lambda i: i),
            blk(lambda i: jnp.minimum(i + 1, nt - 1)),
            pl.BlockSpec((None, POOL_W, POOL_W), lambda i: (l, 0, 0)),
            pl.BlockSpec((None, 1, POOL_W), lambda i: (l, 0, 0)),
        ],
        out_specs=blk(lambda i: i),
        out_shape=jax.ShapeDtypeStruct((T_ALL, POOL_W), bf16),
        compiler_params=_cparams(("arbitrary",)),
        name="pool",
    )(p, p, p, w["w_pool_bd"], w["pool_scale"])


def _seqdft_kernel(m_ref, pq_ref, w_ref, o_ref, *, n):
    f = _dot(m_ref[:, :n], pq_ref[:, :FFT_W]) + _dot(m_ref[:, n:], pq_ref[:, FFT_W:])
    o_ref[...] = _dot(f.astype(bf16), w_ref[...]).astype(bf16)


def _seqdft(pq, m, l, w, *, n, tr, nseq, row0):
    nr = n // tr
    b0 = row0 // n
    return pl.pallas_call(
        functools.partial(_seqdft_kernel, n=n),
        grid=(nr, nseq),
        in_specs=[
            pl.BlockSpec((tr, 2 * n), lambda i, b: (i, 0)),
            pl.BlockSpec((n, 2 * FFT_W), lambda i, b: (b0 + b, 0)),
            pl.BlockSpec((None, FFT_W, FFT_W), lambda i, b: (l, 0, 0)),
        ],
        out_specs=pl.BlockSpec((tr, FFT_W), lambda i, b: (b * nr + i, 0)),
        out_shape=jax.ShapeDtypeStruct((nseq * n, FFT_W), bf16),
        compiler_params=_cparams(("arbitrary", "arbitrary")),
        name="seqdft_%d" % n,
    )(m, pq, w["w_fft"])


def _attn_kernel(*refs, has_cache):
    if has_cache:
        q_ref, k_ref, v_ref, kc_ref, vc_ref, o_ref = refs
    else:
        q_ref, k_ref, v_ref, o_ref = refs
    q = q_ref[...]
    tq = q.shape[0]
    lane = lax.broadcasted_iota(i32, q.shape, 1)
    zero = jnp.zeros_like(q)
    qs = jnp.concatenate([jnp.where(lane < HEAD_DIM, q, zero),
                          jnp.where(lane >= HEAD_DIM, q, zero)], axis=0)
    nt = (((1,), (1,)), ((), ()))
    n = k_ref.shape[0]
    chunk = min(n, ATT_CHUNK)
    parts = [(k_ref, v_ref, c * chunk, chunk) for c in range(n // chunk)]
    if has_cache:
        parts = [(kc_ref, vc_ref, 0, PAST_LEN)] + parts
    m = jnp.full((2 * tq, 1), -jnp.inf, f32)
    den = jnp.zeros((2 * tq, 1), f32)
    acc = jnp.zeros((2 * tq, 2 * HEAD_DIM), f32)
    for kr, vr, off, size in parts:
        s = lax.dot_general(qs, kr[off:off + size, :], nt, preferred_element_type=f32)
        m_new = jnp.maximum(m, jnp.max(s, axis=-1, keepdims=True))
        alpha = jnp.exp2(m - m_new)
        p = jnp.exp2(s - m_new).astype(bf16)
        den = alpha * den + jnp.sum(p.astype(f32), axis=-1, keepdims=True)
        acc = alpha * acc + _dot(p, vr[off:off + size, :])
        m = m_new
    out = acc / den
    o_ref[...] = jnp.where(lane < HEAD_DIM, out[:tq], out[tq:]).astype(bf16)


def _attention(q, kd, vd, cache, *, n, tq, nseq, row0):
    nq = n // tq
    b0 = row0 // n
    q0 = row0 // tq
    in_specs = [
        pl.BlockSpec((tq, 2 * HEAD_DIM), lambda b, h, i: (q0 + b * nq + i, h)),
        pl.BlockSpec((None, n, KV_W), lambda b, h, i: (h, b0 + b, 0)),
        pl.BlockSpec((None, n, KV_W), lambda b, h, i: (h, b0 + b, 0)),
    ]
    args = [q, kd, vd]
    if cache is not None:
        cspec = pl.BlockSpec((None, None, PAST_LEN, KV_W), lambda b, h, i: (h, b, 0, 0))
        in_specs += [cspec, cspec]
        args += list(cache)
    return pl.pallas_call(
        functools.partial(_attn_kernel, has_cache=cache is not None),
        grid=(nseq, N_KV_HEADS, nq),
        in_specs=in_specs,
        out_specs=pl.BlockSpec((tq, 2 * HEAD_DIM), lambda b, h, i: (b * nq + i, h)),
        out_shape=jax.ShapeDtypeStruct((nseq * n, ATTN_W), bf16),
        compiler_params=_cparams(("arbitrary", "arbitrary", "arbitrary")),
        name="attention_%d" % n,
    )(*args)


def _layer_norm(x, g, b):
    mu = jnp.mean(x, axis=-1, keepdims=True)
    xc = x - mu
    var = jnp.mean(xc * xc, axis=-1, keepdims=True)
    return xc * lax.rsqrt(var + LN_EPS) * g + b


def _outproj_kernel(xc_ref, xl_ref, mod_ref, fc_ref, fl_ref, p_ref, s_ref, ac_ref, al_ref, wout_ref,
                    g_ref, b_ref, wrh_ref, wrl_ref, br_ref, tril_ref,
                    x1_ref, h2_ref, route_ref, cnt_ref, carry_ref):
    i = pl.program_id(0)

    @pl.when(i == 0)
    def _():
        carry_ref[...] = jnp.zeros_like(carry_ref)

    mod = mod_ref[...]
    is_ctx = i < T_CTX // TM
    f_mix = jnp.where(is_ctx, fc_ref[...], fl_ref[...])
    a_mix = jnp.where(is_ctx, ac_ref[...], al_ref[...])
    mix = (_dot(f_mix, wout_ref[0:256, :]) + _dot(p_ref[...], wout_ref[256:512, :])
           + _dot(s_ref[...], wout_ref[512:768, :]) + _dot(a_mix, wout_ref[768:1024, :]))
    x = jnp.where(is_ctx, xc_ref[...], xl_ref[...])
    x1 = _layer_norm(DEEPNORM_ALPHA * x + mod[2:3] * mix, g_ref[...], b_ref[...])
    x1_ref[...] = x1
    h2 = x1 * (1.0 + mod[4:5]) + mod[3:4]
    h2_ref[...] = h2

    h_hi, h_lo = _split_hi_lo(h2)
    logits = (_dot(h_hi, wrh_ref[...]) + _dot(h_hi, wrl_ref[...]) + _dot(h_lo, wrh_ref[...])
              + br_ref[...])
    lane = lax.broadcasted_iota(i32, logits.shape, 1).astype(f32)
    neg = jnp.float32(-jnp.inf)
    big = jnp.float32(1 << 20)
    gl = jnp.where(lane < N_GROUPS, logits, neg)
    gmax = jnp.max(gl, axis=-1, keepdims=True)
    gsel = jnp.min(jnp.where(gl == gmax, lane, big), axis=-1, keepdims=True)
    pg = 1.0 / jnp.sum(jnp.exp(gl - gmax), axis=-1, keepdims=True)
    e_lo = ROUTE_E0 + gsel * EXPERTS_PER_GROUP
    el = jnp.where((lane >= e_lo) & (lane < e_lo + EXPERTS_PER_GROUP), logits, neg)
    v1 = jnp.max(el, axis=-1, keepdims=True)
    i1 = jnp.min(jnp.where(el == v1, lane, big), axis=-1, keepdims=True)
    el2 = jnp.where(lane == i1, neg, el)
    v2 = jnp.max(el2, axis=-1, keepdims=True)
    i2 = jnp.min(jnp.where(el2 == v2, lane, big), axis=-1, keepdims=True)
    e2 = jnp.exp(v2 - v1)
    w1 = pg / (1.0 + e2)
    w2 = pg * e2 / (1.0 + e2)
    oh1 = lane == i1
    oh2 = lane == i2
    oh = jnp.where(oh1 | oh2, 1.0, 0.0)
    prefix = _dot(tril_ref[...], oh.astype(bf16)) + carry_ref[0:1, :]
    rank1 = jnp.sum(jnp.where(oh1, prefix, 0.0), axis=-1, keepdims=True)
    rank2 = jnp.sum(jnp.where(oh2, prefix, 0.0), axis=-1, keepdims=True)
    carry = carry_ref[...] + jnp.sum(oh, axis=0, keepdims=True)
    carry_ref[...] = carry
    cnt_ref[...] = carry
    cols = (i1 - ROUTE_E0, i2 - ROUTE_E0, w1, w2, rank1, rank2)
    route = jnp.zeros_like(logits)
    for j, col in enumerate(cols):
        route = jnp.where(lane == j, col, route)
    route_ref[...] = route


def _outproj(x_ctx, x_lat, mod, fo_ctx, fo_lat, po, so, ao_ctx, ao_lat, l, w):
    nt = T_ALL // TM
    tile = lambda wd: pl.BlockSpec((TM, wd), lambda i: (i, 0))
    vec = lambda wd: pl.BlockSpec((None, 1, wd), lambda i: (l, 0, 0))
    return pl.pallas_call(
        _outproj_kernel,
        grid=(nt,),
        in_specs=[
            _ctx_tile(D_MODEL), _lat_tile(D_MODEL),
            pl.BlockSpec((None, None, 6, D_MODEL), lambda i: (l, i // (SEG // TM), 0, 0)),
            _ctx_tile(FFT_W), _lat_tile(FFT_W), tile(POOL_W), tile(SGU_W),
            _ctx_tile(ATTN_W), _lat_tile(ATTN_W),
            pl.BlockSpec((None, D_MODEL, D_MODEL), lambda i: (l, 0, 0)),
            vec(D_MODEL), vec(D_MODEL),
            pl.BlockSpec((None, D_MODEL, 128), lambda i: (l, 0, 0)),
            pl.BlockSpec((None, D_MODEL, 128), lambda i: (l, 0, 0)),
            vec(128),
            pl.BlockSpec((TM, TM), lambda i: (0, 0)),
        ],
        out_specs=[tile(D_MODEL), tile(D_MODEL), tile(128), pl.BlockSpec((8, 128), lambda i: (0, 0))],
        out_shape=[
            jax.ShapeDtypeStruct((T_ALL, D_MODEL), f32),
            jax.ShapeDtypeStruct((T_ALL, D_MODEL), f32),
            jax.ShapeDtypeStruct((T_ALL, 128), f32),
            jax.ShapeDtypeStruct((8, 128), f32),
        ],
        scratch_shapes=[pltpu.VMEM((8, 128), f32)],
        compiler_params=_cparams(("arbitrary",)),
        name="outproj",
    )(x_ctx, x_lat, mod, fo_ctx, fo_lat, po, so, ao_ctx, ao_lat,
      w["w_out"], w["ln1_g"], w["ln1_b"], w["w_r_hi"], w["w_r_lo"], w["b_r"],
      w["tril"])


def _plan_kernel(route_ref, cnt_ref, pos_ref, meta_ref):
    lane = lax.broadcasted_iota(i32, (8, 128), 1)
    sub = lax.broadcasted_iota(i32, (8, 128), 0)
    cnt = cnt_ref[...]
    is_e = (lane >= ROUTE_E0) & (lane < ROUTE_E0 + N_EXPERTS)
    tiles = jnp.where(is_e, jnp.floor((cnt + (MOE_TM - 1.0)) * (1.0 / MOE_TM)), 0.0)
    cum = tiles
    for s in (1, 2, 4, 8, 16):
        cum = cum + jnp.where(lane >= s, pltpu.roll(cum, s, 1), 0.0)
    pstart = (cum - tiles) * MOE_TM
    nused = jnp.max(cum, axis=-1, keepdims=True)
    meta = jnp.where(sub == 0, cnt, jnp.where(sub == 1, nused, 0.0))
    meta_ref[...] = meta.astype(i32)

    r = route_ref[...]
    lane_t = lax.broadcasted_iota(i32, r.shape, 1).astype(f32)
    ps = pstart[0:1, :]

    def dest(ecol, rcol):
        hit = lane_t == (r[:, ecol:ecol + 1] + ROUTE_E0)
        return jnp.sum(jnp.where(hit, ps, 0.0), axis=-1, keepdims=True) + r[:, rcol:rcol + 1]

    pos = jnp.where(lane_t == 0, dest(0, 4), jnp.where(lane_t == 1, dest(1, 5), 0.0))
    pos_ref[...] = pos.astype(i32)


def _plan(route, cnt):
    return pl.pallas_call(
        _plan_kernel,
        grid=(T_ALL // PLAN_TB,),
        in_specs=[pl.BlockSpec((PLAN_TB, 128), lambda i: (i, 0)),
                  pl.BlockSpec((8, 128), lambda i: (0, 0))],
        out_specs=[pl.BlockSpec((PLAN_TB, 128), lambda i: (i, 0)),
                   pl.BlockSpec((8, 128), lambda i: (0, 0))],
        out_shape=[jax.ShapeDtypeStruct((T_ALL, 128), i32), jax.ShapeDtypeStruct((8, 128), i32)],
        compiler_params=_cparams(("arbitrary",)),
        name="plan",
    )(route, cnt)


def _experts_kernel(pos_ref, cnt_ref, nused_ref, h_hbm, wg_hbm, wu_hbm, wd_hbm, ys_ref,
                    xbuf, wg_f, wu_f, wd_f, wg_b, wu_b, wd_b, src, st, gsem, wsem, *, layer):
    i = pl.program_id(0)
    nused = nused_ref[0]
    NXT, NSLOT, LEFT = 0, 1, 2

    def w_copies(e, slot):
        return (pltpu.make_async_copy(wg_hbm.at[layer, e], wg_f.at[slot], wsem.at[slot, 0]),
                pltpu.make_async_copy(wu_hbm.at[layer, e], wu_f.at[slot], wsem.at[slot, 1]),
                pltpu.make_async_copy(wd_hbm.at[layer, e], wd_f.at[slot], wsem.at[slot, 2]))

    def next_nonempty(e):
        return lax.while_loop(
            lambda v: (v < N_EXPERTS) & (cnt_ref[jnp.minimum(v, N_EXPERTS - 1)] == 0),
            lambda v: v + 1, e)

    def issue_gather(tile, slot):
        base = tile * MOE_TM
        for j in range(MOE_TM):
            pltpu.make_async_copy(h_hbm.at[pl.ds(src[base + j], 1)], xbuf.at[slot, pl.ds(j, 1)],
                                  gsem.at[slot]).start()

    def wait_gather(slot):
        pltpu.make_async_copy(h_hbm.at[pl.ds(0, MOE_TM)], xbuf.at[slot], gsem.at[slot]).wait()

    @pl.when(i == 0)
    def _():
        def pad_expert(e, row0):
            n = cnt_ref[e]
            end = row0 + ((n + (MOE_TM - 1)) // MOE_TM) * MOE_TM

            def pad_row(r, c):
                src[r] = 0
                return c

            lax.fori_loop(row0 + n, end, pad_row, 0)
            return end

        lax.fori_loop(0, N_EXPERTS, pad_expert, jnp.int32(0))
        for k in range(2):
            def invert(t, c, k=k):
                src[pos_ref[k * T_ALL + t]] = t
                return c

            lax.fori_loop(0, T_ALL, invert, 0, unroll=8)

        issue_gather(0, 0)
        e0 = next_nonempty(jnp.int32(0))
        for c in w_copies(e0, 0):
            c.start()
        st[NXT] = e0
        st[NSLOT] = 0
        st[LEFT] = 0

    @pl.when(i < nused)
    def _():
        @pl.when(st[LEFT] == 0)
        def _():
            e = st[NXT]
            slot = st[NSLOT]
            for c in w_copies(e, slot):
                c.wait()
            e2 = next_nonempty(e + 1)

            @pl.when(e2 < N_EXPERTS)
            def _():
                for c in w_copies(e2, 1 - slot):
                    c.start()

            st[NXT] = e2
            st[NSLOT] = 1 - slot
            st[LEFT] = (cnt_ref[e] + (MOE_TM - 1)) // MOE_TM
            wg_b[...] = wg_f[slot].astype(bf16)
            wu_b[...] = wu_f[slot].astype(bf16)
            wd_b[...] = wd_f[slot].astype(bf16)

        xslot = i % 2
        wait_gather(xslot)
        issue_gather(jnp.minimum(i + 1, nused - 1), 1 - xslot)
        x = xbuf[xslot].astype(bf16)
        hg = _dot(x, wg_b[...])
        hu = _dot(x, wu_b[...])
        act = (hg * jax.nn.sigmoid(hg)) * hu
        ys_ref[...] = _dot(act.astype(bf16), wd_b[...])
        st[LEFT] = st[LEFT] - 1

    @pl.when(i >= nused)
    def _():
        @pl.when(i == nused)
        def _():
            wait_gather(i % 2)

        ys_ref[...] = jnp.zeros_like(ys_ref)


def _experts(pos_flat, counts, nused, h2, l, w_gate, w_up, w_down):
    hbm = pl.BlockSpec(memory_space=pl.ANY)
    grid_spec = pltpu.PrefetchScalarGridSpec(
        num_scalar_prefetch=3,
        grid=(MOE_NT,),
        in_specs=[hbm, hbm, hbm, hbm],
        out_specs=pl.BlockSpec((MOE_TM, D_MODEL), lambda i, p, c, nu: (i, 0)),
        scratch_shapes=[
            pltpu.VMEM((2, MOE_TM, D_MODEL), f32),
            pltpu.VMEM((2, D_MODEL, EXPERT_FF), f32),
            pltpu.VMEM((2, D_MODEL, EXPERT_FF), f32),
            pltpu.VMEM((2, EXPERT_FF, D_MODEL), f32),
            pltpu.VMEM((D_MODEL, EXPERT_FF), bf16),
            pltpu.VMEM((D_MODEL, EXPERT_FF), bf16),
            pltpu.VMEM((EXPERT_FF, D_MODEL), bf16),
            pltpu.SMEM((MOE_NT * MOE_TM,), i32),
            pltpu.SMEM((4,), i32),
            pltpu.SemaphoreType.DMA((2,)),
            pltpu.SemaphoreType.DMA((2, 3)),
        ],
    )
    return pl.pallas_call(
        functools.partial(_experts_kernel, layer=l),
        grid_spec=grid_spec,
        out_shape=jax.ShapeDtypeStruct((MOE_NT * MOE_TM, D_MODEL), f32),
        compiler_params=_cparams(("arbitrary",)),
        name="experts",
    )(pos_flat, counts, nused, h2, w_gate, w_up, w_down)


def _combine_kernel(pos_ref, x1_ref, mod_ref, route_ref, ys_hbm, g_ref, b_ref, oc_ref, ol_ref, ybuf, sem):
    i = pl.program_id(0)
    nt = pl.num_programs(0)

    def issue(tile, slot):
        for k in range(2):
            base = k * T_ALL + tile * TM
            for j in range(TM):
                pltpu.make_async_copy(ys_hbm.at[pl.ds(pos_ref[base + j], 1)],
                                      ybuf.at[slot, k, pl.ds(j, 1)], sem.at[slot]).start()

    def wait(slot):
        for k in range(2):
            pltpu.make_async_copy(ys_hbm.at[pl.ds(0, TM)], ybuf.at[slot, k], sem.at[slot]).wait()

    @pl.when(i == 0)
    def _():
        issue(0, 0)

    slot = i % 2
    wait(slot)
    issue(jnp.minimum(i + 1, nt - 1), 1 - slot)
    route = route_ref[...]
    mod = mod_ref[...]
    moe = route[:, 2:3] * ybuf[slot, 0] + route[:, 3:4] * ybuf[slot, 1]
    y = _layer_norm(DEEPNORM_ALPHA * x1_ref[...] + mod[5:6] * moe, g_ref[...], b_ref[...])

    @pl.when(i < T_CTX // TM)
    def _():
        oc_ref[...] = y

    @pl.when(i >= T_CTX // TM)
    def _():
        ol_ref[...] = y

    @pl.when(i == nt - 1)
    def _():
        wait(1 - slot)


def _combine(pos_flat, x1, mod, route, ys, l, w):
    nt = T_ALL // TM
    vec = pl.BlockSpec((None, 1, D_MODEL), lambda i, p: (l, 0, 0))
    grid_spec = pltpu.PrefetchScalarGridSpec(
        num_scalar_prefetch=1,
        grid=(nt,),
        in_specs=[
            pl.BlockSpec((TM, D_MODEL), lambda i, p: (i, 0)),
            pl.BlockSpec((None, None, 6, D_MODEL), lambda i, p: (l, i // (SEG // TM), 0, 0)),
            pl.BlockSpec((TM, 128), lambda i, p: (i, 0)),
            pl.BlockSpec(memory_space=pl.ANY),
            vec, vec,
        ],
        out_specs=[_ctx_tile(D_MODEL), _lat_tile(D_MODEL)],
        scratch_shapes=[pltpu.VMEM((2, 2, TM, D_MODEL), f32), pltpu.SemaphoreType.DMA((2,))],
    )
    return pl.pallas_call(
        _combine_kernel,
        grid_spec=grid_spec,
        out_shape=[jax.ShapeDtypeStruct((T_CTX, D_MODEL), f32),
                   jax.ShapeDtypeStruct((T_LAT, D_MODEL), f32)],
        compiler_params=_cparams(("arbitrary",)),
        name="combine",
    )(pos_flat, x1, mod, route, ys, w["ln2_g"], w["ln2_b"])


def _dft_cos_sin(n, scale):
    k = jnp.arange(n, dtype=i32)
    ang = ((k[:, None] * k[None, :]) % n).astype(f32) * np.float32(2.0 * np.pi / n)
    return jnp.cos(ang) * scale, jnp.sin(ang) * scale


def _seq_dft_matrix(n):
    g = min(DFT_SPLIT, n)
    j = jnp.arange(n, dtype=i32)[None, :]
    k1 = jnp.arange(n // g, dtype=i32)[:, None]
    k2 = jnp.arange(g, dtype=i32)[:, None]
    ang_a = ((k1 * j) % (n // g)).astype(f32) * np.float32(2.0 * np.pi * g / n)
    ang_b = ((k2 * j) % n).astype(f32) * np.float32(2.0 * np.pi / n)
    scale = np.float32(n ** -0.5)
    ca, sa = jnp.cos(ang_a), jnp.sin(ang_a)
    cb, sb = jnp.cos(ang_b) * scale, jnp.sin(ang_b) * scale
    ca2 = jnp.concatenate([ca, ca], axis=1)[:, None, :]
    sa2 = jnp.concatenate([sa, sa], axis=1)[:, None, :]
    cb2 = jnp.concatenate([cb, -sb], axis=1)[None, :, :]
    sb2 = jnp.concatenate([sb, cb], axis=1)[None, :, :]
    return (ca2 * cb2 - sa2 * sb2).astype(bf16).reshape(n, 2 * n)


def _rope_tables():
    rows = DEC_SEQ // GRID_W
    row = jnp.repeat(jnp.arange(rows), GRID_W).astype(f32)
    col = jnp.tile(jnp.arange(GRID_W), rows).astype(f32)
    n_freq = HEAD_DIM // 4
    inv = ROPE_THETA ** (-jnp.arange(n_freq, dtype=f32) / n_freq)
    ar = row[:, None] * inv
    ac = col[:, None] * inv
    ang = jnp.concatenate([ar, ar, ac, ac], axis=-1)
    cos = jnp.tile(jnp.cos(ang), (1, N_HEADS))
    sin = jnp.tile(jnp.sin(ang), (1, N_HEADS))
    first = (jnp.arange(ATTN_W) % (HEAD_DIM // 2)) < n_freq
    sin_a = jnp.where(first[None, :], -sin, 0.0)
    sin_b = jnp.where(first[None, :], 0.0, sin)
    ident = jnp.zeros((TM, ATTN_W), f32)
    return (jnp.concatenate([cos, ident + 1.0], axis=0),
            jnp.concatenate([sin_a, ident], axis=0),
            jnp.concatenate([sin_b, ident], axis=0))


def _dup_cache(cache):
    c = jnp.transpose(cache, (1, 3, 0, 2, 4))
    return jnp.concatenate([c, c], axis=-1).astype(bf16)


def kernel(x_prompt, x_sample, cache_k, cache_v, c, c_ctx, w_mod, b_mod, w_in, w_fft, w_pool, pool_scale, sgu_ln_g, sgu_ln_b, w_sgu, b_sgu, q_norm_g, k_norm_g, w_out, ln1_g, ln1_b, w_router_group, b_router_group, w_router_expert, b_router_expert, w_gate, w_up, w_down, ln2_g, ln2_b):
    L = DEPTH
    x_ctx = x_prompt.reshape(T_CTX, D_MODEL)
    x_lat = x_sample.reshape(T_LAT, D_MODEL)

    cond8 = jnp.concatenate([c_ctx[None, :], c, jnp.zeros((8 - 1 - DEC_BATCH, D_MODEL), f32)], axis=0)
    mod = _modulation(cond8, w_mod, b_mod)[:, :N_SEG].reshape(L, N_SEG, 6, D_MODEL)

    cc, sc = _dft_cos_sin(FFT_W, np.float32(FFT_W ** -0.5))
    rope_cos, rope_sin_a, rope_sin_b = _rope_tables()
    head_id = jnp.arange(ATTN_W) // HEAD_DIM
    eye_g = jnp.eye(len(POOL_WINDOWS), dtype=f32)
    w_r = jnp.zeros((L, D_MODEL, 128), f32)
    w_r = w_r.at[:, :, :N_GROUPS].set(w_router_group).at[:, :, ROUTE_E0:ROUTE_E0 + N_EXPERTS].set(w_router_expert)
    b_r = jnp.zeros((L, 1, 128), f32)
    b_r = b_r.at[:, 0, :N_GROUPS].set(b_router_group).at[:, 0, ROUTE_E0:ROUTE_E0 + N_EXPERTS].set(b_router_expert)
    w_r_hi, w_r_lo = _split_hi_lo(w_r)
    w = {
        "w_in": w_in.astype(bf16),
        "csc": jnp.concatenate([cc, sc], axis=1).astype(bf16),
        "w_sgu": jnp.transpose(w_sgu, (0, 2, 1, 3)).reshape(L, CHUNK, SGU_HEADS * CHUNK).astype(bf16),
        "b_sgu": jnp.repeat(jnp.transpose(b_sgu, (0, 2, 1)), SGU_W // SGU_HEADS, axis=2),
        "sgu_ln_g": sgu_ln_g.reshape(L, 1, SGU_W),
        "sgu_ln_b": sgu_ln_b.reshape(L, 1, SGU_W),
        "q_norm_g": jnp.tile(q_norm_g, (1, N_HEADS)).reshape(L, 1, ATTN_W),
        "k_norm_g": jnp.tile(k_norm_g, (1, N_KV_HEADS)).reshape(L, 1, KV_W),
        "rope_cos": rope_cos, "rope_sin_a": rope_sin_a, "rope_sin_b": rope_sin_b,
        "ones_bd": (head_id[:, None] == head_id[None, :]).astype(bf16),
        "w_pool_bd": jnp.einsum("lgcd,gh->lgchd", w_pool, eye_g).reshape(L, POOL_W, POOL_W).astype(bf16),
        "pool_scale": pool_scale.reshape(L, 1, POOL_W),
        "w_fft": w_fft.astype(bf16),
        "w_out": w_out.astype(bf16),
        "ln1_g": ln1_g.reshape(L, 1, D_MODEL), "ln1_b": ln1_b.reshape(L, 1, D_MODEL),
        "ln2_g": ln2_g.reshape(L, 1, D_MODEL), "ln2_b": ln2_b.reshape(L, 1, D_MODEL),
        "w_r_hi": w_r_hi, "w_r_lo": w_r_lo, "b_r": b_r,
        "tril": (jnp.arange(TM)[:, None] > jnp.arange(TM)[None, :]).astype(bf16),
    }
    m_ctx = _seq_dft_matrix(SEQ)
    m_lat = _seq_dft_matrix(DEC_SEQ)
    kc_all = _dup_cache(cache_k)
    vc_all = _dup_cache(cache_v)

    new_k, new_v = [], []
    for l in range(L):
        pq, praw, sgu, q, kd, vd, nk, nv = _inproj(x_ctx, x_lat, mod, l, w)
        new_k.append(nk[:T_CTX].reshape(BATCH, SEQ, N_KV_HEADS, HEAD_DIM))
        new_v.append(nv[:T_CTX].reshape(BATCH, SEQ, N_KV_HEADS, HEAD_DIM))
        po = _pool(praw, l, w)
        fo_ctx = _seqdft(pq, m_ctx, l, w, n=SEQ, tr=SEQ, nseq=BATCH, row0=0)
        fo_lat = _seqdft(pq, m_lat, l, w, n=DEC_SEQ, tr=FFT_TR, nseq=DEC_BATCH, row0=T_CTX)
        ao_ctx = _attention(q, kd, vd, None, n=SEQ, tq=SEQ, nseq=BATCH, row0=0)
        ao_lat = _attention(q, kd, vd, (kc_all[l], vc_all[l]), n=DEC_SEQ, tq=ATT_TQ, nseq=DEC_BATCH,
                            row0=T_CTX)
        x1, h2, route, cnt = _outproj(x_ctx, x_lat, mod, fo_ctx, fo_lat, po, sgu, ao_ctx, ao_lat, l, w)
        pos_slab, meta = _plan(route, cnt)
        pos_flat = pos_slab[:, :2].T.reshape(-1)
        counts = meta[0, ROUTE_E0:ROUTE_E0 + N_EXPERTS]
        nused = meta[1, :1]
        ys = _experts(pos_flat, counts, nused, h2, l, w_gate, w_up, w_down)
        x_ctx, x_lat = _combine(pos_flat, x1, mod, route, ys, l, w)

    y_prompt = x_ctx.reshape(BATCH, SEQ, D_MODEL)
    y_sample = x_lat.reshape(DEC_BATCH, DEC_SEQ, D_MODEL)
    return (y_prompt, y_sample, jnp.stack(new_k, axis=1), jnp.stack(new_v, axis=1))
```

```python
import functools

import numpy as np
import jax
import jax.numpy as jnp
from jax import lax
from jax.experimental import pallas as pl
from jax.experimental.pallas import tpu as pltpu

f32 = jnp.float32
bf16 = jnp.bfloat16
i32 = jnp.int32

D_MODEL = 1024
BATCH = 16
SEQ = 256
DEPTH = 4
DEC_BATCH = 2
DEC_SEQ = 4096
PAST_LEN = 512
GRID_W = 64
FFT_W = 256
POOL_W = 256
POOL_WINDOWS = (2, 4, 8, 16)
POOL_GROUP = 64
SGU_W = 256
SGU_HEADS = 4
CHUNK = 128
HEAD_DIM = 64
ATTN_W = 256
N_HEADS = 4
N_KV_HEADS = 2
KV_W = 128
IN_W = 1536
ROPE_THETA = 10000.0
N_GROUPS = 4
EXPERTS_PER_GROUP = 8
N_EXPERTS = 32
EXPERT_FF = 512
DEEPNORM_ALPHA = float((2 * DEPTH) ** 0.25)
LN_EPS = 1e-5
RMS_EPS = 1e-6

T_CTX = BATCH * SEQ
T_LAT = DEC_BATCH * DEC_SEQ
T_ALL = T_CTX + T_LAT
SEG = 4096
N_SEG = T_ALL // SEG

TM = 512
POOL_TB = 512
POOL_HALO = 8
FFT_TR = 512
ATT_TQ = 256
ATT_CHUNK = 1024
DFT_SPLIT = 64
MOE_TM = 256
MOE_ROWS = 2 * T_ALL
MOE_NT = MOE_ROWS // MOE_TM + N_EXPERTS
PLAN_TB = 2048
DISP_TB = 512
ROUTE_E0 = 32
VMEM_LIMIT = 56 * 1024 * 1024


def _cparams(sem):
    return pltpu.CompilerParams(dimension_semantics=sem, vmem_limit_bytes=VMEM_LIMIT)


def _split_hi_lo(a):
    hi = a.astype(bf16)
    lo = (a - hi.astype(f32)).astype(bf16)
    return hi, lo


def _dot(a, b):
    return jnp.dot(a, b, preferred_element_type=f32)


def _mod_kernel(c_ref, w_ref, b_ref, o_ref):
    c = c_ref[...]
    s = c * jax.nn.sigmoid(c)
    s_hi, s_lo = _split_hi_lo(s)
    w_hi, w_lo = _split_hi_lo(w_ref[...])
    o_ref[...] = _dot(s_hi, w_hi) + _dot(s_hi, w_lo) + _dot(s_lo, w_hi) + b_ref[...]


def _modulation(cond8, w_mod, b_mod):
    tn = 1536
    return pl.pallas_call(
        _mod_kernel,
        grid=(DEPTH, 6 * D_MODEL // tn),
        in_specs=[
            pl.BlockSpec((8, D_MODEL), lambda l, j: (0, 0)),
            pl.BlockSpec((None, D_MODEL, tn), lambda l, j: (l, 0, j)),
            pl.BlockSpec((None, 1, tn), lambda l, j: (l, 0, j)),
        ],
        out_specs=pl.BlockSpec((None, 8, tn), lambda l, j: (l, 0, j)),
        out_shape=jax.ShapeDtypeStruct((DEPTH, 8, 6 * D_MODEL), f32),
        compiler_params=_cparams(("arbitrary", "arbitrary")),
        name="modulation",
    )(cond8, w_mod, b_mod.reshape(DEPTH, 1, 6 * D_MODEL))


def _head_rms(x, ones_bd, gain):
    sq = x * x
    hi, lo = _split_hi_lo(sq)
    ss = _dot(hi, ones_bd) + _dot(lo, ones_bd)
    return x * lax.rsqrt(ss * (1.0 / HEAD_DIM) + RMS_EPS) * gain


def _rope(x, cos, sin_a, sin_b):
    w = x.shape[-1]
    q4 = HEAD_DIM // 4
    return x * cos + pltpu.roll(x, w - q4, 1) * sin_a + pltpu.roll(x, q4, 1) * sin_b


def _dup_half(x, first):
    lane = lax.broadcasted_iota(i32, x.shape, 1)
    r = pltpu.roll(x, HEAD_DIM, 1)
    if first:
        return jnp.where(lane < HEAD_DIM, x, r)
    return jnp.where(lane >= HEAD_DIM, x, r)


def _gelu_tanh(x):
    c = np.sqrt(2.0 / np.pi).astype(np.float32)
    return x * (0.5 * (1.0 + jnp.tanh(c * (x + 0.044715 * (x * x * x)))))


def _inproj_kernel(xc_ref, xl_ref, mod_ref, win_ref, csc_ref, wsgu_ref, bsgu_ref, lng_ref, lnb_ref,
                   qg_ref, kg_ref, cos_ref, sina_ref, sinb_ref, ones_ref,
                   pq_ref, pool_ref, sgu_ref, q_ref, kd_ref, vd_ref, nk_ref, nv_ref):
    x = jnp.where(pl.program_id(0) < T_CTX // TM, xc_ref[...], xl_ref[...])
    mod = mod_ref[...]
    h = (x * (1.0 + mod[1:2]) + mod[0:1]).astype(bf16)
    proj = _dot(h, win_ref[...])

    a = proj[:, 0:FFT_W].astype(bf16)
    pq_ref[...] = _dot(a, csc_ref[...]).astype(bf16)

    pool_ref[...] = proj[:, FFT_W:FFT_W + POOL_W]

    o = FFT_W + POOL_W
    hgu = _gelu_tanh(proj[:, o:o + 2 * SGU_W])
    u = hgu[:, :SGU_W]
    v = hgu[:, SGU_W:]
    mu = jnp.mean(v, axis=-1, keepdims=True)
    vc = v - mu
    var = jnp.mean(vc * vc, axis=-1, keepdims=True)
    v = vc * lax.rsqrt(var + LN_EPS) * lng_ref[...] + lnb_ref[...]
    lane = lax.broadcasted_iota(i32, (CHUNK, SGU_W), 1)
    head = lane // (SGU_W // SGU_HEADS)
    wcat = wsgu_ref[...]
    for cidx in range(TM // CHUNK):
        rows = slice(cidx * CHUNK, (cidx + 1) * CHUNK)
        vch = v[rows]
        vblk = jnp.concatenate(
            [jnp.where(head == g, vch, 0.0) for g in range(SGU_HEADS)], axis=0).astype(bf16)
        sp = _dot(wcat, vblk) + bsgu_ref[...]
        sgu_ref[rows, :] = (u[rows] * sp).astype(bf16)

    o = o + 2 * SGU_W
    ones_bd = ones_ref[...]
    cos = cos_ref[...]
    sin_a = sina_ref[...]
    sin_b = sinb_ref[...]
    q = _head_rms(proj[:, o:o + ATTN_W], ones_bd, qg_ref[...])
    q = _rope(q, cos, sin_a, sin_b) * np.float32(HEAD_DIM ** -0.5 * np.log2(np.e))
    q_ref[...] = q.astype(bf16)
    o = o + ATTN_W
    k = _head_rms(proj[:, o:o + KV_W], ones_bd[:KV_W, :KV_W], kg_ref[...])
    nk_ref[...] = k
    k = _rope(k, cos[:, :KV_W], sin_a[:, :KV_W], sin_b[:, :KV_W])
    kd_ref[0] = _dup_half(k, True).astype(bf16)
    kd_ref[1] = _dup_half(k, False).astype(bf16)
    o = o + KV_W
    vv = proj[:, o:o + KV_W]
    nv_ref[...] = vv
    vd_ref[0] = _dup_half(vv, True).astype(bf16)
    vd_ref[1] = _dup_half(vv, False).astype(bf16)


def _rope_block(i):
    nlat = DEC_SEQ // TM
    nctx = T_CTX // TM
    return jnp.where(i < nctx, nlat, (i - nctx) % nlat)


def _ctx_tile(wd):
    return pl.BlockSpec((TM, wd), lambda i, *_: (jnp.minimum(i, T_CTX // TM - 1), 0))


def _lat_tile(wd):
    return pl.BlockSpec((TM, wd), lambda i, *_: (jnp.maximum(i - T_CTX // TM, 0), 0))


def _inproj(x_ctx, x_lat, mod, l, w):
    nt = T_ALL // TM
    tile = lambda wd: pl.BlockSpec((TM, wd), lambda i: (i, 0))
    const = lambda shape: pl.BlockSpec(shape, lambda i: (0,) * len(shape))
    rope_spec = pl.BlockSpec((TM, ATTN_W), lambda i: (_rope_block(i), 0))
    return pl.pallas_call(
        _inproj_kernel,
        grid=(nt,),
        in_specs=[
            _ctx_tile(D_MODEL), _lat_tile(D_MODEL),
            pl.BlockSpec((None, None, 6, D_MODEL), lambda i: (l, i // (SEG // TM), 0, 0)),
            pl.BlockSpec((None, D_MODEL, IN_W), lambda i: (l, 0, 0)),
            const((FFT_W, 2 * FFT_W)),
            pl.BlockSpec((None, CHUNK, SGU_HEADS * CHUNK), lambda i: (l, 0, 0)),
            pl.BlockSpec((None, CHUNK, SGU_W), lambda i: (l, 0, 0)),
            pl.BlockSpec((None, 1, SGU_W), lambda i: (l, 0, 0)),
            pl.BlockSpec((None, 1, SGU_W), lambda i: (l, 0, 0)),
            pl.BlockSpec((None, 1, ATTN_W), lambda i: (l, 0, 0)),
            pl.BlockSpec((None, 1, KV_W), lambda i: (l, 0, 0)),
            rope_spec, rope_spec, rope_spec,
            const((ATTN_W, ATTN_W)),
        ],
        out_specs=[
            tile(2 * FFT_W), tile(POOL_W), tile(SGU_W), tile(ATTN_W),
            pl.BlockSpec((N_KV_HEADS, TM, KV_W), lambda i: (0, i, 0)),
            pl.BlockSpec((N_KV_HEADS, TM, KV_W), lambda i: (0, i, 0)),
            tile(KV_W), tile(KV_W),
        ],
        out_shape=[
            jax.ShapeDtypeStruct((T_ALL, 2 * FFT_W), bf16),
            jax.ShapeDtypeStruct((T_ALL, POOL_W), f32),
            jax.ShapeDtypeStruct((T_ALL, SGU_W), bf16),
            jax.ShapeDtypeStruct((T_ALL, ATTN_W), bf16),
            jax.ShapeDtypeStruct((N_KV_HEADS, T_ALL, KV_W), bf16),
            jax.ShapeDtypeStruct((N_KV_HEADS, T_ALL, KV_W), bf16),
            jax.ShapeDtypeStruct((T_ALL, KV_W), f32),
            jax.ShapeDtypeStruct((T_ALL, KV_W), f32),
        ],
        compiler_params=_cparams(("arbitrary",)),
        name="inproj",
    )(x_ctx, x_lat, mod, w["w_in"], w["csc"], w["w_sgu"], w["b_sgu"], w["sgu_ln_g"], w["sgu_ln_b"],
      w["q_norm_g"], w["k_norm_g"], w["rope_cos"], w["rope_sin_a"], w["rope_sin_b"], w["ones_bd"])


def _pool_kernel(prev_ref, cur_ref, next_ref, wp_ref, scale_ref, o_ref):
    i = pl.program_id(0)
    n = jnp.where(i < T_CTX // POOL_TB, SEQ, DEC_SEQ)
    hl = POOL_HALO
    ext = jnp.concatenate([prev_ref[POOL_TB - hl:, :], cur_ref[...], next_ref[:hl, :]], axis=0)
    rows = POOL_TB + 2 * hl
    r = lax.broadcasted_iota(i32, (rows, 1), 0)
    pos = (i * POOL_TB + r - hl) & (n - 1)

    def back(a, s):
        return jnp.where(pos >= s, pltpu.roll(a, s, 0), 0.0)

    def fwd(a, s):
        return jnp.where(pos + s < n, pltpu.roll(a, rows - s, 0), 0.0)

    bsum = [back(ext, 1)]
    fsum = [ext]
    for k in range(3):
        s = 1 << k
        bsum.append(bsum[k] + back(bsum[k], s))
        fsum.append(fsum[k] + fwd(fsum[k], s))
    lane = lax.broadcasted_iota(i32, (1, POOL_W), 1)
    grp = lane // POOL_GROUP
    win = bsum[3] + fsum[3]
    half = jnp.full((1, POOL_W), POOL_WINDOWS[3] // 2, i32)
    for g in (2, 1, 0):
        win = jnp.where(grp == g, bsum[g] + fsum[g], win)
        half = jnp.where(grp == g, POOL_WINDOWS[g] // 2, half)
    cnt = (jnp.minimum(pos + half, n) - jnp.maximum(pos - half, 0)).astype(f32)
    y = (win / cnt - ext)[hl:hl + POOL_TB]
    o_ref[...] = (_dot(y.astype(bf16), wp_ref[...]) * scale_ref[...]).astype(bf16)


def _pool(p, l, w):
    nt = T_ALL // POOL_TB
    blk = lambda f: pl.BlockSpec((POOL_TB, POOL_W), lambda i: (f(i), 0))
    return pl.pallas_call(
        _pool_kernel,
        grid=(nt,),
        in_specs=[
            blk(lambda i: jnp.maximum(i - 1, 0)), blk(lambda i: i),
            blk(lambda i: jnp.minimum(i + 1, nt - 1)),
            pl.BlockSpec((None, POOL_W, POOL_W), lambda i: (l, 0, 0)),
            pl.BlockSpec((None, 1, POOL_W), lambda i: (l, 0, 0)),
        ],
        out_specs=blk(lambda i: i),
        out_shape=jax.ShapeDtypeStruct((T_ALL, POOL_W), bf16),
        compiler_params=_cparams(("arbitrary",)),
        name="pool",
    )(p, p, p, w["w_pool_bd"], w["pool_scale"])


def _seqdft_kernel(*refs, n, nseq):
    m_ref, pq_refs, w_ref, o_ref = refs[0], refs[1:-2], refs[-2], refs[-1]
    per_blk = SEG // n
    for b in range(nseq):
        @pl.when(pl.program_id(1) == b)
        def _(b=b):
            pq_ref = pq_refs[b // per_blk]
            r0 = (b % per_blk) * n
            f = (_dot(m_ref[:, :n], pq_ref[r0:r0 + n, :FFT_W])
                 + _dot(m_ref[:, n:], pq_ref[r0:r0 + n, FFT_W:]))
            o_ref[...] = _dot(f.astype(bf16), w_ref[...]).astype(bf16)


def _seqdft(pq, m, l, w, *, n, tr, nseq, row0):
    nr = n // tr
    nblk = nseq * n // SEG
    pq_specs = [pl.BlockSpec((SEG, 2 * FFT_W), lambda i, b, j=j: (row0 // SEG + j, 0))
                for j in range(nblk)]
    return pl.pallas_call(
        functools.partial(_seqdft_kernel, n=n, nseq=nseq),
        grid=(nr, nseq),
        in_specs=[pl.BlockSpec((tr, 2 * n), lambda i, b: (i, 0))] + pq_specs
        + [pl.BlockSpec((None, FFT_W, FFT_W), lambda i, b: (l, 0, 0))],
        out_specs=pl.BlockSpec((tr, FFT_W), lambda i, b: (b * nr + i, 0)),
        out_shape=jax.ShapeDtypeStruct((nseq * n, FFT_W), bf16),
        compiler_params=_cparams(("arbitrary", "arbitrary")),
        name="seqdft_%d" % n,
    )(m, *([pq] * nblk), w["w_fft"])


def _attn_kernel(*refs, has_cache):
    if has_cache:
        q_ref, k_ref, v_ref, kc_ref, vc_ref, o_ref = refs
    else:
        q_ref, k_ref, v_ref, o_ref = refs
    q = q_ref[...]
    tq = q.shape[0]
    lane = lax.broadcasted_iota(i32, q.shape, 1)
    zero = jnp.zeros_like(q)
    qs = jnp.concatenate([jnp.where(lane < HEAD_DIM, q, zero),
                          jnp.where(lane >= HEAD_DIM, q, zero)], axis=0)
    nt = (((1,), (1,)), ((), ()))
    n = k_ref.shape[0]
    chunk = min(n, ATT_CHUNK)
    parts = [(k_ref, v_ref, c * chunk, chunk) for c in range(n // chunk)]
    if has_cache:
        parts = [(kc_ref, vc_ref, 0, PAST_LEN)] + parts
    m = jnp.full((2 * tq, 1), -jnp.inf, f32)
    den = jnp.zeros((2 * tq, 1), f32)
    acc = jnp.zeros((2 * tq, 2 * HEAD_DIM), f32)
    for kr, vr, off, size in parts:
        s = lax.dot_general(qs, kr[off:off + size, :], nt, preferred_element_type=f32)
        m_new = jnp.maximum(m, jnp.max(s, axis=-1, keepdims=True))
        alpha = jnp.exp2(m - m_new)
        p = jnp.exp2(s - m_new).astype(bf16)
        den = alpha * den + jnp.sum(p.astype(f32), axis=-1, keepdims=True)
        acc = alpha * acc + _dot(p, vr[off:off + size, :])
        m = m_new
    out = acc / den
    o_ref[...] = jnp.where(lane < HEAD_DIM, out[:tq], out[tq:]).astype(bf16)


def _attention(q, kd, vd, cache, *, n, tq, nseq, row0):
    nq = n // tq
    b0 = row0 // n
    q0 = row0 // tq
    in_specs = [
        pl.BlockSpec((tq, 2 * HEAD_DIM), lambda b, h, i: (q0 + b * nq + i, h)),
        pl.BlockSpec((None, n, KV_W), lambda b, h, i: (h, b0 + b, 0)),
        pl.BlockSpec((None, n, KV_W), lambda b, h, i: (h, b0 + b, 0)),
    ]
    args = [q, kd, vd]
    if cache is not None:
        cspec = pl.BlockSpec((None, None, PAST_LEN, KV_W), lambda b, h, i: (h, b, 0, 0))
        in_specs += [cspec, cspec]
        args += list(cache)
    return pl.pallas_call(
        functools.partial(_attn_kernel, has_cache=cache is not None),
        grid=(nseq, N_KV_HEADS, nq),
        in_specs=in_specs,
        out_specs=pl.BlockSpec((tq, 2 * HEAD_DIM), lambda b, h, i: (b * nq + i, h)),
        out_shape=jax.ShapeDtypeStruct((nseq * n, ATTN_W), bf16),
        compiler_params=_cparams(("arbitrary", "arbitrary", "arbitrary")),
        name="attention_%d" % n,
    )(*args)


def _layer_norm(x, g, b):
    mu = jnp.mean(x, axis=-1, keepdims=True)
    xc = x - mu
    var = jnp.mean(xc * xc, axis=-1, keepdims=True)
    return xc * lax.rsqrt(var + LN_EPS) * g + b


def _outproj_kernel(xc_ref, xl_ref, mod_ref, fc_ref, fl_ref, p_ref, s_ref, ac_ref, al_ref, wout_ref,
                    g_ref, b_ref, wrh_ref, wrl_ref, br_ref, tril_ref,
                    x1_ref, h2_ref, route_ref, cnt_ref, carry_ref):
    i = pl.program_id(0)

    @pl.when(i == 0)
    def _():
        carry_ref[...] = jnp.zeros_like(carry_ref)

    mod = mod_ref[...]
    is_ctx = i < T_CTX // TM
    f_mix = jnp.where(is_ctx, fc_ref[...], fl_ref[...])
    a_mix = jnp.where(is_ctx, ac_ref[...], al_ref[...])
    mix = _dot(jnp.concatenate([f_mix, p_ref[...], s_ref[...], a_mix], axis=1), wout_ref[...])
    x = jnp.where(is_ctx, xc_ref[...], xl_ref[...])
    x1 = _layer_norm(DEEPNORM_ALPHA * x + mod[2:3] * mix, g_ref[...], b_ref[...])
    x1_ref[...] = x1
    h2 = x1 * (1.0 + mod[4:5]) + mod[3:4]
    h2_ref[...] = h2

    h_hi, h_lo = _split_hi_lo(h2)
    logits = (_dot(h_hi, wrh_ref[...]) + _dot(h_hi, wrl_ref[...]) + _dot(h_lo, wrh_ref[...])
              + br_ref[...])
    lane = lax.broadcasted_iota(i32, logits.shape, 1).astype(f32)
    neg = jnp.float32(-jnp.inf)
    big = jnp.float32(1 << 20)
    gl = jnp.where(lane < N_GROUPS, logits, neg)
    gmax = jnp.max(gl, axis=-1, keepdims=True)
    gsel = jnp.min(jnp.where(gl == gmax, lane, big), axis=-1, keepdims=True)
    pg = 1.0 / jnp.sum(jnp.exp(gl - gmax), axis=-1, keepdims=True)
    e_lo = ROUTE_E0 + gsel * EXPERTS_PER_GROUP
    el = jnp.where((lane >= e_lo) & (lane < e_lo + EXPERTS_PER_GROUP), logits, neg)
    v1 = jnp.max(el, axis=-1, keepdims=True)
    i1 = jnp.min(jnp.where(el == v1, lane, big), axis=-1, keepdims=True)
    el2 = jnp.where(lane == i1, neg, el)
    v2 = jnp.max(el2, axis=-1, keepdims=True)
    i2 = jnp.min(jnp.where(el2 == v2, lane, big), axis=-1, keepdims=True)
    e2 = jnp.exp(v2 - v1)
    w1 = pg / (1.0 + e2)
    w2 = pg * e2 / (1.0 + e2)
    oh1 = lane == i1
    oh2 = lane == i2
    oh = jnp.where(oh1 | oh2, 1.0, 0.0)
    prefix = _dot(tril_ref[...], oh.astype(bf16)) + carry_ref[0:1, :]
    rank1 = jnp.sum(jnp.where(oh1, prefix, 0.0), axis=-1, keepdims=True)
    rank2 = jnp.sum(jnp.where(oh2, prefix, 0.0), axis=-1, keepdims=True)
    carry = carry_ref[...] + jnp.sum(oh, axis=0, keepdims=True)
    carry_ref[...] = carry
    cnt_ref[...] = carry
    cols = (i1 - ROUTE_E0, i2 - ROUTE_E0, w1, w2, rank1, rank2)
    route = jnp.zeros_like(logits)
    for j, col in enumerate(cols):
        route = jnp.where(lane == j, col, route)
    route_ref[...] = route


def _outproj(x_ctx, x_lat, mod, fo_ctx, fo_lat, po, so, ao_ctx, ao_lat, l, w):
    nt = T_ALL // TM
    tile = lambda wd: pl.BlockSpec((TM, wd), lambda i: (i, 0))
    vec = lambda wd: pl.BlockSpec((None, 1, wd), lambda i: (l, 0, 0))
    return pl.pallas_call(
        _outproj_kernel,
        grid=(nt,),
        in_specs=[
            _ctx_tile(D_MODEL), _lat_tile(D_MODEL),
            pl.BlockSpec((None, None, 6, D_MODEL), lambda i: (l, i // (SEG // TM), 0, 0)),
            _ctx_tile(FFT_W), _lat_tile(FFT_W), tile(POOL_W), tile(SGU_W),
            _ctx_tile(ATTN_W), _lat_tile(ATTN_W),
            pl.BlockSpec((None, D_MODEL, D_MODEL), lambda i: (l, 0, 0)),
            vec(D_MODEL), vec(D_MODEL),
            pl.BlockSpec((None, D_MODEL, 128), lambda i: (l, 0, 0)),
            pl.BlockSpec((None, D_MODEL, 128), lambda i: (l, 0, 0)),
            vec(128),
            pl.BlockSpec((TM, TM), lambda i: (0, 0)),
        ],
        out_specs=[tile(D_MODEL), tile(D_MODEL), tile(128), pl.BlockSpec((8, 128), lambda i: (0, 0))],
        out_shape=[
            jax.ShapeDtypeStruct((T_ALL, D_MODEL), f32),
            jax.ShapeDtypeStruct((T_ALL, D_MODEL), f32),
            jax.ShapeDtypeStruct((T_ALL, 128), f32),
            jax.ShapeDtypeStruct((8, 128), f32),
        ],
        scratch_shapes=[pltpu.VMEM((8, 128), f32)],
        compiler_params=_cparams(("arbitrary",)),
        name="outproj",
    )(x_ctx, x_lat, mod, fo_ctx, fo_lat, po, so, ao_ctx, ao_lat,
      w["w_out"], w["ln1_g"], w["ln1_b"], w["w_r_hi"], w["w_r_lo"], w["b_r"],
      w["tril"])


def _plan_kernel(route_ref, cnt_ref, pos_ref, meta_ref):
    lane = lax.broadcasted_iota(i32, (8, 128), 1)
    sub = lax.broadcasted_iota(i32, (8, 128), 0)
    cnt = cnt_ref[...]
    is_e = (lane >= ROUTE_E0) & (lane < ROUTE_E0 + N_EXPERTS)
    tiles = jnp.where(is_e, jnp.floor((cnt + (MOE_TM - 1.0)) * (1.0 / MOE_TM)), 0.0)
    cum = tiles
    for s in (1, 2, 4, 8, 16):
        cum = cum + jnp.where(lane >= s, pltpu.roll(cum, s, 1), 0.0)
    pstart = (cum - tiles) * MOE_TM
    nused = jnp.max(cum, axis=-1, keepdims=True)
    fill = jnp.where(is_e & (cnt != tiles * MOE_TM), pstart + (tiles - 1.0) * MOE_TM, -1.0)
    meta = jnp.where(sub == 0, cnt, jnp.where(sub == 1, nused, jnp.where(sub == 2, fill, 0.0)))
    meta_ref[...] = meta.astype(i32)

    r = route_ref[...]
    lane_t = lax.broadcasted_iota(i32, r.shape, 1).astype(f32)
    ps = pstart[0:1, :]

    def dest(ecol, rcol):
        hit = lane_t == (r[:, ecol:ecol + 1] + ROUTE_E0)
        return jnp.sum(jnp.where(hit, ps, 0.0), axis=-1, keepdims=True) + r[:, rcol:rcol + 1]

    pos = jnp.where(lane_t == 0, dest(0, 4), jnp.where(lane_t == 1, dest(1, 5), 0.0))
    pos_ref[...] = pos.astype(i32)


def _plan(route, cnt):
    return pl.pallas_call(
        _plan_kernel,
        grid=(T_ALL // PLAN_TB,),
        in_specs=[pl.BlockSpec((PLAN_TB, 128), lambda i: (i, 0)),
                  pl.BlockSpec((8, 128), lambda i: (0, 0))],
        out_specs=[pl.BlockSpec((PLAN_TB, 128), lambda i: (i, 0)),
                   pl.BlockSpec((8, 128), lambda i: (0, 0))],
        out_shape=[jax.ShapeDtypeStruct((T_ALL, 128), i32), jax.ShapeDtypeStruct((8, 128), i32)],
        compiler_params=_cparams(("arbitrary",)),
        name="plan",
    )(route, cnt)


def _dispatch_kernel(pos_ref, fill_ref, nused_ref, h_ref, xs_ref, zero_ref, sem, fill_sem):
    i = pl.program_id(0)

    def tile_fill(row0):
        return pltpu.make_async_copy(zero_ref, xs_ref.at[pl.ds(pl.multiple_of(row0, MOE_TM), MOE_TM)],
                                     fill_sem)

    @pl.when(i == 0)
    def _():
        zero_ref[...] = jnp.zeros_like(zero_ref)

        def start(e, c):
            @pl.when(fill_ref[e] >= 0)
            def _():
                tile_fill(jnp.maximum(fill_ref[e], 0)).start()
            return c

        def wait(e, c):
            @pl.when(fill_ref[e] >= 0)
            def _():
                tile_fill(jnp.maximum(fill_ref[e], 0)).wait()
            return c

        def start_tail(t, c):
            tile_fill(t * MOE_TM).start()
            return c

        def wait_tail(t, c):
            tile_fill(t * MOE_TM).wait()
            return c

        lax.fori_loop(0, N_EXPERTS, start, 0)
        lax.fori_loop(nused_ref[0], MOE_NT, start_tail, 0)
        lax.fori_loop(0, N_EXPERTS, wait, 0)
        lax.fori_loop(nused_ref[0], MOE_NT, wait_tail, 0)

    for k in range(2):
        base = k * T_ALL + i * DISP_TB
        for j in range(DISP_TB):
            row = pos_ref[base + j]
            pltpu.make_async_copy(h_ref.at[pl.ds(j, 1)], xs_ref.at[pl.ds(row, 1)], sem).start()
    for k in range(2):
        pltpu.make_async_copy(h_ref, xs_ref.at[pl.ds(0, DISP_TB)], sem).wait()


def _dispatch(pos_flat, fill, nused, h2):
    grid_spec = pltpu.PrefetchScalarGridSpec(
        num_scalar_prefetch=3,
        grid=(T_ALL // DISP_TB,),
        in_specs=[pl.BlockSpec((DISP_TB, D_MODEL), lambda i, p, f, nu: (i, 0))],
        out_specs=pl.BlockSpec(memory_space=pl.ANY),
        scratch_shapes=[pltpu.VMEM((MOE_TM, D_MODEL), f32),
                        pltpu.SemaphoreType.DMA(()), pltpu.SemaphoreType.DMA(())],
    )
    return pl.pallas_call(
        _dispatch_kernel,
        grid_spec=grid_spec,
        out_shape=jax.ShapeDtypeStruct((MOE_NT * MOE_TM, D_MODEL), f32),
        compiler_params=_cparams(("arbitrary",)),
        name="dispatch",
    )(pos_flat, fill, nused, h2)


def _experts_kernel(cnt_ref, nused_ref, xs_ref, wg_hbm, wu_hbm, wd_hbm, ys_ref,
                    wg_f, wu_f, wd_f, wg_b, wu_b, wd_b, st, wsem, *, layer):
    i = pl.program_id(0)
    nused = nused_ref[0]
    NXT, NSLOT, LEFT = 0, 1, 2

    def w_copies(e, slot):
        return (pltpu.make_async_copy(wg_hbm.at[layer, e], wg_f.at[slot], wsem.at[slot, 0]),
                pltpu.make_async_copy(wu_hbm.at[layer, e], wu_f.at[slot], wsem.at[slot, 1]),
                pltpu.make_async_copy(wd_hbm.at[layer, e], wd_f.at[slot], wsem.at[slot, 2]))

    def next_nonempty(e):
        return lax.while_loop(
            lambda v: (v < N_EXPERTS) & (cnt_ref[jnp.minimum(v, N_EXPERTS - 1)] == 0),
            lambda v: v + 1, e)

    @pl.when(i == 0)
    def _():
        e0 = next_nonempty(jnp.int32(0))
        for c in w_copies(e0, 0):
            c.start()
        st[NXT] = e0
        st[NSLOT] = 0
        st[LEFT] = 0

    @pl.when(i < nused)
    def _():
        @pl.when(st[LEFT] == 0)
        def _():
            e = st[NXT]
            slot = st[NSLOT]
            for c in w_copies(e, slot):
                c.wait()
            e2 = next_nonempty(e + 1)

            @pl.when(e2 < N_EXPERTS)
            def _():
                for c in w_copies(e2, 1 - slot):
                    c.start()

            st[NXT] = e2
            st[NSLOT] = 1 - slot
            st[LEFT] = (cnt_ref[e] + (MOE_TM - 1)) // MOE_TM
            wg_b[...] = wg_f[slot].astype(bf16)
            wu_b[...] = wu_f[slot].astype(bf16)
            wd_b[...] = wd_f[slot].astype(bf16)

        x = xs_ref[...].astype(bf16)
        hg = _dot(x, wg_b[...])
        hu = _dot(x, wu_b[...])
        act = (hg * jax.nn.sigmoid(hg)) * hu
        ys_ref[...] = _dot(act.astype(bf16), wd_b[...])
        st[LEFT] = st[LEFT] - 1

    @pl.when(i >= nused)
    def _():
        ys_ref[...] = jnp.zeros_like(ys_ref)


def _experts(counts, nused, xs, l, w_gate, w_up, w_down):
    hbm = pl.BlockSpec(memory_space=pl.ANY)
    grid_spec = pltpu.PrefetchScalarGridSpec(
        num_scalar_prefetch=2,
        grid=(MOE_NT,),
        in_specs=[pl.BlockSpec((MOE_TM, D_MODEL), lambda i, c, nu: (jnp.minimum(i, nu[0] - 1), 0)),
                  hbm, hbm, hbm],
        out_specs=pl.BlockSpec((MOE_TM, D_MODEL), lambda i, c, nu: (i, 0)),
        scratch_shapes=[
            pltpu.VMEM((2, D_MODEL, EXPERT_FF), f32),
            pltpu.VMEM((2, D_MODEL, EXPERT_FF), f32),
            pltpu.VMEM((2, EXPERT_FF, D_MODEL), f32),
            pltpu.VMEM((D_MODEL, EXPERT_FF), bf16),
            pltpu.VMEM((D_MODEL, EXPERT_FF), bf16),
            pltpu.VMEM((EXPERT_FF, D_MODEL), bf16),
            pltpu.SMEM((4,), i32),
            pltpu.SemaphoreType.DMA((2, 3)),
        ],
    )
    return pl.pallas_call(
        functools.partial(_experts_kernel, layer=l),
        grid_spec=grid_spec,
        out_shape=jax.ShapeDtypeStruct((MOE_NT * MOE_TM, D_MODEL), f32),
        compiler_params=_cparams(("arbitrary",)),
        name="experts",
    )(counts, nused, xs, w_gate, w_up, w_down)


def _combine_kernel(pos_ref, x1_ref, mod_ref, route_ref, ys_hbm, g_ref, b_ref, oc_ref, ol_ref, ybuf, sem):
    i = pl.program_id(0)
    nt = pl.num_programs(0) - 1

    @pl.when(i < nt)
    def _():
        slot = i % 2
        for k in range(2):
            base = k * T_ALL + i * TM
            for j in range(TM):
                pltpu.make_async_copy(ys_hbm.at[pl.ds(pos_ref[base + j], 1)],
                                      ybuf.at[slot, k, pl.ds(j, 1)], sem.at[slot]).start()

    @pl.when(i >= 1)
    def _():
        slot = (i - 1) % 2
        for k in range(2):
            pltpu.make_async_copy(ys_hbm.at[pl.ds(0, TM)], ybuf.at[slot, k], sem.at[slot]).wait()
        route = route_ref[...]
        mod = mod_ref[...]
        moe = route[:, 2:3] * ybuf[slot, 0] + route[:, 3:4] * ybuf[slot, 1]
        y = _layer_norm(DEEPNORM_ALPHA * x1_ref[...] + mod[5:6] * moe, g_ref[...], b_ref[...])

        @pl.when(i - 1 < T_CTX // TM)
        def _():
            oc_ref[...] = y

        @pl.when(i - 1 >= T_CTX // TM)
        def _():
            ol_ref[...] = y


def _combine(pos_flat, x1, mod, route, ys, l, w):
    nt = T_ALL // TM
    vec = pl.BlockSpec((None, 1, D_MODEL), lambda i, p: (l, 0, 0))
    nctx = T_CTX // TM
    prev = lambda i: jnp.maximum(i - 1, 0)
    grid_spec = pltpu.PrefetchScalarGridSpec(
        num_scalar_prefetch=1,
        grid=(nt + 1,),
        in_specs=[
            pl.BlockSpec((TM, D_MODEL), lambda i, p: (prev(i), 0)),
            pl.BlockSpec((None, None, 6, D_MODEL), lambda i, p: (l, prev(i) // (SEG // TM), 0, 0)),
            pl.BlockSpec((TM, 128), lambda i, p: (prev(i), 0)),
            pl.BlockSpec(memory_space=pl.ANY),
            vec, vec,
        ],
        out_specs=[pl.BlockSpec((TM, D_MODEL), lambda i, p: (jnp.minimum(prev(i), nctx - 1), 0)),
                   pl.BlockSpec((TM, D_MODEL), lambda i, p: (jnp.maximum(prev(i) - nctx, 0), 0))],
        scratch_shapes=[pltpu.VMEM((2, 2, TM, D_MODEL), f32), pltpu.SemaphoreType.DMA((2,))],
    )
    return pl.pallas_call(
        _combine_kernel,
        grid_spec=grid_spec,
        out_shape=[jax.ShapeDtypeStruct((T_CTX, D_MODEL), f32),
                   jax.ShapeDtypeStruct((T_LAT, D_MODEL), f32)],
        compiler_params=_cparams(("arbitrary",)),
        name="combine",
    )(pos_flat, x1, mod, route, ys, w["ln2_g"], w["ln2_b"])


def _dft_cos_sin(n, scale):
    k = jnp.arange(n, dtype=i32)
    ang = ((k[:, None] * k[None, :]) % n).astype(f32) * np.float32(2.0 * np.pi / n)
    return jnp.cos(ang) * scale, jnp.sin(ang) * scale


def _seq_dft_matrix(n):
    g = min(DFT_SPLIT, n)
    j = jnp.arange(n, dtype=i32)[None, :]
    k1 = jnp.arange(n // g, dtype=i32)[:, None]
    k2 = jnp.arange(g, dtype=i32)[:, None]
    ang_a = ((k1 * j) % (n // g)).astype(f32) * np.float32(2.0 * np.pi * g / n)
    ang_b = ((k2 * j) % n).astype(f32) * np.float32(2.0 * np.pi / n)
    scale = np.float32(n ** -0.5)
    ca, sa = jnp.cos(ang_a), jnp.sin(ang_a)
    cb, sb = jnp.cos(ang_b) * scale, jnp.sin(ang_b) * scale
    ca2 = jnp.concatenate([ca, ca], axis=1)[:, None, :]
    sa2 = jnp.concatenate([sa, sa], axis=1)[:, None, :]
    cb2 = jnp.concatenate([cb, -sb], axis=1)[None, :, :]
    sb2 = jnp.concatenate([sb, cb], axis=1)[None, :, :]
    return (ca2 * cb2 - sa2 * sb2).astype(bf16).reshape(n, 2 * n)


def _rope_tables():
    rows = DEC_SEQ // GRID_W
    row = jnp.repeat(jnp.arange(rows), GRID_W).astype(f32)
    col = jnp.tile(jnp.arange(GRID_W), rows).astype(f32)
    n_freq = HEAD_DIM // 4
    inv = ROPE_THETA ** (-jnp.arange(n_freq, dtype=f32) / n_freq)
    ar = row[:, None] * inv
    ac = col[:, None] * inv
    ang = jnp.concatenate([ar, ar, ac, ac], axis=-1)
    cos = jnp.tile(jnp.cos(ang), (1, N_HEADS))
    sin = jnp.tile(jnp.sin(ang), (1, N_HEADS))
    first = (jnp.arange(ATTN_W) % (HEAD_DIM // 2)) < n_freq
    sin_a = jnp.where(first[None, :], -sin, 0.0)
    sin_b = jnp.where(first[None, :], 0.0, sin)
    ident = jnp.zeros((TM, ATTN_W), f32)
    return (jnp.concatenate([cos, ident + 1.0], axis=0),
            jnp.concatenate([sin_a, ident], axis=0),
            jnp.concatenate([sin_b, ident], axis=0))


def _dup_cache(cache):
    c = jnp.transpose(cache, (1, 3, 0, 2, 4))
    return jnp.concatenate([c, c], axis=-1).astype(bf16)


def kernel(x_prompt, x_sample, cache_k, cache_v, c, c_ctx, w_mod, b_mod, w_in, w_fft, w_pool, pool_scale, sgu_ln_g, sgu_ln_b, w_sgu, b_sgu, q_norm_g, k_norm_g, w_out, ln1_g, ln1_b, w_router_group, b_router_group, w_router_expert, b_router_expert, w_gate, w_up, w_down, ln2_g, ln2_b):
    L = DEPTH
    x_ctx = x_prompt.reshape(T_CTX, D_MODEL)
    x_lat = x_sample.reshape(T_LAT, D_MODEL)

    cond8 = jnp.concatenate([c_ctx[None, :], c, jnp.zeros((8 - 1 - DEC_BATCH, D_MODEL), f32)], axis=0)
    mod = _modulation(cond8, w_mod, b_mod)[:, :N_SEG].reshape(L, N_SEG, 6, D_MODEL)

    cc, sc = _dft_cos_sin(FFT_W, np.float32(FFT_W ** -0.5))
    rope_cos, rope_sin_a, rope_sin_b = _rope_tables()
    head_id = jnp.arange(ATTN_W) // HEAD_DIM
    eye_g = jnp.eye(len(POOL_WINDOWS), dtype=f32)
    w_r = jnp.zeros((L, D_MODEL, 128), f32)
    w_r = w_r.at[:, :, :N_GROUPS].set(w_router_group).at[:, :, ROUTE_E0:ROUTE_E0 + N_EXPERTS].set(w_router_expert)
    b_r = jnp.zeros((L, 1, 128), f32)
    b_r = b_r.at[:, 0, :N_GROUPS].set(b_router_group).at[:, 0, ROUTE_E0:ROUTE_E0 + N_EXPERTS].set(b_router_expert)
    w_r_hi, w_r_lo = _split_hi_lo(w_r)
    w = {
        "w_in": w_in.astype(bf16),
        "csc": jnp.concatenate([cc, sc], axis=1).astype(bf16),
        "w_sgu": jnp.transpose(w_sgu, (0, 2, 1, 3)).reshape(L, CHUNK, SGU_HEADS * CHUNK).astype(bf16),
        "b_sgu": jnp.repeat(jnp.transpose(b_sgu, (0, 2, 1)), SGU_W // SGU_HEADS, axis=2),
        "sgu_ln_g": sgu_ln_g.reshape(L, 1, SGU_W),
        "sgu_ln_b": sgu_ln_b.reshape(L, 1, SGU_W),
        "q_norm_g": jnp.tile(q_norm_g, (1, N_HEADS)).reshape(L, 1, ATTN_W),
        "k_norm_g": jnp.tile(k_norm_g, (1, N_KV_HEADS)).reshape(L, 1, KV_W),
        "rope_cos": rope_cos, "rope_sin_a": rope_sin_a, "rope_sin_b": rope_sin_b,
        "ones_bd": (head_id[:, None] == head_id[None, :]).astype(bf16),
        "w_pool_bd": jnp.einsum("lgcd,gh->lgchd", w_pool, eye_g).reshape(L, POOL_W, POOL_W).astype(bf16),
        "pool_scale": pool_scale.reshape(L, 1, POOL_W),
        "w_fft": w_fft.astype(bf16),
        "w_out": w_out.astype(bf16),
        "ln1_g": ln1_g.reshape(L, 1, D_MODEL), "ln1_b": ln1_b.reshape(L, 1, D_MODEL),
        "ln2_g": ln2_g.reshape(L, 1, D_MODEL), "ln2_b": ln2_b.reshape(L, 1, D_MODEL),
        "w_r_hi": w_r_hi, "w_r_lo": w_r_lo, "b_r": b_r,
        "tril": (jnp.arange(TM)[:, None] > jnp.arange(TM)[None, :]).astype(bf16),
    }
    m_ctx = _seq_dft_matrix(SEQ)
    m_lat = _seq_dft_matrix(DEC_SEQ)
    kc_all = _dup_cache(cache_k)
    vc_all = _dup_cache(cache_v)

    new_k, new_v = [], []
    for l in range(L):
        pq, praw, sgu, q, kd, vd, nk, nv = _inproj(x_ctx, x_lat, mod, l, w)
        new_k.append(nk[:T_CTX].reshape(BATCH, SEQ, N_KV_HEADS, HEAD_DIM))
        new_v.append(nv[:T_CTX].reshape(BATCH, SEQ, N_KV_HEADS, HEAD_DIM))
        po = _pool(praw, l, w)
        fo_ctx = _seqdft(pq, m_ctx, l, w, n=SEQ, tr=SEQ, nseq=BATCH, row0=0)
        fo_lat = _seqdft(pq, m_lat, l, w, n=DEC_SEQ, tr=FFT_TR, nseq=DEC_BATCH, row0=T_CTX)
        ao_ctx = _attention(q, kd, vd, None, n=SEQ, tq=SEQ, nseq=BATCH, row0=0)
        ao_lat = _attention(q, kd, vd, (kc_all[l], vc_all[l]), n=DEC_SEQ, tq=ATT_TQ, nseq=DEC_BATCH,
                            row0=T_CTX)
        x1, h2, route, cnt = _outproj(x_ctx, x_lat, mod, fo_ctx, fo_lat, po, sgu, ao_ctx, ao_lat, l, w)
        pos_slab, meta = _plan(route, cnt)
        pos_flat = pos_slab[:, :2].T.reshape(-1)
        counts = meta[0, ROUTE_E0:ROUTE_E0 + N_EXPERTS]
        nused = meta[1, :1]
        fill = meta[2, ROUTE_E0:ROUTE_E0 + N_EXPERTS]
        xs = _dispatch(pos_flat, fill, nused, h2)
        ys = _experts(counts, nused, xs, l, w_gate, w_up, w_down)
        x_ctx, x_lat = _combine(pos_flat, x1, mod, route, ys, l, w)

    y_prompt = x_ctx.reshape(BATCH, SEQ, D_MODEL)
    y_sample = x_lat.reshape(DEC_BATCH, DEC_SEQ, D_MODEL)
    return (y_prompt, y_sample, jnp.stack(new_k, axis=1), jnp.stack(new_v, axis=1))
```

```python
import functools

import numpy as np
import jax
import jax.numpy as jnp
from jax import lax
from jax.experimental import pallas as pl
from jax.experimental.pallas import tpu as pltpu

f32 = jnp.float32
bf16 = jnp.bfloat16
i32 = jnp.int32

D_MODEL = 1024
BATCH = 16
SEQ = 256
DEPTH = 4
DEC_BATCH = 2
DEC_SEQ = 4096
PAST_LEN = 512
GRID_W = 64
FFT_W = 256
POOL_W = 256
POOL_WINDOWS = (2, 4, 8, 16)
POOL_GROUP = 64
SGU_W = 256
SGU_HEADS = 4
CHUNK = 128
HEAD_DIM = 64
ATTN_W = 256
N_HEADS = 4
N_KV_HEADS = 2
KV_W = 128
IN_W = 1536
ROPE_THETA = 10000.0
N_GROUPS = 4
EXPERTS_PER_GROUP = 8
N_EXPERTS = 32
EXPERT_FF = 512
DEEPNORM_ALPHA = float((2 * DEPTH) ** 0.25)
LN_EPS = 1e-5
RMS_EPS = 1e-6

T_CTX = BATCH * SEQ
T_LAT = DEC_BATCH * DEC_SEQ
T_ALL = T_CTX + T_LAT
SEG = 4096
N_SEG = T_ALL // SEG

TM = 512
POOL_TB = 512
POOL_HALO = 8
FFT_TR = 512
ATT_TQ = 512
ATT_CHUNK = 1024
DFT_SPLIT = 64
MOE_TM = 256
MOE_ROWS = 2 * T_ALL
MOE_NT = MOE_ROWS // MOE_TM + N_EXPERTS
PLAN_TB = 2048
DISP_TB = 512
ROUTE_E0 = 32
VMEM_LIMIT = 56 * 1024 * 1024


def _cparams(sem):
    return pltpu.CompilerParams(dimension_semantics=sem, vmem_limit_bytes=VMEM_LIMIT)


def _split_hi_lo(a):
    hi = a.astype(bf16)
    lo = (a - hi.astype(f32)).astype(bf16)
    return hi, lo


def _dot(a, b):
    return jnp.dot(a, b, preferred_element_type=f32)


def _mod_kernel(c_ref, w_ref, b_ref, o_ref):
    c = c_ref[...]
    s = c * jax.nn.sigmoid(c)
    s_hi, s_lo = _split_hi_lo(s)
    w_hi, w_lo = _split_hi_lo(w_ref[...])
    o_ref[...] = _dot(s_hi, w_hi) + _dot(s_hi, w_lo) + _dot(s_lo, w_hi) + b_ref[...]


def _modulation(cond8, w_mod, b_mod):
    tn = 1536
    return pl.pallas_call(
        _mod_kernel,
        grid=(DEPTH, 6 * D_MODEL // tn),
        in_specs=[
            pl.BlockSpec((8, D_MODEL), lambda l, j: (0, 0)),
            pl.BlockSpec((None, D_MODEL, tn), lambda l, j: (l, 0, j)),
            pl.BlockSpec((None, 1, tn), lambda l, j: (l, 0, j)),
        ],
        out_specs=pl.BlockSpec((None, 8, tn), lambda l, j: (l, 0, j)),
        out_shape=jax.ShapeDtypeStruct((DEPTH, 8, 6 * D_MODEL), f32),
        compiler_params=_cparams(("arbitrary", "arbitrary")),
        name="modulation",
    )(cond8, w_mod, b_mod.reshape(DEPTH, 1, 6 * D_MODEL))


def _head_rms(x, ones_bd, gain):
    ss = _dot((x * x).astype(bf16), ones_bd)
    return x * lax.rsqrt(ss * (1.0 / HEAD_DIM) + RMS_EPS) * gain


def _rope(x, cos, sin_a, sin_b):
    w = x.shape[-1]
    q4 = HEAD_DIM // 4
    return x * cos + pltpu.roll(x, w - q4, 1) * sin_a + pltpu.roll(x, q4, 1) * sin_b


def _dup_half(x, first):
    lane = lax.broadcasted_iota(i32, x.shape, 1)
    r = pltpu.roll(x, HEAD_DIM, 1)
    if first:
        return jnp.where(lane < HEAD_DIM, x, r)
    return jnp.where(lane >= HEAD_DIM, x, r)


def _gelu_tanh(x):
    c = np.sqrt(2.0 / np.pi).astype(np.float32)
    return x * (0.5 * (1.0 + jnp.tanh(c * (x + 0.044715 * (x * x * x)))))


def _inproj_kernel(xc_ref, xl_ref, mod_ref, win_ref, csc_ref, wsgu_ref, bsgu_ref, lng_ref, lnb_ref,
                   qg_ref, kg_ref, cos_ref, sina_ref, sinb_ref, ones_ref,
                   pq_ref, pool_ref, sgu_ref, q_ref, kd_ref, vd_ref, nk_ref, nv_ref):
    x = jnp.where(pl.program_id(0) < T_CTX // TM, xc_ref[...], xl_ref[...])
    mod = mod_ref[...]
    h = (x * (1.0 + mod[1:2]) + mod[0:1]).astype(bf16)
    proj = _dot(h, win_ref[...])

    a = proj[:, 0:FFT_W].astype(bf16)
    pq_ref[...] = _dot(a, csc_ref[...]).astype(bf16)

    pool_ref[...] = proj[:, FFT_W:FFT_W + POOL_W]

    o = FFT_W + POOL_W
    hgu = _gelu_tanh(proj[:, o:o + 2 * SGU_W])
    u = hgu[:, :SGU_W]
    v = hgu[:, SGU_W:]
    mu = jnp.mean(v, axis=-1, keepdims=True)
    vc = v - mu
    var = jnp.mean(vc * vc, axis=-1, keepdims=True)
    v = vc * lax.rsqrt(var + LN_EPS) * lng_ref[...] + lnb_ref[...]
    lane = lax.broadcasted_iota(i32, (CHUNK, SGU_W), 1)
    head = lane // (SGU_W // SGU_HEADS)
    wcat = wsgu_ref[...]
    for cidx in range(TM // CHUNK):
        rows = slice(cidx * CHUNK, (cidx + 1) * CHUNK)
        vch = v[rows]
        vblk = jnp.concatenate(
            [jnp.where(head == g, vch, 0.0) for g in range(SGU_HEADS)], axis=0).astype(bf16)
        sp = _dot(wcat, vblk) + bsgu_ref[...]
        sgu_ref[rows, :] = (u[rows] * sp).astype(bf16)

    o = o + 2 * SGU_W
    ones_bd = ones_ref[...]
    cos = cos_ref[...]
    sin_a = sina_ref[...]
    sin_b = sinb_ref[...]
    q = _head_rms(proj[:, o:o + ATTN_W], ones_bd, qg_ref[...])
    q = _rope(q, cos, sin_a, sin_b) * np.float32(HEAD_DIM ** -0.5 * np.log2(np.e))
    q_ref[...] = q.astype(bf16)
    o = o + ATTN_W
    k = _head_rms(proj[:, o:o + KV_W], ones_bd[:KV_W, :KV_W], kg_ref[...])
    nk_ref[...] = k
    k = _rope(k, cos[:, :KV_W], sin_a[:, :KV_W], sin_b[:, :KV_W])
    kd_ref[0] = _dup_half(k, True).astype(bf16)
    kd_ref[1] = _dup_half(k, False).astype(bf16)
    o = o + KV_W
    vv = proj[:, o:o + KV_W]
    nv_ref[...] = vv
    vd_ref[0] = _dup_half(vv, True).astype(bf16)
    vd_ref[1] = _dup_half(vv, False).astype(bf16)


def _rope_block(i):
    nlat = DEC_SEQ // TM
    nctx = T_CTX // TM
    return jnp.where(i < nctx, nlat, (i - nctx) % nlat)


def _ctx_tile(wd):
    return pl.BlockSpec((TM, wd), lambda i, *_: (jnp.minimum(i, T_CTX // TM - 1), 0))


def _lat_tile(wd):
    return pl.BlockSpec((TM, wd), lambda i, *_: (jnp.maximum(i - T_CTX // TM, 0), 0))


def _inproj(x_ctx, x_lat, mod, l, w):
    nt = T_ALL // TM
    tile = lambda wd: pl.BlockSpec((TM, wd), lambda i: (i, 0))
    const = lambda shape: pl.BlockSpec(shape, lambda i: (0,) * len(shape))
    rope_spec = pl.BlockSpec((TM, ATTN_W), lambda i: (_rope_block(i), 0))
    return pl.pallas_call(
        _inproj_kernel,
        grid=(nt,),
        in_specs=[
            _ctx_tile(D_MODEL), _lat_tile(D_MODEL),
            pl.BlockSpec((None, None, 6, D_MODEL), lambda i: (l, i // (SEG // TM), 0, 0)),
            pl.BlockSpec((None, D_MODEL, IN_W), lambda i: (l, 0, 0)),
            const((FFT_W, 2 * FFT_W)),
            pl.BlockSpec((None, CHUNK, SGU_HEADS * CHUNK), lambda i: (l, 0, 0)),
            pl.BlockSpec((None, CHUNK, SGU_W), lambda i: (l, 0, 0)),
            pl.BlockSpec((None, 1, SGU_W), lambda i: (l, 0, 0)),
            pl.BlockSpec((None, 1, SGU_W), lambda i: (l, 0, 0)),
            pl.BlockSpec((None, 1, ATTN_W), lambda i: (l, 0, 0)),
            pl.BlockSpec((None, 1, KV_W), lambda i: (l, 0, 0)),
            rope_spec, rope_spec, rope_spec,
            const((ATTN_W, ATTN_W)),
        ],
        out_specs=[
            tile(2 * FFT_W), tile(POOL_W), tile(SGU_W), tile(ATTN_W),
            pl.BlockSpec((N_KV_HEADS, TM, KV_W), lambda i: (0, i, 0)),
            pl.BlockSpec((N_KV_HEADS, TM, KV_W), lambda i: (0, i, 0)),
            tile(KV_W), tile(KV_W),
        ],
        out_shape=[
            jax.ShapeDtypeStruct((T_ALL, 2 * FFT_W), bf16),
            jax.ShapeDtypeStruct((T_ALL, POOL_W), f32),
            jax.ShapeDtypeStruct((T_ALL, SGU_W), bf16),
            jax.ShapeDtypeStruct((T_ALL, ATTN_W), bf16),
            jax.ShapeDtypeStruct((N_KV_HEADS, T_ALL, KV_W), bf16),
            jax.ShapeDtypeStruct((N_KV_HEADS, T_ALL, KV_W), bf16),
            jax.ShapeDtypeStruct((T_ALL, KV_W), f32),
            jax.ShapeDtypeStruct((T_ALL, KV_W), f32),
        ],
        compiler_params=_cparams(("arbitrary",)),
        name="inproj",
    )(x_ctx, x_lat, mod, w["w_in"], w["csc"], w["w_sgu"], w["b_sgu"], w["sgu_ln_g"], w["sgu_ln_b"],
      w["q_norm_g"], w["k_norm_g"], w["rope_cos"], w["rope_sin_a"], w["rope_sin_b"], w["ones_bd"])


def _pool_kernel(prev_ref, cur_ref, next_ref, wp_ref, scale_ref, o_ref):
    i = pl.program_id(0)
    n = jnp.where(i < T_CTX // POOL_TB, SEQ, DEC_SEQ)
    hl = POOL_HALO
    ext = jnp.concatenate([prev_ref[POOL_TB - hl:, :], cur_ref[...], next_ref[:hl, :]], axis=0)
    rows = POOL_TB + 2 * hl
    r = lax.broadcasted_iota(i32, (rows, 1), 0)
    pos = (i * POOL_TB + r - hl) & (n - 1)

    def back(a, s):
        return jnp.where(pos >= s, pltpu.roll(a, s, 0), 0.0)

    def fwd(a, s):
        return jnp.where(pos + s < n, pltpu.roll(a, rows - s, 0), 0.0)

    bsum = [back(ext, 1)]
    fsum = [ext]
    for k in range(3):
        s = 1 << k
        bsum.append(bsum[k] + back(bsum[k], s))
        fsum.append(fsum[k] + fwd(fsum[k], s))
    lane = lax.broadcasted_iota(i32, (1, POOL_W), 1)
    grp = lane // POOL_GROUP
    win = bsum[3] + fsum[3]
    half = jnp.full((1, POOL_W), POOL_WINDOWS[3] // 2, i32)
    for g in (2, 1, 0):
        win = jnp.where(grp == g, bsum[g] + fsum[g], win)
        half = jnp.where(grp == g, POOL_WINDOWS[g] // 2, half)
    cnt = (jnp.minimum(pos + half, n) - jnp.maximum(pos - half, 0)).astype(f32)
    y = (win / cnt - ext)[hl:hl + POOL_TB]
    o_ref[...] = (_dot(y.astype(bf16), wp_ref[...]) * scale_ref[...]).astype(bf16)


def _pool(p, l, w):
    nt = T_ALL // POOL_TB
    blk = lambda f: pl.BlockSpec((POOL_TB, POOL_W), lambda i: (f(i), 0))
    return pl.pallas_call(
        _pool_kernel,
        grid=(nt,),
        in_specs=[
            blk(lambda i: jnp.maximum(i - 1, 0)), blk(lambda i: i),
            blk(lambda i: jnp.minimum(i + 1, nt - 1)),
            pl.BlockSpec((None, POOL_W, POOL_W), lambda i: (l, 0, 0)),
            pl.BlockSpec((None, 1, POOL_W), lambda i: (l, 0, 0)),
        ],
        out_specs=blk(lambda i: i),
        out_shape=jax.ShapeDtypeStruct((T_ALL, POOL_W), bf16),
        compiler_params=_cparams(("arbitrary",)),
        name="pool",
    )(p, p, p, w["w_pool_bd"], w["pool_scale"])


def _seqdft_kernel(*refs, n, nseq):
    m_ref, pq_refs, w_ref, o_ref = refs[0], refs[1:-2], refs[-2], refs[-1]
    per_blk = SEG // n
    for b in range(nseq):
        @pl.when(pl.program_id(1) == b)
        def _(b=b):
            pq_ref = pq_refs[b // per_blk]
            r0 = (b % per_blk) * n
            f = (_dot(m_ref[:, :n], pq_ref[r0:r0 + n, :FFT_W])
                 + _dot(m_ref[:, n:], pq_ref[r0:r0 + n, FFT_W:]))
            o_ref[...] = _dot(f.astype(bf16), w_ref[...]).astype(bf16)


def _seqdft(pq, m, l, w, *, n, tr, nseq, row0):
    nr = n // tr
    nblk = nseq * n // SEG
    pq_specs = [pl.BlockSpec((SEG, 2 * FFT_W), lambda i, b, j=j: (row0 // SEG + j, 0))
                for j in range(nblk)]
    return pl.pallas_call(
        functools.partial(_seqdft_kernel, n=n, nseq=nseq),
        grid=(nr, nseq),
        in_specs=[pl.BlockSpec((tr, 2 * n), lambda i, b: (i, 0))] + pq_specs
        + [pl.BlockSpec((None, FFT_W, FFT_W), lambda i, b: (l, 0, 0))],
        out_specs=pl.BlockSpec((tr, FFT_W), lambda i, b: (b * nr + i, 0)),
        out_shape=jax.ShapeDtypeStruct((nseq * n, FFT_W), bf16),
        compiler_params=_cparams(("arbitrary", "arbitrary")),
        name="seqdft_%d" % n,
    )(m, *([pq] * nblk), w["w_fft"])


def _attn_kernel(*refs, has_cache):
    if has_cache:
        q_ref, k_ref, v_ref, kc_ref, vc_ref, o_ref = refs
    else:
        q_ref, k_ref, v_ref, o_ref = refs
    q = q_ref[...]
    tq = q.shape[0]
    lane = lax.broadcasted_iota(i32, q.shape, 1)
    zero = jnp.zeros_like(q)
    qs = jnp.concatenate([jnp.where(lane < HEAD_DIM, q, zero),
                          jnp.where(lane >= HEAD_DIM, q, zero)], axis=0)
    nt = (((1,), (1,)), ((), ()))
    n = k_ref.shape[0]
    chunk = min(n, ATT_CHUNK)
    parts = [(k_ref, v_ref, c * chunk, chunk) for c in range(n // chunk)]
    if has_cache:
        parts = [(kc_ref, vc_ref, 0, PAST_LEN)] + parts
    m = jnp.full((2 * tq, 1), -jnp.inf, f32)
    den = jnp.zeros((2 * tq, 1), f32)
    acc = jnp.zeros((2 * tq, 2 * HEAD_DIM), f32)
    for kr, vr, off, size in parts:
        s = lax.dot_general(qs, kr[off:off + size, :], nt, preferred_element_type=f32)
        m_new = jnp.maximum(m, jnp.max(s, axis=-1, keepdims=True))
        alpha = jnp.exp2(m - m_new)
        p = jnp.exp2(s - m_new).astype(bf16)
        den = alpha * den + jnp.sum(p.astype(f32), axis=-1, keepdims=True)
        acc = alpha * acc + _dot(p, vr[off:off + size, :])
        m = m_new
    out = acc / den
    o_ref[...] = jnp.where(lane < HEAD_DIM, out[:tq], out[tq:]).astype(bf16)


def _attention(q, kd, vd, cache, *, n, tq, nseq, row0):
    nq = n // tq
    b0 = row0 // n
    q0 = row0 // tq
    in_specs = [
        pl.BlockSpec((tq, 2 * HEAD_DIM), lambda b, h, i: (q0 + b * nq + i, h)),
        pl.BlockSpec((None, n, KV_W), lambda b, h, i: (h, b0 + b, 0)),
        pl.BlockSpec((None, n, KV_W), lambda b, h, i: (h, b0 + b, 0)),
    ]
    args = [q, kd, vd]
    if cache is not None:
        cspec = pl.BlockSpec((None, None, PAST_LEN, KV_W), lambda b, h, i: (h, b, 0, 0))
        in_specs += [cspec, cspec]
        args += list(cache)
    return pl.pallas_call(
        functools.partial(_attn_kernel, has_cache=cache is not None),
        grid=(nseq, N_KV_HEADS, nq),
        in_specs=in_specs,
        out_specs=pl.BlockSpec((tq, 2 * HEAD_DIM), lambda b, h, i: (b * nq + i, h)),
        out_shape=jax.ShapeDtypeStruct((nseq * n, ATTN_W), bf16),
        compiler_params=_cparams(("arbitrary", "arbitrary", "arbitrary")),
        name="attention_%d" % n,
    )(*args)


def _layer_norm(x, g, b):
    mu = jnp.mean(x, axis=-1, keepdims=True)
    xc = x - mu
    var = jnp.mean(xc * xc, axis=-1, keepdims=True)
    return xc * lax.rsqrt(var + LN_EPS) * g + b


def _outproj_kernel(xc_ref, xl_ref, mod_ref, fc_ref, fl_ref, p_ref, s_ref, ac_ref, al_ref, wout_ref,
                    g_ref, b_ref, wr_ref, br_ref, tril_ref,
                    x1_ref, h2_ref, route_ref, cnt_ref, carry_ref):
    i = pl.program_id(0)

    @pl.when(i == 0)
    def _():
        carry_ref[...] = jnp.zeros_like(carry_ref)

    mod = mod_ref[...]
    is_ctx = i < T_CTX // TM
    f_mix = jnp.where(is_ctx, fc_ref[...], fl_ref[...])
    a_mix = jnp.where(is_ctx, ac_ref[...], al_ref[...])
    mix = _dot(jnp.concatenate([f_mix, p_ref[...], s_ref[...], a_mix], axis=1), wout_ref[...])
    x = jnp.where(is_ctx, xc_ref[...], xl_ref[...])
    x1 = _layer_norm(DEEPNORM_ALPHA * x + mod[2:3] * mix, g_ref[...], b_ref[...])
    x1_ref[...] = x1
    h2 = x1 * (1.0 + mod[4:5]) + mod[3:4]
    h2_ref[...] = h2

    h_hi, h_lo = _split_hi_lo(h2)
    hw = _dot(h_hi, wr_ref[...])
    logits = hw[:, :128] + hw[:, 128:] + _dot(h_lo, wr_ref[:, :128]) + br_ref[...]
    lane = lax.broadcasted_iota(i32, logits.shape, 1).astype(f32)
    neg = jnp.float32(-jnp.inf)
    big = jnp.float32(1 << 20)
    gl = jnp.where(lane < N_GROUPS, logits, neg)
    gmax = jnp.max(gl, axis=-1, keepdims=True)
    gsel = jnp.min(jnp.where(gl == gmax, lane, big), axis=-1, keepdims=True)
    pg = 1.0 / jnp.sum(jnp.exp(gl - gmax), axis=-1, keepdims=True)
    e_lo = ROUTE_E0 + gsel * EXPERTS_PER_GROUP
    el = jnp.where((lane >= e_lo) & (lane < e_lo + EXPERTS_PER_GROUP), logits, neg)
    v1 = jnp.max(el, axis=-1, keepdims=True)
    i1 = jnp.min(jnp.where(el == v1, lane, big), axis=-1, keepdims=True)
    el2 = jnp.where(lane == i1, neg, el)
    v2 = jnp.max(el2, axis=-1, keepdims=True)
    i2 = jnp.min(jnp.where(el2 == v2, lane, big), axis=-1, keepdims=True)
    e2 = jnp.exp(v2 - v1)
    w1 = pg / (1.0 + e2)
    w2 = pg * e2 / (1.0 + e2)
    oh1 = lane == i1
    oh2 = lane == i2
    oh = jnp.where(oh1 | oh2, 1.0, 0.0)
    prefix = _dot(tril_ref[...], oh.astype(bf16)) + carry_ref[0:1, :]
    rank1 = jnp.sum(jnp.where(oh1, prefix, 0.0), axis=-1, keepdims=True)
    rank2 = jnp.sum(jnp.where(oh2, prefix, 0.0), axis=-1, keepdims=True)
    carry = carry_ref[...] + jnp.sum(oh, axis=0, keepdims=True)
    carry_ref[...] = carry
    cnt_ref[...] = carry
    cols = (i1 - ROUTE_E0, i2 - ROUTE_E0, w1, w2, rank1, rank2)
    route = jnp.zeros_like(logits)
    for j, col in enumerate(cols):
        route = jnp.where(lane == j, col, route)
    route_ref[...] = route


def _outproj(x_ctx, x_lat, mod, fo_ctx, fo_lat, po, so, ao_ctx, ao_lat, l, w):
    nt = T_ALL // TM
    tile = lambda wd: pl.BlockSpec((TM, wd), lambda i: (i, 0))
    vec = lambda wd: pl.BlockSpec((None, 1, wd), lambda i: (l, 0, 0))
    return pl.pallas_call(
        _outproj_kernel,
        grid=(nt,),
        in_specs=[
            _ctx_tile(D_MODEL), _lat_tile(D_MODEL),
            pl.BlockSpec((None, None, 6, D_MODEL), lambda i: (l, i // (SEG // TM), 0, 0)),
            _ctx_tile(FFT_W), _lat_tile(FFT_W), tile(POOL_W), tile(SGU_W),
            _ctx_tile(ATTN_W), _lat_tile(ATTN_W),
            pl.BlockSpec((None, D_MODEL, D_MODEL), lambda i: (l, 0, 0)),
            vec(D_MODEL), vec(D_MODEL),
            pl.BlockSpec((None, D_MODEL, 256), lambda i: (l, 0, 0)),
            vec(128),
            pl.BlockSpec((TM, TM), lambda i: (0, 0)),
        ],
        out_specs=[tile(D_MODEL), tile(D_MODEL), tile(128), pl.BlockSpec((8, 128), lambda i: (0, 0))],
        out_shape=[
            jax.ShapeDtypeStruct((T_ALL, D_MODEL), f32),
            jax.ShapeDtypeStruct((T_ALL, D_MODEL), f32),
            jax.ShapeDtypeStruct((T_ALL, 128), f32),
            jax.ShapeDtypeStruct((8, 128), f32),
        ],
        scratch_shapes=[pltpu.VMEM((8, 128), f32)],
        compiler_params=_cparams(("arbitrary",)),
        name="outproj",
    )(x_ctx, x_lat, mod, fo_ctx, fo_lat, po, so, ao_ctx, ao_lat,
      w["w_out"], w["ln1_g"], w["ln1_b"], w["w_r"], w["b_r"],
      w["tril"])


def _plan_kernel(route_ref, cnt_ref, pos_ref, meta_ref):
    lane = lax.broadcasted_iota(i32, (8, 128), 1)
    sub = lax.broadcasted_iota(i32, (8, 128), 0)
    cnt = cnt_ref[...]
    is_e = (lane >= ROUTE_E0) & (lane < ROUTE_E0 + N_EXPERTS)
    tiles = jnp.where(is_e, jnp.floor((cnt + (MOE_TM - 1.0)) * (1.0 / MOE_TM)), 0.0)
    cum = tiles
    for s in (1, 2, 4, 8, 16):
        cum = cum + jnp.where(lane >= s, pltpu.roll(cum, s, 1), 0.0)
    pstart = (cum - tiles) * MOE_TM
    nused = jnp.max(cum, axis=-1, keepdims=True)
    fill = jnp.where(is_e & (cnt != tiles * MOE_TM), pstart + (tiles - 1.0) * MOE_TM, -1.0)
    meta = jnp.where(sub == 0, cnt, jnp.where(sub == 1, nused, jnp.where(sub == 2, fill, 0.0)))
    meta_ref[...] = meta.astype(i32)

    r = route_ref[...]
    lane_t = lax.broadcasted_iota(i32, r.shape, 1).astype(f32)
    ps = pstart[0:1, :]

    def dest(ecol, rcol):
        hit = lane_t == (r[:, ecol:ecol + 1] + ROUTE_E0)
        return jnp.sum(jnp.where(hit, ps, 0.0), axis=-1, keepdims=True) + r[:, rcol:rcol + 1]

    pos = jnp.where(lane_t == 0, dest(0, 4), jnp.where(lane_t == 1, dest(1, 5), 0.0))
    pos_ref[...] = pos.astype(i32)


def _plan(route, cnt):
    return pl.pallas_call(
        _plan_kernel,
        grid=(T_ALL // PLAN_TB,),
        in_specs=[pl.BlockSpec((PLAN_TB, 128), lambda i: (i, 0)),
                  pl.BlockSpec((8, 128), lambda i: (0, 0))],
        out_specs=[pl.BlockSpec((PLAN_TB, 128), lambda i: (i, 0)),
                   pl.BlockSpec((8, 128), lambda i: (0, 0))],
        out_shape=[jax.ShapeDtypeStruct((T_ALL, 128), i32), jax.ShapeDtypeStruct((8, 128), i32)],
        compiler_params=_cparams(("arbitrary",)),
        name="plan",
    )(route, cnt)


def _dispatch_kernel(pos_ref, fill_ref, nused_ref, h_ref, xs_ref, zero_ref, sem, fill_sem):
    i = pl.program_id(0)

    def tile_fill(row0):
        return pltpu.make_async_copy(zero_ref, xs_ref.at[pl.ds(pl.multiple_of(row0, MOE_TM), MOE_TM)],
                                     fill_sem)

    @pl.when(i == 0)
    def _():
        zero_ref[...] = jnp.zeros_like(zero_ref)

        def start(e, c):
            @pl.when(fill_ref[e] >= 0)
            def _():
                tile_fill(jnp.maximum(fill_ref[e], 0)).start()
            return c

        def wait(e, c):
            @pl.when(fill_ref[e] >= 0)
            def _():
                tile_fill(jnp.maximum(fill_ref[e], 0)).wait()
            return c

        def start_tail(t, c):
            tile_fill(t * MOE_TM).start()
            return c

        def wait_tail(t, c):
            tile_fill(t * MOE_TM).wait()
            return c

        lax.fori_loop(0, N_EXPERTS, start, 0)
        lax.fori_loop(nused_ref[0], MOE_NT, start_tail, 0)
        lax.fori_loop(0, N_EXPERTS, wait, 0)
        lax.fori_loop(nused_ref[0], MOE_NT, wait_tail, 0)

    for k in range(2):
        base = k * T_ALL + i * DISP_TB
        for j in range(DISP_TB):
            row = pos_ref[base + j]
            pltpu.make_async_copy(h_ref.at[pl.ds(j, 1)], xs_ref.at[pl.ds(row, 1)], sem).start()
    for k in range(2):
        pltpu.make_async_copy(h_ref, xs_ref.at[pl.ds(0, DISP_TB)], sem).wait()


def _dispatch(pos_flat, fill, nused, h2):
    grid_spec = pltpu.PrefetchScalarGridSpec(
        num_scalar_prefetch=3,
        grid=(T_ALL // DISP_TB,),
        in_specs=[pl.BlockSpec((DISP_TB, D_MODEL), lambda i, p, f, nu: (i, 0))],
        out_specs=pl.BlockSpec(memory_space=pl.ANY),
        scratch_shapes=[pltpu.VMEM((MOE_TM, D_MODEL), f32),
                        pltpu.SemaphoreType.DMA(()), pltpu.SemaphoreType.DMA(())],
    )
    return pl.pallas_call(
        _dispatch_kernel,
        grid_spec=grid_spec,
        out_shape=jax.ShapeDtypeStruct((MOE_NT * MOE_TM, D_MODEL), f32),
        compiler_params=_cparams(("arbitrary",)),
        name="dispatch",
    )(pos_flat, fill, nused, h2)


def _experts_kernel(cnt_ref, nused_ref, xs_ref, wg_hbm, wu_hbm, wd_hbm, ys_ref,
                    wg_f, wu_f, wd_f, wg_b, wu_b, wd_b, st, wsem, *, layer):
    i = pl.program_id(0)
    nused = nused_ref[0]
    NXT, NSLOT, LEFT = 0, 1, 2

    def w_copies(e, slot):
        return (pltpu.make_async_copy(wg_hbm.at[layer, e], wg_f.at[slot], wsem.at[slot, 0]),
                pltpu.make_async_copy(wu_hbm.at[layer, e], wu_f.at[slot], wsem.at[slot, 1]),
                pltpu.make_async_copy(wd_hbm.at[layer, e], wd_f.at[slot], wsem.at[slot, 2]))

    def next_nonempty(e):
        return lax.while_loop(
            lambda v: (v < N_EXPERTS) & (cnt_ref[jnp.minimum(v, N_EXPERTS - 1)] == 0),
            lambda v: v + 1, e)

    @pl.when(i == 0)
    def _():
        e0 = next_nonempty(jnp.int32(0))
        for c in w_copies(e0, 0):
            c.start()
        st[NXT] = e0
        st[NSLOT] = 0
        st[LEFT] = 0

    @pl.when(i < nused)
    def _():
        @pl.when(st[LEFT] == 0)
        def _():
            e = st[NXT]
            slot = st[NSLOT]
            for c in w_copies(e, slot):
                c.wait()
            e2 = next_nonempty(e + 1)

            @pl.when(e2 < N_EXPERTS)
            def _():
                for c in w_copies(e2, 1 - slot):
                    c.start()

            st[NXT] = e2
            st[NSLOT] = 1 - slot
            st[LEFT] = (cnt_ref[e] + (MOE_TM - 1)) // MOE_TM
            wg_b[...] = wg_f[slot].astype(bf16)
            wu_b[...] = wu_f[slot].astype(bf16)
            wd_b[...] = wd_f[slot].astype(bf16)

        x = xs_ref[...].astype(bf16)
        hg = _dot(x, wg_b[...])
        hu = _dot(x, wu_b[...])
        act = (hg * jax.nn.sigmoid(hg)) * hu
        ys_ref[...] = _dot(act.astype(bf16), wd_b[...])
        st[LEFT] = st[LEFT] - 1

    @pl.when(i >= nused)
    def _():
        ys_ref[...] = jnp.zeros_like(ys_ref)


def _experts(counts, nused, xs, l, w_gate, w_up, w_down):
    hbm = pl.BlockSpec(memory_space=pl.ANY)
    grid_spec = pltpu.PrefetchScalarGridSpec(
        num_scalar_prefetch=2,
        grid=(MOE_NT,),
        in_specs=[pl.BlockSpec((MOE_TM, D_MODEL), lambda i, c, nu: (jnp.minimum(i, nu[0] - 1), 0)),
                  hbm, hbm, hbm],
        out_specs=pl.BlockSpec((MOE_TM, D_MODEL), lambda i, c, nu: (i, 0)),
        scratch_shapes=[
            pltpu.VMEM((2, D_MODEL, EXPERT_FF), f32),
            pltpu.VMEM((2, D_MODEL, EXPERT_FF), f32),
            pltpu.VMEM((2, EXPERT_FF, D_MODEL), f32),
            pltpu.VMEM((D_MODEL, EXPERT_FF), bf16),
            pltpu.VMEM((D_MODEL, EXPERT_FF), bf16),
            pltpu.VMEM((EXPERT_FF, D_MODEL), bf16),
            pltpu.SMEM((4,), i32),
            pltpu.SemaphoreType.DMA((2, 3)),
        ],
    )
    return pl.pallas_call(
        functools.partial(_experts_kernel, layer=l),
        grid_spec=grid_spec,
        out_shape=jax.ShapeDtypeStruct((MOE_NT * MOE_TM, D_MODEL), f32),
        compiler_params=_cparams(("arbitrary",)),
        name="experts",
    )(counts, nused, xs, w_gate, w_up, w_down)


def _combine_kernel(pos_ref, x1_ref, mod_ref, route_ref, ys_hbm, g_ref, b_ref, oc_ref, ol_ref, ybuf, sem):
    i = pl.program_id(0)
    nt = pl.num_programs(0) - 1

    @pl.when(i < nt)
    def _():
        slot = i % 2
        for k in range(2):
            base = k * T_ALL + i * TM
            for j in range(TM):
                pltpu.make_async_copy(ys_hbm.at[pl.ds(pos_ref[base + j], 1)],
                                      ybuf.at[slot, k, pl.ds(j, 1)], sem.at[slot]).start()

    @pl.when(i >= 1)
    def _():
        slot = (i - 1) % 2
        for k in range(2):
            pltpu.make_async_copy(ys_hbm.at[pl.ds(0, TM)], ybuf.at[slot, k], sem.at[slot]).wait()
        route = route_ref[...]
        mod = mod_ref[...]
        moe = route[:, 2:3] * ybuf[slot, 0] + route[:, 3:4] * ybuf[slot, 1]
        y = _layer_norm(DEEPNORM_ALPHA * x1_ref[...] + mod[5:6] * moe, g_ref[...], b_ref[...])

        @pl.when(i - 1 < T_CTX // TM)
        def _():
            oc_ref[...] = y

        @pl.when(i - 1 >= T_CTX // TM)
        def _():
            ol_ref[...] = y


def _combine(pos_flat, x1, mod, route, ys, l, w):
    nt = T_ALL // TM
    vec = pl.BlockSpec((None, 1, D_MODEL), lambda i, p: (l, 0, 0))
    nctx = T_CTX // TM
    prev = lambda i: jnp.maximum(i - 1, 0)
    grid_spec = pltpu.PrefetchScalarGridSpec(
        num_scalar_prefetch=1,
        grid=(nt + 1,),
        in_specs=[
            pl.BlockSpec((TM, D_MODEL), lambda i, p: (prev(i), 0)),
            pl.BlockSpec((None, None, 6, D_MODEL), lambda i, p: (l, prev(i) // (SEG // TM), 0, 0)),
            pl.BlockSpec((TM, 128), lambda i, p: (prev(i), 0)),
            pl.BlockSpec(memory_space=pl.ANY),
            vec, vec,
        ],
        out_specs=[pl.BlockSpec((TM, D_MODEL), lambda i, p: (jnp.minimum(prev(i), nctx - 1), 0)),
                   pl.BlockSpec((TM, D_MODEL), lambda i, p: (jnp.maximum(prev(i) - nctx, 0), 0))],
        scratch_shapes=[pltpu.VMEM((2, 2, TM, D_MODEL), f32), pltpu.SemaphoreType.DMA((2,))],
    )
    return pl.pallas_call(
        _combine_kernel,
        grid_spec=grid_spec,
        out_shape=[jax.ShapeDtypeStruct((T_CTX, D_MODEL), f32),
                   jax.ShapeDtypeStruct((T_LAT, D_MODEL), f32)],
        compiler_params=_cparams(("arbitrary",)),
        name="combine",
    )(pos_flat, x1, mod, route, ys, w["ln2_g"], w["ln2_b"])


def _dft_cos_sin(n, scale):
    k = jnp.arange(n, dtype=i32)
    ang = ((k[:, None] * k[None, :]) % n).astype(f32) * np.float32(2.0 * np.pi / n)
    return jnp.cos(ang) * scale, jnp.sin(ang) * scale


def _seq_dft_matrix(n):
    g = min(DFT_SPLIT, n)
    j = jnp.arange(n, dtype=i32)[None, :]
    k1 = jnp.arange(n // g, dtype=i32)[:, None]
    k2 = jnp.arange(g, dtype=i32)[:, None]
    ang_a = ((k1 * j) % (n // g)).astype(f32) * np.float32(2.0 * np.pi * g / n)
    ang_b = ((k2 * j) % n).astype(f32) * np.float32(2.0 * np.pi / n)
    scale = np.float32(n ** -0.5)
    ca, sa = jnp.cos(ang_a), jnp.sin(ang_a)
    cb, sb = jnp.cos(ang_b) * scale, jnp.sin(ang_b) * scale
    ca2 = jnp.concatenate([ca, ca], axis=1)[:, None, :]
    sa2 = jnp.concatenate([sa, sa], axis=1)[:, None, :]
    cb2 = jnp.concatenate([cb, -sb], axis=1)[None, :, :]
    sb2 = jnp.concatenate([sb, cb], axis=1)[None, :, :]
    return (ca2 * cb2 - sa2 * sb2).astype(bf16).reshape(n, 2 * n)


def _rope_tables():
    rows = DEC_SEQ // GRID_W
    row = jnp.repeat(jnp.arange(rows), GRID_W).astype(f32)
    col = jnp.tile(jnp.arange(GRID_W), rows).astype(f32)
    n_freq = HEAD_DIM // 4
    inv = ROPE_THETA ** (-jnp.arange(n_freq, dtype=f32) / n_freq)
    ar = row[:, None] * inv
    ac = col[:, None] * inv
    ang = jnp.concatenate([ar, ar, ac, ac], axis=-1)
    cos = jnp.tile(jnp.cos(ang), (1, N_HEADS))
    sin = jnp.tile(jnp.sin(ang), (1, N_HEADS))
    first = (jnp.arange(ATTN_W) % (HEAD_DIM // 2)) < n_freq
    sin_a = jnp.where(first[None, :], -sin, 0.0)
    sin_b = jnp.where(first[None, :], 0.0, sin)
    ident = jnp.zeros((TM, ATTN_W), f32)
    return (jnp.concatenate([cos, ident + 1.0], axis=0),
            jnp.concatenate([sin_a, ident], axis=0),
            jnp.concatenate([sin_b, ident], axis=0))


def _dup_cache(cache):
    c = jnp.transpose(cache, (1, 3, 0, 2, 4))
    return jnp.concatenate([c, c], axis=-1).astype(bf16)


def kernel(x_prompt, x_sample, cache_k, cache_v, c, c_ctx, w_mod, b_mod, w_in, w_fft, w_pool, pool_scale, sgu_ln_g, sgu_ln_b, w_sgu, b_sgu, q_norm_g, k_norm_g, w_out, ln1_g, ln1_b, w_router_group, b_router_group, w_router_expert, b_router_expert, w_gate, w_up, w_down, ln2_g, ln2_b):
    L = DEPTH
    x_ctx = x_prompt.reshape(T_CTX, D_MODEL)
    x_lat = x_sample.reshape(T_LAT, D_MODEL)

    cond8 = jnp.concatenate([c_ctx[None, :], c, jnp.zeros((8 - 1 - DEC_BATCH, D_MODEL), f32)], axis=0)
    mod = _modulation(cond8, w_mod, b_mod)[:, :N_SEG].reshape(L, N_SEG, 6, D_MODEL)

    cc, sc = _dft_cos_sin(FFT_W, np.float32(FFT_W ** -0.5))
    rope_cos, rope_sin_a, rope_sin_b = _rope_tables()
    head_id = jnp.arange(ATTN_W) // HEAD_DIM
    eye_g = jnp.eye(len(POOL_WINDOWS), dtype=f32)
    w_r = jnp.zeros((L, D_MODEL, 128), f32)
    w_r = w_r.at[:, :, :N_GROUPS].set(w_router_group).at[:, :, ROUTE_E0:ROUTE_E0 + N_EXPERTS].set(w_router_expert)
    b_r = jnp.zeros((L, 1, 128), f32)
    b_r = b_r.at[:, 0, :N_GROUPS].set(b_router_group).at[:, 0, ROUTE_E0:ROUTE_E0 + N_EXPERTS].set(b_router_expert)
    w_r_hi, w_r_lo = _split_hi_lo(w_r)
    w = {
        "w_in": w_in.astype(bf16),
        "csc": jnp.concatenate([cc, sc], axis=1).astype(bf16),
        "w_sgu": jnp.transpose(w_sgu, (0, 2, 1, 3)).reshape(L, CHUNK, SGU_HEADS * CHUNK).astype(bf16),
        "b_sgu": jnp.repeat(jnp.transpose(b_sgu, (0, 2, 1)), SGU_W // SGU_HEADS, axis=2),
        "sgu_ln_g": sgu_ln_g.reshape(L, 1, SGU_W),
        "sgu_ln_b": sgu_ln_b.reshape(L, 1, SGU_W),
        "q_norm_g": jnp.tile(q_norm_g, (1, N_HEADS)).reshape(L, 1, ATTN_W),
        "k_norm_g": jnp.tile(k_norm_g, (1, N_KV_HEADS)).reshape(L, 1, KV_W),
        "rope_cos": rope_cos, "rope_sin_a": rope_sin_a, "rope_sin_b": rope_sin_b,
        "ones_bd": (head_id[:, None] == head_id[None, :]).astype(bf16),
        "w_pool_bd": jnp.einsum("lgcd,gh->lgchd", w_pool, eye_g).reshape(L, POOL_W, POOL_W).astype(bf16),
        "pool_scale": pool_scale.reshape(L, 1, POOL_W),
        "w_fft": w_fft.astype(bf16),
        "w_out": w_out.astype(bf16),
        "ln1_g": ln1_g.reshape(L, 1, D_MODEL), "ln1_b": ln1_b.reshape(L, 1, D_MODEL),
        "ln2_g": ln2_g.reshape(L, 1, D_MODEL), "ln2_b": ln2_b.reshape(L, 1, D_MODEL),
        "w_r": jnp.concatenate([w_r_hi, w_r_lo], axis=-1), "b_r": b_r,
        "tril": (jnp.arange(TM)[:, None] > jnp.arange(TM)[None, :]).astype(bf16),
    }
    m_ctx = _seq_dft_matrix(SEQ)
    m_lat = _seq_dft_matrix(DEC_SEQ)
    kc_all = _dup_cache(cache_k)
    vc_all = _dup_cache(cache_v)

    new_k, new_v = [], []
    for l in range(L):
        pq, praw, sgu, q, kd, vd, nk, nv = _inproj(x_ctx, x_lat, mod, l, w)
        new_k.append(nk[:T_CTX].reshape(BATCH, SEQ, N_KV_HEADS, HEAD_DIM))
        new_v.append(nv[:T_CTX].reshape(BATCH, SEQ, N_KV_HEADS, HEAD_DIM))
        po = _pool(praw, l, w)
        fo_ctx = _seqdft(pq, m_ctx, l, w, n=SEQ, tr=SEQ, nseq=BATCH, row0=0)
        fo_lat = _seqdft(pq, m_lat, l, w, n=DEC_SEQ, tr=FFT_TR, nseq=DEC_BATCH, row0=T_CTX)
        ao_ctx = _attention(q, kd, vd, None, n=SEQ, tq=SEQ, nseq=BATCH, row0=0)
        ao_lat = _attention(q, kd, vd, (kc_all[l], vc_all[l]), n=DEC_SEQ, tq=ATT_TQ, nseq=DEC_BATCH,
                            row0=T_CTX)
        x1, h2, route, cnt = _outproj(x_ctx, x_lat, mod, fo_ctx, fo_lat, po, sgu, ao_ctx, ao_lat, l, w)
        pos_slab, meta = _plan(route, cnt)
        pos_flat = pos_slab[:, :2].T.reshape(-1)
        counts = meta[0, ROUTE_E0:ROUTE_E0 + N_EXPERTS]
        nused = meta[1, :1]
        fill = meta[2, ROUTE_E0:ROUTE_E0 + N_EXPERTS]
        xs = _dispatch(pos_flat, fill, nused, h2)
        ys = _experts(counts, nused, xs, l, w_gate, w_up, w_down)
        x_ctx, x_lat = _combine(pos_flat, x1, mod, route, ys, l, w)

    y_prompt = x_ctx.reshape(BATCH, SEQ, D_MODEL)
    y_sample = x_lat.reshape(DEC_BATCH, DEC_SEQ, D_MODEL)
    return (y_prompt, y_sample, jnp.stack(new_k, axis=1), jnp.stack(new_v, axis=1))
```

```python
import functools

import numpy as np
import jax
import jax.numpy as jnp
from jax import lax
from jax.experimental import pallas as pl
from jax.experimental.pallas import tpu as pltpu

f32 = jnp.float32
bf16 = jnp.bfloat16
i32 = jnp.int32

D_MODEL = 1024
BATCH = 16
SEQ = 256
DEPTH = 4
DEC_BATCH = 2
DEC_SEQ = 4096
PAST_LEN = 512
GRID_W = 64
FFT_W = 256
POOL_W = 256
POOL_WINDOWS = (2, 4, 8, 16)
POOL_GROUP = 64
SGU_W = 256
SGU_HEADS = 4
CHUNK = 128
HEAD_DIM = 64
ATTN_W = 256
N_HEADS = 4
N_KV_HEADS = 2
KV_W = 128
IN_W = 1536
ROPE_THETA = 10000.0
N_GROUPS = 4
EXPERTS_PER_GROUP = 8
N_EXPERTS = 32
EXPERT_FF = 512
DEEPNORM_ALPHA = float((2 * DEPTH) ** 0.25)
LN_EPS = 1e-5
RMS_EPS = 1e-6

T_CTX = BATCH * SEQ
T_LAT = DEC_BATCH * DEC_SEQ
T_ALL = T_CTX + T_LAT
SEG = 4096
N_SEG = T_ALL // SEG

TM = 512
POOL_TB = 512
POOL_HALO = 8
FFT_TR = 512
ATT_TQ = 512
ATT_CHUNK = 1024
DFT_SPLIT = 64
MOE_TM = 256
ROW_CHUNK = 8
MOE_ROWS = 2 * T_ALL
MOE_PAD_ROWS = (T_ALL // TM) * N_EXPERTS * (ROW_CHUNK - 1)
MOE_NT = -(-(MOE_ROWS + MOE_PAD_ROWS) // MOE_TM) + N_EXPERTS
DISP_ROWS = 2 * TM + N_EXPERTS * ROW_CHUNK
PLAN_TB = 2048
ROUTE_E0 = 32
VMEM_LIMIT = 56 * 1024 * 1024


def _cparams(sem):
    return pltpu.CompilerParams(dimension_semantics=sem, vmem_limit_bytes=VMEM_LIMIT)


def _split_hi_lo(a):
    hi = a.astype(bf16)
    lo = (a - hi.astype(f32)).astype(bf16)
    return hi, lo


def _dot(a, b):
    return jnp.dot(a, b, preferred_element_type=f32)


def _mod_kernel(c_ref, w_ref, b_ref, o_ref):
    c = c_ref[...]
    s = c * jax.nn.sigmoid(c)
    s_hi, s_lo = _split_hi_lo(s)
    w_hi, w_lo = _split_hi_lo(w_ref[...])
    o_ref[...] = _dot(s_hi, w_hi) + _dot(s_hi, w_lo) + _dot(s_lo, w_hi) + b_ref[...]


def _modulation(cond8, w_mod, b_mod):
    tn = 1536
    return pl.pallas_call(
        _mod_kernel,
        grid=(DEPTH, 6 * D_MODEL // tn),
        in_specs=[
            pl.BlockSpec((8, D_MODEL), lambda l, j: (0, 0)),
            pl.BlockSpec((None, D_MODEL, tn), lambda l, j: (l, 0, j)),
            pl.BlockSpec((None, 1, tn), lambda l, j: (l, 0, j)),
        ],
        out_specs=pl.BlockSpec((None, 8, tn), lambda l, j: (l, 0, j)),
        out_shape=jax.ShapeDtypeStruct((DEPTH, 8, 6 * D_MODEL), f32),
        compiler_params=_cparams(("arbitrary", "arbitrary")),
        name="modulation",
    )(cond8, w_mod, b_mod.reshape(DEPTH, 1, 6 * D_MODEL))


def _head_rms(x, ones_bd, gain):
    ss = _dot((x * x).astype(bf16), ones_bd)
    return x * lax.rsqrt(ss * (1.0 / HEAD_DIM) + RMS_EPS) * gain


def _rope(x, cos, sin_a, sin_b):
    w = x.shape[-1]
    q4 = HEAD_DIM // 4
    return x * cos + pltpu.roll(x, w - q4, 1) * sin_a + pltpu.roll(x, q4, 1) * sin_b


def _dup_half(x, first):
    lane = lax.broadcasted_iota(i32, x.shape, 1)
    r = pltpu.roll(x, HEAD_DIM, 1)
    if first:
        return jnp.where(lane < HEAD_DIM, x, r)
    return jnp.where(lane >= HEAD_DIM, x, r)


def _gelu_tanh(x):
    c = np.sqrt(2.0 / np.pi).astype(np.float32)
    return x * (0.5 * (1.0 + jnp.tanh(c * (x + 0.044715 * (x * x * x)))))


def _inproj_kernel(xc_ref, xl_ref, mod_ref, win_ref, csc_ref, wsgu_ref, bsgu_ref, lng_ref, lnb_ref,
                   qg_ref, kg_ref, cos_ref, sina_ref, sinb_ref, ones_ref,
                   pq_ref, pool_ref, sgu_ref, q_ref, kd_ref, vd_ref, nk_ref, nv_ref):
    x = jnp.where(pl.program_id(0) < T_CTX // TM, xc_ref[...], xl_ref[...])
    mod = mod_ref[...]
    h = (x * (1.0 + mod[1:2]) + mod[0:1]).astype(bf16)
    proj = _dot(h, win_ref[...])

    a = proj[:, 0:FFT_W].astype(bf16)
    pq_ref[...] = _dot(a, csc_ref[...]).astype(bf16)

    pool_ref[...] = proj[:, FFT_W:FFT_W + POOL_W]

    o = FFT_W + POOL_W
    hgu = _gelu_tanh(proj[:, o:o + 2 * SGU_W])
    u = hgu[:, :SGU_W]
    v = hgu[:, SGU_W:]
    mu = jnp.mean(v, axis=-1, keepdims=True)
    vc = v - mu
    var = jnp.mean(vc * vc, axis=-1, keepdims=True)
    v = vc * lax.rsqrt(var + LN_EPS) * lng_ref[...] + lnb_ref[...]
    lane = lax.broadcasted_iota(i32, (CHUNK, SGU_W), 1)
    head = lane // (SGU_W // SGU_HEADS)
    wcat = wsgu_ref[...]
    for cidx in range(TM // CHUNK):
        rows = slice(cidx * CHUNK, (cidx + 1) * CHUNK)
        vch = v[rows]
        vblk = jnp.concatenate(
            [jnp.where(head == g, vch, 0.0) for g in range(SGU_HEADS)], axis=0).astype(bf16)
        sp = _dot(wcat, vblk) + bsgu_ref[...]
        sgu_ref[rows, :] = (u[rows] * sp).astype(bf16)

    o = o + 2 * SGU_W
    ones_bd = ones_ref[...]
    cos = cos_ref[...]
    sin_a = sina_ref[...]
    sin_b = sinb_ref[...]
    q = _head_rms(proj[:, o:o + ATTN_W], ones_bd, qg_ref[...])
    q = _rope(q, cos, sin_a, sin_b) * np.float32(HEAD_DIM ** -0.5 * np.log2(np.e))
    q_ref[...] = q.astype(bf16)
    o = o + ATTN_W
    k = _head_rms(proj[:, o:o + KV_W], ones_bd[:KV_W, :KV_W], kg_ref[...])
    nk_ref[...] = k
    k = _rope(k, cos[:, :KV_W], sin_a[:, :KV_W], sin_b[:, :KV_W])
    kd_ref[0] = _dup_half(k, True).astype(bf16)
    kd_ref[1] = _dup_half(k, False).astype(bf16)
    o = o + KV_W
    vv = proj[:, o:o + KV_W]
    nv_ref[...] = vv
    vd_ref[0] = _dup_half(vv, True).astype(bf16)
    vd_ref[1] = _dup_half(vv, False).astype(bf16)


def _rope_block(i):
    nlat = DEC_SEQ // TM
    nctx = T_CTX // TM
    return jnp.where(i < nctx, nlat, (i - nctx) % nlat)


def _ctx_tile(wd):
    return pl.BlockSpec((TM, wd), lambda i, *_: (jnp.minimum(i, T_CTX // TM - 1), 0))


def _lat_tile(wd):
    return pl.BlockSpec((TM, wd), lambda i, *_: (jnp.maximum(i - T_CTX // TM, 0), 0))


def _inproj(x_ctx, x_lat, mod, l, w):
    nt = T_ALL // TM
    tile = lambda wd: pl.BlockSpec((TM, wd), lambda i: (i, 0))
    const = lambda shape: pl.BlockSpec(shape, lambda i: (0,) * len(shape))
    rope_spec = pl.BlockSpec((TM, ATTN_W), lambda i: (_rope_block(i), 0))
    return pl.pallas_call(
        _inproj_kernel,
        grid=(nt,),
        in_specs=[
            _ctx_tile(D_MODEL), _lat_tile(D_MODEL),
            pl.BlockSpec((None, None, 6, D_MODEL), lambda i: (l, i // (SEG // TM), 0, 0)),
            pl.BlockSpec((None, D_MODEL, IN_W), lambda i: (l, 0, 0)),
            const((FFT_W, 2 * FFT_W)),
            pl.BlockSpec((None, CHUNK, SGU_HEADS * CHUNK), lambda i: (l, 0, 0)),
            pl.BlockSpec((None, CHUNK, SGU_W), lambda i: (l, 0, 0)),
            pl.BlockSpec((None, 1, SGU_W), lambda i: (l, 0, 0)),
            pl.BlockSpec((None, 1, SGU_W), lambda i: (l, 0, 0)),
            pl.BlockSpec((None, 1, ATTN_W), lambda i: (l, 0, 0)),
            pl.BlockSpec((None, 1, KV_W), lambda i: (l, 0, 0)),
            rope_spec, rope_spec, rope_spec,
            const((ATTN_W, ATTN_W)),
        ],
        out_specs=[
            tile(2 * FFT_W), tile(POOL_W), tile(SGU_W), tile(ATTN_W),
            pl.BlockSpec((N_KV_HEADS, TM, KV_W), lambda i: (0, i, 0)),
            pl.BlockSpec((N_KV_HEADS, TM, KV_W), lambda i: (0, i, 0)),
            tile(KV_W), tile(KV_W),
        ],
        out_shape=[
            jax.ShapeDtypeStruct((T_ALL, 2 * FFT_W), bf16),
            jax.ShapeDtypeStruct((T_ALL, POOL_W), f32),
            jax.ShapeDtypeStruct((T_ALL, SGU_W), bf16),
            jax.ShapeDtypeStruct((T_ALL, ATTN_W), bf16),
            jax.ShapeDtypeStruct((N_KV_HEADS, T_ALL, KV_W), bf16),
            jax.ShapeDtypeStruct((N_KV_HEADS, T_ALL, KV_W), bf16),
            jax.ShapeDtypeStruct((T_ALL, KV_W), f32),
            jax.ShapeDtypeStruct((T_ALL, KV_W), f32),
        ],
        compiler_params=_cparams(("arbitrary",)),
        name="inproj",
    )(x_ctx, x_lat, mod, w["w_in"], w["csc"], w["w_sgu"], w["b_sgu"], w["sgu_ln_g"], w["sgu_ln_b"],
      w["q_norm_g"], w["k_norm_g"], w["rope_cos"], w["rope_sin_a"], w["rope_sin_b"], w["ones_bd"])


def _pool_kernel(prev_ref, cur_ref, next_ref, wp_ref, scale_ref, o_ref):
    i = pl.program_id(0)
    n = jnp.where(i < T_CTX // POOL_TB, SEQ, DEC_SEQ)
    hl = POOL_HALO
    ext = jnp.concatenate([prev_ref[POOL_TB - hl:, :], cur_ref[...], next_ref[:hl, :]], axis=0)
    rows = POOL_TB + 2 * hl
    r = lax.broadcasted_iota(i32, (rows, 1), 0)
    pos = (i * POOL_TB + r - hl) & (n - 1)

    def back(a, s):
        return jnp.where(pos >= s, pltpu.roll(a, s, 0), 0.0)

    def fwd(a, s):
        return jnp.where(pos + s < n, pltpu.roll(a, rows - s, 0), 0.0)

    bsum = [back(ext, 1)]
    fsum = [ext]
    for k in range(3):
        s = 1 << k
        bsum.append(bsum[k] + back(bsum[k], s))
        fsum.append(fsum[k] + fwd(fsum[k], s))
    lane = lax.broadcasted_iota(i32, (1, POOL_W), 1)
    grp = lane // POOL_GROUP
    win = bsum[3] + fsum[3]
    half = jnp.full((1, POOL_W), POOL_WINDOWS[3] // 2, i32)
    for g in (2, 1, 0):
        win = jnp.where(grp == g, bsum[g] + fsum[g], win)
        half = jnp.where(grp == g, POOL_WINDOWS[g] // 2, half)
    cnt = (jnp.minimum(pos + half, n) - jnp.maximum(pos - half, 0)).astype(f32)
    y = (win / cnt - ext)[hl:hl + POOL_TB]
    o_ref[...] = (_dot(y.astype(bf16), wp_ref[...]) * scale_ref[...]).astype(bf16)


def _pool(p, l, w):
    nt = T_ALL // POOL_TB
    blk = lambda f: pl.BlockSpec((POOL_TB, POOL_W), lambda i: (f(i), 0))
    return pl.pallas_call(
        _pool_kernel,
        grid=(nt,),
        in_specs=[
            blk(lambda i: jnp.maximum(i - 1, 0)), blk(lambda i: i),
            blk(lambda i: jnp.minimum(i + 1, nt - 1)),
            pl.BlockSpec((None, POOL_W, POOL_W), lambda i: (l, 0, 0)),
            pl.BlockSpec((None, 1, POOL_W), lambda i: (l, 0, 0)),
        ],
        out_specs=blk(lambda i: i),
        out_shape=jax.ShapeDtypeStruct((T_ALL, POOL_W), bf16),
        compiler_params=_cparams(("arbitrary",)),
        name="pool",
    )(p, p, p, w["w_pool_bd"], w["pool_scale"])


def _seqdft_kernel(*refs, n, nseq):
    m_ref, pq_refs, w_ref, o_ref = refs[0], refs[1:-2], refs[-2], refs[-1]
    per_blk = SEG // n
    for b in range(nseq):
        @pl.when(pl.program_id(1) == b)
        def _(b=b):
            pq_ref = pq_refs[b // per_blk]
            r0 = (b % per_blk) * n
            f = (_dot(m_ref[:, :n], pq_ref[r0:r0 + n, :FFT_W])
                 + _dot(m_ref[:, n:], pq_ref[r0:r0 + n, FFT_W:]))
            o_ref[...] = _dot(f.astype(bf16), w_ref[...]).astype(bf16)


def _seqdft(pq, m, l, w, *, n, tr, nseq, row0):
    nr = n // tr
    nblk = nseq * n // SEG
    pq_specs = [pl.BlockSpec((SEG, 2 * FFT_W), lambda i, b, j=j: (row0 // SEG + j, 0))
                for j in range(nblk)]
    return pl.pallas_call(
        functools.partial(_seqdft_kernel, n=n, nseq=nseq),
        grid=(nr, nseq),
        in_specs=[pl.BlockSpec((tr, 2 * n), lambda i, b: (i, 0))] + pq_specs
        + [pl.BlockSpec((None, FFT_W, FFT_W), lambda i, b: (l, 0, 0))],
        out_specs=pl.BlockSpec((tr, FFT_W), lambda i, b: (b * nr + i, 0)),
        out_shape=jax.ShapeDtypeStruct((nseq * n, FFT_W), bf16),
        compiler_params=_cparams(("arbitrary", "arbitrary")),
        name="seqdft_%d" % n,
    )(m, *([pq] * nblk), w["w_fft"])


def _attn_kernel(*refs, has_cache):
    if has_cache:
        q_ref, k_ref, v_ref, kc_ref, vc_ref, o_ref = refs
    else:
        q_ref, k_ref, v_ref, o_ref = refs
    q = q_ref[...]
    tq = q.shape[0]
    lane = lax.broadcasted_iota(i32, q.shape, 1)
    zero = jnp.zeros_like(q)
    qs = jnp.concatenate([jnp.where(lane < HEAD_DIM, q, zero),
                          jnp.where(lane >= HEAD_DIM, q, zero)], axis=0)
    nt = (((1,), (1,)), ((), ()))
    n = k_ref.shape[0]
    chunk = min(n, ATT_CHUNK)
    parts = [(k_ref, v_ref, c * chunk, chunk) for c in range(n // chunk)]
    if has_cache:
        parts = [(kc_ref, vc_ref, 0, PAST_LEN)] + parts
    m = jnp.full((2 * tq, 1), -jnp.inf, f32)
    den = jnp.zeros((2 * tq, 1), f32)
    acc = jnp.zeros((2 * tq, 2 * HEAD_DIM), f32)
    for kr, vr, off, size in parts:
        s = lax.dot_general(qs, kr[off:off + size, :], nt, preferred_element_type=f32)
        m_new = jnp.maximum(m, jnp.max(s, axis=-1, keepdims=True))
        alpha = jnp.exp2(m - m_new)
        p = jnp.exp2(s - m_new).astype(bf16)
        den = alpha * den + jnp.sum(p.astype(f32), axis=-1, keepdims=True)
        acc = alpha * acc + _dot(p, vr[off:off + size, :])
        m = m_new
    out = acc / den
    o_ref[...] = jnp.where(lane < HEAD_DIM, out[:tq], out[tq:]).astype(bf16)


def _attention(q, kd, vd, cache, *, n, tq, nseq, row0):
    nq = n // tq
    b0 = row0 // n
    q0 = row0 // tq
    in_specs = [
        pl.BlockSpec((tq, 2 * HEAD_DIM), lambda b, h, i: (q0 + b * nq + i, h)),
        pl.BlockSpec((None, n, KV_W), lambda b, h, i: (h, b0 + b, 0)),
        pl.BlockSpec((None, n, KV_W), lambda b, h, i: (h, b0 + b, 0)),
    ]
    args = [q, kd, vd]
    if cache is not None:
        cspec = pl.BlockSpec((None, None, PAST_LEN, KV_W), lambda b, h, i: (h, b, 0, 0))
        in_specs += [cspec, cspec]
        args += list(cache)
    return pl.pallas_call(
        functools.partial(_attn_kernel, has_cache=cache is not None),
        grid=(nseq, N_KV_HEADS, nq),
        in_specs=in_specs,
        out_specs=pl.BlockSpec((tq, 2 * HEAD_DIM), lambda b, h, i: (b * nq + i, h)),
        out_shape=jax.ShapeDtypeStruct((nseq * n, ATTN_W), bf16),
        compiler_params=_cparams(("arbitrary", "arbitrary", "arbitrary")),
        name="attention_%d" % n,
    )(*args)


def _layer_norm(x, g, b):
    mu = jnp.mean(x, axis=-1, keepdims=True)
    xc = x - mu
    var = jnp.mean(xc * xc, axis=-1, keepdims=True)
    return xc * lax.rsqrt(var + LN_EPS) * g + b


def _outproj_kernel(xc_ref, xl_ref, mod_ref, fc_ref, fl_ref, p_ref, s_ref, ac_ref, al_ref, wout_ref,
                    g_ref, b_ref, wr_ref, br_ref, tril_ref,
                    x1_ref, h2_ref, route_ref, cnt_ref, tab_ref, carry_ref):
    i = pl.program_id(0)

    @pl.when(i == 0)
    def _():
        carry_ref[...] = jnp.zeros_like(carry_ref)

    mod = mod_ref[...]
    is_ctx = i < T_CTX // TM
    f_mix = jnp.where(is_ctx, fc_ref[...], fl_ref[...])
    a_mix = jnp.where(is_ctx, ac_ref[...], al_ref[...])
    mix = _dot(jnp.concatenate([f_mix, p_ref[...], s_ref[...], a_mix], axis=1), wout_ref[...])
    x = jnp.where(is_ctx, xc_ref[...], xl_ref[...])
    x1 = _layer_norm(DEEPNORM_ALPHA * x + mod[2:3] * mix, g_ref[...], b_ref[...])
    x1_ref[...] = x1
    h2 = x1 * (1.0 + mod[4:5]) + mod[3:4]
    h2_ref[...] = h2

    h_hi, h_lo = _split_hi_lo(h2)
    hw = _dot(h_hi, wr_ref[...])
    logits = hw[:, :128] + hw[:, 128:] + _dot(h_lo, wr_ref[:, :128]) + br_ref[...]
    lane = lax.broadcasted_iota(i32, logits.shape, 1).astype(f32)
    neg = jnp.float32(-jnp.inf)
    big = jnp.float32(1 << 20)
    gl = jnp.where(lane < N_GROUPS, logits, neg)
    gmax = jnp.max(gl, axis=-1, keepdims=True)
    gsel = jnp.min(jnp.where(gl == gmax, lane, big), axis=-1, keepdims=True)
    pg = 1.0 / jnp.sum(jnp.exp(gl - gmax), axis=-1, keepdims=True)
    e_lo = ROUTE_E0 + gsel * EXPERTS_PER_GROUP
    el = jnp.where((lane >= e_lo) & (lane < e_lo + EXPERTS_PER_GROUP), logits, neg)
    v1 = jnp.max(el, axis=-1, keepdims=True)
    i1 = jnp.min(jnp.where(el == v1, lane, big), axis=-1, keepdims=True)
    el2 = jnp.where(lane == i1, neg, el)
    v2 = jnp.max(el2, axis=-1, keepdims=True)
    i2 = jnp.min(jnp.where(el2 == v2, lane, big), axis=-1, keepdims=True)
    e2 = jnp.exp(v2 - v1)
    w1 = pg / (1.0 + e2)
    w2 = pg * e2 / (1.0 + e2)
    oh1 = lane == i1
    oh2 = lane == i2
    oh = jnp.where(oh1 | oh2, 1.0, 0.0)
    lrank = _dot(tril_ref[...], oh.astype(bf16))
    seg = jnp.floor((jnp.sum(oh, axis=0, keepdims=True) + (ROW_CHUNK - 1.0)) * (1.0 / ROW_CHUNK)) * ROW_CHUNK
    seg8 = jnp.broadcast_to(seg, (8, 128))
    lane8 = lax.broadcasted_iota(i32, (8, 128), 1)
    off8 = seg8
    for sh in (1, 2, 4, 8, 16):
        off8 = off8 + jnp.where(lane8 >= sh, pltpu.roll(off8, sh, 1), 0.0)
    off8 = off8 - seg8
    carry = carry_ref[...]
    lpos = lrank + off8[0:1, :]
    rank = lrank + carry[0:1, :]
    pick = lambda sel, val: jnp.sum(jnp.where(sel, val, 0.0), axis=-1, keepdims=True)
    sub8 = lax.broadcasted_iota(i32, (8, 128), 0)
    tab_ref[...] = jnp.where(sub8 == 0, seg8, jnp.where(sub8 == 1, off8, jnp.where(sub8 == 2, carry, 0.0)))
    carry = carry + seg8
    carry_ref[...] = carry
    cnt_ref[...] = carry
    cols = (i1 - ROUTE_E0, i2 - ROUTE_E0, w1, w2, pick(oh1, rank), pick(oh2, rank),
            pick(oh1, lpos), pick(oh2, lpos))
    route = jnp.zeros_like(logits)
    for j, col in enumerate(cols):
        route = jnp.where(lane == j, col, route)
    route_ref[...] = route


def _outproj(x_ctx, x_lat, mod, fo_ctx, fo_lat, po, so, ao_ctx, ao_lat, l, w):
    nt = T_ALL // TM
    tile = lambda wd: pl.BlockSpec((TM, wd), lambda i: (i, 0))
    vec = lambda wd: pl.BlockSpec((None, 1, wd), lambda i: (l, 0, 0))
    return pl.pallas_call(
        _outproj_kernel,
        grid=(nt,),
        in_specs=[
            _ctx_tile(D_MODEL), _lat_tile(D_MODEL),
            pl.BlockSpec((None, None, 6, D_MODEL), lambda i: (l, i // (SEG // TM), 0, 0)),
            _ctx_tile(FFT_W), _lat_tile(FFT_W), tile(POOL_W), tile(SGU_W),
            _ctx_tile(ATTN_W), _lat_tile(ATTN_W),
            pl.BlockSpec((None, D_MODEL, D_MODEL), lambda i: (l, 0, 0)),
            vec(D_MODEL), vec(D_MODEL),
            pl.BlockSpec((None, D_MODEL, 256), lambda i: (l, 0, 0)),
            vec(128),
            pl.BlockSpec((TM, TM), lambda i: (0, 0)),
        ],
        out_specs=[tile(D_MODEL), tile(D_MODEL), tile(128), pl.BlockSpec((8, 128), lambda i: (0, 0)),
                   pl.BlockSpec((None, 8, 128), lambda i: (i, 0, 0))],
        out_shape=[
            jax.ShapeDtypeStruct((T_ALL, D_MODEL), f32),
            jax.ShapeDtypeStruct((T_ALL, D_MODEL), f32),
            jax.ShapeDtypeStruct((T_ALL, 128), f32),
            jax.ShapeDtypeStruct((8, 128), f32),
            jax.ShapeDtypeStruct((nt, 8, 128), f32),
        ],
        scratch_shapes=[pltpu.VMEM((8, 128), f32)],
        compiler_params=_cparams(("arbitrary",)),
        name="outproj",
    )(x_ctx, x_lat, mod, fo_ctx, fo_lat, po, so, ao_ctx, ao_lat,
      w["w_out"], w["ln1_g"], w["ln1_b"], w["w_r"], w["b_r"],
      w["tril"])


def _plan_kernel(route_ref, cnt_ref, pos_ref, meta_ref):
    lane = lax.broadcasted_iota(i32, (8, 128), 1)
    sub = lax.broadcasted_iota(i32, (8, 128), 0)
    cnt = cnt_ref[...]
    is_e = (lane >= ROUTE_E0) & (lane < ROUTE_E0 + N_EXPERTS)
    tiles = jnp.where(is_e, jnp.floor((cnt + (MOE_TM - 1.0)) * (1.0 / MOE_TM)), 0.0)
    cum = tiles
    for s in (1, 2, 4, 8, 16):
        cum = cum + jnp.where(lane >= s, pltpu.roll(cum, s, 1), 0.0)
    pstart = (cum - tiles) * MOE_TM
    nused = jnp.max(cum, axis=-1, keepdims=True)
    fill = jnp.where(is_e & (cnt != tiles * MOE_TM), pstart + (tiles - 1.0) * MOE_TM, -1.0)
    meta = jnp.where(sub == 0, cnt, jnp.where(sub == 1, nused, jnp.where(sub == 2, fill,
                     jnp.where(sub == 3, pstart, 0.0))))
    meta_ref[...] = meta.astype(i32)

    r = route_ref[...]
    lane_t = lax.broadcasted_iota(i32, r.shape, 1).astype(f32)
    ps = pstart[0:1, :]

    def dest(ecol, rcol):
        hit = lane_t == (r[:, ecol:ecol + 1] + ROUTE_E0)
        return jnp.sum(jnp.where(hit, ps, 0.0), axis=-1, keepdims=True) + r[:, rcol:rcol + 1]

    pos = jnp.where(lane_t == 0, dest(0, 4), jnp.where(lane_t == 1, dest(1, 5), 0.0))
    pos_ref[...] = pos.astype(i32)


def _plan(route, cnt):
    return pl.pallas_call(
        _plan_kernel,
        grid=(T_ALL // PLAN_TB,),
        in_specs=[pl.BlockSpec((PLAN_TB, 128), lambda i: (i, 0)),
                  pl.BlockSpec((8, 128), lambda i: (0, 0))],
        out_specs=[pl.BlockSpec((PLAN_TB, 128), lambda i: (i, 0)),
                   pl.BlockSpec((8, 128), lambda i: (0, 0))],
        out_shape=[jax.ShapeDtypeStruct((T_ALL, 128), i32), jax.ShapeDtypeStruct((8, 128), i32)],
        compiler_params=_cparams(("arbitrary",)),
        name="plan",
    )(route, cnt)


def _dispatch_kernel(nch_ref, off_ref, dst_ref, fill_ref, nused_ref, h_ref, route_ref, xs_ref,
                     sorted_ref, zero_ref, sem, fill_sem):
    i = pl.program_id(0)

    def tile_fill(row0):
        return pltpu.make_async_copy(zero_ref, xs_ref.at[pl.ds(pl.multiple_of(row0, MOE_TM), MOE_TM)],
                                     fill_sem)

    @pl.when(i == 0)
    def _():
        zero_ref[...] = jnp.zeros_like(zero_ref)

        def start(e, c):
            @pl.when(fill_ref[e] >= 0)
            def _():
                tile_fill(jnp.maximum(fill_ref[e], 0)).start()
            return c

        def wait(e, c):
            @pl.when(fill_ref[e] >= 0)
            def _():
                tile_fill(jnp.maximum(fill_ref[e], 0)).wait()
            return c

        def start_tail(t, c):
            tile_fill(t * MOE_TM).start()
            return c

        def wait_tail(t, c):
            tile_fill(t * MOE_TM).wait()
            return c

        lax.fori_loop(0, N_EXPERTS, start, 0)
        lax.fori_loop(nused_ref[0], MOE_NT, start_tail, 0)
        lax.fori_loop(0, N_EXPERTS, wait, 0)
        lax.fori_loop(nused_ref[0], MOE_NT, wait_tail, 0)

    rt = route_ref[...].T
    j = lax.broadcasted_iota(i32, (DISP_ROWS, 1), 0).astype(f32)
    sel = jnp.where((j == rt[6:7, :]) | (j == rt[7:8, :]), 1.0, 0.0).astype(bf16)
    sorted_ref[...] = _dot(sel, h_ref[...].astype(bf16))

    def chunk_copy(s0, d0, c):
        src = sorted_ref.at[pl.ds(pl.multiple_of(s0 + c * ROW_CHUNK, ROW_CHUNK), ROW_CHUNK)]
        dst = xs_ref.at[pl.ds(pl.multiple_of(d0 + c * ROW_CHUNK, ROW_CHUNK), ROW_CHUNK)]
        return pltpu.make_async_copy(src, dst, sem)

    def for_chunks(action):
        def per_expert(e, c):
            idx = i * N_EXPERTS + e
            s0, d0 = off_ref[idx], dst_ref[idx]

            def per_chunk(cc, c2):
                action(chunk_copy(s0, d0, cc))
                return c2

            lax.fori_loop(0, nch_ref[idx], per_chunk, 0)
            return c

        lax.fori_loop(0, N_EXPERTS, per_expert, 0)

    for_chunks(lambda cp: cp.start())
    for_chunks(lambda cp: cp.wait())


def _dispatch(nch, off, dst, fill, nused, h2, route):
    grid_spec = pltpu.PrefetchScalarGridSpec(
        num_scalar_prefetch=5,
        grid=(T_ALL // TM,),
        in_specs=[pl.BlockSpec((TM, D_MODEL), lambda i, *_: (i, 0)),
                  pl.BlockSpec((TM, 128), lambda i, *_: (i, 0))],
        out_specs=pl.BlockSpec(memory_space=pl.ANY),
        scratch_shapes=[pltpu.VMEM((DISP_ROWS, D_MODEL), f32), pltpu.VMEM((MOE_TM, D_MODEL), f32),
                        pltpu.SemaphoreType.DMA(()), pltpu.SemaphoreType.DMA(())],
    )
    return pl.pallas_call(
        _dispatch_kernel,
        grid_spec=grid_spec,
        out_shape=jax.ShapeDtypeStruct((MOE_NT * MOE_TM, D_MODEL), f32),
        compiler_params=_cparams(("arbitrary",)),
        name="dispatch",
    )(nch, off, dst, fill, nused, h2, route)


def _experts_kernel(cnt_ref, nused_ref, xs_ref, wg_hbm, wu_hbm, wd_hbm, ys_ref,
                    wg_f, wu_f, wd_f, wg_b, wu_b, wd_b, st, wsem, *, layer):
    i = pl.program_id(0)
    nused = nused_ref[0]
    NXT, NSLOT, LEFT = 0, 1, 2

    def w_copies(e, slot):
        return (pltpu.make_async_copy(wg_hbm.at[layer, e], wg_f.at[slot], wsem.at[slot, 0]),
                pltpu.make_async_copy(wu_hbm.at[layer, e], wu_f.at[slot], wsem.at[slot, 1]),
                pltpu.make_async_copy(wd_hbm.at[layer, e], wd_f.at[slot], wsem.at[slot, 2]))

    def next_nonempty(e):
        return lax.while_loop(
            lambda v: (v < N_EXPERTS) & (cnt_ref[jnp.minimum(v, N_EXPERTS - 1)] == 0),
            lambda v: v + 1, e)

    @pl.when(i == 0)
    def _():
        e0 = next_nonempty(jnp.int32(0))
        for c in w_copies(e0, 0):
            c.start()
        st[NXT] = e0
        st[NSLOT] = 0
        st[LEFT] = 0

    @pl.when(i < nused)
    def _():
        @pl.when(st[LEFT] == 0)
        def _():
            e = st[NXT]
            slot = st[NSLOT]
            for c in w_copies(e, slot):
                c.wait()
            e2 = next_nonempty(e + 1)

            @pl.when(e2 < N_EXPERTS)
            def _():
                for c in w_copies(e2, 1 - slot):
                    c.start()

            st[NXT] = e2
            st[NSLOT] = 1 - slot
            st[LEFT] = (cnt_ref[e] + (MOE_TM - 1)) // MOE_TM
            wg_b[...] = wg_f[slot].astype(bf16)
            wu_b[...] = wu_f[slot].astype(bf16)
            wd_b[...] = wd_f[slot].astype(bf16)

        x = xs_ref[...].astype(bf16)
        hg = _dot(x, wg_b[...])
        hu = _dot(x, wu_b[...])
        act = (hg * jax.nn.sigmoid(hg)) * hu
        ys_ref[...] = _dot(act.astype(bf16), wd_b[...])
        st[LEFT] = st[LEFT] - 1

    @pl.when(i >= nused)
    def _():
        ys_ref[...] = jnp.zeros_like(ys_ref)


def _experts(counts, nused, xs, l, w_gate, w_up, w_down):
    hbm = pl.BlockSpec(memory_space=pl.ANY)
    grid_spec = pltpu.PrefetchScalarGridSpec(
        num_scalar_prefetch=2,
        grid=(MOE_NT,),
        in_specs=[pl.BlockSpec((MOE_TM, D_MODEL), lambda i, c, nu: (jnp.minimum(i, nu[0] - 1), 0)),
                  hbm, hbm, hbm],
        out_specs=pl.BlockSpec((MOE_TM, D_MODEL), lambda i, c, nu: (i, 0)),
        scratch_shapes=[
            pltpu.VMEM((2, D_MODEL, EXPERT_FF), f32),
            pltpu.VMEM((2, D_MODEL, EXPERT_FF), f32),
            pltpu.VMEM((2, EXPERT_FF, D_MODEL), f32),
            pltpu.VMEM((D_MODEL, EXPERT_FF), bf16),
            pltpu.VMEM((D_MODEL, EXPERT_FF), bf16),
            pltpu.VMEM((EXPERT_FF, D_MODEL), bf16),
            pltpu.SMEM((4,), i32),
            pltpu.SemaphoreType.DMA((2, 3)),
        ],
    )
    return pl.pallas_call(
        functools.partial(_experts_kernel, layer=l),
        grid_spec=grid_spec,
        out_shape=jax.ShapeDtypeStruct((MOE_NT * MOE_TM, D_MODEL), f32),
        compiler_params=_cparams(("arbitrary",)),
        name="experts",
    )(counts, nused, xs, w_gate, w_up, w_down)


def _combine_kernel(pos_ref, x1_ref, mod_ref, route_ref, ys_hbm, g_ref, b_ref, oc_ref, ol_ref, ybuf, sem):
    i = pl.program_id(0)
    nt = pl.num_programs(0) - 1

    @pl.when(i < nt)
    def _():
        slot = i % 2
        for k in range(2):
            base = k * T_ALL + i * TM
            for j in range(TM):
                pltpu.make_async_copy(ys_hbm.at[pl.ds(pos_ref[base + j], 1)],
                                      ybuf.at[slot, k, pl.ds(j, 1)], sem.at[slot]).start()

    @pl.when(i >= 1)
    def _():
        slot = (i - 1) % 2
        for k in range(2):
            pltpu.make_async_copy(ys_hbm.at[pl.ds(0, TM)], ybuf.at[slot, k], sem.at[slot]).wait()
        route = route_ref[...]
        mod = mod_ref[...]
        moe = route[:, 2:3] * ybuf[slot, 0] + route[:, 3:4] * ybuf[slot, 1]
        y = _layer_norm(DEEPNORM_ALPHA * x1_ref[...] + mod[5:6] * moe, g_ref[...], b_ref[...])

        @pl.when(i - 1 < T_CTX // TM)
        def _():
            oc_ref[...] = y

        @pl.when(i - 1 >= T_CTX // TM)
        def _():
            ol_ref[...] = y


def _combine(pos_flat, x1, mod, route, ys, l, w):
    nt = T_ALL // TM
    vec = pl.BlockSpec((None, 1, D_MODEL), lambda i, p: (l, 0, 0))
    nctx = T_CTX // TM
    prev = lambda i: jnp.maximum(i - 1, 0)
    grid_spec = pltpu.PrefetchScalarGridSpec(
        num_scalar_prefetch=1,
        grid=(nt + 1,),
        in_specs=[
            pl.BlockSpec((TM, D_MODEL), lambda i, p: (prev(i), 0)),
            pl.BlockSpec((None, None, 6, D_MODEL), lambda i, p: (l, prev(i) // (SEG // TM), 0, 0)),
            pl.BlockSpec((TM, 128), lambda i, p: (prev(i), 0)),
            pl.BlockSpec(memory_space=pl.ANY),
            vec, vec,
        ],
        out_specs=[pl.BlockSpec((TM, D_MODEL), lambda i, p: (jnp.minimum(prev(i), nctx - 1), 0)),
                   pl.BlockSpec((TM, D_MODEL), lambda i, p: (jnp.maximum(prev(i) - nctx, 0), 0))],
        scratch_shapes=[pltpu.VMEM((2, 2, TM, D_MODEL), f32), pltpu.SemaphoreType.DMA((2,))],
    )
    return pl.pallas_call(
        _combine_kernel,
        grid_spec=grid_spec,
        out_shape=[jax.ShapeDtypeStruct((T_CTX, D_MODEL), f32),
                   jax.ShapeDtypeStruct((T_LAT, D_MODEL), f32)],
        compiler_params=_cparams(("arbitrary",)),
        name="combine",
    )(pos_flat, x1, mod, route, ys, w["ln2_g"], w["ln2_b"])


def _dft_cos_sin(n, scale):
    k = jnp.arange(n, dtype=i32)
    ang = ((k[:, None] * k[None, :]) % n).astype(f32) * np.float32(2.0 * np.pi / n)
    return jnp.cos(ang) * scale, jnp.sin(ang) * scale


def _seq_dft_matrix(n):
    g = min(DFT_SPLIT, n)
    j = jnp.arange(n, dtype=i32)[None, :]
    k1 = jnp.arange(n // g, dtype=i32)[:, None]
    k2 = jnp.arange(g, dtype=i32)[:, None]
    ang_a = ((k1 * j) % (n // g)).astype(f32) * np.float32(2.0 * np.pi * g / n)
    ang_b = ((k2 * j) % n).astype(f32) * np.float32(2.0 * np.pi / n)
    scale = np.float32(n ** -0.5)
    ca, sa = jnp.cos(ang_a), jnp.sin(ang_a)
    cb, sb = jnp.cos(ang_b) * scale, jnp.sin(ang_b) * scale
    ca2 = jnp.concatenate([ca, ca], axis=1)[:, None, :]
    sa2 = jnp.concatenate([sa, sa], axis=1)[:, None, :]
    cb2 = jnp.concatenate([cb, -sb], axis=1)[None, :, :]
    sb2 = jnp.concatenate([sb, cb], axis=1)[None, :, :]
    return (ca2 * cb2 - sa2 * sb2).astype(bf16).reshape(n, 2 * n)


def _rope_tables():
    rows = DEC_SEQ // GRID_W
    row = jnp.repeat(jnp.arange(rows), GRID_W).astype(f32)
    col = jnp.tile(jnp.arange(GRID_W), rows).astype(f32)
    n_freq = HEAD_DIM // 4
    inv = ROPE_THETA ** (-jnp.arange(n_freq, dtype=f32) / n_freq)
    ar = row[:, None] * inv
    ac = col[:, None] * inv
    ang = jnp.concatenate([ar, ar, ac, ac], axis=-1)
    cos = jnp.tile(jnp.cos(ang), (1, N_HEADS))
    sin = jnp.tile(jnp.sin(ang), (1, N_HEADS))
    first = (jnp.arange(ATTN_W) % (HEAD_DIM // 2)) < n_freq
    sin_a = jnp.where(first[None, :], -sin, 0.0)
    sin_b = jnp.where(first[None, :], 0.0, sin)
    ident = jnp.zeros((TM, ATTN_W), f32)
    return (jnp.concatenate([cos, ident + 1.0], axis=0),
            jnp.concatenate([sin_a, ident], axis=0),
            jnp.concatenate([sin_b, ident], axis=0))


def _dup_cache(cache):
    c = jnp.transpose(cache, (1, 3, 0, 2, 4))
    return jnp.concatenate([c, c], axis=-1).astype(bf16)


def kernel(x_prompt, x_sample, cache_k, cache_v, c, c_ctx, w_mod, b_mod, w_in, w_fft, w_pool, pool_scale, sgu_ln_g, sgu_ln_b, w_sgu, b_sgu, q_norm_g, k_norm_g, w_out, ln1_g, ln1_b, w_router_group, b_router_group, w_router_expert, b_router_expert, w_gate, w_up, w_down, ln2_g, ln2_b):
    L = DEPTH
    x_ctx = x_prompt.reshape(T_CTX, D_MODEL)
    x_lat = x_sample.reshape(T_LAT, D_MODEL)

    cond8 = jnp.concatenate([c_ctx[None, :], c, jnp.zeros((8 - 1 - DEC_BATCH, D_MODEL), f32)], axis=0)
    mod = _modulation(cond8, w_mod, b_mod)[:, :N_SEG].reshape(L, N_SEG, 6, D_MODEL)

    cc, sc = _dft_cos_sin(FFT_W, np.float32(FFT_W ** -0.5))
    rope_cos, rope_sin_a, rope_sin_b = _rope_tables()
    head_id = jnp.arange(ATTN_W) // HEAD_DIM
    eye_g = jnp.eye(len(POOL_WINDOWS), dtype=f32)
    w_r = jnp.zeros((L, D_MODEL, 128), f32)
    w_r = w_r.at[:, :, :N_GROUPS].set(w_router_group).at[:, :, ROUTE_E0:ROUTE_E0 + N_EXPERTS].set(w_router_expert)
    b_r = jnp.zeros((L, 1, 128), f32)
    b_r = b_r.at[:, 0, :N_GROUPS].set(b_router_group).at[:, 0, ROUTE_E0:ROUTE_E0 + N_EXPERTS].set(b_router_expert)
    w_r_hi, w_r_lo = _split_hi_lo(w_r)
    w = {
        "w_in": w_in.astype(bf16),
        "csc": jnp.concatenate([cc, sc], axis=1).astype(bf16),
        "w_sgu": jnp.transpose(w_sgu, (0, 2, 1, 3)).reshape(L, CHUNK, SGU_HEADS * CHUNK).astype(bf16),
        "b_sgu": jnp.repeat(jnp.transpose(b_sgu, (0, 2, 1)), SGU_W // SGU_HEADS, axis=2),
        "sgu_ln_g": sgu_ln_g.reshape(L, 1, SGU_W),
        "sgu_ln_b": sgu_ln_b.reshape(L, 1, SGU_W),
        "q_norm_g": jnp.tile(q_norm_g, (1, N_HEADS)).reshape(L, 1, ATTN_W),
        "k_norm_g": jnp.tile(k_norm_g, (1, N_KV_HEADS)).reshape(L, 1, KV_W),
        "rope_cos": rope_cos, "rope_sin_a": rope_sin_a, "rope_sin_b": rope_sin_b,
        "ones_bd": (head_id[:, None] == head_id[None, :]).astype(bf16),
        "w_pool_bd": jnp.einsum("lgcd,gh->lgchd", w_pool, eye_g).reshape(L, POOL_W, POOL_W).astype(bf16),
        "pool_scale": pool_scale.reshape(L, 1, POOL_W),
        "w_fft": w_fft.astype(bf16),
        "w_out": w_out.astype(bf16),
        "ln1_g": ln1_g.reshape(L, 1, D_MODEL), "ln1_b": ln1_b.reshape(L, 1, D_MODEL),
        "ln2_g": ln2_g.reshape(L, 1, D_MODEL), "ln2_b": ln2_b.reshape(L, 1, D_MODEL),
        "w_r": jnp.concatenate([w_r_hi, w_r_lo], axis=-1), "b_r": b_r,
        "tril": (jnp.arange(TM)[:, None] > jnp.arange(TM)[None, :]).astype(bf16),
    }
    m_ctx = _seq_dft_matrix(SEQ)
    m_lat = _seq_dft_matrix(DEC_SEQ)
    kc_all = _dup_cache(cache_k)
    vc_all = _dup_cache(cache_v)

    new_k, new_v = [], []
    for l in range(L):
        pq, praw, sgu, q, kd, vd, nk, nv = _inproj(x_ctx, x_lat, mod, l, w)
        new_k.append(nk[:T_CTX].reshape(BATCH, SEQ, N_KV_HEADS, HEAD_DIM))
        new_v.append(nv[:T_CTX].reshape(BATCH, SEQ, N_KV_HEADS, HEAD_DIM))
        po = _pool(praw, l, w)
        fo_ctx = _seqdft(pq, m_ctx, l, w, n=SEQ, tr=SEQ, nseq=BATCH, row0=0)
        fo_lat = _seqdft(pq, m_lat, l, w, n=DEC_SEQ, tr=FFT_TR, nseq=DEC_BATCH, row0=T_CTX)
        ao_ctx = _attention(q, kd, vd, None, n=SEQ, tq=SEQ, nseq=BATCH, row0=0)
        ao_lat = _attention(q, kd, vd, (kc_all[l], vc_all[l]), n=DEC_SEQ, tq=ATT_TQ, nseq=DEC_BATCH,
                            row0=T_CTX)
        x1, h2, route, cnt, tab = _outproj(x_ctx, x_lat, mod, fo_ctx, fo_lat, po, sgu, ao_ctx, ao_lat, l, w)
        pos_slab, meta = _plan(route, cnt)
        pos_flat = pos_slab[:, :2].T.reshape(-1)
        experts = slice(ROUTE_E0, ROUTE_E0 + N_EXPERTS)
        counts = meta[0, experts]
        nused = meta[1, :1]
        fill = meta[2, experts]
        tab = tab[:, :, experts].astype(i32)
        nch = (tab[:, 0] // ROW_CHUNK).reshape(-1)
        off = tab[:, 1].reshape(-1)
        dst = (meta[3, experts][None, :] + tab[:, 2]).reshape(-1)
        xs = _dispatch(nch, off, dst, fill, nused, h2, route)
        ys = _experts(counts, nused, xs, l, w_gate, w_up, w_down)
        x_ctx, x_lat = _combine(pos_flat, x1, mod, route, ys, l, w)

    y_prompt = x_ctx.reshape(BATCH, SEQ, D_MODEL)
    y_sample = x_lat.reshape(DEC_BATCH, DEC_SEQ, D_MODEL)
    return (y_prompt, y_sample, jnp.stack(new_k, axis=1), jnp.stack(new_v, axis=1))
```

```python
import functools

import numpy as np
import jax
import jax.numpy as jnp
from jax import lax
from jax.experimental import pallas as pl
from jax.experimental.pallas import tpu as pltpu

f32 = jnp.float32
bf16 = jnp.bfloat16
i32 = jnp.int32

D_MODEL = 1024
BATCH = 16
SEQ = 256
DEPTH = 4
DEC_BATCH = 2
DEC_SEQ = 4096
PAST_LEN = 512
GRID_W = 64
FFT_W = 256
POOL_W = 256
POOL_WINDOWS = (2, 4, 8, 16)
POOL_GROUP = 64
SGU_W = 256
SGU_HEADS = 4
CHUNK = 128
HEAD_DIM = 64
ATTN_W = 256
N_HEADS = 4
N_KV_HEADS = 2
KV_W = 128
IN_W = 1536
ROPE_THETA = 10000.0
N_GROUPS = 4
EXPERTS_PER_GROUP = 8
N_EXPERTS = 32
EXPERT_FF = 512
DEEPNORM_ALPHA = float((2 * DEPTH) ** 0.25)
LN_EPS = 1e-5
RMS_EPS = 1e-6

T_CTX = BATCH * SEQ
T_LAT = DEC_BATCH * DEC_SEQ
T_ALL = T_CTX + T_LAT
SEG = 4096
N_SEG = T_ALL // SEG

TM = 512
POOL_TB = 512
POOL_HALO = 8
FFT_TR = 512
ATT_TQ = 512
ATT_CHUNK = 1024
DFT_SPLIT = 64
MOE_TM = 256
ROW_CHUNK = 8
MOE_ROWS = 2 * T_ALL
MOE_PAD_ROWS = (T_ALL // TM) * N_EXPERTS * (ROW_CHUNK - 1)
MOE_NT = -(-(MOE_ROWS + MOE_PAD_ROWS) // MOE_TM) + N_EXPERTS
DISP_ROWS = 2 * TM + N_EXPERTS * ROW_CHUNK
PLAN_TB = 2048
ROUTE_E0 = 32
VMEM_LIMIT = 56 * 1024 * 1024


def _cparams(sem):
    return pltpu.CompilerParams(dimension_semantics=sem, vmem_limit_bytes=VMEM_LIMIT)


def _split_hi_lo(a):
    hi = a.astype(bf16)
    lo = (a - hi.astype(f32)).astype(bf16)
    return hi, lo


def _dot(a, b):
    return jnp.dot(a, b, preferred_element_type=f32)


def _mod_kernel(c_ref, w_ref, b_ref, o_ref):
    c = c_ref[...]
    s = c * jax.nn.sigmoid(c)
    s_hi, s_lo = _split_hi_lo(s)
    w_hi, w_lo = _split_hi_lo(w_ref[...])
    o_ref[...] = _dot(s_hi, w_hi) + _dot(s_hi, w_lo) + _dot(s_lo, w_hi) + b_ref[...]


def _modulation(cond8, w_mod, b_mod):
    tn = 1536
    return pl.pallas_call(
        _mod_kernel,
        grid=(DEPTH, 6 * D_MODEL // tn),
        in_specs=[
            pl.BlockSpec((8, D_MODEL), lambda l, j: (0, 0)),
            pl.BlockSpec((None, D_MODEL, tn), lambda l, j: (l, 0, j)),
            pl.BlockSpec((None, 1, tn), lambda l, j: (l, 0, j)),
        ],
        out_specs=pl.BlockSpec((None, 8, tn), lambda l, j: (l, 0, j)),
        out_shape=jax.ShapeDtypeStruct((DEPTH, 8, 6 * D_MODEL), f32),
        compiler_params=_cparams(("arbitrary", "arbitrary")),
        name="modulation",
    )(cond8, w_mod, b_mod.reshape(DEPTH, 1, 6 * D_MODEL))


def _head_rms(x, ones_bd, gain):
    ss = _dot((x * x).astype(bf16), ones_bd)
    return x * lax.rsqrt(ss * (1.0 / HEAD_DIM) + RMS_EPS) * gain


def _rope(x, cos, sin_a, sin_b):
    w = x.shape[-1]
    q4 = HEAD_DIM // 4
    return x * cos + pltpu.roll(x, w - q4, 1) * sin_a + pltpu.roll(x, q4, 1) * sin_b


def _dup_half(x, first):
    lane = lax.broadcasted_iota(i32, x.shape, 1)
    r = pltpu.roll(x, HEAD_DIM, 1)
    if first:
        return jnp.where(lane < HEAD_DIM, x, r)
    return jnp.where(lane >= HEAD_DIM, x, r)


def _gelu_tanh(x):
    c = np.sqrt(2.0 / np.pi).astype(np.float32)
    return x * (0.5 * (1.0 + jnp.tanh(c * (x + 0.044715 * (x * x * x)))))


def _inproj_kernel(xc_ref, xl_ref, mod_ref, win_ref, csc_ref, wsgu_ref, bsgu_ref, lng_ref, lnb_ref,
                   qg_ref, kg_ref, cos_ref, sina_ref, sinb_ref, ones_ref,
                   pq_ref, pool_ref, sgu_ref, q_ref, kd_ref, vd_ref, nk_ref, nv_ref):
    x = jnp.where(pl.program_id(0) < T_CTX // TM, xc_ref[...], xl_ref[...])
    mod = mod_ref[...]
    h = (x * (1.0 + mod[1:2]) + mod[0:1]).astype(bf16)
    proj = _dot(h, win_ref[...])

    a = proj[:, 0:FFT_W].astype(bf16)
    pq_ref[...] = _dot(a, csc_ref[...]).astype(bf16)

    pool_ref[...] = proj[:, FFT_W:FFT_W + POOL_W]

    o = FFT_W + POOL_W
    hgu = _gelu_tanh(proj[:, o:o + 2 * SGU_W])
    u = hgu[:, :SGU_W]
    v = hgu[:, SGU_W:]
    mu = jnp.mean(v, axis=-1, keepdims=True)
    vc = v - mu
    var = jnp.mean(vc * vc, axis=-1, keepdims=True)
    v = vc * lax.rsqrt(var + LN_EPS) * lng_ref[...] + lnb_ref[...]
    lane = lax.broadcasted_iota(i32, (CHUNK, SGU_W), 1)
    head = lane // (SGU_W // SGU_HEADS)
    wcat = wsgu_ref[...]
    for cidx in range(TM // CHUNK):
        rows = slice(cidx * CHUNK, (cidx + 1) * CHUNK)
        vch = v[rows]
        vblk = jnp.concatenate(
            [jnp.where(head == g, vch, 0.0) for g in range(SGU_HEADS)], axis=0).astype(bf16)
        sp = _dot(wcat, vblk) + bsgu_ref[...]
        sgu_ref[rows, :] = (u[rows] * sp).astype(bf16)

    o = o + 2 * SGU_W
    ones_bd = ones_ref[...]
    cos = cos_ref[...]
    sin_a = sina_ref[...]
    sin_b = sinb_ref[...]
    q = _head_rms(proj[:, o:o + ATTN_W], ones_bd, qg_ref[...])
    q = _rope(q, cos, sin_a, sin_b) * np.float32(HEAD_DIM ** -0.5 * np.log2(np.e))
    q_ref[...] = q.astype(bf16)
    o = o + ATTN_W
    k = _head_rms(proj[:, o:o + KV_W], ones_bd[:KV_W, :KV_W], kg_ref[...])
    nk_ref[...] = k
    k = _rope(k, cos[:, :KV_W], sin_a[:, :KV_W], sin_b[:, :KV_W])
    kd_ref[0] = _dup_half(k, True).astype(bf16)
    kd_ref[1] = _dup_half(k, False).astype(bf16)
    o = o + KV_W
    vv = proj[:, o:o + KV_W]
    nv_ref[...] = vv
    vd_ref[0] = _dup_half(vv, True).astype(bf16)
    vd_ref[1] = _dup_half(vv, False).astype(bf16)


def _rope_block(i):
    nlat = DEC_SEQ // TM
    nctx = T_CTX // TM
    return jnp.where(i < nctx, nlat, (i - nctx) % nlat)


def _ctx_tile(wd):
    return pl.BlockSpec((TM, wd), lambda i, *_: (jnp.minimum(i, T_CTX // TM - 1), 0))


def _lat_tile(wd):
    return pl.BlockSpec((TM, wd), lambda i, *_: (jnp.maximum(i - T_CTX // TM, 0), 0))


def _inproj(x_ctx, x_lat, mod, l, w):
    nt = T_ALL // TM
    tile = lambda wd: pl.BlockSpec((TM, wd), lambda i: (i, 0))
    const = lambda shape: pl.BlockSpec(shape, lambda i: (0,) * len(shape))
    rope_spec = pl.BlockSpec((TM, ATTN_W), lambda i: (_rope_block(i), 0))
    return pl.pallas_call(
        _inproj_kernel,
        grid=(nt,),
        in_specs=[
            _ctx_tile(D_MODEL), _lat_tile(D_MODEL),
            pl.BlockSpec((None, None, 6, D_MODEL), lambda i: (l, i // (SEG // TM), 0, 0)),
            pl.BlockSpec((None, D_MODEL, IN_W), lambda i: (l, 0, 0)),
            const((FFT_W, 2 * FFT_W)),
            pl.BlockSpec((None, CHUNK, SGU_HEADS * CHUNK), lambda i: (l, 0, 0)),
            pl.BlockSpec((None, CHUNK, SGU_W), lambda i: (l, 0, 0)),
            pl.BlockSpec((None, 1, SGU_W), lambda i: (l, 0, 0)),
            pl.BlockSpec((None, 1, SGU_W), lambda i: (l, 0, 0)),
            pl.BlockSpec((None, 1, ATTN_W), lambda i: (l, 0, 0)),
            pl.BlockSpec((None, 1, KV_W), lambda i: (l, 0, 0)),
            rope_spec, rope_spec, rope_spec,
            const((ATTN_W, ATTN_W)),
        ],
        out_specs=[
            tile(2 * FFT_W), tile(POOL_W), tile(SGU_W), tile(ATTN_W),
            pl.BlockSpec((N_KV_HEADS, TM, KV_W), lambda i: (0, i, 0)),
            pl.BlockSpec((N_KV_HEADS, TM, KV_W), lambda i: (0, i, 0)),
            tile(KV_W), tile(KV_W),
        ],
        out_shape=[
            jax.ShapeDtypeStruct((T_ALL, 2 * FFT_W), bf16),
            jax.ShapeDtypeStruct((T_ALL, POOL_W), f32),
            jax.ShapeDtypeStruct((T_ALL, SGU_W), bf16),
            jax.ShapeDtypeStruct((T_ALL, ATTN_W), bf16),
            jax.ShapeDtypeStruct((N_KV_HEADS, T_ALL, KV_W), bf16),
            jax.ShapeDtypeStruct((N_KV_HEADS, T_ALL, KV_W), bf16),
            jax.ShapeDtypeStruct((T_ALL, KV_W), f32),
            jax.ShapeDtypeStruct((T_ALL, KV_W), f32),
        ],
        compiler_params=_cparams(("arbitrary",)),
        name="inproj",
    )(x_ctx, x_lat, mod, w["w_in"], w["csc"], w["w_sgu"], w["b_sgu"], w["sgu_ln_g"], w["sgu_ln_b"],
      w["q_norm_g"], w["k_norm_g"], w["rope_cos"], w["rope_sin_a"], w["rope_sin_b"], w["ones_bd"])


def _pool_kernel(prev_ref, cur_ref, next_ref, wp_ref, scale_ref, o_ref):
    i = pl.program_id(0)
    n = jnp.where(i < T_CTX // POOL_TB, SEQ, DEC_SEQ)
    hl = POOL_HALO
    ext = jnp.concatenate([prev_ref[POOL_TB - hl:, :], cur_ref[...], next_ref[:hl, :]], axis=0)
    rows = POOL_TB + 2 * hl
    r = lax.broadcasted_iota(i32, (rows, 1), 0)
    pos = (i * POOL_TB + r - hl) & (n - 1)

    def back(a, s):
        return jnp.where(pos >= s, pltpu.roll(a, s, 0), 0.0)

    def fwd(a, s):
        return jnp.where(pos + s < n, pltpu.roll(a, rows - s, 0), 0.0)

    bsum = [back(ext, 1)]
    fsum = [ext]
    for k in range(3):
        s = 1 << k
        bsum.append(bsum[k] + back(bsum[k], s))
        fsum.append(fsum[k] + fwd(fsum[k], s))
    lane = lax.broadcasted_iota(i32, (1, POOL_W), 1)
    grp = lane // POOL_GROUP
    win = bsum[3] + fsum[3]
    half = jnp.full((1, POOL_W), POOL_WINDOWS[3] // 2, i32)
    for g in (2, 1, 0):
        win = jnp.where(grp == g, bsum[g] + fsum[g], win)
        half = jnp.where(grp == g, POOL_WINDOWS[g] // 2, half)
    cnt = (jnp.minimum(pos + half, n) - jnp.maximum(pos - half, 0)).astype(f32)
    y = (win / cnt - ext)[hl:hl + POOL_TB]
    o_ref[...] = (_dot(y.astype(bf16), wp_ref[...]) * scale_ref[...]).astype(bf16)


def _pool(p, l, w):
    nt = T_ALL // POOL_TB
    blk = lambda f: pl.BlockSpec((POOL_TB, POOL_W), lambda i: (f(i), 0))
    return pl.pallas_call(
        _pool_kernel,
        grid=(nt,),
        in_specs=[
            blk(lambda i: jnp.maximum(i - 1, 0)), blk(lambda i: i),
            blk(lambda i: jnp.minimum(i + 1, nt - 1)),
            pl.BlockSpec((None, POOL_W, POOL_W), lambda i: (l, 0, 0)),
            pl.BlockSpec((None, 1, POOL_W), lambda i: (l, 0, 0)),
        ],
        out_specs=blk(lambda i: i),
        out_shape=jax.ShapeDtypeStruct((T_ALL, POOL_W), bf16),
        compiler_params=_cparams(("arbitrary",)),
        name="pool",
    )(p, p, p, w["w_pool_bd"], w["pool_scale"])


def _seqdft_kernel(*refs, n, nseq):
    m_ref, pq_refs, w_ref, o_ref = refs[0], refs[1:-2], refs[-2], refs[-1]
    per_blk = SEG // n
    for b in range(nseq):
        @pl.when(pl.program_id(1) == b)
        def _(b=b):
            pq_ref = pq_refs[b // per_blk]
            r0 = (b % per_blk) * n
            f = (_dot(m_ref[:, :n], pq_ref[r0:r0 + n, :FFT_W])
                 + _dot(m_ref[:, n:], pq_ref[r0:r0 + n, FFT_W:]))
            o_ref[...] = _dot(f.astype(bf16), w_ref[...]).astype(bf16)


def _seqdft(pq, m, l, w, *, n, tr, nseq, row0):
    nr = n // tr
    nblk = nseq * n // SEG
    pq_specs = [pl.BlockSpec((SEG, 2 * FFT_W), lambda i, b, j=j: (row0 // SEG + j, 0))
                for j in range(nblk)]
    return pl.pallas_call(
        functools.partial(_seqdft_kernel, n=n, nseq=nseq),
        grid=(nr, nseq),
        in_specs=[pl.BlockSpec((tr, 2 * n), lambda i, b: (i, 0))] + pq_specs
        + [pl.BlockSpec((None, FFT_W, FFT_W), lambda i, b: (l, 0, 0))],
        out_specs=pl.BlockSpec((tr, FFT_W), lambda i, b: (b * nr + i, 0)),
        out_shape=jax.ShapeDtypeStruct((nseq * n, FFT_W), bf16),
        compiler_params=_cparams(("arbitrary", "arbitrary")),
        name="seqdft_%d" % n,
    )(m, *([pq] * nblk), w["w_fft"])


def _attn_kernel(*refs, has_cache):
    if has_cache:
        q_ref, k_ref, v_ref, kc_ref, vc_ref, o_ref = refs
    else:
        q_ref, k_ref, v_ref, o_ref = refs
    q = q_ref[...]
    tq = q.shape[0]
    lane = lax.broadcasted_iota(i32, q.shape, 1)
    zero = jnp.zeros_like(q)
    qs = jnp.concatenate([jnp.where(lane < HEAD_DIM, q, zero),
                          jnp.where(lane >= HEAD_DIM, q, zero)], axis=0)
    nt = (((1,), (1,)), ((), ()))
    n = k_ref.shape[0]
    chunk = min(n, ATT_CHUNK)
    parts = [(k_ref, v_ref, c * chunk, chunk) for c in range(n // chunk)]
    if has_cache:
        parts = [(kc_ref, vc_ref, 0, PAST_LEN)] + parts
    m = jnp.full((2 * tq, 1), -jnp.inf, f32)
    den = jnp.zeros((2 * tq, 1), f32)
    acc = jnp.zeros((2 * tq, 2 * HEAD_DIM), f32)
    for kr, vr, off, size in parts:
        s = lax.dot_general(qs, kr[off:off + size, :], nt, preferred_element_type=f32)
        m_new = jnp.maximum(m, jnp.max(s, axis=-1, keepdims=True))
        alpha = jnp.exp2(m - m_new)
        p = jnp.exp2(s - m_new).astype(bf16)
        den = alpha * den + jnp.sum(p.astype(f32), axis=-1, keepdims=True)
        acc = alpha * acc + _dot(p, vr[off:off + size, :])
        m = m_new
    out = acc / den
    o_ref[...] = jnp.where(lane < HEAD_DIM, out[:tq], out[tq:]).astype(bf16)


def _attention(q, kd, vd, cache, *, n, tq, nseq, row0):
    nq = n // tq
    b0 = row0 // n
    q0 = row0 // tq
    in_specs = [
        pl.BlockSpec((tq, 2 * HEAD_DIM), lambda b, h, i: (q0 + b * nq + i, h)),
        pl.BlockSpec((None, n, KV_W), lambda b, h, i: (h, b0 + b, 0)),
        pl.BlockSpec((None, n, KV_W), lambda b, h, i: (h, b0 + b, 0)),
    ]
    args = [q, kd, vd]
    if cache is not None:
        cspec = pl.BlockSpec((None, None, PAST_LEN, KV_W), lambda b, h, i: (h, b, 0, 0))
        in_specs += [cspec, cspec]
        args += list(cache)
    return pl.pallas_call(
        functools.partial(_attn_kernel, has_cache=cache is not None),
        grid=(nseq, N_KV_HEADS, nq),
        in_specs=in_specs,
        out_specs=pl.BlockSpec((tq, 2 * HEAD_DIM), lambda b, h, i: (b * nq + i, h)),
        out_shape=jax.ShapeDtypeStruct((nseq * n, ATTN_W), bf16),
        compiler_params=_cparams(("arbitrary", "arbitrary", "arbitrary")),
        name="attention_%d" % n,
    )(*args)


def _layer_norm(x, g, b):
    mu = jnp.mean(x, axis=-1, keepdims=True)
    xc = x - mu
    var = jnp.mean(xc * xc, axis=-1, keepdims=True)
    return xc * lax.rsqrt(var + LN_EPS) * g + b


def _outproj_kernel(xc_ref, xl_ref, mod_ref, fc_ref, fl_ref, p_ref, s_ref, ac_ref, al_ref, wout_ref,
                    g_ref, b_ref, wr_ref, br_ref, tril_ref,
                    x1_ref, h2_ref, route_ref, cnt_ref, tab_ref, carry_ref):
    i = pl.program_id(0)

    @pl.when(i == 0)
    def _():
        carry_ref[...] = jnp.zeros_like(carry_ref)

    mod = mod_ref[...]
    is_ctx = i < T_CTX // TM
    f_mix = jnp.where(is_ctx, fc_ref[...], fl_ref[...])
    a_mix = jnp.where(is_ctx, ac_ref[...], al_ref[...])
    mix = _dot(jnp.concatenate([f_mix, p_ref[...], s_ref[...], a_mix], axis=1), wout_ref[...])
    x = jnp.where(is_ctx, xc_ref[...], xl_ref[...])
    x1 = _layer_norm(DEEPNORM_ALPHA * x + mod[2:3] * mix, g_ref[...], b_ref[...])
    x1_ref[...] = x1
    h2 = x1 * (1.0 + mod[4:5]) + mod[3:4]
    h2_ref[...] = h2

    h_hi, h_lo = _split_hi_lo(h2)
    hw = _dot(h_hi, wr_ref[...])
    logits = hw[:, :128] + hw[:, 128:] + _dot(h_lo, wr_ref[:, :128]) + br_ref[...]
    lane = lax.broadcasted_iota(i32, logits.shape, 1).astype(f32)
    neg = jnp.float32(-jnp.inf)
    big = jnp.float32(1 << 20)
    gl = jnp.where(lane < N_GROUPS, logits, neg)
    gmax = jnp.max(gl, axis=-1, keepdims=True)
    gsel = jnp.min(jnp.where(gl == gmax, lane, big), axis=-1, keepdims=True)
    pg = 1.0 / jnp.sum(jnp.exp(gl - gmax), axis=-1, keepdims=True)
    e_lo = ROUTE_E0 + gsel * EXPERTS_PER_GROUP
    el = jnp.where((lane >= e_lo) & (lane < e_lo + EXPERTS_PER_GROUP), logits, neg)
    v1 = jnp.max(el, axis=-1, keepdims=True)
    i1 = jnp.min(jnp.where(el == v1, lane, big), axis=-1, keepdims=True)
    el2 = jnp.where(lane == i1, neg, el)
    v2 = jnp.max(el2, axis=-1, keepdims=True)
    i2 = jnp.min(jnp.where(el2 == v2, lane, big), axis=-1, keepdims=True)
    e2 = jnp.exp(v2 - v1)
    w1 = pg / (1.0 + e2)
    w2 = pg * e2 / (1.0 + e2)
    oh1 = lane == i1
    oh2 = lane == i2
    oh = jnp.where(oh1 | oh2, 1.0, 0.0)
    lrank = _dot(tril_ref[...], oh.astype(bf16))
    seg = jnp.floor((jnp.sum(oh, axis=0, keepdims=True) + (ROW_CHUNK - 1.0)) * (1.0 / ROW_CHUNK)) * ROW_CHUNK
    seg8 = jnp.broadcast_to(seg, (8, 128))
    lane8 = lax.broadcasted_iota(i32, (8, 128), 1)
    off8 = seg8
    for sh in (1, 2, 4, 8, 16):
        off8 = off8 + jnp.where(lane8 >= sh, pltpu.roll(off8, sh, 1), 0.0)
    off8 = off8 - seg8
    carry = carry_ref[...]
    lpos = lrank + off8[0:1, :]
    rank = lrank + carry[0:1, :]
    pick = lambda sel, val: jnp.sum(jnp.where(sel, val, 0.0), axis=-1, keepdims=True)
    sub8 = lax.broadcasted_iota(i32, (8, 128), 0)
    tab_ref[...] = jnp.where(sub8 == 0, seg8, jnp.where(sub8 == 1, off8, jnp.where(sub8 == 2, carry, 0.0)))
    carry = carry + seg8
    carry_ref[...] = carry
    cnt_ref[...] = carry
    cols = (i1 - ROUTE_E0, i2 - ROUTE_E0, w1, w2, pick(oh1, rank), pick(oh2, rank),
            pick(oh1, lpos), pick(oh2, lpos))
    route = jnp.zeros_like(logits)
    for j, col in enumerate(cols):
        route = jnp.where(lane == j, col, route)
    route_ref[...] = route


def _outproj(x_ctx, x_lat, mod, fo_ctx, fo_lat, po, so, ao_ctx, ao_lat, l, w):
    nt = T_ALL // TM
    tile = lambda wd: pl.BlockSpec((TM, wd), lambda i: (i, 0))
    vec = lambda wd: pl.BlockSpec((None, 1, wd), lambda i: (l, 0, 0))
    return pl.pallas_call(
        _outproj_kernel,
        grid=(nt,),
        in_specs=[
            _ctx_tile(D_MODEL), _lat_tile(D_MODEL),
            pl.BlockSpec((None, None, 6, D_MODEL), lambda i: (l, i // (SEG // TM), 0, 0)),
            _ctx_tile(FFT_W), _lat_tile(FFT_W), tile(POOL_W), tile(SGU_W),
            _ctx_tile(ATTN_W), _lat_tile(ATTN_W),
            pl.BlockSpec((None, D_MODEL, D_MODEL), lambda i: (l, 0, 0)),
            vec(D_MODEL), vec(D_MODEL),
            pl.BlockSpec((None, D_MODEL, 256), lambda i: (l, 0, 0)),
            vec(128),
            pl.BlockSpec((TM, TM), lambda i: (0, 0)),
        ],
        out_specs=[tile(D_MODEL), tile(D_MODEL), tile(128), pl.BlockSpec((8, 128), lambda i: (0, 0)),
                   pl.BlockSpec((None, 8, 128), lambda i: (i, 0, 0))],
        out_shape=[
            jax.ShapeDtypeStruct((T_ALL, D_MODEL), f32),
            jax.ShapeDtypeStruct((T_ALL, D_MODEL), f32),
            jax.ShapeDtypeStruct((T_ALL, 128), f32),
            jax.ShapeDtypeStruct((8, 128), f32),
            jax.ShapeDtypeStruct((nt, 8, 128), f32),
        ],
        scratch_shapes=[pltpu.VMEM((8, 128), f32)],
        compiler_params=_cparams(("arbitrary",)),
        name="outproj",
    )(x_ctx, x_lat, mod, fo_ctx, fo_lat, po, so, ao_ctx, ao_lat,
      w["w_out"], w["ln1_g"], w["ln1_b"], w["w_r"], w["b_r"],
      w["tril"])


def _plan_kernel(route_ref, cnt_ref, pos_ref, meta_ref):
    lane = lax.broadcasted_iota(i32, (8, 128), 1)
    sub = lax.broadcasted_iota(i32, (8, 128), 0)
    cnt = cnt_ref[...]
    is_e = (lane >= ROUTE_E0) & (lane < ROUTE_E0 + N_EXPERTS)
    tiles = jnp.where(is_e, jnp.floor((cnt + (MOE_TM - 1.0)) * (1.0 / MOE_TM)), 0.0)
    cum = tiles
    for s in (1, 2, 4, 8, 16):
        cum = cum + jnp.where(lane >= s, pltpu.roll(cum, s, 1), 0.0)
    pstart = (cum - tiles) * MOE_TM
    nused = jnp.max(cum, axis=-1, keepdims=True)
    fill = jnp.where(is_e & (cnt != tiles * MOE_TM), pstart + (tiles - 1.0) * MOE_TM, -1.0)
    meta = jnp.where(sub == 0, cnt, jnp.where(sub == 1, nused, jnp.where(sub == 2, fill,
                     jnp.where(sub == 3, pstart, 0.0))))
    meta_ref[...] = meta.astype(i32)

    r = route_ref[...]
    lane_t = lax.broadcasted_iota(i32, r.shape, 1).astype(f32)
    ps = pstart[0:1, :]

    def dest(ecol, rcol):
        hit = lane_t == (r[:, ecol:ecol + 1] + ROUTE_E0)
        return jnp.sum(jnp.where(hit, ps, 0.0), axis=-1, keepdims=True) + r[:, rcol:rcol + 1]

    pos = jnp.where(lane_t == 0, dest(0, 4), jnp.where(lane_t == 1, dest(1, 5), 0.0))
    pos_ref[...] = pos.astype(i32)


def _plan(route, cnt):
    return pl.pallas_call(
        _plan_kernel,
        grid=(T_ALL // PLAN_TB,),
        in_specs=[pl.BlockSpec((PLAN_TB, 128), lambda i: (i, 0)),
                  pl.BlockSpec((8, 128), lambda i: (0, 0))],
        out_specs=[pl.BlockSpec((PLAN_TB, 128), lambda i: (i, 0)),
                   pl.BlockSpec((8, 128), lambda i: (0, 0))],
        out_shape=[jax.ShapeDtypeStruct((T_ALL, 128), i32), jax.ShapeDtypeStruct((8, 128), i32)],
        compiler_params=_cparams(("arbitrary",)),
        name="plan",
    )(route, cnt)


def _dispatch_kernel(nch_ref, off_ref, dst_ref, tot_ref, fill_ref, nused_ref, h_ref, route_ref, xs_ref,
                     sorted_ref, zero_ref, sem, fill_sem):
    i = pl.program_id(0)

    def tile_fill(row0):
        return pltpu.make_async_copy(zero_ref, xs_ref.at[pl.ds(pl.multiple_of(row0, MOE_TM), MOE_TM)],
                                     fill_sem)

    @pl.when(i == 0)
    def _():
        zero_ref[...] = jnp.zeros_like(zero_ref)

        def start(e, c):
            @pl.when(fill_ref[e] >= 0)
            def _():
                tile_fill(jnp.maximum(fill_ref[e], 0)).start()
            return c

        def wait(e, c):
            @pl.when(fill_ref[e] >= 0)
            def _():
                tile_fill(jnp.maximum(fill_ref[e], 0)).wait()
            return c

        def start_tail(t, c):
            tile_fill(t * MOE_TM).start()
            return c

        def wait_tail(t, c):
            tile_fill(t * MOE_TM).wait()
            return c

        lax.fori_loop(0, N_EXPERTS, start, 0)
        lax.fori_loop(nused_ref[0], MOE_NT, start_tail, 0)
        lax.fori_loop(0, N_EXPERTS, wait, 0)
        lax.fori_loop(nused_ref[0], MOE_NT, wait_tail, 0)

    rt = route_ref[...].T
    j = lax.broadcasted_iota(i32, (DISP_ROWS, 1), 0).astype(f32)
    sel = jnp.where((j == rt[6:7, :]) | (j == rt[7:8, :]), 1.0, 0.0).astype(bf16)
    sorted_ref[...] = _dot(sel, h_ref[...].astype(bf16))

    def per_expert(e, c):
        idx = i * N_EXPERTS + e
        n, s0, d0 = nch_ref[idx], off_ref[idx], dst_ref[idx]
        b = 1
        while b <= TM // ROW_CHUNK:
            @pl.when((n & b) != 0)
            def _(b=b):
                r0 = (n & (b - 1)) * ROW_CHUNK
                rows = b * ROW_CHUNK
                pltpu.make_async_copy(
                    sorted_ref.at[pl.ds(pl.multiple_of(s0 + r0, ROW_CHUNK), rows)],
                    xs_ref.at[pl.ds(pl.multiple_of(d0 + r0, ROW_CHUNK), rows)], sem).start()
            b *= 2
        return c

    lax.fori_loop(0, N_EXPERTS, per_expert, 0)

    rows = tot_ref[i] * ROW_CHUNK

    @pl.when(rows > 0)
    def _():
        pltpu.make_async_copy(sorted_ref.at[pl.ds(0, rows)], xs_ref.at[pl.ds(0, rows)], sem).wait()


def _dispatch(nch, off, dst, tot, fill, nused, h2, route):
    grid_spec = pltpu.PrefetchScalarGridSpec(
        num_scalar_prefetch=6,
        grid=(T_ALL // TM,),
        in_specs=[pl.BlockSpec((TM, D_MODEL), lambda i, *_: (i, 0)),
                  pl.BlockSpec((TM, 128), lambda i, *_: (i, 0))],
        out_specs=pl.BlockSpec(memory_space=pl.ANY),
        scratch_shapes=[pltpu.VMEM((DISP_ROWS, D_MODEL), f32), pltpu.VMEM((MOE_TM, D_MODEL), f32),
                        pltpu.SemaphoreType.DMA(()), pltpu.SemaphoreType.DMA(())],
    )
    return pl.pallas_call(
        _dispatch_kernel,
        grid_spec=grid_spec,
        out_shape=jax.ShapeDtypeStruct((MOE_NT * MOE_TM, D_MODEL), f32),
        compiler_params=_cparams(("arbitrary",)),
        name="dispatch",
    )(nch, off, dst, tot, fill, nused, h2, route)


def _experts_kernel(cnt_ref, nused_ref, xs_ref, wg_hbm, wu_hbm, wd_hbm, ys_ref,
                    wg_f, wu_f, wd_f, wg_b, wu_b, wd_b, st, wsem, *, layer):
    i = pl.program_id(0)
    nused = nused_ref[0]
    NXT, NSLOT, LEFT = 0, 1, 2

    def w_copies(e, slot):
        return (pltpu.make_async_copy(wg_hbm.at[layer, e], wg_f.at[slot], wsem.at[slot, 0]),
                pltpu.make_async_copy(wu_hbm.at[layer, e], wu_f.at[slot], wsem.at[slot, 1]),
                pltpu.make_async_copy(wd_hbm.at[layer, e], wd_f.at[slot], wsem.at[slot, 2]))

    def next_nonempty(e):
        return lax.while_loop(
            lambda v: (v < N_EXPERTS) & (cnt_ref[jnp.minimum(v, N_EXPERTS - 1)] == 0),
            lambda v: v + 1, e)

    @pl.when(i == 0)
    def _():
        e0 = next_nonempty(jnp.int32(0))
        for c in w_copies(e0, 0):
            c.start()
        st[NXT] = e0
        st[NSLOT] = 0
        st[LEFT] = 0

    @pl.when(i < nused)
    def _():
        @pl.when(st[LEFT] == 0)
        def _():
            e = st[NXT]
            slot = st[NSLOT]
            for c in w_copies(e, slot):
                c.wait()
            e2 = next_nonempty(e + 1)

            @pl.when(e2 < N_EXPERTS)
            def _():
                for c in w_copies(e2, 1 - slot):
                    c.start()

            st[NXT] = e2
            st[NSLOT] = 1 - slot
            st[LEFT] = (cnt_ref[e] + (MOE_TM - 1)) // MOE_TM
            wg_b[...] = wg_f[slot].astype(bf16)
            wu_b[...] = wu_f[slot].astype(bf16)
            wd_b[...] = wd_f[slot].astype(bf16)

        x = xs_ref[...].astype(bf16)
        hg = _dot(x, wg_b[...])
        hu = _dot(x, wu_b[...])
        act = (hg * jax.nn.sigmoid(hg)) * hu
        ys_ref[...] = _dot(act.astype(bf16), wd_b[...])
        st[LEFT] = st[LEFT] - 1

    @pl.when(i >= nused)
    def _():
        ys_ref[...] = jnp.zeros_like(ys_ref)


def _experts(counts, nused, xs, l, w_gate, w_up, w_down):
    hbm = pl.BlockSpec(memory_space=pl.ANY)
    grid_spec = pltpu.PrefetchScalarGridSpec(
        num_scalar_prefetch=2,
        grid=(MOE_NT,),
        in_specs=[pl.BlockSpec((MOE_TM, D_MODEL), lambda i, c, nu: (jnp.minimum(i, nu[0] - 1), 0)),
                  hbm, hbm, hbm],
        out_specs=pl.BlockSpec((MOE_TM, D_MODEL), lambda i, c, nu: (i, 0)),
        scratch_shapes=[
            pltpu.VMEM((2, D_MODEL, EXPERT_FF), f32),
            pltpu.VMEM((2, D_MODEL, EXPERT_FF), f32),
            pltpu.VMEM((2, EXPERT_FF, D_MODEL), f32),
            pltpu.VMEM((D_MODEL, EXPERT_FF), bf16),
            pltpu.VMEM((D_MODEL, EXPERT_FF), bf16),
            pltpu.VMEM((EXPERT_FF, D_MODEL), bf16),
            pltpu.SMEM((4,), i32),
            pltpu.SemaphoreType.DMA((2, 3)),
        ],
    )
    return pl.pallas_call(
        functools.partial(_experts_kernel, layer=l),
        grid_spec=grid_spec,
        out_shape=jax.ShapeDtypeStruct((MOE_NT * MOE_TM, D_MODEL), f32),
        compiler_params=_cparams(("arbitrary",)),
        name="experts",
    )(counts, nused, xs, w_gate, w_up, w_down)


def _combine_kernel(pos_ref, x1_ref, mod_ref, route_ref, ys_hbm, g_ref, b_ref, oc_ref, ol_ref, ybuf, sem):
    i = pl.program_id(0)
    nt = pl.num_programs(0) - 1

    @pl.when(i < nt)
    def _():
        slot = i % 2
        for k in range(2):
            base = k * T_ALL + i * TM
            for j in range(TM):
                pltpu.make_async_copy(ys_hbm.at[pl.ds(pos_ref[base + j], 1)],
                                      ybuf.at[slot, k, pl.ds(j, 1)], sem.at[slot]).start()

    @pl.when(i >= 1)
    def _():
        slot = (i - 1) % 2
        for k in range(2):
            pltpu.make_async_copy(ys_hbm.at[pl.ds(0, TM)], ybuf.at[slot, k], sem.at[slot]).wait()
        route = route_ref[...]
        mod = mod_ref[...]
        moe = route[:, 2:3] * ybuf[slot, 0] + route[:, 3:4] * ybuf[slot, 1]
        y = _layer_norm(DEEPNORM_ALPHA * x1_ref[...] + mod[5:6] * moe, g_ref[...], b_ref[...])

        @pl.when(i - 1 < T_CTX // TM)
        def _():
            oc_ref[...] = y

        @pl.when(i - 1 >= T_CTX // TM)
        def _():
            ol_ref[...] = y


def _combine(pos_flat, x1, mod, route, ys, l, w):
    nt = T_ALL // TM
    vec = pl.BlockSpec((None, 1, D_MODEL), lambda i, p: (l, 0, 0))
    nctx = T_CTX // TM
    prev = lambda i: jnp.maximum(i - 1, 0)
    grid_spec = pltpu.PrefetchScalarGridSpec(
        num_scalar_prefetch=1,
        grid=(nt + 1,),
        in_specs=[
            pl.BlockSpec((TM, D_MODEL), lambda i, p: (prev(i), 0)),
            pl.BlockSpec((None, None, 6, D_MODEL), lambda i, p: (l, prev(i) // (SEG // TM), 0, 0)),
            pl.BlockSpec((TM, 128), lambda i, p: (prev(i), 0)),
            pl.BlockSpec(memory_space=pl.ANY),
            vec, vec,
        ],
        out_specs=[pl.BlockSpec((TM, D_MODEL), lambda i, p: (jnp.minimum(prev(i), nctx - 1), 0)),
                   pl.BlockSpec((TM, D_MODEL), lambda i, p: (jnp.maximum(prev(i) - nctx, 0), 0))],
        scratch_shapes=[pltpu.VMEM((2, 2, TM, D_MODEL), f32), pltpu.SemaphoreType.DMA((2,))],
    )
    return pl.pallas_call(
        _combine_kernel,
        grid_spec=grid_spec,
        out_shape=[jax.ShapeDtypeStruct((T_CTX, D_MODEL), f32),
                   jax.ShapeDtypeStruct((T_LAT, D_MODEL), f32)],
        compiler_params=_cparams(("arbitrary",)),
        name="combine",
    )(pos_flat, x1, mod, route, ys, w["ln2_g"], w["ln2_b"])


def _dft_cos_sin(n, scale):
    k = jnp.arange(n, dtype=i32)
    ang = ((k[:, None] * k[None, :]) % n).astype(f32) * np.float32(2.0 * np.pi / n)
    return jnp.cos(ang) * scale, jnp.sin(ang) * scale


def _seq_dft_matrix(n):
    g = min(DFT_SPLIT, n)
    j = jnp.arange(n, dtype=i32)[None, :]
    k1 = jnp.arange(n // g, dtype=i32)[:, None]
    k2 = jnp.arange(g, dtype=i32)[:, None]
    ang_a = ((k1 * j) % (n // g)).astype(f32) * np.float32(2.0 * np.pi * g / n)
    ang_b = ((k2 * j) % n).astype(f32) * np.float32(2.0 * np.pi / n)
    scale = np.float32(n ** -0.5)
    ca, sa = jnp.cos(ang_a), jnp.sin(ang_a)
    cb, sb = jnp.cos(ang_b) * scale, jnp.sin(ang_b) * scale
    ca2 = jnp.concatenate([ca, ca], axis=1)[:, None, :]
    sa2 = jnp.concatenate([sa, sa], axis=1)[:, None, :]
    cb2 = jnp.concatenate([cb, -sb], axis=1)[None, :, :]
    sb2 = jnp.concatenate([sb, cb], axis=1)[None, :, :]
    return (ca2 * cb2 - sa2 * sb2).astype(bf16).reshape(n, 2 * n)


def _rope_tables():
    rows = DEC_SEQ // GRID_W
    row = jnp.repeat(jnp.arange(rows), GRID_W).astype(f32)
    col = jnp.tile(jnp.arange(GRID_W), rows).astype(f32)
    n_freq = HEAD_DIM // 4
    inv = ROPE_THETA ** (-jnp.arange(n_freq, dtype=f32) / n_freq)
    ar = row[:, None] * inv
    ac = col[:, None] * inv
    ang = jnp.concatenate([ar, ar, ac, ac], axis=-1)
    cos = jnp.tile(jnp.cos(ang), (1, N_HEADS))
    sin = jnp.tile(jnp.sin(ang), (1, N_HEADS))
    first = (jnp.arange(ATTN_W) % (HEAD_DIM // 2)) < n_freq
    sin_a = jnp.where(first[None, :], -sin, 0.0)
    sin_b = jnp.where(first[None, :], 0.0, sin)
    ident = jnp.zeros((TM, ATTN_W), f32)
    return (jnp.concatenate([cos, ident + 1.0], axis=0),
            jnp.concatenate([sin_a, ident], axis=0),
            jnp.concatenate([sin_b, ident], axis=0))


def _dup_cache(cache):
    c = jnp.transpose(cache, (1, 3, 0, 2, 4))
    return jnp.concatenate([c, c], axis=-1).astype(bf16)


def kernel(x_prompt, x_sample, cache_k, cache_v, c, c_ctx, w_mod, b_mod, w_in, w_fft, w_pool, pool_scale, sgu_ln_g, sgu_ln_b, w_sgu, b_sgu, q_norm_g, k_norm_g, w_out, ln1_g, ln1_b, w_router_group, b_router_group, w_router_expert, b_router_expert, w_gate, w_up, w_down, ln2_g, ln2_b):
    L = DEPTH
    x_ctx = x_prompt.reshape(T_CTX, D_MODEL)
    x_lat = x_sample.reshape(T_LAT, D_MODEL)

    cond8 = jnp.concatenate([c_ctx[None, :], c, jnp.zeros((8 - 1 - DEC_BATCH, D_MODEL), f32)], axis=0)
    mod = _modulation(cond8, w_mod, b_mod)[:, :N_SEG].reshape(L, N_SEG, 6, D_MODEL)

    cc, sc = _dft_cos_sin(FFT_W, np.float32(FFT_W ** -0.5))
    rope_cos, rope_sin_a, rope_sin_b = _rope_tables()
    head_id = jnp.arange(ATTN_W) // HEAD_DIM
    eye_g = jnp.eye(len(POOL_WINDOWS), dtype=f32)
    w_r = jnp.zeros((L, D_MODEL, 128), f32)
    w_r = w_r.at[:, :, :N_GROUPS].set(w_router_group).at[:, :, ROUTE_E0:ROUTE_E0 + N_EXPERTS].set(w_router_expert)
    b_r = jnp.zeros((L, 1, 128), f32)
    b_r = b_r.at[:, 0, :N_GROUPS].set(b_router_group).at[:, 0, ROUTE_E0:ROUTE_E0 + N_EXPERTS].set(b_router_expert)
    w_r_hi, w_r_lo = _split_hi_lo(w_r)
    w = {
        "w_in": w_in.astype(bf16),
        "csc": jnp.concatenate([cc, sc], axis=1).astype(bf16),
        "w_sgu": jnp.transpose(w_sgu, (0, 2, 1, 3)).reshape(L, CHUNK, SGU_HEADS * CHUNK).astype(bf16),
        "b_sgu": jnp.repeat(jnp.transpose(b_sgu, (0, 2, 1)), SGU_W // SGU_HEADS, axis=2),
        "sgu_ln_g": sgu_ln_g.reshape(L, 1, SGU_W),
        "sgu_ln_b": sgu_ln_b.reshape(L, 1, SGU_W),
        "q_norm_g": jnp.tile(q_norm_g, (1, N_HEADS)).reshape(L, 1, ATTN_W),
        "k_norm_g": jnp.tile(k_norm_g, (1, N_KV_HEADS)).reshape(L, 1, KV_W),
        "rope_cos": rope_cos, "rope_sin_a": rope_sin_a, "rope_sin_b": rope_sin_b,
        "ones_bd": (head_id[:, None] == head_id[None, :]).astype(bf16),
        "w_pool_bd": jnp.einsum("lgcd,gh->lgchd", w_pool, eye_g).reshape(L, POOL_W, POOL_W).astype(bf16),
        "pool_scale": pool_scale.reshape(L, 1, POOL_W),
        "w_fft": w_fft.astype(bf16),
        "w_out": w_out.astype(bf16),
        "ln1_g": ln1_g.reshape(L, 1, D_MODEL), "ln1_b": ln1_b.reshape(L, 1, D_MODEL),
        "ln2_g": ln2_g.reshape(L, 1, D_MODEL), "ln2_b": ln2_b.reshape(L, 1, D_MODEL),
        "w_r": jnp.concatenate([w_r_hi, w_r_lo], axis=-1), "b_r": b_r,
        "tril": (jnp.arange(TM)[:, None] > jnp.arange(TM)[None, :]).astype(bf16),
    }
    m_ctx = _seq_dft_matrix(SEQ)
    m_lat = _seq_dft_matrix(DEC_SEQ)
    kc_all = _dup_cache(cache_k)
    vc_all = _dup_cache(cache_v)

    new_k, new_v = [], []
    for l in range(L):
        pq, praw, sgu, q, kd, vd, nk, nv = _inproj(x_ctx, x_lat, mod, l, w)
        new_k.append(nk[:T_CTX].reshape(BATCH, SEQ, N_KV_HEADS, HEAD_DIM))
        new_v.append(nv[:T_CTX].reshape(BATCH, SEQ, N_KV_HEADS, HEAD_DIM))
        po = _pool(praw, l, w)
        fo_ctx = _seqdft(pq, m_ctx, l, w, n=SEQ, tr=SEQ, nseq=BATCH, row0=0)
        fo_lat = _seqdft(pq, m_lat, l, w, n=DEC_SEQ, tr=FFT_TR, nseq=DEC_BATCH, row0=T_CTX)
        ao_ctx = _attention(q, kd, vd, None, n=SEQ, tq=SEQ, nseq=BATCH, row0=0)
        ao_lat = _attention(q, kd, vd, (kc_all[l], vc_all[l]), n=DEC_SEQ, tq=ATT_TQ, nseq=DEC_BATCH,
                            row0=T_CTX)
        x1, h2, route, cnt, tab = _outproj(x_ctx, x_lat, mod, fo_ctx, fo_lat, po, sgu, ao_ctx, ao_lat, l, w)
        pos_slab, meta = _plan(route, cnt)
        pos_flat = pos_slab[:, :2].T.reshape(-1)
        experts = slice(ROUTE_E0, ROUTE_E0 + N_EXPERTS)
        counts = meta[0, experts]
        nused = meta[1, :1]
        fill = meta[2, experts]
        tab = tab[:, :, experts].astype(i32)
        nch = (tab[:, 0] // ROW_CHUNK).reshape(-1)
        off = tab[:, 1].reshape(-1)
        dst = (meta[3, experts][None, :] + tab[:, 2]).reshape(-1)
        tot = jnp.sum(tab[:, 0], axis=1) // ROW_CHUNK
        xs = _dispatch(nch, off, dst, tot, fill, nused, h2, route)
        ys = _experts(counts, nused, xs, l, w_gate, w_up, w_down)
        x_ctx, x_lat = _combine(pos_flat, x1, mod, route, ys, l, w)

    y_prompt = x_ctx.reshape(BATCH, SEQ, D_MODEL)
    y_sample = x_lat.reshape(DEC_BATCH, DEC_SEQ, D_MODEL)
    return (y_prompt, y_sample, jnp.stack(new_k, axis=1), jnp.stack(new_v, axis=1))
```

```python
import functools

import numpy as np
import jax
import jax.numpy as jnp
from jax import lax
from jax.experimental import pallas as pl
from jax.experimental.pallas import tpu as pltpu

f32 = jnp.float32
bf16 = jnp.bfloat16
i32 = jnp.int32

D_MODEL = 1024
BATCH = 16
SEQ = 256
DEPTH = 4
DEC_BATCH = 2
DEC_SEQ = 4096
PAST_LEN = 512
GRID_W = 64
FFT_W = 256
POOL_W = 256
POOL_WINDOWS = (2, 4, 8, 16)
POOL_GROUP = 64
SGU_W = 256
SGU_HEADS = 4
CHUNK = 128
HEAD_DIM = 64
ATTN_W = 256
N_HEADS = 4
N_KV_HEADS = 2
KV_W = 128
IN_W = 1536
ROPE_THETA = 10000.0
N_GROUPS = 4
EXPERTS_PER_GROUP = 8
N_EXPERTS = 32
EXPERT_FF = 512
DEEPNORM_ALPHA = float((2 * DEPTH) ** 0.25)
LN_EPS = 1e-5
RMS_EPS = 1e-6

T_CTX = BATCH * SEQ
T_LAT = DEC_BATCH * DEC_SEQ
T_ALL = T_CTX + T_LAT
SEG = 4096
N_SEG = T_ALL // SEG

TM = 512
POOL_TB = 512
POOL_HALO = 8
FFT_TR = 512
ATT_TQ = 512
ATT_CHUNK = 1024
DFT_SPLIT = 64
MOE_TM = 256
ROW_CHUNK = 8
MOE_ROWS = 2 * T_ALL
MOE_PAD_ROWS = (T_ALL // TM) * N_EXPERTS * (ROW_CHUNK - 1)
MOE_NT = -(-(MOE_ROWS + MOE_PAD_ROWS) // MOE_TM) + N_EXPERTS
DISP_ROWS = 2 * TM + N_EXPERTS * ROW_CHUNK
PLAN_TB = 2048
ROUTE_E0 = 32
VMEM_LIMIT = 56 * 1024 * 1024


def _cparams(sem):
    return pltpu.CompilerParams(dimension_semantics=sem, vmem_limit_bytes=VMEM_LIMIT)


def _split_hi_lo(a):
    hi = a.astype(bf16)
    lo = (a - hi.astype(f32)).astype(bf16)
    return hi, lo


def _dot(a, b):
    return jnp.dot(a, b, preferred_element_type=f32)


def _mod_kernel(c_ref, w_ref, b_ref, o_ref):
    c = c_ref[...]
    s = c * jax.nn.sigmoid(c)
    s_hi, s_lo = _split_hi_lo(s)
    w_hi, w_lo = _split_hi_lo(w_ref[...])
    o_ref[...] = _dot(s_hi, w_hi) + _dot(s_hi, w_lo) + _dot(s_lo, w_hi) + b_ref[...]


def _modulation(cond8, w_mod, b_mod):
    tn = 1536
    return pl.pallas_call(
        _mod_kernel,
        grid=(DEPTH, 6 * D_MODEL // tn),
        in_specs=[
            pl.BlockSpec((8, D_MODEL), lambda l, j: (0, 0)),
            pl.BlockSpec((None, D_MODEL, tn), lambda l, j: (l, 0, j)),
            pl.BlockSpec((None, 1, tn), lambda l, j: (l, 0, j)),
        ],
        out_specs=pl.BlockSpec((None, 8, tn), lambda l, j: (l, 0, j)),
        out_shape=jax.ShapeDtypeStruct((DEPTH, 8, 6 * D_MODEL), f32),
        compiler_params=_cparams(("arbitrary", "arbitrary")),
        name="modulation",
    )(cond8, w_mod, b_mod.reshape(DEPTH, 1, 6 * D_MODEL))


def _head_rms(x, ones_bd, gain):
    ss = _dot((x * x).astype(bf16), ones_bd)
    return x * lax.rsqrt(ss * (1.0 / HEAD_DIM) + RMS_EPS) * gain


def _rope(x, cos, sin_a, sin_b):
    w = x.shape[-1]
    q4 = HEAD_DIM // 4
    return x * cos + pltpu.roll(x, w - q4, 1) * sin_a + pltpu.roll(x, q4, 1) * sin_b


def _dup_half(x, first):
    lane = lax.broadcasted_iota(i32, x.shape, 1)
    r = pltpu.roll(x, HEAD_DIM, 1)
    if first:
        return jnp.where(lane < HEAD_DIM, x, r)
    return jnp.where(lane >= HEAD_DIM, x, r)


def _gelu_tanh(x):
    c = np.sqrt(2.0 / np.pi).astype(np.float32)
    return x * (0.5 * (1.0 + jnp.tanh(c * (x + 0.044715 * (x * x * x)))))


def _inproj_kernel(xc_ref, xl_ref, mod_ref, win_ref, csc_ref, wsgu_ref, bsgu_ref, lng_ref, lnb_ref,
                   qg_ref, kg_ref, cos_ref, sina_ref, sinb_ref, ones_ref,
                   pq_ref, pool_ref, sgu_ref, q_ref, kd_ref, vd_ref, nk_ref, nv_ref):
    x = jnp.where(pl.program_id(0) < T_CTX // TM, xc_ref[...], xl_ref[...])
    mod = mod_ref[...]
    h = (x * (1.0 + mod[1:2]) + mod[0:1]).astype(bf16)
    proj = _dot(h, win_ref[...])

    a = proj[:, 0:FFT_W].astype(bf16)
    pq_ref[...] = _dot(a, csc_ref[...]).astype(bf16)

    pool_ref[...] = proj[:, FFT_W:FFT_W + POOL_W]

    o = FFT_W + POOL_W
    hgu = _gelu_tanh(proj[:, o:o + 2 * SGU_W])
    u = hgu[:, :SGU_W]
    v = hgu[:, SGU_W:]
    mu = jnp.mean(v, axis=-1, keepdims=True)
    vc = v - mu
    var = jnp.mean(vc * vc, axis=-1, keepdims=True)
    v = vc * lax.rsqrt(var + LN_EPS) * lng_ref[...] + lnb_ref[...]
    lane = lax.broadcasted_iota(i32, (CHUNK, SGU_W), 1)
    head = lane // (SGU_W // SGU_HEADS)
    wcat = wsgu_ref[...]
    for cidx in range(TM // CHUNK):
        rows = slice(cidx * CHUNK, (cidx + 1) * CHUNK)
        vch = v[rows]
        vblk = jnp.concatenate(
            [jnp.where(head == g, vch, 0.0) for g in range(SGU_HEADS)], axis=0).astype(bf16)
        sp = _dot(wcat, vblk) + bsgu_ref[...]
        sgu_ref[rows, :] = (u[rows] * sp).astype(bf16)

    o = o + 2 * SGU_W
    ones_bd = ones_ref[...]
    cos = cos_ref[...]
    sin_a = sina_ref[...]
    sin_b = sinb_ref[...]
    q = _head_rms(proj[:, o:o + ATTN_W], ones_bd, qg_ref[...])
    q = _rope(q, cos, sin_a, sin_b) * np.float32(HEAD_DIM ** -0.5 * np.log2(np.e))
    q_ref[...] = q.astype(bf16)
    o = o + ATTN_W
    k = _head_rms(proj[:, o:o + KV_W], ones_bd[:KV_W, :KV_W], kg_ref[...])
    nk_ref[...] = k
    k = _rope(k, cos[:, :KV_W], sin_a[:, :KV_W], sin_b[:, :KV_W])
    kd_ref[0] = _dup_half(k, True).astype(bf16)
    kd_ref[1] = _dup_half(k, False).astype(bf16)
    o = o + KV_W
    vv = proj[:, o:o + KV_W]
    nv_ref[...] = vv
    vd_ref[0] = _dup_half(vv, True).astype(bf16)
    vd_ref[1] = _dup_half(vv, False).astype(bf16)


def _rope_block(i):
    nlat = DEC_SEQ // TM
    nctx = T_CTX // TM
    return jnp.where(i < nctx, nlat, (i - nctx) % nlat)


def _ctx_tile(wd):
    return pl.BlockSpec((TM, wd), lambda i, *_: (jnp.minimum(i, T_CTX // TM - 1), 0))


def _lat_tile(wd):
    return pl.BlockSpec((TM, wd), lambda i, *_: (jnp.maximum(i - T_CTX // TM, 0), 0))


def _inproj(x_ctx, x_lat, mod, l, w):
    nt = T_ALL // TM
    tile = lambda wd: pl.BlockSpec((TM, wd), lambda i: (i, 0))
    const = lambda shape: pl.BlockSpec(shape, lambda i: (0,) * len(shape))
    rope_spec = pl.BlockSpec((TM, ATTN_W), lambda i: (_rope_block(i), 0))
    return pl.pallas_call(
        _inproj_kernel,
        grid=(nt,),
        in_specs=[
            _ctx_tile(D_MODEL), _lat_tile(D_MODEL),
            pl.BlockSpec((None, None, 6, D_MODEL), lambda i: (l, i // (SEG // TM), 0, 0)),
            pl.BlockSpec((None, D_MODEL, IN_W), lambda i: (l, 0, 0)),
            const((FFT_W, 2 * FFT_W)),
            pl.BlockSpec((None, CHUNK, SGU_HEADS * CHUNK), lambda i: (l, 0, 0)),
            pl.BlockSpec((None, CHUNK, SGU_W), lambda i: (l, 0, 0)),
            pl.BlockSpec((None, 1, SGU_W), lambda i: (l, 0, 0)),
            pl.BlockSpec((None, 1, SGU_W), lambda i: (l, 0, 0)),
            pl.BlockSpec((None, 1, ATTN_W), lambda i: (l, 0, 0)),
            pl.BlockSpec((None, 1, KV_W), lambda i: (l, 0, 0)),
            rope_spec, rope_spec, rope_spec,
            const((ATTN_W, ATTN_W)),
        ],
        out_specs=[
            tile(2 * FFT_W), tile(POOL_W), tile(SGU_W), tile(ATTN_W),
            pl.BlockSpec((N_KV_HEADS, TM, KV_W), lambda i: (0, i, 0)),
            pl.BlockSpec((N_KV_HEADS, TM, KV_W), lambda i: (0, i, 0)),
            tile(KV_W), tile(KV_W),
        ],
        out_shape=[
            jax.ShapeDtypeStruct((T_ALL, 2 * FFT_W), bf16),
            jax.ShapeDtypeStruct((T_ALL, POOL_W), f32),
            jax.ShapeDtypeStruct((T_ALL, SGU_W), bf16),
            jax.ShapeDtypeStruct((T_ALL, ATTN_W), bf16),
            jax.ShapeDtypeStruct((N_KV_HEADS, T_ALL, KV_W), bf16),
            jax.ShapeDtypeStruct((N_KV_HEADS, T_ALL, KV_W), bf16),
            jax.ShapeDtypeStruct((T_ALL, KV_W), f32),
            jax.ShapeDtypeStruct((T_ALL, KV_W), f32),
        ],
        compiler_params=_cparams(("arbitrary",)),
        name="inproj",
    )(x_ctx, x_lat, mod, w["w_in"], w["csc"], w["w_sgu"], w["b_sgu"], w["sgu_ln_g"], w["sgu_ln_b"],
      w["q_norm_g"], w["k_norm_g"], w["rope_cos"], w["rope_sin_a"], w["rope_sin_b"], w["ones_bd"])


def _pool_kernel(prev_ref, cur_ref, next_ref, wp_ref, scale_ref, o_ref):
    i = pl.program_id(0)
    n = jnp.where(i < T_CTX // POOL_TB, SEQ, DEC_SEQ)
    hl = POOL_HALO
    ext = jnp.concatenate([prev_ref[POOL_TB - hl:, :], cur_ref[...], next_ref[:hl, :]], axis=0)
    rows = POOL_TB + 2 * hl
    r = lax.broadcasted_iota(i32, (rows, 1), 0)
    pos = (i * POOL_TB + r - hl) & (n - 1)

    def back(a, s):
        return jnp.where(pos >= s, pltpu.roll(a, s, 0), 0.0)

    def fwd(a, s):
        return jnp.where(pos + s < n, pltpu.roll(a, rows - s, 0), 0.0)

    bsum = [back(ext, 1)]
    fsum = [ext]
    for k in range(3):
        s = 1 << k
        bsum.append(bsum[k] + back(bsum[k], s))
        fsum.append(fsum[k] + fwd(fsum[k], s))
    lane = lax.broadcasted_iota(i32, (1, POOL_W), 1)
    grp = lane // POOL_GROUP
    win = bsum[3] + fsum[3]
    half = jnp.full((1, POOL_W), POOL_WINDOWS[3] // 2, i32)
    for g in (2, 1, 0):
        win = jnp.where(grp == g, bsum[g] + fsum[g], win)
        half = jnp.where(grp == g, POOL_WINDOWS[g] // 2, half)
    cnt = (jnp.minimum(pos + half, n) - jnp.maximum(pos - half, 0)).astype(f32)
    y = (win / cnt - ext)[hl:hl + POOL_TB]
    o_ref[...] = (_dot(y.astype(bf16), wp_ref[...]) * scale_ref[...]).astype(bf16)


def _pool(p, l, w):
    nt = T_ALL // POOL_TB
    blk = lambda f: pl.BlockSpec((POOL_TB, POOL_W), lambda i: (f(i), 0))
    return pl.pallas_call(
        _pool_kernel,
        grid=(nt,),
        in_specs=[
            blk(lambda i: jnp.maximum(i - 1, 0)), blk(lambda i: i),
            blk(lambda i: jnp.minimum(i + 1, nt - 1)),
            pl.BlockSpec((None, POOL_W, POOL_W), lambda i: (l, 0, 0)),
            pl.BlockSpec((None, 1, POOL_W), lambda i: (l, 0, 0)),
        ],
        out_specs=blk(lambda i: i),
        out_shape=jax.ShapeDtypeStruct((T_ALL, POOL_W), bf16),
        compiler_params=_cparams(("arbitrary",)),
        name="pool",
    )(p, p, p, w["w_pool_bd"], w["pool_scale"])


def _seqdft_kernel(*refs, n, nseq):
    m_ref, pq_refs, w_ref, o_ref = refs[0], refs[1:-2], refs[-2], refs[-1]
    per_blk = SEG // n
    for b in range(nseq):
        @pl.when(pl.program_id(1) == b)
        def _(b=b):
            pq_ref = pq_refs[b // per_blk]
            r0 = (b % per_blk) * n
            f = (_dot(m_ref[:, :n], pq_ref[r0:r0 + n, :FFT_W])
                 + _dot(m_ref[:, n:], pq_ref[r0:r0 + n, FFT_W:]))
            o_ref[...] = _dot(f.astype(bf16), w_ref[...]).astype(bf16)


def _seqdft(pq, m, l, w, *, n, tr, nseq, row0):
    nr = n // tr
    nblk = nseq * n // SEG
    pq_specs = [pl.BlockSpec((SEG, 2 * FFT_W), lambda i, b, j=j: (row0 // SEG + j, 0))
                for j in range(nblk)]
    return pl.pallas_call(
        functools.partial(_seqdft_kernel, n=n, nseq=nseq),
        grid=(nr, nseq),
        in_specs=[pl.BlockSpec((tr, 2 * n), lambda i, b: (i, 0))] + pq_specs
        + [pl.BlockSpec((None, FFT_W, FFT_W), lambda i, b: (l, 0, 0))],
        out_specs=pl.BlockSpec((tr, FFT_W), lambda i, b: (b * nr + i, 0)),
        out_shape=jax.ShapeDtypeStruct((nseq * n, FFT_W), bf16),
        compiler_params=_cparams(("arbitrary", "arbitrary")),
        name="seqdft_%d" % n,
    )(m, *([pq] * nblk), w["w_fft"])


def _attn_kernel(*refs, has_cache):
    if has_cache:
        q_ref, k_ref, v_ref, kc_ref, vc_ref, o_ref = refs
    else:
        q_ref, k_ref, v_ref, o_ref = refs
    q = q_ref[...]
    tq = q.shape[0]
    lane = lax.broadcasted_iota(i32, q.shape, 1)
    zero = jnp.zeros_like(q)
    qs = jnp.concatenate([jnp.where(lane < HEAD_DIM, q, zero),
                          jnp.where(lane >= HEAD_DIM, q, zero)], axis=0)
    nt = (((1,), (1,)), ((), ()))
    n = k_ref.shape[0]
    chunk = min(n, ATT_CHUNK)
    parts = [(k_ref, v_ref, c * chunk, chunk) for c in range(n // chunk)]
    if has_cache:
        parts = [(kc_ref, vc_ref, 0, PAST_LEN)] + parts
    m = jnp.full((2 * tq, 1), -jnp.inf, f32)
    den = jnp.zeros((2 * tq, 1), f32)
    acc = jnp.zeros((2 * tq, 2 * HEAD_DIM), f32)
    for kr, vr, off, size in parts:
        s = lax.dot_general(qs, kr[off:off + size, :], nt, preferred_element_type=f32)
        m_new = jnp.maximum(m, jnp.max(s, axis=-1, keepdims=True))
        alpha = jnp.exp2(m - m_new)
        p = jnp.exp2(s - m_new).astype(bf16)
        den = alpha * den + jnp.sum(p.astype(f32), axis=-1, keepdims=True)
        acc = alpha * acc + _dot(p, vr[off:off + size, :])
        m = m_new
    out = acc / den
    o_ref[...] = jnp.where(lane < HEAD_DIM, out[:tq], out[tq:]).astype(bf16)


def _attention(q, kd, vd, cache, *, n, tq, nseq, row0):
    nq = n // tq
    b0 = row0 // n
    q0 = row0 // tq
    in_specs = [
        pl.BlockSpec((tq, 2 * HEAD_DIM), lambda b, h, i: (q0 + b * nq + i, h)),
        pl.BlockSpec((None, n, KV_W), lambda b, h, i: (h, b0 + b, 0)),
        pl.BlockSpec((None, n, KV_W), lambda b, h, i: (h, b0 + b, 0)),
    ]
    args = [q, kd, vd]
    if cache is not None:
        cspec = pl.BlockSpec((None, None, PAST_LEN, KV_W), lambda b, h, i: (h, b, 0, 0))
        in_specs += [cspec, cspec]
        args += list(cache)
    return pl.pallas_call(
        functools.partial(_attn_kernel, has_cache=cache is not None),
        grid=(nseq, N_KV_HEADS, nq),
        in_specs=in_specs,
        out_specs=pl.BlockSpec((tq, 2 * HEAD_DIM), lambda b, h, i: (b * nq + i, h)),
        out_shape=jax.ShapeDtypeStruct((nseq * n, ATTN_W), bf16),
        compiler_params=_cparams(("arbitrary", "arbitrary", "arbitrary")),
        name="attention_%d" % n,
    )(*args)


def _layer_norm(x, g, b):
    mu = jnp.mean(x, axis=-1, keepdims=True)
    xc = x - mu
    var = jnp.mean(xc * xc, axis=-1, keepdims=True)
    return xc * lax.rsqrt(var + LN_EPS) * g + b


def _outproj_kernel(xc_ref, xl_ref, mod_ref, fc_ref, fl_ref, p_ref, s_ref, ac_ref, al_ref, wout_ref,
                    g_ref, b_ref, wr_ref, br_ref, tril_ref,
                    x1_ref, h2_ref, route_ref, cnt_ref, tab_ref, carry_ref):
    i = pl.program_id(0)

    @pl.when(i == 0)
    def _():
        carry_ref[...] = jnp.zeros_like(carry_ref)

    mod = mod_ref[...]
    is_ctx = i < T_CTX // TM
    f_mix = jnp.where(is_ctx, fc_ref[...], fl_ref[...])
    a_mix = jnp.where(is_ctx, ac_ref[...], al_ref[...])
    mix = _dot(jnp.concatenate([f_mix, p_ref[...], s_ref[...], a_mix], axis=1), wout_ref[...])
    x = jnp.where(is_ctx, xc_ref[...], xl_ref[...])
    x1 = _layer_norm(DEEPNORM_ALPHA * x + mod[2:3] * mix, g_ref[...], b_ref[...])
    x1_ref[...] = x1
    h2 = x1 * (1.0 + mod[4:5]) + mod[3:4]
    h2_ref[...] = h2

    h_hi, h_lo = _split_hi_lo(h2)
    hw = _dot(h_hi, wr_ref[...])
    logits = hw[:, :128] + hw[:, 128:] + _dot(h_lo, wr_ref[:, :128]) + br_ref[...]
    lane = lax.broadcasted_iota(i32, logits.shape, 1).astype(f32)
    neg = jnp.float32(-jnp.inf)
    big = jnp.float32(1 << 20)
    gl = jnp.where(lane < N_GROUPS, logits, neg)
    gmax = jnp.max(gl, axis=-1, keepdims=True)
    gsel = jnp.min(jnp.where(gl == gmax, lane, big), axis=-1, keepdims=True)
    pg = 1.0 / jnp.sum(jnp.exp(gl - gmax), axis=-1, keepdims=True)
    e_lo = ROUTE_E0 + gsel * EXPERTS_PER_GROUP
    el = jnp.where((lane >= e_lo) & (lane < e_lo + EXPERTS_PER_GROUP), logits, neg)
    v1 = jnp.max(el, axis=-1, keepdims=True)
    i1 = jnp.min(jnp.where(el == v1, lane, big), axis=-1, keepdims=True)
    el2 = jnp.where(lane == i1, neg, el)
    v2 = jnp.max(el2, axis=-1, keepdims=True)
    i2 = jnp.min(jnp.where(el2 == v2, lane, big), axis=-1, keepdims=True)
    e2 = jnp.exp(v2 - v1)
    w1 = pg / (1.0 + e2)
    w2 = pg * e2 / (1.0 + e2)
    oh1 = lane == i1
    oh2 = lane == i2
    oh = jnp.where(oh1 | oh2, 1.0, 0.0)
    lrank = _dot(tril_ref[...], oh.astype(bf16))
    seg = jnp.floor((jnp.sum(oh, axis=0, keepdims=True) + (ROW_CHUNK - 1.0)) * (1.0 / ROW_CHUNK)) * ROW_CHUNK
    seg8 = jnp.broadcast_to(seg, (8, 128))
    lane8 = lax.broadcasted_iota(i32, (8, 128), 1)
    off8 = seg8
    for sh in (1, 2, 4, 8, 16):
        off8 = off8 + jnp.where(lane8 >= sh, pltpu.roll(off8, sh, 1), 0.0)
    off8 = off8 - seg8
    carry = carry_ref[...]
    lpos = lrank + off8[0:1, :]
    rank = lrank + carry[0:1, :]
    pick = lambda sel, val: jnp.sum(jnp.where(sel, val, 0.0), axis=-1, keepdims=True)
    sub8 = lax.broadcasted_iota(i32, (8, 128), 0)
    tab_ref[...] = jnp.where(sub8 == 0, seg8, jnp.where(sub8 == 1, off8, jnp.where(sub8 == 2, carry, 0.0)))
    carry = carry + seg8
    carry_ref[...] = carry
    cnt_ref[...] = carry
    cols = (i1 - ROUTE_E0, i2 - ROUTE_E0, w1, w2, pick(oh1, rank), pick(oh2, rank),
            pick(oh1, lpos), pick(oh2, lpos))
    route = jnp.zeros_like(logits)
    for j, col in enumerate(cols):
        route = jnp.where(lane == j, col, route)
    route_ref[...] = route


def _outproj(x_ctx, x_lat, mod, fo_ctx, fo_lat, po, so, ao_ctx, ao_lat, l, w):
    nt = T_ALL // TM
    tile = lambda wd: pl.BlockSpec((TM, wd), lambda i: (i, 0))
    vec = lambda wd: pl.BlockSpec((None, 1, wd), lambda i: (l, 0, 0))
    return pl.pallas_call(
        _outproj_kernel,
        grid=(nt,),
        in_specs=[
            _ctx_tile(D_MODEL), _lat_tile(D_MODEL),
            pl.BlockSpec((None, None, 6, D_MODEL), lambda i: (l, i // (SEG // TM), 0, 0)),
            _ctx_tile(FFT_W), _lat_tile(FFT_W), tile(POOL_W), tile(SGU_W),
            _ctx_tile(ATTN_W), _lat_tile(ATTN_W),
            pl.BlockSpec((None, D_MODEL, D_MODEL), lambda i: (l, 0, 0)),
            vec(D_MODEL), vec(D_MODEL),
            pl.BlockSpec((None, D_MODEL, 256), lambda i: (l, 0, 0)),
            vec(128),
            pl.BlockSpec((TM, TM), lambda i: (0, 0)),
        ],
        out_specs=[tile(D_MODEL), tile(D_MODEL), tile(128), pl.BlockSpec((8, 128), lambda i: (0, 0)),
                   pl.BlockSpec((None, 8, 128), lambda i: (i, 0, 0))],
        out_shape=[
            jax.ShapeDtypeStruct((T_ALL, D_MODEL), f32),
            jax.ShapeDtypeStruct((T_ALL, D_MODEL), f32),
            jax.ShapeDtypeStruct((T_ALL, 128), f32),
            jax.ShapeDtypeStruct((8, 128), f32),
            jax.ShapeDtypeStruct((nt, 8, 128), f32),
        ],
        scratch_shapes=[pltpu.VMEM((8, 128), f32)],
        compiler_params=_cparams(("arbitrary",)),
        name="outproj",
    )(x_ctx, x_lat, mod, fo_ctx, fo_lat, po, so, ao_ctx, ao_lat,
      w["w_out"], w["ln1_g"], w["ln1_b"], w["w_r"], w["b_r"],
      w["tril"])


def _plan_kernel(route_ref, cnt_ref, pos_ref, meta_ref):
    lane = lax.broadcasted_iota(i32, (8, 128), 1)
    sub = lax.broadcasted_iota(i32, (8, 128), 0)
    cnt = cnt_ref[...]
    is_e = (lane >= ROUTE_E0) & (lane < ROUTE_E0 + N_EXPERTS)
    tiles = jnp.where(is_e, jnp.floor((cnt + (MOE_TM - 1.0)) * (1.0 / MOE_TM)), 0.0)
    cum = tiles
    for s in (1, 2, 4, 8, 16):
        cum = cum + jnp.where(lane >= s, pltpu.roll(cum, s, 1), 0.0)
    pstart = (cum - tiles) * MOE_TM
    nused = jnp.max(cum, axis=-1, keepdims=True)
    fill = jnp.where(is_e & (cnt != tiles * MOE_TM), pstart + (tiles - 1.0) * MOE_TM, -1.0)
    meta = jnp.where(sub == 0, cnt, jnp.where(sub == 1, nused, jnp.where(sub == 2, fill,
                     jnp.where(sub == 3, pstart, 0.0))))
    meta_ref[...] = meta.astype(i32)

    r = route_ref[...]
    lane_t = lax.broadcasted_iota(i32, r.shape, 1).astype(f32)
    ps = pstart[0:1, :]

    def dest(ecol, rcol):
        hit = lane_t == (r[:, ecol:ecol + 1] + ROUTE_E0)
        return jnp.sum(jnp.where(hit, ps, 0.0), axis=-1, keepdims=True) + r[:, rcol:rcol + 1]

    pos = jnp.where(lane_t == 0, dest(0, 4), jnp.where(lane_t == 1, dest(1, 5), 0.0))
    pos_ref[...] = pos.astype(i32)


def _plan(route, cnt):
    return pl.pallas_call(
        _plan_kernel,
        grid=(T_ALL // PLAN_TB,),
        in_specs=[pl.BlockSpec((PLAN_TB, 128), lambda i: (i, 0)),
                  pl.BlockSpec((8, 128), lambda i: (0, 0))],
        out_specs=[pl.BlockSpec((PLAN_TB, 128), lambda i: (i, 0)),
                   pl.BlockSpec((8, 128), lambda i: (0, 0))],
        out_shape=[jax.ShapeDtypeStruct((T_ALL, 128), i32), jax.ShapeDtypeStruct((8, 128), i32)],
        compiler_params=_cparams(("arbitrary",)),
        name="plan",
    )(route, cnt)


def _dispatch_kernel(nch_ref, off_ref, dst_ref, tot_ref, fill_ref, nused_ref, h_ref, route_ref, xs_ref,
                     sorted_ref, zero_ref, sem, fill_sem):
    i = pl.program_id(0)

    def tile_fill(row0):
        return pltpu.make_async_copy(zero_ref, xs_ref.at[pl.ds(pl.multiple_of(row0, MOE_TM), MOE_TM)],
                                     fill_sem)

    @pl.when(i == 0)
    def _():
        zero_ref[...] = jnp.zeros_like(zero_ref)

        def start(e, c):
            @pl.when(fill_ref[e] >= 0)
            def _():
                tile_fill(jnp.maximum(fill_ref[e], 0)).start()
            return c

        def wait(e, c):
            @pl.when(fill_ref[e] >= 0)
            def _():
                tile_fill(jnp.maximum(fill_ref[e], 0)).wait()
            return c

        def start_tail(t, c):
            tile_fill(t * MOE_TM).start()
            return c

        def wait_tail(t, c):
            tile_fill(t * MOE_TM).wait()
            return c

        lax.fori_loop(0, N_EXPERTS, start, 0)
        lax.fori_loop(nused_ref[0], MOE_NT, start_tail, 0)
        lax.fori_loop(0, N_EXPERTS, wait, 0)
        lax.fori_loop(nused_ref[0], MOE_NT, wait_tail, 0)

    rt = route_ref[...].T
    j = lax.broadcasted_iota(i32, (DISP_ROWS, 1), 0).astype(f32)
    sel = jnp.where((j == rt[6:7, :]) | (j == rt[7:8, :]), 1.0, 0.0).astype(bf16)
    slot = i % 2
    sorted_ref[slot] = _dot(sel, h_ref[...].astype(bf16))

    def per_expert(e, c):
        idx = i * N_EXPERTS + e
        n, s0, d0 = nch_ref[idx], off_ref[idx], dst_ref[idx]
        b = 1
        while b <= TM // ROW_CHUNK:
            @pl.when((n & b) != 0)
            def _(b=b):
                r0 = (n & (b - 1)) * ROW_CHUNK
                rows = b * ROW_CHUNK
                pltpu.make_async_copy(
                    sorted_ref.at[slot, pl.ds(pl.multiple_of(s0 + r0, ROW_CHUNK), rows)],
                    xs_ref.at[pl.ds(pl.multiple_of(d0 + r0, ROW_CHUNK), rows)], sem.at[slot]).start()
            b *= 2
        return c

    lax.fori_loop(0, N_EXPERTS, per_expert, 0)

    def drain(tile, s):
        rows = tot_ref[tile] * ROW_CHUNK

        @pl.when(rows > 0)
        def _():
            pltpu.make_async_copy(sorted_ref.at[s, pl.ds(0, rows)], xs_ref.at[pl.ds(0, rows)],
                                  sem.at[s]).wait()

    @pl.when(i >= 1)
    def _():
        drain(i - 1, 1 - slot)

    @pl.when(i == pl.num_programs(0) - 1)
    def _():
        drain(i, slot)


def _dispatch(nch, off, dst, tot, fill, nused, h2, route):
    grid_spec = pltpu.PrefetchScalarGridSpec(
        num_scalar_prefetch=6,
        grid=(T_ALL // TM,),
        in_specs=[pl.BlockSpec((TM, D_MODEL), lambda i, *_: (i, 0)),
                  pl.BlockSpec((TM, 128), lambda i, *_: (i, 0))],
        out_specs=pl.BlockSpec(memory_space=pl.ANY),
        scratch_shapes=[pltpu.VMEM((2, DISP_ROWS, D_MODEL), f32), pltpu.VMEM((MOE_TM, D_MODEL), f32),
                        pltpu.SemaphoreType.DMA((2,)), pltpu.SemaphoreType.DMA(())],
    )
    return pl.pallas_call(
        _dispatch_kernel,
        grid_spec=grid_spec,
        out_shape=jax.ShapeDtypeStruct((MOE_NT * MOE_TM, D_MODEL), f32),
        compiler_params=_cparams(("arbitrary",)),
        name="dispatch",
    )(nch, off, dst, tot, fill, nused, h2, route)


def _experts_kernel(cnt_ref, nused_ref, xs_ref, wg_hbm, wu_hbm, wd_hbm, ys_ref,
                    wg_f, wu_f, wd_f, wg_b, wu_b, wd_b, st, wsem, *, layer):
    i = pl.program_id(0)
    nused = nused_ref[0]
    NXT, NSLOT, LEFT = 0, 1, 2

    def w_copies(e, slot):
        return (pltpu.make_async_copy(wg_hbm.at[layer, e], wg_f.at[slot], wsem.at[slot, 0]),
                pltpu.make_async_copy(wu_hbm.at[layer, e], wu_f.at[slot], wsem.at[slot, 1]),
                pltpu.make_async_copy(wd_hbm.at[layer, e], wd_f.at[slot], wsem.at[slot, 2]))

    def next_nonempty(e):
        return lax.while_loop(
            lambda v: (v < N_EXPERTS) & (cnt_ref[jnp.minimum(v, N_EXPERTS - 1)] == 0),
            lambda v: v + 1, e)

    @pl.when(i == 0)
    def _():
        e0 = next_nonempty(jnp.int32(0))
        for c in w_copies(e0, 0):
            c.start()
        st[NXT] = e0
        st[NSLOT] = 0
        st[LEFT] = 0

    @pl.when(i < nused)
    def _():
        @pl.when(st[LEFT] == 0)
        def _():
            e = st[NXT]
            slot = st[NSLOT]
            for c in w_copies(e, slot):
                c.wait()
            e2 = next_nonempty(e + 1)

            @pl.when(e2 < N_EXPERTS)
            def _():
                for c in w_copies(e2, 1 - slot):
                    c.start()

            st[NXT] = e2
            st[NSLOT] = 1 - slot
            st[LEFT] = (cnt_ref[e] + (MOE_TM - 1)) // MOE_TM
            wg_b[...] = wg_f[slot].astype(bf16)
            wu_b[...] = wu_f[slot].astype(bf16)
            wd_b[...] = wd_f[slot].astype(bf16)

        x = xs_ref[...].astype(bf16)
        hg = _dot(x, wg_b[...])
        hu = _dot(x, wu_b[...])
        act = (hg * jax.nn.sigmoid(hg)) * hu
        ys_ref[...] = _dot(act.astype(bf16), wd_b[...])
        st[LEFT] = st[LEFT] - 1

    @pl.when(i >= nused)
    def _():
        ys_ref[...] = jnp.zeros_like(ys_ref)


def _experts(counts, nused, xs, l, w_gate, w_up, w_down):
    hbm = pl.BlockSpec(memory_space=pl.ANY)
    grid_spec = pltpu.PrefetchScalarGridSpec(
        num_scalar_prefetch=2,
        grid=(MOE_NT,),
        in_specs=[pl.BlockSpec((MOE_TM, D_MODEL), lambda i, c, nu: (jnp.minimum(i, nu[0] - 1), 0)),
                  hbm, hbm, hbm],
        out_specs=pl.BlockSpec((MOE_TM, D_MODEL), lambda i, c, nu: (i, 0)),
        scratch_shapes=[
            pltpu.VMEM((2, D_MODEL, EXPERT_FF), f32),
            pltpu.VMEM((2, D_MODEL, EXPERT_FF), f32),
            pltpu.VMEM((2, EXPERT_FF, D_MODEL), f32),
            pltpu.VMEM((D_MODEL, EXPERT_FF), bf16),
            pltpu.VMEM((D_MODEL, EXPERT_FF), bf16),
            pltpu.VMEM((EXPERT_FF, D_MODEL), bf16),
            pltpu.SMEM((4,), i32),
            pltpu.SemaphoreType.DMA((2, 3)),
        ],
    )
    return pl.pallas_call(
        functools.partial(_experts_kernel, layer=l),
        grid_spec=grid_spec,
        out_shape=jax.ShapeDtypeStruct((MOE_NT * MOE_TM, D_MODEL), f32),
        compiler_params=_cparams(("arbitrary",)),
        name="experts",
    )(counts, nused, xs, w_gate, w_up, w_down)


def _combine_kernel(pos_ref, x1_ref, mod_ref, route_ref, ys_hbm, g_ref, b_ref, oc_ref, ol_ref, ybuf, sem):
    i = pl.program_id(0)
    nt = pl.num_programs(0) - 1

    @pl.when(i < nt)
    def _():
        slot = i % 2
        for k in range(2):
            base = k * T_ALL + i * TM
            for j in range(TM):
                pltpu.make_async_copy(ys_hbm.at[pl.ds(pos_ref[base + j], 1)],
                                      ybuf.at[slot, k, pl.ds(j, 1)], sem.at[slot]).start()

    @pl.when(i >= 1)
    def _():
        slot = (i - 1) % 2
        for k in range(2):
            pltpu.make_async_copy(ys_hbm.at[pl.ds(0, TM)], ybuf.at[slot, k], sem.at[slot]).wait()
        route = route_ref[...]
        mod = mod_ref[...]
        moe = route[:, 2:3] * ybuf[slot, 0] + route[:, 3:4] * ybuf[slot, 1]
        y = _layer_norm(DEEPNORM_ALPHA * x1_ref[...] + mod[5:6] * moe, g_ref[...], b_ref[...])

        @pl.when(i - 1 < T_CTX // TM)
        def _():
            oc_ref[...] = y

        @pl.when(i - 1 >= T_CTX // TM)
        def _():
            ol_ref[...] = y


def _combine(pos_flat, x1, mod, route, ys, l, w):
    nt = T_ALL // TM
    vec = pl.BlockSpec((None, 1, D_MODEL), lambda i, p: (l, 0, 0))
    nctx = T_CTX // TM
    prev = lambda i: jnp.maximum(i - 1, 0)
    grid_spec = pltpu.PrefetchScalarGridSpec(
        num_scalar_prefetch=1,
        grid=(nt + 1,),
        in_specs=[
            pl.BlockSpec((TM, D_MODEL), lambda i, p: (prev(i), 0)),
            pl.BlockSpec((None, None, 6, D_MODEL), lambda i, p: (l, prev(i) // (SEG // TM), 0, 0)),
            pl.BlockSpec((TM, 128), lambda i, p: (prev(i), 0)),
            pl.BlockSpec(memory_space=pl.ANY),
            vec, vec,
        ],
        out_specs=[pl.BlockSpec((TM, D_MODEL), lambda i, p: (jnp.minimum(prev(i), nctx - 1), 0)),
                   pl.BlockSpec((TM, D_MODEL), lambda i, p: (jnp.maximum(prev(i) - nctx, 0), 0))],
        scratch_shapes=[pltpu.VMEM((2, 2, TM, D_MODEL), f32), pltpu.SemaphoreType.DMA((2,))],
    )
    return pl.pallas_call(
        _combine_kernel,
        grid_spec=grid_spec,
        out_shape=[jax.ShapeDtypeStruct((T_CTX, D_MODEL), f32),
                   jax.ShapeDtypeStruct((T_LAT, D_MODEL), f32)],
        compiler_params=_cparams(("arbitrary",)),
        name="combine",
    )(pos_flat, x1, mod, route, ys, w["ln2_g"], w["ln2_b"])


def _dft_cos_sin(n, scale):
    k = jnp.arange(n, dtype=i32)
    ang = ((k[:, None] * k[None, :]) % n).astype(f32) * np.float32(2.0 * np.pi / n)
    return jnp.cos(ang) * scale, jnp.sin(ang) * scale


def _seq_dft_matrix(n):
    g = min(DFT_SPLIT, n)
    j = jnp.arange(n, dtype=i32)[None, :]
    k1 = jnp.arange(n // g, dtype=i32)[:, None]
    k2 = jnp.arange(g, dtype=i32)[:, None]
    ang_a = ((k1 * j) % (n // g)).astype(f32) * np.float32(2.0 * np.pi * g / n)
    ang_b = ((k2 * j) % n).astype(f32) * np.float32(2.0 * np.pi / n)
    scale = np.float32(n ** -0.5)
    ca, sa = jnp.cos(ang_a), jnp.sin(ang_a)
    cb, sb = jnp.cos(ang_b) * scale, jnp.sin(ang_b) * scale
    ca2 = jnp.concatenate([ca, ca], axis=1)[:, None, :]
    sa2 = jnp.concatenate([sa, sa], axis=1)[:, None, :]
    cb2 = jnp.concatenate([cb, -sb], axis=1)[None, :, :]
    sb2 = jnp.concatenate([sb, cb], axis=1)[None, :, :]
    return (ca2 * cb2 - sa2 * sb2).astype(bf16).reshape(n, 2 * n)


def _rope_tables():
    rows = DEC_SEQ // GRID_W
    row = jnp.repeat(jnp.arange(rows), GRID_W).astype(f32)
    col = jnp.tile(jnp.arange(GRID_W), rows).astype(f32)
    n_freq = HEAD_DIM // 4
    inv = ROPE_THETA ** (-jnp.arange(n_freq, dtype=f32) / n_freq)
    ar = row[:, None] * inv
    ac = col[:, None] * inv
    ang = jnp.concatenate([ar, ar, ac, ac], axis=-1)
    cos = jnp.tile(jnp.cos(ang), (1, N_HEADS))
    sin = jnp.tile(jnp.sin(ang), (1, N_HEADS))
    first = (jnp.arange(ATTN_W) % (HEAD_DIM // 2)) < n_freq
    sin_a = jnp.where(first[None, :], -sin, 0.0)
    sin_b = jnp.where(first[None, :], 0.0, sin)
    ident = jnp.zeros((TM, ATTN_W), f32)
    return (jnp.concatenate([cos, ident + 1.0], axis=0),
            jnp.concatenate([sin_a, ident], axis=0),
            jnp.concatenate([sin_b, ident], axis=0))


def _dup_cache(cache):
    c = jnp.transpose(cache, (1, 3, 0, 2, 4))
    return jnp.concatenate([c, c], axis=-1).astype(bf16)


def kernel(x_prompt, x_sample, cache_k, cache_v, c, c_ctx, w_mod, b_mod, w_in, w_fft, w_pool, pool_scale, sgu_ln_g, sgu_ln_b, w_sgu, b_sgu, q_norm_g, k_norm_g, w_out, ln1_g, ln1_b, w_router_group, b_router_group, w_router_expert, b_router_expert, w_gate, w_up, w_down, ln2_g, ln2_b):
    L = DEPTH
    x_ctx = x_prompt.reshape(T_CTX, D_MODEL)
    x_lat = x_sample.reshape(T_LAT, D_MODEL)

    cond8 = jnp.concatenate([c_ctx[None, :], c, jnp.zeros((8 - 1 - DEC_BATCH, D_MODEL), f32)], axis=0)
    mod = _modulation(cond8, w_mod, b_mod)[:, :N_SEG].reshape(L, N_SEG, 6, D_MODEL)

    cc, sc = _dft_cos_sin(FFT_W, np.float32(FFT_W ** -0.5))
    rope_cos, rope_sin_a, rope_sin_b = _rope_tables()
    head_id = jnp.arange(ATTN_W) // HEAD_DIM
    eye_g = jnp.eye(len(POOL_WINDOWS), dtype=f32)
    w_r = jnp.zeros((L, D_MODEL, 128), f32)
    w_r = w_r.at[:, :, :N_GROUPS].set(w_router_group).at[:, :, ROUTE_E0:ROUTE_E0 + N_EXPERTS].set(w_router_expert)
    b_r = jnp.zeros((L, 1, 128), f32)
    b_r = b_r.at[:, 0, :N_GROUPS].set(b_router_group).at[:, 0, ROUTE_E0:ROUTE_E0 + N_EXPERTS].set(b_router_expert)
    w_r_hi, w_r_lo = _split_hi_lo(w_r)
    w = {
        "w_in": w_in.astype(bf16),
        "csc": jnp.concatenate([cc, sc], axis=1).astype(bf16),
        "w_sgu": jnp.transpose(w_sgu, (0, 2, 1, 3)).reshape(L, CHUNK, SGU_HEADS * CHUNK).astype(bf16),
        "b_sgu": jnp.repeat(jnp.transpose(b_sgu, (0, 2, 1)), SGU_W // SGU_HEADS, axis=2),
        "sgu_ln_g": sgu_ln_g.reshape(L, 1, SGU_W),
        "sgu_ln_b": sgu_ln_b.reshape(L, 1, SGU_W),
        "q_norm_g": jnp.tile(q_norm_g, (1, N_HEADS)).reshape(L, 1, ATTN_W),
        "k_norm_g": jnp.tile(k_norm_g, (1, N_KV_HEADS)).reshape(L, 1, KV_W),
        "rope_cos": rope_cos, "rope_sin_a": rope_sin_a, "rope_sin_b": rope_sin_b,
        "ones_bd": (head_id[:, None] == head_id[None, :]).astype(bf16),
        "w_pool_bd": jnp.einsum("lgcd,gh->lgchd", w_pool, eye_g).reshape(L, POOL_W, POOL_W).astype(bf16),
        "pool_scale": pool_scale.reshape(L, 1, POOL_W),
        "w_fft": w_fft.astype(bf16),
        "w_out": w_out.astype(bf16),
        "ln1_g": ln1_g.reshape(L, 1, D_MODEL), "ln1_b": ln1_b.reshape(L, 1, D_MODEL),
        "ln2_g": ln2_g.reshape(L, 1, D_MODEL), "ln2_b": ln2_b.reshape(L, 1, D_MODEL),
        "w_r": jnp.concatenate([w_r_hi, w_r_lo], axis=-1), "b_r": b_r,
        "tril": (jnp.arange(TM)[:, None] > jnp.arange(TM)[None, :]).astype(bf16),
    }
    m_ctx = _seq_dft_matrix(SEQ)
    m_lat = _seq_dft_matrix(DEC_SEQ)
    kc_all = _dup_cache(cache_k)
    vc_all = _dup_cache(cache_v)

    new_k, new_v = [], []
    for l in range(L):
        pq, praw, sgu, q, kd, vd, nk, nv = _inproj(x_ctx, x_lat, mod, l, w)
        new_k.append(nk[:T_CTX].reshape(BATCH, SEQ, N_KV_HEADS, HEAD_DIM))
        new_v.append(nv[:T_CTX].reshape(BATCH, SEQ, N_KV_HEADS, HEAD_DIM))
        po = _pool(praw, l, w)
        fo_ctx = _seqdft(pq, m_ctx, l, w, n=SEQ, tr=SEQ, nseq=BATCH, row0=0)
        fo_lat = _seqdft(pq, m_lat, l, w, n=DEC_SEQ, tr=FFT_TR, nseq=DEC_BATCH, row0=T_CTX)
        ao_ctx = _attention(q, kd, vd, None, n=SEQ, tq=SEQ, nseq=BATCH, row0=0)
        ao_lat = _attention(q, kd, vd, (kc_all[l], vc_all[l]), n=DEC_SEQ, tq=ATT_TQ, nseq=DEC_BATCH,
                            row0=T_CTX)
        x1, h2, route, cnt, tab = _outproj(x_ctx, x_lat, mod, fo_ctx, fo_lat, po, sgu, ao_ctx, ao_lat, l, w)
        pos_slab, meta = _plan(route, cnt)
        pos_flat = pos_slab[:, :2].T.reshape(-1)
        experts = slice(ROUTE_E0, ROUTE_E0 + N_EXPERTS)
        counts = meta[0, experts]
        nused = meta[1, :1]
        fill = meta[2, experts]
        tab = tab[:, :, experts].astype(i32)
        nch = (tab[:, 0] // ROW_CHUNK).reshape(-1)
        off = tab[:, 1].reshape(-1)
        dst = (meta[3, experts][None, :] + tab[:, 2]).reshape(-1)
        tot = jnp.sum(tab[:, 0], axis=1) // ROW_CHUNK
        xs = _dispatch(nch, off, dst, tot, fill, nused, h2, route)
        ys = _experts(counts, nused, xs, l, w_gate, w_up, w_down)
        x_ctx, x_lat = _combine(pos_flat, x1, mod, route, ys, l, w)

    y_prompt = x_ctx.reshape(BATCH, SEQ, D_MODEL)
    y_sample = x_lat.reshape(DEC_BATCH, DEC_SEQ, D_MODEL)
    return (y_prompt, y_sample, jnp.stack(new_k, axis=1), jnp.stack(new_v, axis=1))
```

```python
import functools

import numpy as np
import jax
import jax.numpy as jnp
from jax import lax
from jax.experimental import pallas as pl
from jax.experimental.pallas import tpu as pltpu

f32 = jnp.float32
bf16 = jnp.bfloat16
i32 = jnp.int32

D_MODEL = 1024
BATCH = 16
SEQ = 256
DEPTH = 4
DEC_BATCH = 2
DEC_SEQ = 4096
PAST_LEN = 512
GRID_W = 64
FFT_W = 256
POOL_W = 256
POOL_WINDOWS = (2, 4, 8, 16)
POOL_GROUP = 64
SGU_W = 256
SGU_HEADS = 4
CHUNK = 128
HEAD_DIM = 64
ATTN_W = 256
N_HEADS = 4
N_KV_HEADS = 2
KV_W = 128
IN_W = 1536
ROPE_THETA = 10000.0
N_GROUPS = 4
EXPERTS_PER_GROUP = 8
N_EXPERTS = 32
EXPERT_FF = 512
DEEPNORM_ALPHA = float((2 * DEPTH) ** 0.25)
LN_EPS = 1e-5
RMS_EPS = 1e-6

T_CTX = BATCH * SEQ
T_LAT = DEC_BATCH * DEC_SEQ
T_ALL = T_CTX + T_LAT
SEG = 4096
N_SEG = T_ALL // SEG

TM = 512
POOL_TB = 512
POOL_HALO = 8
FFT_TR = 512
ATT_TQ = 512
ATT_CHUNK = 1024
DFT_SPLIT = 64
MOE_TM = 256
ROW_CHUNK = 8
MOE_ROWS = 2 * T_ALL
MOE_PAD_ROWS = (T_ALL // TM) * N_EXPERTS * (ROW_CHUNK - 1)
MOE_NT = -(-(MOE_ROWS + MOE_PAD_ROWS) // MOE_TM) + N_EXPERTS
DISP_ROWS = 2 * TM + N_EXPERTS * ROW_CHUNK
ROUTE_E0 = 32
VMEM_LIMIT = 56 * 1024 * 1024


def _cparams(sem):
    return pltpu.CompilerParams(dimension_semantics=sem, vmem_limit_bytes=VMEM_LIMIT)


def _split_hi_lo(a):
    hi = a.astype(bf16)
    lo = (a - hi.astype(f32)).astype(bf16)
    return hi, lo


def _dot(a, b):
    return jnp.dot(a, b, preferred_element_type=f32)


def _mod_kernel(c_ref, w_ref, b_ref, o_ref):
    c = c_ref[...]
    s = c * jax.nn.sigmoid(c)
    s_hi, s_lo = _split_hi_lo(s)
    w_hi, w_lo = _split_hi_lo(w_ref[...])
    o_ref[...] = _dot(s_hi, w_hi) + _dot(s_hi, w_lo) + _dot(s_lo, w_hi) + b_ref[...]


def _modulation(cond8, w_mod, b_mod):
    tn = 1536
    return pl.pallas_call(
        _mod_kernel,
        grid=(DEPTH, 6 * D_MODEL // tn),
        in_specs=[
            pl.BlockSpec((8, D_MODEL), lambda l, j: (0, 0)),
            pl.BlockSpec((None, D_MODEL, tn), lambda l, j: (l, 0, j)),
            pl.BlockSpec((None, 1, tn), lambda l, j: (l, 0, j)),
        ],
        out_specs=pl.BlockSpec((None, 8, tn), lambda l, j: (l, 0, j)),
        out_shape=jax.ShapeDtypeStruct((DEPTH, 8, 6 * D_MODEL), f32),
        compiler_params=_cparams(("arbitrary", "arbitrary")),
        name="modulation",
    )(cond8, w_mod, b_mod.reshape(DEPTH, 1, 6 * D_MODEL))


def _head_rms(x, ones_bd, gain):
    ss = _dot((x * x).astype(bf16), ones_bd)
    return x * lax.rsqrt(ss * (1.0 / HEAD_DIM) + RMS_EPS) * gain


def _rope(x, cos, sin_a, sin_b):
    w = x.shape[-1]
    q4 = HEAD_DIM // 4
    return x * cos + pltpu.roll(x, w - q4, 1) * sin_a + pltpu.roll(x, q4, 1) * sin_b


def _dup_half(x, first):
    lane = lax.broadcasted_iota(i32, x.shape, 1)
    r = pltpu.roll(x, HEAD_DIM, 1)
    if first:
        return jnp.where(lane < HEAD_DIM, x, r)
    return jnp.where(lane >= HEAD_DIM, x, r)


def _gelu_tanh(x):
    c = np.sqrt(2.0 / np.pi).astype(np.float32)
    return x * (0.5 * (1.0 + jnp.tanh(c * (x + 0.044715 * (x * x * x)))))


def _inproj_kernel(xc_ref, xl_ref, mod_ref, win_ref, csc_ref, wsgu_ref, bsgu_ref, lng_ref, lnb_ref,
                   qg_ref, kg_ref, cos_ref, sina_ref, sinb_ref, ones_ref,
                   pq_ref, pool_ref, sgu_ref, q_ref, kd_ref, vd_ref, nk_ref, nv_ref):
    x = jnp.where(pl.program_id(0) < T_CTX // TM, xc_ref[...], xl_ref[...])
    mod = mod_ref[...]
    h = (x * (1.0 + mod[1:2]) + mod[0:1]).astype(bf16)
    proj = _dot(h, win_ref[...])

    a = proj[:, 0:FFT_W].astype(bf16)
    pq_ref[...] = _dot(a, csc_ref[...]).astype(bf16)

    pool_ref[...] = proj[:, FFT_W:FFT_W + POOL_W]

    o = FFT_W + POOL_W
    hgu = _gelu_tanh(proj[:, o:o + 2 * SGU_W])
    u = hgu[:, :SGU_W]
    v = hgu[:, SGU_W:]
    mu = jnp.mean(v, axis=-1, keepdims=True)
    vc = v - mu
    var = jnp.mean(vc * vc, axis=-1, keepdims=True)
    v = vc * lax.rsqrt(var + LN_EPS) * lng_ref[...] + lnb_ref[...]
    lane = lax.broadcasted_iota(i32, (CHUNK, SGU_W), 1)
    head = lane // (SGU_W // SGU_HEADS)
    wcat = wsgu_ref[...]
    for cidx in range(TM // CHUNK):
        rows = slice(cidx * CHUNK, (cidx + 1) * CHUNK)
        vch = v[rows]
        vblk = jnp.concatenate(
            [jnp.where(head == g, vch, 0.0) for g in range(SGU_HEADS)], axis=0).astype(bf16)
        sp = _dot(wcat, vblk) + bsgu_ref[...]
        sgu_ref[rows, :] = (u[rows] * sp).astype(bf16)

    o = o + 2 * SGU_W
    ones_bd = ones_ref[...]
    cos = cos_ref[...]
    sin_a = sina_ref[...]
    sin_b = sinb_ref[...]
    q = _head_rms(proj[:, o:o + ATTN_W], ones_bd, qg_ref[...])
    q = _rope(q, cos, sin_a, sin_b) * np.float32(HEAD_DIM ** -0.5 * np.log2(np.e))
    q_ref[...] = q.astype(bf16)
    o = o + ATTN_W
    k = _head_rms(proj[:, o:o + KV_W], ones_bd[:KV_W, :KV_W], kg_ref[...])
    nk_ref[...] = k
    k = _rope(k, cos[:, :KV_W], sin_a[:, :KV_W], sin_b[:, :KV_W])
    kd_ref[0] = _dup_half(k, True).astype(bf16)
    kd_ref[1] = _dup_half(k, False).astype(bf16)
    o = o + KV_W
    vv = proj[:, o:o + KV_W]
    nv_ref[...] = vv
    vd_ref[0] = _dup_half(vv, True).astype(bf16)
    vd_ref[1] = _dup_half(vv, False).astype(bf16)


def _rope_block(i):
    nlat = DEC_SEQ // TM
    nctx = T_CTX // TM
    return jnp.where(i < nctx, nlat, (i - nctx) % nlat)


def _ctx_tile(wd):
    return pl.BlockSpec((TM, wd), lambda i, *_: (jnp.minimum(i, T_CTX // TM - 1), 0))


def _lat_tile(wd):
    return pl.BlockSpec((TM, wd), lambda i, *_: (jnp.maximum(i - T_CTX // TM, 0), 0))


def _inproj(x_ctx, x_lat, mod, l, w):
    nt = T_ALL // TM
    tile = lambda wd: pl.BlockSpec((TM, wd), lambda i: (i, 0))
    const = lambda shape: pl.BlockSpec(shape, lambda i: (0,) * len(shape))
    rope_spec = pl.BlockSpec((TM, ATTN_W), lambda i: (_rope_block(i), 0))
    return pl.pallas_call(
        _inproj_kernel,
        grid=(nt,),
        in_specs=[
            _ctx_tile(D_MODEL), _lat_tile(D_MODEL),
            pl.BlockSpec((None, None, 6, D_MODEL), lambda i: (l, i // (SEG // TM), 0, 0)),
            pl.BlockSpec((None, D_MODEL, IN_W), lambda i: (l, 0, 0)),
            const((FFT_W, 2 * FFT_W)),
            pl.BlockSpec((None, CHUNK, SGU_HEADS * CHUNK), lambda i: (l, 0, 0)),
            pl.BlockSpec((None, CHUNK, SGU_W), lambda i: (l, 0, 0)),
            pl.BlockSpec((None, 1, SGU_W), lambda i: (l, 0, 0)),
            pl.BlockSpec((None, 1, SGU_W), lambda i: (l, 0, 0)),
            pl.BlockSpec((None, 1, ATTN_W), lambda i: (l, 0, 0)),
            pl.BlockSpec((None, 1, KV_W), lambda i: (l, 0, 0)),
            rope_spec, rope_spec, rope_spec,
            const((ATTN_W, ATTN_W)),
        ],
        out_specs=[
            tile(2 * FFT_W), tile(POOL_W), tile(SGU_W), tile(ATTN_W),
            pl.BlockSpec((N_KV_HEADS, TM, KV_W), lambda i: (0, i, 0)),
            pl.BlockSpec((N_KV_HEADS, TM, KV_W), lambda i: (0, i, 0)),
            tile(KV_W), tile(KV_W),
        ],
        out_shape=[
            jax.ShapeDtypeStruct((T_ALL, 2 * FFT_W), bf16),
            jax.ShapeDtypeStruct((T_ALL, POOL_W), f32),
            jax.ShapeDtypeStruct((T_ALL, SGU_W), bf16),
            jax.ShapeDtypeStruct((T_ALL, ATTN_W), bf16),
            jax.ShapeDtypeStruct((N_KV_HEADS, T_ALL, KV_W), bf16),
            jax.ShapeDtypeStruct((N_KV_HEADS, T_ALL, KV_W), bf16),
            jax.ShapeDtypeStruct((T_ALL, KV_W), f32),
            jax.ShapeDtypeStruct((T_ALL, KV_W), f32),
        ],
        compiler_params=_cparams(("arbitrary",)),
        name="inproj",
    )(x_ctx, x_lat, mod, w["w_in"], w["csc"], w["w_sgu"], w["b_sgu"], w["sgu_ln_g"], w["sgu_ln_b"],
      w["q_norm_g"], w["k_norm_g"], w["rope_cos"], w["rope_sin_a"], w["rope_sin_b"], w["ones_bd"])


def _pool_kernel(prev_ref, cur_ref, next_ref, wp_ref, scale_ref, o_ref):
    i = pl.program_id(0)
    n = jnp.where(i < T_CTX // POOL_TB, SEQ, DEC_SEQ)
    hl = POOL_HALO
    ext = jnp.concatenate([prev_ref[POOL_TB - hl:, :], cur_ref[...], next_ref[:hl, :]], axis=0)
    rows = POOL_TB + 2 * hl
    r = lax.broadcasted_iota(i32, (rows, 1), 0)
    pos = (i * POOL_TB + r - hl) & (n - 1)

    def back(a, s):
        return jnp.where(pos >= s, pltpu.roll(a, s, 0), 0.0)

    def fwd(a, s):
        return jnp.where(pos + s < n, pltpu.roll(a, rows - s, 0), 0.0)

    bsum = [back(ext, 1)]
    fsum = [ext]
    for k in range(3):
        s = 1 << k
        bsum.append(bsum[k] + back(bsum[k], s))
        fsum.append(fsum[k] + fwd(fsum[k], s))
    lane = lax.broadcasted_iota(i32, (1, POOL_W), 1)
    grp = lane // POOL_GROUP
    win = bsum[3] + fsum[3]
    half = jnp.full((1, POOL_W), POOL_WINDOWS[3] // 2, i32)
    for g in (2, 1, 0):
        win = jnp.where(grp == g, bsum[g] + fsum[g], win)
        half = jnp.where(grp == g, POOL_WINDOWS[g] // 2, half)
    cnt = (jnp.minimum(pos + half, n) - jnp.maximum(pos - half, 0)).astype(f32)
    y = (win / cnt - ext)[hl:hl + POOL_TB]
    o_ref[...] = (_dot(y.astype(bf16), wp_ref[...]) * scale_ref[...]).astype(bf16)


def _pool(p, l, w):
    nt = T_ALL // POOL_TB
    blk = lambda f: pl.BlockSpec((POOL_TB, POOL_W), lambda i: (f(i), 0))
    return pl.pallas_call(
        _pool_kernel,
        grid=(nt,),
        in_specs=[
            blk(lambda i: jnp.maximum(i - 1, 0)), blk(lambda i: i),
            blk(lambda i: jnp.minimum(i + 1, nt - 1)),
            pl.BlockSpec((None, POOL_W, POOL_W), lambda i: (l, 0, 0)),
            pl.BlockSpec((None, 1, POOL_W), lambda i: (l, 0, 0)),
        ],
        out_specs=blk(lambda i: i),
        out_shape=jax.ShapeDtypeStruct((T_ALL, POOL_W), bf16),
        compiler_params=_cparams(("arbitrary",)),
        name="pool",
    )(p, p, p, w["w_pool_bd"], w["pool_scale"])


def _seqdft_kernel(*refs, n, nseq):
    m_ref, pq_refs, w_ref, o_ref = refs[0], refs[1:-2], refs[-2], refs[-1]
    per_blk = SEG // n
    for b in range(nseq):
        @pl.when(pl.program_id(1) == b)
        def _(b=b):
            pq_ref = pq_refs[b // per_blk]
            r0 = (b % per_blk) * n
            f = (_dot(m_ref[:, :n], pq_ref[r0:r0 + n, :FFT_W])
                 + _dot(m_ref[:, n:], pq_ref[r0:r0 + n, FFT_W:]))
            o_ref[...] = _dot(f.astype(bf16), w_ref[...]).astype(bf16)


def _seqdft(pq, m, l, w, *, n, tr, nseq, row0):
    nr = n // tr
    nblk = nseq * n // SEG
    pq_specs = [pl.BlockSpec((SEG, 2 * FFT_W), lambda i, b, j=j: (row0 // SEG + j, 0))
                for j in range(nblk)]
    return pl.pallas_call(
        functools.partial(_seqdft_kernel, n=n, nseq=nseq),
        grid=(nr, nseq),
        in_specs=[pl.BlockSpec((tr, 2 * n), lambda i, b: (i, 0))] + pq_specs
        + [pl.BlockSpec((None, FFT_W, FFT_W), lambda i, b: (l, 0, 0))],
        out_specs=pl.BlockSpec((tr, FFT_W), lambda i, b: (b * nr + i, 0)),
        out_shape=jax.ShapeDtypeStruct((nseq * n, FFT_W), bf16),
        compiler_params=_cparams(("arbitrary", "arbitrary")),
        name="seqdft_%d" % n,
    )(m, *([pq] * nblk), w["w_fft"])


def _attn_kernel(*refs, has_cache):
    if has_cache:
        q_ref, k_ref, v_ref, kc_ref, vc_ref, o_ref = refs
    else:
        q_ref, k_ref, v_ref, o_ref = refs
    q = q_ref[...]
    tq = q.shape[0]
    lane = lax.broadcasted_iota(i32, q.shape, 1)
    zero = jnp.zeros_like(q)
    qs = jnp.concatenate([jnp.where(lane < HEAD_DIM, q, zero),
                          jnp.where(lane >= HEAD_DIM, q, zero)], axis=0)
    nt = (((1,), (1,)), ((), ()))
    n = k_ref.shape[0]
    chunk = min(n, ATT_CHUNK)
    parts = [(k_ref, v_ref, c * chunk, chunk) for c in range(n // chunk)]
    if has_cache:
        parts = [(kc_ref, vc_ref, 0, PAST_LEN)] + parts
    m = jnp.full((2 * tq, 1), -jnp.inf, f32)
    den = jnp.zeros((2 * tq, 1), f32)
    acc = jnp.zeros((2 * tq, 2 * HEAD_DIM), f32)
    for kr, vr, off, size in parts:
        s = lax.dot_general(qs, kr[off:off + size, :], nt, preferred_element_type=f32)
        m_new = jnp.maximum(m, jnp.max(s, axis=-1, keepdims=True))
        alpha = jnp.exp2(m - m_new)
        p = jnp.exp2(s - m_new).astype(bf16)
        den = alpha * den + jnp.sum(p.astype(f32), axis=-1, keepdims=True)
        acc = alpha * acc + _dot(p, vr[off:off + size, :])
        m = m_new
    out = acc / den
    o_ref[...] = jnp.where(lane < HEAD_DIM, out[:tq], out[tq:]).astype(bf16)


def _attention(q, kd, vd, cache, *, n, tq, nseq, row0):
    nq = n // tq
    b0 = row0 // n
    q0 = row0 // tq
    in_specs = [
        pl.BlockSpec((tq, 2 * HEAD_DIM), lambda b, h, i: (q0 + b * nq + i, h)),
        pl.BlockSpec((None, n, KV_W), lambda b, h, i: (h, b0 + b, 0)),
        pl.BlockSpec((None, n, KV_W), lambda b, h, i: (h, b0 + b, 0)),
    ]
    args = [q, kd, vd]
    if cache is not None:
        cspec = pl.BlockSpec((None, None, PAST_LEN, KV_W), lambda b, h, i: (h, b, 0, 0))
        in_specs += [cspec, cspec]
        args += list(cache)
    return pl.pallas_call(
        functools.partial(_attn_kernel, has_cache=cache is not None),
        grid=(nseq, N_KV_HEADS, nq),
        in_specs=in_specs,
        out_specs=pl.BlockSpec((tq, 2 * HEAD_DIM), lambda b, h, i: (b * nq + i, h)),
        out_shape=jax.ShapeDtypeStruct((nseq * n, ATTN_W), bf16),
        compiler_params=_cparams(("arbitrary", "arbitrary", "arbitrary")),
        name="attention_%d" % n,
    )(*args)


def _layer_norm(x, g, b):
    mu = jnp.mean(x, axis=-1, keepdims=True)
    xc = x - mu
    var = jnp.mean(xc * xc, axis=-1, keepdims=True)
    return xc * lax.rsqrt(var + LN_EPS) * g + b


def _outproj_kernel(xc_ref, xl_ref, mod_ref, fc_ref, fl_ref, p_ref, s_ref, ac_ref, al_ref, wout_ref,
                    g_ref, b_ref, wr_ref, br_ref, tril_ref,
                    x1_ref, h2_ref, route_ref, cnt_ref, tab_ref, carry_ref):
    i = pl.program_id(0)

    @pl.when(i == 0)
    def _():
        carry_ref[...] = jnp.zeros_like(carry_ref)

    mod = mod_ref[...]
    is_ctx = i < T_CTX // TM
    f_mix = jnp.where(is_ctx, fc_ref[...], fl_ref[...])
    a_mix = jnp.where(is_ctx, ac_ref[...], al_ref[...])
    mix = _dot(jnp.concatenate([f_mix, p_ref[...], s_ref[...], a_mix], axis=1), wout_ref[...])
    x = jnp.where(is_ctx, xc_ref[...], xl_ref[...])
    x1 = _layer_norm(DEEPNORM_ALPHA * x + mod[2:3] * mix, g_ref[...], b_ref[...])
    x1_ref[...] = x1
    h2 = x1 * (1.0 + mod[4:5]) + mod[3:4]
    h2_ref[...] = h2

    h_hi, h_lo = _split_hi_lo(h2)
    hw = _dot(h_hi, wr_ref[...])
    logits = hw[:, :128] + hw[:, 128:] + _dot(h_lo, wr_ref[:, :128]) + br_ref[...]
    lane = lax.broadcasted_iota(i32, logits.shape, 1).astype(f32)
    neg = jnp.float32(-jnp.inf)
    big = jnp.float32(1 << 20)
    gl = jnp.where(lane < N_GROUPS, logits, neg)
    gmax = jnp.max(gl, axis=-1, keepdims=True)
    gsel = jnp.min(jnp.where(gl == gmax, lane, big), axis=-1, keepdims=True)
    pg = 1.0 / jnp.sum(jnp.exp(gl - gmax), axis=-1, keepdims=True)
    e_lo = ROUTE_E0 + gsel * EXPERTS_PER_GROUP
    el = jnp.where((lane >= e_lo) & (lane < e_lo + EXPERTS_PER_GROUP), logits, neg)
    v1 = jnp.max(el, axis=-1, keepdims=True)
    i1 = jnp.min(jnp.where(el == v1, lane, big), axis=-1, keepdims=True)
    el2 = jnp.where(lane == i1, neg, el)
    v2 = jnp.max(el2, axis=-1, keepdims=True)
    i2 = jnp.min(jnp.where(el2 == v2, lane, big), axis=-1, keepdims=True)
    e2 = jnp.exp(v2 - v1)
    w1 = pg / (1.0 + e2)
    w2 = pg * e2 / (1.0 + e2)
    oh1 = lane == i1
    oh2 = lane == i2
    oh = jnp.where(oh1 | oh2, 1.0, 0.0)
    lrank = _dot(tril_ref[...], oh.astype(bf16))
    seg = jnp.floor((jnp.sum(oh, axis=0, keepdims=True) + (ROW_CHUNK - 1.0)) * (1.0 / ROW_CHUNK)) * ROW_CHUNK
    seg8 = jnp.broadcast_to(seg, (8, 128))
    lane8 = lax.broadcasted_iota(i32, (8, 128), 1)
    off8 = seg8
    for sh in (1, 2, 4, 8, 16):
        off8 = off8 + jnp.where(lane8 >= sh, pltpu.roll(off8, sh, 1), 0.0)
    off8 = off8 - seg8
    carry = carry_ref[...]
    lpos = lrank + off8[0:1, :]
    pick = lambda sel, val: jnp.sum(jnp.where(sel, val, 0.0), axis=-1, keepdims=True)
    sub8 = lax.broadcasted_iota(i32, (8, 128), 0)
    tab_ref[...] = jnp.where(sub8 == 0, seg8, jnp.where(sub8 == 1, off8, jnp.where(sub8 == 2, carry, 0.0)))
    carry = carry + seg8
    carry_ref[...] = carry
    cnt_ref[...] = carry
    cols = (i1 - ROUTE_E0, i2 - ROUTE_E0, w1, w2, pick(oh1, lpos), pick(oh2, lpos))
    route = jnp.zeros_like(logits)
    for j, col in enumerate(cols):
        route = jnp.where(lane == j, col, route)
    route_ref[...] = route


def _outproj(x_ctx, x_lat, mod, fo_ctx, fo_lat, po, so, ao_ctx, ao_lat, l, w):
    nt = T_ALL // TM
    tile = lambda wd: pl.BlockSpec((TM, wd), lambda i: (i, 0))
    vec = lambda wd: pl.BlockSpec((None, 1, wd), lambda i: (l, 0, 0))
    return pl.pallas_call(
        _outproj_kernel,
        grid=(nt,),
        in_specs=[
            _ctx_tile(D_MODEL), _lat_tile(D_MODEL),
            pl.BlockSpec((None, None, 6, D_MODEL), lambda i: (l, i // (SEG // TM), 0, 0)),
            _ctx_tile(FFT_W), _lat_tile(FFT_W), tile(POOL_W), tile(SGU_W),
            _ctx_tile(ATTN_W), _lat_tile(ATTN_W),
            pl.BlockSpec((None, D_MODEL, D_MODEL), lambda i: (l, 0, 0)),
            vec(D_MODEL), vec(D_MODEL),
            pl.BlockSpec((None, D_MODEL, 256), lambda i: (l, 0, 0)),
            vec(128),
            pl.BlockSpec((TM, TM), lambda i: (0, 0)),
        ],
        out_specs=[tile(D_MODEL), tile(D_MODEL), tile(128), pl.BlockSpec((8, 128), lambda i: (0, 0)),
                   pl.BlockSpec((None, 8, 128), lambda i: (i, 0, 0))],
        out_shape=[
            jax.ShapeDtypeStruct((T_ALL, D_MODEL), f32),
            jax.ShapeDtypeStruct((T_ALL, D_MODEL), f32),
            jax.ShapeDtypeStruct((T_ALL, 128), f32),
            jax.ShapeDtypeStruct((8, 128), f32),
            jax.ShapeDtypeStruct((nt, 8, 128), f32),
        ],
        scratch_shapes=[pltpu.VMEM((8, 128), f32)],
        compiler_params=_cparams(("arbitrary",)),
        name="outproj",
    )(x_ctx, x_lat, mod, fo_ctx, fo_lat, po, so, ao_ctx, ao_lat,
      w["w_out"], w["ln1_g"], w["ln1_b"], w["w_r"], w["b_r"],
      w["tril"])


def _plan_kernel(cnt_ref, meta_ref):
    lane = lax.broadcasted_iota(i32, (8, 128), 1)
    sub = lax.broadcasted_iota(i32, (8, 128), 0)
    cnt = cnt_ref[...]
    is_e = (lane >= ROUTE_E0) & (lane < ROUTE_E0 + N_EXPERTS)
    tiles = jnp.where(is_e, jnp.floor((cnt + (MOE_TM - 1.0)) * (1.0 / MOE_TM)), 0.0)
    cum = tiles
    for s in (1, 2, 4, 8, 16):
        cum = cum + jnp.where(lane >= s, pltpu.roll(cum, s, 1), 0.0)
    pstart = (cum - tiles) * MOE_TM
    nused = jnp.max(cum, axis=-1, keepdims=True)
    fill = jnp.where(is_e & (cnt != tiles * MOE_TM), pstart + (tiles - 1.0) * MOE_TM, -1.0)
    meta = jnp.where(sub == 0, cnt, jnp.where(sub == 1, nused, jnp.where(sub == 2, fill,
                     jnp.where(sub == 3, pstart, 0.0))))
    meta_ref[...] = meta.astype(i32)


def _plan(cnt):
    return pl.pallas_call(
        _plan_kernel,
        grid=(1,),
        in_specs=[pl.BlockSpec((8, 128), lambda i: (0, 0))],
        out_specs=pl.BlockSpec((8, 128), lambda i: (0, 0)),
        out_shape=jax.ShapeDtypeStruct((8, 128), i32),
        compiler_params=_cparams(("arbitrary",)),
        name="plan",
    )(cnt)


def _dispatch_kernel(nch_ref, off_ref, dst_ref, tot_ref, fill_ref, nused_ref, h_ref, route_ref, xs_ref,
                     sorted_ref, zero_ref, sem, fill_sem):
    i = pl.program_id(0)

    def tile_fill(row0):
        return pltpu.make_async_copy(zero_ref, xs_ref.at[pl.ds(pl.multiple_of(row0, MOE_TM), MOE_TM)],
                                     fill_sem)

    @pl.when(i == 0)
    def _():
        zero_ref[...] = jnp.zeros_like(zero_ref)

        def start(e, c):
            @pl.when(fill_ref[e] >= 0)
            def _():
                tile_fill(jnp.maximum(fill_ref[e], 0)).start()
            return c

        def wait(e, c):
            @pl.when(fill_ref[e] >= 0)
            def _():
                tile_fill(jnp.maximum(fill_ref[e], 0)).wait()
            return c

        def start_tail(t, c):
            tile_fill(t * MOE_TM).start()
            return c

        def wait_tail(t, c):
            tile_fill(t * MOE_TM).wait()
            return c

        lax.fori_loop(0, N_EXPERTS, start, 0)
        lax.fori_loop(nused_ref[0], MOE_NT, start_tail, 0)
        lax.fori_loop(0, N_EXPERTS, wait, 0)
        lax.fori_loop(nused_ref[0], MOE_NT, wait_tail, 0)

    rt = route_ref[...].T
    j = lax.broadcasted_iota(i32, (DISP_ROWS, 1), 0).astype(f32)
    sel = jnp.where((j == rt[4:5, :]) | (j == rt[5:6, :]), 1.0, 0.0).astype(bf16)
    slot = i % 2
    sorted_ref[slot] = _dot(sel, h_ref[...].astype(bf16))

    def per_expert(e, c):
        idx = i * N_EXPERTS + e
        n, s0, d0 = nch_ref[idx], off_ref[idx], dst_ref[idx]
        b = 1
        while b <= TM // ROW_CHUNK:
            @pl.when((n & b) != 0)
            def _(b=b):
                r0 = (n & (b - 1)) * ROW_CHUNK
                rows = b * ROW_CHUNK
                pltpu.make_async_copy(
                    sorted_ref.at[slot, pl.ds(pl.multiple_of(s0 + r0, ROW_CHUNK), rows)],
                    xs_ref.at[pl.ds(pl.multiple_of(d0 + r0, ROW_CHUNK), rows)], sem.at[slot]).start()
            b *= 2
        return c

    lax.fori_loop(0, N_EXPERTS, per_expert, 0)

    def drain(tile, s):
        rows = tot_ref[tile] * ROW_CHUNK

        @pl.when(rows > 0)
        def _():
            pltpu.make_async_copy(sorted_ref.at[s, pl.ds(0, rows)], xs_ref.at[pl.ds(0, rows)],
                                  sem.at[s]).wait()

    @pl.when(i >= 1)
    def _():
        drain(i - 1, 1 - slot)

    @pl.when(i == pl.num_programs(0) - 1)
    def _():
        drain(i, slot)


def _dispatch(nch, off, dst, tot, fill, nused, h2, route):
    grid_spec = pltpu.PrefetchScalarGridSpec(
        num_scalar_prefetch=6,
        grid=(T_ALL // TM,),
        in_specs=[pl.BlockSpec((TM, D_MODEL), lambda i, *_: (i, 0)),
                  pl.BlockSpec((TM, 128), lambda i, *_: (i, 0))],
        out_specs=pl.BlockSpec(memory_space=pl.ANY),
        scratch_shapes=[pltpu.VMEM((2, DISP_ROWS, D_MODEL), f32), pltpu.VMEM((MOE_TM, D_MODEL), f32),
                        pltpu.SemaphoreType.DMA((2,)), pltpu.SemaphoreType.DMA(())],
    )
    return pl.pallas_call(
        _dispatch_kernel,
        grid_spec=grid_spec,
        out_shape=jax.ShapeDtypeStruct((MOE_NT * MOE_TM, D_MODEL), f32),
        compiler_params=_cparams(("arbitrary",)),
        name="dispatch",
    )(nch, off, dst, tot, fill, nused, h2, route)


def _experts_kernel(cnt_ref, nused_ref, xs_ref, wg_hbm, wu_hbm, wd_hbm, ys_ref,
                    wg_f, wu_f, wd_f, wg_b, wu_b, wd_b, st, wsem, *, layer):
    i = pl.program_id(0)
    nused = nused_ref[0]
    NXT, NSLOT, LEFT = 0, 1, 2

    def w_copies(e, slot):
        return (pltpu.make_async_copy(wg_hbm.at[layer, e], wg_f.at[slot], wsem.at[slot, 0]),
                pltpu.make_async_copy(wu_hbm.at[layer, e], wu_f.at[slot], wsem.at[slot, 1]),
                pltpu.make_async_copy(wd_hbm.at[layer, e], wd_f.at[slot], wsem.at[slot, 2]))

    def next_nonempty(e):
        return lax.while_loop(
            lambda v: (v < N_EXPERTS) & (cnt_ref[jnp.minimum(v, N_EXPERTS - 1)] == 0),
            lambda v: v + 1, e)

    @pl.when(i == 0)
    def _():
        e0 = next_nonempty(jnp.int32(0))
        for c in w_copies(e0, 0):
            c.start()
        st[NXT] = e0
        st[NSLOT] = 0
        st[LEFT] = 0

    @pl.when(i < nused)
    def _():
        @pl.when(st[LEFT] == 0)
        def _():
            e = st[NXT]
            slot = st[NSLOT]
            for c in w_copies(e, slot):
                c.wait()
            e2 = next_nonempty(e + 1)

            @pl.when(e2 < N_EXPERTS)
            def _():
                for c in w_copies(e2, 1 - slot):
                    c.start()

            st[NXT] = e2
            st[NSLOT] = 1 - slot
            st[LEFT] = (cnt_ref[e] + (MOE_TM - 1)) // MOE_TM
            wg_b[...] = wg_f[slot].astype(bf16)
            wu_b[...] = wu_f[slot].astype(bf16)
            wd_b[...] = wd_f[slot].astype(bf16)

        x = xs_ref[...].astype(bf16)
        hg = _dot(x, wg_b[...])
        hu = _dot(x, wu_b[...])
        act = (hg * jax.nn.sigmoid(hg)) * hu
        ys_ref[...] = _dot(act.astype(bf16), wd_b[...])
        st[LEFT] = st[LEFT] - 1

    @pl.when(i >= nused)
    def _():
        ys_ref[...] = jnp.zeros_like(ys_ref)


def _experts(counts, nused, xs, l, w_gate, w_up, w_down):
    hbm = pl.BlockSpec(memory_space=pl.ANY)
    grid_spec = pltpu.PrefetchScalarGridSpec(
        num_scalar_prefetch=2,
        grid=(MOE_NT,),
        in_specs=[pl.BlockSpec((MOE_TM, D_MODEL), lambda i, c, nu: (jnp.minimum(i, nu[0] - 1), 0)),
                  hbm, hbm, hbm],
        out_specs=pl.BlockSpec((MOE_TM, D_MODEL), lambda i, c, nu: (i, 0)),
        scratch_shapes=[
            pltpu.VMEM((2, D_MODEL, EXPERT_FF), f32),
            pltpu.VMEM((2, D_MODEL, EXPERT_FF), f32),
            pltpu.VMEM((2, EXPERT_FF, D_MODEL), f32),
            pltpu.VMEM((D_MODEL, EXPERT_FF), bf16),
            pltpu.VMEM((D_MODEL, EXPERT_FF), bf16),
            pltpu.VMEM((EXPERT_FF, D_MODEL), bf16),
            pltpu.SMEM((4,), i32),
            pltpu.SemaphoreType.DMA((2, 3)),
        ],
    )
    return pl.pallas_call(
        functools.partial(_experts_kernel, layer=l),
        grid_spec=grid_spec,
        out_shape=jax.ShapeDtypeStruct((MOE_NT * MOE_TM, D_MODEL), f32),
        compiler_params=_cparams(("arbitrary",)),
        name="experts",
    )(counts, nused, xs, w_gate, w_up, w_down)


def _combine_kernel(nch_ref, off_ref, dst_ref, tot_ref, x1_ref, mod_ref, route_ref, ys_hbm, g_ref, b_ref,
                    oc_ref, ol_ref, ybuf, sem):
    i = pl.program_id(0)
    nt = pl.num_programs(0) - 1

    @pl.when(i == 0)
    def _():
        ybuf[...] = jnp.zeros_like(ybuf)

    @pl.when(i < nt)
    def _():
        slot = i % 2

        def per_expert(e, c):
            idx = i * N_EXPERTS + e
            n, s0, d0 = nch_ref[idx], off_ref[idx], dst_ref[idx]
            b = 1
            while b <= TM // ROW_CHUNK:
                @pl.when((n & b) != 0)
                def _(b=b):
                    r0 = (n & (b - 1)) * ROW_CHUNK
                    rows = b * ROW_CHUNK
                    pltpu.make_async_copy(
                        ys_hbm.at[pl.ds(pl.multiple_of(d0 + r0, ROW_CHUNK), rows)],
                        ybuf.at[slot, pl.ds(pl.multiple_of(s0 + r0, ROW_CHUNK), rows)],
                        sem.at[slot]).start()
                b *= 2
            return c

        lax.fori_loop(0, N_EXPERTS, per_expert, 0)

    @pl.when(i >= 1)
    def _():
        slot = (i - 1) % 2
        rows = tot_ref[i - 1] * ROW_CHUNK

        @pl.when(rows > 0)
        def _():
            pltpu.make_async_copy(ys_hbm.at[pl.ds(0, rows)], ybuf.at[slot, pl.ds(0, rows)],
                                  sem.at[slot]).wait()

        route = route_ref[...]
        mod = mod_ref[...]
        lane = lax.broadcasted_iota(i32, (1, DISP_ROWS), 1).astype(f32)
        wmat = (jnp.where(lane == route[:, 4:5], route[:, 2:3], 0.0)
                + jnp.where(lane == route[:, 5:6], route[:, 3:4], 0.0))
        moe = _dot(wmat.astype(bf16), ybuf[slot].astype(bf16))
        y = _layer_norm(DEEPNORM_ALPHA * x1_ref[...] + mod[5:6] * moe, g_ref[...], b_ref[...])

        @pl.when(i - 1 < T_CTX // TM)
        def _():
            oc_ref[...] = y

        @pl.when(i - 1 >= T_CTX // TM)
        def _():
            ol_ref[...] = y


def _combine(nch, off, dst, tot, x1, mod, route, ys, l, w):
    nt = T_ALL // TM
    vec = pl.BlockSpec((None, 1, D_MODEL), lambda i, *_: (l, 0, 0))
    nctx = T_CTX // TM
    prev = lambda i: jnp.maximum(i - 1, 0)
    grid_spec = pltpu.PrefetchScalarGridSpec(
        num_scalar_prefetch=4,
        grid=(nt + 1,),
        in_specs=[
            pl.BlockSpec((TM, D_MODEL), lambda i, *_: (prev(i), 0)),
            pl.BlockSpec((None, None, 6, D_MODEL), lambda i, *_: (l, prev(i) // (SEG // TM), 0, 0)),
            pl.BlockSpec((TM, 128), lambda i, *_: (prev(i), 0)),
            pl.BlockSpec(memory_space=pl.ANY),
            vec, vec,
        ],
        out_specs=[pl.BlockSpec((TM, D_MODEL), lambda i, *_: (jnp.minimum(prev(i), nctx - 1), 0)),
                   pl.BlockSpec((TM, D_MODEL), lambda i, *_: (jnp.maximum(prev(i) - nctx, 0), 0))],
        scratch_shapes=[pltpu.VMEM((2, DISP_ROWS, D_MODEL), f32), pltpu.SemaphoreType.DMA((2,))],
    )
    return pl.pallas_call(
        _combine_kernel,
        grid_spec=grid_spec,
        out_shape=[jax.ShapeDtypeStruct((T_CTX, D_MODEL), f32),
                   jax.ShapeDtypeStruct((T_LAT, D_MODEL), f32)],
        compiler_params=_cparams(("arbitrary",)),
        name="combine",
    )(nch, off, dst, tot, x1, mod, route, ys, w["ln2_g"], w["ln2_b"])


def _dft_cos_sin(n, scale):
    k = jnp.arange(n, dtype=i32)
    ang = ((k[:, None] * k[None, :]) % n).astype(f32) * np.float32(2.0 * np.pi / n)
    return jnp.cos(ang) * scale, jnp.sin(ang) * scale


def _seq_dft_matrix(n):
    g = min(DFT_SPLIT, n)
    j = jnp.arange(n, dtype=i32)[None, :]
    k1 = jnp.arange(n // g, dtype=i32)[:, None]
    k2 = jnp.arange(g, dtype=i32)[:, None]
    ang_a = ((k1 * j) % (n // g)).astype(f32) * np.float32(2.0 * np.pi * g / n)
    ang_b = ((k2 * j) % n).astype(f32) * np.float32(2.0 * np.pi / n)
    scale = np.float32(n ** -0.5)
    ca, sa = jnp.cos(ang_a), jnp.sin(ang_a)
    cb, sb = jnp.cos(ang_b) * scale, jnp.sin(ang_b) * scale
    ca2 = jnp.concatenate([ca, ca], axis=1)[:, None, :]
    sa2 = jnp.concatenate([sa, sa], axis=1)[:, None, :]
    cb2 = jnp.concatenate([cb, -sb], axis=1)[None, :, :]
    sb2 = jnp.concatenate([sb, cb], axis=1)[None, :, :]
    return (ca2 * cb2 - sa2 * sb2).astype(bf16).reshape(n, 2 * n)


def _rope_tables():
    rows = DEC_SEQ // GRID_W
    row = jnp.repeat(jnp.arange(rows), GRID_W).astype(f32)
    col = jnp.tile(jnp.arange(GRID_W), rows).astype(f32)
    n_freq = HEAD_DIM // 4
    inv = ROPE_THETA ** (-jnp.arange(n_freq, dtype=f32) / n_freq)
    ar = row[:, None] * inv
    ac = col[:, None] * inv
    ang = jnp.concatenate([ar, ar, ac, ac], axis=-1)
    cos = jnp.tile(jnp.cos(ang), (1, N_HEADS))
    sin = jnp.tile(jnp.sin(ang), (1, N_HEADS))
    first = (jnp.arange(ATTN_W) % (HEAD_DIM // 2)) < n_freq
    sin_a = jnp.where(first[None, :], -sin, 0.0)
    sin_b = jnp.where(first[None, :], 0.0, sin)
    ident = jnp.zeros((TM, ATTN_W), f32)
    return (jnp.concatenate([cos, ident + 1.0], axis=0),
            jnp.concatenate([sin_a, ident], axis=0),
            jnp.concatenate([sin_b, ident], axis=0))


def _dup_cache(cache):
    c = jnp.transpose(cache, (1, 3, 0, 2, 4))
    return jnp.concatenate([c, c], axis=-1).astype(bf16)


def kernel(x_prompt, x_sample, cache_k, cache_v, c, c_ctx, w_mod, b_mod, w_in, w_fft, w_pool, pool_scale, sgu_ln_g, sgu_ln_b, w_sgu, b_sgu, q_norm_g, k_norm_g, w_out, ln1_g, ln1_b, w_router_group, b_router_group, w_router_expert, b_router_expert, w_gate, w_up, w_down, ln2_g, ln2_b):
    L = DEPTH
    x_ctx = x_prompt.reshape(T_CTX, D_MODEL)
    x_lat = x_sample.reshape(T_LAT, D_MODEL)

    cond8 = jnp.concatenate([c_ctx[None, :], c, jnp.zeros((8 - 1 - DEC_BATCH, D_MODEL), f32)], axis=0)
    mod = _modulation(cond8, w_mod, b_mod)[:, :N_SEG].reshape(L, N_SEG, 6, D_MODEL)

    cc, sc = _dft_cos_sin(FFT_W, np.float32(FFT_W ** -0.5))
    rope_cos, rope_sin_a, rope_sin_b = _rope_tables()
    head_id = jnp.arange(ATTN_W) // HEAD_DIM
    eye_g = jnp.eye(len(POOL_WINDOWS), dtype=f32)
    w_r = jnp.zeros((L, D_MODEL, 128), f32)
    w_r = w_r.at[:, :, :N_GROUPS].set(w_router_group).at[:, :, ROUTE_E0:ROUTE_E0 + N_EXPERTS].set(w_router_expert)
    b_r = jnp.zeros((L, 1, 128), f32)
    b_r = b_r.at[:, 0, :N_GROUPS].set(b_router_group).at[:, 0, ROUTE_E0:ROUTE_E0 + N_EXPERTS].set(b_router_expert)
    w_r_hi, w_r_lo = _split_hi_lo(w_r)
    w = {
        "w_in": w_in.astype(bf16),
        "csc": jnp.concatenate([cc, sc], axis=1).astype(bf16),
        "w_sgu": jnp.transpose(w_sgu, (0, 2, 1, 3)).reshape(L, CHUNK, SGU_HEADS * CHUNK).astype(bf16),
        "b_sgu": jnp.repeat(jnp.transpose(b_sgu, (0, 2, 1)), SGU_W // SGU_HEADS, axis=2),
        "sgu_ln_g": sgu_ln_g.reshape(L, 1, SGU_W),
        "sgu_ln_b": sgu_ln_b.reshape(L, 1, SGU_W),
        "q_norm_g": jnp.tile(q_norm_g, (1, N_HEADS)).reshape(L, 1, ATTN_W),
        "k_norm_g": jnp.tile(k_norm_g, (1, N_KV_HEADS)).reshape(L, 1, KV_W),
        "rope_cos": rope_cos, "rope_sin_a": rope_sin_a, "rope_sin_b": rope_sin_b,
        "ones_bd": (head_id[:, None] == head_id[None, :]).astype(bf16),
        "w_pool_bd": jnp.einsum("lgcd,gh->lgchd", w_pool, eye_g).reshape(L, POOL_W, POOL_W).astype(bf16),
        "pool_scale": pool_scale.reshape(L, 1, POOL_W),
        "w_fft": w_fft.astype(bf16),
        "w_out": w_out.astype(bf16),
        "ln1_g": ln1_g.reshape(L, 1, D_MODEL), "ln1_b": ln1_b.reshape(L, 1, D_MODEL),
        "ln2_g": ln2_g.reshape(L, 1, D_MODEL), "ln2_b": ln2_b.reshape(L, 1, D_MODEL),
        "w_r": jnp.concatenate([w_r_hi, w_r_lo], axis=-1), "b_r": b_r,
        "tril": (jnp.arange(TM)[:, None] > jnp.arange(TM)[None, :]).astype(bf16),
    }
    m_ctx = _seq_dft_matrix(SEQ)
    m_lat = _seq_dft_matrix(DEC_SEQ)
    kc_all = _dup_cache(cache_k)
    vc_all = _dup_cache(cache_v)

    new_k, new_v = [], []
    for l in range(L):
        pq, praw, sgu, q, kd, vd, nk, nv = _inproj(x_ctx, x_lat, mod, l, w)
        new_k.append(nk[:T_CTX].reshape(BATCH, SEQ, N_KV_HEADS, HEAD_DIM))
        new_v.append(nv[:T_CTX].reshape(BATCH, SEQ, N_KV_HEADS, HEAD_DIM))
        po = _pool(praw, l, w)
        fo_ctx = _seqdft(pq, m_ctx, l, w, n=SEQ, tr=SEQ, nseq=BATCH, row0=0)
        fo_lat = _seqdft(pq, m_lat, l, w, n=DEC_SEQ, tr=FFT_TR, nseq=DEC_BATCH, row0=T_CTX)
        ao_ctx = _attention(q, kd, vd, None, n=SEQ, tq=SEQ, nseq=BATCH, row0=0)
        ao_lat = _attention(q, kd, vd, (kc_all[l], vc_all[l]), n=DEC_SEQ, tq=ATT_TQ, nseq=DEC_BATCH,
                            row0=T_CTX)
        x1, h2, route, cnt, tab = _outproj(x_ctx, x_lat, mod, fo_ctx, fo_lat, po, sgu, ao_ctx, ao_lat, l, w)
        meta = _plan(cnt)
        experts = slice(ROUTE_E0, ROUTE_E0 + N_EXPERTS)
        counts = meta[0, experts]
        nused = meta[1, :1]
        fill = meta[2, experts]
        tab = tab[:, :, experts].astype(i32)
        nch = (tab[:, 0] // ROW_CHUNK).reshape(-1)
        off = tab[:, 1].reshape(-1)
        dst = (meta[3, experts][None, :] + tab[:, 2]).reshape(-1)
        tot = jnp.sum(tab[:, 0], axis=1) // ROW_CHUNK
        xs = _dispatch(nch, off, dst, tot, fill, nused, h2, route)
        ys = _experts(counts, nused, xs, l, w_gate, w_up, w_down)
        x_ctx, x_lat = _combine(nch, off, dst, tot, x1, mod, route, ys, l, w)

    y_prompt = x_ctx.reshape(BATCH, SEQ, D_MODEL)
    y_sample = x_lat.reshape(DEC_BATCH, DEC_SEQ, D_MODEL)
    return (y_prompt, y_sample, jnp.stack(new_k, axis=1), jnp.stack(new_v, axis=1))
```

```python
import functools

import numpy as np
import jax
import jax.numpy as jnp
from jax import lax
from jax.experimental import pallas as pl
from jax.experimental.pallas import tpu as pltpu

f32 = jnp.float32
bf16 = jnp.bfloat16
i32 = jnp.int32

D_MODEL = 1024
BATCH = 16
SEQ = 256
DEPTH = 4
DEC_BATCH = 2
DEC_SEQ = 4096
PAST_LEN = 512
GRID_W = 64
FFT_W = 256
POOL_W = 256
POOL_WINDOWS = (2, 4, 8, 16)
POOL_GROUP = 64
SGU_W = 256
SGU_HEADS = 4
CHUNK = 128
HEAD_DIM = 64
ATTN_W = 256
N_HEADS = 4
N_KV_HEADS = 2
KV_W = 128
IN_W = 1536
ROPE_THETA = 10000.0
N_GROUPS = 4
EXPERTS_PER_GROUP = 8
N_EXPERTS = 32
EXPERT_FF = 512
DEEPNORM_ALPHA = float((2 * DEPTH) ** 0.25)
LN_EPS = 1e-5
RMS_EPS = 1e-6

T_CTX = BATCH * SEQ
T_LAT = DEC_BATCH * DEC_SEQ
T_ALL = T_CTX + T_LAT
SEG = 4096
N_SEG = T_ALL // SEG

TM = 512
POOL_TB = 512
POOL_HALO = 8
FFT_TR = 512
ATT_TQ = 512
ATT_CHUNK = 1024
DFT_SPLIT = 64
MOE_TM = 256
ROW_CHUNK = 16
MOE_ROWS = 2 * T_ALL
MOE_PAD_ROWS = (T_ALL // TM) * N_EXPERTS * (ROW_CHUNK - 1)
MOE_NT = -(-(MOE_ROWS + MOE_PAD_ROWS) // MOE_TM) + N_EXPERTS
DISP_ROWS = 2 * TM + N_EXPERTS * ROW_CHUNK
ROUTE_E0 = 32
VMEM_LIMIT = 56 * 1024 * 1024


def _cparams(sem):
    return pltpu.CompilerParams(dimension_semantics=sem, vmem_limit_bytes=VMEM_LIMIT)


def _split_hi_lo(a):
    hi = a.astype(bf16)
    lo = (a - hi.astype(f32)).astype(bf16)
    return hi, lo


def _dot(a, b):
    return jnp.dot(a, b, preferred_element_type=f32)


def _mod_kernel(c_ref, w_ref, b_ref, o_ref):
    c = c_ref[...]
    s = c * jax.nn.sigmoid(c)
    s_hi, s_lo = _split_hi_lo(s)
    w_hi, w_lo = _split_hi_lo(w_ref[...])
    o_ref[...] = _dot(s_hi, w_hi) + _dot(s_hi, w_lo) + _dot(s_lo, w_hi) + b_ref[...]


def _modulation(cond8, w_mod, b_mod):
    tn = 1536
    return pl.pallas_call(
        _mod_kernel,
        grid=(DEPTH, 6 * D_MODEL // tn),
        in_specs=[
            pl.BlockSpec((8, D_MODEL), lambda l, j: (0, 0)),
            pl.BlockSpec((None, D_MODEL, tn), lambda l, j: (l, 0, j)),
            pl.BlockSpec((None, 1, tn), lambda l, j: (l, 0, j)),
        ],
        out_specs=pl.BlockSpec((None, 8, tn), lambda l, j: (l, 0, j)),
        out_shape=jax.ShapeDtypeStruct((DEPTH, 8, 6 * D_MODEL), f32),
        compiler_params=_cparams(("arbitrary", "arbitrary")),
        name="modulation",
    )(cond8, w_mod, b_mod.reshape(DEPTH, 1, 6 * D_MODEL))


def _head_rms(x, ones_bd, gain):
    ss = _dot((x * x).astype(bf16), ones_bd)
    return x * lax.rsqrt(ss * (1.0 / HEAD_DIM) + RMS_EPS) * gain


def _rope(x, cos, sin_a, sin_b):
    w = x.shape[-1]
    q4 = HEAD_DIM // 4
    return x * cos + pltpu.roll(x, w - q4, 1) * sin_a + pltpu.roll(x, q4, 1) * sin_b


def _dup_half(x, first):
    lane = lax.broadcasted_iota(i32, x.shape, 1)
    r = pltpu.roll(x, HEAD_DIM, 1)
    if first:
        return jnp.where(lane < HEAD_DIM, x, r)
    return jnp.where(lane >= HEAD_DIM, x, r)


def _gelu_tanh(x):
    c = np.sqrt(2.0 / np.pi).astype(np.float32)
    return x * (0.5 * (1.0 + jnp.tanh(c * (x + 0.044715 * (x * x * x)))))


def _inproj_kernel(xc_ref, xl_ref, mod_ref, win_ref, csc_ref, wsgu_ref, bsgu_ref, lng_ref, lnb_ref,
                   qg_ref, kg_ref, cos_ref, sina_ref, sinb_ref, ones_ref,
                   pq_ref, pool_ref, sgu_ref, q_ref, kd_ref, vd_ref, nk_ref, nv_ref):
    x = jnp.where(pl.program_id(0) < T_CTX // TM, xc_ref[...], xl_ref[...])
    mod = mod_ref[...]
    h = (x * (1.0 + mod[1:2]) + mod[0:1]).astype(bf16)
    proj = _dot(h, win_ref[...])

    a = proj[:, 0:FFT_W].astype(bf16)
    pq_ref[...] = _dot(a, csc_ref[...]).astype(bf16)

    pool_ref[...] = proj[:, FFT_W:FFT_W + POOL_W]

    o = FFT_W + POOL_W
    hgu = _gelu_tanh(proj[:, o:o + 2 * SGU_W])
    u = hgu[:, :SGU_W]
    v = hgu[:, SGU_W:]
    mu = jnp.mean(v, axis=-1, keepdims=True)
    vc = v - mu
    var = jnp.mean(vc * vc, axis=-1, keepdims=True)
    v = vc * lax.rsqrt(var + LN_EPS) * lng_ref[...] + lnb_ref[...]
    lane = lax.broadcasted_iota(i32, (CHUNK, SGU_W), 1)
    head = lane // (SGU_W // SGU_HEADS)
    wcat = wsgu_ref[...]
    for cidx in range(TM // CHUNK):
        rows = slice(cidx * CHUNK, (cidx + 1) * CHUNK)
        vch = v[rows]
        vblk = jnp.concatenate(
            [jnp.where(head == g, vch, 0.0) for g in range(SGU_HEADS)], axis=0).astype(bf16)
        sp = _dot(wcat, vblk) + bsgu_ref[...]
        sgu_ref[rows, :] = (u[rows] * sp).astype(bf16)

    o = o + 2 * SGU_W
    ones_bd = ones_ref[...]
    cos = cos_ref[...]
    sin_a = sina_ref[...]
    sin_b = sinb_ref[...]
    q = _head_rms(proj[:, o:o + ATTN_W], ones_bd, qg_ref[...])
    q = _rope(q, cos, sin_a, sin_b) * np.float32(HEAD_DIM ** -0.5 * np.log2(np.e))
    q_ref[...] = q.astype(bf16)
    o = o + ATTN_W
    k = _head_rms(proj[:, o:o + KV_W], ones_bd[:KV_W, :KV_W], kg_ref[...])
    nk_ref[...] = k
    k = _rope(k, cos[:, :KV_W], sin_a[:, :KV_W], sin_b[:, :KV_W])
    kd_ref[0] = _dup_half(k, True).astype(bf16)
    kd_ref[1] = _dup_half(k, False).astype(bf16)
    o = o + KV_W
    vv = proj[:, o:o + KV_W]
    nv_ref[...] = vv
    vd_ref[0] = _dup_half(vv, True).astype(bf16)
    vd_ref[1] = _dup_half(vv, False).astype(bf16)


def _rope_block(i):
    nlat = DEC_SEQ // TM
    nctx = T_CTX // TM
    return jnp.where(i < nctx, nlat, (i - nctx) % nlat)


def _ctx_tile(wd):
    return pl.BlockSpec((TM, wd), lambda i, *_: (jnp.minimum(i, T_CTX // TM - 1), 0))


def _lat_tile(wd):
    return pl.BlockSpec((TM, wd), lambda i, *_: (jnp.maximum(i - T_CTX // TM, 0), 0))


def _inproj(x_ctx, x_lat, mod, l, w):
    nt = T_ALL // TM
    tile = lambda wd: pl.BlockSpec((TM, wd), lambda i: (i, 0))
    const = lambda shape: pl.BlockSpec(shape, lambda i: (0,) * len(shape))
    rope_spec = pl.BlockSpec((TM, ATTN_W), lambda i: (_rope_block(i), 0))
    return pl.pallas_call(
        _inproj_kernel,
        grid=(nt,),
        in_specs=[
            _ctx_tile(D_MODEL), _lat_tile(D_MODEL),
            pl.BlockSpec((None, None, 6, D_MODEL), lambda i: (l, i // (SEG // TM), 0, 0)),
            pl.BlockSpec((None, D_MODEL, IN_W), lambda i: (l, 0, 0)),
            const((FFT_W, 2 * FFT_W)),
            pl.BlockSpec((None, CHUNK, SGU_HEADS * CHUNK), lambda i: (l, 0, 0)),
            pl.BlockSpec((None, CHUNK, SGU_W), lambda i: (l, 0, 0)),
            pl.BlockSpec((None, 1, SGU_W), lambda i: (l, 0, 0)),
            pl.BlockSpec((None, 1, SGU_W), lambda i: (l, 0, 0)),
            pl.BlockSpec((None, 1, ATTN_W), lambda i: (l, 0, 0)),
            pl.BlockSpec((None, 1, KV_W), lambda i: (l, 0, 0)),
            rope_spec, rope_spec, rope_spec,
            const((ATTN_W, ATTN_W)),
        ],
        out_specs=[
            tile(2 * FFT_W), tile(POOL_W), tile(SGU_W), tile(ATTN_W),
            pl.BlockSpec((N_KV_HEADS, TM, KV_W), lambda i: (0, i, 0)),
            pl.BlockSpec((N_KV_HEADS, TM, KV_W), lambda i: (0, i, 0)),
            tile(KV_W), tile(KV_W),
        ],
        out_shape=[
            jax.ShapeDtypeStruct((T_ALL, 2 * FFT_W), bf16),
            jax.ShapeDtypeStruct((T_ALL, POOL_W), f32),
            jax.ShapeDtypeStruct((T_ALL, SGU_W), bf16),
            jax.ShapeDtypeStruct((T_ALL, ATTN_W), bf16),
            jax.ShapeDtypeStruct((N_KV_HEADS, T_ALL, KV_W), bf16),
            jax.ShapeDtypeStruct((N_KV_HEADS, T_ALL, KV_W), bf16),
            jax.ShapeDtypeStruct((T_ALL, KV_W), f32),
            jax.ShapeDtypeStruct((T_ALL, KV_W), f32),
        ],
        compiler_params=_cparams(("arbitrary",)),
        name="inproj",
    )(x_ctx, x_lat, mod, w["w_in"], w["csc"], w["w_sgu"], w["b_sgu"], w["sgu_ln_g"], w["sgu_ln_b"],
      w["q_norm_g"], w["k_norm_g"], w["rope_cos"], w["rope_sin_a"], w["rope_sin_b"], w["ones_bd"])


def _pool_kernel(prev_ref, cur_ref, next_ref, wp_ref, scale_ref, o_ref):
    i = pl.program_id(0)
    n = jnp.where(i < T_CTX // POOL_TB, SEQ, DEC_SEQ)
    hl = POOL_HALO
    ext = jnp.concatenate([prev_ref[POOL_TB - hl:, :], cur_ref[...], next_ref[:hl, :]], axis=0)
    rows = POOL_TB + 2 * hl
    r = lax.broadcasted_iota(i32, (rows, 1), 0)
    pos = (i * POOL_TB + r - hl) & (n - 1)

    def back(a, s):
        return jnp.where(pos >= s, pltpu.roll(a, s, 0), 0.0)

    def fwd(a, s):
        return jnp.where(pos + s < n, pltpu.roll(a, rows - s, 0), 0.0)

    bsum = [back(ext, 1)]
    fsum = [ext]
    for k in range(3):
        s = 1 << k
        bsum.append(bsum[k] + back(bsum[k], s))
        fsum.append(fsum[k] + fwd(fsum[k], s))
    lane = lax.broadcasted_iota(i32, (1, POOL_W), 1)
    grp = lane // POOL_GROUP
    win = bsum[3] + fsum[3]
    half = jnp.full((1, POOL_W), POOL_WINDOWS[3] // 2, i32)
    for g in (2, 1, 0):
        win = jnp.where(grp == g, bsum[g] + fsum[g], win)
        half = jnp.where(grp == g, POOL_WINDOWS[g] // 2, half)
    cnt = (jnp.minimum(pos + half, n) - jnp.maximum(pos - half, 0)).astype(f32)
    y = (win / cnt - ext)[hl:hl + POOL_TB]
    o_ref[...] = (_dot(y.astype(bf16), wp_ref[...]) * scale_ref[...]).astype(bf16)


def _pool(p, l, w):
    nt = T_ALL // POOL_TB
    blk = lambda f: pl.BlockSpec((POOL_TB, POOL_W), lambda i: (f(i), 0))
    return pl.pallas_call(
        _pool_kernel,
        grid=(nt,),
        in_specs=[
            blk(lambda i: jnp.maximum(i - 1, 0)), blk(lambda i: i),
            blk(lambda i: jnp.minimum(i + 1, nt - 1)),
            pl.BlockSpec((None, POOL_W, POOL_W), lambda i: (l, 0, 0)),
            pl.BlockSpec((None, 1, POOL_W), lambda i: (l, 0, 0)),
        ],
        out_specs=blk(lambda i: i),
        out_shape=jax.ShapeDtypeStruct((T_ALL, POOL_W), bf16),
        compiler_params=_cparams(("arbitrary",)),
        name="pool",
    )(p, p, p, w["w_pool_bd"], w["pool_scale"])


def _seqdft_kernel(*refs, n, nseq):
    m_ref, pq_refs, w_ref, o_ref = refs[0], refs[1:-2], refs[-2], refs[-1]
    per_blk = SEG // n
    for b in range(nseq):
        @pl.when(pl.program_id(1) == b)
        def _(b=b):
            pq_ref = pq_refs[b // per_blk]
            r0 = (b % per_blk) * n
            f = (_dot(m_ref[:, :n], pq_ref[r0:r0 + n, :FFT_W])
                 + _dot(m_ref[:, n:], pq_ref[r0:r0 + n, FFT_W:]))
            o_ref[...] = _dot(f.astype(bf16), w_ref[...]).astype(bf16)


def _seqdft(pq, m, l, w, *, n, tr, nseq, row0):
    nr = n // tr
    nblk = nseq * n // SEG
    pq_specs = [pl.BlockSpec((SEG, 2 * FFT_W), lambda i, b, j=j: (row0 // SEG + j, 0))
                for j in range(nblk)]
    return pl.pallas_call(
        functools.partial(_seqdft_kernel, n=n, nseq=nseq),
        grid=(nr, nseq),
        in_specs=[pl.BlockSpec((tr, 2 * n), lambda i, b: (i, 0))] + pq_specs
        + [pl.BlockSpec((None, FFT_W, FFT_W), lambda i, b: (l, 0, 0))],
        out_specs=pl.BlockSpec((tr, FFT_W), lambda i, b: (b * nr + i, 0)),
        out_shape=jax.ShapeDtypeStruct((nseq * n, FFT_W), bf16),
        compiler_params=_cparams(("arbitrary", "arbitrary")),
        name="seqdft_%d" % n,
    )(m, *([pq] * nblk), w["w_fft"])


def _attn_kernel(*refs, has_cache):
    if has_cache:
        q_ref, k_ref, v_ref, kc_ref, vc_ref, o_ref = refs
    else:
        q_ref, k_ref, v_ref, o_ref = refs
    q = q_ref[...]
    tq = q.shape[0]
    lane = lax.broadcasted_iota(i32, q.shape, 1)
    zero = jnp.zeros_like(q)
    qs = jnp.concatenate([jnp.where(lane < HEAD_DIM, q, zero),
                          jnp.where(lane >= HEAD_DIM, q, zero)], axis=0)
    nt = (((1,), (1,)), ((), ()))
    n = k_ref.shape[0]
    chunk = min(n, ATT_CHUNK)
    parts = [(k_ref, v_ref, c * chunk, chunk) for c in range(n // chunk)]
    if has_cache:
        parts = [(kc_ref, vc_ref, 0, PAST_LEN)] + parts
    m = jnp.full((2 * tq, 1), -jnp.inf, f32)
    den = jnp.zeros((2 * tq, 1), f32)
    acc = jnp.zeros((2 * tq, 2 * HEAD_DIM), f32)
    for kr, vr, off, size in parts:
        s = lax.dot_general(qs, kr[off:off + size, :], nt, preferred_element_type=f32)
        m_new = jnp.maximum(m, jnp.max(s, axis=-1, keepdims=True))
        alpha = jnp.exp2(m - m_new)
        p = jnp.exp2(s - m_new).astype(bf16)
        den = alpha * den + jnp.sum(p.astype(f32), axis=-1, keepdims=True)
        acc = alpha * acc + _dot(p, vr[off:off + size, :])
        m = m_new
    out = acc / den
    o_ref[...] = jnp.where(lane < HEAD_DIM, out[:tq], out[tq:]).astype(bf16)


def _attention(q, kd, vd, cache, *, n, tq, nseq, row0):
    nq = n // tq
    b0 = row0 // n
    q0 = row0 // tq
    in_specs = [
        pl.BlockSpec((tq, 2 * HEAD_DIM), lambda b, h, i: (q0 + b * nq + i, h)),
        pl.BlockSpec((None, n, KV_W), lambda b, h, i: (h, b0 + b, 0)),
        pl.BlockSpec((None, n, KV_W), lambda b, h, i: (h, b0 + b, 0)),
    ]
    args = [q, kd, vd]
    if cache is not None:
        cspec = pl.BlockSpec((None, None, PAST_LEN, KV_W), lambda b, h, i: (h, b, 0, 0))
        in_specs += [cspec, cspec]
        args += list(cache)
    return pl.pallas_call(
        functools.partial(_attn_kernel, has_cache=cache is not None),
        grid=(nseq, N_KV_HEADS, nq),
        in_specs=in_specs,
        out_specs=pl.BlockSpec((tq, 2 * HEAD_DIM), lambda b, h, i: (b * nq + i, h)),
        out_shape=jax.ShapeDtypeStruct((nseq * n, ATTN_W), bf16),
        compiler_params=_cparams(("arbitrary", "arbitrary", "arbitrary")),
        name="attention_%d" % n,
    )(*args)


def _layer_norm(x, g, b):
    mu = jnp.mean(x, axis=-1, keepdims=True)
    xc = x - mu
    var = jnp.mean(xc * xc, axis=-1, keepdims=True)
    return xc * lax.rsqrt(var + LN_EPS) * g + b


def _outproj_kernel(xc_ref, xl_ref, mod_ref, fc_ref, fl_ref, p_ref, s_ref, ac_ref, al_ref, wout_ref,
                    g_ref, b_ref, wr_ref, br_ref, tril_ref,
                    x1_ref, h2_ref, route_ref, cnt_ref, tab_ref, carry_ref):
    i = pl.program_id(0)

    @pl.when(i == 0)
    def _():
        carry_ref[...] = jnp.zeros_like(carry_ref)

    mod = mod_ref[...]
    is_ctx = i < T_CTX // TM
    f_mix = jnp.where(is_ctx, fc_ref[...], fl_ref[...])
    a_mix = jnp.where(is_ctx, ac_ref[...], al_ref[...])
    mix = _dot(jnp.concatenate([f_mix, p_ref[...], s_ref[...], a_mix], axis=1), wout_ref[...])
    x = jnp.where(is_ctx, xc_ref[...], xl_ref[...])
    x1 = _layer_norm(DEEPNORM_ALPHA * x + mod[2:3] * mix, g_ref[...], b_ref[...])
    x1_ref[...] = x1
    h2 = x1 * (1.0 + mod[4:5]) + mod[3:4]
    h2_ref[...] = h2

    h_hi, h_lo = _split_hi_lo(h2)
    hw = _dot(h_hi, wr_ref[...])
    logits = hw[:, :128] + hw[:, 128:] + _dot(h_lo, wr_ref[:, :128]) + br_ref[...]
    lane = lax.broadcasted_iota(i32, logits.shape, 1).astype(f32)
    neg = jnp.float32(-jnp.inf)
    big = jnp.float32(1 << 20)
    gl = jnp.where(lane < N_GROUPS, logits, neg)
    gmax = jnp.max(gl, axis=-1, keepdims=True)
    gsel = jnp.min(jnp.where(gl == gmax, lane, big), axis=-1, keepdims=True)
    pg = 1.0 / jnp.sum(jnp.exp(gl - gmax), axis=-1, keepdims=True)
    e_lo = ROUTE_E0 + gsel * EXPERTS_PER_GROUP
    el = jnp.where((lane >= e_lo) & (lane < e_lo + EXPERTS_PER_GROUP), logits, neg)
    v1 = jnp.max(el, axis=-1, keepdims=True)
    i1 = jnp.min(jnp.where(el == v1, lane, big), axis=-1, keepdims=True)
    el2 = jnp.where(lane == i1, neg, el)
    v2 = jnp.max(el2, axis=-1, keepdims=True)
    i2 = jnp.min(jnp.where(el2 == v2, lane, big), axis=-1, keepdims=True)
    e2 = jnp.exp(v2 - v1)
    w1 = pg / (1.0 + e2)
    w2 = pg * e2 / (1.0 + e2)
    oh1 = lane == i1
    oh2 = lane == i2
    oh = jnp.where(oh1 | oh2, 1.0, 0.0)
    lrank = _dot(tril_ref[...], oh.astype(bf16))
    seg = jnp.floor((jnp.sum(oh, axis=0, keepdims=True) + (ROW_CHUNK - 1.0)) * (1.0 / ROW_CHUNK)) * ROW_CHUNK
    seg8 = jnp.broadcast_to(seg, (8, 128))
    lane8 = lax.broadcasted_iota(i32, (8, 128), 1)
    off8 = seg8
    for sh in (1, 2, 4, 8, 16):
        off8 = off8 + jnp.where(lane8 >= sh, pltpu.roll(off8, sh, 1), 0.0)
    off8 = off8 - seg8
    carry = carry_ref[...]
    lpos = lrank + off8[0:1, :]
    pick = lambda sel, val: jnp.sum(jnp.where(sel, val, 0.0), axis=-1, keepdims=True)
    sub8 = lax.broadcasted_iota(i32, (8, 128), 0)
    tab_ref[...] = jnp.where(sub8 == 0, seg8, jnp.where(sub8 == 1, off8, jnp.where(sub8 == 2, carry, 0.0)))
    carry = carry + seg8
    carry_ref[...] = carry
    cnt_ref[...] = carry
    cols = (i1 - ROUTE_E0, i2 - ROUTE_E0, w1, w2, pick(oh1, lpos), pick(oh2, lpos))
    route = jnp.zeros_like(logits)
    for j, col in enumerate(cols):
        route = jnp.where(lane == j, col, route)
    route_ref[...] = route


def _outproj(x_ctx, x_lat, mod, fo_ctx, fo_lat, po, so, ao_ctx, ao_lat, l, w):
    nt = T_ALL // TM
    tile = lambda wd: pl.BlockSpec((TM, wd), lambda i: (i, 0))
    vec = lambda wd: pl.BlockSpec((None, 1, wd), lambda i: (l, 0, 0))
    return pl.pallas_call(
        _outproj_kernel,
        grid=(nt,),
        in_specs=[
            _ctx_tile(D_MODEL), _lat_tile(D_MODEL),
            pl.BlockSpec((None, None, 6, D_MODEL), lambda i: (l, i // (SEG // TM), 0, 0)),
            _ctx_tile(FFT_W), _lat_tile(FFT_W), tile(POOL_W), tile(SGU_W),
            _ctx_tile(ATTN_W), _lat_tile(ATTN_W),
            pl.BlockSpec((None, D_MODEL, D_MODEL), lambda i: (l, 0, 0)),
            vec(D_MODEL), vec(D_MODEL),
            pl.BlockSpec((None, D_MODEL, 256), lambda i: (l, 0, 0)),
            vec(128),
            pl.BlockSpec((TM, TM), lambda i: (0, 0)),
        ],
        out_specs=[tile(D_MODEL), tile(D_MODEL), tile(128), pl.BlockSpec((8, 128), lambda i: (0, 0)),
                   pl.BlockSpec((None, 8, 128), lambda i: (i, 0, 0))],
        out_shape=[
            jax.ShapeDtypeStruct((T_ALL, D_MODEL), f32),
            jax.ShapeDtypeStruct((T_ALL, D_MODEL), f32),
            jax.ShapeDtypeStruct((T_ALL, 128), f32),
            jax.ShapeDtypeStruct((8, 128), f32),
            jax.ShapeDtypeStruct((nt, 8, 128), f32),
        ],
        scratch_shapes=[pltpu.VMEM((8, 128), f32)],
        compiler_params=_cparams(("arbitrary",)),
        name="outproj",
    )(x_ctx, x_lat, mod, fo_ctx, fo_lat, po, so, ao_ctx, ao_lat,
      w["w_out"], w["ln1_g"], w["ln1_b"], w["w_r"], w["b_r"],
      w["tril"])


def _plan_kernel(cnt_ref, meta_ref):
    lane = lax.broadcasted_iota(i32, (8, 128), 1)
    sub = lax.broadcasted_iota(i32, (8, 128), 0)
    cnt = cnt_ref[...]
    is_e = (lane >= ROUTE_E0) & (lane < ROUTE_E0 + N_EXPERTS)
    tiles = jnp.where(is_e, jnp.floor((cnt + (MOE_TM - 1.0)) * (1.0 / MOE_TM)), 0.0)
    cum = tiles
    for s in (1, 2, 4, 8, 16):
        cum = cum + jnp.where(lane >= s, pltpu.roll(cum, s, 1), 0.0)
    pstart = (cum - tiles) * MOE_TM
    nused = jnp.max(cum, axis=-1, keepdims=True)
    fill = jnp.where(is_e & (cnt != tiles * MOE_TM), pstart + (tiles - 1.0) * MOE_TM, -1.0)
    meta = jnp.where(sub == 0, cnt, jnp.where(sub == 1, nused, jnp.where(sub == 2, fill,
                     jnp.where(sub == 3, pstart, 0.0))))
    meta_ref[...] = meta.astype(i32)


def _plan(cnt):
    return pl.pallas_call(
        _plan_kernel,
        grid=(1,),
        in_specs=[pl.BlockSpec((8, 128), lambda i: (0, 0))],
        out_specs=pl.BlockSpec((8, 128), lambda i: (0, 0)),
        out_shape=jax.ShapeDtypeStruct((8, 128), i32),
        compiler_params=_cparams(("arbitrary",)),
        name="plan",
    )(cnt)


def _dispatch_kernel(nch_ref, off_ref, dst_ref, tot_ref, fill_ref, nused_ref, h_ref, route_ref, xs_ref,
                     sorted_ref, zero_ref, sem, fill_sem):
    i = pl.program_id(0)

    def tile_fill(row0):
        return pltpu.make_async_copy(zero_ref, xs_ref.at[pl.ds(pl.multiple_of(row0, MOE_TM), MOE_TM)],
                                     fill_sem)

    @pl.when(i == 0)
    def _():
        zero_ref[...] = jnp.zeros_like(zero_ref)

        def start(e, c):
            @pl.when(fill_ref[e] >= 0)
            def _():
                tile_fill(jnp.maximum(fill_ref[e], 0)).start()
            return c

        def wait(e, c):
            @pl.when(fill_ref[e] >= 0)
            def _():
                tile_fill(jnp.maximum(fill_ref[e], 0)).wait()
            return c

        def start_tail(t, c):
            tile_fill(t * MOE_TM).start()
            return c

        def wait_tail(t, c):
            tile_fill(t * MOE_TM).wait()
            return c

        lax.fori_loop(0, N_EXPERTS, start, 0)
        lax.fori_loop(nused_ref[0], MOE_NT, start_tail, 0)
        lax.fori_loop(0, N_EXPERTS, wait, 0)
        lax.fori_loop(nused_ref[0], MOE_NT, wait_tail, 0)

    rt = route_ref[...].T
    j = lax.broadcasted_iota(i32, (DISP_ROWS, 1), 0).astype(f32)
    sel = jnp.where((j == rt[4:5, :]) | (j == rt[5:6, :]), 1.0, 0.0).astype(bf16)
    slot = i % 2
    sorted_ref[slot] = _dot(sel, h_ref[...].astype(bf16)).astype(bf16)

    def per_expert(e, c):
        idx = i * N_EXPERTS + e
        n, s0, d0 = nch_ref[idx], off_ref[idx], dst_ref[idx]
        b = 1
        while b <= TM // ROW_CHUNK:
            @pl.when((n & b) != 0)
            def _(b=b):
                r0 = (n & (b - 1)) * ROW_CHUNK
                rows = b * ROW_CHUNK
                pltpu.make_async_copy(
                    sorted_ref.at[slot, pl.ds(pl.multiple_of(s0 + r0, ROW_CHUNK), rows)],
                    xs_ref.at[pl.ds(pl.multiple_of(d0 + r0, ROW_CHUNK), rows)], sem.at[slot]).start()
            b *= 2
        return c

    lax.fori_loop(0, N_EXPERTS, per_expert, 0)

    def drain(tile, s):
        rows = tot_ref[tile] * ROW_CHUNK

        @pl.when(rows > 0)
        def _():
            pltpu.make_async_copy(sorted_ref.at[s, pl.ds(0, rows)], xs_ref.at[pl.ds(0, rows)],
                                  sem.at[s]).wait()

    @pl.when(i >= 1)
    def _():
        drain(i - 1, 1 - slot)

    @pl.when(i == pl.num_programs(0) - 1)
    def _():
        drain(i, slot)


def _dispatch(nch, off, dst, tot, fill, nused, h2, route):
    grid_spec = pltpu.PrefetchScalarGridSpec(
        num_scalar_prefetch=6,
        grid=(T_ALL // TM,),
        in_specs=[pl.BlockSpec((TM, D_MODEL), lambda i, *_: (i, 0)),
                  pl.BlockSpec((TM, 128), lambda i, *_: (i, 0))],
        out_specs=pl.BlockSpec(memory_space=pl.ANY),
        scratch_shapes=[pltpu.VMEM((2, DISP_ROWS, D_MODEL), bf16), pltpu.VMEM((MOE_TM, D_MODEL), bf16),
                        pltpu.SemaphoreType.DMA((2,)), pltpu.SemaphoreType.DMA(())],
    )
    return pl.pallas_call(
        _dispatch_kernel,
        grid_spec=grid_spec,
        out_shape=jax.ShapeDtypeStruct((MOE_NT * MOE_TM, D_MODEL), bf16),
        compiler_params=_cparams(("arbitrary",)),
        name="dispatch",
    )(nch, off, dst, tot, fill, nused, h2, route)


def _experts_kernel(cnt_ref, nused_ref, xs_ref, wg_hbm, wu_hbm, wd_hbm, ys_ref,
                    wg_f, wu_f, wd_f, wg_b, wu_b, wd_b, st, wsem, *, layer):
    i = pl.program_id(0)
    nused = nused_ref[0]
    NXT, NSLOT, LEFT = 0, 1, 2

    def w_copies(e, slot):
        return (pltpu.make_async_copy(wg_hbm.at[layer, e], wg_f.at[slot], wsem.at[slot, 0]),
                pltpu.make_async_copy(wu_hbm.at[layer, e], wu_f.at[slot], wsem.at[slot, 1]),
                pltpu.make_async_copy(wd_hbm.at[layer, e], wd_f.at[slot], wsem.at[slot, 2]))

    def next_nonempty(e):
        return lax.while_loop(
            lambda v: (v < N_EXPERTS) & (cnt_ref[jnp.minimum(v, N_EXPERTS - 1)] == 0),
            lambda v: v + 1, e)

    @pl.when(i == 0)
    def _():
        e0 = next_nonempty(jnp.int32(0))
        for c in w_copies(e0, 0):
            c.start()
        st[NXT] = e0
        st[NSLOT] = 0
        st[LEFT] = 0

    @pl.when(i < nused)
    def _():
        @pl.when(st[LEFT] == 0)
        def _():
            e = st[NXT]
            slot = st[NSLOT]
            for c in w_copies(e, slot):
                c.wait()
            e2 = next_nonempty(e + 1)

            @pl.when(e2 < N_EXPERTS)
            def _():
                for c in w_copies(e2, 1 - slot):
                    c.start()

            st[NXT] = e2
            st[NSLOT] = 1 - slot
            st[LEFT] = (cnt_ref[e] + (MOE_TM - 1)) // MOE_TM
            wg_b[...] = wg_f[slot].astype(bf16)
            wu_b[...] = wu_f[slot].astype(bf16)
            wd_b[...] = wd_f[slot].astype(bf16)

        x = xs_ref[...]
        hg = _dot(x, wg_b[...])
        hu = _dot(x, wu_b[...])
        act = (hg * jax.nn.sigmoid(hg)) * hu
        ys_ref[...] = _dot(act.astype(bf16), wd_b[...]).astype(bf16)
        st[LEFT] = st[LEFT] - 1

    @pl.when(i >= nused)
    def _():
        ys_ref[...] = jnp.zeros_like(ys_ref)


def _experts(counts, nused, xs, l, w_gate, w_up, w_down):
    hbm = pl.BlockSpec(memory_space=pl.ANY)
    grid_spec = pltpu.PrefetchScalarGridSpec(
        num_scalar_prefetch=2,
        grid=(MOE_NT,),
        in_specs=[pl.BlockSpec((MOE_TM, D_MODEL), lambda i, c, nu: (jnp.minimum(i, nu[0] - 1), 0)),
                  hbm, hbm, hbm],
        out_specs=pl.BlockSpec((MOE_TM, D_MODEL), lambda i, c, nu: (i, 0)),
        scratch_shapes=[
            pltpu.VMEM((2, D_MODEL, EXPERT_FF), f32),
            pltpu.VMEM((2, D_MODEL, EXPERT_FF), f32),
            pltpu.VMEM((2, EXPERT_FF, D_MODEL), f32),
            pltpu.VMEM((D_MODEL, EXPERT_FF), bf16),
            pltpu.VMEM((D_MODEL, EXPERT_FF), bf16),
            pltpu.VMEM((EXPERT_FF, D_MODEL), bf16),
            pltpu.SMEM((4,), i32),
            pltpu.SemaphoreType.DMA((2, 3)),
        ],
    )
    return pl.pallas_call(
        functools.partial(_experts_kernel, layer=l),
        grid_spec=grid_spec,
        out_shape=jax.ShapeDtypeStruct((MOE_NT * MOE_TM, D_MODEL), bf16),
        compiler_params=_cparams(("arbitrary",)),
        name="experts",
    )(counts, nused, xs, w_gate, w_up, w_down)


def _combine_kernel(nch_ref, off_ref, dst_ref, tot_ref, x1_ref, mod_ref, route_ref, ys_hbm, g_ref, b_ref,
                    oc_ref, ol_ref, ybuf, sem):
    i = pl.program_id(0)
    nt = pl.num_programs(0) - 1

    @pl.when(i == 0)
    def _():
        ybuf[...] = jnp.zeros_like(ybuf)

    @pl.when(i < nt)
    def _():
        slot = i % 2

        def per_expert(e, c):
            idx = i * N_EXPERTS + e
            n, s0, d0 = nch_ref[idx], off_ref[idx], dst_ref[idx]
            b = 1
            while b <= TM // ROW_CHUNK:
                @pl.when((n & b) != 0)
                def _(b=b):
                    r0 = (n & (b - 1)) * ROW_CHUNK
                    rows = b * ROW_CHUNK
                    pltpu.make_async_copy(
                        ys_hbm.at[pl.ds(pl.multiple_of(d0 + r0, ROW_CHUNK), rows)],
                        ybuf.at[slot, pl.ds(pl.multiple_of(s0 + r0, ROW_CHUNK), rows)],
                        sem.at[slot]).start()
                b *= 2
            return c

        lax.fori_loop(0, N_EXPERTS, per_expert, 0)

    @pl.when(i >= 1)
    def _():
        slot = (i - 1) % 2
        rows = tot_ref[i - 1] * ROW_CHUNK

        @pl.when(rows > 0)
        def _():
            pltpu.make_async_copy(ys_hbm.at[pl.ds(0, rows)], ybuf.at[slot, pl.ds(0, rows)],
                                  sem.at[slot]).wait()

        route = route_ref[...]
        mod = mod_ref[...]
        lane = lax.broadcasted_iota(i32, (1, DISP_ROWS), 1).astype(f32)
        wmat = (jnp.where(lane == route[:, 4:5], route[:, 2:3], 0.0)
                + jnp.where(lane == route[:, 5:6], route[:, 3:4], 0.0))
        moe = _dot(wmat.astype(bf16), ybuf[slot])
        y = _layer_norm(DEEPNORM_ALPHA * x1_ref[...] + mod[5:6] * moe, g_ref[...], b_ref[...])

        @pl.when(i - 1 < T_CTX // TM)
        def _():
            oc_ref[...] = y

        @pl.when(i - 1 >= T_CTX // TM)
        def _():
            ol_ref[...] = y


def _combine(nch, off, dst, tot, x1, mod, route, ys, l, w):
    nt = T_ALL // TM
    vec = pl.BlockSpec((None, 1, D_MODEL), lambda i, *_: (l, 0, 0))
    nctx = T_CTX // TM
    prev = lambda i: jnp.maximum(i - 1, 0)
    grid_spec = pltpu.PrefetchScalarGridSpec(
        num_scalar_prefetch=4,
        grid=(nt + 1,),
        in_specs=[
            pl.BlockSpec((TM, D_MODEL), lambda i, *_: (prev(i), 0)),
            pl.BlockSpec((None, None, 6, D_MODEL), lambda i, *_: (l, prev(i) // (SEG // TM), 0, 0)),
            pl.BlockSpec((TM, 128), lambda i, *_: (prev(i), 0)),
            pl.BlockSpec(memory_space=pl.ANY),
            vec, vec,
        ],
        out_specs=[pl.BlockSpec((TM, D_MODEL), lambda i, *_: (jnp.minimum(prev(i), nctx - 1), 0)),
                   pl.BlockSpec((TM, D_MODEL), lambda i, *_: (jnp.maximum(prev(i) - nctx, 0), 0))],
        scratch_shapes=[pltpu.VMEM((2, DISP_ROWS, D_MODEL), bf16), pltpu.SemaphoreType.DMA((2,))],
    )
    return pl.pallas_call(
        _combine_kernel,
        grid_spec=grid_spec,
        out_shape=[jax.ShapeDtypeStruct((T_CTX, D_MODEL), f32),
                   jax.ShapeDtypeStruct((T_LAT, D_MODEL), f32)],
        compiler_params=_cparams(("arbitrary",)),
        name="combine",
    )(nch, off, dst, tot, x1, mod, route, ys, w["ln2_g"], w["ln2_b"])


def _dft_cos_sin(n, scale):
    k = jnp.arange(n, dtype=i32)
    ang = ((k[:, None] * k[None, :]) % n).astype(f32) * np.float32(2.0 * np.pi / n)
    return jnp.cos(ang) * scale, jnp.sin(ang) * scale


def _seq_dft_matrix(n):
    g = min(DFT_SPLIT, n)
    j = jnp.arange(n, dtype=i32)[None, :]
    k1 = jnp.arange(n // g, dtype=i32)[:, None]
    k2 = jnp.arange(g, dtype=i32)[:, None]
    ang_a = ((k1 * j) % (n // g)).astype(f32) * np.float32(2.0 * np.pi * g / n)
    ang_b = ((k2 * j) % n).astype(f32) * np.float32(2.0 * np.pi / n)
    scale = np.float32(n ** -0.5)
    ca, sa = jnp.cos(ang_a), jnp.sin(ang_a)
    cb, sb = jnp.cos(ang_b) * scale, jnp.sin(ang_b) * scale
    ca2 = jnp.concatenate([ca, ca], axis=1)[:, None, :]
    sa2 = jnp.concatenate([sa, sa], axis=1)[:, None, :]
    cb2 = jnp.concatenate([cb, -sb], axis=1)[None, :, :]
    sb2 = jnp.concatenate([sb, cb], axis=1)[None, :, :]
    return (ca2 * cb2 - sa2 * sb2).astype(bf16).reshape(n, 2 * n)


def _rope_tables():
    rows = DEC_SEQ // GRID_W
    row = jnp.repeat(jnp.arange(rows), GRID_W).astype(f32)
    col = jnp.tile(jnp.arange(GRID_W), rows).astype(f32)
    n_freq = HEAD_DIM // 4
    inv = ROPE_THETA ** (-jnp.arange(n_freq, dtype=f32) / n_freq)
    ar = row[:, None] * inv
    ac = col[:, None] * inv
    ang = jnp.concatenate([ar, ar, ac, ac], axis=-1)
    cos = jnp.tile(jnp.cos(ang), (1, N_HEADS))
    sin = jnp.tile(jnp.sin(ang), (1, N_HEADS))
    first = (jnp.arange(ATTN_W) % (HEAD_DIM // 2)) < n_freq
    sin_a = jnp.where(first[None, :], -sin, 0.0)
    sin_b = jnp.where(first[None, :], 0.0, sin)
    ident = jnp.zeros((TM, ATTN_W), f32)
    return (jnp.concatenate([cos, ident + 1.0], axis=0),
            jnp.concatenate([sin_a, ident], axis=0),
            jnp.concatenate([sin_b, ident], axis=0))


def _dup_cache(cache):
    c = jnp.transpose(cache, (1, 3, 0, 2, 4))
    return jnp.concatenate([c, c], axis=-1).astype(bf16)


def kernel(x_prompt, x_sample, cache_k, cache_v, c, c_ctx, w_mod, b_mod, w_in, w_fft, w_pool, pool_scale, sgu_ln_g, sgu_ln_b, w_sgu, b_sgu, q_norm_g, k_norm_g, w_out, ln1_g, ln1_b, w_router_group, b_router_group, w_router_expert, b_router_expert, w_gate, w_up, w_down, ln2_g, ln2_b):
    L = DEPTH
    x_ctx = x_prompt.reshape(T_CTX, D_MODEL)
    x_lat = x_sample.reshape(T_LAT, D_MODEL)

    cond8 = jnp.concatenate([c_ctx[None, :], c, jnp.zeros((8 - 1 - DEC_BATCH, D_MODEL), f32)], axis=0)
    mod = _modulation(cond8, w_mod, b_mod)[:, :N_SEG].reshape(L, N_SEG, 6, D_MODEL)

    cc, sc = _dft_cos_sin(FFT_W, np.float32(FFT_W ** -0.5))
    rope_cos, rope_sin_a, rope_sin_b = _rope_tables()
    head_id = jnp.arange(ATTN_W) // HEAD_DIM
    eye_g = jnp.eye(len(POOL_WINDOWS), dtype=f32)
    w_r = jnp.zeros((L, D_MODEL, 128), f32)
    w_r = w_r.at[:, :, :N_GROUPS].set(w_router_group).at[:, :, ROUTE_E0:ROUTE_E0 + N_EXPERTS].set(w_router_expert)
    b_r = jnp.zeros((L, 1, 128), f32)
    b_r = b_r.at[:, 0, :N_GROUPS].set(b_router_group).at[:, 0, ROUTE_E0:ROUTE_E0 + N_EXPERTS].set(b_router_expert)
    w_r_hi, w_r_lo = _split_hi_lo(w_r)
    w = {
        "w_in": w_in.astype(bf16),
        "csc": jnp.concatenate([cc, sc], axis=1).astype(bf16),
        "w_sgu": jnp.transpose(w_sgu, (0, 2, 1, 3)).reshape(L, CHUNK, SGU_HEADS * CHUNK).astype(bf16),
        "b_sgu": jnp.repeat(jnp.transpose(b_sgu, (0, 2, 1)), SGU_W // SGU_HEADS, axis=2),
        "sgu_ln_g": sgu_ln_g.reshape(L, 1, SGU_W),
        "sgu_ln_b": sgu_ln_b.reshape(L, 1, SGU_W),
        "q_norm_g": jnp.tile(q_norm_g, (1, N_HEADS)).reshape(L, 1, ATTN_W),
        "k_norm_g": jnp.tile(k_norm_g, (1, N_KV_HEADS)).reshape(L, 1, KV_W),
        "rope_cos": rope_cos, "rope_sin_a": rope_sin_a, "rope_sin_b": rope_sin_b,
        "ones_bd": (head_id[:, None] == head_id[None, :]).astype(bf16),
        "w_pool_bd": jnp.einsum("lgcd,gh->lgchd", w_pool, eye_g).reshape(L, POOL_W, POOL_W).astype(bf16),
        "pool_scale": pool_scale.reshape(L, 1, POOL_W),
        "w_fft": w_fft.astype(bf16),
        "w_out": w_out.astype(bf16),
        "ln1_g": ln1_g.reshape(L, 1, D_MODEL), "ln1_b": ln1_b.reshape(L, 1, D_MODEL),
        "ln2_g": ln2_g.reshape(L, 1, D_MODEL), "ln2_b": ln2_b.reshape(L, 1, D_MODEL),
        "w_r": jnp.concatenate([w_r_hi, w_r_lo], axis=-1), "b_r": b_r,
        "tril": (jnp.arange(TM)[:, None] > jnp.arange(TM)[None, :]).astype(bf16),
    }
    m_ctx = _seq_dft_matrix(SEQ)
    m_lat = _seq_dft_matrix(DEC_SEQ)
    kc_all = _dup_cache(cache_k)
    vc_all = _dup_cache(cache_v)

    new_k, new_v = [], []
    for l in range(L):
        pq, praw, sgu, q, kd, vd, nk, nv = _inproj(x_ctx, x_lat, mod, l, w)
        new_k.append(nk[:T_CTX].reshape(BATCH, SEQ, N_KV_HEADS, HEAD_DIM))
        new_v.append(nv[:T_CTX].reshape(BATCH, SEQ, N_KV_HEADS, HEAD_DIM))
        po = _pool(praw, l, w)
        fo_ctx = _seqdft(pq, m_ctx, l, w, n=SEQ, tr=SEQ, nseq=BATCH, row0=0)
        fo_lat = _seqdft(pq, m_lat, l, w, n=DEC_SEQ, tr=FFT_TR, nseq=DEC_BATCH, row0=T_CTX)
        ao_ctx = _attention(q, kd, vd, None, n=SEQ, tq=SEQ, nseq=BATCH, row0=0)
        ao_lat = _attention(q, kd, vd, (kc_all[l], vc_all[l]), n=DEC_SEQ, tq=ATT_TQ, nseq=DEC_BATCH,
                            row0=T_CTX)
        x1, h2, route, cnt, tab = _outproj(x_ctx, x_lat, mod, fo_ctx, fo_lat, po, sgu, ao_ctx, ao_lat, l, w)
        meta = _plan(cnt)
        experts = slice(ROUTE_E0, ROUTE_E0 + N_EXPERTS)
        counts = meta[0, experts]
        nused = meta[1, :1]
        fill = meta[2, experts]
        tab = tab[:, :, experts].astype(i32)
        nch = (tab[:, 0] // ROW_CHUNK).reshape(-1)
        off = tab[:, 1].reshape(-1)
        dst = (meta[3, experts][None, :] + tab[:, 2]).reshape(-1)
        tot = jnp.sum(tab[:, 0], axis=1) // ROW_CHUNK
        xs = _dispatch(nch, off, dst, tot, fill, nused, h2, route)
        ys = _experts(counts, nused, xs, l, w_gate, w_up, w_down)
        x_ctx, x_lat = _combine(nch, off, dst, tot, x1, mod, route, ys, l, w)

    y_prompt = x_ctx.reshape(BATCH, SEQ, D_MODEL)
    y_sample = x_lat.reshape(DEC_BATCH, DEC_SEQ, D_MODEL)
    return (y_prompt, y_sample, jnp.stack(new_k, axis=1), jnp.stack(new_v, axis=1))
```

```python
import functools

import numpy as np
import jax
import jax.numpy as jnp
from jax import lax
from jax.experimental import pallas as pl
from jax.experimental.pallas import tpu as pltpu

f32 = jnp.float32
bf16 = jnp.bfloat16
i32 = jnp.int32

D_MODEL = 1024
BATCH = 16
SEQ = 256
DEPTH = 4
DEC_BATCH = 2
DEC_SEQ = 4096
PAST_LEN = 512
GRID_W = 64
FFT_W = 256
POOL_W = 256
POOL_WINDOWS = (2, 4, 8, 16)
POOL_GROUP = 64
SGU_W = 256
SGU_HEADS = 4
CHUNK = 128
HEAD_DIM = 64
ATTN_W = 256
N_HEADS = 4
N_KV_HEADS = 2
KV_W = 128
IN_W = 1536
ROPE_THETA = 10000.0
N_GROUPS = 4
EXPERTS_PER_GROUP = 8
N_EXPERTS = 32
EXPERT_FF = 512
DEEPNORM_ALPHA = float((2 * DEPTH) ** 0.25)
LN_EPS = 1e-5
RMS_EPS = 1e-6

T_CTX = BATCH * SEQ
T_LAT = DEC_BATCH * DEC_SEQ
T_ALL = T_CTX + T_LAT
SEG = 4096
N_SEG = T_ALL // SEG

TM = 512
POOL_TB = 512
POOL_HALO = 8
FFT_TR = 512
ATT_TQ = 512
ATT_CHUNK = 1024
DFT_SPLIT = 64
MOE_TM = 256
ROW_CHUNK = 16
MOE_ROWS = 2 * T_ALL
MOE_PAD_ROWS = (T_ALL // TM) * N_EXPERTS * (ROW_CHUNK - 1)
MOE_NT = -(-(MOE_ROWS + MOE_PAD_ROWS) // MOE_TM) + N_EXPERTS
DISP_ROWS = 2 * TM + N_EXPERTS * ROW_CHUNK
ROUTE_E0 = 32
VMEM_LIMIT = 56 * 1024 * 1024


def _cparams(sem):
    return pltpu.CompilerParams(dimension_semantics=sem, vmem_limit_bytes=VMEM_LIMIT)


def _split_hi_lo(a):
    hi = a.astype(bf16)
    lo = (a - hi.astype(f32)).astype(bf16)
    return hi, lo


def _dot(a, b):
    return jnp.dot(a, b, preferred_element_type=f32)


def _mod_kernel(c_ref, w_ref, b_ref, o_ref):
    c = c_ref[...]
    s = c * jax.nn.sigmoid(c)
    s_hi, s_lo = _split_hi_lo(s)
    w_hi, w_lo = _split_hi_lo(w_ref[...])
    o_ref[...] = _dot(s_hi, w_hi) + _dot(s_hi, w_lo) + _dot(s_lo, w_hi) + b_ref[...]


def _modulation(cond8, w_mod, b_mod):
    tn = 1536
    return pl.pallas_call(
        _mod_kernel,
        grid=(DEPTH, 6 * D_MODEL // tn),
        in_specs=[
            pl.BlockSpec((8, D_MODEL), lambda l, j: (0, 0)),
            pl.BlockSpec((None, D_MODEL, tn), lambda l, j: (l, 0, j)),
            pl.BlockSpec((None, 1, tn), lambda l, j: (l, 0, j)),
        ],
        out_specs=pl.BlockSpec((None, 8, tn), lambda l, j: (l, 0, j)),
        out_shape=jax.ShapeDtypeStruct((DEPTH, 8, 6 * D_MODEL), f32),
        compiler_params=_cparams(("arbitrary", "arbitrary")),
        name="modulation",
    )(cond8, w_mod, b_mod.reshape(DEPTH, 1, 6 * D_MODEL))


def _head_rms(x, ones_bd, gain):
    ss = _dot((x * x).astype(bf16), ones_bd)
    return x * lax.rsqrt(ss * (1.0 / HEAD_DIM) + RMS_EPS) * gain


def _rope(x, cos, sin_a, sin_b):
    w = x.shape[-1]
    q4 = HEAD_DIM // 4
    return x * cos + pltpu.roll(x, w - q4, 1) * sin_a + pltpu.roll(x, q4, 1) * sin_b


def _dup_half(x, first):
    lane = lax.broadcasted_iota(i32, x.shape, 1)
    r = pltpu.roll(x, HEAD_DIM, 1)
    if first:
        return jnp.where(lane < HEAD_DIM, x, r)
    return jnp.where(lane >= HEAD_DIM, x, r)


def _gelu_tanh(x):
    c = np.sqrt(2.0 / np.pi).astype(np.float32)
    return x * (0.5 * (1.0 + jnp.tanh(c * (x + 0.044715 * (x * x * x)))))


def _inproj_kernel(xc_ref, xl_ref, mod_ref, win_ref, csc_ref, wsgu_ref, bsgu_ref, lng_ref, lnb_ref,
                   qg_ref, kg_ref, cos_ref, sina_ref, sinb_ref, ones_ref,
                   pq_ref, pool_ref, sgu_ref, q_ref, kd_ref, vd_ref, nk_ref, nv_ref):
    x = jnp.where(pl.program_id(0) < T_CTX // TM, xc_ref[...], xl_ref[...])
    mod = mod_ref[...]
    h = (x * (1.0 + mod[1:2]) + mod[0:1]).astype(bf16)
    proj = _dot(h, win_ref[...])

    a = proj[:, 0:FFT_W].astype(bf16)
    pq_ref[...] = _dot(a, csc_ref[...]).astype(bf16)

    pool_ref[...] = proj[:, FFT_W:FFT_W + POOL_W]

    o = FFT_W + POOL_W
    hgu = _gelu_tanh(proj[:, o:o + 2 * SGU_W])
    u = hgu[:, :SGU_W]
    v = hgu[:, SGU_W:]
    mu = jnp.mean(v, axis=-1, keepdims=True)
    vc = v - mu
    var = jnp.mean(vc * vc, axis=-1, keepdims=True)
    v = vc * lax.rsqrt(var + LN_EPS) * lng_ref[...] + lnb_ref[...]
    lane = lax.broadcasted_iota(i32, (CHUNK, SGU_W), 1)
    head = lane // (SGU_W // SGU_HEADS)
    wcat = wsgu_ref[...]
    for cidx in range(TM // CHUNK):
        rows = slice(cidx * CHUNK, (cidx + 1) * CHUNK)
        vch = v[rows]
        vblk = jnp.concatenate(
            [jnp.where(head == g, vch, 0.0) for g in range(SGU_HEADS)], axis=0).astype(bf16)
        sp = _dot(wcat, vblk) + bsgu_ref[...]
        sgu_ref[rows, :] = (u[rows] * sp).astype(bf16)

    o = o + 2 * SGU_W
    ones_bd = ones_ref[...]
    cos = cos_ref[...]
    sin_a = sina_ref[...]
    sin_b = sinb_ref[...]
    q = _head_rms(proj[:, o:o + ATTN_W], ones_bd, qg_ref[...])
    q = _rope(q, cos, sin_a, sin_b) * np.float32(HEAD_DIM ** -0.5 * np.log2(np.e))
    q_ref[...] = q.astype(bf16)
    o = o + ATTN_W
    k = _head_rms(proj[:, o:o + KV_W], ones_bd[:KV_W, :KV_W], kg_ref[...])
    nk_ref[...] = k
    k = _rope(k, cos[:, :KV_W], sin_a[:, :KV_W], sin_b[:, :KV_W])
    kd_ref[0] = _dup_half(k, True).astype(bf16)
    kd_ref[1] = _dup_half(k, False).astype(bf16)
    o = o + KV_W
    vv = proj[:, o:o + KV_W]
    nv_ref[...] = vv
    vd_ref[0] = _dup_half(vv, True).astype(bf16)
    vd_ref[1] = _dup_half(vv, False).astype(bf16)


def _rope_block(i):
    nlat = DEC_SEQ // TM
    nctx = T_CTX // TM
    return jnp.where(i < nctx, nlat, (i - nctx) % nlat)


def _ctx_tile(wd):
    return pl.BlockSpec((TM, wd), lambda i, *_: (jnp.minimum(i, T_CTX // TM - 1), 0))


def _lat_tile(wd):
    return pl.BlockSpec((TM, wd), lambda i, *_: (jnp.maximum(i - T_CTX // TM, 0), 0))


def _inproj(x_ctx, x_lat, mod, l, w):
    nt = T_ALL // TM
    tile = lambda wd: pl.BlockSpec((TM, wd), lambda i: (i, 0))
    const = lambda shape: pl.BlockSpec(shape, lambda i: (0,) * len(shape))
    rope_spec = pl.BlockSpec((TM, ATTN_W), lambda i: (_rope_block(i), 0))
    return pl.pallas_call(
        _inproj_kernel,
        grid=(nt,),
        in_specs=[
            _ctx_tile(D_MODEL), _lat_tile(D_MODEL),
            pl.BlockSpec((None, None, 6, D_MODEL), lambda i: (l, i // (SEG // TM), 0, 0)),
            pl.BlockSpec((None, D_MODEL, IN_W), lambda i: (l, 0, 0)),
            const((FFT_W, 2 * FFT_W)),
            pl.BlockSpec((None, CHUNK, SGU_HEADS * CHUNK), lambda i: (l, 0, 0)),
            pl.BlockSpec((None, CHUNK, SGU_W), lambda i: (l, 0, 0)),
            pl.BlockSpec((None, 1, SGU_W), lambda i: (l, 0, 0)),
            pl.BlockSpec((None, 1, SGU_W), lambda i: (l, 0, 0)),
            pl.BlockSpec((None, 1, ATTN_W), lambda i: (l, 0, 0)),
            pl.BlockSpec((None, 1, KV_W), lambda i: (l, 0, 0)),
            rope_spec, rope_spec, rope_spec,
            const((ATTN_W, ATTN_W)),
        ],
        out_specs=[
            tile(2 * FFT_W), tile(POOL_W), tile(SGU_W), tile(ATTN_W),
            pl.BlockSpec((N_KV_HEADS, TM, KV_W), lambda i: (0, i, 0)),
            pl.BlockSpec((N_KV_HEADS, TM, KV_W), lambda i: (0, i, 0)),
            tile(KV_W), tile(KV_W),
        ],
        out_shape=[
            jax.ShapeDtypeStruct((T_ALL, 2 * FFT_W), bf16),
            jax.ShapeDtypeStruct((T_ALL, POOL_W), f32),
            jax.ShapeDtypeStruct((T_ALL, SGU_W), bf16),
            jax.ShapeDtypeStruct((T_ALL, ATTN_W), bf16),
            jax.ShapeDtypeStruct((N_KV_HEADS, T_ALL, KV_W), bf16),
            jax.ShapeDtypeStruct((N_KV_HEADS, T_ALL, KV_W), bf16),
            jax.ShapeDtypeStruct((T_ALL, KV_W), f32),
            jax.ShapeDtypeStruct((T_ALL, KV_W), f32),
        ],
        compiler_params=_cparams(("arbitrary",)),
        name="inproj",
    )(x_ctx, x_lat, mod, w["w_in"], w["csc"], w["w_sgu"], w["b_sgu"], w["sgu_ln_g"], w["sgu_ln_b"],
      w["q_norm_g"], w["k_norm_g"], w["rope_cos"], w["rope_sin_a"], w["rope_sin_b"], w["ones_bd"])


def _pool_kernel(prev_ref, cur_ref, next_ref, wp_ref, scale_ref, o_ref):
    i = pl.program_id(0)
    n = jnp.where(i < T_CTX // POOL_TB, SEQ, DEC_SEQ)
    hl = POOL_HALO
    ext = jnp.concatenate([prev_ref[POOL_TB - hl:, :], cur_ref[...], next_ref[:hl, :]], axis=0)
    rows = POOL_TB + 2 * hl
    r = lax.broadcasted_iota(i32, (rows, 1), 0)
    pos = (i * POOL_TB + r - hl) & (n - 1)

    def back(a, s):
        return jnp.where(pos >= s, pltpu.roll(a, s, 0), 0.0)

    def fwd(a, s):
        return jnp.where(pos + s < n, pltpu.roll(a, rows - s, 0), 0.0)

    bsum = [back(ext, 1)]
    fsum = [ext]
    for k in range(3):
        s = 1 << k
        bsum.append(bsum[k] + back(bsum[k], s))
        fsum.append(fsum[k] + fwd(fsum[k], s))
    lane = lax.broadcasted_iota(i32, (1, POOL_W), 1)
    grp = lane // POOL_GROUP
    win = bsum[3] + fsum[3]
    half = jnp.full((1, POOL_W), POOL_WINDOWS[3] // 2, i32)
    for g in (2, 1, 0):
        win = jnp.where(grp == g, bsum[g] + fsum[g], win)
        half = jnp.where(grp == g, POOL_WINDOWS[g] // 2, half)
    cnt = (jnp.minimum(pos + half, n) - jnp.maximum(pos - half, 0)).astype(f32)
    y = (win / cnt - ext)[hl:hl + POOL_TB]
    o_ref[...] = (_dot(y.astype(bf16), wp_ref[...]) * scale_ref[...]).astype(bf16)


def _pool(p, l, w):
    nt = T_ALL // POOL_TB
    blk = lambda f: pl.BlockSpec((POOL_TB, POOL_W), lambda i: (f(i), 0))
    return pl.pallas_call(
        _pool_kernel,
        grid=(nt,),
        in_specs=[
            blk(lambda i: jnp.maximum(i - 1, 0)), blk(lambda i: i),
            blk(lambda i: jnp.minimum(i + 1, nt - 1)),
            pl.BlockSpec((None, POOL_W, POOL_W), lambda i: (l, 0, 0)),
            pl.BlockSpec((None, 1, POOL_W), lambda i: (l, 0, 0)),
        ],
        out_specs=blk(lambda i: i),
        out_shape=jax.ShapeDtypeStruct((T_ALL, POOL_W), bf16),
        compiler_params=_cparams(("arbitrary",)),
        name="pool",
    )(p, p, p, w["w_pool_bd"], w["pool_scale"])


def _seqdft_kernel(*refs, n, nseq):
    m_ref, pq_refs, w_ref, o_ref = refs[0], refs[1:-2], refs[-2], refs[-1]
    per_blk = SEG // n
    tr = m_ref.shape[0]

    def one_sequence(b, rows):
        pq_ref = pq_refs[b // per_blk]
        r0 = (b % per_blk) * n
        f = (_dot(m_ref[:, :n], pq_ref[r0:r0 + n, :FFT_W])
             + _dot(m_ref[:, n:], pq_ref[r0:r0 + n, FFT_W:]))
        o_ref[rows, :] = _dot(f.astype(bf16), w_ref[...]).astype(bf16)

    if tr == n:
        for b in range(nseq):
            one_sequence(b, slice(b * n, (b + 1) * n))
    else:
        for b in range(nseq):
            @pl.when(pl.program_id(1) == b)
            def _(b=b):
                one_sequence(b, slice(0, tr))


def _seqdft(pq, m, l, w, *, n, tr, nseq, row0):
    nr = n // tr
    nblk = nseq * n // SEG
    pq_specs = [pl.BlockSpec((SEG, 2 * FFT_W), lambda i, b, j=j: (row0 // SEG + j, 0))
                for j in range(nblk)]
    if nr == 1:
        grid, out_spec = (1, 1), pl.BlockSpec((nseq * n, FFT_W), lambda i, b: (0, 0))
    else:
        grid, out_spec = (nr, nseq), pl.BlockSpec((tr, FFT_W), lambda i, b: (b * nr + i, 0))
    return pl.pallas_call(
        functools.partial(_seqdft_kernel, n=n, nseq=nseq),
        grid=grid,
        in_specs=[pl.BlockSpec((tr, 2 * n), lambda i, b: (i, 0))] + pq_specs
        + [pl.BlockSpec((None, FFT_W, FFT_W), lambda i, b: (l, 0, 0))],
        out_specs=out_spec,
        out_shape=jax.ShapeDtypeStruct((nseq * n, FFT_W), bf16),
        compiler_params=_cparams(("arbitrary", "arbitrary")),
        name="seqdft_%d" % n,
    )(m, *([pq] * nblk), w["w_fft"])


def _attn_kernel(*refs, has_cache):
    if has_cache:
        q_ref, k_ref, v_ref, kc_ref, vc_ref, o_ref = refs
    else:
        q_ref, k_ref, v_ref, o_ref = refs
    pair = 2 * HEAD_DIM
    tq = q_ref.shape[0]
    n = k_ref.shape[1]
    nt = (((1,), (1,)), ((), ()))
    chunk = min(n, ATT_CHUNK)
    lane = lax.broadcasted_iota(i32, (tq, pair), 1)
    for h in range(k_ref.shape[0]):
        q = q_ref[:, h * pair:(h + 1) * pair]
        zero = jnp.zeros_like(q)
        qs = jnp.concatenate([jnp.where(lane < HEAD_DIM, q, zero),
                              jnp.where(lane >= HEAD_DIM, q, zero)], axis=0)
        parts = [(k_ref, v_ref, c * chunk, chunk) for c in range(n // chunk)]
        if has_cache:
            parts = [(kc_ref, vc_ref, 0, PAST_LEN)] + parts
        m = jnp.full((2 * tq, 1), -jnp.inf, f32)
        den = jnp.zeros((2 * tq, 1), f32)
        acc = jnp.zeros((2 * tq, pair), f32)
        for kr, vr, off, size in parts:
            s = lax.dot_general(qs, kr[h, off:off + size, :], nt, preferred_element_type=f32)
            m_new = jnp.maximum(m, jnp.max(s, axis=-1, keepdims=True))
            alpha = jnp.exp2(m - m_new)
            p = jnp.exp2(s - m_new).astype(bf16)
            den = alpha * den + jnp.sum(p.astype(f32), axis=-1, keepdims=True)
            acc = alpha * acc + _dot(p, vr[h, off:off + size, :])
            m = m_new
        out = acc / den
        o_ref[:, h * pair:(h + 1) * pair] = jnp.where(lane < HEAD_DIM, out[:tq], out[tq:]).astype(bf16)


def _attention(q, kd, vd, cache, *, n, tq, nseq, row0, heads):
    nq = n // tq
    b0 = row0 // n
    q0 = row0 // tq
    in_specs = [
        pl.BlockSpec((tq, heads * 2 * HEAD_DIM), lambda b, h, i: (q0 + b * nq + i, h)),
        pl.BlockSpec((heads, n, KV_W), lambda b, h, i: (h, b0 + b, 0)),
        pl.BlockSpec((heads, n, KV_W), lambda b, h, i: (h, b0 + b, 0)),
    ]
    args = [q, kd, vd]
    if cache is not None:
        cspec = pl.BlockSpec((heads, None, PAST_LEN, KV_W), lambda b, h, i: (h, b, 0, 0))
        in_specs += [cspec, cspec]
        args += list(cache)
    return pl.pallas_call(
        functools.partial(_attn_kernel, has_cache=cache is not None),
        grid=(nseq, N_KV_HEADS // heads, nq),
        in_specs=in_specs,
        out_specs=pl.BlockSpec((tq, heads * 2 * HEAD_DIM), lambda b, h, i: (b * nq + i, h)),
        out_shape=jax.ShapeDtypeStruct((nseq * n, ATTN_W), bf16),
        compiler_params=_cparams(("arbitrary", "arbitrary", "arbitrary")),
        name="attention_%d" % n,
    )(*args)


def _layer_norm(x, g, b):
    mu = jnp.mean(x, axis=-1, keepdims=True)
    xc = x - mu
    var = jnp.mean(xc * xc, axis=-1, keepdims=True)
    return xc * lax.rsqrt(var + LN_EPS) * g + b


def _outproj_kernel(xc_ref, xl_ref, mod_ref, fc_ref, fl_ref, p_ref, s_ref, ac_ref, al_ref, wout_ref,
                    g_ref, b_ref, wr_ref, br_ref, tril_ref,
                    x1_ref, h2_ref, route_ref, cnt_ref, tab_ref, carry_ref):
    i = pl.program_id(0)

    @pl.when(i == 0)
    def _():
        carry_ref[...] = jnp.zeros_like(carry_ref)

    mod = mod_ref[...]
    is_ctx = i < T_CTX // TM
    f_mix = jnp.where(is_ctx, fc_ref[...], fl_ref[...])
    a_mix = jnp.where(is_ctx, ac_ref[...], al_ref[...])
    mix = _dot(jnp.concatenate([f_mix, p_ref[...], s_ref[...], a_mix], axis=1), wout_ref[...])
    x = jnp.where(is_ctx, xc_ref[...], xl_ref[...])
    x1 = _layer_norm(DEEPNORM_ALPHA * x + mod[2:3] * mix, g_ref[...], b_ref[...])
    x1_ref[...] = x1
    h2 = x1 * (1.0 + mod[4:5]) + mod[3:4]
    h2_ref[...] = h2

    h_hi, h_lo = _split_hi_lo(h2)
    hw = _dot(h_hi, wr_ref[...])
    logits = hw[:, :128] + hw[:, 128:] + _dot(h_lo, wr_ref[:, :128]) + br_ref[...]
    lane = lax.broadcasted_iota(i32, logits.shape, 1).astype(f32)
    neg = jnp.float32(-jnp.inf)
    big = jnp.float32(1 << 20)
    gl = jnp.where(lane < N_GROUPS, logits, neg)
    gmax = jnp.max(gl, axis=-1, keepdims=True)
    gsel = jnp.min(jnp.where(gl == gmax, lane, big), axis=-1, keepdims=True)
    pg = 1.0 / jnp.sum(jnp.exp(gl - gmax), axis=-1, keepdims=True)
    e_lo = ROUTE_E0 + gsel * EXPERTS_PER_GROUP
    el = jnp.where((lane >= e_lo) & (lane < e_lo + EXPERTS_PER_GROUP), logits, neg)
    v1 = jnp.max(el, axis=-1, keepdims=True)
    i1 = jnp.min(jnp.where(el == v1, lane, big), axis=-1, keepdims=True)
    el2 = jnp.where(lane == i1, neg, el)
    v2 = jnp.max(el2, axis=-1, keepdims=True)
    i2 = jnp.min(jnp.where(el2 == v2, lane, big), axis=-1, keepdims=True)
    e2 = jnp.exp(v2 - v1)
    w1 = pg / (1.0 + e2)
    w2 = pg * e2 / (1.0 + e2)
    oh1 = lane == i1
    oh2 = lane == i2
    oh = jnp.where(oh1 | oh2, 1.0, 0.0)
    lrank = _dot(tril_ref[...], oh.astype(bf16))
    seg = jnp.floor((jnp.sum(oh, axis=0, keepdims=True) + (ROW_CHUNK - 1.0)) * (1.0 / ROW_CHUNK)) * ROW_CHUNK
    seg8 = jnp.broadcast_to(seg, (8, 128))
    lane8 = lax.broadcasted_iota(i32, (8, 128), 1)
    off8 = seg8
    for sh in (1, 2, 4, 8, 16):
        off8 = off8 + jnp.where(lane8 >= sh, pltpu.roll(off8, sh, 1), 0.0)
    off8 = off8 - seg8
    carry = carry_ref[...]
    lpos = lrank + off8[0:1, :]
    pick = lambda sel, val: jnp.sum(jnp.where(sel, val, 0.0), axis=-1, keepdims=True)
    sub8 = lax.broadcasted_iota(i32, (8, 128), 0)
    tab_ref[...] = jnp.where(sub8 == 0, seg8, jnp.where(sub8 == 1, off8, jnp.where(sub8 == 2, carry, 0.0)))
    carry = carry + seg8
    carry_ref[...] = carry
    cnt_ref[...] = carry
    cols = (i1 - ROUTE_E0, i2 - ROUTE_E0, w1, w2, pick(oh1, lpos), pick(oh2, lpos))
    route = jnp.zeros_like(logits)
    for j, col in enumerate(cols):
        route = jnp.where(lane == j, col, route)
    route_ref[...] = route


def _outproj(x_ctx, x_lat, mod, fo_ctx, fo_lat, po, so, ao_ctx, ao_lat, l, w):
    nt = T_ALL // TM
    tile = lambda wd: pl.BlockSpec((TM, wd), lambda i: (i, 0))
    vec = lambda wd: pl.BlockSpec((None, 1, wd), lambda i: (l, 0, 0))
    return pl.pallas_call(
        _outproj_kernel,
        grid=(nt,),
        in_specs=[
            _ctx_tile(D_MODEL), _lat_tile(D_MODEL),
            pl.BlockSpec((None, None, 6, D_MODEL), lambda i: (l, i // (SEG // TM), 0, 0)),
            _ctx_tile(FFT_W), _lat_tile(FFT_W), tile(POOL_W), tile(SGU_W),
            _ctx_tile(ATTN_W), _lat_tile(ATTN_W),
            pl.BlockSpec((None, D_MODEL, D_MODEL), lambda i: (l, 0, 0)),
            vec(D_MODEL), vec(D_MODEL),
            pl.BlockSpec((None, D_MODEL, 256), lambda i: (l, 0, 0)),
            vec(128),
            pl.BlockSpec((TM, TM), lambda i: (0, 0)),
        ],
        out_specs=[tile(D_MODEL), tile(D_MODEL), tile(128), pl.BlockSpec((8, 128), lambda i: (0, 0)),
                   pl.BlockSpec((None, 8, 128), lambda i: (i, 0, 0))],
        out_shape=[
            jax.ShapeDtypeStruct((T_ALL, D_MODEL), f32),
            jax.ShapeDtypeStruct((T_ALL, D_MODEL), f32),
            jax.ShapeDtypeStruct((T_ALL, 128), f32),
            jax.ShapeDtypeStruct((8, 128), f32),
            jax.ShapeDtypeStruct((nt, 8, 128), f32),
        ],
        scratch_shapes=[pltpu.VMEM((8, 128), f32)],
        compiler_params=_cparams(("arbitrary",)),
        name="outproj",
    )(x_ctx, x_lat, mod, fo_ctx, fo_lat, po, so, ao_ctx, ao_lat,
      w["w_out"], w["ln1_g"], w["ln1_b"], w["w_r"], w["b_r"],
      w["tril"])


def _plan_kernel(cnt_ref, meta_ref):
    lane = lax.broadcasted_iota(i32, (8, 128), 1)
    sub = lax.broadcasted_iota(i32, (8, 128), 0)
    cnt = cnt_ref[...]
    is_e = (lane >= ROUTE_E0) & (lane < ROUTE_E0 + N_EXPERTS)
    tiles = jnp.where(is_e, jnp.floor((cnt + (MOE_TM - 1.0)) * (1.0 / MOE_TM)), 0.0)
    cum = tiles
    for s in (1, 2, 4, 8, 16):
        cum = cum + jnp.where(lane >= s, pltpu.roll(cum, s, 1), 0.0)
    pstart = (cum - tiles) * MOE_TM
    nused = jnp.max(cum, axis=-1, keepdims=True)
    fill = jnp.where(is_e & (cnt != tiles * MOE_TM), pstart + (tiles - 1.0) * MOE_TM, -1.0)
    meta = jnp.where(sub == 0, cnt, jnp.where(sub == 1, nused, jnp.where(sub == 2, fill,
                     jnp.where(sub == 3, pstart, 0.0))))
    meta_ref[...] = meta.astype(i32)


def _plan(cnt):
    return pl.pallas_call(
        _plan_kernel,
        grid=(1,),
        in_specs=[pl.BlockSpec((8, 128), lambda i: (0, 0))],
        out_specs=pl.BlockSpec((8, 128), lambda i: (0, 0)),
        out_shape=jax.ShapeDtypeStruct((8, 128), i32),
        compiler_params=_cparams(("arbitrary",)),
        name="plan",
    )(cnt)


def _dispatch_kernel(nch_ref, off_ref, dst_ref, tot_ref, fill_ref, nused_ref, h_ref, route_ref, xs_ref,
                     sorted_ref, zero_ref, sem, fill_sem):
    i = pl.program_id(0)

    def tile_fill(row0):
        return pltpu.make_async_copy(zero_ref, xs_ref.at[pl.ds(pl.multiple_of(row0, MOE_TM), MOE_TM)],
                                     fill_sem)

    @pl.when(i == 0)
    def _():
        zero_ref[...] = jnp.zeros_like(zero_ref)

        def start(e, c):
            @pl.when(fill_ref[e] >= 0)
            def _():
                tile_fill(jnp.maximum(fill_ref[e], 0)).start()
            return c

        def wait(e, c):
            @pl.when(fill_ref[e] >= 0)
            def _():
                tile_fill(jnp.maximum(fill_ref[e], 0)).wait()
            return c

        def start_tail(t, c):
            tile_fill(t * MOE_TM).start()
            return c

        def wait_tail(t, c):
            tile_fill(t * MOE_TM).wait()
            return c

        lax.fori_loop(0, N_EXPERTS, start, 0)
        lax.fori_loop(nused_ref[0], MOE_NT, start_tail, 0)
        lax.fori_loop(0, N_EXPERTS, wait, 0)
        lax.fori_loop(nused_ref[0], MOE_NT, wait_tail, 0)

    rt = route_ref[...].T
    hb = h_ref[...].astype(bf16)
    slot = i % 2
    used_rows = tot_ref[i] * ROW_CHUNK
    for rb in range(DISP_ROWS // MOE_TM):
        @pl.when(rb * MOE_TM < used_rows)
        def _(rb=rb):
            j = (lax.broadcasted_iota(i32, (MOE_TM, 1), 0) + rb * MOE_TM).astype(f32)
            sel = jnp.where((j == rt[4:5, :]) | (j == rt[5:6, :]), 1.0, 0.0).astype(bf16)
            sorted_ref[slot, rb * MOE_TM:(rb + 1) * MOE_TM, :] = _dot(sel, hb).astype(bf16)

    def per_expert(e, c):
        idx = i * N_EXPERTS + e
        n, s0, d0 = nch_ref[idx], off_ref[idx], dst_ref[idx]
        b = 1
        while b <= TM // ROW_CHUNK:
            @pl.when((n & b) != 0)
            def _(b=b):
                r0 = (n & (b - 1)) * ROW_CHUNK
                rows = b * ROW_CHUNK
                pltpu.make_async_copy(
                    sorted_ref.at[slot, pl.ds(pl.multiple_of(s0 + r0, ROW_CHUNK), rows)],
                    xs_ref.at[pl.ds(pl.multiple_of(d0 + r0, ROW_CHUNK), rows)], sem.at[slot]).start()
            b *= 2
        return c

    lax.fori_loop(0, N_EXPERTS, per_expert, 0)

    def drain(tile, s):
        rows = tot_ref[tile] * ROW_CHUNK

        @pl.when(rows > 0)
        def _():
            pltpu.make_async_copy(sorted_ref.at[s, pl.ds(0, rows)], xs_ref.at[pl.ds(0, rows)],
                                  sem.at[s]).wait()

    @pl.when(i >= 1)
    def _():
        drain(i - 1, 1 - slot)

    @pl.when(i == pl.num_programs(0) - 1)
    def _():
        drain(i, slot)


def _dispatch(nch, off, dst, tot, fill, nused, h2, route):
    grid_spec = pltpu.PrefetchScalarGridSpec(
        num_scalar_prefetch=6,
        grid=(T_ALL // TM,),
        in_specs=[pl.BlockSpec((TM, D_MODEL), lambda i, *_: (i, 0)),
                  pl.BlockSpec((TM, 128), lambda i, *_: (i, 0))],
        out_specs=pl.BlockSpec(memory_space=pl.ANY),
        scratch_shapes=[pltpu.VMEM((2, DISP_ROWS, D_MODEL), bf16), pltpu.VMEM((MOE_TM, D_MODEL), bf16),
                        pltpu.SemaphoreType.DMA((2,)), pltpu.SemaphoreType.DMA(())],
    )
    return pl.pallas_call(
        _dispatch_kernel,
        grid_spec=grid_spec,
        out_shape=jax.ShapeDtypeStruct((MOE_NT * MOE_TM, D_MODEL), bf16),
        compiler_params=_cparams(("arbitrary",)),
        name="dispatch",
    )(nch, off, dst, tot, fill, nused, h2, route)


def _experts_kernel(cnt_ref, nused_ref, xs_ref, wg_hbm, wu_hbm, wd_hbm, ys_ref,
                    wg_f, wu_f, wd_f, wg_b, wu_b, wd_b, st, wsem, *, layer):
    i = pl.program_id(0)
    nused = nused_ref[0]
    NXT, NSLOT, LEFT, ROWS = 0, 1, 2, 3

    def w_copies(e, slot):
        return (pltpu.make_async_copy(wg_hbm.at[layer, e], wg_f.at[slot], wsem.at[slot, 0]),
                pltpu.make_async_copy(wu_hbm.at[layer, e], wu_f.at[slot], wsem.at[slot, 1]),
                pltpu.make_async_copy(wd_hbm.at[layer, e], wd_f.at[slot], wsem.at[slot, 2]))

    def next_nonempty(e):
        return lax.while_loop(
            lambda v: (v < N_EXPERTS) & (cnt_ref[jnp.minimum(v, N_EXPERTS - 1)] == 0),
            lambda v: v + 1, e)

    @pl.when(i == 0)
    def _():
        e0 = next_nonempty(jnp.int32(0))
        for c in w_copies(e0, 0):
            c.start()
        st[NXT] = e0
        st[NSLOT] = 0
        st[LEFT] = 0

    @pl.when(i < nused)
    def _():
        @pl.when(st[LEFT] == 0)
        def _():
            e = st[NXT]
            slot = st[NSLOT]
            for c in w_copies(e, slot):
                c.wait()
            e2 = next_nonempty(e + 1)

            @pl.when(e2 < N_EXPERTS)
            def _():
                for c in w_copies(e2, 1 - slot):
                    c.start()

            st[NXT] = e2
            st[NSLOT] = 1 - slot
            st[LEFT] = (cnt_ref[e] + (MOE_TM - 1)) // MOE_TM
            st[ROWS] = cnt_ref[e]
            wg_b[...] = wg_f[slot].astype(bf16)
            wu_b[...] = wu_f[slot].astype(bf16)
            wd_b[...] = wd_f[slot].astype(bf16)

        def ffn(rows):
            x = xs_ref[rows, :]
            hg = _dot(x, wg_b[...])
            hu = _dot(x, wu_b[...])
            act = (hg * jax.nn.sigmoid(hg)) * hu
            ys_ref[rows, :] = _dot(act.astype(bf16), wd_b[...]).astype(bf16)

        half = MOE_TM // 2
        short = st[ROWS] <= half

        @pl.when(short)
        def _():
            ffn(slice(0, half))
            ys_ref[half:, :] = jnp.zeros((MOE_TM - half, D_MODEL), bf16)

        @pl.when(jnp.logical_not(short))
        def _():
            ffn(slice(0, MOE_TM))

        st[LEFT] = st[LEFT] - 1
        st[ROWS] = st[ROWS] - MOE_TM

    @pl.when(i >= nused)
    def _():
        ys_ref[...] = jnp.zeros_like(ys_ref)


def _experts(counts, nused, xs, l, w_gate, w_up, w_down):
    hbm = pl.BlockSpec(memory_space=pl.ANY)
    grid_spec = pltpu.PrefetchScalarGridSpec(
        num_scalar_prefetch=2,
        grid=(MOE_NT,),
        in_specs=[pl.BlockSpec((MOE_TM, D_MODEL), lambda i, c, nu: (jnp.minimum(i, nu[0] - 1), 0)),
                  hbm, hbm, hbm],
        out_specs=pl.BlockSpec((MOE_TM, D_MODEL), lambda i, c, nu: (i, 0)),
        scratch_shapes=[
            pltpu.VMEM((2, D_MODEL, EXPERT_FF), f32),
            pltpu.VMEM((2, D_MODEL, EXPERT_FF), f32),
            pltpu.VMEM((2, EXPERT_FF, D_MODEL), f32),
            pltpu.VMEM((D_MODEL, EXPERT_FF), bf16),
            pltpu.VMEM((D_MODEL, EXPERT_FF), bf16),
            pltpu.VMEM((EXPERT_FF, D_MODEL), bf16),
            pltpu.SMEM((4,), i32),
            pltpu.SemaphoreType.DMA((2, 3)),
        ],
    )
    return pl.pallas_call(
        functools.partial(_experts_kernel, layer=l),
        grid_spec=grid_spec,
        out_shape=jax.ShapeDtypeStruct((MOE_NT * MOE_TM, D_MODEL), bf16),
        compiler_params=_cparams(("arbitrary",)),
        name="experts",
    )(counts, nused, xs, w_gate, w_up, w_down)


def _combine_kernel(nch_ref, off_ref, dst_ref, tot_ref, x1_ref, mod_ref, route_ref, ys_hbm, g_ref, b_ref,
                    oc_ref, ol_ref, ybuf, sem):
    i = pl.program_id(0)
    nt = pl.num_programs(0) - 1

    @pl.when(i == 0)
    def _():
        ybuf[...] = jnp.zeros_like(ybuf)

    @pl.when(i < nt)
    def _():
        slot = i % 2

        def per_expert(e, c):
            idx = i * N_EXPERTS + e
            n, s0, d0 = nch_ref[idx], off_ref[idx], dst_ref[idx]
            b = 1
            while b <= TM // ROW_CHUNK:
                @pl.when((n & b) != 0)
                def _(b=b):
                    r0 = (n & (b - 1)) * ROW_CHUNK
                    rows = b * ROW_CHUNK
                    pltpu.make_async_copy(
                        ys_hbm.at[pl.ds(pl.multiple_of(d0 + r0, ROW_CHUNK), rows)],
                        ybuf.at[slot, pl.ds(pl.multiple_of(s0 + r0, ROW_CHUNK), rows)],
                        sem.at[slot]).start()
                b *= 2
            return c

        lax.fori_loop(0, N_EXPERTS, per_expert, 0)

    @pl.when(i >= 1)
    def _():
        slot = (i - 1) % 2
        rows = tot_ref[i - 1] * ROW_CHUNK

        @pl.when(rows > 0)
        def _():
            pltpu.make_async_copy(ys_hbm.at[pl.ds(0, rows)], ybuf.at[slot, pl.ds(0, rows)],
                                  sem.at[slot]).wait()

        route = route_ref[...]
        mod = mod_ref[...]
        lane = lax.broadcasted_iota(i32, (1, DISP_ROWS), 1).astype(f32)
        wmat = (jnp.where(lane == route[:, 4:5], route[:, 2:3], 0.0)
                + jnp.where(lane == route[:, 5:6], route[:, 3:4], 0.0))
        moe = _dot(wmat.astype(bf16), ybuf[slot])
        y = _layer_norm(DEEPNORM_ALPHA * x1_ref[...] + mod[5:6] * moe, g_ref[...], b_ref[...])

        @pl.when(i - 1 < T_CTX // TM)
        def _():
            oc_ref[...] = y

        @pl.when(i - 1 >= T_CTX // TM)
        def _():
            ol_ref[...] = y


def _combine(nch, off, dst, tot, x1, mod, route, ys, l, w):
    nt = T_ALL // TM
    vec = pl.BlockSpec((None, 1, D_MODEL), lambda i, *_: (l, 0, 0))
    nctx = T_CTX // TM
    prev = lambda i: jnp.maximum(i - 1, 0)
    grid_spec = pltpu.PrefetchScalarGridSpec(
        num_scalar_prefetch=4,
        grid=(nt + 1,),
        in_specs=[
            pl.BlockSpec((TM, D_MODEL), lambda i, *_: (prev(i), 0)),
            pl.BlockSpec((None, None, 6, D_MODEL), lambda i, *_: (l, prev(i) // (SEG // TM), 0, 0)),
            pl.BlockSpec((TM, 128), lambda i, *_: (prev(i), 0)),
            pl.BlockSpec(memory_space=pl.ANY),
            vec, vec,
        ],
        out_specs=[pl.BlockSpec((TM, D_MODEL), lambda i, *_: (jnp.minimum(prev(i), nctx - 1), 0)),
                   pl.BlockSpec((TM, D_MODEL), lambda i, *_: (jnp.maximum(prev(i) - nctx, 0), 0))],
        scratch_shapes=[pltpu.VMEM((2, DISP_ROWS, D_MODEL), bf16), pltpu.SemaphoreType.DMA((2,))],
    )
    return pl.pallas_call(
        _combine_kernel,
        grid_spec=grid_spec,
        out_shape=[jax.ShapeDtypeStruct((T_CTX, D_MODEL), f32),
                   jax.ShapeDtypeStruct((T_LAT, D_MODEL), f32)],
        compiler_params=_cparams(("arbitrary",)),
        name="combine",
    )(nch, off, dst, tot, x1, mod, route, ys, w["ln2_g"], w["ln2_b"])


def _dft_cos_sin(n, scale):
    k = jnp.arange(n, dtype=i32)
    ang = ((k[:, None] * k[None, :]) % n).astype(f32) * np.float32(2.0 * np.pi / n)
    return jnp.cos(ang) * scale, jnp.sin(ang) * scale


def _seq_dft_matrix(n):
    g = min(DFT_SPLIT, n)
    j = jnp.arange(n, dtype=i32)[None, :]
    k1 = jnp.arange(n // g, dtype=i32)[:, None]
    k2 = jnp.arange(g, dtype=i32)[:, None]
    ang_a = ((k1 * j) % (n // g)).astype(f32) * np.float32(2.0 * np.pi * g / n)
    ang_b = ((k2 * j) % n).astype(f32) * np.float32(2.0 * np.pi / n)
    scale = np.float32(n ** -0.5)
    ca, sa = jnp.cos(ang_a), jnp.sin(ang_a)
    cb, sb = jnp.cos(ang_b) * scale, jnp.sin(ang_b) * scale
    ca2 = jnp.concatenate([ca, ca], axis=1)[:, None, :]
    sa2 = jnp.concatenate([sa, sa], axis=1)[:, None, :]
    cb2 = jnp.concatenate([cb, -sb], axis=1)[None, :, :]
    sb2 = jnp.concatenate([sb, cb], axis=1)[None, :, :]
    return (ca2 * cb2 - sa2 * sb2).astype(bf16).reshape(n, 2 * n)


def _rope_tables():
    rows = DEC_SEQ // GRID_W
    row = jnp.repeat(jnp.arange(rows), GRID_W).astype(f32)
    col = jnp.tile(jnp.arange(GRID_W), rows).astype(f32)
    n_freq = HEAD_DIM // 4
    inv = ROPE_THETA ** (-jnp.arange(n_freq, dtype=f32) / n_freq)
    ar = row[:, None] * inv
    ac = col[:, None] * inv
    ang = jnp.concatenate([ar, ar, ac, ac], axis=-1)
    cos = jnp.tile(jnp.cos(ang), (1, N_HEADS))
    sin = jnp.tile(jnp.sin(ang), (1, N_HEADS))
    first = (jnp.arange(ATTN_W) % (HEAD_DIM // 2)) < n_freq
    sin_a = jnp.where(first[None, :], -sin, 0.0)
    sin_b = jnp.where(first[None, :], 0.0, sin)
    ident = jnp.zeros((TM, ATTN_W), f32)
    return (jnp.concatenate([cos, ident + 1.0], axis=0),
            jnp.concatenate([sin_a, ident], axis=0),
            jnp.concatenate([sin_b, ident], axis=0))


def _dup_cache(cache):
    c = jnp.transpose(cache, (1, 3, 0, 2, 4))
    return jnp.concatenate([c, c], axis=-1).astype(bf16)


def kernel(x_prompt, x_sample, cache_k, cache_v, c, c_ctx, w_mod, b_mod, w_in, w_fft, w_pool, pool_scale, sgu_ln_g, sgu_ln_b, w_sgu, b_sgu, q_norm_g, k_norm_g, w_out, ln1_g, ln1_b, w_router_group, b_router_group, w_router_expert, b_router_expert, w_gate, w_up, w_down, ln2_g, ln2_b):
    L = DEPTH
    x_ctx = x_prompt.reshape(T_CTX, D_MODEL)
    x_lat = x_sample.reshape(T_LAT, D_MODEL)

    cond8 = jnp.concatenate([c_ctx[None, :], c, jnp.zeros((8 - 1 - DEC_BATCH, D_MODEL), f32)], axis=0)
    mod = _modulation(cond8, w_mod, b_mod)[:, :N_SEG].reshape(L, N_SEG, 6, D_MODEL)

    cc, sc = _dft_cos_sin(FFT_W, np.float32(FFT_W ** -0.5))
    rope_cos, rope_sin_a, rope_sin_b = _rope_tables()
    head_id = jnp.arange(ATTN_W) // HEAD_DIM
    eye_g = jnp.eye(len(POOL_WINDOWS), dtype=f32)
    w_r = jnp.zeros((L, D_MODEL, 128), f32)
    w_r = w_r.at[:, :, :N_GROUPS].set(w_router_group).at[:, :, ROUTE_E0:ROUTE_E0 + N_EXPERTS].set(w_router_expert)
    b_r = jnp.zeros((L, 1, 128), f32)
    b_r = b_r.at[:, 0, :N_GROUPS].set(b_router_group).at[:, 0, ROUTE_E0:ROUTE_E0 + N_EXPERTS].set(b_router_expert)
    w_r_hi, w_r_lo = _split_hi_lo(w_r)
    w = {
        "w_in": w_in.astype(bf16),
        "csc": jnp.concatenate([cc, sc], axis=1).astype(bf16),
        "w_sgu": jnp.transpose(w_sgu, (0, 2, 1, 3)).reshape(L, CHUNK, SGU_HEADS * CHUNK).astype(bf16),
        "b_sgu": jnp.repeat(jnp.transpose(b_sgu, (0, 2, 1)), SGU_W // SGU_HEADS, axis=2),
        "sgu_ln_g": sgu_ln_g.reshape(L, 1, SGU_W),
        "sgu_ln_b": sgu_ln_b.reshape(L, 1, SGU_W),
        "q_norm_g": jnp.tile(q_norm_g, (1, N_HEADS)).reshape(L, 1, ATTN_W),
        "k_norm_g": jnp.tile(k_norm_g, (1, N_KV_HEADS)).reshape(L, 1, KV_W),
        "rope_cos": rope_cos, "rope_sin_a": rope_sin_a, "rope_sin_b": rope_sin_b,
        "ones_bd": (head_id[:, None] == head_id[None, :]).astype(bf16),
        "w_pool_bd": jnp.einsum("lgcd,gh->lgchd", w_pool, eye_g).reshape(L, POOL_W, POOL_W).astype(bf16),
        "pool_scale": pool_scale.reshape(L, 1, POOL_W),
        "w_fft": w_fft.astype(bf16),
        "w_out": w_out.astype(bf16),
        "ln1_g": ln1_g.reshape(L, 1, D_MODEL), "ln1_b": ln1_b.reshape(L, 1, D_MODEL),
        "ln2_g": ln2_g.reshape(L, 1, D_MODEL), "ln2_b": ln2_b.reshape(L, 1, D_MODEL),
        "w_r": jnp.concatenate([w_r_hi, w_r_lo], axis=-1), "b_r": b_r,
        "tril": (jnp.arange(TM)[:, None] > jnp.arange(TM)[None, :]).astype(bf16),
    }
    m_ctx = _seq_dft_matrix(SEQ)
    m_lat = _seq_dft_matrix(DEC_SEQ)
    kc_all = _dup_cache(cache_k)
    vc_all = _dup_cache(cache_v)

    new_k, new_v = [], []
    for l in range(L):
        pq, praw, sgu, q, kd, vd, nk, nv = _inproj(x_ctx, x_lat, mod, l, w)
        new_k.append(nk[:T_CTX].reshape(BATCH, SEQ, N_KV_HEADS, HEAD_DIM))
        new_v.append(nv[:T_CTX].reshape(BATCH, SEQ, N_KV_HEADS, HEAD_DIM))
        po = _pool(praw, l, w)
        fo_ctx = _seqdft(pq, m_ctx, l, w, n=SEQ, tr=SEQ, nseq=BATCH, row0=0)
        fo_lat = _seqdft(pq, m_lat, l, w, n=DEC_SEQ, tr=FFT_TR, nseq=DEC_BATCH, row0=T_CTX)
        ao_ctx = _attention(q, kd, vd, None, n=SEQ, tq=SEQ, nseq=BATCH, row0=0, heads=N_KV_HEADS)
        ao_lat = _attention(q, kd, vd, (kc_all[l], vc_all[l]), n=DEC_SEQ, tq=ATT_TQ, nseq=DEC_BATCH,
                            row0=T_CTX, heads=1)
        x1, h2, route, cnt, tab = _outproj(x_ctx, x_lat, mod, fo_ctx, fo_lat, po, sgu, ao_ctx, ao_lat, l, w)
        meta = _plan(cnt)
        experts = slice(ROUTE_E0, ROUTE_E0 + N_EXPERTS)
        counts = meta[0, experts]
        nused = meta[1, :1]
        fill = meta[2, experts]
        tab = tab[:, :, experts].astype(i32)
        nch = (tab[:, 0] // ROW_CHUNK).reshape(-1)
        off = tab[:, 1].reshape(-1)
        dst = (meta[3, experts][None, :] + tab[:, 2]).reshape(-1)
        tot = jnp.sum(tab[:, 0], axis=1) // ROW_CHUNK
        xs = _dispatch(nch, off, dst, tot, fill, nused, h2, route)
        ys = _experts(counts, nused, xs, l, w_gate, w_up, w_down)
        x_ctx, x_lat = _combine(nch, off, dst, tot, x1, mod, route, ys, l, w)

    y_prompt = x_ctx.reshape(BATCH, SEQ, D_MODEL)
    y_sample = x_lat.reshape(DEC_BATCH, DEC_SEQ, D_MODEL)
    return (y_prompt, y_sample, jnp.stack(new_k, axis=1), jnp.stack(new_v, axis=1))
```

```python
import functools

import numpy as np
import jax
import jax.numpy as jnp
from jax import lax
from jax.experimental import pallas as pl
from jax.experimental.pallas import tpu as pltpu

f32 = jnp.float32
bf16 = jnp.bfloat16
i32 = jnp.int32

D_MODEL = 1024
BATCH = 16
SEQ = 256
DEPTH = 4
DEC_BATCH = 2
DEC_SEQ = 4096
PAST_LEN = 512
GRID_W = 64
FFT_W = 256
POOL_W = 256
POOL_WINDOWS = (2, 4, 8, 16)
POOL_GROUP = 64
SGU_W = 256
SGU_HEADS = 4
CHUNK = 128
HEAD_DIM = 64
ATTN_W = 256
N_HEADS = 4
N_KV_HEADS = 2
KV_W = 128
IN_W = 1536
ROPE_THETA = 10000.0
N_GROUPS = 4
EXPERTS_PER_GROUP = 8
N_EXPERTS = 32
EXPERT_FF = 512
DEEPNORM_ALPHA = float((2 * DEPTH) ** 0.25)
LN_EPS = 1e-5
RMS_EPS = 1e-6

T_CTX = BATCH * SEQ
T_LAT = DEC_BATCH * DEC_SEQ
T_ALL = T_CTX + T_LAT
SEG = 4096
N_SEG = T_ALL // SEG

TM = 512
POOL_TB = 512
POOL_HALO = 8
FFT_TR = 512
ATT_TQ = 512
ATT_CHUNK = 1024
DFT_SPLIT = 64
MOE_TM = 256
ROW_CHUNK = 16
MOE_ROWS = 2 * T_ALL
MOE_PAD_ROWS = (T_ALL // TM) * N_EXPERTS * (ROW_CHUNK - 1)
MOE_NT = -(-(MOE_ROWS + MOE_PAD_ROWS) // MOE_TM) + N_EXPERTS
DISP_ROWS = 2 * TM + N_EXPERTS * ROW_CHUNK
DISP_ROWS_SHORT = 2 * TM + N_EXPERTS * ROW_CHUNK // 2
ROUTE_E0 = 32
VMEM_LIMIT = 56 * 1024 * 1024


def _cparams(sem):
    return pltpu.CompilerParams(dimension_semantics=sem, vmem_limit_bytes=VMEM_LIMIT)


def _split_hi_lo(a):
    hi = a.astype(bf16)
    lo = (a - hi.astype(f32)).astype(bf16)
    return hi, lo


def _dot(a, b):
    return jnp.dot(a, b, preferred_element_type=f32)


def _mod_kernel(c_ref, w_ref, b_ref, o_ref):
    c = c_ref[...]
    s = c * jax.nn.sigmoid(c)
    s_hi, s_lo = _split_hi_lo(s)
    w_hi, w_lo = _split_hi_lo(w_ref[...])
    o_ref[...] = _dot(s_hi, w_hi) + _dot(s_hi, w_lo) + _dot(s_lo, w_hi) + b_ref[...]


def _modulation(cond8, w_mod, b_mod):
    tn = 1536
    return pl.pallas_call(
        _mod_kernel,
        grid=(DEPTH, 6 * D_MODEL // tn),
        in_specs=[
            pl.BlockSpec((8, D_MODEL), lambda l, j: (0, 0)),
            pl.BlockSpec((None, D_MODEL, tn), lambda l, j: (l, 0, j)),
            pl.BlockSpec((None, 1, tn), lambda l, j: (l, 0, j)),
        ],
        out_specs=pl.BlockSpec((None, 8, tn), lambda l, j: (l, 0, j)),
        out_shape=jax.ShapeDtypeStruct((DEPTH, 8, 6 * D_MODEL), f32),
        compiler_params=_cparams(("arbitrary", "arbitrary")),
        name="modulation",
    )(cond8, w_mod, b_mod.reshape(DEPTH, 1, 6 * D_MODEL))


def _head_rms(x, ones_bd, gain):
    ss = _dot((x * x).astype(bf16), ones_bd)
    return x * lax.rsqrt(ss * (1.0 / HEAD_DIM) + RMS_EPS) * gain


def _rope(x, cos, sin_a, sin_b):
    w = x.shape[-1]
    q4 = HEAD_DIM // 4
    return x * cos + pltpu.roll(x, w - q4, 1) * sin_a + pltpu.roll(x, q4, 1) * sin_b


def _dup_half(x, first):
    lane = lax.broadcasted_iota(i32, x.shape, 1)
    r = pltpu.roll(x, HEAD_DIM, 1)
    if first:
        return jnp.where(lane < HEAD_DIM, x, r)
    return jnp.where(lane >= HEAD_DIM, x, r)


def _gelu_tanh(x):
    c = np.sqrt(2.0 / np.pi).astype(np.float32)
    return x * (0.5 * (1.0 + jnp.tanh(c * (x + 0.044715 * (x * x * x)))))


def _inproj_kernel(xc_ref, xl_ref, mod_ref, win_ref, csc_ref, wsgu_ref, bsgu_ref, lng_ref, lnb_ref,
                   qg_ref, kg_ref, cos_ref, sina_ref, sinb_ref, ones_ref,
                   pq_ref, pool_ref, sgu_ref, q_ref, kd_ref, vd_ref, nk_ref, nv_ref):
    x = jnp.where(pl.program_id(0) < T_CTX // TM, xc_ref[...], xl_ref[...])
    mod = mod_ref[...]
    h = (x * (1.0 + mod[1:2]) + mod[0:1]).astype(bf16)
    proj = _dot(h, win_ref[...])

    a = proj[:, 0:FFT_W].astype(bf16)
    pq_ref[...] = _dot(a, csc_ref[...]).astype(bf16)

    pool_ref[...] = proj[:, FFT_W:FFT_W + POOL_W]

    o = FFT_W + POOL_W
    hgu = _gelu_tanh(proj[:, o:o + 2 * SGU_W])
    u = hgu[:, :SGU_W]
    v = hgu[:, SGU_W:]
    mu = jnp.mean(v, axis=-1, keepdims=True)
    vc = v - mu
    var = jnp.mean(vc * vc, axis=-1, keepdims=True)
    v = vc * lax.rsqrt(var + LN_EPS) * lng_ref[...] + lnb_ref[...]
    lane = lax.broadcasted_iota(i32, (CHUNK, SGU_W), 1)
    head = lane // (SGU_W // SGU_HEADS)
    wcat = wsgu_ref[...]
    for cidx in range(TM // CHUNK):
        rows = slice(cidx * CHUNK, (cidx + 1) * CHUNK)
        vch = v[rows]
        vblk = jnp.concatenate(
            [jnp.where(head == g, vch, 0.0) for g in range(SGU_HEADS)], axis=0).astype(bf16)
        sp = _dot(wcat, vblk) + bsgu_ref[...]
        sgu_ref[rows, :] = (u[rows] * sp).astype(bf16)

    o = o + 2 * SGU_W
    ones_bd = ones_ref[...]
    cos = cos_ref[...]
    sin_a = sina_ref[...]
    sin_b = sinb_ref[...]
    q = _head_rms(proj[:, o:o + ATTN_W], ones_bd, qg_ref[...])
    q = _rope(q, cos, sin_a, sin_b) * np.float32(HEAD_DIM ** -0.5 * np.log2(np.e))
    q_ref[...] = q.astype(bf16)
    o = o + ATTN_W
    k = _head_rms(proj[:, o:o + KV_W], ones_bd[:KV_W, :KV_W], kg_ref[...])
    nk_ref[...] = k
    k = _rope(k, cos[:, :KV_W], sin_a[:, :KV_W], sin_b[:, :KV_W])
    kd_ref[0] = _dup_half(k, True).astype(bf16)
    kd_ref[1] = _dup_half(k, False).astype(bf16)
    o = o + KV_W
    vv = proj[:, o:o + KV_W]
    nv_ref[...] = vv
    vd_ref[0] = _dup_half(vv, True).astype(bf16)
    vd_ref[1] = _dup_half(vv, False).astype(bf16)


def _rope_block(i):
    nlat = DEC_SEQ // TM
    nctx = T_CTX // TM
    return jnp.where(i < nctx, nlat, (i - nctx) % nlat)


def _ctx_tile(wd):
    return pl.BlockSpec((TM, wd), lambda i, *_: (jnp.minimum(i, T_CTX // TM - 1), 0))


def _lat_tile(wd):
    return pl.BlockSpec((TM, wd), lambda i, *_: (jnp.maximum(i - T_CTX // TM, 0), 0))


def _inproj(x_ctx, x_lat, mod, l, w):
    nt = T_ALL // TM
    tile = lambda wd: pl.BlockSpec((TM, wd), lambda i: (i, 0))
    const = lambda shape: pl.BlockSpec(shape, lambda i: (0,) * len(shape))
    rope_spec = pl.BlockSpec((TM, ATTN_W), lambda i: (_rope_block(i), 0))
    return pl.pallas_call(
        _inproj_kernel,
        grid=(nt,),
        in_specs=[
            _ctx_tile(D_MODEL), _lat_tile(D_MODEL),
            pl.BlockSpec((None, None, 6, D_MODEL), lambda i: (l, i // (SEG // TM), 0, 0)),
            pl.BlockSpec((None, D_MODEL, IN_W), lambda i: (l, 0, 0)),
            const((FFT_W, 2 * FFT_W)),
            pl.BlockSpec((None, CHUNK, SGU_HEADS * CHUNK), lambda i: (l, 0, 0)),
            pl.BlockSpec((None, CHUNK, SGU_W), lambda i: (l, 0, 0)),
            pl.BlockSpec((None, 1, SGU_W), lambda i: (l, 0, 0)),
            pl.BlockSpec((None, 1, SGU_W), lambda i: (l, 0, 0)),
            pl.BlockSpec((None, 1, ATTN_W), lambda i: (l, 0, 0)),
            pl.BlockSpec((None, 1, KV_W), lambda i: (l, 0, 0)),
            rope_spec, rope_spec, rope_spec,
            const((ATTN_W, ATTN_W)),
        ],
        out_specs=[
            tile(2 * FFT_W), tile(POOL_W), tile(SGU_W), tile(ATTN_W),
            pl.BlockSpec((N_KV_HEADS, TM, KV_W), lambda i: (0, i, 0)),
            pl.BlockSpec((N_KV_HEADS, TM, KV_W), lambda i: (0, i, 0)),
            tile(KV_W), tile(KV_W),
        ],
        out_shape=[
            jax.ShapeDtypeStruct((T_ALL, 2 * FFT_W), bf16),
            jax.ShapeDtypeStruct((T_ALL, POOL_W), f32),
            jax.ShapeDtypeStruct((T_ALL, SGU_W), bf16),
            jax.ShapeDtypeStruct((T_ALL, ATTN_W), bf16),
            jax.ShapeDtypeStruct((N_KV_HEADS, T_ALL, KV_W), bf16),
            jax.ShapeDtypeStruct((N_KV_HEADS, T_ALL, KV_W), bf16),
            jax.ShapeDtypeStruct((T_ALL, KV_W), f32),
            jax.ShapeDtypeStruct((T_ALL, KV_W), f32),
        ],
        compiler_params=_cparams(("arbitrary",)),
        name="inproj",
    )(x_ctx, x_lat, mod, w["w_in"], w["csc"], w["w_sgu"], w["b_sgu"], w["sgu_ln_g"], w["sgu_ln_b"],
      w["q_norm_g"], w["k_norm_g"], w["rope_cos"], w["rope_sin_a"], w["rope_sin_b"], w["ones_bd"])


def _pool_kernel(prev_ref, cur_ref, next_ref, wp_ref, scale_ref, o_ref):
    i = pl.program_id(0)
    n = jnp.where(i < T_CTX // POOL_TB, SEQ, DEC_SEQ)
    hl = POOL_HALO
    ext = jnp.concatenate([prev_ref[POOL_TB - hl:, :], cur_ref[...], next_ref[:hl, :]], axis=0)
    rows = POOL_TB + 2 * hl
    r = lax.broadcasted_iota(i32, (rows, 1), 0)
    pos = (i * POOL_TB + r - hl) & (n - 1)

    def back(a, s):
        return jnp.where(pos >= s, pltpu.roll(a, s, 0), 0.0)

    def fwd(a, s):
        return jnp.where(pos + s < n, pltpu.roll(a, rows - s, 0), 0.0)

    bsum = [back(ext, 1)]
    fsum = [ext]
    for k in range(3):
        s = 1 << k
        bsum.append(bsum[k] + back(bsum[k], s))
        fsum.append(fsum[k] + fwd(fsum[k], s))
    lane = lax.broadcasted_iota(i32, (1, POOL_W), 1)
    grp = lane // POOL_GROUP
    win = bsum[3] + fsum[3]
    half = jnp.full((1, POOL_W), POOL_WINDOWS[3] // 2, i32)
    for g in (2, 1, 0):
        win = jnp.where(grp == g, bsum[g] + fsum[g], win)
        half = jnp.where(grp == g, POOL_WINDOWS[g] // 2, half)
    cnt = (jnp.minimum(pos + half, n) - jnp.maximum(pos - half, 0)).astype(f32)
    y = (win / cnt - ext)[hl:hl + POOL_TB]
    o_ref[...] = (_dot(y.astype(bf16), wp_ref[...]) * scale_ref[...]).astype(bf16)


def _pool(p, l, w):
    nt = T_ALL // POOL_TB
    blk = lambda f: pl.BlockSpec((POOL_TB, POOL_W), lambda i: (f(i), 0))
    return pl.pallas_call(
        _pool_kernel,
        grid=(nt,),
        in_specs=[
            blk(lambda i: jnp.maximum(i - 1, 0)), blk(lambda i: i),
            blk(lambda i: jnp.minimum(i + 1, nt - 1)),
            pl.BlockSpec((None, POOL_W, POOL_W), lambda i: (l, 0, 0)),
            pl.BlockSpec((None, 1, POOL_W), lambda i: (l, 0, 0)),
        ],
        out_specs=blk(lambda i: i),
        out_shape=jax.ShapeDtypeStruct((T_ALL, POOL_W), bf16),
        compiler_params=_cparams(("arbitrary",)),
        name="pool",
    )(p, p, p, w["w_pool_bd"], w["pool_scale"])


def _seqdft_kernel(*refs, n, nseq):
    m_ref, pq_refs, w_ref, o_ref = refs[0], refs[1:-2], refs[-2], refs[-1]
    per_blk = SEG // n
    tr = m_ref.shape[0]

    def one_sequence(b, rows):
        pq_ref = pq_refs[b // per_blk]
        r0 = (b % per_blk) * n
        f = (_dot(m_ref[:, :n], pq_ref[r0:r0 + n, :FFT_W])
             + _dot(m_ref[:, n:], pq_ref[r0:r0 + n, FFT_W:]))
        o_ref[rows, :] = _dot(f.astype(bf16), w_ref[...]).astype(bf16)

    if tr == n:
        for b in range(nseq):
            one_sequence(b, slice(b * n, (b + 1) * n))
    else:
        for b in range(nseq):
            @pl.when(pl.program_id(1) == b)
            def _(b=b):
                one_sequence(b, slice(0, tr))


def _seqdft(pq, m, l, w, *, n, tr, nseq, row0):
    nr = n // tr
    nblk = nseq * n // SEG
    pq_specs = [pl.BlockSpec((SEG, 2 * FFT_W), lambda i, b, j=j: (row0 // SEG + j, 0))
                for j in range(nblk)]
    if nr == 1:
        grid, out_spec = (1, 1), pl.BlockSpec((nseq * n, FFT_W), lambda i, b: (0, 0))
    else:
        grid, out_spec = (nr, nseq), pl.BlockSpec((tr, FFT_W), lambda i, b: (b * nr + i, 0))
    return pl.pallas_call(
        functools.partial(_seqdft_kernel, n=n, nseq=nseq),
        grid=grid,
        in_specs=[pl.BlockSpec((tr, 2 * n), lambda i, b: (i, 0))] + pq_specs
        + [pl.BlockSpec((None, FFT_W, FFT_W), lambda i, b: (l, 0, 0))],
        out_specs=out_spec,
        out_shape=jax.ShapeDtypeStruct((nseq * n, FFT_W), bf16),
        compiler_params=_cparams(("arbitrary", "arbitrary")),
        name="seqdft_%d" % n,
    )(m, *([pq] * nblk), w["w_fft"])


def _attn_kernel(*refs, has_cache):
    if has_cache:
        q_ref, k_ref, v_ref, kc_ref, vc_ref, o_ref = refs
    else:
        q_ref, k_ref, v_ref, o_ref = refs
    pair = 2 * HEAD_DIM
    tq = q_ref.shape[0]
    n = k_ref.shape[1]
    nt = (((1,), (1,)), ((), ()))
    chunk = min(n, ATT_CHUNK)
    lane = lax.broadcasted_iota(i32, (tq, pair), 1)
    for h in range(k_ref.shape[0]):
        q = q_ref[:, h * pair:(h + 1) * pair]
        zero = jnp.zeros_like(q)
        qs = jnp.concatenate([jnp.where(lane < HEAD_DIM, q, zero),
                              jnp.where(lane >= HEAD_DIM, q, zero)], axis=0)
        parts = [(k_ref, v_ref, c * chunk, chunk) for c in range(n // chunk)]
        if has_cache:
            parts = [(kc_ref, vc_ref, 0, PAST_LEN)] + parts
        m = jnp.full((2 * tq, 1), -jnp.inf, f32)
        den = jnp.zeros((2 * tq, 1), f32)
        acc = jnp.zeros((2 * tq, pair), f32)
        for kr, vr, off, size in parts:
            s = lax.dot_general(qs, kr[h, off:off + size, :], nt, preferred_element_type=f32)
            m_new = jnp.maximum(m, jnp.max(s, axis=-1, keepdims=True))
            alpha = jnp.exp2(m - m_new)
            p = jnp.exp2(s - m_new).astype(bf16)
            den = alpha * den + jnp.sum(p.astype(f32), axis=-1, keepdims=True)
            acc = alpha * acc + _dot(p, vr[h, off:off + size, :])
            m = m_new
        out = acc / den
        o_ref[:, h * pair:(h + 1) * pair] = jnp.where(lane < HEAD_DIM, out[:tq], out[tq:]).astype(bf16)


def _attention(q, kd, vd, cache, *, n, tq, nseq, row0, heads):
    nq = n // tq
    b0 = row0 // n
    q0 = row0 // tq
    in_specs = [
        pl.BlockSpec((tq, heads * 2 * HEAD_DIM), lambda b, h, i: (q0 + b * nq + i, h)),
        pl.BlockSpec((heads, n, KV_W), lambda b, h, i: (h, b0 + b, 0)),
        pl.BlockSpec((heads, n, KV_W), lambda b, h, i: (h, b0 + b, 0)),
    ]
    args = [q, kd, vd]
    if cache is not None:
        cspec = pl.BlockSpec((heads, None, PAST_LEN, KV_W), lambda b, h, i: (h, b, 0, 0))
        in_specs += [cspec, cspec]
        args += list(cache)
    return pl.pallas_call(
        functools.partial(_attn_kernel, has_cache=cache is not None),
        grid=(nseq, N_KV_HEADS // heads, nq),
        in_specs=in_specs,
        out_specs=pl.BlockSpec((tq, heads * 2 * HEAD_DIM), lambda b, h, i: (b * nq + i, h)),
        out_shape=jax.ShapeDtypeStruct((nseq * n, ATTN_W), bf16),
        compiler_params=_cparams(("arbitrary", "arbitrary", "arbitrary")),
        name="attention_%d" % n,
    )(*args)


def _layer_norm(x, g, b):
    mu = jnp.mean(x, axis=-1, keepdims=True)
    xc = x - mu
    var = jnp.mean(xc * xc, axis=-1, keepdims=True)
    return xc * lax.rsqrt(var + LN_EPS) * g + b


def _outproj_kernel(xc_ref, xl_ref, mod_ref, fc_ref, fl_ref, p_ref, s_ref, ac_ref, al_ref, wout_ref,
                    g_ref, b_ref, wr_ref, br_ref, tril_ref,
                    x1_ref, h2_ref, route_ref, cnt_ref, tab_ref, carry_ref):
    i = pl.program_id(0)

    @pl.when(i == 0)
    def _():
        carry_ref[...] = jnp.zeros_like(carry_ref)

    mod = mod_ref[...]
    is_ctx = i < T_CTX // TM
    f_mix = jnp.where(is_ctx, fc_ref[...], fl_ref[...])
    a_mix = jnp.where(is_ctx, ac_ref[...], al_ref[...])
    mix = _dot(jnp.concatenate([f_mix, p_ref[...], s_ref[...], a_mix], axis=1), wout_ref[...])
    x = jnp.where(is_ctx, xc_ref[...], xl_ref[...])
    x1 = _layer_norm(DEEPNORM_ALPHA * x + mod[2:3] * mix, g_ref[...], b_ref[...])
    x1_ref[...] = x1
    h2 = x1 * (1.0 + mod[4:5]) + mod[3:4]
    h2_ref[...] = h2

    h_hi, h_lo = _split_hi_lo(h2)
    hw = _dot(h_hi, wr_ref[...])
    logits = hw[:, :128] + hw[:, 128:] + _dot(h_lo, wr_ref[:, :128]) + br_ref[...]
    lane = lax.broadcasted_iota(i32, logits.shape, 1).astype(f32)
    neg = jnp.float32(-jnp.inf)
    big = jnp.float32(1 << 20)
    gl = jnp.where(lane < N_GROUPS, logits, neg)
    gmax = jnp.max(gl, axis=-1, keepdims=True)
    gsel = jnp.min(jnp.where(gl == gmax, lane, big), axis=-1, keepdims=True)
    pg = 1.0 / jnp.sum(jnp.exp(gl - gmax), axis=-1, keepdims=True)
    e_lo = ROUTE_E0 + gsel * EXPERTS_PER_GROUP
    el = jnp.where((lane >= e_lo) & (lane < e_lo + EXPERTS_PER_GROUP), logits, neg)
    v1 = jnp.max(el, axis=-1, keepdims=True)
    i1 = jnp.min(jnp.where(el == v1, lane, big), axis=-1, keepdims=True)
    el2 = jnp.where(lane == i1, neg, el)
    v2 = jnp.max(el2, axis=-1, keepdims=True)
    i2 = jnp.min(jnp.where(el2 == v2, lane, big), axis=-1, keepdims=True)
    e2 = jnp.exp(v2 - v1)
    w1 = pg / (1.0 + e2)
    w2 = pg * e2 / (1.0 + e2)
    oh1 = lane == i1
    oh2 = lane == i2
    oh = jnp.where(oh1 | oh2, 1.0, 0.0)
    lrank = _dot(tril_ref[...], oh.astype(bf16))
    seg = jnp.floor((jnp.sum(oh, axis=0, keepdims=True) + (ROW_CHUNK - 1.0)) * (1.0 / ROW_CHUNK)) * ROW_CHUNK
    seg8 = jnp.broadcast_to(seg, (8, 128))
    lane8 = lax.broadcasted_iota(i32, (8, 128), 1)
    off8 = seg8
    for sh in (1, 2, 4, 8, 16):
        off8 = off8 + jnp.where(lane8 >= sh, pltpu.roll(off8, sh, 1), 0.0)
    off8 = off8 - seg8
    carry = carry_ref[...]
    lpos = lrank + off8[0:1, :]
    pick = lambda sel, val: jnp.sum(jnp.where(sel, val, 0.0), axis=-1, keepdims=True)
    sub8 = lax.broadcasted_iota(i32, (8, 128), 0)
    tab_ref[...] = jnp.where(sub8 == 0, seg8, jnp.where(sub8 == 1, off8, jnp.where(sub8 == 2, carry, 0.0)))
    carry = carry + seg8
    carry_ref[...] = carry
    cnt_ref[...] = carry
    cols = (i1 - ROUTE_E0, i2 - ROUTE_E0, w1, w2, pick(oh1, lpos), pick(oh2, lpos))
    route = jnp.zeros_like(logits)
    for j, col in enumerate(cols):
        route = jnp.where(lane == j, col, route)
    route_ref[...] = route


def _outproj(x_ctx, x_lat, mod, fo_ctx, fo_lat, po, so, ao_ctx, ao_lat, l, w):
    nt = T_ALL // TM
    tile = lambda wd: pl.BlockSpec((TM, wd), lambda i: (i, 0))
    vec = lambda wd: pl.BlockSpec((None, 1, wd), lambda i: (l, 0, 0))
    return pl.pallas_call(
        _outproj_kernel,
        grid=(nt,),
        in_specs=[
            _ctx_tile(D_MODEL), _lat_tile(D_MODEL),
            pl.BlockSpec((None, None, 6, D_MODEL), lambda i: (l, i // (SEG // TM), 0, 0)),
            _ctx_tile(FFT_W), _lat_tile(FFT_W), tile(POOL_W), tile(SGU_W),
            _ctx_tile(ATTN_W), _lat_tile(ATTN_W),
            pl.BlockSpec((None, D_MODEL, D_MODEL), lambda i: (l, 0, 0)),
            vec(D_MODEL), vec(D_MODEL),
            pl.BlockSpec((None, D_MODEL, 256), lambda i: (l, 0, 0)),
            vec(128),
            pl.BlockSpec((TM, TM), lambda i: (0, 0)),
        ],
        out_specs=[tile(D_MODEL), tile(D_MODEL), tile(128), pl.BlockSpec((8, 128), lambda i: (0, 0)),
                   pl.BlockSpec((None, 8, 128), lambda i: (i, 0, 0))],
        out_shape=[
            jax.ShapeDtypeStruct((T_ALL, D_MODEL), f32),
            jax.ShapeDtypeStruct((T_ALL, D_MODEL), f32),
            jax.ShapeDtypeStruct((T_ALL, 128), f32),
            jax.ShapeDtypeStruct((8, 128), f32),
            jax.ShapeDtypeStruct((nt, 8, 128), f32),
        ],
        scratch_shapes=[pltpu.VMEM((8, 128), f32)],
        compiler_params=_cparams(("arbitrary",)),
        name="outproj",
    )(x_ctx, x_lat, mod, fo_ctx, fo_lat, po, so, ao_ctx, ao_lat,
      w["w_out"], w["ln1_g"], w["ln1_b"], w["w_r"], w["b_r"],
      w["tril"])


def _plan_kernel(cnt_ref, meta_ref):
    lane = lax.broadcasted_iota(i32, (8, 128), 1)
    sub = lax.broadcasted_iota(i32, (8, 128), 0)
    cnt = cnt_ref[...]
    is_e = (lane >= ROUTE_E0) & (lane < ROUTE_E0 + N_EXPERTS)
    tiles = jnp.where(is_e, jnp.floor((cnt + (MOE_TM - 1.0)) * (1.0 / MOE_TM)), 0.0)
    cum = tiles
    for s in (1, 2, 4, 8, 16):
        cum = cum + jnp.where(lane >= s, pltpu.roll(cum, s, 1), 0.0)
    pstart = (cum - tiles) * MOE_TM
    nused = jnp.max(cum, axis=-1, keepdims=True)
    fill = jnp.where(is_e & (cnt != tiles * MOE_TM), pstart + (tiles - 1.0) * MOE_TM, -1.0)
    meta = jnp.where(sub == 0, cnt, jnp.where(sub == 1, nused, jnp.where(sub == 2, fill,
                     jnp.where(sub == 3, pstart, 0.0))))
    meta_ref[...] = meta.astype(i32)


def _plan(cnt):
    return pl.pallas_call(
        _plan_kernel,
        grid=(1,),
        in_specs=[pl.BlockSpec((8, 128), lambda i: (0, 0))],
        out_specs=pl.BlockSpec((8, 128), lambda i: (0, 0)),
        out_shape=jax.ShapeDtypeStruct((8, 128), i32),
        compiler_params=_cparams(("arbitrary",)),
        name="plan",
    )(cnt)


def _dispatch_kernel(nch_ref, off_ref, dst_ref, tot_ref, fill_ref, nused_ref, h_ref, route_ref, xs_ref,
                     sorted_ref, zero_ref, sem, fill_sem):
    i = pl.program_id(0)

    def tile_fill(row0):
        return pltpu.make_async_copy(zero_ref, xs_ref.at[pl.ds(pl.multiple_of(row0, MOE_TM), MOE_TM)],
                                     fill_sem)

    @pl.when(i == 0)
    def _():
        zero_ref[...] = jnp.zeros_like(zero_ref)

        def start(e, c):
            @pl.when(fill_ref[e] >= 0)
            def _():
                tile_fill(jnp.maximum(fill_ref[e], 0)).start()
            return c

        def wait(e, c):
            @pl.when(fill_ref[e] >= 0)
            def _():
                tile_fill(jnp.maximum(fill_ref[e], 0)).wait()
            return c

        def start_tail(t, c):
            tile_fill(t * MOE_TM).start()
            return c

        def wait_tail(t, c):
            tile_fill(t * MOE_TM).wait()
            return c

        lax.fori_loop(0, N_EXPERTS, start, 0)
        lax.fori_loop(nused_ref[0], MOE_NT, start_tail, 0)
        lax.fori_loop(0, N_EXPERTS, wait, 0)
        lax.fori_loop(nused_ref[0], MOE_NT, wait_tail, 0)

    rt = route_ref[...].T
    hb = h_ref[...].astype(bf16)
    slot = i % 2
    used_rows = tot_ref[i] * ROW_CHUNK

    def sort_rows(nrows):
        j = lax.broadcasted_iota(i32, (nrows, 1), 0).astype(f32)
        sel = jnp.where((j == rt[4:5, :]) | (j == rt[5:6, :]), 1.0, 0.0).astype(bf16)
        sorted_ref[slot, 0:nrows, :] = _dot(sel, hb).astype(bf16)

    @pl.when(used_rows <= DISP_ROWS_SHORT)
    def _():
        sort_rows(DISP_ROWS_SHORT)

    @pl.when(used_rows > DISP_ROWS_SHORT)
    def _():
        sort_rows(DISP_ROWS)

    def per_expert(e, c):
        idx = i * N_EXPERTS + e
        n, s0, d0 = nch_ref[idx], off_ref[idx], dst_ref[idx]
        b = 1
        while b <= TM // ROW_CHUNK:
            @pl.when((n & b) != 0)
            def _(b=b):
                r0 = (n & (b - 1)) * ROW_CHUNK
                rows = b * ROW_CHUNK
                pltpu.make_async_copy(
                    sorted_ref.at[slot, pl.ds(pl.multiple_of(s0 + r0, ROW_CHUNK), rows)],
                    xs_ref.at[pl.ds(pl.multiple_of(d0 + r0, ROW_CHUNK), rows)], sem.at[slot]).start()
            b *= 2
        return c

    lax.fori_loop(0, N_EXPERTS, per_expert, 0)

    def drain(tile, s):
        rows = tot_ref[tile] * ROW_CHUNK

        @pl.when(rows > 0)
        def _():
            pltpu.make_async_copy(sorted_ref.at[s, pl.ds(0, rows)], xs_ref.at[pl.ds(0, rows)],
                                  sem.at[s]).wait()

    @pl.when(i >= 1)
    def _():
        drain(i - 1, 1 - slot)

    @pl.when(i == pl.num_programs(0) - 1)
    def _():
        drain(i, slot)


def _dispatch(nch, off, dst, tot, fill, nused, h2, route):
    grid_spec = pltpu.PrefetchScalarGridSpec(
        num_scalar_prefetch=6,
        grid=(T_ALL // TM,),
        in_specs=[pl.BlockSpec((TM, D_MODEL), lambda i, *_: (i, 0)),
                  pl.BlockSpec((TM, 128), lambda i, *_: (i, 0))],
        out_specs=pl.BlockSpec(memory_space=pl.ANY),
        scratch_shapes=[pltpu.VMEM((2, DISP_ROWS, D_MODEL), bf16), pltpu.VMEM((MOE_TM, D_MODEL), bf16),
                        pltpu.SemaphoreType.DMA((2,)), pltpu.SemaphoreType.DMA(())],
    )
    return pl.pallas_call(
        _dispatch_kernel,
        grid_spec=grid_spec,
        out_shape=jax.ShapeDtypeStruct((MOE_NT * MOE_TM, D_MODEL), bf16),
        compiler_params=_cparams(("arbitrary",)),
        name="dispatch",
    )(nch, off, dst, tot, fill, nused, h2, route)


def _experts_kernel(cnt_ref, nused_ref, xs_ref, wg_hbm, wu_hbm, wd_hbm, ys_ref,
                    wg_f, wu_f, wd_f, wg_b, wu_b, wd_b, st, wsem, *, layer):
    i = pl.program_id(0)
    nused = nused_ref[0]
    NXT, NSLOT, LEFT, ROWS = 0, 1, 2, 3

    def w_copies(e, slot):
        return (pltpu.make_async_copy(wg_hbm.at[layer, e], wg_f.at[slot], wsem.at[slot, 0]),
                pltpu.make_async_copy(wu_hbm.at[layer, e], wu_f.at[slot], wsem.at[slot, 1]),
                pltpu.make_async_copy(wd_hbm.at[layer, e], wd_f.at[slot], wsem.at[slot, 2]))

    def next_nonempty(e):
        return lax.while_loop(
            lambda v: (v < N_EXPERTS) & (cnt_ref[jnp.minimum(v, N_EXPERTS - 1)] == 0),
            lambda v: v + 1, e)

    @pl.when(i == 0)
    def _():
        e0 = next_nonempty(jnp.int32(0))
        for c in w_copies(e0, 0):
            c.start()
        st[NXT] = e0
        st[NSLOT] = 0
        st[LEFT] = 0

    @pl.when(i < nused)
    def _():
        @pl.when(st[LEFT] == 0)
        def _():
            e = st[NXT]
            slot = st[NSLOT]
            for c in w_copies(e, slot):
                c.wait()
            e2 = next_nonempty(e + 1)

            @pl.when(e2 < N_EXPERTS)
            def _():
                for c in w_copies(e2, 1 - slot):
                    c.start()

            st[NXT] = e2
            st[NSLOT] = 1 - slot
            st[LEFT] = (cnt_ref[e] + (MOE_TM - 1)) // MOE_TM
            st[ROWS] = cnt_ref[e]
            wg_b[...] = wg_f[slot].astype(bf16)
            wu_b[...] = wu_f[slot].astype(bf16)
            wd_b[...] = wd_f[slot].astype(bf16)

        def ffn(rows):
            x = xs_ref[rows, :]
            hg = _dot(x, wg_b[...])
            hu = _dot(x, wu_b[...])
            act = (hg * jax.nn.sigmoid(hg)) * hu
            ys_ref[rows, :] = _dot(act.astype(bf16), wd_b[...]).astype(bf16)

        half = MOE_TM // 2
        short = st[ROWS] <= half

        @pl.when(short)
        def _():
            ffn(slice(0, half))
            ys_ref[half:, :] = jnp.zeros((MOE_TM - half, D_MODEL), bf16)

        @pl.when(jnp.logical_not(short))
        def _():
            ffn(slice(0, MOE_TM))

        st[LEFT] = st[LEFT] - 1
        st[ROWS] = st[ROWS] - MOE_TM

    @pl.when(i >= nused)
    def _():
        ys_ref[...] = jnp.zeros_like(ys_ref)


def _experts(counts, nused, xs, l, w_gate, w_up, w_down):
    hbm = pl.BlockSpec(memory_space=pl.ANY)
    grid_spec = pltpu.PrefetchScalarGridSpec(
        num_scalar_prefetch=2,
        grid=(MOE_NT,),
        in_specs=[pl.BlockSpec((MOE_TM, D_MODEL), lambda i, c, nu: (jnp.minimum(i, nu[0] - 1), 0)),
                  hbm, hbm, hbm],
        out_specs=pl.BlockSpec((MOE_TM, D_MODEL), lambda i, c, nu: (i, 0)),
        scratch_shapes=[
            pltpu.VMEM((2, D_MODEL, EXPERT_FF), f32),
            pltpu.VMEM((2, D_MODEL, EXPERT_FF), f32),
            pltpu.VMEM((2, EXPERT_FF, D_MODEL), f32),
            pltpu.VMEM((D_MODEL, EXPERT_FF), bf16),
            pltpu.VMEM((D_MODEL, EXPERT_FF), bf16),
            pltpu.VMEM((EXPERT_FF, D_MODEL), bf16),
            pltpu.SMEM((4,), i32),
            pltpu.SemaphoreType.DMA((2, 3)),
        ],
    )
    return pl.pallas_call(
        functools.partial(_experts_kernel, layer=l),
        grid_spec=grid_spec,
        out_shape=jax.ShapeDtypeStruct((MOE_NT * MOE_TM, D_MODEL), bf16),
        compiler_params=_cparams(("arbitrary",)),
        name="experts",
    )(counts, nused, xs, w_gate, w_up, w_down)


def _combine_kernel(nch_ref, off_ref, dst_ref, tot_ref, x1_ref, mod_ref, route_ref, ys_hbm, g_ref, b_ref,
                    oc_ref, ol_ref, ybuf, moe_ref, sem):
    i = pl.program_id(0)
    nt = pl.num_programs(0) - 1

    @pl.when(i == 0)
    def _():
        ybuf[...] = jnp.zeros_like(ybuf)

    @pl.when(i < nt)
    def _():
        slot = i % 2

        def per_expert(e, c):
            idx = i * N_EXPERTS + e
            n, s0, d0 = nch_ref[idx], off_ref[idx], dst_ref[idx]
            b = 1
            while b <= TM // ROW_CHUNK:
                @pl.when((n & b) != 0)
                def _(b=b):
                    r0 = (n & (b - 1)) * ROW_CHUNK
                    rows = b * ROW_CHUNK
                    pltpu.make_async_copy(
                        ys_hbm.at[pl.ds(pl.multiple_of(d0 + r0, ROW_CHUNK), rows)],
                        ybuf.at[slot, pl.ds(pl.multiple_of(s0 + r0, ROW_CHUNK), rows)],
                        sem.at[slot]).start()
                b *= 2
            return c

        lax.fori_loop(0, N_EXPERTS, per_expert, 0)

    @pl.when(i >= 1)
    def _():
        slot = (i - 1) % 2
        rows = tot_ref[i - 1] * ROW_CHUNK

        @pl.when(rows > 0)
        def _():
            pltpu.make_async_copy(ys_hbm.at[pl.ds(0, rows)], ybuf.at[slot, pl.ds(0, rows)],
                                  sem.at[slot]).wait()

        route = route_ref[...]
        mod = mod_ref[...]

        def unsort(nrows):
            lane = lax.broadcasted_iota(i32, (1, nrows), 1).astype(f32)
            wmat = (jnp.where(lane == route[:, 4:5], route[:, 2:3], 0.0)
                    + jnp.where(lane == route[:, 5:6], route[:, 3:4], 0.0))
            moe_ref[...] = _dot(wmat.astype(bf16), ybuf[slot, 0:nrows, :])

        @pl.when(rows <= DISP_ROWS_SHORT)
        def _():
            unsort(DISP_ROWS_SHORT)

        @pl.when(rows > DISP_ROWS_SHORT)
        def _():
            unsort(DISP_ROWS)

        y = _layer_norm(DEEPNORM_ALPHA * x1_ref[...] + mod[5:6] * moe_ref[...], g_ref[...], b_ref[...])

        @pl.when(i - 1 < T_CTX // TM)
        def _():
            oc_ref[...] = y

        @pl.when(i - 1 >= T_CTX // TM)
        def _():
            ol_ref[...] = y


def _combine(nch, off, dst, tot, x1, mod, route, ys, l, w):
    nt = T_ALL // TM
    vec = pl.BlockSpec((None, 1, D_MODEL), lambda i, *_: (l, 0, 0))
    nctx = T_CTX // TM
    prev = lambda i: jnp.maximum(i - 1, 0)
    grid_spec = pltpu.PrefetchScalarGridSpec(
        num_scalar_prefetch=4,
        grid=(nt + 1,),
        in_specs=[
            pl.BlockSpec((TM, D_MODEL), lambda i, *_: (prev(i), 0)),
            pl.BlockSpec((None, None, 6, D_MODEL), lambda i, *_: (l, prev(i) // (SEG // TM), 0, 0)),
            pl.BlockSpec((TM, 128), lambda i, *_: (prev(i), 0)),
            pl.BlockSpec(memory_space=pl.ANY),
            vec, vec,
        ],
        out_specs=[pl.BlockSpec((TM, D_MODEL), lambda i, *_: (jnp.minimum(prev(i), nctx - 1), 0)),
                   pl.BlockSpec((TM, D_MODEL), lambda i, *_: (jnp.maximum(prev(i) - nctx, 0), 0))],
        scratch_shapes=[pltpu.VMEM((2, DISP_ROWS, D_MODEL), bf16), pltpu.VMEM((TM, D_MODEL), f32),
                        pltpu.SemaphoreType.DMA((2,))],
    )
    return pl.pallas_call(
        _combine_kernel,
        grid_spec=grid_spec,
        out_shape=[jax.ShapeDtypeStruct((T_CTX, D_MODEL), f32),
                   jax.ShapeDtypeStruct((T_LAT, D_MODEL), f32)],
        compiler_params=_cparams(("arbitrary",)),
        name="combine",
    )(nch, off, dst, tot, x1, mod, route, ys, w["ln2_g"], w["ln2_b"])


def _dft_cos_sin(n, scale):
    k = jnp.arange(n, dtype=i32)
    ang = ((k[:, None] * k[None, :]) % n).astype(f32) * np.float32(2.0 * np.pi / n)
    return jnp.cos(ang) * scale, jnp.sin(ang) * scale


def _seq_dft_matrix(n):
    g = min(DFT_SPLIT, n)
    j = jnp.arange(n, dtype=i32)[None, :]
    k1 = jnp.arange(n // g, dtype=i32)[:, None]
    k2 = jnp.arange(g, dtype=i32)[:, None]
    ang_a = ((k1 * j) % (n // g)).astype(f32) * np.float32(2.0 * np.pi * g / n)
    ang_b = ((k2 * j) % n).astype(f32) * np.float32(2.0 * np.pi / n)
    scale = np.float32(n ** -0.5)
    ca, sa = jnp.cos(ang_a), jnp.sin(ang_a)
    cb, sb = jnp.cos(ang_b) * scale, jnp.sin(ang_b) * scale
    ca2 = jnp.concatenate([ca, ca], axis=1)[:, None, :]
    sa2 = jnp.concatenate([sa, sa], axis=1)[:, None, :]
    cb2 = jnp.concatenate([cb, -sb], axis=1)[None, :, :]
    sb2 = jnp.concatenate([sb, cb], axis=1)[None, :, :]
    return (ca2 * cb2 - sa2 * sb2).astype(bf16).reshape(n, 2 * n)


def _rope_tables():
    rows = DEC_SEQ // GRID_W
    row = jnp.repeat(jnp.arange(rows), GRID_W).astype(f32)
    col = jnp.tile(jnp.arange(GRID_W), rows).astype(f32)
    n_freq = HEAD_DIM // 4
    inv = ROPE_THETA ** (-jnp.arange(n_freq, dtype=f32) / n_freq)
    ar = row[:, None] * inv
    ac = col[:, None] * inv
    ang = jnp.concatenate([ar, ar, ac, ac], axis=-1)
    cos = jnp.tile(jnp.cos(ang), (1, N_HEADS))
    sin = jnp.tile(jnp.sin(ang), (1, N_HEADS))
    first = (jnp.arange(ATTN_W) % (HEAD_DIM // 2)) < n_freq
    sin_a = jnp.where(first[None, :], -sin, 0.0)
    sin_b = jnp.where(first[None, :], 0.0, sin)
    ident = jnp.zeros((TM, ATTN_W), f32)
    return (jnp.concatenate([cos, ident + 1.0], axis=0),
            jnp.concatenate([sin_a, ident], axis=0),
            jnp.concatenate([sin_b, ident], axis=0))


def _dup_cache(cache):
    c = jnp.transpose(cache, (1, 3, 0, 2, 4))
    return jnp.concatenate([c, c], axis=-1).astype(bf16)


def kernel(x_prompt, x_sample, cache_k, cache_v, c, c_ctx, w_mod, b_mod, w_in, w_fft, w_pool, pool_scale, sgu_ln_g, sgu_ln_b, w_sgu, b_sgu, q_norm_g, k_norm_g, w_out, ln1_g, ln1_b, w_router_group, b_router_group, w_router_expert, b_router_expert, w_gate, w_up, w_down, ln2_g, ln2_b):
    L = DEPTH
    x_ctx = x_prompt.reshape(T_CTX, D_MODEL)
    x_lat = x_sample.reshape(T_LAT, D_MODEL)

    cond8 = jnp.concatenate([c_ctx[None, :], c, jnp.zeros((8 - 1 - DEC_BATCH, D_MODEL), f32)], axis=0)
    mod = _modulation(cond8, w_mod, b_mod)[:, :N_SEG].reshape(L, N_SEG, 6, D_MODEL)

    cc, sc = _dft_cos_sin(FFT_W, np.float32(FFT_W ** -0.5))
    rope_cos, rope_sin_a, rope_sin_b = _rope_tables()
    head_id = jnp.arange(ATTN_W) // HEAD_DIM
    eye_g = jnp.eye(len(POOL_WINDOWS), dtype=f32)
    w_r = jnp.zeros((L, D_MODEL, 128), f32)
    w_r = w_r.at[:, :, :N_GROUPS].set(w_router_group).at[:, :, ROUTE_E0:ROUTE_E0 + N_EXPERTS].set(w_router_expert)
    b_r = jnp.zeros((L, 1, 128), f32)
    b_r = b_r.at[:, 0, :N_GROUPS].set(b_router_group).at[:, 0, ROUTE_E0:ROUTE_E0 + N_EXPERTS].set(b_router_expert)
    w_r_hi, w_r_lo = _split_hi_lo(w_r)
    w = {
        "w_in": w_in.astype(bf16),
        "csc": jnp.concatenate([cc, sc], axis=1).astype(bf16),
        "w_sgu": jnp.transpose(w_sgu, (0, 2, 1, 3)).reshape(L, CHUNK, SGU_HEADS * CHUNK).astype(bf16),
        "b_sgu": jnp.repeat(jnp.transpose(b_sgu, (0, 2, 1)), SGU_W // SGU_HEADS, axis=2),
        "sgu_ln_g": sgu_ln_g.reshape(L, 1, SGU_W),
        "sgu_ln_b": sgu_ln_b.reshape(L, 1, SGU_W),
        "q_norm_g": jnp.tile(q_norm_g, (1, N_HEADS)).reshape(L, 1, ATTN_W),
        "k_norm_g": jnp.tile(k_norm_g, (1, N_KV_HEADS)).reshape(L, 1, KV_W),
        "rope_cos": rope_cos, "rope_sin_a": rope_sin_a, "rope_sin_b": rope_sin_b,
        "ones_bd": (head_id[:, None] == head_id[None, :]).astype(bf16),
        "w_pool_bd": jnp.einsum("lgcd,gh->lgchd", w_pool, eye_g).reshape(L, POOL_W, POOL_W).astype(bf16),
        "pool_scale": pool_scale.reshape(L, 1, POOL_W),
        "w_fft": w_fft.astype(bf16),
        "w_out": w_out.astype(bf16),
        "ln1_g": ln1_g.reshape(L, 1, D_MODEL), "ln1_b": ln1_b.reshape(L, 1, D_MODEL),
        "ln2_g": ln2_g.reshape(L, 1, D_MODEL), "ln2_b": ln2_b.reshape(L, 1, D_MODEL),
        "w_r": jnp.concatenate([w_r_hi, w_r_lo], axis=-1), "b_r": b_r,
        "tril": (jnp.arange(TM)[:, None] > jnp.arange(TM)[None, :]).astype(bf16),
    }
    m_ctx = _seq_dft_matrix(SEQ)
    m_lat = _seq_dft_matrix(DEC_SEQ)
    kc_all = _dup_cache(cache_k)
    vc_all = _dup_cache(cache_v)

    new_k, new_v = [], []
    for l in range(L):
        pq, praw, sgu, q, kd, vd, nk, nv = _inproj(x_ctx, x_lat, mod, l, w)
        new_k.append(nk[:T_CTX].reshape(BATCH, SEQ, N_KV_HEADS, HEAD_DIM))
        new_v.append(nv[:T_CTX].reshape(BATCH, SEQ, N_KV_HEADS, HEAD_DIM))
        po = _pool(praw, l, w)
        fo_ctx = _seqdft(pq, m_ctx, l, w, n=SEQ, tr=SEQ, nseq=BATCH, row0=0)
        fo_lat = _seqdft(pq, m_lat, l, w, n=DEC_SEQ, tr=FFT_TR, nseq=DEC_BATCH, row0=T_CTX)
        ao_ctx = _attention(q, kd, vd, None, n=SEQ, tq=SEQ, nseq=BATCH, row0=0, heads=N_KV_HEADS)
        ao_lat = _attention(q, kd, vd, (kc_all[l], vc_all[l]), n=DEC_SEQ, tq=ATT_TQ, nseq=DEC_BATCH,
                            row0=T_CTX, heads=1)
        x1, h2, route, cnt, tab = _outproj(x_ctx, x_lat, mod, fo_ctx, fo_lat, po, sgu, ao_ctx, ao_lat, l, w)
        meta = _plan(cnt)
        experts = slice(ROUTE_E0, ROUTE_E0 + N_EXPERTS)
        counts = meta[0, experts]
        nused = meta[1, :1]
        fill = meta[2, experts]
        tab = tab[:, :, experts].astype(i32)
        nch = (tab[:, 0] // ROW_CHUNK).reshape(-1)
        off = tab[:, 1].reshape(-1)
        dst = (meta[3, experts][None, :] + tab[:, 2]).reshape(-1)
        tot = jnp.sum(tab[:, 0], axis=1) // ROW_CHUNK
        xs = _dispatch(nch, off, dst, tot, fill, nused, h2, route)
        ys = _experts(counts, nused, xs, l, w_gate, w_up, w_down)
        x_ctx, x_lat = _combine(nch, off, dst, tot, x1, mod, route, ys, l, w)

    y_prompt = x_ctx.reshape(BATCH, SEQ, D_MODEL)
    y_sample = x_lat.reshape(DEC_BATCH, DEC_SEQ, D_MODEL)
    return (y_prompt, y_sample, jnp.stack(new_k, axis=1), jnp.stack(new_v, axis=1))
```

```python
import functools

import numpy as np
import jax
import jax.numpy as jnp
from jax import lax
from jax.experimental import pallas as pl
from jax.experimental.pallas import tpu as pltpu

f32 = jnp.float32
bf16 = jnp.bfloat16
i32 = jnp.int32

D_MODEL = 1024
BATCH = 16
SEQ = 256
DEPTH = 4
DEC_BATCH = 2
DEC_SEQ = 4096
PAST_LEN = 512
GRID_W = 64
FFT_W = 256
POOL_W = 256
POOL_WINDOWS = (2, 4, 8, 16)
POOL_GROUP = 64
SGU_W = 256
SGU_HEADS = 4
CHUNK = 128
HEAD_DIM = 64
ATTN_W = 256
N_HEADS = 4
N_KV_HEADS = 2
KV_W = 128
IN_W = 1536
ROPE_THETA = 10000.0
N_GROUPS = 4
EXPERTS_PER_GROUP = 8
N_EXPERTS = 32
EXPERT_FF = 512
DEEPNORM_ALPHA = float((2 * DEPTH) ** 0.25)
LN_EPS = 1e-5
RMS_EPS = 1e-6

T_CTX = BATCH * SEQ
T_LAT = DEC_BATCH * DEC_SEQ
T_ALL = T_CTX + T_LAT
SEG = 4096
N_SEG = T_ALL // SEG

TM = 512
POOL_TB = 1024
POOL_HALO = 8
FFT_TR = 512
ATT_TQ = 512
ATT_CHUNK = 1024
DFT_SPLIT = 64
MOE_TM = 256
ROW_CHUNK = 16
MOE_ROWS = 2 * T_ALL
MOE_PAD_ROWS = (T_ALL // TM) * N_EXPERTS * (ROW_CHUNK - 1)
MOE_NT = -(-(MOE_ROWS + MOE_PAD_ROWS) // MOE_TM) + N_EXPERTS
DISP_ROWS = 2 * TM + N_EXPERTS * ROW_CHUNK
DISP_ROWS_SHORT = 2 * TM + N_EXPERTS * ROW_CHUNK // 2
ROUTE_E0 = 32
VMEM_LIMIT = 56 * 1024 * 1024


def _cparams(sem):
    return pltpu.CompilerParams(dimension_semantics=sem, vmem_limit_bytes=VMEM_LIMIT)


def _split_hi_lo(a):
    hi = a.astype(bf16)
    lo = (a - hi.astype(f32)).astype(bf16)
    return hi, lo


def _dot(a, b):
    return jnp.dot(a, b, preferred_element_type=f32)


def _mod_kernel(c_ref, w_ref, b_ref, o_ref):
    c = c_ref[...]
    s = c * jax.nn.sigmoid(c)
    s_hi, s_lo = _split_hi_lo(s)
    w_hi, w_lo = _split_hi_lo(w_ref[...])
    o_ref[...] = _dot(s_hi, w_hi) + _dot(s_hi, w_lo) + _dot(s_lo, w_hi) + b_ref[...]


def _modulation(cond8, w_mod, b_mod):
    tn = 1536
    return pl.pallas_call(
        _mod_kernel,
        grid=(DEPTH, 6 * D_MODEL // tn),
        in_specs=[
            pl.BlockSpec((8, D_MODEL), lambda l, j: (0, 0)),
            pl.BlockSpec((None, D_MODEL, tn), lambda l, j: (l, 0, j)),
            pl.BlockSpec((None, 1, tn), lambda l, j: (l, 0, j)),
        ],
        out_specs=pl.BlockSpec((None, 8, tn), lambda l, j: (l, 0, j)),
        out_shape=jax.ShapeDtypeStruct((DEPTH, 8, 6 * D_MODEL), f32),
        compiler_params=_cparams(("arbitrary", "arbitrary")),
        name="modulation",
    )(cond8, w_mod, b_mod.reshape(DEPTH, 1, 6 * D_MODEL))


def _head_rms(x, ones_bd, gain):
    ss = _dot((x * x).astype(bf16), ones_bd)
    return x * lax.rsqrt(ss * (1.0 / HEAD_DIM) + RMS_EPS) * gain


def _rope(x, cos, sin_a, sin_b):
    w = x.shape[-1]
    q4 = HEAD_DIM // 4
    return x * cos + pltpu.roll(x, w - q4, 1) * sin_a + pltpu.roll(x, q4, 1) * sin_b


def _dup_half(x, first):
    lane = lax.broadcasted_iota(i32, x.shape, 1)
    r = pltpu.roll(x, HEAD_DIM, 1)
    if first:
        return jnp.where(lane < HEAD_DIM, x, r)
    return jnp.where(lane >= HEAD_DIM, x, r)


def _gelu_tanh(x):
    c = np.sqrt(2.0 / np.pi).astype(np.float32)
    return x * (0.5 * (1.0 + jnp.tanh(c * (x + 0.044715 * (x * x * x)))))


def _inproj_kernel(xc_ref, xl_ref, mod_ref, win_ref, csc_ref, wsgu_ref, bsgu_ref, lng_ref, lnb_ref,
                   qg_ref, kg_ref, cos_ref, sina_ref, sinb_ref, ones_ref,
                   pq_ref, pool_ref, sgu_ref, q_ref, kd_ref, vd_ref, nk_ref, nv_ref):
    x = jnp.where(pl.program_id(0) < T_CTX // TM, xc_ref[...], xl_ref[...])
    mod = mod_ref[...]
    h = (x * (1.0 + mod[1:2]) + mod[0:1]).astype(bf16)
    proj = _dot(h, win_ref[...])

    a = proj[:, 0:FFT_W].astype(bf16)
    pq_ref[...] = _dot(a, csc_ref[...]).astype(bf16)

    pool_ref[...] = proj[:, FFT_W:FFT_W + POOL_W]

    o = FFT_W + POOL_W
    hgu = _gelu_tanh(proj[:, o:o + 2 * SGU_W])
    u = hgu[:, :SGU_W]
    v = hgu[:, SGU_W:]
    mu = jnp.mean(v, axis=-1, keepdims=True)
    vc = v - mu
    var = jnp.mean(vc * vc, axis=-1, keepdims=True)
    v = vc * lax.rsqrt(var + LN_EPS) * lng_ref[...] + lnb_ref[...]
    lane = lax.broadcasted_iota(i32, (CHUNK, SGU_W), 1)
    head = lane // (SGU_W // SGU_HEADS)
    wcat = wsgu_ref[...]
    for cidx in range(TM // CHUNK):
        rows = slice(cidx * CHUNK, (cidx + 1) * CHUNK)
        vch = v[rows]
        vblk = jnp.concatenate(
            [jnp.where(head == g, vch, 0.0) for g in range(SGU_HEADS)], axis=0).astype(bf16)
        sp = _dot(wcat, vblk) + bsgu_ref[...]
        sgu_ref[rows, :] = (u[rows] * sp).astype(bf16)

    o = o + 2 * SGU_W
    ones_bd = ones_ref[...]
    cos = cos_ref[...]
    sin_a = sina_ref[...]
    sin_b = sinb_ref[...]
    q = _head_rms(proj[:, o:o + ATTN_W], ones_bd, qg_ref[...])
    q = _rope(q, cos, sin_a, sin_b) * np.float32(HEAD_DIM ** -0.5 * np.log2(np.e))
    q_ref[...] = q.astype(bf16)
    o = o + ATTN_W
    k = _head_rms(proj[:, o:o + KV_W], ones_bd[:KV_W, :KV_W], kg_ref[...])
    nk_ref[...] = k
    k = _rope(k, cos[:, :KV_W], sin_a[:, :KV_W], sin_b[:, :KV_W])
    kd_ref[0] = _dup_half(k, True).astype(bf16)
    kd_ref[1] = _dup_half(k, False).astype(bf16)
    o = o + KV_W
    vv = proj[:, o:o + KV_W]
    nv_ref[...] = vv
    vd_ref[0] = _dup_half(vv, True).astype(bf16)
    vd_ref[1] = _dup_half(vv, False).astype(bf16)


def _rope_block(i):
    nlat = DEC_SEQ // TM
    nctx = T_CTX // TM
    return jnp.where(i < nctx, nlat, (i - nctx) % nlat)


def _ctx_tile(wd):
    return pl.BlockSpec((TM, wd), lambda i, *_: (jnp.minimum(i, T_CTX // TM - 1), 0))


def _lat_tile(wd):
    return pl.BlockSpec((TM, wd), lambda i, *_: (jnp.maximum(i - T_CTX // TM, 0), 0))


def _inproj(x_ctx, x_lat, mod, l, w):
    nt = T_ALL // TM
    tile = lambda wd: pl.BlockSpec((TM, wd), lambda i: (i, 0))
    const = lambda shape: pl.BlockSpec(shape, lambda i: (0,) * len(shape))
    rope_spec = pl.BlockSpec((TM, ATTN_W), lambda i: (_rope_block(i), 0))
    return pl.pallas_call(
        _inproj_kernel,
        grid=(nt,),
        in_specs=[
            _ctx_tile(D_MODEL), _lat_tile(D_MODEL),
            pl.BlockSpec((None, None, 6, D_MODEL), lambda i: (l, i // (SEG // TM), 0, 0)),
            pl.BlockSpec((None, D_MODEL, IN_W), lambda i: (l, 0, 0)),
            const((FFT_W, 2 * FFT_W)),
            pl.BlockSpec((None, CHUNK, SGU_HEADS * CHUNK), lambda i: (l, 0, 0)),
            pl.BlockSpec((None, CHUNK, SGU_W), lambda i: (l, 0, 0)),
            pl.BlockSpec((None, 1, SGU_W), lambda i: (l, 0, 0)),
            pl.BlockSpec((None, 1, SGU_W), lambda i: (l, 0, 0)),
            pl.BlockSpec((None, 1, ATTN_W), lambda i: (l, 0, 0)),
            pl.BlockSpec((None, 1, KV_W), lambda i: (l, 0, 0)),
            rope_spec, rope_spec, rope_spec,
            const((ATTN_W, ATTN_W)),
        ],
        out_specs=[
            tile(2 * FFT_W), tile(POOL_W), tile(SGU_W), tile(ATTN_W),
            pl.BlockSpec((N_KV_HEADS, TM, KV_W), lambda i: (0, i, 0)),
            pl.BlockSpec((N_KV_HEADS, TM, KV_W), lambda i: (0, i, 0)),
            tile(KV_W), tile(KV_W),
        ],
        out_shape=[
            jax.ShapeDtypeStruct((T_ALL, 2 * FFT_W), bf16),
            jax.ShapeDtypeStruct((T_ALL, POOL_W), f32),
            jax.ShapeDtypeStruct((T_ALL, SGU_W), bf16),
            jax.ShapeDtypeStruct((T_ALL, ATTN_W), bf16),
            jax.ShapeDtypeStruct((N_KV_HEADS, T_ALL, KV_W), bf16),
            jax.ShapeDtypeStruct((N_KV_HEADS, T_ALL, KV_W), bf16),
            jax.ShapeDtypeStruct((T_ALL, KV_W), f32),
            jax.ShapeDtypeStruct((T_ALL, KV_W), f32),
        ],
        compiler_params=_cparams(("arbitrary",)),
        name="inproj",
    )(x_ctx, x_lat, mod, w["w_in"], w["csc"], w["w_sgu"], w["b_sgu"], w["sgu_ln_g"], w["sgu_ln_b"],
      w["q_norm_g"], w["k_norm_g"], w["rope_cos"], w["rope_sin_a"], w["rope_sin_b"], w["ones_bd"])


def _pool_kernel(prev_ref, cur_ref, next_ref, wp_ref, scale_ref, o_ref):
    i = pl.program_id(0)
    n = jnp.where(i < T_CTX // POOL_TB, SEQ, DEC_SEQ)
    hl = POOL_HALO
    ext = jnp.concatenate([prev_ref[POOL_TB - hl:, :], cur_ref[...], next_ref[:hl, :]], axis=0)
    rows = POOL_TB + 2 * hl
    r = lax.broadcasted_iota(i32, (rows, 1), 0)
    pos = (i * POOL_TB + r - hl) & (n - 1)

    def back(a, s):
        return jnp.where(pos >= s, pltpu.roll(a, s, 0), 0.0)

    def fwd(a, s):
        return jnp.where(pos + s < n, pltpu.roll(a, rows - s, 0), 0.0)

    bsum = [back(ext, 1)]
    fsum = [ext]
    for k in range(3):
        s = 1 << k
        bsum.append(bsum[k] + back(bsum[k], s))
        fsum.append(fsum[k] + fwd(fsum[k], s))
    lane = lax.broadcasted_iota(i32, (1, POOL_W), 1)
    grp = lane // POOL_GROUP
    win = bsum[3] + fsum[3]
    half = jnp.full((1, POOL_W), POOL_WINDOWS[3] // 2, i32)
    for g in (2, 1, 0):
        win = jnp.where(grp == g, bsum[g] + fsum[g], win)
        half = jnp.where(grp == g, POOL_WINDOWS[g] // 2, half)
    cnt = (jnp.minimum(pos + half, n) - jnp.maximum(pos - half, 0)).astype(f32)
    y = (win / cnt - ext)[hl:hl + POOL_TB]
    o_ref[...] = (_dot(y.astype(bf16), wp_ref[...]) * scale_ref[...]).astype(bf16)


def _pool(p, l, w):
    nt = T_ALL // POOL_TB
    blk = lambda f: pl.BlockSpec((POOL_TB, POOL_W), lambda i: (f(i), 0))
    return pl.pallas_call(
        _pool_kernel,
        grid=(nt,),
        in_specs=[
            blk(lambda i: jnp.maximum(i - 1, 0)), blk(lambda i: i),
            blk(lambda i: jnp.minimum(i + 1, nt - 1)),
            pl.BlockSpec((None, POOL_W, POOL_W), lambda i: (l, 0, 0)),
            pl.BlockSpec((None, 1, POOL_W), lambda i: (l, 0, 0)),
        ],
        out_specs=blk(lambda i: i),
        out_shape=jax.ShapeDtypeStruct((T_ALL, POOL_W), bf16),
        compiler_params=_cparams(("arbitrary",)),
        name="pool",
    )(p, p, p, w["w_pool_bd"], w["pool_scale"])


def _seqdft_kernel(*refs, n, nseq):
    m_ref, pq_refs, w_ref, o_ref = refs[0], refs[1:-2], refs[-2], refs[-1]
    per_blk = SEG // n
    tr = m_ref.shape[0]

    def one_sequence(b, rows):
        pq_ref = pq_refs[b // per_blk]
        r0 = (b % per_blk) * n
        f = (_dot(m_ref[:, :n], pq_ref[r0:r0 + n, :FFT_W])
             + _dot(m_ref[:, n:], pq_ref[r0:r0 + n, FFT_W:]))
        o_ref[rows, :] = _dot(f.astype(bf16), w_ref[...]).astype(bf16)

    if tr == n:
        for b in range(nseq):
            one_sequence(b, slice(b * n, (b + 1) * n))
    else:
        for b in range(nseq):
            @pl.when(pl.program_id(1) == b)
            def _(b=b):
                one_sequence(b, slice(0, tr))


def _seqdft(pq, m, l, w, *, n, tr, nseq, row0):
    nr = n // tr
    nblk = nseq * n // SEG
    pq_specs = [pl.BlockSpec((SEG, 2 * FFT_W), lambda i, b, j=j: (row0 // SEG + j, 0))
                for j in range(nblk)]
    if nr == 1:
        grid, out_spec = (1, 1), pl.BlockSpec((nseq * n, FFT_W), lambda i, b: (0, 0))
    else:
        grid, out_spec = (nr, nseq), pl.BlockSpec((tr, FFT_W), lambda i, b: (b * nr + i, 0))
    return pl.pallas_call(
        functools.partial(_seqdft_kernel, n=n, nseq=nseq),
        grid=grid,
        in_specs=[pl.BlockSpec((tr, 2 * n), lambda i, b: (i, 0))] + pq_specs
        + [pl.BlockSpec((None, FFT_W, FFT_W), lambda i, b: (l, 0, 0))],
        out_specs=out_spec,
        out_shape=jax.ShapeDtypeStruct((nseq * n, FFT_W), bf16),
        compiler_params=_cparams(("arbitrary", "arbitrary")),
        name="seqdft_%d" % n,
    )(m, *([pq] * nblk), w["w_fft"])


def _attn_kernel(*refs, has_cache):
    if has_cache:
        q_ref, k_ref, v_ref, kc_ref, vc_ref, o_ref = refs
    else:
        q_ref, k_ref, v_ref, o_ref = refs
    pair = 2 * HEAD_DIM
    tq = q_ref.shape[0]
    n = k_ref.shape[1]
    nt = (((1,), (1,)), ((), ()))
    chunk = min(n, ATT_CHUNK)
    lane = lax.broadcasted_iota(i32, (tq, pair), 1)
    for h in range(k_ref.shape[0]):
        q = q_ref[:, h * pair:(h + 1) * pair]
        zero = jnp.zeros_like(q)
        qs = jnp.concatenate([jnp.where(lane < HEAD_DIM, q, zero),
                              jnp.where(lane >= HEAD_DIM, q, zero)], axis=0)
        parts = [(k_ref, v_ref, c * chunk, chunk) for c in range(n // chunk)]
        if has_cache:
            parts = [(kc_ref, vc_ref, 0, PAST_LEN)] + parts
        m = jnp.full((2 * tq, 1), -jnp.inf, f32)
        den = jnp.zeros((2 * tq, 1), f32)
        acc = jnp.zeros((2 * tq, pair), f32)
        for kr, vr, off, size in parts:
            s = lax.dot_general(qs, kr[h, off:off + size, :], nt, preferred_element_type=f32)
            m_new = jnp.maximum(m, jnp.max(s, axis=-1, keepdims=True))
            alpha = jnp.exp2(m - m_new)
            p = jnp.exp2(s - m_new).astype(bf16)
            den = alpha * den + jnp.sum(p.astype(f32), axis=-1, keepdims=True)
            acc = alpha * acc + _dot(p, vr[h, off:off + size, :])
            m = m_new
        out = acc / den
        o_ref[:, h * pair:(h + 1) * pair] = jnp.where(lane < HEAD_DIM, out[:tq], out[tq:]).astype(bf16)


def _attention(q, kd, vd, cache, *, n, tq, nseq, row0, heads):
    nq = n // tq
    b0 = row0 // n
    q0 = row0 // tq
    in_specs = [
        pl.BlockSpec((tq, heads * 2 * HEAD_DIM), lambda b, h, i: (q0 + b * nq + i, h)),
        pl.BlockSpec((heads, n, KV_W), lambda b, h, i: (h, b0 + b, 0)),
        pl.BlockSpec((heads, n, KV_W), lambda b, h, i: (h, b0 + b, 0)),
    ]
    args = [q, kd, vd]
    if cache is not None:
        cspec = pl.BlockSpec((heads, None, PAST_LEN, KV_W), lambda b, h, i: (h, b, 0, 0))
        in_specs += [cspec, cspec]
        args += list(cache)
    return pl.pallas_call(
        functools.partial(_attn_kernel, has_cache=cache is not None),
        grid=(nseq, N_KV_HEADS // heads, nq),
        in_specs=in_specs,
        out_specs=pl.BlockSpec((tq, heads * 2 * HEAD_DIM), lambda b, h, i: (b * nq + i, h)),
        out_shape=jax.ShapeDtypeStruct((nseq * n, ATTN_W), bf16),
        compiler_params=_cparams(("arbitrary", "arbitrary", "arbitrary")),
        name="attention_%d" % n,
    )(*args)


def _layer_norm(x, g, b):
    mu = jnp.mean(x, axis=-1, keepdims=True)
    xc = x - mu
    var = jnp.mean(xc * xc, axis=-1, keepdims=True)
    return xc * lax.rsqrt(var + LN_EPS) * g + b


def _outproj_kernel(xc_ref, xl_ref, mod_ref, fc_ref, fl_ref, p_ref, s_ref, ac_ref, al_ref, wout_ref,
                    g_ref, b_ref, wr_ref, br_ref, tril_ref,
                    x1_ref, h2_ref, route_ref, cnt_ref, tab_ref, carry_ref):
    i = pl.program_id(0)

    @pl.when(i == 0)
    def _():
        carry_ref[...] = jnp.zeros_like(carry_ref)

    mod = mod_ref[...]
    is_ctx = i < T_CTX // TM
    f_mix = jnp.where(is_ctx, fc_ref[...], fl_ref[...])
    a_mix = jnp.where(is_ctx, ac_ref[...], al_ref[...])
    mix = _dot(jnp.concatenate([f_mix, p_ref[...], s_ref[...], a_mix], axis=1), wout_ref[...])
    x = jnp.where(is_ctx, xc_ref[...], xl_ref[...])
    x1 = _layer_norm(DEEPNORM_ALPHA * x + mod[2:3] * mix, g_ref[...], b_ref[...])
    x1_ref[...] = x1
    h2 = x1 * (1.0 + mod[4:5]) + mod[3:4]
    h_hi, h_lo = _split_hi_lo(h2)
    h2_ref[...] = h_hi
    hw = _dot(h_hi, wr_ref[...])
    logits = hw[:, :128] + hw[:, 128:] + _dot(h_lo, wr_ref[:, :128]) + br_ref[...]
    lane = lax.broadcasted_iota(i32, logits.shape, 1).astype(f32)
    neg = jnp.float32(-jnp.inf)
    big = jnp.float32(1 << 20)
    gl = jnp.where(lane < N_GROUPS, logits, neg)
    gmax = jnp.max(gl, axis=-1, keepdims=True)
    gsel = jnp.min(jnp.where(gl == gmax, lane, big), axis=-1, keepdims=True)
    pg = 1.0 / jnp.sum(jnp.exp(gl - gmax), axis=-1, keepdims=True)
    e_lo = ROUTE_E0 + gsel * EXPERTS_PER_GROUP
    el = jnp.where((lane >= e_lo) & (lane < e_lo + EXPERTS_PER_GROUP), logits, neg)
    v1 = jnp.max(el, axis=-1, keepdims=True)
    i1 = jnp.min(jnp.where(el == v1, lane, big), axis=-1, keepdims=True)
    el2 = jnp.where(lane == i1, neg, el)
    v2 = jnp.max(el2, axis=-1, keepdims=True)
    i2 = jnp.min(jnp.where(el2 == v2, lane, big), axis=-1, keepdims=True)
    e2 = jnp.exp(v2 - v1)
    w1 = pg / (1.0 + e2)
    w2 = pg * e2 / (1.0 + e2)
    oh1 = lane == i1
    oh2 = lane == i2
    oh = jnp.where(oh1 | oh2, 1.0, 0.0)
    lrank = _dot(tril_ref[...], oh.astype(bf16))
    seg = jnp.floor((jnp.sum(oh, axis=0, keepdims=True) + (ROW_CHUNK - 1.0)) * (1.0 / ROW_CHUNK)) * ROW_CHUNK
    seg8 = jnp.broadcast_to(seg, (8, 128))
    lane8 = lax.broadcasted_iota(i32, (8, 128), 1)
    off8 = seg8
    for sh in (1, 2, 4, 8, 16):
        off8 = off8 + jnp.where(lane8 >= sh, pltpu.roll(off8, sh, 1), 0.0)
    off8 = off8 - seg8
    carry = carry_ref[...]
    lpos = lrank + off8[0:1, :]
    pick = lambda sel, val: jnp.sum(jnp.where(sel, val, 0.0), axis=-1, keepdims=True)
    sub8 = lax.broadcasted_iota(i32, (8, 128), 0)
    tab_ref[...] = jnp.where(sub8 == 0, seg8, jnp.where(sub8 == 1, off8, jnp.where(sub8 == 2, carry, 0.0)))
    carry = carry + seg8
    carry_ref[...] = carry
    cnt_ref[...] = carry
    cols = (i1 - ROUTE_E0, i2 - ROUTE_E0, w1, w2, pick(oh1, lpos), pick(oh2, lpos))
    route = jnp.zeros_like(logits)
    for j, col in enumerate(cols):
        route = jnp.where(lane == j, col, route)
    route_ref[...] = route


def _outproj(x_ctx, x_lat, mod, fo_ctx, fo_lat, po, so, ao_ctx, ao_lat, l, w):
    nt = T_ALL // TM
    tile = lambda wd: pl.BlockSpec((TM, wd), lambda i: (i, 0))
    vec = lambda wd: pl.BlockSpec((None, 1, wd), lambda i: (l, 0, 0))
    return pl.pallas_call(
        _outproj_kernel,
        grid=(nt,),
        in_specs=[
            _ctx_tile(D_MODEL), _lat_tile(D_MODEL),
            pl.BlockSpec((None, None, 6, D_MODEL), lambda i: (l, i // (SEG // TM), 0, 0)),
            _ctx_tile(FFT_W), _lat_tile(FFT_W), tile(POOL_W), tile(SGU_W),
            _ctx_tile(ATTN_W), _lat_tile(ATTN_W),
            pl.BlockSpec((None, D_MODEL, D_MODEL), lambda i: (l, 0, 0)),
            vec(D_MODEL), vec(D_MODEL),
            pl.BlockSpec((None, D_MODEL, 256), lambda i: (l, 0, 0)),
            vec(128),
            pl.BlockSpec((TM, TM), lambda i: (0, 0)),
        ],
        out_specs=[tile(D_MODEL), tile(D_MODEL), tile(128), pl.BlockSpec((8, 128), lambda i: (0, 0)),
                   pl.BlockSpec((None, 8, 128), lambda i: (i, 0, 0))],
        out_shape=[
            jax.ShapeDtypeStruct((T_ALL, D_MODEL), f32),
            jax.ShapeDtypeStruct((T_ALL, D_MODEL), bf16),
            jax.ShapeDtypeStruct((T_ALL, 128), f32),
            jax.ShapeDtypeStruct((8, 128), f32),
            jax.ShapeDtypeStruct((nt, 8, 128), f32),
        ],
        scratch_shapes=[pltpu.VMEM((8, 128), f32)],
        compiler_params=_cparams(("arbitrary",)),
        name="outproj",
    )(x_ctx, x_lat, mod, fo_ctx, fo_lat, po, so, ao_ctx, ao_lat,
      w["w_out"], w["ln1_g"], w["ln1_b"], w["w_r"], w["b_r"],
      w["tril"])


def _plan_kernel(cnt_ref, meta_ref):
    lane = lax.broadcasted_iota(i32, (8, 128), 1)
    sub = lax.broadcasted_iota(i32, (8, 128), 0)
    cnt = cnt_ref[...]
    is_e = (lane >= ROUTE_E0) & (lane < ROUTE_E0 + N_EXPERTS)
    tiles = jnp.where(is_e, jnp.floor((cnt + (MOE_TM - 1.0)) * (1.0 / MOE_TM)), 0.0)
    cum = tiles
    for s in (1, 2, 4, 8, 16):
        cum = cum + jnp.where(lane >= s, pltpu.roll(cum, s, 1), 0.0)
    pstart = (cum - tiles) * MOE_TM
    nused = jnp.max(cum, axis=-1, keepdims=True)
    fill = jnp.where(is_e & (cnt != tiles * MOE_TM), pstart + (tiles - 1.0) * MOE_TM, -1.0)
    meta = jnp.where(sub == 0, cnt, jnp.where(sub == 1, nused, jnp.where(sub == 2, fill,
                     jnp.where(sub == 3, pstart, 0.0))))
    meta_ref[...] = meta.astype(i32)


def _plan(cnt):
    return pl.pallas_call(
        _plan_kernel,
        grid=(1,),
        in_specs=[pl.BlockSpec((8, 128), lambda i: (0, 0))],
        out_specs=pl.BlockSpec((8, 128), lambda i: (0, 0)),
        out_shape=jax.ShapeDtypeStruct((8, 128), i32),
        compiler_params=_cparams(("arbitrary",)),
        name="plan",
    )(cnt)


def _dispatch_kernel(nch_ref, off_ref, dst_ref, tot_ref, fill_ref, nused_ref, h_ref, route_ref, xs_ref,
                     sorted_ref, zero_ref, sem, fill_sem):
    i = pl.program_id(0)

    def tile_fill(row0):
        return pltpu.make_async_copy(zero_ref, xs_ref.at[pl.ds(pl.multiple_of(row0, MOE_TM), MOE_TM)],
                                     fill_sem)

    @pl.when(i == 0)
    def _():
        zero_ref[...] = jnp.zeros_like(zero_ref)

        def start(e, c):
            @pl.when(fill_ref[e] >= 0)
            def _():
                tile_fill(jnp.maximum(fill_ref[e], 0)).start()
            return c

        def wait(e, c):
            @pl.when(fill_ref[e] >= 0)
            def _():
                tile_fill(jnp.maximum(fill_ref[e], 0)).wait()
            return c

        def start_tail(t, c):
            tile_fill(t * MOE_TM).start()
            return c

        def wait_tail(t, c):
            tile_fill(t * MOE_TM).wait()
            return c

        lax.fori_loop(0, N_EXPERTS, start, 0)
        lax.fori_loop(nused_ref[0], MOE_NT, start_tail, 0)
        lax.fori_loop(0, N_EXPERTS, wait, 0)
        lax.fori_loop(nused_ref[0], MOE_NT, wait_tail, 0)

    rt = route_ref[...].T
    hb = h_ref[...]
    slot = i % 2
    used_rows = tot_ref[i] * ROW_CHUNK

    def sort_rows(nrows):
        j = lax.broadcasted_iota(i32, (nrows, 1), 0).astype(f32)
        sel = jnp.where((j == rt[4:5, :]) | (j == rt[5:6, :]), 1.0, 0.0).astype(bf16)
        sorted_ref[slot, 0:nrows, :] = _dot(sel, hb).astype(bf16)

    @pl.when(used_rows <= DISP_ROWS_SHORT)
    def _():
        sort_rows(DISP_ROWS_SHORT)

    @pl.when(used_rows > DISP_ROWS_SHORT)
    def _():
        sort_rows(DISP_ROWS)

    def per_expert(e, c):
        idx = i * N_EXPERTS + e
        n, s0, d0 = nch_ref[idx], off_ref[idx], dst_ref[idx]
        b = 1
        while b <= TM // ROW_CHUNK:
            @pl.when((n & b) != 0)
            def _(b=b):
                r0 = (n & (b - 1)) * ROW_CHUNK
                rows = b * ROW_CHUNK
                pltpu.make_async_copy(
                    sorted_ref.at[slot, pl.ds(pl.multiple_of(s0 + r0, ROW_CHUNK), rows)],
                    xs_ref.at[pl.ds(pl.multiple_of(d0 + r0, ROW_CHUNK), rows)], sem.at[slot]).start()
            b *= 2
        return c

    lax.fori_loop(0, N_EXPERTS, per_expert, 0)

    def drain(tile, s):
        rows = tot_ref[tile] * ROW_CHUNK

        @pl.when(rows > 0)
        def _():
            pltpu.make_async_copy(sorted_ref.at[s, pl.ds(0, rows)], xs_ref.at[pl.ds(0, rows)],
                                  sem.at[s]).wait()

    @pl.when(i >= 1)
    def _():
        drain(i - 1, 1 - slot)

    @pl.when(i == pl.num_programs(0) - 1)
    def _():
        drain(i, slot)


def _dispatch(nch, off, dst, tot, fill, nused, h2, route):
    grid_spec = pltpu.PrefetchScalarGridSpec(
        num_scalar_prefetch=6,
        grid=(T_ALL // TM,),
        in_specs=[pl.BlockSpec((TM, D_MODEL), lambda i, *_: (i, 0)),
                  pl.BlockSpec((TM, 128), lambda i, *_: (i, 0))],
        out_specs=pl.BlockSpec(memory_space=pl.ANY),
        scratch_shapes=[pltpu.VMEM((2, DISP_ROWS, D_MODEL), bf16), pltpu.VMEM((MOE_TM, D_MODEL), bf16),
                        pltpu.SemaphoreType.DMA((2,)), pltpu.SemaphoreType.DMA(())],
    )
    return pl.pallas_call(
        _dispatch_kernel,
        grid_spec=grid_spec,
        out_shape=jax.ShapeDtypeStruct((MOE_NT * MOE_TM, D_MODEL), bf16),
        compiler_params=_cparams(("arbitrary",)),
        name="dispatch",
    )(nch, off, dst, tot, fill, nused, h2, route)


def _experts_kernel(cnt_ref, nused_ref, xs_ref, wg_hbm, wu_hbm, wd_hbm, ys_ref,
                    wg_f, wu_f, wd_f, wg_b, wu_b, wd_b, st, wsem, *, layer):
    i = pl.program_id(0)
    nused = nused_ref[0]
    NXT, NSLOT, LEFT, ROWS = 0, 1, 2, 3

    def w_copies(e, slot):
        return (pltpu.make_async_copy(wg_hbm.at[layer, e], wg_f.at[slot], wsem.at[slot, 0]),
                pltpu.make_async_copy(wu_hbm.at[layer, e], wu_f.at[slot], wsem.at[slot, 1]),
                pltpu.make_async_copy(wd_hbm.at[layer, e], wd_f.at[slot], wsem.at[slot, 2]))

    def next_nonempty(e):
        return lax.while_loop(
            lambda v: (v < N_EXPERTS) & (cnt_ref[jnp.minimum(v, N_EXPERTS - 1)] == 0),
            lambda v: v + 1, e)

    @pl.when(i == 0)
    def _():
        e0 = next_nonempty(jnp.int32(0))
        for c in w_copies(e0, 0):
            c.start()
        st[NXT] = e0
        st[NSLOT] = 0
        st[LEFT] = 0

    @pl.when(i < nused)
    def _():
        @pl.when(st[LEFT] == 0)
        def _():
            e = st[NXT]
            slot = st[NSLOT]
            for c in w_copies(e, slot):
                c.wait()
            e2 = next_nonempty(e + 1)

            @pl.when(e2 < N_EXPERTS)
            def _():
                for c in w_copies(e2, 1 - slot):
                    c.start()

            st[NXT] = e2
            st[NSLOT] = 1 - slot
            st[LEFT] = (cnt_ref[e] + (MOE_TM - 1)) // MOE_TM
            st[ROWS] = cnt_ref[e]
            wg_b[...] = wg_f[slot].astype(bf16)
            wu_b[...] = wu_f[slot].astype(bf16)
            wd_b[...] = wd_f[slot].astype(bf16)

        def ffn(rows):
            x = xs_ref[rows, :]
            hg = _dot(x, wg_b[...])
            hu = _dot(x, wu_b[...])
            act = (hg * jax.nn.sigmoid(hg)) * hu
            ys_ref[rows, :] = _dot(act.astype(bf16), wd_b[...]).astype(bf16)

        half = MOE_TM // 2
        short = st[ROWS] <= half

        @pl.when(short)
        def _():
            ffn(slice(0, half))
            ys_ref[half:, :] = jnp.zeros((MOE_TM - half, D_MODEL), bf16)

        @pl.when(jnp.logical_not(short))
        def _():
            ffn(slice(0, MOE_TM))

        st[LEFT] = st[LEFT] - 1
        st[ROWS] = st[ROWS] - MOE_TM

    @pl.when(i >= nused)
    def _():
        ys_ref[...] = jnp.zeros_like(ys_ref)


def _experts(counts, nused, xs, l, w_gate, w_up, w_down):
    hbm = pl.BlockSpec(memory_space=pl.ANY)
    grid_spec = pltpu.PrefetchScalarGridSpec(
        num_scalar_prefetch=2,
        grid=(MOE_NT,),
        in_specs=[pl.BlockSpec((MOE_TM, D_MODEL), lambda i, c, nu: (jnp.minimum(i, nu[0] - 1), 0)),
                  hbm, hbm, hbm],
        out_specs=pl.BlockSpec((MOE_TM, D_MODEL), lambda i, c, nu: (i, 0)),
        scratch_shapes=[
            pltpu.VMEM((2, D_MODEL, EXPERT_FF), f32),
            pltpu.VMEM((2, D_MODEL, EXPERT_FF), f32),
            pltpu.VMEM((2, EXPERT_FF, D_MODEL), f32),
            pltpu.VMEM((D_MODEL, EXPERT_FF), bf16),
            pltpu.VMEM((D_MODEL, EXPERT_FF), bf16),
            pltpu.VMEM((EXPERT_FF, D_MODEL), bf16),
            pltpu.SMEM((4,), i32),
            pltpu.SemaphoreType.DMA((2, 3)),
        ],
    )
    return pl.pallas_call(
        functools.partial(_experts_kernel, layer=l),
        grid_spec=grid_spec,
        out_shape=jax.ShapeDtypeStruct((MOE_NT * MOE_TM, D_MODEL), bf16),
        compiler_params=_cparams(("arbitrary",)),
        name="experts",
    )(counts, nused, xs, w_gate, w_up, w_down)


def _combine_kernel(nch_ref, off_ref, dst_ref, tot_ref, x1_ref, mod_ref, route_ref, ys_hbm, g_ref, b_ref,
                    oc_ref, ol_ref, ybuf, moe_ref, sem):
    i = pl.program_id(0)
    nt = pl.num_programs(0) - 1

    @pl.when(i == 0)
    def _():
        ybuf[...] = jnp.zeros_like(ybuf)

    @pl.when(i < nt)
    def _():
        slot = i % 2

        def per_expert(e, c):
            idx = i * N_EXPERTS + e
            n, s0, d0 = nch_ref[idx], off_ref[idx], dst_ref[idx]
            b = 1
            while b <= TM // ROW_CHUNK:
                @pl.when((n & b) != 0)
                def _(b=b):
                    r0 = (n & (b - 1)) * ROW_CHUNK
                    rows = b * ROW_CHUNK
                    pltpu.make_async_copy(
                        ys_hbm.at[pl.ds(pl.multiple_of(d0 + r0, ROW_CHUNK), rows)],
                        ybuf.at[slot, pl.ds(pl.multiple_of(s0 + r0, ROW_CHUNK), rows)],
                        sem.at[slot]).start()
                b *= 2
            return c

        lax.fori_loop(0, N_EXPERTS, per_expert, 0)

    @pl.when(i >= 1)
    def _():
        slot = (i - 1) % 2
        rows = tot_ref[i - 1] * ROW_CHUNK

        @pl.when(rows > 0)
        def _():
            pltpu.make_async_copy(ys_hbm.at[pl.ds(0, rows)], ybuf.at[slot, pl.ds(0, rows)],
                                  sem.at[slot]).wait()

        route = route_ref[...]
        mod = mod_ref[...]

        def unsort(nrows):
            lane = lax.broadcasted_iota(i32, (1, nrows), 1).astype(f32)
            wmat = (jnp.where(lane == route[:, 4:5], route[:, 2:3], 0.0)
                    + jnp.where(lane == route[:, 5:6], route[:, 3:4], 0.0))
            moe_ref[...] = _dot(wmat.astype(bf16), ybuf[slot, 0:nrows, :])

        @pl.when(rows <= DISP_ROWS_SHORT)
        def _():
            unsort(DISP_ROWS_SHORT)

        @pl.when(rows > DISP_ROWS_SHORT)
        def _():
            unsort(DISP_ROWS)

        y = _layer_norm(DEEPNORM_ALPHA * x1_ref[...] + mod[5:6] * moe_ref[...], g_ref[...], b_ref[...])

        @pl.when(i - 1 < T_CTX // TM)
        def _():
            oc_ref[...] = y

        @pl.when(i - 1 >= T_CTX // TM)
        def _():
            ol_ref[...] = y


def _combine(nch, off, dst, tot, x1, mod, route, ys, l, w):
    nt = T_ALL // TM
    vec = pl.BlockSpec((None, 1, D_MODEL), lambda i, *_: (l, 0, 0))
    nctx = T_CTX // TM
    prev = lambda i: jnp.maximum(i - 1, 0)
    grid_spec = pltpu.PrefetchScalarGridSpec(
        num_scalar_prefetch=4,
        grid=(nt + 1,),
        in_specs=[
            pl.BlockSpec((TM, D_MODEL), lambda i, *_: (prev(i), 0)),
            pl.BlockSpec((None, None, 6, D_MODEL), lambda i, *_: (l, prev(i) // (SEG // TM), 0, 0)),
            pl.BlockSpec((TM, 128), lambda i, *_: (prev(i), 0)),
            pl.BlockSpec(memory_space=pl.ANY),
            vec, vec,
        ],
        out_specs=[pl.BlockSpec((TM, D_MODEL), lambda i, *_: (jnp.minimum(prev(i), nctx - 1), 0)),
                   pl.BlockSpec((TM, D_MODEL), lambda i, *_: (jnp.maximum(prev(i) - nctx, 0), 0))],
        scratch_shapes=[pltpu.VMEM((2, DISP_ROWS, D_MODEL), bf16), pltpu.VMEM((TM, D_MODEL), f32),
                        pltpu.SemaphoreType.DMA((2,))],
    )
    return pl.pallas_call(
        _combine_kernel,
        grid_spec=grid_spec,
        out_shape=[jax.ShapeDtypeStruct((T_CTX, D_MODEL), f32),
                   jax.ShapeDtypeStruct((T_LAT, D_MODEL), f32)],
        compiler_params=_cparams(("arbitrary",)),
        name="combine",
    )(nch, off, dst, tot, x1, mod, route, ys, w["ln2_g"], w["ln2_b"])


def _dft_cos_sin(n, scale):
    k = jnp.arange(n, dtype=i32)
    ang = ((k[:, None] * k[None, :]) % n).astype(f32) * np.float32(2.0 * np.pi / n)
    return jnp.cos(ang) * scale, jnp.sin(ang) * scale


def _seq_dft_matrix(n):
    g = min(DFT_SPLIT, n)
    j = jnp.arange(n, dtype=i32)[None, :]
    k1 = jnp.arange(n // g, dtype=i32)[:, None]
    k2 = jnp.arange(g, dtype=i32)[:, None]
    ang_a = ((k1 * j) % (n // g)).astype(f32) * np.float32(2.0 * np.pi * g / n)
    ang_b = ((k2 * j) % n).astype(f32) * np.float32(2.0 * np.pi / n)
    scale = np.float32(n ** -0.5)
    ca, sa = jnp.cos(ang_a), jnp.sin(ang_a)
    cb, sb = jnp.cos(ang_b) * scale, jnp.sin(ang_b) * scale
    ca2 = jnp.concatenate([ca, ca], axis=1)[:, None, :]
    sa2 = jnp.concatenate([sa, sa], axis=1)[:, None, :]
    cb2 = jnp.concatenate([cb, -sb], axis=1)[None, :, :]
    sb2 = jnp.concatenate([sb, cb], axis=1)[None, :, :]
    return (ca2 * cb2 - sa2 * sb2).astype(bf16).reshape(n, 2 * n)


def _rope_tables():
    rows = DEC_SEQ // GRID_W
    row = jnp.repeat(jnp.arange(rows), GRID_W).astype(f32)
    col = jnp.tile(jnp.arange(GRID_W), rows).astype(f32)
    n_freq = HEAD_DIM // 4
    inv = ROPE_THETA ** (-jnp.arange(n_freq, dtype=f32) / n_freq)
    ar = row[:, None] * inv
    ac = col[:, None] * inv
    ang = jnp.concatenate([ar, ar, ac, ac], axis=-1)
    cos = jnp.tile(jnp.cos(ang), (1, N_HEADS))
    sin = jnp.tile(jnp.sin(ang), (1, N_HEADS))
    first = (jnp.arange(ATTN_W) % (HEAD_DIM // 2)) < n_freq
    sin_a = jnp.where(first[None, :], -sin, 0.0)
    sin_b = jnp.where(first[None, :], 0.0, sin)
    ident = jnp.zeros((TM, ATTN_W), f32)
    return (jnp.concatenate([cos, ident + 1.0], axis=0),
            jnp.concatenate([sin_a, ident], axis=0),
            jnp.concatenate([sin_b, ident], axis=0))


def _dup_cache(cache):
    c = jnp.transpose(cache, (1, 3, 0, 2, 4))
    return jnp.concatenate([c, c], axis=-1).astype(bf16)


def kernel(x_prompt, x_sample, cache_k, cache_v, c, c_ctx, w_mod, b_mod, w_in, w_fft, w_pool, pool_scale, sgu_ln_g, sgu_ln_b, w_sgu, b_sgu, q_norm_g, k_norm_g, w_out, ln1_g, ln1_b, w_router_group, b_router_group, w_router_expert, b_router_expert, w_gate, w_up, w_down, ln2_g, ln2_b):
    L = DEPTH
    x_ctx = x_prompt.reshape(T_CTX, D_MODEL)
    x_lat = x_sample.reshape(T_LAT, D_MODEL)

    cond8 = jnp.concatenate([c_ctx[None, :], c, jnp.zeros((8 - 1 - DEC_BATCH, D_MODEL), f32)], axis=0)
    mod = _modulation(cond8, w_mod, b_mod)[:, :N_SEG].reshape(L, N_SEG, 6, D_MODEL)

    cc, sc = _dft_cos_sin(FFT_W, np.float32(FFT_W ** -0.5))
    rope_cos, rope_sin_a, rope_sin_b = _rope_tables()
    head_id = jnp.arange(ATTN_W) // HEAD_DIM
    eye_g = jnp.eye(len(POOL_WINDOWS), dtype=f32)
    w_r = jnp.zeros((L, D_MODEL, 128), f32)
    w_r = w_r.at[:, :, :N_GROUPS].set(w_router_group).at[:, :, ROUTE_E0:ROUTE_E0 + N_EXPERTS].set(w_router_expert)
    b_r = jnp.zeros((L, 1, 128), f32)
    b_r = b_r.at[:, 0, :N_GROUPS].set(b_router_group).at[:, 0, ROUTE_E0:ROUTE_E0 + N_EXPERTS].set(b_router_expert)
    w_r_hi, w_r_lo = _split_hi_lo(w_r)
    w = {
        "w_in": w_in.astype(bf16),
        "csc": jnp.concatenate([cc, sc], axis=1).astype(bf16),
        "w_sgu": jnp.transpose(w_sgu, (0, 2, 1, 3)).reshape(L, CHUNK, SGU_HEADS * CHUNK).astype(bf16),
        "b_sgu": jnp.repeat(jnp.transpose(b_sgu, (0, 2, 1)), SGU_W // SGU_HEADS, axis=2),
        "sgu_ln_g": sgu_ln_g.reshape(L, 1, SGU_W),
        "sgu_ln_b": sgu_ln_b.reshape(L, 1, SGU_W),
        "q_norm_g": jnp.tile(q_norm_g, (1, N_HEADS)).reshape(L, 1, ATTN_W),
        "k_norm_g": jnp.tile(k_norm_g, (1, N_KV_HEADS)).reshape(L, 1, KV_W),
        "rope_cos": rope_cos, "rope_sin_a": rope_sin_a, "rope_sin_b": rope_sin_b,
        "ones_bd": (head_id[:, None] == head_id[None, :]).astype(bf16),
        "w_pool_bd": jnp.einsum("lgcd,gh->lgchd", w_pool, eye_g).reshape(L, POOL_W, POOL_W).astype(bf16),
        "pool_scale": pool_scale.reshape(L, 1, POOL_W),
        "w_fft": w_fft.astype(bf16),
        "w_out": w_out.astype(bf16),
        "ln1_g": ln1_g.reshape(L, 1, D_MODEL), "ln1_b": ln1_b.reshape(L, 1, D_MODEL),
        "ln2_g": ln2_g.reshape(L, 1, D_MODEL), "ln2_b": ln2_b.reshape(L, 1, D_MODEL),
        "w_r": jnp.concatenate([w_r_hi, w_r_lo], axis=-1), "b_r": b_r,
        "tril": (jnp.arange(TM)[:, None] > jnp.arange(TM)[None, :]).astype(bf16),
    }
    m_ctx = _seq_dft_matrix(SEQ)
    m_lat = _seq_dft_matrix(DEC_SEQ)
    kc_all = _dup_cache(cache_k)
    vc_all = _dup_cache(cache_v)

    new_k, new_v = [], []
    for l in range(L):
        pq, praw, sgu, q, kd, vd, nk, nv = _inproj(x_ctx, x_lat, mod, l, w)
        new_k.append(nk[:T_CTX].reshape(BATCH, SEQ, N_KV_HEADS, HEAD_DIM))
        new_v.append(nv[:T_CTX].reshape(BATCH, SEQ, N_KV_HEADS, HEAD_DIM))
        po = _pool(praw, l, w)
        fo_ctx = _seqdft(pq, m_ctx, l, w, n=SEQ, tr=SEQ, nseq=BATCH, row0=0)
        fo_lat = _seqdft(pq, m_lat, l, w, n=DEC_SEQ, tr=FFT_TR, nseq=DEC_BATCH, row0=T_CTX)
        ao_ctx = _attention(q, kd, vd, None, n=SEQ, tq=SEQ, nseq=BATCH, row0=0, heads=N_KV_HEADS)
        ao_lat = _attention(q, kd, vd, (kc_all[l], vc_all[l]), n=DEC_SEQ, tq=ATT_TQ, nseq=DEC_BATCH,
                            row0=T_CTX, heads=1)
        x1, h2, route, cnt, tab = _outproj(x_ctx, x_lat, mod, fo_ctx, fo_lat, po, sgu, ao_ctx, ao_lat, l, w)
        meta = _plan(cnt)
        experts = slice(ROUTE_E0, ROUTE_E0 + N_EXPERTS)
        counts = meta[0, experts]
        nused = meta[1, :1]
        fill = meta[2, experts]
        tab = tab[:, :, experts].astype(i32)
        nch = (tab[:, 0] // ROW_CHUNK).reshape(-1)
        off = tab[:, 1].reshape(-1)
        dst = (meta[3, experts][None, :] + tab[:, 2]).reshape(-1)
        tot = jnp.sum(tab[:, 0], axis=1) // ROW_CHUNK
        xs = _dispatch(nch, off, dst, tot, fill, nused, h2, route)
        ys = _experts(counts, nused, xs, l, w_gate, w_up, w_down)
        x_ctx, x_lat = _combine(nch, off, dst, tot, x1, mod, route, ys, l, w)

    y_prompt = x_ctx.reshape(BATCH, SEQ, D_MODEL)
    y_sample = x_lat.reshape(DEC_BATCH, DEC_SEQ, D_MODEL)
    return (y_prompt, y_sample, jnp.stack(new_k, axis=1), jnp.stack(new_v, axis=1))
```

```python
import functools

import numpy as np
import jax
import jax.numpy as jnp
from jax import lax
from jax.experimental import pallas as pl
from jax.experimental.pallas import tpu as pltpu

f32 = jnp.float32
bf16 = jnp.bfloat16
i32 = jnp.int32

D_MODEL = 1024
BATCH = 16
SEQ = 256
DEPTH = 4
DEC_BATCH = 2
DEC_SEQ = 4096
PAST_LEN = 512
GRID_W = 64
FFT_W = 256
POOL_W = 256
POOL_WINDOWS = (2, 4, 8, 16)
POOL_GROUP = 64
SGU_W = 256
SGU_HEADS = 4
CHUNK = 128
HEAD_DIM = 64
ATTN_W = 256
N_HEADS = 4
N_KV_HEADS = 2
KV_W = 128
IN_W = 1536
ROPE_THETA = 10000.0
N_GROUPS = 4
EXPERTS_PER_GROUP = 8
N_EXPERTS = 32
EXPERT_FF = 512
DEEPNORM_ALPHA = float((2 * DEPTH) ** 0.25)
LN_EPS = 1e-5
RMS_EPS = 1e-6

T_CTX = BATCH * SEQ
T_LAT = DEC_BATCH * DEC_SEQ
T_ALL = T_CTX + T_LAT
SEG = 4096
N_SEG = T_ALL // SEG

TM = 512
POOL_TB = 1024
POOL_HALO = 8
DFT_HALF_ROWS = 2304
DFT_HALF_TR = 384
ATT_TQ = 512
ATT_CHUNK = 1024
DFT_SPLIT = 64
MOE_TM = 256
ROW_CHUNK = 16
MOE_ROWS = 2 * T_ALL
MOE_PAD_ROWS = (T_ALL // TM) * N_EXPERTS * (ROW_CHUNK - 1)
MOE_NT = -(-(MOE_ROWS + MOE_PAD_ROWS) // MOE_TM) + N_EXPERTS
DISP_ROWS = 2 * TM + N_EXPERTS * ROW_CHUNK
DISP_ROWS_SHORT = 2 * TM + N_EXPERTS * ROW_CHUNK // 2
ROUTE_E0 = 32
VMEM_LIMIT = 56 * 1024 * 1024


def _cparams(sem):
    return pltpu.CompilerParams(dimension_semantics=sem, vmem_limit_bytes=VMEM_LIMIT)


def _split_hi_lo(a):
    hi = a.astype(bf16)
    lo = (a - hi.astype(f32)).astype(bf16)
    return hi, lo


def _dot(a, b):
    return jnp.dot(a, b, preferred_element_type=f32)


def _mod_kernel(c_ref, w_ref, b_ref, o_ref):
    c = c_ref[...]
    s = c * jax.nn.sigmoid(c)
    s_hi, s_lo = _split_hi_lo(s)
    w_hi, w_lo = _split_hi_lo(w_ref[...])
    o_ref[...] = _dot(s_hi, w_hi) + _dot(s_hi, w_lo) + _dot(s_lo, w_hi) + b_ref[...]


def _modulation(cond8, w_mod, b_mod):
    tn = 1536
    return pl.pallas_call(
        _mod_kernel,
        grid=(DEPTH, 6 * D_MODEL // tn),
        in_specs=[
            pl.BlockSpec((8, D_MODEL), lambda l, j: (0, 0)),
            pl.BlockSpec((None, D_MODEL, tn), lambda l, j: (l, 0, j)),
            pl.BlockSpec((None, 1, tn), lambda l, j: (l, 0, j)),
        ],
        out_specs=pl.BlockSpec((None, 8, tn), lambda l, j: (l, 0, j)),
        out_shape=jax.ShapeDtypeStruct((DEPTH, 8, 6 * D_MODEL), f32),
        compiler_params=_cparams(("arbitrary", "arbitrary")),
        name="modulation",
    )(cond8, w_mod, b_mod.reshape(DEPTH, 1, 6 * D_MODEL))


def _head_rms(x, ones_bd, gain):
    ss = _dot((x * x).astype(bf16), ones_bd)
    return x * lax.rsqrt(ss * (1.0 / HEAD_DIM) + RMS_EPS) * gain


def _rope(x, cos, sin_a, sin_b):
    w = x.shape[-1]
    q4 = HEAD_DIM // 4
    return x * cos + pltpu.roll(x, w - q4, 1) * sin_a + pltpu.roll(x, q4, 1) * sin_b


def _dup_half(x, first):
    lane = lax.broadcasted_iota(i32, x.shape, 1)
    r = pltpu.roll(x, HEAD_DIM, 1)
    if first:
        return jnp.where(lane < HEAD_DIM, x, r)
    return jnp.where(lane >= HEAD_DIM, x, r)


def _gelu_tanh(x):
    c = np.sqrt(2.0 / np.pi).astype(np.float32)
    return x * (0.5 * (1.0 + jnp.tanh(c * (x + 0.044715 * (x * x * x)))))


def _inproj_kernel(xc_ref, xl_ref, mod_ref, win_ref, csc_ref, wsgu_ref, bsgu_ref, lng_ref, lnb_ref,
                   qg_ref, kg_ref, cos_ref, sina_ref, sinb_ref, ones_ref,
                   pq_ref, pool_ref, sgu_ref, q_ref, kd_ref, vd_ref, nk_ref, nv_ref):
    x = jnp.where(pl.program_id(0) < T_CTX // TM, xc_ref[...], xl_ref[...])
    mod = mod_ref[...]
    h = (x * (1.0 + mod[1:2]) + mod[0:1]).astype(bf16)
    proj = _dot(h, win_ref[...])

    a = proj[:, 0:FFT_W].astype(bf16)
    pq_ref[...] = _dot(a, csc_ref[...]).astype(bf16)

    pool_ref[...] = proj[:, FFT_W:FFT_W + POOL_W]

    o = FFT_W + POOL_W
    hgu = _gelu_tanh(proj[:, o:o + 2 * SGU_W])
    u = hgu[:, :SGU_W]
    v = hgu[:, SGU_W:]
    mu = jnp.mean(v, axis=-1, keepdims=True)
    vc = v - mu
    var = jnp.mean(vc * vc, axis=-1, keepdims=True)
    v = vc * lax.rsqrt(var + LN_EPS) * lng_ref[...] + lnb_ref[...]
    lane = lax.broadcasted_iota(i32, (CHUNK, SGU_W), 1)
    head = lane // (SGU_W // SGU_HEADS)
    wcat = wsgu_ref[...]
    for cidx in range(TM // CHUNK):
        rows = slice(cidx * CHUNK, (cidx + 1) * CHUNK)
        vch = v[rows]
        vblk = jnp.concatenate(
            [jnp.where(head == g, vch, 0.0) for g in range(SGU_HEADS)], axis=0).astype(bf16)
        sp = _dot(wcat, vblk) + bsgu_ref[...]
        sgu_ref[rows, :] = (u[rows] * sp).astype(bf16)

    o = o + 2 * SGU_W
    ones_bd = ones_ref[...]
    cos = cos_ref[...]
    sin_a = sina_ref[...]
    sin_b = sinb_ref[...]
    q = _head_rms(proj[:, o:o + ATTN_W], ones_bd, qg_ref[...])
    q = _rope(q, cos, sin_a, sin_b) * np.float32(HEAD_DIM ** -0.5 * np.log2(np.e))
    q_ref[...] = q.astype(bf16)
    o = o + ATTN_W
    k = _head_rms(proj[:, o:o + KV_W], ones_bd[:KV_W, :KV_W], kg_ref[...])
    nk_ref[...] = k
    k = _rope(k, cos[:, :KV_W], sin_a[:, :KV_W], sin_b[:, :KV_W])
    kd_ref[0] = _dup_half(k, True).astype(bf16)
    kd_ref[1] = _dup_half(k, False).astype(bf16)
    o = o + KV_W
    vv = proj[:, o:o + KV_W]
    nv_ref[...] = vv
    vd_ref[0] = _dup_half(vv, True).astype(bf16)
    vd_ref[1] = _dup_half(vv, False).astype(bf16)


def _rope_block(i):
    nlat = DEC_SEQ // TM
    nctx = T_CTX // TM
    return jnp.where(i < nctx, nlat, (i - nctx) % nlat)


def _ctx_tile(wd):
    return pl.BlockSpec((TM, wd), lambda i, *_: (jnp.minimum(i, T_CTX // TM - 1), 0))


def _lat_tile(wd):
    return pl.BlockSpec((TM, wd), lambda i, *_: (jnp.maximum(i - T_CTX // TM, 0), 0))


def _inproj(x_ctx, x_lat, mod, l, w):
    nt = T_ALL // TM
    tile = lambda wd: pl.BlockSpec((TM, wd), lambda i: (i, 0))
    const = lambda shape: pl.BlockSpec(shape, lambda i: (0,) * len(shape))
    rope_spec = pl.BlockSpec((TM, ATTN_W), lambda i: (_rope_block(i), 0))
    return pl.pallas_call(
        _inproj_kernel,
        grid=(nt,),
        in_specs=[
            _ctx_tile(D_MODEL), _lat_tile(D_MODEL),
            pl.BlockSpec((None, None, 6, D_MODEL), lambda i: (l, i // (SEG // TM), 0, 0)),
            pl.BlockSpec((None, D_MODEL, IN_W), lambda i: (l, 0, 0)),
            const((FFT_W, 2 * FFT_W)),
            pl.BlockSpec((None, CHUNK, SGU_HEADS * CHUNK), lambda i: (l, 0, 0)),
            pl.BlockSpec((None, CHUNK, SGU_W), lambda i: (l, 0, 0)),
            pl.BlockSpec((None, 1, SGU_W), lambda i: (l, 0, 0)),
            pl.BlockSpec((None, 1, SGU_W), lambda i: (l, 0, 0)),
            pl.BlockSpec((None, 1, ATTN_W), lambda i: (l, 0, 0)),
            pl.BlockSpec((None, 1, KV_W), lambda i: (l, 0, 0)),
            rope_spec, rope_spec, rope_spec,
            const((ATTN_W, ATTN_W)),
        ],
        out_specs=[
            tile(2 * FFT_W), tile(POOL_W), tile(SGU_W), tile(ATTN_W),
            pl.BlockSpec((N_KV_HEADS, TM, KV_W), lambda i: (0, i, 0)),
            pl.BlockSpec((N_KV_HEADS, TM, KV_W), lambda i: (0, i, 0)),
            tile(KV_W), tile(KV_W),
        ],
        out_shape=[
            jax.ShapeDtypeStruct((T_ALL, 2 * FFT_W), bf16),
            jax.ShapeDtypeStruct((T_ALL, POOL_W), f32),
            jax.ShapeDtypeStruct((T_ALL, SGU_W), bf16),
            jax.ShapeDtypeStruct((T_ALL, ATTN_W), bf16),
            jax.ShapeDtypeStruct((N_KV_HEADS, T_ALL, KV_W), bf16),
            jax.ShapeDtypeStruct((N_KV_HEADS, T_ALL, KV_W), bf16),
            jax.ShapeDtypeStruct((T_ALL, KV_W), f32),
            jax.ShapeDtypeStruct((T_ALL, KV_W), f32),
        ],
        compiler_params=_cparams(("arbitrary",)),
        name="inproj",
    )(x_ctx, x_lat, mod, w["w_in"], w["csc"], w["w_sgu"], w["b_sgu"], w["sgu_ln_g"], w["sgu_ln_b"],
      w["q_norm_g"], w["k_norm_g"], w["rope_cos"], w["rope_sin_a"], w["rope_sin_b"], w["ones_bd"])


def _pool_kernel(prev_ref, cur_ref, next_ref, wp_ref, scale_ref, o_ref):
    i = pl.program_id(0)
    n = jnp.where(i < T_CTX // POOL_TB, SEQ, DEC_SEQ)
    hl = POOL_HALO
    ext = jnp.concatenate([prev_ref[POOL_TB - hl:, :], cur_ref[...], next_ref[:hl, :]], axis=0)
    rows = POOL_TB + 2 * hl
    r = lax.broadcasted_iota(i32, (rows, 1), 0)
    pos = (i * POOL_TB + r - hl) & (n - 1)

    def back(a, s):
        return jnp.where(pos >= s, pltpu.roll(a, s, 0), 0.0)

    def fwd(a, s):
        return jnp.where(pos + s < n, pltpu.roll(a, rows - s, 0), 0.0)

    bsum = [back(ext, 1)]
    fsum = [ext]
    for k in range(3):
        s = 1 << k
        bsum.append(bsum[k] + back(bsum[k], s))
        fsum.append(fsum[k] + fwd(fsum[k], s))
    lane = lax.broadcasted_iota(i32, (1, POOL_W), 1)
    grp = lane // POOL_GROUP
    win = bsum[3] + fsum[3]
    half = jnp.full((1, POOL_W), POOL_WINDOWS[3] // 2, i32)
    for g in (2, 1, 0):
        win = jnp.where(grp == g, bsum[g] + fsum[g], win)
        half = jnp.where(grp == g, POOL_WINDOWS[g] // 2, half)
    cnt = (jnp.minimum(pos + half, n) - jnp.maximum(pos - half, 0)).astype(f32)
    y = (win / cnt - ext)[hl:hl + POOL_TB]
    o_ref[...] = (_dot(y.astype(bf16), wp_ref[...]) * scale_ref[...]).astype(bf16)


def _pool(p, l, w):
    nt = T_ALL // POOL_TB
    blk = lambda f: pl.BlockSpec((POOL_TB, POOL_W), lambda i: (f(i), 0))
    return pl.pallas_call(
        _pool_kernel,
        grid=(nt,),
        in_specs=[
            blk(lambda i: jnp.maximum(i - 1, 0)), blk(lambda i: i),
            blk(lambda i: jnp.minimum(i + 1, nt - 1)),
            pl.BlockSpec((None, POOL_W, POOL_W), lambda i: (l, 0, 0)),
            pl.BlockSpec((None, 1, POOL_W), lambda i: (l, 0, 0)),
        ],
        out_specs=blk(lambda i: i),
        out_shape=jax.ShapeDtypeStruct((T_ALL, POOL_W), bf16),
        compiler_params=_cparams(("arbitrary",)),
        name="pool",
    )(p, p, p, w["w_pool_bd"], w["pool_scale"])


def _seqdft_kernel(*refs, n, nseq):
    m_ref, pq_refs, w_ref, o_ref = refs[0], refs[1:-2], refs[-2], refs[-1]
    per_blk = SEG // n
    tr = m_ref.shape[0]

    def one_sequence(b, rows):
        pq_ref = pq_refs[b // per_blk]
        r0 = (b % per_blk) * n
        f = (_dot(m_ref[:, :n], pq_ref[r0:r0 + n, :FFT_W])
             + _dot(m_ref[:, n:], pq_ref[r0:r0 + n, FFT_W:]))
        o_ref[rows, :] = _dot(f.astype(bf16), w_ref[...]).astype(bf16)

    if tr == n:
        for b in range(nseq):
            one_sequence(b, slice(b * n, (b + 1) * n))
    else:
        for b in range(nseq):
            @pl.when(pl.program_id(1) == b)
            def _(b=b):
                one_sequence(b, slice(0, tr))


def _seqdft(pq, m, l, w, *, n, tr, nseq, row0):
    nr = n // tr
    nblk = nseq * n // SEG
    pq_specs = [pl.BlockSpec((SEG, 2 * FFT_W), lambda i, b, j=j: (row0 // SEG + j, 0))
                for j in range(nblk)]
    if nr == 1:
        grid, out_spec = (1, 1), pl.BlockSpec((nseq * n, FFT_W), lambda i, b: (0, 0))
    else:
        grid, out_spec = (nr, nseq), pl.BlockSpec((tr, FFT_W), lambda i, b: (b * nr + i, 0))
    return pl.pallas_call(
        functools.partial(_seqdft_kernel, n=n, nseq=nseq),
        grid=grid,
        in_specs=[pl.BlockSpec((tr, 2 * n), lambda i, b: (i, 0))] + pq_specs
        + [pl.BlockSpec((None, FFT_W, FFT_W), lambda i, b: (l, 0, 0))],
        out_specs=out_spec,
        out_shape=jax.ShapeDtypeStruct((nseq * n, FFT_W), bf16),
        compiler_params=_cparams(("arbitrary", "arbitrary")),
        name="seqdft_%d" % n,
    )(m, *([pq] * nblk), w["w_fft"])


def _seqdft_half_kernel(mc_ref, ms_ref, pq0_ref, pq1_ref, w_ref, u_ref, v_ref):
    for b, pq_ref in enumerate((pq0_ref, pq1_ref)):
        @pl.when(pl.program_id(1) == b)
        def _(pq_ref=pq_ref):
            a = _dot(mc_ref[...], pq_ref[:, :FFT_W])
            bq = _dot(ms_ref[...], pq_ref[:, FFT_W:])
            u_ref[...] = _dot((a - bq).astype(bf16), w_ref[...]).astype(bf16)
            v_ref[...] = _dot((a + bq).astype(bf16), w_ref[...]).astype(bf16)


def _seqdft_half(pq, mc, ms, l, w):
    nr = DFT_HALF_ROWS // DFT_HALF_TR
    table = pl.BlockSpec((DFT_HALF_TR, DEC_SEQ), lambda i, b: (i, 0))
    pq_specs = [pl.BlockSpec((SEG, 2 * FFT_W), lambda i, b, j=j: (T_CTX // SEG + j, 0))
                for j in range(DEC_BATCH)]
    out_spec = pl.BlockSpec((DFT_HALF_TR, FFT_W), lambda i, b: (b * nr + i, 0))
    out_shape = jax.ShapeDtypeStruct((DEC_BATCH * DFT_HALF_ROWS, FFT_W), bf16)
    u, v = pl.pallas_call(
        _seqdft_half_kernel,
        grid=(nr, DEC_BATCH),
        in_specs=[table, table] + pq_specs + [pl.BlockSpec((None, FFT_W, FFT_W), lambda i, b: (l, 0, 0))],
        out_specs=[out_spec, out_spec],
        out_shape=[out_shape, out_shape],
        compiler_params=_cparams(("arbitrary", "arbitrary")),
        name="seqdft_half",
    )(mc, ms, pq, pq, w["w_fft"])
    h = DEC_SEQ // 2
    u = u.reshape(DEC_BATCH, DFT_HALF_ROWS, FFT_W)
    v = v.reshape(DEC_BATCH, DFT_HALF_ROWS, FFT_W)
    return jnp.concatenate([u[:, :h + 1], v[:, h - 1:0:-1]], axis=1).reshape(T_LAT, FFT_W)


def _attn_kernel(*refs, has_cache):
    if has_cache:
        q_ref, k_ref, v_ref, kc_ref, vc_ref, o_ref = refs
    else:
        q_ref, k_ref, v_ref, o_ref = refs
    pair = 2 * HEAD_DIM
    tq = q_ref.shape[0]
    n = k_ref.shape[1]
    nt = (((1,), (1,)), ((), ()))
    chunk = min(n, ATT_CHUNK)
    lane = lax.broadcasted_iota(i32, (tq, pair), 1)
    for h in range(k_ref.shape[0]):
        q = q_ref[:, h * pair:(h + 1) * pair]
        zero = jnp.zeros_like(q)
        qs = jnp.concatenate([jnp.where(lane < HEAD_DIM, q, zero),
                              jnp.where(lane >= HEAD_DIM, q, zero)], axis=0)
        parts = [(k_ref, v_ref, c * chunk, chunk) for c in range(n // chunk)]
        if has_cache:
            parts = [(kc_ref, vc_ref, 0, PAST_LEN)] + parts
        m = jnp.full((2 * tq, 1), -jnp.inf, f32)
        den = jnp.zeros((2 * tq, 1), f32)
        acc = jnp.zeros((2 * tq, pair), f32)
        for kr, vr, off, size in parts:
            s = lax.dot_general(qs, kr[h, off:off + size, :], nt, preferred_element_type=f32)
            m_new = jnp.maximum(m, jnp.max(s, axis=-1, keepdims=True))
            alpha = jnp.exp2(m - m_new)
            p = jnp.exp2(s - m_new).astype(bf16)
            den = alpha * den + jnp.sum(p.astype(f32), axis=-1, keepdims=True)
            acc = alpha * acc + _dot(p, vr[h, off:off + size, :])
            m = m_new
        out = acc / den
        o_ref[:, h * pair:(h + 1) * pair] = jnp.where(lane < HEAD_DIM, out[:tq], out[tq:]).astype(bf16)


def _attention(q, kd, vd, cache, *, n, tq, nseq, row0, heads):
    nq = n // tq
    b0 = row0 // n
    q0 = row0 // tq
    in_specs = [
        pl.BlockSpec((tq, heads * 2 * HEAD_DIM), lambda b, h, i: (q0 + b * nq + i, h)),
        pl.BlockSpec((heads, n, KV_W), lambda b, h, i: (h, b0 + b, 0)),
        pl.BlockSpec((heads, n, KV_W), lambda b, h, i: (h, b0 + b, 0)),
    ]
    args = [q, kd, vd]
    if cache is not None:
        cspec = pl.BlockSpec((heads, None, PAST_LEN, KV_W), lambda b, h, i: (h, b, 0, 0))
        in_specs += [cspec, cspec]
        args += list(cache)
    return pl.pallas_call(
        functools.partial(_attn_kernel, has_cache=cache is not None),
        grid=(nseq, N_KV_HEADS // heads, nq),
        in_specs=in_specs,
        out_specs=pl.BlockSpec((tq, heads * 2 * HEAD_DIM), lambda b, h, i: (b * nq + i, h)),
        out_shape=jax.ShapeDtypeStruct((nseq * n, ATTN_W), bf16),
        compiler_params=_cparams(("arbitrary", "arbitrary", "arbitrary")),
        name="attention_%d" % n,
    )(*args)


def _layer_norm(x, g, b):
    mu = jnp.mean(x, axis=-1, keepdims=True)
    xc = x - mu
    var = jnp.mean(xc * xc, axis=-1, keepdims=True)
    return xc * lax.rsqrt(var + LN_EPS) * g + b


def _outproj_kernel(xc_ref, xl_ref, mod_ref, fc_ref, fl_ref, p_ref, s_ref, ac_ref, al_ref, wout_ref,
                    g_ref, b_ref, wr_ref, br_ref, tril_ref,
                    x1_ref, h2_ref, route_ref, cnt_ref, tab_ref, carry_ref):
    i = pl.program_id(0)

    @pl.when(i == 0)
    def _():
        carry_ref[...] = jnp.zeros_like(carry_ref)

    mod = mod_ref[...]
    is_ctx = i < T_CTX // TM
    f_mix = jnp.where(is_ctx, fc_ref[...], fl_ref[...])
    a_mix = jnp.where(is_ctx, ac_ref[...], al_ref[...])
    mix = _dot(jnp.concatenate([f_mix, p_ref[...], s_ref[...], a_mix], axis=1), wout_ref[...])
    x = jnp.where(is_ctx, xc_ref[...], xl_ref[...])
    x1 = _layer_norm(DEEPNORM_ALPHA * x + mod[2:3] * mix, g_ref[...], b_ref[...])
    x1_ref[...] = x1
    h2 = x1 * (1.0 + mod[4:5]) + mod[3:4]
    h_hi, h_lo = _split_hi_lo(h2)
    h2_ref[...] = h_hi
    hw = _dot(h_hi, wr_ref[...])
    logits = hw[:, :128] + hw[:, 128:] + _dot(h_lo, wr_ref[:, :128]) + br_ref[...]
    lane = lax.broadcasted_iota(i32, logits.shape, 1).astype(f32)
    neg = jnp.float32(-jnp.inf)
    big = jnp.float32(1 << 20)
    gl = jnp.where(lane < N_GROUPS, logits, neg)
    gmax = jnp.max(gl, axis=-1, keepdims=True)
    gsel = jnp.min(jnp.where(gl == gmax, lane, big), axis=-1, keepdims=True)
    pg = 1.0 / jnp.sum(jnp.exp(gl - gmax), axis=-1, keepdims=True)
    e_lo = ROUTE_E0 + gsel * EXPERTS_PER_GROUP
    el = jnp.where((lane >= e_lo) & (lane < e_lo + EXPERTS_PER_GROUP), logits, neg)
    v1 = jnp.max(el, axis=-1, keepdims=True)
    i1 = jnp.min(jnp.where(el == v1, lane, big), axis=-1, keepdims=True)
    el2 = jnp.where(lane == i1, neg, el)
    v2 = jnp.max(el2, axis=-1, keepdims=True)
    i2 = jnp.min(jnp.where(el2 == v2, lane, big), axis=-1, keepdims=True)
    e2 = jnp.exp(v2 - v1)
    w1 = pg / (1.0 + e2)
    w2 = pg * e2 / (1.0 + e2)
    oh1 = lane == i1
    oh2 = lane == i2
    oh = jnp.where(oh1 | oh2, 1.0, 0.0)
    lrank = _dot(tril_ref[...], oh.astype(bf16))
    seg = jnp.floor((jnp.sum(oh, axis=0, keepdims=True) + (ROW_CHUNK - 1.0)) * (1.0 / ROW_CHUNK)) * ROW_CHUNK
    seg8 = jnp.broadcast_to(seg, (8, 128))
    lane8 = lax.broadcasted_iota(i32, (8, 128), 1)
    off8 = seg8
    for sh in (1, 2, 4, 8, 16):
        off8 = off8 + jnp.where(lane8 >= sh, pltpu.roll(off8, sh, 1), 0.0)
    off8 = off8 - seg8
    carry = carry_ref[...]
    lpos = lrank + off8[0:1, :]
    pick = lambda sel, val: jnp.sum(jnp.where(sel, val, 0.0), axis=-1, keepdims=True)
    sub8 = lax.broadcasted_iota(i32, (8, 128), 0)
    tab_ref[...] = jnp.where(sub8 == 0, seg8, jnp.where(sub8 == 1, off8, jnp.where(sub8 == 2, carry, 0.0)))
    carry = carry + seg8
    carry_ref[...] = carry
    cnt_ref[...] = carry
    cols = (i1 - ROUTE_E0, i2 - ROUTE_E0, w1, w2, pick(oh1, lpos), pick(oh2, lpos))
    route = jnp.zeros_like(logits)
    for j, col in enumerate(cols):
        route = jnp.where(lane == j, col, route)
    route_ref[...] = route


def _outproj(x_ctx, x_lat, mod, fo_ctx, fo_lat, po, so, ao_ctx, ao_lat, l, w):
    nt = T_ALL // TM
    tile = lambda wd: pl.BlockSpec((TM, wd), lambda i: (i, 0))
    vec = lambda wd: pl.BlockSpec((None, 1, wd), lambda i: (l, 0, 0))
    return pl.pallas_call(
        _outproj_kernel,
        grid=(nt,),
        in_specs=[
            _ctx_tile(D_MODEL), _lat_tile(D_MODEL),
            pl.BlockSpec((None, None, 6, D_MODEL), lambda i: (l, i // (SEG // TM), 0, 0)),
            _ctx_tile(FFT_W), _lat_tile(FFT_W), tile(POOL_W), tile(SGU_W),
            _ctx_tile(ATTN_W), _lat_tile(ATTN_W),
            pl.BlockSpec((None, D_MODEL, D_MODEL), lambda i: (l, 0, 0)),
            vec(D_MODEL), vec(D_MODEL),
            pl.BlockSpec((None, D_MODEL, 256), lambda i: (l, 0, 0)),
            vec(128),
            pl.BlockSpec((TM, TM), lambda i: (0, 0)),
        ],
        out_specs=[tile(D_MODEL), tile(D_MODEL), tile(128), pl.BlockSpec((8, 128), lambda i: (0, 0)),
                   pl.BlockSpec((None, 8, 128), lambda i: (i, 0, 0))],
        out_shape=[
            jax.ShapeDtypeStruct((T_ALL, D_MODEL), f32),
            jax.ShapeDtypeStruct((T_ALL, D_MODEL), bf16),
            jax.ShapeDtypeStruct((T_ALL, 128), f32),
            jax.ShapeDtypeStruct((8, 128), f32),
            jax.ShapeDtypeStruct((nt, 8, 128), f32),
        ],
        scratch_shapes=[pltpu.VMEM((8, 128), f32)],
        compiler_params=_cparams(("arbitrary",)),
        name="outproj",
    )(x_ctx, x_lat, mod, fo_ctx, fo_lat, po, so, ao_ctx, ao_lat,
      w["w_out"], w["ln1_g"], w["ln1_b"], w["w_r"], w["b_r"],
      w["tril"])


def _plan_kernel(cnt_ref, meta_ref):
    lane = lax.broadcasted_iota(i32, (8, 128), 1)
    sub = lax.broadcasted_iota(i32, (8, 128), 0)
    cnt = cnt_ref[...]
    is_e = (lane >= ROUTE_E0) & (lane < ROUTE_E0 + N_EXPERTS)
    tiles = jnp.where(is_e, jnp.floor((cnt + (MOE_TM - 1.0)) * (1.0 / MOE_TM)), 0.0)
    cum = tiles
    for s in (1, 2, 4, 8, 16):
        cum = cum + jnp.where(lane >= s, pltpu.roll(cum, s, 1), 0.0)
    pstart = (cum - tiles) * MOE_TM
    nused = jnp.max(cum, axis=-1, keepdims=True)
    fill = jnp.where(is_e & (cnt != tiles * MOE_TM), pstart + (tiles - 1.0) * MOE_TM, -1.0)
    meta = jnp.where(sub == 0, cnt, jnp.where(sub == 1, nused, jnp.where(sub == 2, fill,
                     jnp.where(sub == 3, pstart, 0.0))))
    meta_ref[...] = meta.astype(i32)


def _plan(cnt):
    return pl.pallas_call(
        _plan_kernel,
        grid=(1,),
        in_specs=[pl.BlockSpec((8, 128), lambda i: (0, 0))],
        out_specs=pl.BlockSpec((8, 128), lambda i: (0, 0)),
        out_shape=jax.ShapeDtypeStruct((8, 128), i32),
        compiler_params=_cparams(("arbitrary",)),
        name="plan",
    )(cnt)


def _dispatch_kernel(nch_ref, off_ref, dst_ref, tot_ref, fill_ref, nused_ref, h_ref, route_ref, xs_ref,
                     sorted_ref, zero_ref, sem, fill_sem):
    i = pl.program_id(0)

    def tile_fill(row0):
        return pltpu.make_async_copy(zero_ref, xs_ref.at[pl.ds(pl.multiple_of(row0, MOE_TM), MOE_TM)],
                                     fill_sem)

    @pl.when(i == 0)
    def _():
        zero_ref[...] = jnp.zeros_like(zero_ref)

        def start(e, c):
            @pl.when(fill_ref[e] >= 0)
            def _():
                tile_fill(jnp.maximum(fill_ref[e], 0)).start()
            return c

        def wait(e, c):
            @pl.when(fill_ref[e] >= 0)
            def _():
                tile_fill(jnp.maximum(fill_ref[e], 0)).wait()
            return c

        def start_tail(t, c):
            tile_fill(t * MOE_TM).start()
            return c

        def wait_tail(t, c):
            tile_fill(t * MOE_TM).wait()
            return c

        lax.fori_loop(0, N_EXPERTS, start, 0)
        lax.fori_loop(nused_ref[0], MOE_NT, start_tail, 0)
        lax.fori_loop(0, N_EXPERTS, wait, 0)
        lax.fori_loop(nused_ref[0], MOE_NT, wait_tail, 0)

    rt = route_ref[...].T
    hb = h_ref[...]
    slot = i % 2
    used_rows = tot_ref[i] * ROW_CHUNK

    def sort_rows(nrows):
        j = lax.broadcasted_iota(i32, (nrows, 1), 0).astype(f32)
        sel = jnp.where((j == rt[4:5, :]) | (j == rt[5:6, :]), 1.0, 0.0).astype(bf16)
        sorted_ref[slot, 0:nrows, :] = _dot(sel, hb).astype(bf16)

    @pl.when(used_rows <= DISP_ROWS_SHORT)
    def _():
        sort_rows(DISP_ROWS_SHORT)

    @pl.when(used_rows > DISP_ROWS_SHORT)
    def _():
        sort_rows(DISP_ROWS)

    def per_expert(e, c):
        idx = i * N_EXPERTS + e
        n, s0, d0 = nch_ref[idx], off_ref[idx], dst_ref[idx]
        b = 1
        while b <= TM // ROW_CHUNK:
            @pl.when((n & b) != 0)
            def _(b=b):
                r0 = (n & (b - 1)) * ROW_CHUNK
                rows = b * ROW_CHUNK
                pltpu.make_async_copy(
                    sorted_ref.at[slot, pl.ds(pl.multiple_of(s0 + r0, ROW_CHUNK), rows)],
                    xs_ref.at[pl.ds(pl.multiple_of(d0 + r0, ROW_CHUNK), rows)], sem.at[slot]).start()
            b *= 2
        return c

    lax.fori_loop(0, N_EXPERTS, per_expert, 0)

    def drain(tile, s):
        rows = tot_ref[tile] * ROW_CHUNK

        @pl.when(rows > 0)
        def _():
            pltpu.make_async_copy(sorted_ref.at[s, pl.ds(0, rows)], xs_ref.at[pl.ds(0, rows)],
                                  sem.at[s]).wait()

    @pl.when(i >= 1)
    def _():
        drain(i - 1, 1 - slot)

    @pl.when(i == pl.num_programs(0) - 1)
    def _():
        drain(i, slot)


def _dispatch(nch, off, dst, tot, fill, nused, h2, route):
    grid_spec = pltpu.PrefetchScalarGridSpec(
        num_scalar_prefetch=6,
        grid=(T_ALL // TM,),
        in_specs=[pl.BlockSpec((TM, D_MODEL), lambda i, *_: (i, 0)),
                  pl.BlockSpec((TM, 128), lambda i, *_: (i, 0))],
        out_specs=pl.BlockSpec(memory_space=pl.ANY),
        scratch_shapes=[pltpu.VMEM((2, DISP_ROWS, D_MODEL), bf16), pltpu.VMEM((MOE_TM, D_MODEL), bf16),
                        pltpu.SemaphoreType.DMA((2,)), pltpu.SemaphoreType.DMA(())],
    )
    return pl.pallas_call(
        _dispatch_kernel,
        grid_spec=grid_spec,
        out_shape=jax.ShapeDtypeStruct((MOE_NT * MOE_TM, D_MODEL), bf16),
        compiler_params=_cparams(("arbitrary",)),
        name="dispatch",
    )(nch, off, dst, tot, fill, nused, h2, route)


def _experts_kernel(cnt_ref, nused_ref, xs_ref, wg_hbm, wu_hbm, wd_hbm, ys_ref,
                    wg_f, wu_f, wd_f, wg_b, wu_b, wd_b, st, wsem, *, layer):
    i = pl.program_id(0)
    nused = nused_ref[0]
    NXT, NSLOT, LEFT, ROWS = 0, 1, 2, 3

    def w_copies(e, slot):
        return (pltpu.make_async_copy(wg_hbm.at[layer, e], wg_f.at[slot], wsem.at[slot, 0]),
                pltpu.make_async_copy(wu_hbm.at[layer, e], wu_f.at[slot], wsem.at[slot, 1]),
                pltpu.make_async_copy(wd_hbm.at[layer, e], wd_f.at[slot], wsem.at[slot, 2]))

    def next_nonempty(e):
        return lax.while_loop(
            lambda v: (v < N_EXPERTS) & (cnt_ref[jnp.minimum(v, N_EXPERTS - 1)] == 0),
            lambda v: v + 1, e)

    @pl.when(i == 0)
    def _():
        e0 = next_nonempty(jnp.int32(0))
        for c in w_copies(e0, 0):
            c.start()
        st[NXT] = e0
        st[NSLOT] = 0
        st[LEFT] = 0

    @pl.when(i < nused)
    def _():
        @pl.when(st[LEFT] == 0)
        def _():
            e = st[NXT]
            slot = st[NSLOT]
            for c in w_copies(e, slot):
                c.wait()
            e2 = next_nonempty(e + 1)

            @pl.when(e2 < N_EXPERTS)
            def _():
                for c in w_copies(e2, 1 - slot):
                    c.start()

            st[NXT] = e2
            st[NSLOT] = 1 - slot
            st[LEFT] = (cnt_ref[e] + (MOE_TM - 1)) // MOE_TM
            st[ROWS] = cnt_ref[e]
            wg_b[...] = wg_f[slot].astype(bf16)
            wu_b[...] = wu_f[slot].astype(bf16)
            wd_b[...] = wd_f[slot].astype(bf16)

        def ffn(rows):
            x = xs_ref[rows, :]
            hg = _dot(x, wg_b[...])
            hu = _dot(x, wu_b[...])
            act = (hg * jax.nn.sigmoid(hg)) * hu
            ys_ref[rows, :] = _dot(act.astype(bf16), wd_b[...]).astype(bf16)

        half = MOE_TM // 2
        short = st[ROWS] <= half

        @pl.when(short)
        def _():
            ffn(slice(0, half))
            ys_ref[half:, :] = jnp.zeros((MOE_TM - half, D_MODEL), bf16)

        @pl.when(jnp.logical_not(short))
        def _():
            ffn(slice(0, MOE_TM))

        st[LEFT] = st[LEFT] - 1
        st[ROWS] = st[ROWS] - MOE_TM

    @pl.when(i >= nused)
    def _():
        ys_ref[...] = jnp.zeros_like(ys_ref)


def _experts(counts, nused, xs, l, w_gate, w_up, w_down):
    hbm = pl.BlockSpec(memory_space=pl.ANY)
    grid_spec = pltpu.PrefetchScalarGridSpec(
        num_scalar_prefetch=2,
        grid=(MOE_NT,),
        in_specs=[pl.BlockSpec((MOE_TM, D_MODEL), lambda i, c, nu: (jnp.minimum(i, nu[0] - 1), 0)),
                  hbm, hbm, hbm],
        out_specs=pl.BlockSpec((MOE_TM, D_MODEL), lambda i, c, nu: (i, 0)),
        scratch_shapes=[
            pltpu.VMEM((2, D_MODEL, EXPERT_FF), f32),
            pltpu.VMEM((2, D_MODEL, EXPERT_FF), f32),
            pltpu.VMEM((2, EXPERT_FF, D_MODEL), f32),
            pltpu.VMEM((D_MODEL, EXPERT_FF), bf16),
            pltpu.VMEM((D_MODEL, EXPERT_FF), bf16),
            pltpu.VMEM((EXPERT_FF, D_MODEL), bf16),
            pltpu.SMEM((4,), i32),
            pltpu.SemaphoreType.DMA((2, 3)),
        ],
    )
    return pl.pallas_call(
        functools.partial(_experts_kernel, layer=l),
        grid_spec=grid_spec,
        out_shape=jax.ShapeDtypeStruct((MOE_NT * MOE_TM, D_MODEL), bf16),
        compiler_params=_cparams(("arbitrary",)),
        name="experts",
    )(counts, nused, xs, w_gate, w_up, w_down)


def _combine_kernel(nch_ref, off_ref, dst_ref, tot_ref, x1_ref, mod_ref, route_ref, ys_hbm, g_ref, b_ref,
                    oc_ref, ol_ref, ybuf, moe_ref, sem):
    i = pl.program_id(0)
    nt = pl.num_programs(0) - 1

    @pl.when(i == 0)
    def _():
        ybuf[...] = jnp.zeros_like(ybuf)

    @pl.when(i < nt)
    def _():
        slot = i % 2

        def per_expert(e, c):
            idx = i * N_EXPERTS + e
            n, s0, d0 = nch_ref[idx], off_ref[idx], dst_ref[idx]
            b = 1
            while b <= TM // ROW_CHUNK:
                @pl.when((n & b) != 0)
                def _(b=b):
                    r0 = (n & (b - 1)) * ROW_CHUNK
                    rows = b * ROW_CHUNK
                    pltpu.make_async_copy(
                        ys_hbm.at[pl.ds(pl.multiple_of(d0 + r0, ROW_CHUNK), rows)],
                        ybuf.at[slot, pl.ds(pl.multiple_of(s0 + r0, ROW_CHUNK), rows)],
                        sem.at[slot]).start()
                b *= 2
            return c

        lax.fori_loop(0, N_EXPERTS, per_expert, 0)

    @pl.when(i >= 1)
    def _():
        slot = (i - 1) % 2
        rows = tot_ref[i - 1] * ROW_CHUNK

        @pl.when(rows > 0)
        def _():
            pltpu.make_async_copy(ys_hbm.at[pl.ds(0, rows)], ybuf.at[slot, pl.ds(0, rows)],
                                  sem.at[slot]).wait()

        route = route_ref[...]
        mod = mod_ref[...]

        def unsort(nrows):
            lane = lax.broadcasted_iota(i32, (1, nrows), 1).astype(f32)
            wmat = (jnp.where(lane == route[:, 4:5], route[:, 2:3], 0.0)
                    + jnp.where(lane == route[:, 5:6], route[:, 3:4], 0.0))
            moe_ref[...] = _dot(wmat.astype(bf16), ybuf[slot, 0:nrows, :])

        @pl.when(rows <= DISP_ROWS_SHORT)
        def _():
            unsort(DISP_ROWS_SHORT)

        @pl.when(rows > DISP_ROWS_SHORT)
        def _():
            unsort(DISP_ROWS)

        y = _layer_norm(DEEPNORM_ALPHA * x1_ref[...] + mod[5:6] * moe_ref[...], g_ref[...], b_ref[...])

        @pl.when(i - 1 < T_CTX // TM)
        def _():
            oc_ref[...] = y

        @pl.when(i - 1 >= T_CTX // TM)
        def _():
            ol_ref[...] = y


def _combine(nch, off, dst, tot, x1, mod, route, ys, l, w):
    nt = T_ALL // TM
    vec = pl.BlockSpec((None, 1, D_MODEL), lambda i, *_: (l, 0, 0))
    nctx = T_CTX // TM
    prev = lambda i: jnp.maximum(i - 1, 0)
    grid_spec = pltpu.PrefetchScalarGridSpec(
        num_scalar_prefetch=4,
        grid=(nt + 1,),
        in_specs=[
            pl.BlockSpec((TM, D_MODEL), lambda i, *_: (prev(i), 0)),
            pl.BlockSpec((None, None, 6, D_MODEL), lambda i, *_: (l, prev(i) // (SEG // TM), 0, 0)),
            pl.BlockSpec((TM, 128), lambda i, *_: (prev(i), 0)),
            pl.BlockSpec(memory_space=pl.ANY),
            vec, vec,
        ],
        out_specs=[pl.BlockSpec((TM, D_MODEL), lambda i, *_: (jnp.minimum(prev(i), nctx - 1), 0)),
                   pl.BlockSpec((TM, D_MODEL), lambda i, *_: (jnp.maximum(prev(i) - nctx, 0), 0))],
        scratch_shapes=[pltpu.VMEM((2, DISP_ROWS, D_MODEL), bf16), pltpu.VMEM((TM, D_MODEL), f32),
                        pltpu.SemaphoreType.DMA((2,))],
    )
    return pl.pallas_call(
        _combine_kernel,
        grid_spec=grid_spec,
        out_shape=[jax.ShapeDtypeStruct((T_CTX, D_MODEL), f32),
                   jax.ShapeDtypeStruct((T_LAT, D_MODEL), f32)],
        compiler_params=_cparams(("arbitrary",)),
        name="combine",
    )(nch, off, dst, tot, x1, mod, route, ys, w["ln2_g"], w["ln2_b"])


def _dft_cos_sin(n, scale):
    k = jnp.arange(n, dtype=i32)
    ang = ((k[:, None] * k[None, :]) % n).astype(f32) * np.float32(2.0 * np.pi / n)
    return jnp.cos(ang) * scale, jnp.sin(ang) * scale


def _seq_dft_matrix(n):
    g = min(DFT_SPLIT, n)
    j = jnp.arange(n, dtype=i32)[None, :]
    k1 = jnp.arange(n // g, dtype=i32)[:, None]
    k2 = jnp.arange(g, dtype=i32)[:, None]
    ang_a = ((k1 * j) % (n // g)).astype(f32) * np.float32(2.0 * np.pi * g / n)
    ang_b = ((k2 * j) % n).astype(f32) * np.float32(2.0 * np.pi / n)
    scale = np.float32(n ** -0.5)
    ca, sa = jnp.cos(ang_a), jnp.sin(ang_a)
    cb, sb = jnp.cos(ang_b) * scale, jnp.sin(ang_b) * scale
    ca2 = jnp.concatenate([ca, ca], axis=1)[:, None, :]
    sa2 = jnp.concatenate([sa, sa], axis=1)[:, None, :]
    cb2 = jnp.concatenate([cb, -sb], axis=1)[None, :, :]
    sb2 = jnp.concatenate([sb, cb], axis=1)[None, :, :]
    return (ca2 * cb2 - sa2 * sb2).astype(bf16).reshape(n, 2 * n)


def _seq_dft_half_tables(n, rows):
    g = DFT_SPLIT
    j = jnp.arange(n, dtype=i32)[None, :]
    k1 = jnp.arange(rows // g, dtype=i32)[:, None]
    k2 = jnp.arange(g, dtype=i32)[:, None]
    ang_a = ((k1 * j) % (n // g)).astype(f32) * np.float32(2.0 * np.pi * g / n)
    ang_b = ((k2 * j) % n).astype(f32) * np.float32(2.0 * np.pi / n)
    scale = np.float32(n ** -0.5)
    ca, sa = jnp.cos(ang_a)[:, None, :], jnp.sin(ang_a)[:, None, :]
    cb, sb = (jnp.cos(ang_b) * scale)[None], (jnp.sin(ang_b) * scale)[None]
    return ((ca * cb - sa * sb).astype(bf16).reshape(rows, n),
            (sa * cb + ca * sb).astype(bf16).reshape(rows, n))


def _rope_tables():
    rows = DEC_SEQ // GRID_W
    row = jnp.repeat(jnp.arange(rows), GRID_W).astype(f32)
    col = jnp.tile(jnp.arange(GRID_W), rows).astype(f32)
    n_freq = HEAD_DIM // 4
    inv = ROPE_THETA ** (-jnp.arange(n_freq, dtype=f32) / n_freq)
    ar = row[:, None] * inv
    ac = col[:, None] * inv
    ang = jnp.concatenate([ar, ar, ac, ac], axis=-1)
    cos = jnp.tile(jnp.cos(ang), (1, N_HEADS))
    sin = jnp.tile(jnp.sin(ang), (1, N_HEADS))
    first = (jnp.arange(ATTN_W) % (HEAD_DIM // 2)) < n_freq
    sin_a = jnp.where(first[None, :], -sin, 0.0)
    sin_b = jnp.where(first[None, :], 0.0, sin)
    ident = jnp.zeros((TM, ATTN_W), f32)
    return (jnp.concatenate([cos, ident + 1.0], axis=0),
            jnp.concatenate([sin_a, ident], axis=0),
            jnp.concatenate([sin_b, ident], axis=0))


def _dup_cache(cache):
    c = jnp.transpose(cache, (1, 3, 0, 2, 4))
    return jnp.concatenate([c, c], axis=-1).astype(bf16)


def kernel(x_prompt, x_sample, cache_k, cache_v, c, c_ctx, w_mod, b_mod, w_in, w_fft, w_pool, pool_scale, sgu_ln_g, sgu_ln_b, w_sgu, b_sgu, q_norm_g, k_norm_g, w_out, ln1_g, ln1_b, w_router_group, b_router_group, w_router_expert, b_router_expert, w_gate, w_up, w_down, ln2_g, ln2_b):
    L = DEPTH
    x_ctx = x_prompt.reshape(T_CTX, D_MODEL)
    x_lat = x_sample.reshape(T_LAT, D_MODEL)

    cond8 = jnp.concatenate([c_ctx[None, :], c, jnp.zeros((8 - 1 - DEC_BATCH, D_MODEL), f32)], axis=0)
    mod = _modulation(cond8, w_mod, b_mod)[:, :N_SEG].reshape(L, N_SEG, 6, D_MODEL)

    cc, sc = _dft_cos_sin(FFT_W, np.float32(FFT_W ** -0.5))
    rope_cos, rope_sin_a, rope_sin_b = _rope_tables()
    head_id = jnp.arange(ATTN_W) // HEAD_DIM
    eye_g = jnp.eye(len(POOL_WINDOWS), dtype=f32)
    w_r = jnp.zeros((L, D_MODEL, 128), f32)
    w_r = w_r.at[:, :, :N_GROUPS].set(w_router_group).at[:, :, ROUTE_E0:ROUTE_E0 + N_EXPERTS].set(w_router_expert)
    b_r = jnp.zeros((L, 1, 128), f32)
    b_r = b_r.at[:, 0, :N_GROUPS].set(b_router_group).at[:, 0, ROUTE_E0:ROUTE_E0 + N_EXPERTS].set(b_router_expert)
    w_r_hi, w_r_lo = _split_hi_lo(w_r)
    w = {
        "w_in": w_in.astype(bf16),
        "csc": jnp.concatenate([cc, sc], axis=1).astype(bf16),
        "w_sgu": jnp.transpose(w_sgu, (0, 2, 1, 3)).reshape(L, CHUNK, SGU_HEADS * CHUNK).astype(bf16),
        "b_sgu": jnp.repeat(jnp.transpose(b_sgu, (0, 2, 1)), SGU_W // SGU_HEADS, axis=2),
        "sgu_ln_g": sgu_ln_g.reshape(L, 1, SGU_W),
        "sgu_ln_b": sgu_ln_b.reshape(L, 1, SGU_W),
        "q_norm_g": jnp.tile(q_norm_g, (1, N_HEADS)).reshape(L, 1, ATTN_W),
        "k_norm_g": jnp.tile(k_norm_g, (1, N_KV_HEADS)).reshape(L, 1, KV_W),
        "rope_cos": rope_cos, "rope_sin_a": rope_sin_a, "rope_sin_b": rope_sin_b,
        "ones_bd": (head_id[:, None] == head_id[None, :]).astype(bf16),
        "w_pool_bd": jnp.einsum("lgcd,gh->lgchd", w_pool, eye_g).reshape(L, POOL_W, POOL_W).astype(bf16),
        "pool_scale": pool_scale.reshape(L, 1, POOL_W),
        "w_fft": w_fft.astype(bf16),
        "w_out": w_out.astype(bf16),
        "ln1_g": ln1_g.reshape(L, 1, D_MODEL), "ln1_b": ln1_b.reshape(L, 1, D_MODEL),
        "ln2_g": ln2_g.reshape(L, 1, D_MODEL), "ln2_b": ln2_b.reshape(L, 1, D_MODEL),
        "w_r": jnp.concatenate([w_r_hi, w_r_lo], axis=-1), "b_r": b_r,
        "tril": (jnp.arange(TM)[:, None] > jnp.arange(TM)[None, :]).astype(bf16),
    }
    m_ctx = _seq_dft_matrix(SEQ)
    mc_lat, ms_lat = _seq_dft_half_tables(DEC_SEQ, DFT_HALF_ROWS)
    kc_all = _dup_cache(cache_k)
    vc_all = _dup_cache(cache_v)

    new_k, new_v = [], []
    for l in range(L):
        pq, praw, sgu, q, kd, vd, nk, nv = _inproj(x_ctx, x_lat, mod, l, w)
        new_k.append(nk[:T_CTX].reshape(BATCH, SEQ, N_KV_HEADS, HEAD_DIM))
        new_v.append(nv[:T_CTX].reshape(BATCH, SEQ, N_KV_HEADS, HEAD_DIM))
        po = _pool(praw, l, w)
        fo_ctx = _seqdft(pq, m_ctx, l, w, n=SEQ, tr=SEQ, nseq=BATCH, row0=0)
        fo_lat = _seqdft_half(pq, mc_lat, ms_lat, l, w)
        ao_ctx = _attention(q, kd, vd, None, n=SEQ, tq=SEQ, nseq=BATCH, row0=0, heads=N_KV_HEADS)
        ao_lat = _attention(q, kd, vd, (kc_all[l], vc_all[l]), n=DEC_SEQ, tq=ATT_TQ, nseq=DEC_BATCH,
                            row0=T_CTX, heads=1)
        x1, h2, route, cnt, tab = _outproj(x_ctx, x_lat, mod, fo_ctx, fo_lat, po, sgu, ao_ctx, ao_lat, l, w)
        meta = _plan(cnt)
        experts = slice(ROUTE_E0, ROUTE_E0 + N_EXPERTS)
        counts = meta[0, experts]
        nused = meta[1, :1]
        fill = meta[2, experts]
        tab = tab[:, :, experts].astype(i32)
        nch = (tab[:, 0] // ROW_CHUNK).reshape(-1)
        off = tab[:, 1].reshape(-1)
        dst = (meta[3, experts][None, :] + tab[:, 2]).reshape(-1)
        tot = jnp.sum(tab[:, 0], axis=1) // ROW_CHUNK
        xs = _dispatch(nch, off, dst, tot, fill, nused, h2, route)
        ys = _experts(counts, nused, xs, l, w_gate, w_up, w_down)
        x_ctx, x_lat = _combine(nch, off, dst, tot, x1, mod, route, ys, l, w)

    y_prompt = x_ctx.reshape(BATCH, SEQ, D_MODEL)
    y_sample = x_lat.reshape(DEC_BATCH, DEC_SEQ, D_MODEL)
    return (y_prompt, y_sample, jnp.stack(new_k, axis=1), jnp.stack(new_v, axis=1))
```

```python
import functools

import numpy as np
import jax
import jax.numpy as jnp
from jax import lax
from jax.experimental import pallas as pl
from jax.experimental.pallas import tpu as pltpu

f32 = jnp.float32
bf16 = jnp.bfloat16
i32 = jnp.int32

D_MODEL = 1024
BATCH = 16
SEQ = 256
DEPTH = 4
DEC_BATCH = 2
DEC_SEQ = 4096
PAST_LEN = 512
GRID_W = 64
FFT_W = 256
POOL_W = 256
POOL_WINDOWS = (2, 4, 8, 16)
POOL_GROUP = 64
SGU_W = 256
SGU_HEADS = 4
CHUNK = 128
HEAD_DIM = 64
ATTN_W = 256
N_HEADS = 4
N_KV_HEADS = 2
KV_W = 128
IN_W = 1536
ROPE_THETA = 10000.0
N_GROUPS = 4
EXPERTS_PER_GROUP = 8
N_EXPERTS = 32
EXPERT_FF = 512
DEEPNORM_ALPHA = float((2 * DEPTH) ** 0.25)
LN_EPS = 1e-5
RMS_EPS = 1e-6

T_CTX = BATCH * SEQ
T_LAT = DEC_BATCH * DEC_SEQ
T_ALL = T_CTX + T_LAT
SEG = 4096
N_SEG = T_ALL // SEG

TM = 512
POOL_TB = 1024
POOL_HALO = 8
FFT_TR = 1024
ATT_TQ = 512
ATT_CHUNK = 1024
DFT_SPLIT = 64
MOE_TM = 256
ROW_CHUNK = 16
MOE_ROWS = 2 * T_ALL
MOE_PAD_ROWS = (T_ALL // TM) * N_EXPERTS * (ROW_CHUNK - 1)
MOE_NT = -(-(MOE_ROWS + MOE_PAD_ROWS) // MOE_TM) + N_EXPERTS
DISP_ROWS = 2 * TM + N_EXPERTS * ROW_CHUNK
DISP_ROWS_SHORT = 2 * TM + N_EXPERTS * ROW_CHUNK // 2
ROUTE_E0 = 32
VMEM_LIMIT = 56 * 1024 * 1024


def _cparams(sem):
    return pltpu.CompilerParams(dimension_semantics=sem, vmem_limit_bytes=VMEM_LIMIT)


def _split_hi_lo(a):
    hi = a.astype(bf16)
    lo = (a - hi.astype(f32)).astype(bf16)
    return hi, lo


def _dot(a, b):
    return jnp.dot(a, b, preferred_element_type=f32)


def _mod_kernel(c_ref, w_ref, b_ref, o_ref):
    c = c_ref[...]
    s = c * jax.nn.sigmoid(c)
    s_hi, s_lo = _split_hi_lo(s)
    w_hi, w_lo = _split_hi_lo(w_ref[...])
    o_ref[...] = _dot(s_hi, w_hi) + _dot(s_hi, w_lo) + _dot(s_lo, w_hi) + b_ref[...]


def _modulation(cond8, w_mod, b_mod):
    tn = 1536
    return pl.pallas_call(
        _mod_kernel,
        grid=(DEPTH, 6 * D_MODEL // tn),
        in_specs=[
            pl.BlockSpec((8, D_MODEL), lambda l, j: (0, 0)),
            pl.BlockSpec((None, D_MODEL, tn), lambda l, j: (l, 0, j)),
            pl.BlockSpec((None, 1, tn), lambda l, j: (l, 0, j)),
        ],
        out_specs=pl.BlockSpec((None, 8, tn), lambda l, j: (l, 0, j)),
        out_shape=jax.ShapeDtypeStruct((DEPTH, 8, 6 * D_MODEL), f32),
        compiler_params=_cparams(("arbitrary", "arbitrary")),
        name="modulation",
    )(cond8, w_mod, b_mod.reshape(DEPTH, 1, 6 * D_MODEL))


def _head_rms(x, ones_bd, gain):
    ss = _dot((x * x).astype(bf16), ones_bd)
    return x * lax.rsqrt(ss * (1.0 / HEAD_DIM) + RMS_EPS) * gain


def _rope(x, cos, sin_a, sin_b):
    w = x.shape[-1]
    q4 = HEAD_DIM // 4
    return x * cos + pltpu.roll(x, w - q4, 1) * sin_a + pltpu.roll(x, q4, 1) * sin_b


def _dup_half(x, first):
    lane = lax.broadcasted_iota(i32, x.shape, 1)
    r = pltpu.roll(x, HEAD_DIM, 1)
    if first:
        return jnp.where(lane < HEAD_DIM, x, r)
    return jnp.where(lane >= HEAD_DIM, x, r)


def _gelu_tanh(x):
    c = np.sqrt(2.0 / np.pi).astype(np.float32)
    return x * (0.5 * (1.0 + jnp.tanh(c * (x + 0.044715 * (x * x * x)))))


def _inproj_kernel(xc_ref, xl_ref, mod_ref, win_ref, csc_ref, wsgu_ref, bsgu_ref, lng_ref, lnb_ref,
                   qg_ref, kg_ref, cos_ref, sina_ref, sinb_ref, ones_ref,
                   pq_ref, pool_ref, sgu_ref, q_ref, kd_ref, vd_ref, nk_ref, nv_ref):
    x = jnp.where(pl.program_id(0) < T_CTX // TM, xc_ref[...], xl_ref[...])
    mod = mod_ref[...]
    h = (x * (1.0 + mod[1:2]) + mod[0:1]).astype(bf16)
    proj = _dot(h, win_ref[...])

    a = proj[:, 0:FFT_W].astype(bf16)
    pq_ref[...] = _dot(a, csc_ref[...]).astype(bf16)

    pool_ref[...] = proj[:, FFT_W:FFT_W + POOL_W]

    o = FFT_W + POOL_W
    hgu = _gelu_tanh(proj[:, o:o + 2 * SGU_W])
    u = hgu[:, :SGU_W]
    v = hgu[:, SGU_W:]
    mu = jnp.mean(v, axis=-1, keepdims=True)
    vc = v - mu
    var = jnp.mean(vc * vc, axis=-1, keepdims=True)
    v = vc * lax.rsqrt(var + LN_EPS) * lng_ref[...] + lnb_ref[...]
    lane = lax.broadcasted_iota(i32, (CHUNK, SGU_W), 1)
    head = lane // (SGU_W // SGU_HEADS)
    wcat = wsgu_ref[...]
    for cidx in range(TM // CHUNK):
        rows = slice(cidx * CHUNK, (cidx + 1) * CHUNK)
        vch = v[rows]
        vblk = jnp.concatenate(
            [jnp.where(head == g, vch, 0.0) for g in range(SGU_HEADS)], axis=0).astype(bf16)
        sp = _dot(wcat, vblk) + bsgu_ref[...]
        sgu_ref[rows, :] = (u[rows] * sp).astype(bf16)

    o = o + 2 * SGU_W
    ones_bd = ones_ref[...]
    cos = cos_ref[...]
    sin_a = sina_ref[...]
    sin_b = sinb_ref[...]
    q = _head_rms(proj[:, o:o + ATTN_W], ones_bd, qg_ref[...])
    q = _rope(q, cos, sin_a, sin_b) * np.float32(HEAD_DIM ** -0.5 * np.log2(np.e))
    q_ref[...] = q.astype(bf16)
    o = o + ATTN_W
    k = _head_rms(proj[:, o:o + KV_W], ones_bd[:KV_W, :KV_W], kg_ref[...])
    nk_ref[...] = k
    k = _rope(k, cos[:, :KV_W], sin_a[:, :KV_W], sin_b[:, :KV_W])
    kd_ref[0] = _dup_half(k, True).astype(bf16)
    kd_ref[1] = _dup_half(k, False).astype(bf16)
    o = o + KV_W
    vv = proj[:, o:o + KV_W]
    nv_ref[...] = vv
    vd_ref[0] = _dup_half(vv, True).astype(bf16)
    vd_ref[1] = _dup_half(vv, False).astype(bf16)


def _rope_block(i):
    nlat = DEC_SEQ // TM
    nctx = T_CTX // TM
    return jnp.where(i < nctx, nlat, (i - nctx) % nlat)


def _ctx_tile(wd):
    return pl.BlockSpec((TM, wd), lambda i, *_: (jnp.minimum(i, T_CTX // TM - 1), 0))


def _lat_tile(wd):
    return pl.BlockSpec((TM, wd), lambda i, *_: (jnp.maximum(i - T_CTX // TM, 0), 0))


def _inproj(x_ctx, x_lat, mod, l, w):
    nt = T_ALL // TM
    tile = lambda wd: pl.BlockSpec((TM, wd), lambda i: (i, 0))
    const = lambda shape: pl.BlockSpec(shape, lambda i: (0,) * len(shape))
    rope_spec = pl.BlockSpec((TM, ATTN_W), lambda i: (_rope_block(i), 0))
    return pl.pallas_call(
        _inproj_kernel,
        grid=(nt,),
        in_specs=[
            _ctx_tile(D_MODEL), _lat_tile(D_MODEL),
            pl.BlockSpec((None, None, 6, D_MODEL), lambda i: (l, i // (SEG // TM), 0, 0)),
            pl.BlockSpec((None, D_MODEL, IN_W), lambda i: (l, 0, 0)),
            const((FFT_W, 2 * FFT_W)),
            pl.BlockSpec((None, CHUNK, SGU_HEADS * CHUNK), lambda i: (l, 0, 0)),
            pl.BlockSpec((None, CHUNK, SGU_W), lambda i: (l, 0, 0)),
            pl.BlockSpec((None, 1, SGU_W), lambda i: (l, 0, 0)),
            pl.BlockSpec((None, 1, SGU_W), lambda i: (l, 0, 0)),
            pl.BlockSpec((None, 1, ATTN_W), lambda i: (l, 0, 0)),
            pl.BlockSpec((None, 1, KV_W), lambda i: (l, 0, 0)),
            rope_spec, rope_spec, rope_spec,
            const((ATTN_W, ATTN_W)),
        ],
        out_specs=[
            tile(2 * FFT_W), tile(POOL_W), tile(SGU_W), tile(ATTN_W),
            pl.BlockSpec((N_KV_HEADS, TM, KV_W), lambda i: (0, i, 0)),
            pl.BlockSpec((N_KV_HEADS, TM, KV_W), lambda i: (0, i, 0)),
            tile(KV_W), tile(KV_W),
        ],
        out_shape=[
            jax.ShapeDtypeStruct((T_ALL, 2 * FFT_W), bf16),
            jax.ShapeDtypeStruct((T_ALL, POOL_W), f32),
            jax.ShapeDtypeStruct((T_ALL, SGU_W), bf16),
            jax.ShapeDtypeStruct((T_ALL, ATTN_W), bf16),
            jax.ShapeDtypeStruct((N_KV_HEADS, T_ALL, KV_W), bf16),
            jax.ShapeDtypeStruct((N_KV_HEADS, T_ALL, KV_W), bf16),
            jax.ShapeDtypeStruct((T_ALL, KV_W), f32),
            jax.ShapeDtypeStruct((T_ALL, KV_W), f32),
        ],
        compiler_params=_cparams(("arbitrary",)),
        name="inproj",
    )(x_ctx, x_lat, mod, w["w_in"], w["csc"], w["w_sgu"], w["b_sgu"], w["sgu_ln_g"], w["sgu_ln_b"],
      w["q_norm_g"], w["k_norm_g"], w["rope_cos"], w["rope_sin_a"], w["rope_sin_b"], w["ones_bd"])


def _pool_kernel(prev_ref, cur_ref, next_ref, wp_ref, scale_ref, o_ref):
    i = pl.program_id(0)
    n = jnp.where(i < T_CTX // POOL_TB, SEQ, DEC_SEQ)
    hl = POOL_HALO
    ext = jnp.concatenate([prev_ref[POOL_TB - hl:, :], cur_ref[...], next_ref[:hl, :]], axis=0)
    rows = POOL_TB + 2 * hl
    r = lax.broadcasted_iota(i32, (rows, 1), 0)
    pos = (i * POOL_TB + r - hl) & (n - 1)

    def back(a, s):
        return jnp.where(pos >= s, pltpu.roll(a, s, 0), 0.0)

    def fwd(a, s):
        return jnp.where(pos + s < n, pltpu.roll(a, rows - s, 0), 0.0)

    bsum = [back(ext, 1)]
    fsum = [ext]
    for k in range(3):
        s = 1 << k
        bsum.append(bsum[k] + back(bsum[k], s))
        fsum.append(fsum[k] + fwd(fsum[k], s))
    lane = lax.broadcasted_iota(i32, (1, POOL_W), 1)
    grp = lane // POOL_GROUP
    win = bsum[3] + fsum[3]
    half = jnp.full((1, POOL_W), POOL_WINDOWS[3] // 2, i32)
    for g in (2, 1, 0):
        win = jnp.where(grp == g, bsum[g] + fsum[g], win)
        half = jnp.where(grp == g, POOL_WINDOWS[g] // 2, half)
    cnt = (jnp.minimum(pos + half, n) - jnp.maximum(pos - half, 0)).astype(f32)
    y = (win / cnt - ext)[hl:hl + POOL_TB]
    o_ref[...] = (_dot(y.astype(bf16), wp_ref[...]) * scale_ref[...]).astype(bf16)


def _pool(p, l, w):
    nt = T_ALL // POOL_TB
    blk = lambda f: pl.BlockSpec((POOL_TB, POOL_W), lambda i: (f(i), 0))
    return pl.pallas_call(
        _pool_kernel,
        grid=(nt,),
        in_specs=[
            blk(lambda i: jnp.maximum(i - 1, 0)), blk(lambda i: i),
            blk(lambda i: jnp.minimum(i + 1, nt - 1)),
            pl.BlockSpec((None, POOL_W, POOL_W), lambda i: (l, 0, 0)),
            pl.BlockSpec((None, 1, POOL_W), lambda i: (l, 0, 0)),
        ],
        out_specs=blk(lambda i: i),
        out_shape=jax.ShapeDtypeStruct((T_ALL, POOL_W), bf16),
        compiler_params=_cparams(("arbitrary",)),
        name="pool",
    )(p, p, p, w["w_pool_bd"], w["pool_scale"])


def _seqdft_kernel(*refs, n, nseq):
    m_ref, pq_refs, w_ref, o_ref = refs[0], refs[1:-2], refs[-2], refs[-1]
    per_blk = SEG // n
    tr = m_ref.shape[0]

    def one_sequence(b, rows):
        pq_ref = pq_refs[b // per_blk]
        r0 = (b % per_blk) * n
        f = (_dot(m_ref[:, :n], pq_ref[r0:r0 + n, :FFT_W])
             + _dot(m_ref[:, n:], pq_ref[r0:r0 + n, FFT_W:]))
        o_ref[rows, :] = _dot(f.astype(bf16), w_ref[...]).astype(bf16)

    if tr == n:
        for b in range(nseq):
            one_sequence(b, slice(b * n, (b + 1) * n))
    else:
        for b in range(nseq):
            @pl.when(pl.program_id(1) == b)
            def _(b=b):
                one_sequence(b, slice(0, tr))


def _seqdft(pq, m, l, w, *, n, tr, nseq, row0):
    nr = n // tr
    nblk = nseq * n // SEG
    pq_specs = [pl.BlockSpec((SEG, 2 * FFT_W), lambda i, b, j=j: (row0 // SEG + j, 0))
                for j in range(nblk)]
    if nr == 1:
        grid, out_spec = (1, 1), pl.BlockSpec((nseq * n, FFT_W), lambda i, b: (0, 0))
    else:
        grid, out_spec = (nr, nseq), pl.BlockSpec((tr, FFT_W), lambda i, b: (b * nr + i, 0))
    return pl.pallas_call(
        functools.partial(_seqdft_kernel, n=n, nseq=nseq),
        grid=grid,
        in_specs=[pl.BlockSpec((tr, 2 * n), lambda i, b: (i, 0))] + pq_specs
        + [pl.BlockSpec((None, FFT_W, FFT_W), lambda i, b: (l, 0, 0))],
        out_specs=out_spec,
        out_shape=jax.ShapeDtypeStruct((nseq * n, FFT_W), bf16),
        compiler_params=_cparams(("arbitrary", "arbitrary")),
        name="seqdft_%d" % n,
    )(m, *([pq] * nblk), w["w_fft"])


def _attn_kernel(*refs, has_cache):
    if has_cache:
        q_ref, k_ref, v_ref, kc_ref, vc_ref, o_ref = refs
    else:
        q_ref, k_ref, v_ref, o_ref = refs
    pair = 2 * HEAD_DIM
    tq = q_ref.shape[0]
    n = k_ref.shape[1]
    nt = (((1,), (1,)), ((), ()))
    chunk = min(n, ATT_CHUNK)
    lane = lax.broadcasted_iota(i32, (tq, pair), 1)
    for h in range(k_ref.shape[0]):
        q = q_ref[:, h * pair:(h + 1) * pair]
        zero = jnp.zeros_like(q)
        qs = jnp.concatenate([jnp.where(lane < HEAD_DIM, q, zero),
                              jnp.where(lane >= HEAD_DIM, q, zero)], axis=0)
        parts = [(k_ref, v_ref, c * chunk, chunk) for c in range(n // chunk)]
        if has_cache:
            parts = [(kc_ref, vc_ref, 0, PAST_LEN)] + parts
        m = jnp.full((2 * tq, 1), -jnp.inf, f32)
        den = jnp.zeros((2 * tq, 1), f32)
        acc = jnp.zeros((2 * tq, pair), f32)
        for kr, vr, off, size in parts:
            s = lax.dot_general(qs, kr[h, off:off + size, :], nt, preferred_element_type=f32)
            m_new = jnp.maximum(m, jnp.max(s, axis=-1, keepdims=True))
            alpha = jnp.exp2(m - m_new)
            p = jnp.exp2(s - m_new).astype(bf16)
            den = alpha * den + jnp.sum(p.astype(f32), axis=-1, keepdims=True)
            acc = alpha * acc + _dot(p, vr[h, off:off + size, :])
            m = m_new
        out = acc / den
        o_ref[:, h * pair:(h + 1) * pair] = jnp.where(lane < HEAD_DIM, out[:tq], out[tq:]).astype(bf16)


def _attention(q, kd, vd, cache, *, n, tq, nseq, row0, heads):
    nq = n // tq
    b0 = row0 // n
    q0 = row0 // tq
    in_specs = [
        pl.BlockSpec((tq, heads * 2 * HEAD_DIM), lambda b, h, i: (q0 + b * nq + i, h)),
        pl.BlockSpec((heads, n, KV_W), lambda b, h, i: (h, b0 + b, 0)),
        pl.BlockSpec((heads, n, KV_W), lambda b, h, i: (h, b0 + b, 0)),
    ]
    args = [q, kd, vd]
    if cache is not None:
        cspec = pl.BlockSpec((heads, None, PAST_LEN, KV_W), lambda b, h, i: (h, b, 0, 0))
        in_specs += [cspec, cspec]
        args += list(cache)
    return pl.pallas_call(
        functools.partial(_attn_kernel, has_cache=cache is not None),
        grid=(nseq, N_KV_HEADS // heads, nq),
        in_specs=in_specs,
        out_specs=pl.BlockSpec((tq, heads * 2 * HEAD_DIM), lambda b, h, i: (b * nq + i, h)),
        out_shape=jax.ShapeDtypeStruct((nseq * n, ATTN_W), bf16),
        compiler_params=_cparams(("arbitrary", "arbitrary", "arbitrary")),
        name="attention_%d" % n,
    )(*args)


def _layer_norm(x, g, b):
    mu = jnp.mean(x, axis=-1, keepdims=True)
    xc = x - mu
    var = jnp.mean(xc * xc, axis=-1, keepdims=True)
    return xc * lax.rsqrt(var + LN_EPS) * g + b


def _outproj_kernel(xc_ref, xl_ref, mod_ref, fc_ref, fl_ref, p_ref, s_ref, ac_ref, al_ref, wout_ref,
                    g_ref, b_ref, wr_ref, br_ref, tril_ref,
                    x1_ref, h2_ref, route_ref, cnt_ref, tab_ref, carry_ref):
    i = pl.program_id(0)

    @pl.when(i == 0)
    def _():
        carry_ref[...] = jnp.zeros_like(carry_ref)

    mod = mod_ref[...]
    is_ctx = i < T_CTX // TM
    f_mix = jnp.where(is_ctx, fc_ref[...], fl_ref[...])
    a_mix = jnp.where(is_ctx, ac_ref[...], al_ref[...])
    mix = _dot(jnp.concatenate([f_mix, p_ref[...], s_ref[...], a_mix], axis=1), wout_ref[...])
    x = jnp.where(is_ctx, xc_ref[...], xl_ref[...])
    x1 = _layer_norm(DEEPNORM_ALPHA * x + mod[2:3] * mix, g_ref[...], b_ref[...])
    x1_ref[...] = x1
    h2 = x1 * (1.0 + mod[4:5]) + mod[3:4]
    h_hi, h_lo = _split_hi_lo(h2)
    h2_ref[...] = h_hi
    hw = _dot(h_hi, wr_ref[...])
    logits = hw[:, :128] + hw[:, 128:] + _dot(h_lo, wr_ref[:, :128]) + br_ref[...]
    lane = lax.broadcasted_iota(i32, logits.shape, 1).astype(f32)
    neg = jnp.float32(-jnp.inf)
    big = jnp.float32(1 << 20)
    gl = jnp.where(lane < N_GROUPS, logits, neg)
    gmax = jnp.max(gl, axis=-1, keepdims=True)
    gsel = jnp.min(jnp.where(gl == gmax, lane, big), axis=-1, keepdims=True)
    pg = 1.0 / jnp.sum(jnp.exp(gl - gmax), axis=-1, keepdims=True)
    e_lo = ROUTE_E0 + gsel * EXPERTS_PER_GROUP
    el = jnp.where((lane >= e_lo) & (lane < e_lo + EXPERTS_PER_GROUP), logits, neg)
    v1 = jnp.max(el, axis=-1, keepdims=True)
    i1 = jnp.min(jnp.where(el == v1, lane, big), axis=-1, keepdims=True)
    el2 = jnp.where(lane == i1, neg, el)
    v2 = jnp.max(el2, axis=-1, keepdims=True)
    i2 = jnp.min(jnp.where(el2 == v2, lane, big), axis=-1, keepdims=True)
    e2 = jnp.exp(v2 - v1)
    w1 = pg / (1.0 + e2)
    w2 = pg * e2 / (1.0 + e2)
    oh1 = lane == i1
    oh2 = lane == i2
    oh = jnp.where(oh1 | oh2, 1.0, 0.0)
    lrank = _dot(tril_ref[...], oh.astype(bf16))
    seg = jnp.floor((jnp.sum(oh, axis=0, keepdims=True) + (ROW_CHUNK - 1.0)) * (1.0 / ROW_CHUNK)) * ROW_CHUNK
    seg8 = jnp.broadcast_to(seg, (8, 128))
    lane8 = lax.broadcasted_iota(i32, (8, 128), 1)
    off8 = seg8
    for sh in (1, 2, 4, 8, 16):
        off8 = off8 + jnp.where(lane8 >= sh, pltpu.roll(off8, sh, 1), 0.0)
    off8 = off8 - seg8
    carry = carry_ref[...]
    lpos = lrank + off8[0:1, :]
    pick = lambda sel, val: jnp.sum(jnp.where(sel, val, 0.0), axis=-1, keepdims=True)
    sub8 = lax.broadcasted_iota(i32, (8, 128), 0)
    tab_ref[...] = jnp.where(sub8 == 0, seg8, jnp.where(sub8 == 1, off8, jnp.where(sub8 == 2, carry, 0.0)))
    carry = carry + seg8
    carry_ref[...] = carry
    cnt_ref[...] = carry
    cols = (i1 - ROUTE_E0, i2 - ROUTE_E0, w1, w2, pick(oh1, lpos), pick(oh2, lpos))
    route = jnp.zeros_like(logits)
    for j, col in enumerate(cols):
        route = jnp.where(lane == j, col, route)
    route_ref[...] = route


def _outproj(x_ctx, x_lat, mod, fo_ctx, fo_lat, po, so, ao_ctx, ao_lat, l, w):
    nt = T_ALL // TM
    tile = lambda wd: pl.BlockSpec((TM, wd), lambda i: (i, 0))
    vec = lambda wd: pl.BlockSpec((None, 1, wd), lambda i: (l, 0, 0))
    return pl.pallas_call(
        _outproj_kernel,
        grid=(nt,),
        in_specs=[
            _ctx_tile(D_MODEL), _lat_tile(D_MODEL),
            pl.BlockSpec((None, None, 6, D_MODEL), lambda i: (l, i // (SEG // TM), 0, 0)),
            _ctx_tile(FFT_W), _lat_tile(FFT_W), tile(POOL_W), tile(SGU_W),
            _ctx_tile(ATTN_W), _lat_tile(ATTN_W),
            pl.BlockSpec((None, D_MODEL, D_MODEL), lambda i: (l, 0, 0)),
            vec(D_MODEL), vec(D_MODEL),
            pl.BlockSpec((None, D_MODEL, 256), lambda i: (l, 0, 0)),
            vec(128),
            pl.BlockSpec((TM, TM), lambda i: (0, 0)),
        ],
        out_specs=[tile(D_MODEL), tile(D_MODEL), tile(128), pl.BlockSpec((8, 128), lambda i: (0, 0)),
                   pl.BlockSpec((None, 8, 128), lambda i: (i, 0, 0))],
        out_shape=[
            jax.ShapeDtypeStruct((T_ALL, D_MODEL), f32),
            jax.ShapeDtypeStruct((T_ALL, D_MODEL), bf16),
            jax.ShapeDtypeStruct((T_ALL, 128), f32),
            jax.ShapeDtypeStruct((8, 128), f32),
            jax.ShapeDtypeStruct((nt, 8, 128), f32),
        ],
        scratch_shapes=[pltpu.VMEM((8, 128), f32)],
        compiler_params=_cparams(("arbitrary",)),
        name="outproj",
    )(x_ctx, x_lat, mod, fo_ctx, fo_lat, po, so, ao_ctx, ao_lat,
      w["w_out"], w["ln1_g"], w["ln1_b"], w["w_r"], w["b_r"],
      w["tril"])


def _plan_kernel(cnt_ref, meta_ref):
    lane = lax.broadcasted_iota(i32, (8, 128), 1)
    sub = lax.broadcasted_iota(i32, (8, 128), 0)
    cnt = cnt_ref[...]
    is_e = (lane >= ROUTE_E0) & (lane < ROUTE_E0 + N_EXPERTS)
    tiles = jnp.where(is_e, jnp.floor((cnt + (MOE_TM - 1.0)) * (1.0 / MOE_TM)), 0.0)
    cum = tiles
    for s in (1, 2, 4, 8, 16):
        cum = cum + jnp.where(lane >= s, pltpu.roll(cum, s, 1), 0.0)
    pstart = (cum - tiles) * MOE_TM
    nused = jnp.max(cum, axis=-1, keepdims=True)
    fill = jnp.where(is_e & (cnt != tiles * MOE_TM), pstart + (tiles - 1.0) * MOE_TM, -1.0)
    meta = jnp.where(sub == 0, cnt, jnp.where(sub == 1, nused, jnp.where(sub == 2, fill,
                     jnp.where(sub == 3, pstart, 0.0))))
    meta_ref[...] = meta.astype(i32)


def _plan(cnt):
    return pl.pallas_call(
        _plan_kernel,
        grid=(1,),
        in_specs=[pl.BlockSpec((8, 128), lambda i: (0, 0))],
        out_specs=pl.BlockSpec((8, 128), lambda i: (0, 0)),
        out_shape=jax.ShapeDtypeStruct((8, 128), i32),
        compiler_params=_cparams(("arbitrary",)),
        name="plan",
    )(cnt)


def _dispatch_kernel(nch_ref, off_ref, dst_ref, tot_ref, fill_ref, nused_ref, h_ref, route_ref, xs_ref,
                     sorted_ref, zero_ref, sem, fill_sem):
    i = pl.program_id(0)

    def tile_fill(row0):
        return pltpu.make_async_copy(zero_ref, xs_ref.at[pl.ds(pl.multiple_of(row0, MOE_TM), MOE_TM)],
                                     fill_sem)

    @pl.when(i == 0)
    def _():
        zero_ref[...] = jnp.zeros_like(zero_ref)

        def start(e, c):
            @pl.when(fill_ref[e] >= 0)
            def _():
                tile_fill(jnp.maximum(fill_ref[e], 0)).start()
            return c

        def wait(e, c):
            @pl.when(fill_ref[e] >= 0)
            def _():
                tile_fill(jnp.maximum(fill_ref[e], 0)).wait()
            return c

        def start_tail(t, c):
            tile_fill(t * MOE_TM).start()
            return c

        def wait_tail(t, c):
            tile_fill(t * MOE_TM).wait()
            return c

        lax.fori_loop(0, N_EXPERTS, start, 0)
        lax.fori_loop(nused_ref[0], MOE_NT, start_tail, 0)
        lax.fori_loop(0, N_EXPERTS, wait, 0)
        lax.fori_loop(nused_ref[0], MOE_NT, wait_tail, 0)

    rt = route_ref[...].T
    hb = h_ref[...]
    slot = i % 2
    used_rows = tot_ref[i] * ROW_CHUNK

    def sort_rows(nrows):
        j = lax.broadcasted_iota(i32, (nrows, 1), 0).astype(f32)
        sel = jnp.where((j == rt[4:5, :]) | (j == rt[5:6, :]), 1.0, 0.0).astype(bf16)
        sorted_ref[slot, 0:nrows, :] = _dot(sel, hb).astype(bf16)

    @pl.when(used_rows <= DISP_ROWS_SHORT)
    def _():
        sort_rows(DISP_ROWS_SHORT)

    @pl.when(used_rows > DISP_ROWS_SHORT)
    def _():
        sort_rows(DISP_ROWS)

    def per_expert(e, c):
        idx = i * N_EXPERTS + e
        n, s0, d0 = nch_ref[idx], off_ref[idx], dst_ref[idx]
        b = 1
        while b <= TM // ROW_CHUNK:
            @pl.when((n & b) != 0)
            def _(b=b):
                r0 = (n & (b - 1)) * ROW_CHUNK
                rows = b * ROW_CHUNK
                pltpu.make_async_copy(
                    sorted_ref.at[slot, pl.ds(pl.multiple_of(s0 + r0, ROW_CHUNK), rows)],
                    xs_ref.at[pl.ds(pl.multiple_of(d0 + r0, ROW_CHUNK), rows)], sem.at[slot]).start()
            b *= 2
        return c

    lax.fori_loop(0, N_EXPERTS, per_expert, 0)

    def drain(tile, s):
        rows = tot_ref[tile] * ROW_CHUNK

        @pl.when(rows > 0)
        def _():
            pltpu.make_async_copy(sorted_ref.at[s, pl.ds(0, rows)], xs_ref.at[pl.ds(0, rows)],
                                  sem.at[s]).wait()

    @pl.when(i >= 1)
    def _():
        drain(i - 1, 1 - slot)

    @pl.when(i == pl.num_programs(0) - 1)
    def _():
        drain(i, slot)


def _dispatch(nch, off, dst, tot, fill, nused, h2, route):
    grid_spec = pltpu.PrefetchScalarGridSpec(
        num_scalar_prefetch=6,
        grid=(T_ALL // TM,),
        in_specs=[pl.BlockSpec((TM, D_MODEL), lambda i, *_: (i, 0)),
                  pl.BlockSpec((TM, 128), lambda i, *_: (i, 0))],
        out_specs=pl.BlockSpec(memory_space=pl.ANY),
        scratch_shapes=[pltpu.VMEM((2, DISP_ROWS, D_MODEL), bf16), pltpu.VMEM((MOE_TM, D_MODEL), bf16),
                        pltpu.SemaphoreType.DMA((2,)), pltpu.SemaphoreType.DMA(())],
    )
    return pl.pallas_call(
        _dispatch_kernel,
        grid_spec=grid_spec,
        out_shape=jax.ShapeDtypeStruct((MOE_NT * MOE_TM, D_MODEL), bf16),
        compiler_params=_cparams(("arbitrary",)),
        name="dispatch",
    )(nch, off, dst, tot, fill, nused, h2, route)


def _experts_kernel(cnt_ref, nused_ref, xs_ref, wg_hbm, wu_hbm, wd_hbm, ys_ref,
                    wg_f, wu_f, wd_f, wg_b, wu_b, wd_b, st, wsem, *, layer):
    i = pl.program_id(0)
    nused = nused_ref[0]
    NXT, NSLOT, LEFT, ROWS = 0, 1, 2, 3

    def w_copies(e, slot):
        return (pltpu.make_async_copy(wg_hbm.at[layer, e], wg_f.at[slot], wsem.at[slot, 0]),
                pltpu.make_async_copy(wu_hbm.at[layer, e], wu_f.at[slot], wsem.at[slot, 1]),
                pltpu.make_async_copy(wd_hbm.at[layer, e], wd_f.at[slot], wsem.at[slot, 2]))

    def next_nonempty(e):
        return lax.while_loop(
            lambda v: (v < N_EXPERTS) & (cnt_ref[jnp.minimum(v, N_EXPERTS - 1)] == 0),
            lambda v: v + 1, e)

    @pl.when(i == 0)
    def _():
        e0 = next_nonempty(jnp.int32(0))
        for c in w_copies(e0, 0):
            c.start()
        st[NXT] = e0
        st[NSLOT] = 0
        st[LEFT] = 0

    @pl.when(i < nused)
    def _():
        @pl.when(st[LEFT] == 0)
        def _():
            e = st[NXT]
            slot = st[NSLOT]
            for c in w_copies(e, slot):
                c.wait()
            e2 = next_nonempty(e + 1)

            @pl.when(e2 < N_EXPERTS)
            def _():
                for c in w_copies(e2, 1 - slot):
                    c.start()

            st[NXT] = e2
            st[NSLOT] = 1 - slot
            st[LEFT] = (cnt_ref[e] + (MOE_TM - 1)) // MOE_TM
            st[ROWS] = cnt_ref[e]
            wg_b[...] = wg_f[slot].astype(bf16)
            wu_b[...] = wu_f[slot].astype(bf16)
            wd_b[...] = wd_f[slot].astype(bf16)

        def ffn(rows):
            x = xs_ref[rows, :]
            hg = _dot(x, wg_b[...])
            hu = _dot(x, wu_b[...])
            act = (hg * jax.nn.sigmoid(hg)) * hu
            ys_ref[rows, :] = _dot(act.astype(bf16), wd_b[...]).astype(bf16)

        half = MOE_TM // 2
        short = st[ROWS] <= half

        @pl.when(short)
        def _():
            ffn(slice(0, half))
            ys_ref[half:, :] = jnp.zeros((MOE_TM - half, D_MODEL), bf16)

        @pl.when(jnp.logical_not(short))
        def _():
            ffn(slice(0, MOE_TM))

        st[LEFT] = st[LEFT] - 1
        st[ROWS] = st[ROWS] - MOE_TM

    @pl.when(i >= nused)
    def _():
        ys_ref[...] = jnp.zeros_like(ys_ref)


def _experts(counts, nused, xs, l, w_gate, w_up, w_down):
    hbm = pl.BlockSpec(memory_space=pl.ANY)
    grid_spec = pltpu.PrefetchScalarGridSpec(
        num_scalar_prefetch=2,
        grid=(MOE_NT,),
        in_specs=[pl.BlockSpec((MOE_TM, D_MODEL), lambda i, c, nu: (jnp.minimum(i, nu[0] - 1), 0)),
                  hbm, hbm, hbm],
        out_specs=pl.BlockSpec((MOE_TM, D_MODEL), lambda i, c, nu: (i, 0)),
        scratch_shapes=[
            pltpu.VMEM((2, D_MODEL, EXPERT_FF), f32),
            pltpu.VMEM((2, D_MODEL, EXPERT_FF), f32),
            pltpu.VMEM((2, EXPERT_FF, D_MODEL), f32),
            pltpu.VMEM((D_MODEL, EXPERT_FF), bf16),
            pltpu.VMEM((D_MODEL, EXPERT_FF), bf16),
            pltpu.VMEM((EXPERT_FF, D_MODEL), bf16),
            pltpu.SMEM((4,), i32),
            pltpu.SemaphoreType.DMA((2, 3)),
        ],
    )
    return pl.pallas_call(
        functools.partial(_experts_kernel, layer=l),
        grid_spec=grid_spec,
        out_shape=jax.ShapeDtypeStruct((MOE_NT * MOE_TM, D_MODEL), bf16),
        compiler_params=_cparams(("arbitrary",)),
        name="experts",
    )(counts, nused, xs, w_gate, w_up, w_down)


def _combine_kernel(nch_ref, off_ref, dst_ref, tot_ref, x1_ref, mod_ref, route_ref, ys_hbm, g_ref, b_ref,
                    oc_ref, ol_ref, ybuf, moe_ref, sem):
    i = pl.program_id(0)
    nt = pl.num_programs(0) - 1

    @pl.when(i == 0)
    def _():
        ybuf[...] = jnp.zeros_like(ybuf)

    @pl.when(i < nt)
    def _():
        slot = i % 2

        def per_expert(e, c):
            idx = i * N_EXPERTS + e
            n, s0, d0 = nch_ref[idx], off_ref[idx], dst_ref[idx]
            b = 1
            while b <= TM // ROW_CHUNK:
                @pl.when((n & b) != 0)
                def _(b=b):
                    r0 = (n & (b - 1)) * ROW_CHUNK
                    rows = b * ROW_CHUNK
                    pltpu.make_async_copy(
                        ys_hbm.at[pl.ds(pl.multiple_of(d0 + r0, ROW_CHUNK), rows)],
                        ybuf.at[slot, pl.ds(pl.multiple_of(s0 + r0, ROW_CHUNK), rows)],
                        sem.at[slot]).start()
                b *= 2
            return c

        lax.fori_loop(0, N_EXPERTS, per_expert, 0)

    @pl.when(i >= 1)
    def _():
        slot = (i - 1) % 2
        rows = tot_ref[i - 1] * ROW_CHUNK

        @pl.when(rows > 0)
        def _():
            pltpu.make_async_copy(ys_hbm.at[pl.ds(0, rows)], ybuf.at[slot, pl.ds(0, rows)],
                                  sem.at[slot]).wait()

        route = route_ref[...]
        mod = mod_ref[...]

        def unsort(nrows):
            lane = lax.broadcasted_iota(i32, (1, nrows), 1).astype(f32)
            wmat = (jnp.where(lane == route[:, 4:5], route[:, 2:3], 0.0)
                    + jnp.where(lane == route[:, 5:6], route[:, 3:4], 0.0))
            moe_ref[...] = _dot(wmat.astype(bf16), ybuf[slot, 0:nrows, :])

        @pl.when(rows <= DISP_ROWS_SHORT)
        def _():
            unsort(DISP_ROWS_SHORT)

        @pl.when(rows > DISP_ROWS_SHORT)
        def _():
            unsort(DISP_ROWS)

        y = _layer_norm(DEEPNORM_ALPHA * x1_ref[...] + mod[5:6] * moe_ref[...], g_ref[...], b_ref[...])

        @pl.when(i - 1 < T_CTX // TM)
        def _():
            oc_ref[...] = y

        @pl.when(i - 1 >= T_CTX // TM)
        def _():
            ol_ref[...] = y


def _combine(nch, off, dst, tot, x1, mod, route, ys, l, w):
    nt = T_ALL // TM
    vec = pl.BlockSpec((None, 1, D_MODEL), lambda i, *_: (l, 0, 0))
    nctx = T_CTX // TM
    prev = lambda i: jnp.maximum(i - 1, 0)
    grid_spec = pltpu.PrefetchScalarGridSpec(
        num_scalar_prefetch=4,
        grid=(nt + 1,),
        in_specs=[
            pl.BlockSpec((TM, D_MODEL), lambda i, *_: (prev(i), 0)),
            pl.BlockSpec((None, None, 6, D_MODEL), lambda i, *_: (l, prev(i) // (SEG // TM), 0, 0)),
            pl.BlockSpec((TM, 128), lambda i, *_: (prev(i), 0)),
            pl.BlockSpec(memory_space=pl.ANY),
            vec, vec,
        ],
        out_specs=[pl.BlockSpec((TM, D_MODEL), lambda i, *_: (jnp.minimum(prev(i), nctx - 1), 0)),
                   pl.BlockSpec((TM, D_MODEL), lambda i, *_: (jnp.maximum(prev(i) - nctx, 0), 0))],
        scratch_shapes=[pltpu.VMEM((2, DISP_ROWS, D_MODEL), bf16), pltpu.VMEM((TM, D_MODEL), f32),
                        pltpu.SemaphoreType.DMA((2,))],
    )
    return pl.pallas_call(
        _combine_kernel,
        grid_spec=grid_spec,
        out_shape=[jax.ShapeDtypeStruct((T_CTX, D_MODEL), f32),
                   jax.ShapeDtypeStruct((T_LAT, D_MODEL), f32)],
        compiler_params=_cparams(("arbitrary",)),
        name="combine",
    )(nch, off, dst, tot, x1, mod, route, ys, w["ln2_g"], w["ln2_b"])


def _dft_cos_sin(n, scale):
    k = jnp.arange(n, dtype=i32)
    ang = ((k[:, None] * k[None, :]) % n).astype(f32) * np.float32(2.0 * np.pi / n)
    return jnp.cos(ang) * scale, jnp.sin(ang) * scale


def _seq_dft_matrix(n):
    g = min(DFT_SPLIT, n)
    j = jnp.arange(n, dtype=i32)[None, :]
    k1 = jnp.arange(n // g, dtype=i32)[:, None]
    k2 = jnp.arange(g, dtype=i32)[:, None]
    ang_a = ((k1 * j) % (n // g)).astype(f32) * np.float32(2.0 * np.pi * g / n)
    ang_b = ((k2 * j) % n).astype(f32) * np.float32(2.0 * np.pi / n)
    scale = np.float32(n ** -0.5)
    ca, sa = jnp.cos(ang_a), jnp.sin(ang_a)
    cb, sb = jnp.cos(ang_b) * scale, jnp.sin(ang_b) * scale
    ca2 = jnp.concatenate([ca, ca], axis=1)[:, None, :]
    sa2 = jnp.concatenate([sa, sa], axis=1)[:, None, :]
    cb2 = jnp.concatenate([cb, -sb], axis=1)[None, :, :]
    sb2 = jnp.concatenate([sb, cb], axis=1)[None, :, :]
    return (ca2 * cb2 - sa2 * sb2).astype(bf16).reshape(n, 2 * n)


def _rope_tables():
    rows = DEC_SEQ // GRID_W
    row = jnp.repeat(jnp.arange(rows), GRID_W).astype(f32)
    col = jnp.tile(jnp.arange(GRID_W), rows).astype(f32)
    n_freq = HEAD_DIM // 4
    inv = ROPE_THETA ** (-jnp.arange(n_freq, dtype=f32) / n_freq)
    ar = row[:, None] * inv
    ac = col[:, None] * inv
    ang = jnp.concatenate([ar, ar, ac, ac], axis=-1)
    cos = jnp.tile(jnp.cos(ang), (1, N_HEADS))
    sin = jnp.tile(jnp.sin(ang), (1, N_HEADS))
    first = (jnp.arange(ATTN_W) % (HEAD_DIM // 2)) < n_freq
    sin_a = jnp.where(first[None, :], -sin, 0.0)
    sin_b = jnp.where(first[None, :], 0.0, sin)
    ident = jnp.zeros((TM, ATTN_W), f32)
    return (jnp.concatenate([cos, ident + 1.0], axis=0),
            jnp.concatenate([sin_a, ident], axis=0),
            jnp.concatenate([sin_b, ident], axis=0))


def _dup_cache(cache):
    c = jnp.transpose(cache, (1, 3, 0, 2, 4))
    return jnp.concatenate([c, c], axis=-1).astype(bf16)


def kernel(x_prompt, x_sample, cache_k, cache_v, c, c_ctx, w_mod, b_mod, w_in, w_fft, w_pool, pool_scale, sgu_ln_g, sgu_ln_b, w_sgu, b_sgu, q_norm_g, k_norm_g, w_out, ln1_g, ln1_b, w_router_group, b_router_group, w_router_expert, b_router_expert, w_gate, w_up, w_down, ln2_g, ln2_b):
    L = DEPTH
    x_ctx = x_prompt.reshape(T_CTX, D_MODEL)
    x_lat = x_sample.reshape(T_LAT, D_MODEL)

    cond8 = jnp.concatenate([c_ctx[None, :], c, jnp.zeros((8 - 1 - DEC_BATCH, D_MODEL), f32)], axis=0)
    mod = _modulation(cond8, w_mod, b_mod)[:, :N_SEG].reshape(L, N_SEG, 6, D_MODEL)

    cc, sc = _dft_cos_sin(FFT_W, np.float32(FFT_W ** -0.5))
    rope_cos, rope_sin_a, rope_sin_b = _rope_tables()
    head_id = jnp.arange(ATTN_W) // HEAD_DIM
    eye_g = jnp.eye(len(POOL_WINDOWS), dtype=f32)
    w_r = jnp.zeros((L, D_MODEL, 128), f32)
    w_r = w_r.at[:, :, :N_GROUPS].set(w_router_group).at[:, :, ROUTE_E0:ROUTE_E0 + N_EXPERTS].set(w_router_expert)
    b_r = jnp.zeros((L, 1, 128), f32)
    b_r = b_r.at[:, 0, :N_GROUPS].set(b_router_group).at[:, 0, ROUTE_E0:ROUTE_E0 + N_EXPERTS].set(b_router_expert)
    w_r_hi, w_r_lo = _split_hi_lo(w_r)
    w = {
        "w_in": w_in.astype(bf16),
        "csc": jnp.concatenate([cc, sc], axis=1).astype(bf16),
        "w_sgu": jnp.transpose(w_sgu, (0, 2, 1, 3)).reshape(L, CHUNK, SGU_HEADS * CHUNK).astype(bf16),
        "b_sgu": jnp.repeat(jnp.transpose(b_sgu, (0, 2, 1)), SGU_W // SGU_HEADS, axis=2),
        "sgu_ln_g": sgu_ln_g.reshape(L, 1, SGU_W),
        "sgu_ln_b": sgu_ln_b.reshape(L, 1, SGU_W),
        "q_norm_g": jnp.tile(q_norm_g, (1, N_HEADS)).reshape(L, 1, ATTN_W),
        "k_norm_g": jnp.tile(k_norm_g, (1, N_KV_HEADS)).reshape(L, 1, KV_W),
        "rope_cos": rope_cos, "rope_sin_a": rope_sin_a, "rope_sin_b": rope_sin_b,
        "ones_bd": (head_id[:, None] == head_id[None, :]).astype(bf16),
        "w_pool_bd": jnp.einsum("lgcd,gh->lgchd", w_pool, eye_g).reshape(L, POOL_W, POOL_W).astype(bf16),
        "pool_scale": pool_scale.reshape(L, 1, POOL_W),
        "w_fft": w_fft.astype(bf16),
        "w_out": w_out.astype(bf16),
        "ln1_g": ln1_g.reshape(L, 1, D_MODEL), "ln1_b": ln1_b.reshape(L, 1, D_MODEL),
        "ln2_g": ln2_g.reshape(L, 1, D_MODEL), "ln2_b": ln2_b.reshape(L, 1, D_MODEL),
        "w_r": jnp.concatenate([w_r_hi, w_r_lo], axis=-1), "b_r": b_r,
        "tril": (jnp.arange(TM)[:, None] > jnp.arange(TM)[None, :]).astype(bf16),
    }
    m_ctx = _seq_dft_matrix(SEQ)
    m_lat = _seq_dft_matrix(DEC_SEQ)
    kc_all = _dup_cache(cache_k)
    vc_all = _dup_cache(cache_v)

    new_k, new_v = [], []
    for l in range(L):
        pq, praw, sgu, q, kd, vd, nk, nv = _inproj(x_ctx, x_lat, mod, l, w)
        new_k.append(nk[:T_CTX].reshape(BATCH, SEQ, N_KV_HEADS, HEAD_DIM))
        new_v.append(nv[:T_CTX].reshape(BATCH, SEQ, N_KV_HEADS, HEAD_DIM))
        po = _pool(praw, l, w)
        fo_ctx = _seqdft(pq, m_ctx, l, w, n=SEQ, tr=SEQ, nseq=BATCH, row0=0)
        fo_lat = _seqdft(pq, m_lat, l, w, n=DEC_SEQ, tr=FFT_TR, nseq=DEC_BATCH, row0=T_CTX)
        ao_ctx = _attention(q, kd, vd, None, n=SEQ, tq=SEQ, nseq=BATCH, row0=0, heads=N_KV_HEADS)
        ao_lat = _attention(q, kd, vd, (kc_all[l], vc_all[l]), n=DEC_SEQ, tq=ATT_TQ, nseq=DEC_BATCH,
                            row0=T_CTX, heads=1)
        x1, h2, route, cnt, tab = _outproj(x_ctx, x_lat, mod, fo_ctx, fo_lat, po, sgu, ao_ctx, ao_lat, l, w)
        meta = _plan(cnt)
        experts = slice(ROUTE_E0, ROUTE_E0 + N_EXPERTS)
        counts = meta[0, experts]
        nused = meta[1, :1]
        fill = meta[2, experts]
        tab = tab[:, :, experts].astype(i32)
        nch = (tab[:, 0] // ROW_CHUNK).reshape(-1)
        off = tab[:, 1].reshape(-1)
        dst = (meta[3, experts][None, :] + tab[:, 2]).reshape(-1)
        tot = jnp.sum(tab[:, 0], axis=1) // ROW_CHUNK
        xs = _dispatch(nch, off, dst, tot, fill, nused, h2, route)
        ys = _experts(counts, nused, xs, l, w_gate, w_up, w_down)
        x_ctx, x_lat = _combine(nch, off, dst, tot, x1, mod, route, ys, l, w)

    y_prompt = x_ctx.reshape(BATCH, SEQ, D_MODEL)
    y_sample = x_lat.reshape(DEC_BATCH, DEC_SEQ, D_MODEL)
    return (y_prompt, y_sample, jnp.stack(new_k, axis=1), jnp.stack(new_v, axis=1))
```

```python
import functools

import numpy as np
import jax
import jax.numpy as jnp
from jax import lax
from jax.experimental import pallas as pl
from jax.experimental.pallas import tpu as pltpu

f32 = jnp.float32
bf16 = jnp.bfloat16
i32 = jnp.int32

D_MODEL = 1024
BATCH = 16
SEQ = 256
DEPTH = 4
DEC_BATCH = 2
DEC_SEQ = 4096
PAST_LEN = 512
GRID_W = 64
FFT_W = 256
POOL_W = 256
POOL_WINDOWS = (2, 4, 8, 16)
POOL_GROUP = 64
SGU_W = 256
SGU_HEADS = 4
CHUNK = 128
HEAD_DIM = 64
ATTN_W = 256
N_HEADS = 4
N_KV_HEADS = 2
KV_W = 128
IN_W = 1536
ROPE_THETA = 10000.0
N_GROUPS = 4
EXPERTS_PER_GROUP = 8
N_EXPERTS = 32
EXPERT_FF = 512
DEEPNORM_ALPHA = float((2 * DEPTH) ** 0.25)
LN_EPS = 1e-5
RMS_EPS = 1e-6

T_CTX = BATCH * SEQ
T_LAT = DEC_BATCH * DEC_SEQ
T_ALL = T_CTX + T_LAT
SEG = 4096
N_SEG = T_ALL // SEG

TM = 512
POOL_TB = 1024
POOL_HALO = 8
FFT_TR = 1024
ATT_TQ = 512
ATT_CHUNK = 1024
DFT_SPLIT = 64
MOE_TM = 256
ROW_CHUNK = 16
MOE_ROWS = 2 * T_ALL
MOE_PAD_ROWS = (T_ALL // TM) * N_EXPERTS * (ROW_CHUNK - 1)
MOE_NT = -(-(MOE_ROWS + MOE_PAD_ROWS) // MOE_TM) + N_EXPERTS
DISP_ROWS = 2 * TM + N_EXPERTS * ROW_CHUNK
DISP_ROWS_SHORT = 2 * TM + N_EXPERTS * ROW_CHUNK // 2
ROUTE_E0 = 32
VMEM_LIMIT = 56 * 1024 * 1024


def _cparams(sem):
    return pltpu.CompilerParams(dimension_semantics=sem, vmem_limit_bytes=VMEM_LIMIT)


def _split_hi_lo(a):
    hi = a.astype(bf16)
    lo = (a - hi.astype(f32)).astype(bf16)
    return hi, lo


def _dot(a, b):
    return jnp.dot(a, b, preferred_element_type=f32)


def _mod_kernel(c_ref, w_ref, b_ref, o_ref):
    c = c_ref[...]
    s = c * jax.nn.sigmoid(c)
    s_hi, s_lo = _split_hi_lo(s)
    w_hi, w_lo = _split_hi_lo(w_ref[...])
    o_ref[...] = _dot(s_hi, w_hi) + _dot(s_hi, w_lo) + _dot(s_lo, w_hi) + b_ref[...]


def _modulation(cond8, w_mod, b_mod):
    tn = 1536
    return pl.pallas_call(
        _mod_kernel,
        grid=(DEPTH, 6 * D_MODEL // tn),
        in_specs=[
            pl.BlockSpec((8, D_MODEL), lambda l, j: (0, 0)),
            pl.BlockSpec((None, D_MODEL, tn), lambda l, j: (l, 0, j)),
            pl.BlockSpec((None, 1, tn), lambda l, j: (l, 0, j)),
        ],
        out_specs=pl.BlockSpec((None, 8, tn), lambda l, j: (l, 0, j)),
        out_shape=jax.ShapeDtypeStruct((DEPTH, 8, 6 * D_MODEL), f32),
        compiler_params=_cparams(("arbitrary", "arbitrary")),
        name="modulation",
    )(cond8, w_mod, b_mod.reshape(DEPTH, 1, 6 * D_MODEL))


def _head_rms(x, ones_bd, gain):
    ss = _dot((x * x).astype(bf16), ones_bd)
    return x * lax.rsqrt(ss * (1.0 / HEAD_DIM) + RMS_EPS) * gain


def _rope(x, cos, sin_a, sin_b):
    w = x.shape[-1]
    q4 = HEAD_DIM // 4
    return x * cos + pltpu.roll(x, w - q4, 1) * sin_a + pltpu.roll(x, q4, 1) * sin_b


def _dup_half(x, first):
    lane = lax.broadcasted_iota(i32, x.shape, 1)
    r = pltpu.roll(x, HEAD_DIM, 1)
    if first:
        return jnp.where(lane < HEAD_DIM, x, r)
    return jnp.where(lane >= HEAD_DIM, x, r)


def _gelu_tanh(x):
    c = np.sqrt(2.0 / np.pi).astype(np.float32)
    return x * (0.5 * (1.0 + jnp.tanh(c * (x + 0.044715 * (x * x * x)))))


def _inproj_kernel(xc_ref, xl_ref, mod_ref, win_ref, csc_ref, wsgu_ref, bsgu_ref, lng_ref, lnb_ref,
                   qg_ref, kg_ref, cos_ref, sina_ref, sinb_ref, ones_ref,
                   pq_ref, pool_ref, sgu_ref, q_ref, kd_ref, vd_ref, nk_ref, nv_ref):
    x = jnp.where(pl.program_id(0) < T_CTX // TM, xc_ref[...], xl_ref[...])
    mod = mod_ref[...]
    h = (x * (1.0 + mod[1:2]) + mod[0:1]).astype(bf16)
    proj = _dot(h, win_ref[...])

    a = proj[:, 0:FFT_W].astype(bf16)
    pq_ref[...] = _dot(a, csc_ref[...]).astype(bf16)

    pool_ref[...] = proj[:, FFT_W:FFT_W + POOL_W]

    o = FFT_W + POOL_W
    hgu = _gelu_tanh(proj[:, o:o + 2 * SGU_W])
    u = hgu[:, :SGU_W]
    v = hgu[:, SGU_W:]
    mu = jnp.mean(v, axis=-1, keepdims=True)
    vc = v - mu
    var = jnp.mean(vc * vc, axis=-1, keepdims=True)
    v = vc * lax.rsqrt(var + LN_EPS) * lng_ref[...] + lnb_ref[...]
    lane = lax.broadcasted_iota(i32, (CHUNK, SGU_W), 1)
    head = lane // (SGU_W // SGU_HEADS)
    wcat = wsgu_ref[...]
    for cidx in range(TM // CHUNK):
        rows = slice(cidx * CHUNK, (cidx + 1) * CHUNK)
        vch = v[rows]
        vblk = jnp.concatenate(
            [jnp.where(head == g, vch, 0.0) for g in range(SGU_HEADS)], axis=0).astype(bf16)
        sp = _dot(wcat, vblk) + bsgu_ref[...]
        sgu_ref[rows, :] = (u[rows] * sp).astype(bf16)

    o = o + 2 * SGU_W
    ones_bd = ones_ref[...]
    cos = cos_ref[...]
    sin_a = sina_ref[...]
    sin_b = sinb_ref[...]
    q = _head_rms(proj[:, o:o + ATTN_W], ones_bd, qg_ref[...])
    q = _rope(q, cos, sin_a, sin_b) * np.float32(HEAD_DIM ** -0.5 * np.log2(np.e))
    q_ref[...] = q.astype(bf16)
    o = o + ATTN_W
    k = _head_rms(proj[:, o:o + KV_W], ones_bd[:KV_W, :KV_W], kg_ref[...])
    nk_ref[...] = k
    k = _rope(k, cos[:, :KV_W], sin_a[:, :KV_W], sin_b[:, :KV_W])
    kd_ref[0] = _dup_half(k, True).astype(bf16)
    kd_ref[1] = _dup_half(k, False).astype(bf16)
    o = o + KV_W
    vv = proj[:, o:o + KV_W]
    nv_ref[...] = vv
    vd_ref[0] = _dup_half(vv, True).astype(bf16)
    vd_ref[1] = _dup_half(vv, False).astype(bf16)


def _rope_block(i):
    nlat = DEC_SEQ // TM
    nctx = T_CTX // TM
    return jnp.where(i < nctx, nlat, (i - nctx) % nlat)


def _ctx_tile(wd):
    return pl.BlockSpec((TM, wd), lambda i, *_: (jnp.minimum(i, T_CTX // TM - 1), 0))


def _lat_tile(wd):
    return pl.BlockSpec((TM, wd), lambda i, *_: (jnp.maximum(i - T_CTX // TM, 0), 0))


def _inproj(x_ctx, x_lat, mod, l, w):
    nt = T_ALL // TM
    tile = lambda wd: pl.BlockSpec((TM, wd), lambda i: (i, 0))
    const = lambda shape: pl.BlockSpec(shape, lambda i: (0,) * len(shape))
    rope_spec = pl.BlockSpec((TM, ATTN_W), lambda i: (_rope_block(i), 0))
    return pl.pallas_call(
        _inproj_kernel,
        grid=(nt,),
        in_specs=[
            _ctx_tile(D_MODEL), _lat_tile(D_MODEL),
            pl.BlockSpec((None, None, 6, D_MODEL), lambda i: (l, i // (SEG // TM), 0, 0)),
            pl.BlockSpec((None, D_MODEL, IN_W), lambda i: (l, 0, 0)),
            const((FFT_W, 2 * FFT_W)),
            pl.BlockSpec((None, CHUNK, SGU_HEADS * CHUNK), lambda i: (l, 0, 0)),
            pl.BlockSpec((None, CHUNK, SGU_W), lambda i: (l, 0, 0)),
            pl.BlockSpec((None, 1, SGU_W), lambda i: (l, 0, 0)),
            pl.BlockSpec((None, 1, SGU_W), lambda i: (l, 0, 0)),
            pl.BlockSpec((None, 1, ATTN_W), lambda i: (l, 0, 0)),
            pl.BlockSpec((None, 1, KV_W), lambda i: (l, 0, 0)),
            rope_spec, rope_spec, rope_spec,
            const((ATTN_W, ATTN_W)),
        ],
        out_specs=[
            tile(2 * FFT_W), tile(POOL_W), tile(SGU_W), tile(ATTN_W),
            pl.BlockSpec((N_KV_HEADS, TM, KV_W), lambda i: (0, i, 0)),
            pl.BlockSpec((N_KV_HEADS, TM, KV_W), lambda i: (0, i, 0)),
            tile(KV_W), tile(KV_W),
        ],
        out_shape=[
            jax.ShapeDtypeStruct((T_ALL, 2 * FFT_W), bf16),
            jax.ShapeDtypeStruct((T_ALL, POOL_W), f32),
            jax.ShapeDtypeStruct((T_ALL, SGU_W), bf16),
            jax.ShapeDtypeStruct((T_ALL, ATTN_W), bf16),
            jax.ShapeDtypeStruct((N_KV_HEADS, T_ALL, KV_W), bf16),
            jax.ShapeDtypeStruct((N_KV_HEADS, T_ALL, KV_W), bf16),
            jax.ShapeDtypeStruct((T_ALL, KV_W), f32),
            jax.ShapeDtypeStruct((T_ALL, KV_W), f32),
        ],
        compiler_params=_cparams(("arbitrary",)),
        name="inproj",
    )(x_ctx, x_lat, mod, w["w_in"], w["csc"], w["w_sgu"], w["b_sgu"], w["sgu_ln_g"], w["sgu_ln_b"],
      w["q_norm_g"], w["k_norm_g"], w["rope_cos"], w["rope_sin_a"], w["rope_sin_b"], w["ones_bd"])


def _pool_kernel(prev_ref, cur_ref, next_ref, wp_ref, scale_ref, o_ref):
    i = pl.program_id(0)
    n = jnp.where(i < T_CTX // POOL_TB, SEQ, DEC_SEQ)
    hl = POOL_HALO
    ext = jnp.concatenate([prev_ref[POOL_TB - hl:, :], cur_ref[...], next_ref[:hl, :]], axis=0)
    rows = POOL_TB + 2 * hl
    r = lax.broadcasted_iota(i32, (rows, 1), 0)
    pos = (i * POOL_TB + r - hl) & (n - 1)

    def back(a, s):
        return jnp.where(pos >= s, pltpu.roll(a, s, 0), 0.0)

    def fwd(a, s):
        return jnp.where(pos + s < n, pltpu.roll(a, rows - s, 0), 0.0)

    bsum = [back(ext, 1)]
    fsum = [ext]
    for k in range(3):
        s = 1 << k
        bsum.append(bsum[k] + back(bsum[k], s))
        fsum.append(fsum[k] + fwd(fsum[k], s))
    lane = lax.broadcasted_iota(i32, (1, POOL_W), 1)
    grp = lane // POOL_GROUP
    win = bsum[3] + fsum[3]
    half = jnp.full((1, POOL_W), POOL_WINDOWS[3] // 2, i32)
    for g in (2, 1, 0):
        win = jnp.where(grp == g, bsum[g] + fsum[g], win)
        half = jnp.where(grp == g, POOL_WINDOWS[g] // 2, half)
    cnt = (jnp.minimum(pos + half, n) - jnp.maximum(pos - half, 0)).astype(f32)
    y = (win / cnt - ext)[hl:hl + POOL_TB]
    o_ref[...] = (_dot(y.astype(bf16), wp_ref[...]) * scale_ref[...]).astype(bf16)


def _pool(p, l, w):
    nt = T_ALL // POOL_TB
    blk = lambda f: pl.BlockSpec((POOL_TB, POOL_W), lambda i: (f(i), 0))
    return pl.pallas_call(
        _pool_kernel,
        grid=(nt,),
        in_specs=[
            blk(lambda i: jnp.maximum(i - 1, 0)), blk(lambda i: i),
            blk(lambda i: jnp.minimum(i + 1, nt - 1)),
            pl.BlockSpec((None, POOL_W, POOL_W), lambda i: (l, 0, 0)),
            pl.BlockSpec((None, 1, POOL_W), lambda i: (l, 0, 0)),
        ],
        out_specs=blk(lambda i: i),
        out_shape=jax.ShapeDtypeStruct((T_ALL, POOL_W), bf16),
        compiler_params=_cparams(("arbitrary",)),
        name="pool",
    )(p, p, p, w["w_pool_bd"], w["pool_scale"])


def _seqdft_kernel(*refs, n, nseq):
    m_ref, pq_refs, w_ref, o_ref = refs[0], refs[1:-2], refs[-2], refs[-1]
    per_blk = SEG // n
    tr = m_ref.shape[0]

    def one_sequence(b, rows):
        pq_ref = pq_refs[b // per_blk]
        r0 = (b % per_blk) * n
        f = (_dot(m_ref[:, :n], pq_ref[r0:r0 + n, :FFT_W])
             + _dot(m_ref[:, n:], pq_ref[r0:r0 + n, FFT_W:]))
        o_ref[rows, :] = _dot(f.astype(bf16), w_ref[...]).astype(bf16)

    if tr == n:
        for b in range(nseq):
            one_sequence(b, slice(b * n, (b + 1) * n))
    else:
        for b in range(nseq):
            @pl.when(pl.program_id(1) == b)
            def _(b=b):
                one_sequence(b, slice(0, tr))


def _seqdft(pq, m, l, w, *, n, tr, nseq, row0):
    nr = n // tr
    nblk = nseq * n // SEG
    pq_specs = [pl.BlockSpec((SEG, 2 * FFT_W), lambda i, b, j=j: (row0 // SEG + j, 0))
                for j in range(nblk)]
    if nr == 1:
        grid, out_spec = (1, 1), pl.BlockSpec((nseq * n, FFT_W), lambda i, b: (0, 0))
    else:
        grid, out_spec = (nr, nseq), pl.BlockSpec((tr, FFT_W), lambda i, b: (b * nr + i, 0))
    return pl.pallas_call(
        functools.partial(_seqdft_kernel, n=n, nseq=nseq),
        grid=grid,
        in_specs=[pl.BlockSpec((tr, 2 * n), lambda i, b: (i, 0))] + pq_specs
        + [pl.BlockSpec((None, FFT_W, FFT_W), lambda i, b: (l, 0, 0))],
        out_specs=out_spec,
        out_shape=jax.ShapeDtypeStruct((nseq * n, FFT_W), bf16),
        compiler_params=_cparams(("arbitrary", "arbitrary")),
        name="seqdft_%d" % n,
    )(m, *([pq] * nblk), w["w_fft"])


def _attn_kernel(*refs, has_cache):
    if has_cache:
        q_ref, k_ref, v_ref, kc_ref, vc_ref, o_ref = refs
    else:
        q_ref, k_ref, v_ref, o_ref = refs
    pair = 2 * HEAD_DIM
    tq = q_ref.shape[0]
    n = k_ref.shape[1]
    nt = (((1,), (1,)), ((), ()))
    chunk = min(n, ATT_CHUNK)
    lane = lax.broadcasted_iota(i32, (tq, pair), 1)
    for h in range(k_ref.shape[0]):
        q = q_ref[:, h * pair:(h + 1) * pair]
        zero = jnp.zeros_like(q)
        qs = jnp.concatenate([jnp.where(lane < HEAD_DIM, q, zero),
                              jnp.where(lane >= HEAD_DIM, q, zero)], axis=0)
        parts = [(k_ref, v_ref, c * chunk, chunk) for c in range(n // chunk)]
        if has_cache:
            parts = [(kc_ref, vc_ref, 0, PAST_LEN)] + parts
        m = jnp.full((2 * tq, 1), -jnp.inf, f32)
        den = jnp.zeros((2 * tq, 1), f32)
        acc = jnp.zeros((2 * tq, pair), f32)
        for kr, vr, off, size in parts:
            s = lax.dot_general(qs, kr[h, off:off + size, :], nt, preferred_element_type=f32)
            m_new = jnp.maximum(m, jnp.max(s, axis=-1, keepdims=True))
            alpha = jnp.exp2(m - m_new)
            p = jnp.exp2(s - m_new).astype(bf16)
            den = alpha * den + jnp.sum(p.astype(f32), axis=-1, keepdims=True)
            acc = alpha * acc + _dot(p, vr[h, off:off + size, :])
            m = m_new
        out = acc / den
        o_ref[:, h * pair:(h + 1) * pair] = jnp.where(lane < HEAD_DIM, out[:tq], out[tq:]).astype(bf16)


def _attention(q, kd, vd, cache, *, n, tq, nseq, row0, heads):
    nq = n // tq
    b0 = row0 // n
    q0 = row0 // tq
    in_specs = [
        pl.BlockSpec((tq, heads * 2 * HEAD_DIM), lambda b, h, i: (q0 + b * nq + i, h)),
        pl.BlockSpec((heads, n, KV_W), lambda b, h, i: (h, b0 + b, 0)),
        pl.BlockSpec((heads, n, KV_W), lambda b, h, i: (h, b0 + b, 0)),
    ]
    args = [q, kd, vd]
    if cache is not None:
        cspec = pl.BlockSpec((heads, None, PAST_LEN, KV_W), lambda b, h, i: (h, b, 0, 0))
        in_specs += [cspec, cspec]
        args += list(cache)
    return pl.pallas_call(
        functools.partial(_attn_kernel, has_cache=cache is not None),
        grid=(nseq, N_KV_HEADS // heads, nq),
        in_specs=in_specs,
        out_specs=pl.BlockSpec((tq, heads * 2 * HEAD_DIM), lambda b, h, i: (b * nq + i, h)),
        out_shape=jax.ShapeDtypeStruct((nseq * n, ATTN_W), bf16),
        compiler_params=_cparams(("arbitrary", "arbitrary", "arbitrary")),
        name="attention_%d" % n,
    )(*args)


def _layer_norm(x, g, b):
    mu = jnp.mean(x, axis=-1, keepdims=True)
    xc = x - mu
    var = jnp.mean(xc * xc, axis=-1, keepdims=True)
    return xc * lax.rsqrt(var + LN_EPS) * g + b


def _outproj_kernel(xc_ref, xl_ref, mod_ref, fc_ref, fl_ref, p_ref, s_ref, ac_ref, al_ref, wout_ref,
                    g_ref, b_ref, wr_ref, br_ref, tril_ref,
                    x1_ref, h2_ref, route_ref, cnt_ref, tab_ref, carry_ref):
    i = pl.program_id(0)

    @pl.when(i == 0)
    def _():
        carry_ref[...] = jnp.zeros_like(carry_ref)

    mod = mod_ref[...]
    is_ctx = i < T_CTX // TM
    f_mix = jnp.where(is_ctx, fc_ref[...], fl_ref[...])
    a_mix = jnp.where(is_ctx, ac_ref[...], al_ref[...])
    mix = _dot(jnp.concatenate([f_mix, p_ref[...], s_ref[...], a_mix], axis=1), wout_ref[...])
    x = jnp.where(is_ctx, xc_ref[...], xl_ref[...])
    x1 = _layer_norm(DEEPNORM_ALPHA * x + mod[2:3] * mix, g_ref[...], b_ref[...])
    x1_ref[...] = x1
    h2 = x1 * (1.0 + mod[4:5]) + mod[3:4]
    h_hi, h_lo = _split_hi_lo(h2)
    h2_ref[...] = h_hi
    hw = _dot(h_hi, wr_ref[...])
    logits = hw[:, :128] + hw[:, 128:] + _dot(h_lo, wr_ref[:, :128]) + br_ref[...]
    lane = lax.broadcasted_iota(i32, logits.shape, 1).astype(f32)
    neg = jnp.float32(-jnp.inf)
    big = jnp.float32(1 << 20)
    gl = jnp.where(lane < N_GROUPS, logits, neg)
    gmax = jnp.max(gl, axis=-1, keepdims=True)
    gsel = jnp.min(jnp.where(gl == gmax, lane, big), axis=-1, keepdims=True)
    pg = 1.0 / jnp.sum(jnp.exp(gl - gmax), axis=-1, keepdims=True)
    e_lo = ROUTE_E0 + gsel * EXPERTS_PER_GROUP
    el = jnp.where((lane >= e_lo) & (lane < e_lo + EXPERTS_PER_GROUP), logits, neg)
    v1 = jnp.max(el, axis=-1, keepdims=True)
    i1 = jnp.min(jnp.where(el == v1, lane, big), axis=-1, keepdims=True)
    el2 = jnp.where(lane == i1, neg, el)
    v2 = jnp.max(el2, axis=-1, keepdims=True)
    i2 = jnp.min(jnp.where(el2 == v2, lane, big), axis=-1, keepdims=True)
    e2 = jnp.exp(v2 - v1)
    w1 = pg / (1.0 + e2)
    w2 = pg * e2 / (1.0 + e2)
    oh1 = lane == i1
    oh2 = lane == i2
    oh = jnp.where(oh1 | oh2, 1.0, 0.0)
    lrank = _dot(tril_ref[...], oh.astype(bf16))
    seg = jnp.floor((jnp.sum(oh, axis=0, keepdims=True) + (ROW_CHUNK - 1.0)) * (1.0 / ROW_CHUNK)) * ROW_CHUNK
    seg8 = jnp.broadcast_to(seg, (8, 128))
    lane8 = lax.broadcasted_iota(i32, (8, 128), 1)
    off8 = seg8
    for sh in (1, 2, 4, 8, 16):
        off8 = off8 + jnp.where(lane8 >= sh, pltpu.roll(off8, sh, 1), 0.0)
    off8 = off8 - seg8
    carry = carry_ref[...]
    lpos = lrank + off8[0:1, :]
    pick = lambda sel, val: jnp.sum(jnp.where(sel, val, 0.0), axis=-1, keepdims=True)
    sub8 = lax.broadcasted_iota(i32, (8, 128), 0)
    tab_ref[...] = jnp.where(sub8 == 0, seg8, jnp.where(sub8 == 1, off8, jnp.where(sub8 == 2, carry, 0.0)))
    carry = carry + seg8
    carry_ref[...] = carry
    cnt_ref[...] = carry
    cols = (i1 - ROUTE_E0, i2 - ROUTE_E0, w1, w2, pick(oh1, lpos), pick(oh2, lpos))
    route = jnp.zeros_like(logits)
    for j, col in enumerate(cols):
        route = jnp.where(lane == j, col, route)
    route_ref[...] = route


def _outproj(x_ctx, x_lat, mod, fo_ctx, fo_lat, po, so, ao_ctx, ao_lat, l, w):
    nt = T_ALL // TM
    tile = lambda wd: pl.BlockSpec((TM, wd), lambda i: (i, 0))
    vec = lambda wd: pl.BlockSpec((None, 1, wd), lambda i: (l, 0, 0))
    return pl.pallas_call(
        _outproj_kernel,
        grid=(nt,),
        in_specs=[
            _ctx_tile(D_MODEL), _lat_tile(D_MODEL),
            pl.BlockSpec((None, None, 6, D_MODEL), lambda i: (l, i // (SEG // TM), 0, 0)),
            _ctx_tile(FFT_W), _lat_tile(FFT_W), tile(POOL_W), tile(SGU_W),
            _ctx_tile(ATTN_W), _lat_tile(ATTN_W),
            pl.BlockSpec((None, D_MODEL, D_MODEL), lambda i: (l, 0, 0)),
            vec(D_MODEL), vec(D_MODEL),
            pl.BlockSpec((None, D_MODEL, 256), lambda i: (l, 0, 0)),
            vec(128),
            pl.BlockSpec((TM, TM), lambda i: (0, 0)),
        ],
        out_specs=[tile(D_MODEL), tile(D_MODEL), tile(128), pl.BlockSpec((8, 128), lambda i: (0, 0)),
                   pl.BlockSpec((None, 8, 128), lambda i: (i, 0, 0))],
        out_shape=[
            jax.ShapeDtypeStruct((T_ALL, D_MODEL), f32),
            jax.ShapeDtypeStruct((T_ALL, D_MODEL), bf16),
            jax.ShapeDtypeStruct((T_ALL, 128), f32),
            jax.ShapeDtypeStruct((8, 128), f32),
            jax.ShapeDtypeStruct((nt, 8, 128), f32),
        ],
        scratch_shapes=[pltpu.VMEM((8, 128), f32)],
        compiler_params=_cparams(("arbitrary",)),
        name="outproj",
    )(x_ctx, x_lat, mod, fo_ctx, fo_lat, po, so, ao_ctx, ao_lat,
      w["w_out"], w["ln1_g"], w["ln1_b"], w["w_r"], w["b_r"],
      w["tril"])


def _plan_kernel(cnt_ref, meta_ref):
    lane = lax.broadcasted_iota(i32, (8, 128), 1)
    sub = lax.broadcasted_iota(i32, (8, 128), 0)
    cnt = cnt_ref[...]
    is_e = (lane >= ROUTE_E0) & (lane < ROUTE_E0 + N_EXPERTS)
    tiles = jnp.where(is_e, jnp.floor((cnt + (MOE_TM - 1.0)) * (1.0 / MOE_TM)), 0.0)
    cum = tiles
    for s in (1, 2, 4, 8, 16):
        cum = cum + jnp.where(lane >= s, pltpu.roll(cum, s, 1), 0.0)
    pstart = (cum - tiles) * MOE_TM
    nused = jnp.max(cum, axis=-1, keepdims=True)
    fill = jnp.where(is_e & (cnt != tiles * MOE_TM), pstart + (tiles - 1.0) * MOE_TM, -1.0)
    meta = jnp.where(sub == 0, cnt, jnp.where(sub == 1, nused, jnp.where(sub == 2, fill,
                     jnp.where(sub == 3, pstart, 0.0))))
    meta_ref[...] = meta.astype(i32)


def _plan(cnt):
    return pl.pallas_call(
        _plan_kernel,
        grid=(1,),
        in_specs=[pl.BlockSpec((8, 128), lambda i: (0, 0))],
        out_specs=pl.BlockSpec((8, 128), lambda i: (0, 0)),
        out_shape=jax.ShapeDtypeStruct((8, 128), i32),
        compiler_params=_cparams(("arbitrary",)),
        name="plan",
    )(cnt)


def _dispatch_kernel(nch_ref, off_ref, dst_ref, tot_ref, fill_ref, nused_ref, h_ref, route_ref, xs_ref,
                     sorted_ref, zero_ref, sem, fill_sem):
    i = pl.program_id(0)

    def tile_fill(row0):
        return pltpu.make_async_copy(zero_ref, xs_ref.at[pl.ds(pl.multiple_of(row0, MOE_TM), MOE_TM)],
                                     fill_sem)

    @pl.when(i == 0)
    def _():
        zero_ref[...] = jnp.zeros_like(zero_ref)

        def start(e, c):
            @pl.when(fill_ref[e] >= 0)
            def _():
                tile_fill(jnp.maximum(fill_ref[e], 0)).start()
            return c

        def wait(e, c):
            @pl.when(fill_ref[e] >= 0)
            def _():
                tile_fill(jnp.maximum(fill_ref[e], 0)).wait()
            return c

        def start_tail(t, c):
            tile_fill(t * MOE_TM).start()
            return c

        def wait_tail(t, c):
            tile_fill(t * MOE_TM).wait()
            return c

        lax.fori_loop(0, N_EXPERTS, start, 0)
        lax.fori_loop(nused_ref[0], MOE_NT, start_tail, 0)
        lax.fori_loop(0, N_EXPERTS, wait, 0)
        lax.fori_loop(nused_ref[0], MOE_NT, wait_tail, 0)

    rt = route_ref[...].T
    hb = h_ref[...]
    slot = i % 2
    used_rows = tot_ref[i] * ROW_CHUNK

    def sort_rows(nrows):
        j = lax.broadcasted_iota(i32, (nrows, 1), 0).astype(f32)
        sel = jnp.where((j == rt[4:5, :]) | (j == rt[5:6, :]), 1.0, 0.0).astype(bf16)
        sorted_ref[slot, 0:nrows, :] = _dot(sel, hb).astype(bf16)

    @pl.when(used_rows <= DISP_ROWS_SHORT)
    def _():
        sort_rows(DISP_ROWS_SHORT)

    @pl.when(used_rows > DISP_ROWS_SHORT)
    def _():
        sort_rows(DISP_ROWS)

    def per_expert(e, c):
        idx = i * N_EXPERTS + e
        n, s0, d0 = nch_ref[idx], off_ref[idx], dst_ref[idx]
        b = 1
        while b <= TM // ROW_CHUNK:
            @pl.when((n & b) != 0)
            def _(b=b):
                r0 = (n & (b - 1)) * ROW_CHUNK
                rows = b * ROW_CHUNK
                pltpu.make_async_copy(
                    sorted_ref.at[slot, pl.ds(pl.multiple_of(s0 + r0, ROW_CHUNK), rows)],
                    xs_ref.at[pl.ds(pl.multiple_of(d0 + r0, ROW_CHUNK), rows)], sem.at[slot]).start()
            b *= 2
        return c

    lax.fori_loop(0, N_EXPERTS, per_expert, 0)

    def drain(tile, s):
        rows = tot_ref[tile] * ROW_CHUNK

        @pl.when(rows > 0)
        def _():
            pltpu.make_async_copy(sorted_ref.at[s, pl.ds(0, rows)], xs_ref.at[pl.ds(0, rows)],
                                  sem.at[s]).wait()

    @pl.when(i >= 1)
    def _():
        drain(i - 1, 1 - slot)

    @pl.when(i == pl.num_programs(0) - 1)
    def _():
        drain(i, slot)


def _dispatch(nch, off, dst, tot, fill, nused, h2, route):
    grid_spec = pltpu.PrefetchScalarGridSpec(
        num_scalar_prefetch=6,
        grid=(T_ALL // TM,),
        in_specs=[pl.BlockSpec((TM, D_MODEL), lambda i, *_: (i, 0)),
                  pl.BlockSpec((TM, 128), lambda i, *_: (i, 0))],
        out_specs=pl.BlockSpec(memory_space=pl.ANY),
        scratch_shapes=[pltpu.VMEM((2, DISP_ROWS, D_MODEL), bf16), pltpu.VMEM((MOE_TM, D_MODEL), bf16),
                        pltpu.SemaphoreType.DMA((2,)), pltpu.SemaphoreType.DMA(())],
    )
    return pl.pallas_call(
        _dispatch_kernel,
        grid_spec=grid_spec,
        out_shape=jax.ShapeDtypeStruct((MOE_NT * MOE_TM, D_MODEL), bf16),
        compiler_params=_cparams(("arbitrary",)),
        name="dispatch",
    )(nch, off, dst, tot, fill, nused, h2, route)


def _experts_kernel(cnt_ref, nused_ref, xs_ref, wg_hbm, wu_hbm, wd_hbm, ys_ref,
                    wg_f, wu_f, wd_f, wg_b, wu_b, wd_b, st, wsem, *, layer):
    i = pl.program_id(0)
    nused = nused_ref[0]
    NXT, NSLOT, LEFT, ROWS = 0, 1, 2, 3

    def w_copies(e, slot):
        return (pltpu.make_async_copy(wg_hbm.at[layer, e], wg_f.at[slot], wsem.at[slot, 0]),
                pltpu.make_async_copy(wu_hbm.at[layer, e], wu_f.at[slot], wsem.at[slot, 1]),
                pltpu.make_async_copy(wd_hbm.at[layer, e], wd_f.at[slot], wsem.at[slot, 2]))

    def next_nonempty(e):
        return lax.while_loop(
            lambda v: (v < N_EXPERTS) & (cnt_ref[jnp.minimum(v, N_EXPERTS - 1)] == 0),
            lambda v: v + 1, e)

    @pl.when(i == 0)
    def _():
        e0 = next_nonempty(jnp.int32(0))
        for c in w_copies(e0, 0):
            c.start()
        st[NXT] = e0
        st[NSLOT] = 0
        st[LEFT] = 0

    @pl.when(i < nused)
    def _():
        @pl.when(st[LEFT] == 0)
        def _():
            e = st[NXT]
            slot = st[NSLOT]
            for c in w_copies(e, slot):
                c.wait()
            e2 = next_nonempty(e + 1)

            @pl.when(e2 < N_EXPERTS)
            def _():
                for c in w_copies(e2, 1 - slot):
                    c.start()

            st[NXT] = e2
            st[NSLOT] = 1 - slot
            st[LEFT] = (cnt_ref[e] + (MOE_TM - 1)) // MOE_TM
            st[ROWS] = cnt_ref[e]
            wg_b[...] = wg_f[slot].astype(bf16)
            wu_b[...] = wu_f[slot].astype(bf16)
            wd_b[...] = wd_f[slot].astype(bf16)

        def ffn(rows):
            x = xs_ref[rows, :]
            hg = _dot(x, wg_b[...])
            hu = _dot(x, wu_b[...])
            act = (hg * jax.nn.sigmoid(hg)) * hu
            ys_ref[rows, :] = _dot(act.astype(bf16), wd_b[...]).astype(bf16)

        quarter = MOE_TM // 4
        quarters = jnp.minimum((st[ROWS] + (quarter - 1)) // quarter, 4)
        for nq in range(1, 5):
            @pl.when(quarters == nq)
            def _(nq=nq):
                ffn(slice(0, nq * quarter))
                if nq < 4:
                    ys_ref[nq * quarter:, :] = jnp.zeros((MOE_TM - nq * quarter, D_MODEL), bf16)

        st[LEFT] = st[LEFT] - 1
        st[ROWS] = st[ROWS] - MOE_TM

    @pl.when(i >= nused)
    def _():
        ys_ref[...] = jnp.zeros_like(ys_ref)


def _experts(counts, nused, xs, l, w_gate, w_up, w_down):
    hbm = pl.BlockSpec(memory_space=pl.ANY)
    grid_spec = pltpu.PrefetchScalarGridSpec(
        num_scalar_prefetch=2,
        grid=(MOE_NT,),
        in_specs=[pl.BlockSpec((MOE_TM, D_MODEL), lambda i, c, nu: (jnp.minimum(i, nu[0] - 1), 0)),
                  hbm, hbm, hbm],
        out_specs=pl.BlockSpec((MOE_TM, D_MODEL), lambda i, c, nu: (i, 0)),
        scratch_shapes=[
            pltpu.VMEM((2, D_MODEL, EXPERT_FF), f32),
            pltpu.VMEM((2, D_MODEL, EXPERT_FF), f32),
            pltpu.VMEM((2, EXPERT_FF, D_MODEL), f32),
            pltpu.VMEM((D_MODEL, EXPERT_FF), bf16),
            pltpu.VMEM((D_MODEL, EXPERT_FF), bf16),
            pltpu.VMEM((EXPERT_FF, D_MODEL), bf16),
            pltpu.SMEM((4,), i32),
            pltpu.SemaphoreType.DMA((2, 3)),
        ],
    )
    return pl.pallas_call(
        functools.partial(_experts_kernel, layer=l),
        grid_spec=grid_spec,
        out_shape=jax.ShapeDtypeStruct((MOE_NT * MOE_TM, D_MODEL), bf16),
        compiler_params=_cparams(("arbitrary",)),
        name="experts",
    )(counts, nused, xs, w_gate, w_up, w_down)


def _combine_kernel(nch_ref, off_ref, dst_ref, tot_ref, x1_ref, mod_ref, route_ref, ys_hbm, g_ref, b_ref,
                    oc_ref, ol_ref, ybuf, moe_ref, sem):
    i = pl.program_id(0)
    nt = pl.num_programs(0) - 1

    @pl.when(i == 0)
    def _():
        ybuf[...] = jnp.zeros_like(ybuf)

    @pl.when(i < nt)
    def _():
        slot = i % 2

        def per_expert(e, c):
            idx = i * N_EXPERTS + e
            n, s0, d0 = nch_ref[idx], off_ref[idx], dst_ref[idx]
            b = 1
            while b <= TM // ROW_CHUNK:
                @pl.when((n & b) != 0)
                def _(b=b):
                    r0 = (n & (b - 1)) * ROW_CHUNK
                    rows = b * ROW_CHUNK
                    pltpu.make_async_copy(
                        ys_hbm.at[pl.ds(pl.multiple_of(d0 + r0, ROW_CHUNK), rows)],
                        ybuf.at[slot, pl.ds(pl.multiple_of(s0 + r0, ROW_CHUNK), rows)],
                        sem.at[slot]).start()
                b *= 2
            return c

        lax.fori_loop(0, N_EXPERTS, per_expert, 0)

    @pl.when(i >= 1)
    def _():
        slot = (i - 1) % 2
        rows = tot_ref[i - 1] * ROW_CHUNK

        @pl.when(rows > 0)
        def _():
            pltpu.make_async_copy(ys_hbm.at[pl.ds(0, rows)], ybuf.at[slot, pl.ds(0, rows)],
                                  sem.at[slot]).wait()

        route = route_ref[...]
        mod = mod_ref[...]

        def unsort(nrows):
            lane = lax.broadcasted_iota(i32, (1, nrows), 1).astype(f32)
            wmat = (jnp.where(lane == route[:, 4:5], route[:, 2:3], 0.0)
                    + jnp.where(lane == route[:, 5:6], route[:, 3:4], 0.0))
            moe_ref[...] = _dot(wmat.astype(bf16), ybuf[slot, 0:nrows, :])

        @pl.when(rows <= DISP_ROWS_SHORT)
        def _():
            unsort(DISP_ROWS_SHORT)

        @pl.when(rows > DISP_ROWS_SHORT)
        def _():
            unsort(DISP_ROWS)

        y = _layer_norm(DEEPNORM_ALPHA * x1_ref[...] + mod[5:6] * moe_ref[...], g_ref[...], b_ref[...])

        @pl.when(i - 1 < T_CTX // TM)
        def _():
            oc_ref[...] = y

        @pl.when(i - 1 >= T_CTX // TM)
        def _():
            ol_ref[...] = y


def _combine(nch, off, dst, tot, x1, mod, route, ys, l, w):
    nt = T_ALL // TM
    vec = pl.BlockSpec((None, 1, D_MODEL), lambda i, *_: (l, 0, 0))
    nctx = T_CTX // TM
    prev = lambda i: jnp.maximum(i - 1, 0)
    grid_spec = pltpu.PrefetchScalarGridSpec(
        num_scalar_prefetch=4,
        grid=(nt + 1,),
        in_specs=[
            pl.BlockSpec((TM, D_MODEL), lambda i, *_: (prev(i), 0)),
            pl.BlockSpec((None, None, 6, D_MODEL), lambda i, *_: (l, prev(i) // (SEG // TM), 0, 0)),
            pl.BlockSpec((TM, 128), lambda i, *_: (prev(i), 0)),
            pl.BlockSpec(memory_space=pl.ANY),
            vec, vec,
        ],
        out_specs=[pl.BlockSpec((TM, D_MODEL), lambda i, *_: (jnp.minimum(prev(i), nctx - 1), 0)),
                   pl.BlockSpec((TM, D_MODEL), lambda i, *_: (jnp.maximum(prev(i) - nctx, 0), 0))],
        scratch_shapes=[pltpu.VMEM((2, DISP_ROWS, D_MODEL), bf16), pltpu.VMEM((TM, D_MODEL), f32),
                        pltpu.SemaphoreType.DMA((2,))],
    )
    return pl.pallas_call(
        _combine_kernel,
        grid_spec=grid_spec,
        out_shape=[jax.ShapeDtypeStruct((T_CTX, D_MODEL), f32),
                   jax.ShapeDtypeStruct((T_LAT, D_MODEL), f32)],
        compiler_params=_cparams(("arbitrary",)),
        name="combine",
    )(nch, off, dst, tot, x1, mod, route, ys, w["ln2_g"], w["ln2_b"])


def _dft_cos_sin(n, scale):
    k = jnp.arange(n, dtype=i32)
    ang = ((k[:, None] * k[None, :]) % n).astype(f32) * np.float32(2.0 * np.pi / n)
    return jnp.cos(ang) * scale, jnp.sin(ang) * scale


def _seq_dft_matrix(n):
    g = min(DFT_SPLIT, n)
    j = jnp.arange(n, dtype=i32)[None, :]
    k1 = jnp.arange(n // g, dtype=i32)[:, None]
    k2 = jnp.arange(g, dtype=i32)[:, None]
    ang_a = ((k1 * j) % (n // g)).astype(f32) * np.float32(2.0 * np.pi * g / n)
    ang_b = ((k2 * j) % n).astype(f32) * np.float32(2.0 * np.pi / n)
    scale = np.float32(n ** -0.5)
    ca, sa = jnp.cos(ang_a), jnp.sin(ang_a)
    cb, sb = jnp.cos(ang_b) * scale, jnp.sin(ang_b) * scale
    ca2 = jnp.concatenate([ca, ca], axis=1)[:, None, :]
    sa2 = jnp.concatenate([sa, sa], axis=1)[:, None, :]
    cb2 = jnp.concatenate([cb, -sb], axis=1)[None, :, :]
    sb2 = jnp.concatenate([sb, cb], axis=1)[None, :, :]
    return (ca2 * cb2 - sa2 * sb2).astype(bf16).reshape(n, 2 * n)


def _rope_tables():
    rows = DEC_SEQ // GRID_W
    row = jnp.repeat(jnp.arange(rows), GRID_W).astype(f32)
    col = jnp.tile(jnp.arange(GRID_W), rows).astype(f32)
    n_freq = HEAD_DIM // 4
    inv = ROPE_THETA ** (-jnp.arange(n_freq, dtype=f32) / n_freq)
    ar = row[:, None] * inv
    ac = col[:, None] * inv
    ang = jnp.concatenate([ar, ar, ac, ac], axis=-1)
    cos = jnp.tile(jnp.cos(ang), (1, N_HEADS))
    sin = jnp.tile(jnp.sin(ang), (1, N_HEADS))
    first = (jnp.arange(ATTN_W) % (HEAD_DIM // 2)) < n_freq
    sin_a = jnp.where(first[None, :], -sin, 0.0)
    sin_b = jnp.where(first[None, :], 0.0, sin)
    ident = jnp.zeros((TM, ATTN_W), f32)
    return (jnp.concatenate([cos, ident + 1.0], axis=0),
            jnp.concatenate([sin_a, ident], axis=0),
            jnp.concatenate([sin_b, ident], axis=0))


def _dup_cache(cache):
    c = jnp.transpose(cache, (1, 3, 0, 2, 4))
    return jnp.concatenate([c, c], axis=-1).astype(bf16)


def kernel(x_prompt, x_sample, cache_k, cache_v, c, c_ctx, w_mod, b_mod, w_in, w_fft, w_pool, pool_scale, sgu_ln_g, sgu_ln_b, w_sgu, b_sgu, q_norm_g, k_norm_g, w_out, ln1_g, ln1_b, w_router_group, b_router_group, w_router_expert, b_router_expert, w_gate, w_up, w_down, ln2_g, ln2_b):
    L = DEPTH
    x_ctx = x_prompt.reshape(T_CTX, D_MODEL)
    x_lat = x_sample.reshape(T_LAT, D_MODEL)

    cond8 = jnp.concatenate([c_ctx[None, :], c, jnp.zeros((8 - 1 - DEC_BATCH, D_MODEL), f32)], axis=0)
    mod = _modulation(cond8, w_mod, b_mod)[:, :N_SEG].reshape(L, N_SEG, 6, D_MODEL)

    cc, sc = _dft_cos_sin(FFT_W, np.float32(FFT_W ** -0.5))
    rope_cos, rope_sin_a, rope_sin_b = _rope_tables()
    head_id = jnp.arange(ATTN_W) // HEAD_DIM
    eye_g = jnp.eye(len(POOL_WINDOWS), dtype=f32)
    w_r = jnp.zeros((L, D_MODEL, 128), f32)
    w_r = w_r.at[:, :, :N_GROUPS].set(w_router_group).at[:, :, ROUTE_E0:ROUTE_E0 + N_EXPERTS].set(w_router_expert)
    b_r = jnp.zeros((L, 1, 128), f32)
    b_r = b_r.at[:, 0, :N_GROUPS].set(b_router_group).at[:, 0, ROUTE_E0:ROUTE_E0 + N_EXPERTS].set(b_router_expert)
    w_r_hi, w_r_lo = _split_hi_lo(w_r)
    w = {
        "w_in": w_in.astype(bf16),
        "csc": jnp.concatenate([cc, sc], axis=1).astype(bf16),
        "w_sgu": jnp.transpose(w_sgu, (0, 2, 1, 3)).reshape(L, CHUNK, SGU_HEADS * CHUNK).astype(bf16),
        "b_sgu": jnp.repeat(jnp.transpose(b_sgu, (0, 2, 1)), SGU_W // SGU_HEADS, axis=2),
        "sgu_ln_g": sgu_ln_g.reshape(L, 1, SGU_W),
        "sgu_ln_b": sgu_ln_b.reshape(L, 1, SGU_W),
        "q_norm_g": jnp.tile(q_norm_g, (1, N_HEADS)).reshape(L, 1, ATTN_W),
        "k_norm_g": jnp.tile(k_norm_g, (1, N_KV_HEADS)).reshape(L, 1, KV_W),
        "rope_cos": rope_cos, "rope_sin_a": rope_sin_a, "rope_sin_b": rope_sin_b,
        "ones_bd": (head_id[:, None] == head_id[None, :]).astype(bf16),
        "w_pool_bd": jnp.einsum("lgcd,gh->lgchd", w_pool, eye_g).reshape(L, POOL_W, POOL_W).astype(bf16),
        "pool_scale": pool_scale.reshape(L, 1, POOL_W),
        "w_fft": w_fft.astype(bf16),
        "w_out": w_out.astype(bf16),
        "ln1_g": ln1_g.reshape(L, 1, D_MODEL), "ln1_b": ln1_b.reshape(L, 1, D_MODEL),
        "ln2_g": ln2_g.reshape(L, 1, D_MODEL), "ln2_b": ln2_b.reshape(L, 1, D_MODEL),
        "w_r": jnp.concatenate([w_r_hi, w_r_lo], axis=-1), "b_r": b_r,
        "tril": (jnp.arange(TM)[:, None] > jnp.arange(TM)[None, :]).astype(bf16),
    }
    m_ctx = _seq_dft_matrix(SEQ)
    m_lat = _seq_dft_matrix(DEC_SEQ)
    kc_all = _dup_cache(cache_k)
    vc_all = _dup_cache(cache_v)

    new_k, new_v = [], []
    for l in range(L):
        pq, praw, sgu, q, kd, vd, nk, nv = _inproj(x_ctx, x_lat, mod, l, w)
        new_k.append(nk[:T_CTX].reshape(BATCH, SEQ, N_KV_HEADS, HEAD_DIM))
        new_v.append(nv[:T_CTX].reshape(BATCH, SEQ, N_KV_HEADS, HEAD_DIM))
        po = _pool(praw, l, w)
        fo_ctx = _seqdft(pq, m_ctx, l, w, n=SEQ, tr=SEQ, nseq=BATCH, row0=0)
        fo_lat = _seqdft(pq, m_lat, l, w, n=DEC_SEQ, tr=FFT_TR, nseq=DEC_BATCH, row0=T_CTX)
        ao_ctx = _attention(q, kd, vd, None, n=SEQ, tq=SEQ, nseq=BATCH, row0=0, heads=N_KV_HEADS)
        ao_lat = _attention(q, kd, vd, (kc_all[l], vc_all[l]), n=DEC_SEQ, tq=ATT_TQ, nseq=DEC_BATCH,
                            row0=T_CTX, heads=1)
        x1, h2, route, cnt, tab = _outproj(x_ctx, x_lat, mod, fo_ctx, fo_lat, po, sgu, ao_ctx, ao_lat, l, w)
        meta = _plan(cnt)
        experts = slice(ROUTE_E0, ROUTE_E0 + N_EXPERTS)
        counts = meta[0, experts]
        nused = meta[1, :1]
        fill = meta[2, experts]
        tab = tab[:, :, experts].astype(i32)
        nch = (tab[:, 0] // ROW_CHUNK).reshape(-1)
        off = tab[:, 1].reshape(-1)
        dst = (meta[3, experts][None, :] + tab[:, 2]).reshape(-1)
        tot = jnp.sum(tab[:, 0], axis=1) // ROW_CHUNK
        xs = _dispatch(nch, off, dst, tot, fill, nused, h2, route)
        ys = _experts(counts, nused, xs, l, w_gate, w_up, w_down)
        x_ctx, x_lat = _combine(nch, off, dst, tot, x1, mod, route, ys, l, w)

    y_prompt = x_ctx.reshape(BATCH, SEQ, D_MODEL)
    y_sample = x_lat.reshape(DEC_BATCH, DEC_SEQ, D_MODEL)
    return (y_prompt, y_sample, jnp.stack(new_k, axis=1), jnp.stack(new_v, axis=1))
```

```python
import functools

import numpy as np
import jax
import jax.numpy as jnp
from jax import lax
from jax.experimental import pallas as pl
from jax.experimental.pallas import tpu as pltpu

f32 = jnp.float32
bf16 = jnp.bfloat16
i32 = jnp.int32

D_MODEL = 1024
BATCH = 16
SEQ = 256
DEPTH = 4
DEC_BATCH = 2
DEC_SEQ = 4096
PAST_LEN = 512
GRID_W = 64
FFT_W = 256
POOL_W = 256
POOL_WINDOWS = (2, 4, 8, 16)
POOL_GROUP = 64
SGU_W = 256
SGU_HEADS = 4
CHUNK = 128
HEAD_DIM = 64
ATTN_W = 256
N_HEADS = 4
N_KV_HEADS = 2
KV_W = 128
IN_W = 1536
ROPE_THETA = 10000.0
N_GROUPS = 4
EXPERTS_PER_GROUP = 8
N_EXPERTS = 32
EXPERT_FF = 512
DEEPNORM_ALPHA = float((2 * DEPTH) ** 0.25)
LN_EPS = 1e-5
RMS_EPS = 1e-6

T_CTX = BATCH * SEQ
T_LAT = DEC_BATCH * DEC_SEQ
T_ALL = T_CTX + T_LAT
SEG = 4096
N_SEG = T_ALL // SEG

TM = 512
POOL_TB = 1024
POOL_HALO = 8
FFT_TR = 1024
ATT_TQ = 512
ATT_CHUNK = 1024
DFT_SPLIT = 64
MOE_TM = 512
ROW_CHUNK = 16
MOE_ROWS = 2 * T_ALL
MOE_PAD_ROWS = (T_ALL // TM) * N_EXPERTS * (ROW_CHUNK - 1)
MOE_NT = -(-(MOE_ROWS + MOE_PAD_ROWS) // MOE_TM) + N_EXPERTS
DISP_ROWS = 2 * TM + N_EXPERTS * ROW_CHUNK
DISP_ROWS_SHORT = 2 * TM + N_EXPERTS * ROW_CHUNK // 2
ROUTE_E0 = 32
VMEM_LIMIT = 56 * 1024 * 1024


def _cparams(sem):
    return pltpu.CompilerParams(dimension_semantics=sem, vmem_limit_bytes=VMEM_LIMIT)


def _split_hi_lo(a):
    hi = a.astype(bf16)
    lo = (a - hi.astype(f32)).astype(bf16)
    return hi, lo


def _dot(a, b):
    return jnp.dot(a, b, preferred_element_type=f32)


def _mod_kernel(c_ref, w_ref, b_ref, o_ref):
    c = c_ref[...]
    s = c * jax.nn.sigmoid(c)
    s_hi, s_lo = _split_hi_lo(s)
    w_hi, w_lo = _split_hi_lo(w_ref[...])
    o_ref[...] = _dot(s_hi, w_hi) + _dot(s_hi, w_lo) + _dot(s_lo, w_hi) + b_ref[...]


def _modulation(cond8, w_mod, b_mod):
    tn = 1536
    return pl.pallas_call(
        _mod_kernel,
        grid=(DEPTH, 6 * D_MODEL // tn),
        in_specs=[
            pl.BlockSpec((8, D_MODEL), lambda l, j: (0, 0)),
            pl.BlockSpec((None, D_MODEL, tn), lambda l, j: (l, 0, j)),
            pl.BlockSpec((None, 1, tn), lambda l, j: (l, 0, j)),
        ],
        out_specs=pl.BlockSpec((None, 8, tn), lambda l, j: (l, 0, j)),
        out_shape=jax.ShapeDtypeStruct((DEPTH, 8, 6 * D_MODEL), f32),
        compiler_params=_cparams(("arbitrary", "arbitrary")),
        name="modulation",
    )(cond8, w_mod, b_mod.reshape(DEPTH, 1, 6 * D_MODEL))


def _head_rms(x, ones_bd, gain):
    ss = _dot((x * x).astype(bf16), ones_bd)
    return x * lax.rsqrt(ss * (1.0 / HEAD_DIM) + RMS_EPS) * gain


def _rope(x, cos, sin_a, sin_b):
    w = x.shape[-1]
    q4 = HEAD_DIM // 4
    return x * cos + pltpu.roll(x, w - q4, 1) * sin_a + pltpu.roll(x, q4, 1) * sin_b


def _dup_half(x, first):
    lane = lax.broadcasted_iota(i32, x.shape, 1)
    r = pltpu.roll(x, HEAD_DIM, 1)
    if first:
        return jnp.where(lane < HEAD_DIM, x, r)
    return jnp.where(lane >= HEAD_DIM, x, r)


def _gelu_tanh(x):
    c = np.sqrt(2.0 / np.pi).astype(np.float32)
    return x * (0.5 * (1.0 + jnp.tanh(c * (x + 0.044715 * (x * x * x)))))


def _inproj_kernel(xc_ref, xl_ref, mod_ref, win_ref, csc_ref, wsgu_ref, bsgu_ref, lng_ref, lnb_ref,
                   qg_ref, kg_ref, cos_ref, sina_ref, sinb_ref, ones_ref,
                   pq_ref, pool_ref, sgu_ref, q_ref, kd_ref, vd_ref, nk_ref, nv_ref):
    x = jnp.where(pl.program_id(0) < T_CTX // TM, xc_ref[...], xl_ref[...])
    mod = mod_ref[...]
    h = (x * (1.0 + mod[1:2]) + mod[0:1]).astype(bf16)
    proj = _dot(h, win_ref[...])

    a = proj[:, 0:FFT_W].astype(bf16)
    pq_ref[...] = _dot(a, csc_ref[...]).astype(bf16)

    pool_ref[...] = proj[:, FFT_W:FFT_W + POOL_W]

    o = FFT_W + POOL_W
    hgu = _gelu_tanh(proj[:, o:o + 2 * SGU_W])
    u = hgu[:, :SGU_W]
    v = hgu[:, SGU_W:]
    mu = jnp.mean(v, axis=-1, keepdims=True)
    vc = v - mu
    var = jnp.mean(vc * vc, axis=-1, keepdims=True)
    v = vc * lax.rsqrt(var + LN_EPS) * lng_ref[...] + lnb_ref[...]
    lane = lax.broadcasted_iota(i32, (CHUNK, SGU_W), 1)
    head = lane // (SGU_W // SGU_HEADS)
    wcat = wsgu_ref[...]
    for cidx in range(TM // CHUNK):
        rows = slice(cidx * CHUNK, (cidx + 1) * CHUNK)
        vch = v[rows]
        vblk = jnp.concatenate(
            [jnp.where(head == g, vch, 0.0) for g in range(SGU_HEADS)], axis=0).astype(bf16)
        sp = _dot(wcat, vblk) + bsgu_ref[...]
        sgu_ref[rows, :] = (u[rows] * sp).astype(bf16)

    o = o + 2 * SGU_W
    ones_bd = ones_ref[...]
    cos = cos_ref[...]
    sin_a = sina_ref[...]
    sin_b = sinb_ref[...]
    q = _head_rms(proj[:, o:o + ATTN_W], ones_bd, qg_ref[...])
    q = _rope(q, cos, sin_a, sin_b) * np.float32(HEAD_DIM ** -0.5 * np.log2(np.e))
    q_ref[...] = q.astype(bf16)
    o = o + ATTN_W
    k = _head_rms(proj[:, o:o + KV_W], ones_bd[:KV_W, :KV_W], kg_ref[...])
    nk_ref[...] = k
    k = _rope(k, cos[:, :KV_W], sin_a[:, :KV_W], sin_b[:, :KV_W])
    kd_ref[0] = _dup_half(k, True).astype(bf16)
    kd_ref[1] = _dup_half(k, False).astype(bf16)
    o = o + KV_W
    vv = proj[:, o:o + KV_W]
    nv_ref[...] = vv
    vd_ref[0] = _dup_half(vv, True).astype(bf16)
    vd_ref[1] = _dup_half(vv, False).astype(bf16)


def _rope_block(i):
    nlat = DEC_SEQ // TM
    nctx = T_CTX // TM
    return jnp.where(i < nctx, nlat, (i - nctx) % nlat)


def _ctx_tile(wd):
    return pl.BlockSpec((TM, wd), lambda i, *_: (jnp.minimum(i, T_CTX // TM - 1), 0))


def _lat_tile(wd):
    return pl.BlockSpec((TM, wd), lambda i, *_: (jnp.maximum(i - T_CTX // TM, 0), 0))


def _inproj(x_ctx, x_lat, mod, l, w):
    nt = T_ALL // TM
    tile = lambda wd: pl.BlockSpec((TM, wd), lambda i: (i, 0))
    const = lambda shape: pl.BlockSpec(shape, lambda i: (0,) * len(shape))
    rope_spec = pl.BlockSpec((TM, ATTN_W), lambda i: (_rope_block(i), 0))
    return pl.pallas_call(
        _inproj_kernel,
        grid=(nt,),
        in_specs=[
            _ctx_tile(D_MODEL), _lat_tile(D_MODEL),
            pl.BlockSpec((None, None, 6, D_MODEL), lambda i: (l, i // (SEG // TM), 0, 0)),
            pl.BlockSpec((None, D_MODEL, IN_W), lambda i: (l, 0, 0)),
            const((FFT_W, 2 * FFT_W)),
            pl.BlockSpec((None, CHUNK, SGU_HEADS * CHUNK), lambda i: (l, 0, 0)),
            pl.BlockSpec((None, CHUNK, SGU_W), lambda i: (l, 0, 0)),
            pl.BlockSpec((None, 1, SGU_W), lambda i: (l, 0, 0)),
            pl.BlockSpec((None, 1, SGU_W), lambda i: (l, 0, 0)),
            pl.BlockSpec((None, 1, ATTN_W), lambda i: (l, 0, 0)),
            pl.BlockSpec((None, 1, KV_W), lambda i: (l, 0, 0)),
            rope_spec, rope_spec, rope_spec,
            const((ATTN_W, ATTN_W)),
        ],
        out_specs=[
            tile(2 * FFT_W), tile(POOL_W), tile(SGU_W), tile(ATTN_W),
            pl.BlockSpec((N_KV_HEADS, TM, KV_W), lambda i: (0, i, 0)),
            pl.BlockSpec((N_KV_HEADS, TM, KV_W), lambda i: (0, i, 0)),
            tile(KV_W), tile(KV_W),
        ],
        out_shape=[
            jax.ShapeDtypeStruct((T_ALL, 2 * FFT_W), bf16),
            jax.ShapeDtypeStruct((T_ALL, POOL_W), f32),
            jax.ShapeDtypeStruct((T_ALL, SGU_W), bf16),
            jax.ShapeDtypeStruct((T_ALL, ATTN_W), bf16),
            jax.ShapeDtypeStruct((N_KV_HEADS, T_ALL, KV_W), bf16),
            jax.ShapeDtypeStruct((N_KV_HEADS, T_ALL, KV_W), bf16),
            jax.ShapeDtypeStruct((T_ALL, KV_W), f32),
            jax.ShapeDtypeStruct((T_ALL, KV_W), f32),
        ],
        compiler_params=_cparams(("arbitrary",)),
        name="inproj",
    )(x_ctx, x_lat, mod, w["w_in"], w["csc"], w["w_sgu"], w["b_sgu"], w["sgu_ln_g"], w["sgu_ln_b"],
      w["q_norm_g"], w["k_norm_g"], w["rope_cos"], w["rope_sin_a"], w["rope_sin_b"], w["ones_bd"])


def _pool_kernel(prev_ref, cur_ref, next_ref, wp_ref, scale_ref, o_ref):
    i = pl.program_id(0)
    n = jnp.where(i < T_CTX // POOL_TB, SEQ, DEC_SEQ)
    hl = POOL_HALO
    ext = jnp.concatenate([prev_ref[POOL_TB - hl:, :], cur_ref[...], next_ref[:hl, :]], axis=0)
    rows = POOL_TB + 2 * hl
    r = lax.broadcasted_iota(i32, (rows, 1), 0)
    pos = (i * POOL_TB + r - hl) & (n - 1)

    def back(a, s):
        return jnp.where(pos >= s, pltpu.roll(a, s, 0), 0.0)

    def fwd(a, s):
        return jnp.where(pos + s < n, pltpu.roll(a, rows - s, 0), 0.0)

    bsum = [back(ext, 1)]
    fsum = [ext]
    for k in range(3):
        s = 1 << k
        bsum.append(bsum[k] + back(bsum[k], s))
        fsum.append(fsum[k] + fwd(fsum[k], s))
    lane = lax.broadcasted_iota(i32, (1, POOL_W), 1)
    grp = lane // POOL_GROUP
    win = bsum[3] + fsum[3]
    half = jnp.full((1, POOL_W), POOL_WINDOWS[3] // 2, i32)
    for g in (2, 1, 0):
        win = jnp.where(grp == g, bsum[g] + fsum[g], win)
        half = jnp.where(grp == g, POOL_WINDOWS[g] // 2, half)
    cnt = (jnp.minimum(pos + half, n) - jnp.maximum(pos - half, 0)).astype(f32)
    y = (win / cnt - ext)[hl:hl + POOL_TB]
    o_ref[...] = (_dot(y.astype(bf16), wp_ref[...]) * scale_ref[...]).astype(bf16)


def _pool(p, l, w):
    nt = T_ALL // POOL_TB
    blk = lambda f: pl.BlockSpec((POOL_TB, POOL_W), lambda i: (f(i), 0))
    return pl.pallas_call(
        _pool_kernel,
        grid=(nt,),
        in_specs=[
            blk(lambda i: jnp.maximum(i - 1, 0)), blk(lambda i: i),
            blk(lambda i: jnp.minimum(i + 1, nt - 1)),
            pl.BlockSpec((None, POOL_W, POOL_W), lambda i: (l, 0, 0)),
            pl.BlockSpec((None, 1, POOL_W), lambda i: (l, 0, 0)),
        ],
        out_specs=blk(lambda i: i),
        out_shape=jax.ShapeDtypeStruct((T_ALL, POOL_W), bf16),
        compiler_params=_cparams(("arbitrary",)),
        name="pool",
    )(p, p, p, w["w_pool_bd"], w["pool_scale"])


def _seqdft_kernel(*refs, n, nseq):
    m_ref, pq_refs, w_ref, o_ref = refs[0], refs[1:-2], refs[-2], refs[-1]
    per_blk = SEG // n
    tr = m_ref.shape[0]

    def one_sequence(b, rows):
        pq_ref = pq_refs[b // per_blk]
        r0 = (b % per_blk) * n
        f = (_dot(m_ref[:, :n], pq_ref[r0:r0 + n, :FFT_W])
             + _dot(m_ref[:, n:], pq_ref[r0:r0 + n, FFT_W:]))
        o_ref[rows, :] = _dot(f.astype(bf16), w_ref[...]).astype(bf16)

    if tr == n:
        for b in range(nseq):
            one_sequence(b, slice(b * n, (b + 1) * n))
    else:
        for b in range(nseq):
            @pl.when(pl.program_id(1) == b)
            def _(b=b):
                one_sequence(b, slice(0, tr))


def _seqdft(pq, m, l, w, *, n, tr, nseq, row0):
    nr = n // tr
    nblk = nseq * n // SEG
    pq_specs = [pl.BlockSpec((SEG, 2 * FFT_W), lambda i, b, j=j: (row0 // SEG + j, 0))
                for j in range(nblk)]
    if nr == 1:
        grid, out_spec = (1, 1), pl.BlockSpec((nseq * n, FFT_W), lambda i, b: (0, 0))
    else:
        grid, out_spec = (nr, nseq), pl.BlockSpec((tr, FFT_W), lambda i, b: (b * nr + i, 0))
    return pl.pallas_call(
        functools.partial(_seqdft_kernel, n=n, nseq=nseq),
        grid=grid,
        in_specs=[pl.BlockSpec((tr, 2 * n), lambda i, b: (i, 0))] + pq_specs
        + [pl.BlockSpec((None, FFT_W, FFT_W), lambda i, b: (l, 0, 0))],
        out_specs=out_spec,
        out_shape=jax.ShapeDtypeStruct((nseq * n, FFT_W), bf16),
        compiler_params=_cparams(("arbitrary", "arbitrary")),
        name="seqdft_%d" % n,
    )(m, *([pq] * nblk), w["w_fft"])


def _attn_kernel(*refs, has_cache):
    if has_cache:
        q_ref, k_ref, v_ref, kc_ref, vc_ref, o_ref = refs
    else:
        q_ref, k_ref, v_ref, o_ref = refs
    pair = 2 * HEAD_DIM
    tq = q_ref.shape[0]
    n = k_ref.shape[1]
    nt = (((1,), (1,)), ((), ()))
    chunk = min(n, ATT_CHUNK)
    lane = lax.broadcasted_iota(i32, (tq, pair), 1)
    for h in range(k_ref.shape[0]):
        q = q_ref[:, h * pair:(h + 1) * pair]
        zero = jnp.zeros_like(q)
        qs = jnp.concatenate([jnp.where(lane < HEAD_DIM, q, zero),
                              jnp.where(lane >= HEAD_DIM, q, zero)], axis=0)
        parts = [(k_ref, v_ref, c * chunk, chunk) for c in range(n // chunk)]
        if has_cache:
            parts = [(kc_ref, vc_ref, 0, PAST_LEN)] + parts
        m = jnp.full((2 * tq, 1), -jnp.inf, f32)
        den = jnp.zeros((2 * tq, 1), f32)
        acc = jnp.zeros((2 * tq, pair), f32)
        for kr, vr, off, size in parts:
            s = lax.dot_general(qs, kr[h, off:off + size, :], nt, preferred_element_type=f32)
            m_new = jnp.maximum(m, jnp.max(s, axis=-1, keepdims=True))
            alpha = jnp.exp2(m - m_new)
            p = jnp.exp2(s - m_new).astype(bf16)
            den = alpha * den + jnp.sum(p.astype(f32), axis=-1, keepdims=True)
            acc = alpha * acc + _dot(p, vr[h, off:off + size, :])
            m = m_new
        out = acc / den
        o_ref[:, h * pair:(h + 1) * pair] = jnp.where(lane < HEAD_DIM, out[:tq], out[tq:]).astype(bf16)


def _attention(q, kd, vd, cache, *, n, tq, nseq, row0, heads):
    nq = n // tq
    b0 = row0 // n
    q0 = row0 // tq
    in_specs = [
        pl.BlockSpec((tq, heads * 2 * HEAD_DIM), lambda b, h, i: (q0 + b * nq + i, h)),
        pl.BlockSpec((heads, n, KV_W), lambda b, h, i: (h, b0 + b, 0)),
        pl.BlockSpec((heads, n, KV_W), lambda b, h, i: (h, b0 + b, 0)),
    ]
    args = [q, kd, vd]
    if cache is not None:
        cspec = pl.BlockSpec((heads, None, PAST_LEN, KV_W), lambda b, h, i: (h, b, 0, 0))
        in_specs += [cspec, cspec]
        args += list(cache)
    return pl.pallas_call(
        functools.partial(_attn_kernel, has_cache=cache is not None),
        grid=(nseq, N_KV_HEADS // heads, nq),
        in_specs=in_specs,
        out_specs=pl.BlockSpec((tq, heads * 2 * HEAD_DIM), lambda b, h, i: (b * nq + i, h)),
        out_shape=jax.ShapeDtypeStruct((nseq * n, ATTN_W), bf16),
        compiler_params=_cparams(("arbitrary", "arbitrary", "arbitrary")),
        name="attention_%d" % n,
    )(*args)


def _layer_norm(x, g, b):
    mu = jnp.mean(x, axis=-1, keepdims=True)
    xc = x - mu
    var = jnp.mean(xc * xc, axis=-1, keepdims=True)
    return xc * lax.rsqrt(var + LN_EPS) * g + b


def _outproj_kernel(xc_ref, xl_ref, mod_ref, fc_ref, fl_ref, p_ref, s_ref, ac_ref, al_ref, wout_ref,
                    g_ref, b_ref, wr_ref, br_ref, tril_ref,
                    x1_ref, h2_ref, route_ref, cnt_ref, tab_ref, carry_ref):
    i = pl.program_id(0)

    @pl.when(i == 0)
    def _():
        carry_ref[...] = jnp.zeros_like(carry_ref)

    mod = mod_ref[...]
    is_ctx = i < T_CTX // TM
    f_mix = jnp.where(is_ctx, fc_ref[...], fl_ref[...])
    a_mix = jnp.where(is_ctx, ac_ref[...], al_ref[...])
    mix = _dot(jnp.concatenate([f_mix, p_ref[...], s_ref[...], a_mix], axis=1), wout_ref[...])
    x = jnp.where(is_ctx, xc_ref[...], xl_ref[...])
    x1 = _layer_norm(DEEPNORM_ALPHA * x + mod[2:3] * mix, g_ref[...], b_ref[...])
    x1_ref[...] = x1
    h2 = x1 * (1.0 + mod[4:5]) + mod[3:4]
    h_hi, h_lo = _split_hi_lo(h2)
    h2_ref[...] = h_hi
    hw = _dot(h_hi, wr_ref[...])
    logits = hw[:, :128] + hw[:, 128:] + _dot(h_lo, wr_ref[:, :128]) + br_ref[...]
    lane = lax.broadcasted_iota(i32, logits.shape, 1).astype(f32)
    neg = jnp.float32(-jnp.inf)
    big = jnp.float32(1 << 20)
    gl = jnp.where(lane < N_GROUPS, logits, neg)
    gmax = jnp.max(gl, axis=-1, keepdims=True)
    gsel = jnp.min(jnp.where(gl == gmax, lane, big), axis=-1, keepdims=True)
    pg = 1.0 / jnp.sum(jnp.exp(gl - gmax), axis=-1, keepdims=True)
    e_lo = ROUTE_E0 + gsel * EXPERTS_PER_GROUP
    el = jnp.where((lane >= e_lo) & (lane < e_lo + EXPERTS_PER_GROUP), logits, neg)
    v1 = jnp.max(el, axis=-1, keepdims=True)
    i1 = jnp.min(jnp.where(el == v1, lane, big), axis=-1, keepdims=True)
    el2 = jnp.where(lane == i1, neg, el)
    v2 = jnp.max(el2, axis=-1, keepdims=True)
    i2 = jnp.min(jnp.where(el2 == v2, lane, big), axis=-1, keepdims=True)
    e2 = jnp.exp(v2 - v1)
    w1 = pg / (1.0 + e2)
    w2 = pg * e2 / (1.0 + e2)
    oh1 = lane == i1
    oh2 = lane == i2
    oh = jnp.where(oh1 | oh2, 1.0, 0.0)
    lrank = _dot(tril_ref[...], oh.astype(bf16))
    seg = jnp.floor((jnp.sum(oh, axis=0, keepdims=True) + (ROW_CHUNK - 1.0)) * (1.0 / ROW_CHUNK)) * ROW_CHUNK
    seg8 = jnp.broadcast_to(seg, (8, 128))
    lane8 = lax.broadcasted_iota(i32, (8, 128), 1)
    off8 = seg8
    for sh in (1, 2, 4, 8, 16):
        off8 = off8 + jnp.where(lane8 >= sh, pltpu.roll(off8, sh, 1), 0.0)
    off8 = off8 - seg8
    carry = carry_ref[...]
    lpos = lrank + off8[0:1, :]
    pick = lambda sel, val: jnp.sum(jnp.where(sel, val, 0.0), axis=-1, keepdims=True)
    sub8 = lax.broadcasted_iota(i32, (8, 128), 0)
    tab_ref[...] = jnp.where(sub8 == 0, seg8, jnp.where(sub8 == 1, off8, jnp.where(sub8 == 2, carry, 0.0)))
    carry = carry + seg8
    carry_ref[...] = carry
    cnt_ref[...] = carry
    cols = (i1 - ROUTE_E0, i2 - ROUTE_E0, w1, w2, pick(oh1, lpos), pick(oh2, lpos))
    route = jnp.zeros_like(logits)
    for j, col in enumerate(cols):
        route = jnp.where(lane == j, col, route)
    route_ref[...] = route


def _outproj(x_ctx, x_lat, mod, fo_ctx, fo_lat, po, so, ao_ctx, ao_lat, l, w):
    nt = T_ALL // TM
    tile = lambda wd: pl.BlockSpec((TM, wd), lambda i: (i, 0))
    vec = lambda wd: pl.BlockSpec((None, 1, wd), lambda i: (l, 0, 0))
    return pl.pallas_call(
        _outproj_kernel,
        grid=(nt,),
        in_specs=[
            _ctx_tile(D_MODEL), _lat_tile(D_MODEL),
            pl.BlockSpec((None, None, 6, D_MODEL), lambda i: (l, i // (SEG // TM), 0, 0)),
            _ctx_tile(FFT_W), _lat_tile(FFT_W), tile(POOL_W), tile(SGU_W),
            _ctx_tile(ATTN_W), _lat_tile(ATTN_W),
            pl.BlockSpec((None, D_MODEL, D_MODEL), lambda i: (l, 0, 0)),
            vec(D_MODEL), vec(D_MODEL),
            pl.BlockSpec((None, D_MODEL, 256), lambda i: (l, 0, 0)),
            vec(128),
            pl.BlockSpec((TM, TM), lambda i: (0, 0)),
        ],
        out_specs=[tile(D_MODEL), tile(D_MODEL), tile(128), pl.BlockSpec((8, 128), lambda i: (0, 0)),
                   pl.BlockSpec((None, 8, 128), lambda i: (i, 0, 0))],
        out_shape=[
            jax.ShapeDtypeStruct((T_ALL, D_MODEL), f32),
            jax.ShapeDtypeStruct((T_ALL, D_MODEL), bf16),
            jax.ShapeDtypeStruct((T_ALL, 128), f32),
            jax.ShapeDtypeStruct((8, 128), f32),
            jax.ShapeDtypeStruct((nt, 8, 128), f32),
        ],
        scratch_shapes=[pltpu.VMEM((8, 128), f32)],
        compiler_params=_cparams(("arbitrary",)),
        name="outproj",
    )(x_ctx, x_lat, mod, fo_ctx, fo_lat, po, so, ao_ctx, ao_lat,
      w["w_out"], w["ln1_g"], w["ln1_b"], w["w_r"], w["b_r"],
      w["tril"])


def _plan_kernel(cnt_ref, meta_ref):
    lane = lax.broadcasted_iota(i32, (8, 128), 1)
    sub = lax.broadcasted_iota(i32, (8, 128), 0)
    cnt = cnt_ref[...]
    is_e = (lane >= ROUTE_E0) & (lane < ROUTE_E0 + N_EXPERTS)
    tiles = jnp.where(is_e, jnp.floor((cnt + (MOE_TM - 1.0)) * (1.0 / MOE_TM)), 0.0)
    cum = tiles
    for s in (1, 2, 4, 8, 16):
        cum = cum + jnp.where(lane >= s, pltpu.roll(cum, s, 1), 0.0)
    pstart = (cum - tiles) * MOE_TM
    nused = jnp.max(cum, axis=-1, keepdims=True)
    fill = jnp.where(is_e & (cnt != tiles * MOE_TM), pstart + (tiles - 1.0) * MOE_TM, -1.0)
    meta = jnp.where(sub == 0, cnt, jnp.where(sub == 1, nused, jnp.where(sub == 2, fill,
                     jnp.where(sub == 3, pstart, 0.0))))
    meta_ref[...] = meta.astype(i32)


def _plan(cnt):
    return pl.pallas_call(
        _plan_kernel,
        grid=(1,),
        in_specs=[pl.BlockSpec((8, 128), lambda i: (0, 0))],
        out_specs=pl.BlockSpec((8, 128), lambda i: (0, 0)),
        out_shape=jax.ShapeDtypeStruct((8, 128), i32),
        compiler_params=_cparams(("arbitrary",)),
        name="plan",
    )(cnt)


def _dispatch_kernel(nch_ref, off_ref, dst_ref, tot_ref, fill_ref, nused_ref, h_ref, route_ref, xs_ref,
                     sorted_ref, zero_ref, sem, fill_sem):
    i = pl.program_id(0)

    def tile_fill(row0):
        return pltpu.make_async_copy(zero_ref, xs_ref.at[pl.ds(pl.multiple_of(row0, MOE_TM), MOE_TM)],
                                     fill_sem)

    @pl.when(i == 0)
    def _():
        zero_ref[...] = jnp.zeros_like(zero_ref)

        def start(e, c):
            @pl.when(fill_ref[e] >= 0)
            def _():
                tile_fill(jnp.maximum(fill_ref[e], 0)).start()
            return c

        def wait(e, c):
            @pl.when(fill_ref[e] >= 0)
            def _():
                tile_fill(jnp.maximum(fill_ref[e], 0)).wait()
            return c

        def start_tail(t, c):
            tile_fill(t * MOE_TM).start()
            return c

        def wait_tail(t, c):
            tile_fill(t * MOE_TM).wait()
            return c

        lax.fori_loop(0, N_EXPERTS, start, 0)
        lax.fori_loop(nused_ref[0], MOE_NT, start_tail, 0)
        lax.fori_loop(0, N_EXPERTS, wait, 0)
        lax.fori_loop(nused_ref[0], MOE_NT, wait_tail, 0)

    rt = route_ref[...].T
    hb = h_ref[...]
    slot = i % 2
    used_rows = tot_ref[i] * ROW_CHUNK

    def sort_rows(nrows):
        j = lax.broadcasted_iota(i32, (nrows, 1), 0).astype(f32)
        sel = jnp.where((j == rt[4:5, :]) | (j == rt[5:6, :]), 1.0, 0.0).astype(bf16)
        sorted_ref[slot, 0:nrows, :] = _dot(sel, hb).astype(bf16)

    @pl.when(used_rows <= DISP_ROWS_SHORT)
    def _():
        sort_rows(DISP_ROWS_SHORT)

    @pl.when(used_rows > DISP_ROWS_SHORT)
    def _():
        sort_rows(DISP_ROWS)

    def per_expert(e, c):
        idx = i * N_EXPERTS + e
        n, s0, d0 = nch_ref[idx], off_ref[idx], dst_ref[idx]
        b = 1
        while b <= TM // ROW_CHUNK:
            @pl.when((n & b) != 0)
            def _(b=b):
                r0 = (n & (b - 1)) * ROW_CHUNK
                rows = b * ROW_CHUNK
                pltpu.make_async_copy(
                    sorted_ref.at[slot, pl.ds(pl.multiple_of(s0 + r0, ROW_CHUNK), rows)],
                    xs_ref.at[pl.ds(pl.multiple_of(d0 + r0, ROW_CHUNK), rows)], sem.at[slot]).start()
            b *= 2
        return c

    lax.fori_loop(0, N_EXPERTS, per_expert, 0)

    def drain(tile, s):
        rows = tot_ref[tile] * ROW_CHUNK

        @pl.when(rows > 0)
        def _():
            pltpu.make_async_copy(sorted_ref.at[s, pl.ds(0, rows)], xs_ref.at[pl.ds(0, rows)],
                                  sem.at[s]).wait()

    @pl.when(i >= 1)
    def _():
        drain(i - 1, 1 - slot)

    @pl.when(i == pl.num_programs(0) - 1)
    def _():
        drain(i, slot)


def _dispatch(nch, off, dst, tot, fill, nused, h2, route):
    grid_spec = pltpu.PrefetchScalarGridSpec(
        num_scalar_prefetch=6,
        grid=(T_ALL // TM,),
        in_specs=[pl.BlockSpec((TM, D_MODEL), lambda i, *_: (i, 0)),
                  pl.BlockSpec((TM, 128), lambda i, *_: (i, 0))],
        out_specs=pl.BlockSpec(memory_space=pl.ANY),
        scratch_shapes=[pltpu.VMEM((2, DISP_ROWS, D_MODEL), bf16), pltpu.VMEM((MOE_TM, D_MODEL), bf16),
                        pltpu.SemaphoreType.DMA((2,)), pltpu.SemaphoreType.DMA(())],
    )
    return pl.pallas_call(
        _dispatch_kernel,
        grid_spec=grid_spec,
        out_shape=jax.ShapeDtypeStruct((MOE_NT * MOE_TM, D_MODEL), bf16),
        compiler_params=_cparams(("arbitrary",)),
        name="dispatch",
    )(nch, off, dst, tot, fill, nused, h2, route)


def _experts_kernel(cnt_ref, nused_ref, xs_ref, wg_hbm, wu_hbm, wd_hbm, ys_ref,
                    wg_f, wu_f, wd_f, wg_b, wu_b, wd_b, st, wsem, *, layer):
    i = pl.program_id(0)
    nused = nused_ref[0]
    NXT, NSLOT, LEFT, ROWS = 0, 1, 2, 3

    def w_copies(e, slot):
        return (pltpu.make_async_copy(wg_hbm.at[layer, e], wg_f.at[slot], wsem.at[slot, 0]),
                pltpu.make_async_copy(wu_hbm.at[layer, e], wu_f.at[slot], wsem.at[slot, 1]),
                pltpu.make_async_copy(wd_hbm.at[layer, e], wd_f.at[slot], wsem.at[slot, 2]))

    def next_nonempty(e):
        return lax.while_loop(
            lambda v: (v < N_EXPERTS) & (cnt_ref[jnp.minimum(v, N_EXPERTS - 1)] == 0),
            lambda v: v + 1, e)

    @pl.when(i == 0)
    def _():
        e0 = next_nonempty(jnp.int32(0))
        for c in w_copies(e0, 0):
            c.start()
        st[NXT] = e0
        st[NSLOT] = 0
        st[LEFT] = 0

    @pl.when(i < nused)
    def _():
        @pl.when(st[LEFT] == 0)
        def _():
            e = st[NXT]
            slot = st[NSLOT]
            for c in w_copies(e, slot):
                c.wait()
            e2 = next_nonempty(e + 1)

            @pl.when(e2 < N_EXPERTS)
            def _():
                for c in w_copies(e2, 1 - slot):
                    c.start()

            st[NXT] = e2
            st[NSLOT] = 1 - slot
            st[LEFT] = (cnt_ref[e] + (MOE_TM - 1)) // MOE_TM
            st[ROWS] = cnt_ref[e]
            wg_b[...] = wg_f[slot].astype(bf16)
            wu_b[...] = wu_f[slot].astype(bf16)
            wd_b[...] = wd_f[slot].astype(bf16)

        def ffn(rows):
            x = xs_ref[rows, :]
            hg = _dot(x, wg_b[...])
            hu = _dot(x, wu_b[...])
            act = (hg * jax.nn.sigmoid(hg)) * hu
            ys_ref[rows, :] = _dot(act.astype(bf16), wd_b[...]).astype(bf16)

        half = MOE_TM // 2
        short = st[ROWS] <= half

        @pl.when(short)
        def _():
            ffn(slice(0, half))
            ys_ref[half:, :] = jnp.zeros((MOE_TM - half, D_MODEL), bf16)

        @pl.when(jnp.logical_not(short))
        def _():
            ffn(slice(0, MOE_TM))

        st[LEFT] = st[LEFT] - 1
        st[ROWS] = st[ROWS] - MOE_TM

    @pl.when(i >= nused)
    def _():
        ys_ref[...] = jnp.zeros_like(ys_ref)


def _experts(counts, nused, xs, l, w_gate, w_up, w_down):
    hbm = pl.BlockSpec(memory_space=pl.ANY)
    grid_spec = pltpu.PrefetchScalarGridSpec(
        num_scalar_prefetch=2,
        grid=(MOE_NT,),
        in_specs=[pl.BlockSpec((MOE_TM, D_MODEL), lambda i, c, nu: (jnp.minimum(i, nu[0] - 1), 0)),
                  hbm, hbm, hbm],
        out_specs=pl.BlockSpec((MOE_TM, D_MODEL), lambda i, c, nu: (i, 0)),
        scratch_shapes=[
            pltpu.VMEM((2, D_MODEL, EXPERT_FF), f32),
            pltpu.VMEM((2, D_MODEL, EXPERT_FF), f32),
            pltpu.VMEM((2, EXPERT_FF, D_MODEL), f32),
            pltpu.VMEM((D_MODEL, EXPERT_FF), bf16),
            pltpu.VMEM((D_MODEL, EXPERT_FF), bf16),
            pltpu.VMEM((EXPERT_FF, D_MODEL), bf16),
            pltpu.SMEM((4,), i32),
            pltpu.SemaphoreType.DMA((2, 3)),
        ],
    )
    return pl.pallas_call(
        functools.partial(_experts_kernel, layer=l),
        grid_spec=grid_spec,
        out_shape=jax.ShapeDtypeStruct((MOE_NT * MOE_TM, D_MODEL), bf16),
        compiler_params=_cparams(("arbitrary",)),
        name="experts",
    )(counts, nused, xs, w_gate, w_up, w_down)


def _combine_kernel(nch_ref, off_ref, dst_ref, tot_ref, x1_ref, mod_ref, route_ref, ys_hbm, g_ref, b_ref,
                    oc_ref, ol_ref, ybuf, moe_ref, sem):
    i = pl.program_id(0)
    nt = pl.num_programs(0) - 1

    @pl.when(i == 0)
    def _():
        ybuf[...] = jnp.zeros_like(ybuf)

    @pl.when(i < nt)
    def _():
        slot = i % 2

        def per_expert(e, c):
            idx = i * N_EXPERTS + e
            n, s0, d0 = nch_ref[idx], off_ref[idx], dst_ref[idx]
            b = 1
            while b <= TM // ROW_CHUNK:
                @pl.when((n & b) != 0)
                def _(b=b):
                    r0 = (n & (b - 1)) * ROW_CHUNK
                    rows = b * ROW_CHUNK
                    pltpu.make_async_copy(
                        ys_hbm.at[pl.ds(pl.multiple_of(d0 + r0, ROW_CHUNK), rows)],
                        ybuf.at[slot, pl.ds(pl.multiple_of(s0 + r0, ROW_CHUNK), rows)],
                        sem.at[slot]).start()
                b *= 2
            return c

        lax.fori_loop(0, N_EXPERTS, per_expert, 0)

    @pl.when(i >= 1)
    def _():
        slot = (i - 1) % 2
        rows = tot_ref[i - 1] * ROW_CHUNK

        @pl.when(rows > 0)
        def _():
            pltpu.make_async_copy(ys_hbm.at[pl.ds(0, rows)], ybuf.at[slot, pl.ds(0, rows)],
                                  sem.at[slot]).wait()

        route = route_ref[...]
        mod = mod_ref[...]

        def unsort(nrows):
            lane = lax.broadcasted_iota(i32, (1, nrows), 1).astype(f32)
            wmat = (jnp.where(lane == route[:, 4:5], route[:, 2:3], 0.0)
                    + jnp.where(lane == route[:, 5:6], route[:, 3:4], 0.0))
            moe_ref[...] = _dot(wmat.astype(bf16), ybuf[slot, 0:nrows, :])

        @pl.when(rows <= DISP_ROWS_SHORT)
        def _():
            unsort(DISP_ROWS_SHORT)

        @pl.when(rows > DISP_ROWS_SHORT)
        def _():
            unsort(DISP_ROWS)

        y = _layer_norm(DEEPNORM_ALPHA * x1_ref[...] + mod[5:6] * moe_ref[...], g_ref[...], b_ref[...])

        @pl.when(i - 1 < T_CTX // TM)
        def _():
            oc_ref[...] = y

        @pl.when(i - 1 >= T_CTX // TM)
        def _():
            ol_ref[...] = y


def _combine(nch, off, dst, tot, x1, mod, route, ys, l, w):
    nt = T_ALL // TM
    vec = pl.BlockSpec((None, 1, D_MODEL), lambda i, *_: (l, 0, 0))
    nctx = T_CTX // TM
    prev = lambda i: jnp.maximum(i - 1, 0)
    grid_spec = pltpu.PrefetchScalarGridSpec(
        num_scalar_prefetch=4,
        grid=(nt + 1,),
        in_specs=[
            pl.BlockSpec((TM, D_MODEL), lambda i, *_: (prev(i), 0)),
            pl.BlockSpec((None, None, 6, D_MODEL), lambda i, *_: (l, prev(i) // (SEG // TM), 0, 0)),
            pl.BlockSpec((TM, 128), lambda i, *_: (prev(i), 0)),
            pl.BlockSpec(memory_space=pl.ANY),
            vec, vec,
        ],
        out_specs=[pl.BlockSpec((TM, D_MODEL), lambda i, *_: (jnp.minimum(prev(i), nctx - 1), 0)),
                   pl.BlockSpec((TM, D_MODEL), lambda i, *_: (jnp.maximum(prev(i) - nctx, 0), 0))],
        scratch_shapes=[pltpu.VMEM((2, DISP_ROWS, D_MODEL), bf16), pltpu.VMEM((TM, D_MODEL), f32),
                        pltpu.SemaphoreType.DMA((2,))],
    )
    return pl.pallas_call(
        _combine_kernel,
        grid_spec=grid_spec,
        out_shape=[jax.ShapeDtypeStruct((T_CTX, D_MODEL), f32),
                   jax.ShapeDtypeStruct((T_LAT, D_MODEL), f32)],
        compiler_params=_cparams(("arbitrary",)),
        name="combine",
    )(nch, off, dst, tot, x1, mod, route, ys, w["ln2_g"], w["ln2_b"])


def _dft_cos_sin(n, scale):
    k = jnp.arange(n, dtype=i32)
    ang = ((k[:, None] * k[None, :]) % n).astype(f32) * np.float32(2.0 * np.pi / n)
    return jnp.cos(ang) * scale, jnp.sin(ang) * scale


def _seq_dft_matrix(n):
    g = min(DFT_SPLIT, n)
    j = jnp.arange(n, dtype=i32)[None, :]
    k1 = jnp.arange(n // g, dtype=i32)[:, None]
    k2 = jnp.arange(g, dtype=i32)[:, None]
    ang_a = ((k1 * j) % (n // g)).astype(f32) * np.float32(2.0 * np.pi * g / n)
    ang_b = ((k2 * j) % n).astype(f32) * np.float32(2.0 * np.pi / n)
    scale = np.float32(n ** -0.5)
    ca, sa = jnp.cos(ang_a), jnp.sin(ang_a)
    cb, sb = jnp.cos(ang_b) * scale, jnp.sin(ang_b) * scale
    ca2 = jnp.concatenate([ca, ca], axis=1)[:, None, :]
    sa2 = jnp.concatenate([sa, sa], axis=1)[:, None, :]
    cb2 = jnp.concatenate([cb, -sb], axis=1)[None, :, :]
    sb2 = jnp.concatenate([sb, cb], axis=1)[None, :, :]
    return (ca2 * cb2 - sa2 * sb2).astype(bf16).reshape(n, 2 * n)


def _rope_tables():
    rows = DEC_SEQ // GRID_W
    row = jnp.repeat(jnp.arange(rows), GRID_W).astype(f32)
    col = jnp.tile(jnp.arange(GRID_W), rows).astype(f32)
    n_freq = HEAD_DIM // 4
    inv = ROPE_THETA ** (-jnp.arange(n_freq, dtype=f32) / n_freq)
    ar = row[:, None] * inv
    ac = col[:, None] * inv
    ang = jnp.concatenate([ar, ar, ac, ac], axis=-1)
    cos = jnp.tile(jnp.cos(ang), (1, N_HEADS))
    sin = jnp.tile(jnp.sin(ang), (1, N_HEADS))
    first = (jnp.arange(ATTN_W) % (HEAD_DIM // 2)) < n_freq
    sin_a = jnp.where(first[None, :], -sin, 0.0)
    sin_b = jnp.where(first[None, :], 0.0, sin)
    ident = jnp.zeros((TM, ATTN_W), f32)
    return (jnp.concatenate([cos, ident + 1.0], axis=0),
            jnp.concatenate([sin_a, ident], axis=0),
            jnp.concatenate([sin_b, ident], axis=0))


def _dup_cache(cache):
    c = jnp.transpose(cache, (1, 3, 0, 2, 4))
    return jnp.concatenate([c, c], axis=-1).astype(bf16)


def kernel(x_prompt, x_sample, cache_k, cache_v, c, c_ctx, w_mod, b_mod, w_in, w_fft, w_pool, pool_scale, sgu_ln_g, sgu_ln_b, w_sgu, b_sgu, q_norm_g, k_norm_g, w_out, ln1_g, ln1_b, w_router_group, b_router_group, w_router_expert, b_router_expert, w_gate, w_up, w_down, ln2_g, ln2_b):
    L = DEPTH
    x_ctx = x_prompt.reshape(T_CTX, D_MODEL)
    x_lat = x_sample.reshape(T_LAT, D_MODEL)

    cond8 = jnp.concatenate([c_ctx[None, :], c, jnp.zeros((8 - 1 - DEC_BATCH, D_MODEL), f32)], axis=0)
    mod = _modulation(cond8, w_mod, b_mod)[:, :N_SEG].reshape(L, N_SEG, 6, D_MODEL)

    cc, sc = _dft_cos_sin(FFT_W, np.float32(FFT_W ** -0.5))
    rope_cos, rope_sin_a, rope_sin_b = _rope_tables()
    head_id = jnp.arange(ATTN_W) // HEAD_DIM
    eye_g = jnp.eye(len(POOL_WINDOWS), dtype=f32)
    w_r = jnp.zeros((L, D_MODEL, 128), f32)
    w_r = w_r.at[:, :, :N_GROUPS].set(w_router_group).at[:, :, ROUTE_E0:ROUTE_E0 + N_EXPERTS].set(w_router_expert)
    b_r = jnp.zeros((L, 1, 128), f32)
    b_r = b_r.at[:, 0, :N_GROUPS].set(b_router_group).at[:, 0, ROUTE_E0:ROUTE_E0 + N_EXPERTS].set(b_router_expert)
    w_r_hi, w_r_lo = _split_hi_lo(w_r)
    w = {
        "w_in": w_in.astype(bf16),
        "csc": jnp.concatenate([cc, sc], axis=1).astype(bf16),
        "w_sgu": jnp.transpose(w_sgu, (0, 2, 1, 3)).reshape(L, CHUNK, SGU_HEADS * CHUNK).astype(bf16),
        "b_sgu": jnp.repeat(jnp.transpose(b_sgu, (0, 2, 1)), SGU_W // SGU_HEADS, axis=2),
        "sgu_ln_g": sgu_ln_g.reshape(L, 1, SGU_W),
        "sgu_ln_b": sgu_ln_b.reshape(L, 1, SGU_W),
        "q_norm_g": jnp.tile(q_norm_g, (1, N_HEADS)).reshape(L, 1, ATTN_W),
        "k_norm_g": jnp.tile(k_norm_g, (1, N_KV_HEADS)).reshape(L, 1, KV_W),
        "rope_cos": rope_cos, "rope_sin_a": rope_sin_a, "rope_sin_b": rope_sin_b,
        "ones_bd": (head_id[:, None] == head_id[None, :]).astype(bf16),
        "w_pool_bd": jnp.einsum("lgcd,gh->lgchd", w_pool, eye_g).reshape(L, POOL_W, POOL_W).astype(bf16),
        "pool_scale": pool_scale.reshape(L, 1, POOL_W),
        "w_fft": w_fft.astype(bf16),
        "w_out": w_out.astype(bf16),
        "ln1_g": ln1_g.reshape(L, 1, D_MODEL), "ln1_b": ln1_b.reshape(L, 1, D_MODEL),
        "ln2_g": ln2_g.reshape(L, 1, D_MODEL), "ln2_b": ln2_b.reshape(L, 1, D_MODEL),
        "w_r": jnp.concatenate([w_r_hi, w_r_lo], axis=-1), "b_r": b_r,
        "tril": (jnp.arange(TM)[:, None] > jnp.arange(TM)[None, :]).astype(bf16),
    }
    m_ctx = _seq_dft_matrix(SEQ)
    m_lat = _seq_dft_matrix(DEC_SEQ)
    kc_all = _dup_cache(cache_k)
    vc_all = _dup_cache(cache_v)

    new_k, new_v = [], []
    for l in range(L):
        pq, praw, sgu, q, kd, vd, nk, nv = _inproj(x_ctx, x_lat, mod, l, w)
        new_k.append(nk[:T_CTX].reshape(BATCH, SEQ, N_KV_HEADS, HEAD_DIM))
        new_v.append(nv[:T_CTX].reshape(BATCH, SEQ, N_KV_HEADS, HEAD_DIM))
        po = _pool(praw, l, w)
        fo_ctx = _seqdft(pq, m_ctx, l, w, n=SEQ, tr=SEQ, nseq=BATCH, row0=0)
        fo_lat = _seqdft(pq, m_lat, l, w, n=DEC_SEQ, tr=FFT_TR, nseq=DEC_BATCH, row0=T_CTX)
        ao_ctx = _attention(q, kd, vd, None, n=SEQ, tq=SEQ, nseq=BATCH, row0=0, heads=N_KV_HEADS)
        ao_lat = _attention(q, kd, vd, (kc_all[l], vc_all[l]), n=DEC_SEQ, tq=ATT_TQ, nseq=DEC_BATCH,
                            row0=T_CTX, heads=1)
        x1, h2, route, cnt, tab = _outproj(x_ctx, x_lat, mod, fo_ctx, fo_lat, po, sgu, ao_ctx, ao_lat, l, w)
        meta = _plan(cnt)
        experts = slice(ROUTE_E0, ROUTE_E0 + N_EXPERTS)
        counts = meta[0, experts]
        nused = meta[1, :1]
        fill = meta[2, experts]
        tab = tab[:, :, experts].astype(i32)
        nch = (tab[:, 0] // ROW_CHUNK).reshape(-1)
        off = tab[:, 1].reshape(-1)
        dst = (meta[3, experts][None, :] + tab[:, 2]).reshape(-1)
        tot = jnp.sum(tab[:, 0], axis=1) // ROW_CHUNK
        xs = _dispatch(nch, off, dst, tot, fill, nused, h2, route)
        ys = _experts(counts, nused, xs, l, w_gate, w_up, w_down)
        x_ctx, x_lat = _combine(nch, off, dst, tot, x1, mod, route, ys, l, w)

    y_prompt = x_ctx.reshape(BATCH, SEQ, D_MODEL)
    y_sample = x_lat.reshape(DEC_BATCH, DEC_SEQ, D_MODEL)
    return (y_prompt, y_sample, jnp.stack(new_k, axis=1), jnp.stack(new_v, axis=1))
```

```python
import functools

import numpy as np
import jax
import jax.numpy as jnp
from jax import lax
from jax.experimental import pallas as pl
from jax.experimental.pallas import tpu as pltpu

f32 = jnp.float32
bf16 = jnp.bfloat16
i32 = jnp.int32

D_MODEL = 1024
BATCH = 16
SEQ = 256
DEPTH = 4
DEC_BATCH = 2
DEC_SEQ = 4096
PAST_LEN = 512
GRID_W = 64
FFT_W = 256
POOL_W = 256
POOL_WINDOWS = (2, 4, 8, 16)
POOL_GROUP = 64
SGU_W = 256
SGU_HEADS = 4
CHUNK = 128
HEAD_DIM = 64
ATTN_W = 256
N_HEADS = 4
N_KV_HEADS = 2
KV_W = 128
IN_W = 1536
ROPE_THETA = 10000.0
N_GROUPS = 4
EXPERTS_PER_GROUP = 8
N_EXPERTS = 32
EXPERT_FF = 512
DEEPNORM_ALPHA = float((2 * DEPTH) ** 0.25)
LN_EPS = 1e-5
RMS_EPS = 1e-6

T_CTX = BATCH * SEQ
T_LAT = DEC_BATCH * DEC_SEQ
T_ALL = T_CTX + T_LAT
SEG = 4096
N_SEG = T_ALL // SEG

TM = 512
POOL_TB = 1024
POOL_HALO = 8
FFT_TR = 1024
ATT_TQ = 512
ATT_CHUNK = 1024
DFT_SPLIT = 64
MOE_TM = 1024
ROW_CHUNK = 16
MOE_ROWS = 2 * T_ALL
MOE_PAD_ROWS = (T_ALL // TM) * N_EXPERTS * (ROW_CHUNK - 1)
MOE_NT = -(-(MOE_ROWS + MOE_PAD_ROWS) // MOE_TM) + N_EXPERTS
DISP_ROWS = 2 * TM + N_EXPERTS * ROW_CHUNK
DISP_ROWS_SHORT = 2 * TM + N_EXPERTS * ROW_CHUNK // 2
ROUTE_E0 = 32
VMEM_LIMIT = 56 * 1024 * 1024


def _cparams(sem):
    return pltpu.CompilerParams(dimension_semantics=sem, vmem_limit_bytes=VMEM_LIMIT)


def _split_hi_lo(a):
    hi = a.astype(bf16)
    lo = (a - hi.astype(f32)).astype(bf16)
    return hi, lo


def _dot(a, b):
    return jnp.dot(a, b, preferred_element_type=f32)


def _mod_kernel(c_ref, w_ref, b_ref, o_ref):
    c = c_ref[...]
    s = c * jax.nn.sigmoid(c)
    s_hi, s_lo = _split_hi_lo(s)
    w_hi, w_lo = _split_hi_lo(w_ref[...])
    o_ref[...] = _dot(s_hi, w_hi) + _dot(s_hi, w_lo) + _dot(s_lo, w_hi) + b_ref[...]


def _modulation(cond8, w_mod, b_mod):
    tn = 1536
    return pl.pallas_call(
        _mod_kernel,
        grid=(DEPTH, 6 * D_MODEL // tn),
        in_specs=[
            pl.BlockSpec((8, D_MODEL), lambda l, j: (0, 0)),
            pl.BlockSpec((None, D_MODEL, tn), lambda l, j: (l, 0, j)),
            pl.BlockSpec((None, 1, tn), lambda l, j: (l, 0, j)),
        ],
        out_specs=pl.BlockSpec((None, 8, tn), lambda l, j: (l, 0, j)),
        out_shape=jax.ShapeDtypeStruct((DEPTH, 8, 6 * D_MODEL), f32),
        compiler_params=_cparams(("arbitrary", "arbitrary")),
        name="modulation",
    )(cond8, w_mod, b_mod.reshape(DEPTH, 1, 6 * D_MODEL))


def _head_rms(x, ones_bd, gain):
    ss = _dot((x * x).astype(bf16), ones_bd)
    return x * lax.rsqrt(ss * (1.0 / HEAD_DIM) + RMS_EPS) * gain


def _rope(x, cos, sin_a, sin_b):
    w = x.shape[-1]
    q4 = HEAD_DIM // 4
    return x * cos + pltpu.roll(x, w - q4, 1) * sin_a + pltpu.roll(x, q4, 1) * sin_b


def _dup_half(x, first):
    lane = lax.broadcasted_iota(i32, x.shape, 1)
    r = pltpu.roll(x, HEAD_DIM, 1)
    if first:
        return jnp.where(lane < HEAD_DIM, x, r)
    return jnp.where(lane >= HEAD_DIM, x, r)


def _gelu_tanh(x):
    c = np.sqrt(2.0 / np.pi).astype(np.float32)
    return x * (0.5 * (1.0 + jnp.tanh(c * (x + 0.044715 * (x * x * x)))))


def _inproj_kernel(xc_ref, xl_ref, mod_ref, win_ref, csc_ref, wsgu_ref, bsgu_ref, lng_ref, lnb_ref,
                   qg_ref, kg_ref, cos_ref, sina_ref, sinb_ref, ones_ref,
                   pq_ref, pool_ref, sgu_ref, q_ref, kd_ref, vd_ref, nk_ref, nv_ref):
    x = jnp.where(pl.program_id(0) < T_CTX // TM, xc_ref[...], xl_ref[...])
    mod = mod_ref[...]
    h = (x * (1.0 + mod[1:2]) + mod[0:1]).astype(bf16)
    proj = _dot(h, win_ref[...])

    a = proj[:, 0:FFT_W].astype(bf16)
    pq_ref[...] = _dot(a, csc_ref[...]).astype(bf16)

    pool_ref[...] = proj[:, FFT_W:FFT_W + POOL_W]

    o = FFT_W + POOL_W
    hgu = _gelu_tanh(proj[:, o:o + 2 * SGU_W])
    u = hgu[:, :SGU_W]
    v = hgu[:, SGU_W:]
    mu = jnp.mean(v, axis=-1, keepdims=True)
    vc = v - mu
    var = jnp.mean(vc * vc, axis=-1, keepdims=True)
    v = vc * lax.rsqrt(var + LN_EPS) * lng_ref[...] + lnb_ref[...]
    lane = lax.broadcasted_iota(i32, (CHUNK, SGU_W), 1)
    head = lane // (SGU_W // SGU_HEADS)
    wcat = wsgu_ref[...]
    for cidx in range(TM // CHUNK):
        rows = slice(cidx * CHUNK, (cidx + 1) * CHUNK)
        vch = v[rows]
        vblk = jnp.concatenate(
            [jnp.where(head == g, vch, 0.0) for g in range(SGU_HEADS)], axis=0).astype(bf16)
        sp = _dot(wcat, vblk) + bsgu_ref[...]
        sgu_ref[rows, :] = (u[rows] * sp).astype(bf16)

    o = o + 2 * SGU_W
    ones_bd = ones_ref[...]
    cos = cos_ref[...]
    sin_a = sina_ref[...]
    sin_b = sinb_ref[...]
    q = _head_rms(proj[:, o:o + ATTN_W], ones_bd, qg_ref[...])
    q = _rope(q, cos, sin_a, sin_b) * np.float32(HEAD_DIM ** -0.5 * np.log2(np.e))
    q_ref[...] = q.astype(bf16)
    o = o + ATTN_W
    k = _head_rms(proj[:, o:o + KV_W], ones_bd[:KV_W, :KV_W], kg_ref[...])
    nk_ref[...] = k
    k = _rope(k, cos[:, :KV_W], sin_a[:, :KV_W], sin_b[:, :KV_W])
    kd_ref[0] = _dup_half(k, True).astype(bf16)
    kd_ref[1] = _dup_half(k, False).astype(bf16)
    o = o + KV_W
    vv = proj[:, o:o + KV_W]
    nv_ref[...] = vv
    vd_ref[0] = _dup_half(vv, True).astype(bf16)
    vd_ref[1] = _dup_half(vv, False).astype(bf16)


def _rope_block(i):
    nlat = DEC_SEQ // TM
    nctx = T_CTX // TM
    return jnp.where(i < nctx, nlat, (i - nctx) % nlat)


def _ctx_tile(wd):
    return pl.BlockSpec((TM, wd), lambda i, *_: (jnp.minimum(i, T_CTX // TM - 1), 0))


def _lat_tile(wd):
    return pl.BlockSpec((TM, wd), lambda i, *_: (jnp.maximum(i - T_CTX // TM, 0), 0))


def _inproj(x_ctx, x_lat, mod, l, w):
    nt = T_ALL // TM
    tile = lambda wd: pl.BlockSpec((TM, wd), lambda i: (i, 0))
    const = lambda shape: pl.BlockSpec(shape, lambda i: (0,) * len(shape))
    rope_spec = pl.BlockSpec((TM, ATTN_W), lambda i: (_rope_block(i), 0))
    return pl.pallas_call(
        _inproj_kernel,
        grid=(nt,),
        in_specs=[
            _ctx_tile(D_MODEL), _lat_tile(D_MODEL),
            pl.BlockSpec((None, None, 6, D_MODEL), lambda i: (l, i // (SEG // TM), 0, 0)),
            pl.BlockSpec((None, D_MODEL, IN_W), lambda i: (l, 0, 0)),
            const((FFT_W, 2 * FFT_W)),
            pl.BlockSpec((None, CHUNK, SGU_HEADS * CHUNK), lambda i: (l, 0, 0)),
            pl.BlockSpec((None, CHUNK, SGU_W), lambda i: (l, 0, 0)),
            pl.BlockSpec((None, 1, SGU_W), lambda i: (l, 0, 0)),
            pl.BlockSpec((None, 1, SGU_W), lambda i: (l, 0, 0)),
            pl.BlockSpec((None, 1, ATTN_W), lambda i: (l, 0, 0)),
            pl.BlockSpec((None, 1, KV_W), lambda i: (l, 0, 0)),
            rope_spec, rope_spec, rope_spec,
            const((ATTN_W, ATTN_W)),
        ],
        out_specs=[
            tile(2 * FFT_W), tile(POOL_W), tile(SGU_W), tile(ATTN_W),
            pl.BlockSpec((N_KV_HEADS, TM, KV_W), lambda i: (0, i, 0)),
            pl.BlockSpec((N_KV_HEADS, TM, KV_W), lambda i: (0, i, 0)),
            tile(KV_W), tile(KV_W),
        ],
        out_shape=[
            jax.ShapeDtypeStruct((T_ALL, 2 * FFT_W), bf16),
            jax.ShapeDtypeStruct((T_ALL, POOL_W), f32),
            jax.ShapeDtypeStruct((T_ALL, SGU_W), bf16),
            jax.ShapeDtypeStruct((T_ALL, ATTN_W), bf16),
            jax.ShapeDtypeStruct((N_KV_HEADS, T_ALL, KV_W), bf16),
            jax.ShapeDtypeStruct((N_KV_HEADS, T_ALL, KV_W), bf16),
            jax.ShapeDtypeStruct((T_ALL, KV_W), f32),
            jax.ShapeDtypeStruct((T_ALL, KV_W), f32),
        ],
        compiler_params=_cparams(("arbitrary",)),
        name="inproj",
    )(x_ctx, x_lat, mod, w["w_in"], w["csc"], w["w_sgu"], w["b_sgu"], w["sgu_ln_g"], w["sgu_ln_b"],
      w["q_norm_g"], w["k_norm_g"], w["rope_cos"], w["rope_sin_a"], w["rope_sin_b"], w["ones_bd"])


def _pool_kernel(prev_ref, cur_ref, next_ref, wp_ref, scale_ref, o_ref):
    i = pl.program_id(0)
    n = jnp.where(i < T_CTX // POOL_TB, SEQ, DEC_SEQ)
    hl = POOL_HALO
    ext = jnp.concatenate([prev_ref[POOL_TB - hl:, :], cur_ref[...], next_ref[:hl, :]], axis=0)
    rows = POOL_TB + 2 * hl
    r = lax.broadcasted_iota(i32, (rows, 1), 0)
    pos = (i * POOL_TB + r - hl) & (n - 1)

    def back(a, s):
        return jnp.where(pos >= s, pltpu.roll(a, s, 0), 0.0)

    def fwd(a, s):
        return jnp.where(pos + s < n, pltpu.roll(a, rows - s, 0), 0.0)

    bsum = [back(ext, 1)]
    fsum = [ext]
    for k in range(3):
        s = 1 << k
        bsum.append(bsum[k] + back(bsum[k], s))
        fsum.append(fsum[k] + fwd(fsum[k], s))
    lane = lax.broadcasted_iota(i32, (1, POOL_W), 1)
    grp = lane // POOL_GROUP
    win = bsum[3] + fsum[3]
    half = jnp.full((1, POOL_W), POOL_WINDOWS[3] // 2, i32)
    for g in (2, 1, 0):
        win = jnp.where(grp == g, bsum[g] + fsum[g], win)
        half = jnp.where(grp == g, POOL_WINDOWS[g] // 2, half)
    cnt = (jnp.minimum(pos + half, n) - jnp.maximum(pos - half, 0)).astype(f32)
    y = (win / cnt - ext)[hl:hl + POOL_TB]
    o_ref[...] = (_dot(y.astype(bf16), wp_ref[...]) * scale_ref[...]).astype(bf16)


def _pool(p, l, w):
    nt = T_ALL // POOL_TB
    blk = lambda f: pl.BlockSpec((POOL_TB, POOL_W), lambda i: (f(i), 0))
    return pl.pallas_call(
        _pool_kernel,
        grid=(nt,),
        in_specs=[
            blk(lambda i: jnp.maximum(i - 1, 0)), blk(lambda i: i),
            blk(lambda i: jnp.minimum(i + 1, nt - 1)),
            pl.BlockSpec((None, POOL_W, POOL_W), lambda i: (l, 0, 0)),
            pl.BlockSpec((None, 1, POOL_W), lambda i: (l, 0, 0)),
        ],
        out_specs=blk(lambda i: i),
        out_shape=jax.ShapeDtypeStruct((T_ALL, POOL_W), bf16),
        compiler_params=_cparams(("arbitrary",)),
        name="pool",
    )(p, p, p, w["w_pool_bd"], w["pool_scale"])


def _seqdft_kernel(*refs, n, nseq):
    m_ref, pq_refs, w_ref, o_ref = refs[0], refs[1:-2], refs[-2], refs[-1]
    per_blk = SEG // n
    tr = m_ref.shape[0]

    def one_sequence(b, rows):
        pq_ref = pq_refs[b // per_blk]
        r0 = (b % per_blk) * n
        f = (_dot(m_ref[:, :n], pq_ref[r0:r0 + n, :FFT_W])
             + _dot(m_ref[:, n:], pq_ref[r0:r0 + n, FFT_W:]))
        o_ref[rows, :] = _dot(f.astype(bf16), w_ref[...]).astype(bf16)

    if tr == n:
        for b in range(nseq):
            one_sequence(b, slice(b * n, (b + 1) * n))
    else:
        for b in range(nseq):
            @pl.when(pl.program_id(1) == b)
            def _(b=b):
                one_sequence(b, slice(0, tr))


def _seqdft(pq, m, l, w, *, n, tr, nseq, row0):
    nr = n // tr
    nblk = nseq * n // SEG
    pq_specs = [pl.BlockSpec((SEG, 2 * FFT_W), lambda i, b, j=j: (row0 // SEG + j, 0))
                for j in range(nblk)]
    if nr == 1:
        grid, out_spec = (1, 1), pl.BlockSpec((nseq * n, FFT_W), lambda i, b: (0, 0))
    else:
        grid, out_spec = (nr, nseq), pl.BlockSpec((tr, FFT_W), lambda i, b: (b * nr + i, 0))
    return pl.pallas_call(
        functools.partial(_seqdft_kernel, n=n, nseq=nseq),
        grid=grid,
        in_specs=[pl.BlockSpec((tr, 2 * n), lambda i, b: (i, 0))] + pq_specs
        + [pl.BlockSpec((None, FFT_W, FFT_W), lambda i, b: (l, 0, 0))],
        out_specs=out_spec,
        out_shape=jax.ShapeDtypeStruct((nseq * n, FFT_W), bf16),
        compiler_params=_cparams(("arbitrary", "arbitrary")),
        name="seqdft_%d" % n,
    )(m, *([pq] * nblk), w["w_fft"])


def _attn_kernel(*refs, has_cache):
    if has_cache:
        q_ref, k_ref, v_ref, kc_ref, vc_ref, o_ref = refs
    else:
        q_ref, k_ref, v_ref, o_ref = refs
    pair = 2 * HEAD_DIM
    tq = q_ref.shape[0]
    n = k_ref.shape[1]
    nt = (((1,), (1,)), ((), ()))
    chunk = min(n, ATT_CHUNK)
    lane = lax.broadcasted_iota(i32, (tq, pair), 1)
    for h in range(k_ref.shape[0]):
        q = q_ref[:, h * pair:(h + 1) * pair]
        zero = jnp.zeros_like(q)
        qs = jnp.concatenate([jnp.where(lane < HEAD_DIM, q, zero),
                              jnp.where(lane >= HEAD_DIM, q, zero)], axis=0)
        parts = [(k_ref, v_ref, c * chunk, chunk) for c in range(n // chunk)]
        if has_cache:
            parts = [(kc_ref, vc_ref, 0, PAST_LEN)] + parts
        m = jnp.full((2 * tq, 1), -jnp.inf, f32)
        den = jnp.zeros((2 * tq, 1), f32)
        acc = jnp.zeros((2 * tq, pair), f32)
        for kr, vr, off, size in parts:
            s = lax.dot_general(qs, kr[h, off:off + size, :], nt, preferred_element_type=f32)
            m_new = jnp.maximum(m, jnp.max(s, axis=-1, keepdims=True))
            alpha = jnp.exp2(m - m_new)
            p = jnp.exp2(s - m_new).astype(bf16)
            den = alpha * den + jnp.sum(p.astype(f32), axis=-1, keepdims=True)
            acc = alpha * acc + _dot(p, vr[h, off:off + size, :])
            m = m_new
        out = acc / den
        o_ref[:, h * pair:(h + 1) * pair] = jnp.where(lane < HEAD_DIM, out[:tq], out[tq:]).astype(bf16)


def _attention(q, kd, vd, cache, *, n, tq, nseq, row0, heads):
    nq = n // tq
    b0 = row0 // n
    q0 = row0 // tq
    in_specs = [
        pl.BlockSpec((tq, heads * 2 * HEAD_DIM), lambda b, h, i: (q0 + b * nq + i, h)),
        pl.BlockSpec((heads, n, KV_W), lambda b, h, i: (h, b0 + b, 0)),
        pl.BlockSpec((heads, n, KV_W), lambda b, h, i: (h, b0 + b, 0)),
    ]
    args = [q, kd, vd]
    if cache is not None:
        cspec = pl.BlockSpec((heads, None, PAST_LEN, KV_W), lambda b, h, i: (h, b, 0, 0))
        in_specs += [cspec, cspec]
        args += list(cache)
    return pl.pallas_call(
        functools.partial(_attn_kernel, has_cache=cache is not None),
        grid=(nseq, N_KV_HEADS // heads, nq),
        in_specs=in_specs,
        out_specs=pl.BlockSpec((tq, heads * 2 * HEAD_DIM), lambda b, h, i: (b * nq + i, h)),
        out_shape=jax.ShapeDtypeStruct((nseq * n, ATTN_W), bf16),
        compiler_params=_cparams(("arbitrary", "arbitrary", "arbitrary")),
        name="attention_%d" % n,
    )(*args)


def _layer_norm(x, g, b):
    mu = jnp.mean(x, axis=-1, keepdims=True)
    xc = x - mu
    var = jnp.mean(xc * xc, axis=-1, keepdims=True)
    return xc * lax.rsqrt(var + LN_EPS) * g + b


def _outproj_kernel(xc_ref, xl_ref, mod_ref, fc_ref, fl_ref, p_ref, s_ref, ac_ref, al_ref, wout_ref,
                    g_ref, b_ref, wr_ref, br_ref, tril_ref,
                    x1_ref, h2_ref, route_ref, cnt_ref, tab_ref, carry_ref):
    i = pl.program_id(0)

    @pl.when(i == 0)
    def _():
        carry_ref[...] = jnp.zeros_like(carry_ref)

    mod = mod_ref[...]
    is_ctx = i < T_CTX // TM
    f_mix = jnp.where(is_ctx, fc_ref[...], fl_ref[...])
    a_mix = jnp.where(is_ctx, ac_ref[...], al_ref[...])
    mix = _dot(jnp.concatenate([f_mix, p_ref[...], s_ref[...], a_mix], axis=1), wout_ref[...])
    x = jnp.where(is_ctx, xc_ref[...], xl_ref[...])
    x1 = _layer_norm(DEEPNORM_ALPHA * x + mod[2:3] * mix, g_ref[...], b_ref[...])
    x1_ref[...] = x1
    h2 = x1 * (1.0 + mod[4:5]) + mod[3:4]
    h_hi, h_lo = _split_hi_lo(h2)
    h2_ref[...] = h_hi
    hw = _dot(h_hi, wr_ref[...])
    logits = hw[:, :128] + hw[:, 128:] + _dot(h_lo, wr_ref[:, :128]) + br_ref[...]
    lane = lax.broadcasted_iota(i32, logits.shape, 1).astype(f32)
    neg = jnp.float32(-jnp.inf)
    big = jnp.float32(1 << 20)
    gl = jnp.where(lane < N_GROUPS, logits, neg)
    gmax = jnp.max(gl, axis=-1, keepdims=True)
    gsel = jnp.min(jnp.where(gl == gmax, lane, big), axis=-1, keepdims=True)
    pg = 1.0 / jnp.sum(jnp.exp(gl - gmax), axis=-1, keepdims=True)
    e_lo = ROUTE_E0 + gsel * EXPERTS_PER_GROUP
    el = jnp.where((lane >= e_lo) & (lane < e_lo + EXPERTS_PER_GROUP), logits, neg)
    v1 = jnp.max(el, axis=-1, keepdims=True)
    i1 = jnp.min(jnp.where(el == v1, lane, big), axis=-1, keepdims=True)
    el2 = jnp.where(lane == i1, neg, el)
    v2 = jnp.max(el2, axis=-1, keepdims=True)
    i2 = jnp.min(jnp.where(el2 == v2, lane, big), axis=-1, keepdims=True)
    e2 = jnp.exp(v2 - v1)
    w1 = pg / (1.0 + e2)
    w2 = pg * e2 / (1.0 + e2)
    oh1 = lane == i1
    oh2 = lane == i2
    oh = jnp.where(oh1 | oh2, 1.0, 0.0)
    lrank = _dot(tril_ref[...], oh.astype(bf16))
    seg = jnp.floor((jnp.sum(oh, axis=0, keepdims=True) + (ROW_CHUNK - 1.0)) * (1.0 / ROW_CHUNK)) * ROW_CHUNK
    seg8 = jnp.broadcast_to(seg, (8, 128))
    lane8 = lax.broadcasted_iota(i32, (8, 128), 1)
    off8 = seg8
    for sh in (1, 2, 4, 8, 16):
        off8 = off8 + jnp.where(lane8 >= sh, pltpu.roll(off8, sh, 1), 0.0)
    off8 = off8 - seg8
    carry = carry_ref[...]
    lpos = lrank + off8[0:1, :]
    pick = lambda sel, val: jnp.sum(jnp.where(sel, val, 0.0), axis=-1, keepdims=True)
    sub8 = lax.broadcasted_iota(i32, (8, 128), 0)
    tab_ref[...] = jnp.where(sub8 == 0, seg8, jnp.where(sub8 == 1, off8, jnp.where(sub8 == 2, carry, 0.0)))
    carry = carry + seg8
    carry_ref[...] = carry
    cnt_ref[...] = carry
    cols = (i1 - ROUTE_E0, i2 - ROUTE_E0, w1, w2, pick(oh1, lpos), pick(oh2, lpos))
    route = jnp.zeros_like(logits)
    for j, col in enumerate(cols):
        route = jnp.where(lane == j, col, route)
    route_ref[...] = route


def _outproj(x_ctx, x_lat, mod, fo_ctx, fo_lat, po, so, ao_ctx, ao_lat, l, w):
    nt = T_ALL // TM
    tile = lambda wd: pl.BlockSpec((TM, wd), lambda i: (i, 0))
    vec = lambda wd: pl.BlockSpec((None, 1, wd), lambda i: (l, 0, 0))
    return pl.pallas_call(
        _outproj_kernel,
        grid=(nt,),
        in_specs=[
            _ctx_tile(D_MODEL), _lat_tile(D_MODEL),
            pl.BlockSpec((None, None, 6, D_MODEL), lambda i: (l, i // (SEG // TM), 0, 0)),
            _ctx_tile(FFT_W), _lat_tile(FFT_W), tile(POOL_W), tile(SGU_W),
            _ctx_tile(ATTN_W), _lat_tile(ATTN_W),
            pl.BlockSpec((None, D_MODEL, D_MODEL), lambda i: (l, 0, 0)),
            vec(D_MODEL), vec(D_MODEL),
            pl.BlockSpec((None, D_MODEL, 256), lambda i: (l, 0, 0)),
            vec(128),
            pl.BlockSpec((TM, TM), lambda i: (0, 0)),
        ],
        out_specs=[tile(D_MODEL), tile(D_MODEL), tile(128), pl.BlockSpec((8, 128), lambda i: (0, 0)),
                   pl.BlockSpec((None, 8, 128), lambda i: (i, 0, 0))],
        out_shape=[
            jax.ShapeDtypeStruct((T_ALL, D_MODEL), f32),
            jax.ShapeDtypeStruct((T_ALL, D_MODEL), bf16),
            jax.ShapeDtypeStruct((T_ALL, 128), f32),
            jax.ShapeDtypeStruct((8, 128), f32),
            jax.ShapeDtypeStruct((nt, 8, 128), f32),
        ],
        scratch_shapes=[pltpu.VMEM((8, 128), f32)],
        compiler_params=_cparams(("arbitrary",)),
        name="outproj",
    )(x_ctx, x_lat, mod, fo_ctx, fo_lat, po, so, ao_ctx, ao_lat,
      w["w_out"], w["ln1_g"], w["ln1_b"], w["w_r"], w["b_r"],
      w["tril"])


def _plan_kernel(cnt_ref, meta_ref):
    lane = lax.broadcasted_iota(i32, (8, 128), 1)
    sub = lax.broadcasted_iota(i32, (8, 128), 0)
    cnt = cnt_ref[...]
    is_e = (lane >= ROUTE_E0) & (lane < ROUTE_E0 + N_EXPERTS)
    tiles = jnp.where(is_e, jnp.floor((cnt + (MOE_TM - 1.0)) * (1.0 / MOE_TM)), 0.0)
    cum = tiles
    for s in (1, 2, 4, 8, 16):
        cum = cum + jnp.where(lane >= s, pltpu.roll(cum, s, 1), 0.0)
    pstart = (cum - tiles) * MOE_TM
    nused = jnp.max(cum, axis=-1, keepdims=True)
    fill = jnp.where(is_e & (cnt != tiles * MOE_TM), pstart + (tiles - 1.0) * MOE_TM, -1.0)
    meta = jnp.where(sub == 0, cnt, jnp.where(sub == 1, nused, jnp.where(sub == 2, fill,
                     jnp.where(sub == 3, pstart, 0.0))))
    meta_ref[...] = meta.astype(i32)


def _plan(cnt):
    return pl.pallas_call(
        _plan_kernel,
        grid=(1,),
        in_specs=[pl.BlockSpec((8, 128), lambda i: (0, 0))],
        out_specs=pl.BlockSpec((8, 128), lambda i: (0, 0)),
        out_shape=jax.ShapeDtypeStruct((8, 128), i32),
        compiler_params=_cparams(("arbitrary",)),
        name="plan",
    )(cnt)


def _dispatch_kernel(nch_ref, off_ref, dst_ref, tot_ref, fill_ref, nused_ref, h_ref, route_ref, xs_ref,
                     sorted_ref, zero_ref, sem, fill_sem):
    i = pl.program_id(0)

    def tile_fill(row0):
        return pltpu.make_async_copy(zero_ref, xs_ref.at[pl.ds(pl.multiple_of(row0, MOE_TM), MOE_TM)],
                                     fill_sem)

    @pl.when(i == 0)
    def _():
        zero_ref[...] = jnp.zeros_like(zero_ref)

        def start(e, c):
            @pl.when(fill_ref[e] >= 0)
            def _():
                tile_fill(jnp.maximum(fill_ref[e], 0)).start()
            return c

        def wait(e, c):
            @pl.when(fill_ref[e] >= 0)
            def _():
                tile_fill(jnp.maximum(fill_ref[e], 0)).wait()
            return c

        def start_tail(t, c):
            tile_fill(t * MOE_TM).start()
            return c

        def wait_tail(t, c):
            tile_fill(t * MOE_TM).wait()
            return c

        lax.fori_loop(0, N_EXPERTS, start, 0)
        lax.fori_loop(nused_ref[0], MOE_NT, start_tail, 0)
        lax.fori_loop(0, N_EXPERTS, wait, 0)
        lax.fori_loop(nused_ref[0], MOE_NT, wait_tail, 0)

    rt = route_ref[...].T
    hb = h_ref[...]
    slot = i % 2
    used_rows = tot_ref[i] * ROW_CHUNK

    def sort_rows(nrows):
        j = lax.broadcasted_iota(i32, (nrows, 1), 0).astype(f32)
        sel = jnp.where((j == rt[4:5, :]) | (j == rt[5:6, :]), 1.0, 0.0).astype(bf16)
        sorted_ref[slot, 0:nrows, :] = _dot(sel, hb).astype(bf16)

    @pl.when(used_rows <= DISP_ROWS_SHORT)
    def _():
        sort_rows(DISP_ROWS_SHORT)

    @pl.when(used_rows > DISP_ROWS_SHORT)
    def _():
        sort_rows(DISP_ROWS)

    def per_expert(e, c):
        idx = i * N_EXPERTS + e
        n, s0, d0 = nch_ref[idx], off_ref[idx], dst_ref[idx]
        b = 1
        while b <= TM // ROW_CHUNK:
            @pl.when((n & b) != 0)
            def _(b=b):
                r0 = (n & (b - 1)) * ROW_CHUNK
                rows = b * ROW_CHUNK
                pltpu.make_async_copy(
                    sorted_ref.at[slot, pl.ds(pl.multiple_of(s0 + r0, ROW_CHUNK), rows)],
                    xs_ref.at[pl.ds(pl.multiple_of(d0 + r0, ROW_CHUNK), rows)], sem.at[slot]).start()
            b *= 2
        return c

    lax.fori_loop(0, N_EXPERTS, per_expert, 0)

    def drain(tile, s):
        rows = tot_ref[tile] * ROW_CHUNK

        @pl.when(rows > 0)
        def _():
            pltpu.make_async_copy(sorted_ref.at[s, pl.ds(0, rows)], xs_ref.at[pl.ds(0, rows)],
                                  sem.at[s]).wait()

    @pl.when(i >= 1)
    def _():
        drain(i - 1, 1 - slot)

    @pl.when(i == pl.num_programs(0) - 1)
    def _():
        drain(i, slot)


def _dispatch(nch, off, dst, tot, fill, nused, h2, route):
    grid_spec = pltpu.PrefetchScalarGridSpec(
        num_scalar_prefetch=6,
        grid=(T_ALL // TM,),
        in_specs=[pl.BlockSpec((TM, D_MODEL), lambda i, *_: (i, 0)),
                  pl.BlockSpec((TM, 128), lambda i, *_: (i, 0))],
        out_specs=pl.BlockSpec(memory_space=pl.ANY),
        scratch_shapes=[pltpu.VMEM((2, DISP_ROWS, D_MODEL), bf16), pltpu.VMEM((MOE_TM, D_MODEL), bf16),
                        pltpu.SemaphoreType.DMA((2,)), pltpu.SemaphoreType.DMA(())],
    )
    return pl.pallas_call(
        _dispatch_kernel,
        grid_spec=grid_spec,
        out_shape=jax.ShapeDtypeStruct((MOE_NT * MOE_TM, D_MODEL), bf16),
        compiler_params=_cparams(("arbitrary",)),
        name="dispatch",
    )(nch, off, dst, tot, fill, nused, h2, route)


def _experts_kernel(cnt_ref, nused_ref, xs_ref, wg_hbm, wu_hbm, wd_hbm, ys_ref,
                    wg_f, wu_f, wd_f, wg_b, wu_b, wd_b, st, wsem, *, layer):
    i = pl.program_id(0)
    nused = nused_ref[0]
    NXT, NSLOT, LEFT, ROWS = 0, 1, 2, 3

    def w_copies(e, slot):
        return (pltpu.make_async_copy(wg_hbm.at[layer, e], wg_f.at[slot], wsem.at[slot, 0]),
                pltpu.make_async_copy(wu_hbm.at[layer, e], wu_f.at[slot], wsem.at[slot, 1]),
                pltpu.make_async_copy(wd_hbm.at[layer, e], wd_f.at[slot], wsem.at[slot, 2]))

    def next_nonempty(e):
        return lax.while_loop(
            lambda v: (v < N_EXPERTS) & (cnt_ref[jnp.minimum(v, N_EXPERTS - 1)] == 0),
            lambda v: v + 1, e)

    @pl.when(i == 0)
    def _():
        e0 = next_nonempty(jnp.int32(0))
        for c in w_copies(e0, 0):
            c.start()
        st[NXT] = e0
        st[NSLOT] = 0
        st[LEFT] = 0

    @pl.when(i < nused)
    def _():
        @pl.when(st[LEFT] == 0)
        def _():
            e = st[NXT]
            slot = st[NSLOT]
            for c in w_copies(e, slot):
                c.wait()
            e2 = next_nonempty(e + 1)

            @pl.when(e2 < N_EXPERTS)
            def _():
                for c in w_copies(e2, 1 - slot):
                    c.start()

            st[NXT] = e2
            st[NSLOT] = 1 - slot
            st[LEFT] = (cnt_ref[e] + (MOE_TM - 1)) // MOE_TM
            st[ROWS] = cnt_ref[e]
            wg_b[...] = wg_f[slot].astype(bf16)
            wu_b[...] = wu_f[slot].astype(bf16)
            wd_b[...] = wd_f[slot].astype(bf16)

        def ffn(rows):
            x = xs_ref[rows, :]
            hg = _dot(x, wg_b[...])
            hu = _dot(x, wu_b[...])
            act = (hg * jax.nn.sigmoid(hg)) * hu
            ys_ref[rows, :] = _dot(act.astype(bf16), wd_b[...]).astype(bf16)

        half = MOE_TM // 2
        short = st[ROWS] <= half

        @pl.when(short)
        def _():
            ffn(slice(0, half))
            ys_ref[half:, :] = jnp.zeros((MOE_TM - half, D_MODEL), bf16)

        @pl.when(jnp.logical_not(short))
        def _():
            ffn(slice(0, MOE_TM))

        st[LEFT] = st[LEFT] - 1
        st[ROWS] = st[ROWS] - MOE_TM

    @pl.when(i >= nused)
    def _():
        ys_ref[...] = jnp.zeros_like(ys_ref)


def _experts(counts, nused, xs, l, w_gate, w_up, w_down):
    hbm = pl.BlockSpec(memory_space=pl.ANY)
    grid_spec = pltpu.PrefetchScalarGridSpec(
        num_scalar_prefetch=2,
        grid=(MOE_NT,),
        in_specs=[pl.BlockSpec((MOE_TM, D_MODEL), lambda i, c, nu: (jnp.minimum(i, nu[0] - 1), 0)),
                  hbm, hbm, hbm],
        out_specs=pl.BlockSpec((MOE_TM, D_MODEL), lambda i, c, nu: (i, 0)),
        scratch_shapes=[
            pltpu.VMEM((2, D_MODEL, EXPERT_FF), f32),
            pltpu.VMEM((2, D_MODEL, EXPERT_FF), f32),
            pltpu.VMEM((2, EXPERT_FF, D_MODEL), f32),
            pltpu.VMEM((D_MODEL, EXPERT_FF), bf16),
            pltpu.VMEM((D_MODEL, EXPERT_FF), bf16),
            pltpu.VMEM((EXPERT_FF, D_MODEL), bf16),
            pltpu.SMEM((4,), i32),
            pltpu.SemaphoreType.DMA((2, 3)),
        ],
    )
    return pl.pallas_call(
        functools.partial(_experts_kernel, layer=l),
        grid_spec=grid_spec,
        out_shape=jax.ShapeDtypeStruct((MOE_NT * MOE_TM, D_MODEL), bf16),
        compiler_params=_cparams(("arbitrary",)),
        name="experts",
    )(counts, nused, xs, w_gate, w_up, w_down)


def _combine_kernel(nch_ref, off_ref, dst_ref, tot_ref, x1_ref, mod_ref, route_ref, ys_hbm, g_ref, b_ref,
                    oc_ref, ol_ref, ybuf, moe_ref, sem):
    i = pl.program_id(0)
    nt = pl.num_programs(0) - 1

    @pl.when(i == 0)
    def _():
        ybuf[...] = jnp.zeros_like(ybuf)

    @pl.when(i < nt)
    def _():
        slot = i % 2

        def per_expert(e, c):
            idx = i * N_EXPERTS + e
            n, s0, d0 = nch_ref[idx], off_ref[idx], dst_ref[idx]
            b = 1
            while b <= TM // ROW_CHUNK:
                @pl.when((n & b) != 0)
                def _(b=b):
                    r0 = (n & (b - 1)) * ROW_CHUNK
                    rows = b * ROW_CHUNK
                    pltpu.make_async_copy(
                        ys_hbm.at[pl.ds(pl.multiple_of(d0 + r0, ROW_CHUNK), rows)],
                        ybuf.at[slot, pl.ds(pl.multiple_of(s0 + r0, ROW_CHUNK), rows)],
                        sem.at[slot]).start()
                b *= 2
            return c

        lax.fori_loop(0, N_EXPERTS, per_expert, 0)

    @pl.when(i >= 1)
    def _():
        slot = (i - 1) % 2
        rows = tot_ref[i - 1] * ROW_CHUNK

        @pl.when(rows > 0)
        def _():
            pltpu.make_async_copy(ys_hbm.at[pl.ds(0, rows)], ybuf.at[slot, pl.ds(0, rows)],
                                  sem.at[slot]).wait()

        route = route_ref[...]
        mod = mod_ref[...]

        def unsort(nrows):
            lane = lax.broadcasted_iota(i32, (1, nrows), 1).astype(f32)
            wmat = (jnp.where(lane == route[:, 4:5], route[:, 2:3], 0.0)
                    + jnp.where(lane == route[:, 5:6], route[:, 3:4], 0.0))
            moe_ref[...] = _dot(wmat.astype(bf16), ybuf[slot, 0:nrows, :])

        @pl.when(rows <= DISP_ROWS_SHORT)
        def _():
            unsort(DISP_ROWS_SHORT)

        @pl.when(rows > DISP_ROWS_SHORT)
        def _():
            unsort(DISP_ROWS)

        y = _layer_norm(DEEPNORM_ALPHA * x1_ref[...] + mod[5:6] * moe_ref[...], g_ref[...], b_ref[...])

        @pl.when(i - 1 < T_CTX // TM)
        def _():
            oc_ref[...] = y

        @pl.when(i - 1 >= T_CTX // TM)
        def _():
            ol_ref[...] = y


def _combine(nch, off, dst, tot, x1, mod, route, ys, l, w):
    nt = T_ALL // TM
    vec = pl.BlockSpec((None, 1, D_MODEL), lambda i, *_: (l, 0, 0))
    nctx = T_CTX // TM
    prev = lambda i: jnp.maximum(i - 1, 0)
    grid_spec = pltpu.PrefetchScalarGridSpec(
        num_scalar_prefetch=4,
        grid=(nt + 1,),
        in_specs=[
            pl.BlockSpec((TM, D_MODEL), lambda i, *_: (prev(i), 0)),
            pl.BlockSpec((None, None, 6, D_MODEL), lambda i, *_: (l, prev(i) // (SEG // TM), 0, 0)),
            pl.BlockSpec((TM, 128), lambda i, *_: (prev(i), 0)),
            pl.BlockSpec(memory_space=pl.ANY),
            vec, vec,
        ],
        out_specs=[pl.BlockSpec((TM, D_MODEL), lambda i, *_: (jnp.minimum(prev(i), nctx - 1), 0)),
                   pl.BlockSpec((TM, D_MODEL), lambda i, *_: (jnp.maximum(prev(i) - nctx, 0), 0))],
        scratch_shapes=[pltpu.VMEM((2, DISP_ROWS, D_MODEL), bf16), pltpu.VMEM((TM, D_MODEL), f32),
                        pltpu.SemaphoreType.DMA((2,))],
    )
    return pl.pallas_call(
        _combine_kernel,
        grid_spec=grid_spec,
        out_shape=[jax.ShapeDtypeStruct((T_CTX, D_MODEL), f32),
                   jax.ShapeDtypeStruct((T_LAT, D_MODEL), f32)],
        compiler_params=_cparams(("arbitrary",)),
        name="combine",
    )(nch, off, dst, tot, x1, mod, route, ys, w["ln2_g"], w["ln2_b"])


def _dft_cos_sin(n, scale):
    k = jnp.arange(n, dtype=i32)
    ang = ((k[:, None] * k[None, :]) % n).astype(f32) * np.float32(2.0 * np.pi / n)
    return jnp.cos(ang) * scale, jnp.sin(ang) * scale


def _seq_dft_matrix(n):
    g = min(DFT_SPLIT, n)
    j = jnp.arange(n, dtype=i32)[None, :]
    k1 = jnp.arange(n // g, dtype=i32)[:, None]
    k2 = jnp.arange(g, dtype=i32)[:, None]
    ang_a = ((k1 * j) % (n // g)).astype(f32) * np.float32(2.0 * np.pi * g / n)
    ang_b = ((k2 * j) % n).astype(f32) * np.float32(2.0 * np.pi / n)
    scale = np.float32(n ** -0.5)
    ca, sa = jnp.cos(ang_a), jnp.sin(ang_a)
    cb, sb = jnp.cos(ang_b) * scale, jnp.sin(ang_b) * scale
    ca2 = jnp.concatenate([ca, ca], axis=1)[:, None, :]
    sa2 = jnp.concatenate([sa, sa], axis=1)[:, None, :]
    cb2 = jnp.concatenate([cb, -sb], axis=1)[None, :, :]
    sb2 = jnp.concatenate([sb, cb], axis=1)[None, :, :]
    return (ca2 * cb2 - sa2 * sb2).astype(bf16).reshape(n, 2 * n)


def _rope_tables():
    rows = DEC_SEQ // GRID_W
    row = jnp.repeat(jnp.arange(rows), GRID_W).astype(f32)
    col = jnp.tile(jnp.arange(GRID_W), rows).astype(f32)
    n_freq = HEAD_DIM // 4
    inv = ROPE_THETA ** (-jnp.arange(n_freq, dtype=f32) / n_freq)
    ar = row[:, None] * inv
    ac = col[:, None] * inv
    ang = jnp.concatenate([ar, ar, ac, ac], axis=-1)
    cos = jnp.tile(jnp.cos(ang), (1, N_HEADS))
    sin = jnp.tile(jnp.sin(ang), (1, N_HEADS))
    first = (jnp.arange(ATTN_W) % (HEAD_DIM // 2)) < n_freq
    sin_a = jnp.where(first[None, :], -sin, 0.0)
    sin_b = jnp.where(first[None, :], 0.0, sin)
    ident = jnp.zeros((TM, ATTN_W), f32)
    return (jnp.concatenate([cos, ident + 1.0], axis=0),
            jnp.concatenate([sin_a, ident], axis=0),
            jnp.concatenate([sin_b, ident], axis=0))


def _dup_cache(cache):
    c = jnp.transpose(cache, (1, 3, 0, 2, 4))
    return jnp.concatenate([c, c], axis=-1).astype(bf16)


def kernel(x_prompt, x_sample, cache_k, cache_v, c, c_ctx, w_mod, b_mod, w_in, w_fft, w_pool, pool_scale, sgu_ln_g, sgu_ln_b, w_sgu, b_sgu, q_norm_g, k_norm_g, w_out, ln1_g, ln1_b, w_router_group, b_router_group, w_router_expert, b_router_expert, w_gate, w_up, w_down, ln2_g, ln2_b):
    L = DEPTH
    x_ctx = x_prompt.reshape(T_CTX, D_MODEL)
    x_lat = x_sample.reshape(T_LAT, D_MODEL)

    cond8 = jnp.concatenate([c_ctx[None, :], c, jnp.zeros((8 - 1 - DEC_BATCH, D_MODEL), f32)], axis=0)
    mod = _modulation(cond8, w_mod, b_mod)[:, :N_SEG].reshape(L, N_SEG, 6, D_MODEL)

    cc, sc = _dft_cos_sin(FFT_W, np.float32(FFT_W ** -0.5))
    rope_cos, rope_sin_a, rope_sin_b = _rope_tables()
    head_id = jnp.arange(ATTN_W) // HEAD_DIM
    eye_g = jnp.eye(len(POOL_WINDOWS), dtype=f32)
    w_r = jnp.zeros((L, D_MODEL, 128), f32)
    w_r = w_r.at[:, :, :N_GROUPS].set(w_router_group).at[:, :, ROUTE_E0:ROUTE_E0 + N_EXPERTS].set(w_router_expert)
    b_r = jnp.zeros((L, 1, 128), f32)
    b_r = b_r.at[:, 0, :N_GROUPS].set(b_router_group).at[:, 0, ROUTE_E0:ROUTE_E0 + N_EXPERTS].set(b_router_expert)
    w_r_hi, w_r_lo = _split_hi_lo(w_r)
    w = {
        "w_in": w_in.astype(bf16),
        "csc": jnp.concatenate([cc, sc], axis=1).astype(bf16),
        "w_sgu": jnp.transpose(w_sgu, (0, 2, 1, 3)).reshape(L, CHUNK, SGU_HEADS * CHUNK).astype(bf16),
        "b_sgu": jnp.repeat(jnp.transpose(b_sgu, (0, 2, 1)), SGU_W // SGU_HEADS, axis=2),
        "sgu_ln_g": sgu_ln_g.reshape(L, 1, SGU_W),
        "sgu_ln_b": sgu_ln_b.reshape(L, 1, SGU_W),
        "q_norm_g": jnp.tile(q_norm_g, (1, N_HEADS)).reshape(L, 1, ATTN_W),
        "k_norm_g": jnp.tile(k_norm_g, (1, N_KV_HEADS)).reshape(L, 1, KV_W),
        "rope_cos": rope_cos, "rope_sin_a": rope_sin_a, "rope_sin_b": rope_sin_b,
        "ones_bd": (head_id[:, None] == head_id[None, :]).astype(bf16),
        "w_pool_bd": jnp.einsum("lgcd,gh->lgchd", w_pool, eye_g).reshape(L, POOL_W, POOL_W).astype(bf16),
        "pool_scale": pool_scale.reshape(L, 1, POOL_W),
        "w_fft": w_fft.astype(bf16),
        "w_out": w_out.astype(bf16),
        "ln1_g": ln1_g.reshape(L, 1, D_MODEL), "ln1_b": ln1_b.reshape(L, 1, D_MODEL),
        "ln2_g": ln2_g.reshape(L, 1, D_MODEL), "ln2_b": ln2_b.reshape(L, 1, D_MODEL),
        "w_r": jnp.concatenate([w_r_hi, w_r_lo], axis=-1), "b_r": b_r,
        "tril": (jnp.arange(TM)[:, None] > jnp.arange(TM)[None, :]).astype(bf16),
    }
    m_ctx = _seq_dft_matrix(SEQ)
    m_lat = _seq_dft_matrix(DEC_SEQ)
    kc_all = _dup_cache(cache_k)
    vc_all = _dup_cache(cache_v)

    new_k, new_v = [], []
    for l in range(L):
        pq, praw, sgu, q, kd, vd, nk, nv = _inproj(x_ctx, x_lat, mod, l, w)
        new_k.append(nk[:T_CTX].reshape(BATCH, SEQ, N_KV_HEADS, HEAD_DIM))
        new_v.append(nv[:T_CTX].reshape(BATCH, SEQ, N_KV_HEADS, HEAD_DIM))
        po = _pool(praw, l, w)
        fo_ctx = _seqdft(pq, m_ctx, l, w, n=SEQ, tr=SEQ, nseq=BATCH, row0=0)
        fo_lat = _seqdft(pq, m_lat, l, w, n=DEC_SEQ, tr=FFT_TR, nseq=DEC_BATCH, row0=T_CTX)
        ao_ctx = _attention(q, kd, vd, None, n=SEQ, tq=SEQ, nseq=BATCH, row0=0, heads=N_KV_HEADS)
        ao_lat = _attention(q, kd, vd, (kc_all[l], vc_all[l]), n=DEC_SEQ, tq=ATT_TQ, nseq=DEC_BATCH,
                            row0=T_CTX, heads=1)
        x1, h2, route, cnt, tab = _outproj(x_ctx, x_lat, mod, fo_ctx, fo_lat, po, sgu, ao_ctx, ao_lat, l, w)
        meta = _plan(cnt)
        experts = slice(ROUTE_E0, ROUTE_E0 + N_EXPERTS)
        counts = meta[0, experts]
        nused = meta[1, :1]
        fill = meta[2, experts]
        tab = tab[:, :, experts].astype(i32)
        nch = (tab[:, 0] // ROW_CHUNK).reshape(-1)
        off = tab[:, 1].reshape(-1)
        dst = (meta[3, experts][None, :] + tab[:, 2]).reshape(-1)
        tot = jnp.sum(tab[:, 0], axis=1) // ROW_CHUNK
        xs = _dispatch(nch, off, dst, tot, fill, nused, h2, route)
        ys = _experts(counts, nused, xs, l, w_gate, w_up, w_down)
        x_ctx, x_lat = _combine(nch, off, dst, tot, x1, mod, route, ys, l, w)

    y_prompt = x_ctx.reshape(BATCH, SEQ, D_MODEL)
    y_sample = x_lat.reshape(DEC_BATCH, DEC_SEQ, D_MODEL)
    return (y_prompt, y_sample, jnp.stack(new_k, axis=1), jnp.stack(new_v, axis=1))
```
